```python
import math
import jax, jax.numpy as jnp
from jax import lax
import numpy as np

D_MODEL = 1024
BATCH = 8
SEQ = 8192
DEPTH = 4

D_CONV = 256
CONV_WIDTH = 3
D_SSM = 256
SSM_GROUP = 16
N_SSM_GROUPS = D_SSM // SSM_GROUP
SSM_STATE = 64
DT_MIN = 1e-3
DT_MAX = 1e-1
N_Q_HEADS = 8
N_KV_HEADS = 2
HEAD_DIM = 64
D_ATTN = N_Q_HEADS * HEAD_DIM
D_KV = N_KV_HEADS * HEAD_DIM
WINDOW = 128
BLOCK = 128
N_BRANCH = 3
D_FF = 2816
ALPHA = (2 * DEPTH) ** 0.25
BETA = (8 * DEPTH) ** -0.25
LN_EPS = 1e-5
D_IN = 3 * D_CONV + D_SSM + D_ATTN + 2 * D_KV + N_BRANCH * D_MODEL

kernel_name = 'hybrid_gated_conv_s5_swa_macaron_deepnorm'


def layer_norm(x, g, b):
    xf = x.astype(jnp.float32)
    mu = jnp.mean(xf, axis=-1, keepdims=True)
    var = jnp.mean(jnp.square(xf - mu), axis=-1, keepdims=True)
    y = (xf - mu) * lax.rsqrt(var + LN_EPS)
    return (y * g.astype(jnp.float32) + b.astype(jnp.float32)).astype(x.dtype)


def swiglu(x, w_gate, w_up, w_down):
    return (jax.nn.silu(x @ w_gate) * (x @ w_up)) @ w_down


def short_conv_mixer(b_gate, c_gate, h, conv_w, conv_b):
    z = c_gate * h
    kern = conv_w[:, None, :].astype(z.dtype)
    y = lax.conv_general_dilated(z, kern, window_strides=(1,), padding=[(CONV_WIDTH - 1, 0)],
                                 dimension_numbers=('NWC', 'WIO', 'NWC'), feature_group_count=D_CONV)
    return b_gate * (y + conv_b)


def s5_mixer(u, a_re, a_im, log_dt, b_re, b_im, c_re, c_im, d_skip, w_glu):
    f32 = jnp.float32
    bsz, s, _ = u.shape
    uf = u.astype(f32).reshape(bsz, s, N_SSM_GROUPS, SSM_GROUP)
    lr = a_re.astype(f32)
    li = a_im.astype(f32)
    dt = jnp.exp(log_dt.astype(f32))[:, None]
    mag = jnp.exp(lr * dt)
    ang = li * dt
    abar_re = mag * jnp.cos(ang)
    abar_im = mag * jnp.sin(ang)
    nr = abar_re - 1.0
    ni = abar_im
    den = lr * lr + li * li
    coef_re = (nr * lr + ni * li) / den
    coef_im = (ni * lr - nr * li) / den
    br = b_re.astype(f32)
    bi = b_im.astype(f32)
    bbar_re = coef_re[..., None] * br - coef_im[..., None] * bi
    bbar_im = coef_re[..., None] * bi + coef_im[..., None] * br
    bu_re = jnp.einsum('bsgc,gpc->bsgp', uf, bbar_re)
    bu_im = jnp.einsum('bsgc,gpc->bsgp', uf, bbar_im)
    aa_re = jnp.broadcast_to(abar_re, bu_re.shape)
    aa_im = jnp.broadcast_to(abar_im, bu_im.shape)

    def combine(e1, e2):
        a1r, a1i, b1r, b1i = e1
        a2r, a2i, b2r, b2i = e2
        return (a2r * a1r - a2i * a1i,
                a2r * a1i + a2i * a1r,
                a2r * b1r - a2i * b1i + b2r,
                a2r * b1i + a2i * b1r + b2i)

    _, _, xs_re, xs_im = lax.associative_scan(combine, (aa_re, aa_im, bu_re, bu_im), axis=1)
    y = (jnp.einsum('bsgp,gcp->bsgc', xs_re, c_re.astype(f32))
         - jnp.einsum('bsgp,gcp->bsgc', xs_im, c_im.astype(f32)))
    y = y + d_skip.astype(f32).reshape(N_SSM_GROUPS, SSM_GROUP) * uf
    y = jax.nn.gelu(y.reshape(bsz, s, D_SSM))
    y = y * jax.nn.sigmoid(y @ w_glu.astype(f32))
    return y.astype(u.dtype)


def sliding_window_attention(q, k, v, sinks):
    f32 = jnp.float32
    bsz, s, _ = q.shape
    nb = s // BLOCK
    grp = N_Q_HEADS // N_KV_HEADS
    qb = q.astype(f32).reshape(bsz, nb, BLOCK, N_KV_HEADS, grp, HEAD_DIM) * (HEAD_DIM ** -0.5)
    kb = k.astype(f32).reshape(bsz, nb, BLOCK, N_KV_HEADS, HEAD_DIM)
    vb = v.astype(f32).reshape(bsz, nb, BLOCK, N_KV_HEADS, HEAD_DIM)
    pad = ((0, 0), (1, 0), (0, 0), (0, 0), (0, 0))
    kk = jnp.concatenate([jnp.pad(kb, pad)[:, :-1], kb], axis=2)
    vv = jnp.concatenate([jnp.pad(vb, pad)[:, :-1], vb], axis=2)
    scores = jnp.einsum('bnqhgd,bnkhd->bnhgqk', qb, kk)
    qpos = jnp.arange(BLOCK)[:, None] + BLOCK
    kpos = jnp.arange(2 * BLOCK)[None, :]
    diff = qpos - kpos
    band = (diff >= 0) & (diff < WINDOW)
    has_prev = (jnp.arange(nb) > 0)[:, None, None] | (kpos >= BLOCK)[None]
    mask = band[None] & has_prev
    scores = jnp.where(mask[None, :, None, None], scores, -jnp.inf)
    sink = sinks.astype(f32).reshape(N_KV_HEADS, grp)[None, None, :, :, None, None]
    m = jnp.maximum(jnp.max(scores, axis=-1, keepdims=True), sink)
    p = jnp.exp(scores - m)
    probs = p / (jnp.sum(p, axis=-1, keepdims=True) + jnp.exp(sink - m))
    out = jnp.einsum('bnhgqk,bnkhd->bnqhgd', probs, vv)
    return out.reshape(bsz, s, D_ATTN).astype(q.dtype)


def hybrid_mixer(x, w_in, conv_w, conv_b, ssm_a_re, ssm_a_im, ssm_log_dt, ssm_b_re, ssm_b_im,
                 ssm_c_re, ssm_c_im, ssm_d, ssm_w_glu, attn_sinks, w_br_conv, w_br_ssm, w_br_attn, w_out):
    bsz, s, _ = x.shape
    proj = x @ w_in
    widths = [D_CONV, D_CONV, D_CONV, D_SSM, D_ATTN, D_KV, D_KV]
    offs = [int(o) for o in np.cumsum(widths)]
    b_g, c_g, h, u, q, k, v, gates = jnp.split(proj, offs, axis=-1)
    y_conv = short_conv_mixer(b_g, c_g, h, conv_w, conv_b) @ w_br_conv
    y_ssm = s5_mixer(u, ssm_a_re, ssm_a_im, ssm_log_dt, ssm_b_re, ssm_b_im,
                     ssm_c_re, ssm_c_im, ssm_d, ssm_w_glu) @ w_br_ssm
    y_attn = sliding_window_attention(q, k, v, attn_sinks) @ w_br_attn
    g = jax.nn.sigmoid(gates).reshape(bsz, s, N_BRANCH, D_MODEL)
    merged = g[:, :, 0] * y_conv + g[:, :, 1] * y_ssm + g[:, :, 2] * y_attn
    return merged @ w_out


def _fwd_setup_inputs(seed: int = 0) -> dict:
    key = jax.random.key(seed)
    ks = jax.random.split(key, 32)
    L = DEPTH

    def nrm(k, shape, scale):
        return jax.random.normal(k, shape, jnp.float32) * scale

    def gain(k):
        return 1.0 + nrm(k, (L, D_MODEL), 0.01)

    n_idx = jnp.arange(SSM_STATE, dtype=jnp.float32)
    return {
        'x': nrm(ks[0], (BATCH, SEQ, D_MODEL), 1.0),
        'ffn1_w_gate': nrm(ks[1], (L, D_MODEL, D_FF), D_MODEL ** -0.5),
        'ffn1_w_up': nrm(ks[2], (L, D_MODEL, D_FF), D_MODEL ** -0.5),
        'ffn1_w_down': nrm(ks[3], (L, D_FF, D_MODEL), BETA * D_FF ** -0.5),
        'ln1_g': gain(ks[4]),
        'ln1_b': nrm(ks[5], (L, D_MODEL), 0.01),
        'w_in': nrm(ks[6], (L, D_MODEL, D_IN), D_MODEL ** -0.5),
        'conv_w': nrm(ks[7], (L, CONV_WIDTH, D_CONV), CONV_WIDTH ** -0.5),
        'conv_b': nrm(ks[8], (L, D_CONV), 0.01),
        'ssm_a_re': -0.5 + nrm(ks[9], (L, N_SSM_GROUPS, SSM_STATE), 0.01),
        'ssm_a_im': jnp.pi * n_idx + nrm(ks[10], (L, N_SSM_GROUPS, SSM_STATE), 0.01),
        'ssm_log_dt': jax.random.uniform(ks[11], (L, N_SSM_GROUPS), jnp.float32,
                                         math.log(DT_MIN), math.log(DT_MAX)),
        'ssm_b_re': nrm(ks[12], (L, N_SSM_GROUPS, SSM_STATE, SSM_GROUP), (2 * SSM_GROUP) ** -0.5),
        'ssm_b_im': nrm(ks[13], (L, N_SSM_GROUPS, SSM_STATE, SSM_GROUP), (2 * SSM_GROUP) ** -0.5),
        'ssm_c_re': nrm(ks[14], (L, N_SSM_GROUPS, SSM_GROUP, SSM_STATE), SSM_STATE ** -0.5),
        'ssm_c_im': nrm(ks[15], (L, N_SSM_GROUPS, SSM_GROUP, SSM_STATE), SSM_STATE ** -0.5),
        'ssm_d': nrm(ks[16], (L, D_SSM), 1.0),
        'ssm_w_glu': nrm(ks[17], (L, D_SSM, D_SSM), D_SSM ** -0.5),
        'attn_sinks': nrm(ks[18], (L, N_Q_HEADS), 0.5),
        'w_br_conv': nrm(ks[19], (L, D_CONV, D_MODEL), D_CONV ** -0.5),
        'w_br_ssm': nrm(ks[20], (L, D_SSM, D_MODEL), D_SSM ** -0.5),
        'w_br_attn': nrm(ks[21], (L, D_ATTN, D_MODEL), D_ATTN ** -0.5),
        'w_out': nrm(ks[22], (L, D_MODEL, D_MODEL), BETA * D_MODEL ** -0.5),
        'ln2_g': gain(ks[23]),
        'ln2_b': nrm(ks[24], (L, D_MODEL), 0.01),
        'ffn2_w_gate': nrm(ks[25], (L, D_MODEL, D_FF), D_MODEL ** -0.5),
        'ffn2_w_up': nrm(ks[26], (L, D_MODEL, D_FF), D_MODEL ** -0.5),
        'ffn2_w_down': nrm(ks[27], (L, D_FF, D_MODEL), BETA * D_FF ** -0.5),
        'ln3_g': gain(ks[28]),
        'ln3_b': nrm(ks[29], (L, D_MODEL), 0.01),
    }


def _fwd_reference(x, ffn1_w_gate, ffn1_w_up, ffn1_w_down, ln1_g, ln1_b, w_in, conv_w, conv_b,
              ssm_a_re, ssm_a_im, ssm_log_dt, ssm_b_re, ssm_b_im, ssm_c_re, ssm_c_im, ssm_d,
              ssm_w_glu, attn_sinks, w_br_conv, w_br_ssm, w_br_attn, w_out, ln2_g, ln2_b,
              ffn2_w_gate, ffn2_w_up, ffn2_w_down, ln3_g, ln3_b):
    for l in range(DEPTH):
        x = layer_norm(ALPHA * x + 0.5 * swiglu(x, ffn1_w_gate[l], ffn1_w_up[l], ffn1_w_down[l]),
                       ln1_g[l], ln1_b[l])
        mix = hybrid_mixer(x, w_in[l], conv_w[l], conv_b[l], ssm_a_re[l], ssm_a_im[l], ssm_log_dt[l],
                           ssm_b_re[l], ssm_b_im[l], ssm_c_re[l], ssm_c_im[l], ssm_d[l], ssm_w_glu[l],
                           attn_sinks[l], w_br_conv[l], w_br_ssm[l], w_br_attn[l], w_out[l])
        x = layer_norm(ALPHA * x + mix, ln2_g[l], ln2_b[l])
        x = layer_norm(ALPHA * x + 0.5 * swiglu(x, ffn2_w_gate[l], ffn2_w_up[l], ffn2_w_down[l]),
                       ln3_g[l], ln3_b[l])
    return x


import jax as _jax
import jax.numpy as _jnp

TWIN_FORMAT = 'train_step'
FWD_PARAMS = ['x', 'ffn1_w_gate', 'ffn1_w_up', 'ffn1_w_down', 'ln1_g', 'ln1_b', 'w_in', 'conv_w', 'conv_b', 'ssm_a_re', 'ssm_a_im', 'ssm_log_dt', 'ssm_b_re', 'ssm_b_im', 'ssm_c_re', 'ssm_c_im', 'ssm_d', 'ssm_w_glu', 'attn_sinks', 'w_br_conv', 'w_br_ssm', 'w_br_attn', 'w_out', 'ln2_g', 'ln2_b', 'ffn2_w_gate', 'ffn2_w_up', 'ffn2_w_down', 'ln3_g', 'ln3_b']
TWIN_WEIGHTS = ['ffn1_w_gate', 'ffn1_w_up', 'ffn1_w_down', 'ln1_g', 'ln1_b', 'w_in', 'conv_w', 'conv_b', 'ssm_a_re', 'ssm_a_im', 'ssm_log_dt', 'ssm_b_re', 'ssm_b_im', 'ssm_c_re', 'ssm_c_im', 'ssm_d', 'ssm_w_glu', 'attn_sinks', 'w_br_conv', 'w_br_ssm', 'w_br_attn', 'w_out', 'ln2_g', 'ln2_b', 'ffn2_w_gate', 'ffn2_w_up', 'ffn2_w_down', 'ln3_g', 'ln3_b']
TWIN_DIFF_INPUT = 'x'
TWIN_INPUTS = ['x', 'ffn1_w_gate', 'ffn1_w_up', 'ffn1_w_down', 'ln1_g', 'ln1_b', 'w_in', 'conv_w', 'conv_b', 'ssm_a_re', 'ssm_a_im', 'ssm_log_dt', 'ssm_b_re', 'ssm_b_im', 'ssm_c_re', 'ssm_c_im', 'ssm_d', 'ssm_w_glu', 'attn_sinks', 'w_br_conv', 'w_br_ssm', 'w_br_attn', 'w_out', 'ln2_g', 'ln2_b', 'ffn2_w_gate', 'ffn2_w_up', 'ffn2_w_down', 'ln3_g', 'ln3_b', 'loss_target', 'm_ffn1_w_gate', 'm_ffn1_w_up', 'm_ffn1_w_down', 'm_ln1_g', 'm_ln1_b', 'm_w_in', 'm_conv_w', 'm_conv_b', 'm_ssm_a_re', 'm_ssm_a_im', 'm_ssm_log_dt', 'm_ssm_b_re', 'm_ssm_b_im', 'm_ssm_c_re', 'm_ssm_c_im', 'm_ssm_d', 'm_ssm_w_glu', 'm_attn_sinks', 'm_w_br_conv', 'm_w_br_ssm', 'm_w_br_attn', 'm_w_out', 'm_ln2_g', 'm_ln2_b', 'm_ffn2_w_gate', 'm_ffn2_w_up', 'm_ffn2_w_down', 'm_ln3_g', 'm_ln3_b', 'v_ffn1_w_gate', 'v_ffn1_w_up', 'v_ffn1_w_down', 'v_ln1_g', 'v_ln1_b', 'v_w_in', 'v_conv_w', 'v_conv_b', 'v_ssm_a_re', 'v_ssm_a_im', 'v_ssm_log_dt', 'v_ssm_b_re', 'v_ssm_b_im', 'v_ssm_c_re', 'v_ssm_c_im', 'v_ssm_d', 'v_ssm_w_glu', 'v_attn_sinks', 'v_w_br_conv', 'v_w_br_ssm', 'v_w_br_attn', 'v_w_out', 'v_ln2_g', 'v_ln2_b', 'v_ffn2_w_gate', 'v_ffn2_w_up', 'v_ffn2_w_down', 'v_ln3_g', 'v_ln3_b']
TWIN_OUTPUTS = ['loss', 'grad_x', 'grad_ffn1_w_gate', 'grad_ffn1_w_up', 'grad_ffn1_w_down', 'grad_ln1_g', 'grad_ln1_b', 'grad_w_in', 'grad_conv_w', 'grad_conv_b', 'grad_ssm_a_re', 'grad_ssm_a_im', 'grad_ssm_log_dt', 'grad_ssm_b_re', 'grad_ssm_b_im', 'grad_ssm_c_re', 'grad_ssm_c_im', 'grad_ssm_d', 'grad_ssm_w_glu', 'grad_attn_sinks', 'grad_w_br_conv', 'grad_w_br_ssm', 'grad_w_br_attn', 'grad_w_out', 'grad_ln2_g', 'grad_ln2_b', 'grad_ffn2_w_gate', 'grad_ffn2_w_up', 'grad_ffn2_w_down', 'grad_ln3_g', 'grad_ln3_b', 'delta_ffn1_w_gate', 'delta_ffn1_w_up', 'delta_ffn1_w_down', 'delta_ln1_g', 'delta_ln1_b', 'delta_w_in', 'delta_conv_w', 'delta_conv_b', 'delta_ssm_a_re', 'delta_ssm_a_im', 'delta_ssm_log_dt', 'delta_ssm_b_re', 'delta_ssm_b_im', 'delta_ssm_c_re', 'delta_ssm_c_im', 'delta_ssm_d', 'delta_ssm_w_glu', 'delta_attn_sinks', 'delta_w_br_conv', 'delta_w_br_ssm', 'delta_w_br_attn', 'delta_w_out', 'delta_ln2_g', 'delta_ln2_b', 'delta_ffn2_w_gate', 'delta_ffn2_w_up', 'delta_ffn2_w_down', 'delta_ln3_g', 'delta_ln3_b', 'new_m_ffn1_w_gate', 'new_m_ffn1_w_up', 'new_m_ffn1_w_down', 'new_m_ln1_g', 'new_m_ln1_b', 'new_m_w_in', 'new_m_conv_w', 'new_m_conv_b', 'new_m_ssm_a_re', 'new_m_ssm_a_im', 'new_m_ssm_log_dt', 'new_m_ssm_b_re', 'new_m_ssm_b_im', 'new_m_ssm_c_re', 'new_m_ssm_c_im', 'new_m_ssm_d', 'new_m_ssm_w_glu', 'new_m_attn_sinks', 'new_m_w_br_conv', 'new_m_w_br_ssm', 'new_m_w_br_attn', 'new_m_w_out', 'new_m_ln2_g', 'new_m_ln2_b', 'new_m_ffn2_w_gate', 'new_m_ffn2_w_up', 'new_m_ffn2_w_down', 'new_m_ln3_g', 'new_m_ln3_b', 'new_v_ffn1_w_gate', 'new_v_ffn1_w_up', 'new_v_ffn1_w_down', 'new_v_ln1_g', 'new_v_ln1_b', 'new_v_w_in', 'new_v_conv_w', 'new_v_conv_b', 'new_v_ssm_a_re', 'new_v_ssm_a_im', 'new_v_ssm_log_dt', 'new_v_ssm_b_re', 'new_v_ssm_b_im', 'new_v_ssm_c_re', 'new_v_ssm_c_im', 'new_v_ssm_d', 'new_v_ssm_w_glu', 'new_v_attn_sinks', 'new_v_w_br_conv', 'new_v_w_br_ssm', 'new_v_w_br_attn', 'new_v_w_out', 'new_v_ln2_g', 'new_v_ln2_b', 'new_v_ffn2_w_gate', 'new_v_ffn2_w_up', 'new_v_ffn2_w_down', 'new_v_ln3_g', 'new_v_ln3_b']
TWIN_LEAF_KINDS = {'loss': 'loss', 'grad_x': 'grad_x', 'grad_ffn1_w_gate': 'grad_w', 'grad_ffn1_w_up': 'grad_w', 'grad_ffn1_w_down': 'grad_w', 'grad_ln1_g': 'grad_w', 'grad_ln1_b': 'grad_w', 'grad_w_in': 'grad_w', 'grad_conv_w': 'grad_w', 'grad_conv_b': 'grad_w', 'grad_ssm_a_re': 'grad_w', 'grad_ssm_a_im': 'grad_w', 'grad_ssm_log_dt': 'grad_w', 'grad_ssm_b_re': 'grad_w', 'grad_ssm_b_im': 'grad_w', 'grad_ssm_c_re': 'grad_w', 'grad_ssm_c_im': 'grad_w', 'grad_ssm_d': 'grad_w', 'grad_ssm_w_glu': 'grad_w', 'grad_attn_sinks': 'grad_w', 'grad_w_br_conv': 'grad_w', 'grad_w_br_ssm': 'grad_w', 'grad_w_br_attn': 'grad_w', 'grad_w_out': 'grad_w', 'grad_ln2_g': 'grad_w', 'grad_ln2_b': 'grad_w', 'grad_ffn2_w_gate': 'grad_w', 'grad_ffn2_w_up': 'grad_w', 'grad_ffn2_w_down': 'grad_w', 'grad_ln3_g': 'grad_w', 'grad_ln3_b': 'grad_w', 'delta_ffn1_w_gate': 'delta_w', 'delta_ffn1_w_up': 'delta_w', 'delta_ffn1_w_down': 'delta_w', 'delta_ln1_g': 'delta_w', 'delta_ln1_b': 'delta_w', 'delta_w_in': 'delta_w', 'delta_conv_w': 'delta_w', 'delta_conv_b': 'delta_w', 'delta_ssm_a_re': 'delta_w', 'delta_ssm_a_im': 'delta_w', 'delta_ssm_log_dt': 'delta_w', 'delta_ssm_b_re': 'delta_w', 'delta_ssm_b_im': 'delta_w', 'delta_ssm_c_re': 'delta_w', 'delta_ssm_c_im': 'delta_w', 'delta_ssm_d': 'delta_w', 'delta_ssm_w_glu': 'delta_w', 'delta_attn_sinks': 'delta_w', 'delta_w_br_conv': 'delta_w', 'delta_w_br_ssm': 'delta_w', 'delta_w_br_attn': 'delta_w', 'delta_w_out': 'delta_w', 'delta_ln2_g': 'delta_w', 'delta_ln2_b': 'delta_w', 'delta_ffn2_w_gate': 'delta_w', 'delta_ffn2_w_up': 'delta_w', 'delta_ffn2_w_down': 'delta_w', 'delta_ln3_g': 'delta_w', 'delta_ln3_b': 'delta_w', 'new_m_ffn1_w_gate': 'new_m', 'new_m_ffn1_w_up': 'new_m', 'new_m_ffn1_w_down': 'new_m', 'new_m_ln1_g': 'new_m', 'new_m_ln1_b': 'new_m', 'new_m_w_in': 'new_m', 'new_m_conv_w': 'new_m', 'new_m_conv_b': 'new_m', 'new_m_ssm_a_re': 'new_m', 'new_m_ssm_a_im': 'new_m', 'new_m_ssm_log_dt': 'new_m', 'new_m_ssm_b_re': 'new_m', 'new_m_ssm_b_im': 'new_m', 'new_m_ssm_c_re': 'new_m', 'new_m_ssm_c_im': 'new_m', 'new_m_ssm_d': 'new_m', 'new_m_ssm_w_glu': 'new_m', 'new_m_attn_sinks': 'new_m', 'new_m_w_br_conv': 'new_m', 'new_m_w_br_ssm': 'new_m', 'new_m_w_br_attn': 'new_m', 'new_m_w_out': 'new_m', 'new_m_ln2_g': 'new_m', 'new_m_ln2_b': 'new_m', 'new_m_ffn2_w_gate': 'new_m', 'new_m_ffn2_w_up': 'new_m', 'new_m_ffn2_w_down': 'new_m', 'new_m_ln3_g': 'new_m', 'new_m_ln3_b': 'new_m', 'new_v_ffn1_w_gate': 'new_v', 'new_v_ffn1_w_up': 'new_v', 'new_v_ffn1_w_down': 'new_v', 'new_v_ln1_g': 'new_v', 'new_v_ln1_b': 'new_v', 'new_v_w_in': 'new_v', 'new_v_conv_w': 'new_v', 'new_v_conv_b': 'new_v', 'new_v_ssm_a_re': 'new_v', 'new_v_ssm_a_im': 'new_v', 'new_v_ssm_log_dt': 'new_v', 'new_v_ssm_b_re': 'new_v', 'new_v_ssm_b_im': 'new_v', 'new_v_ssm_c_re': 'new_v', 'new_v_ssm_c_im': 'new_v', 'new_v_ssm_d': 'new_v', 'new_v_ssm_w_glu': 'new_v', 'new_v_attn_sinks': 'new_v', 'new_v_w_br_conv': 'new_v', 'new_v_w_br_ssm': 'new_v', 'new_v_w_br_attn': 'new_v', 'new_v_w_out': 'new_v', 'new_v_ln2_g': 'new_v', 'new_v_ln2_b': 'new_v', 'new_v_ffn2_w_gate': 'new_v', 'new_v_ffn2_w_up': 'new_v', 'new_v_ffn2_w_down': 'new_v', 'new_v_ln3_g': 'new_v', 'new_v_ln3_b': 'new_v'}


def _forward(args):
    return _fwd_reference(*[args[k] for k in FWD_PARAMS])


def _output_shape():
    def fwd():
        inp = _fwd_setup_inputs(0)
        return _fwd_reference(*[inp[k] for k in FWD_PARAMS])
    out = _jax.eval_shape(fwd)
    return out.shape, out.dtype

N_MICROBATCH = 1
ADAM_LR = 0.001
ADAM_B1 = 0.9
ADAM_B2 = 0.999
ADAM_EPS = 1e-08
ADAM_WD = 0.01
ADAM_STEP = 10
PER_EXAMPLE_BATCH_AXIS = {'x': 0, 'loss_target': 0}
SHARED_INPUTS = []
_WEIGHT_DTYPES = {'ffn1_w_gate': _jnp.float32, 'ffn1_w_up': _jnp.float32, 'ffn1_w_down': _jnp.float32, 'ln1_g': _jnp.float32, 'ln1_b': _jnp.float32, 'w_in': _jnp.float32, 'conv_w': _jnp.float32, 'conv_b': _jnp.float32, 'ssm_a_re': _jnp.float32, 'ssm_a_im': _jnp.float32, 'ssm_log_dt': _jnp.float32, 'ssm_b_re': _jnp.float32, 'ssm_b_im': _jnp.float32, 'ssm_c_re': _jnp.float32, 'ssm_c_im': _jnp.float32, 'ssm_d': _jnp.float32, 'ssm_w_glu': _jnp.float32, 'attn_sinks': _jnp.float32, 'w_br_conv': _jnp.float32, 'w_br_ssm': _jnp.float32, 'w_br_attn': _jnp.float32, 'w_out': _jnp.float32, 'ln2_g': _jnp.float32, 'ln2_b': _jnp.float32, 'ffn2_w_gate': _jnp.float32, 'ffn2_w_up': _jnp.float32, 'ffn2_w_down': _jnp.float32, 'ln3_g': _jnp.float32, 'ln3_b': _jnp.float32}
MOMENT_SCALE = {'ffn1_w_gate': 1.230122e-02, 'ffn1_w_up': 1.191359e-02, 'ffn1_w_down': 4.699894e-02, 'ln1_g': 8.810484e-01, 'ln1_b': 4.772442e-01, 'w_in': 2.949919e-02, 'conv_w': 7.016351e-02, 'conv_b': 7.649992e-02, 'ssm_a_re': 1.889715e-03, 'ssm_a_im': 2.056202e-03, 'ssm_log_dt': 2.595960e+00, 'ssm_b_re': 1.228102e-03, 'ssm_b_im': 1.217893e-03, 'ssm_c_re': 1.734652e-03, 'ssm_c_im': 1.722101e-03, 'ssm_d': 2.818563e-02, 'ssm_w_glu': 7.529805e-03, 'attn_sinks': 6.663551e-03, 'w_br_conv': 3.471883e-02, 'w_br_ssm': 1.391115e-02, 'w_br_attn': 7.331396e-03, 'w_out': 8.923465e-02, 'ln2_g': 9.046354e-01, 'ln2_b': 4.724567e-01, 'ffn2_w_gate': 1.201450e-02, 'ffn2_w_up': 1.163990e-02, 'ffn2_w_down': 4.590405e-02, 'ln3_g': 3.204133e+01, 'ln3_b': 1.519885e+00}


def _to_microbatches(a, axis):
    t = _jnp.moveaxis(a, axis, 0)
    t = t.reshape((N_MICROBATCH, t.shape[0] // N_MICROBATCH) + t.shape[1:])
    return _jnp.moveaxis(t, 1, axis + 1)


def setup_inputs(seed: int = 0) -> dict:
    inp = _fwd_setup_inputs(seed)
    key = _jax.random.fold_in(_jax.random.key(seed), 7919)
    shape, _ = _output_shape()
    out = dict(inp)
    out["loss_target"] = _jax.random.normal(_jax.random.fold_in(key, 0), shape, _jnp.float32)
    for i, name in enumerate(TWIN_WEIGHTS):
        w = inp[name].astype(_jnp.float32)
        if MOMENT_SCALE is None:
            s = _jnp.sqrt(_jnp.mean(_jnp.square(w)) + 1e-30)
        else:
            s = MOMENT_SCALE[name]
        km, kv = _jax.random.split(_jax.random.fold_in(key, i + 1))
        out[name] = w
        out["m_" + name] = s * _jax.random.normal(km, w.shape, _jnp.float32)
        out["v_" + name] = (s * s) * _jax.random.uniform(kv, w.shape, _jnp.float32, 0.5, 1.5)
    if N_MICROBATCH > 1:
        for name, axis in PER_EXAMPLE_BATCH_AXIS.items():
            out[name] = _to_microbatches(out[name], axis)
    return {'x': out['x'], 'ffn1_w_gate': out['ffn1_w_gate'], 'ffn1_w_up': out['ffn1_w_up'], 'ffn1_w_down': out['ffn1_w_down'], 'ln1_g': out['ln1_g'], 'ln1_b': out['ln1_b'], 'w_in': out['w_in'], 'conv_w': out['conv_w'], 'conv_b': out['conv_b'], 'ssm_a_re': out['ssm_a_re'], 'ssm_a_im': out['ssm_a_im'], 'ssm_log_dt': out['ssm_log_dt'], 'ssm_b_re': out['ssm_b_re'], 'ssm_b_im': out['ssm_b_im'], 'ssm_c_re': out['ssm_c_re'], 'ssm_c_im': out['ssm_c_im'], 'ssm_d': out['ssm_d'], 'ssm_w_glu': out['ssm_w_glu'], 'attn_sinks': out['attn_sinks'], 'w_br_conv': out['w_br_conv'], 'w_br_ssm': out['w_br_ssm'], 'w_br_attn': out['w_br_attn'], 'w_out': out['w_out'], 'ln2_g': out['ln2_g'], 'ln2_b': out['ln2_b'], 'ffn2_w_gate': out['ffn2_w_gate'], 'ffn2_w_up': out['ffn2_w_up'], 'ffn2_w_down': out['ffn2_w_down'], 'ln3_g': out['ln3_g'], 'ln3_b': out['ln3_b'], 'loss_target': out['loss_target'], 'm_ffn1_w_gate': out['m_ffn1_w_gate'], 'm_ffn1_w_up': out['m_ffn1_w_up'], 'm_ffn1_w_down': out['m_ffn1_w_down'], 'm_ln1_g': out['m_ln1_g'], 'm_ln1_b': out['m_ln1_b'], 'm_w_in': out['m_w_in'], 'm_conv_w': out['m_conv_w'], 'm_conv_b': out['m_conv_b'], 'm_ssm_a_re': out['m_ssm_a_re'], 'm_ssm_a_im': out['m_ssm_a_im'], 'm_ssm_log_dt': out['m_ssm_log_dt'], 'm_ssm_b_re': out['m_ssm_b_re'], 'm_ssm_b_im': out['m_ssm_b_im'], 'm_ssm_c_re': out['m_ssm_c_re'], 'm_ssm_c_im': out['m_ssm_c_im'], 'm_ssm_d': out['m_ssm_d'], 'm_ssm_w_glu': out['m_ssm_w_glu'], 'm_attn_sinks': out['m_attn_sinks'], 'm_w_br_conv': out['m_w_br_conv'], 'm_w_br_ssm': out['m_w_br_ssm'], 'm_w_br_attn': out['m_w_br_attn'], 'm_w_out': out['m_w_out'], 'm_ln2_g': out['m_ln2_g'], 'm_ln2_b': out['m_ln2_b'], 'm_ffn2_w_gate': out['m_ffn2_w_gate'], 'm_ffn2_w_up': out['m_ffn2_w_up'], 'm_ffn2_w_down': out['m_ffn2_w_down'], 'm_ln3_g': out['m_ln3_g'], 'm_ln3_b': out['m_ln3_b'], 'v_ffn1_w_gate': out['v_ffn1_w_gate'], 'v_ffn1_w_up': out['v_ffn1_w_up'], 'v_ffn1_w_down': out['v_ffn1_w_down'], 'v_ln1_g': out['v_ln1_g'], 'v_ln1_b': out['v_ln1_b'], 'v_w_in': out['v_w_in'], 'v_conv_w': out['v_conv_w'], 'v_conv_b': out['v_conv_b'], 'v_ssm_a_re': out['v_ssm_a_re'], 'v_ssm_a_im': out['v_ssm_a_im'], 'v_ssm_log_dt': out['v_ssm_log_dt'], 'v_ssm_b_re': out['v_ssm_b_re'], 'v_ssm_b_im': out['v_ssm_b_im'], 'v_ssm_c_re': out['v_ssm_c_re'], 'v_ssm_c_im': out['v_ssm_c_im'], 'v_ssm_d': out['v_ssm_d'], 'v_ssm_w_glu': out['v_ssm_w_glu'], 'v_attn_sinks': out['v_attn_sinks'], 'v_w_br_conv': out['v_w_br_conv'], 'v_w_br_ssm': out['v_w_br_ssm'], 'v_w_br_attn': out['v_w_br_attn'], 'v_w_out': out['v_w_out'], 'v_ln2_g': out['v_ln2_g'], 'v_ln2_b': out['v_ln2_b'], 'v_ffn2_w_gate': out['v_ffn2_w_gate'], 'v_ffn2_w_up': out['v_ffn2_w_up'], 'v_ffn2_w_down': out['v_ffn2_w_down'], 'v_ln3_g': out['v_ln3_g'], 'v_ln3_b': out['v_ln3_b']}


def _loss(weights, diff, rest, loss_target):
    with _jax.named_scope("forward"):
        args = {**rest, TWIN_DIFF_INPUT: diff, **{k: w.astype(_WEIGHT_DTYPES[k]) for k, w in weights.items()}}
        y = _forward(args)
    with _jax.named_scope("loss_head"):
        err = _jnp.square(y.astype(_jnp.float32) - loss_target)
        return 0.5 * _jnp.sum(_jnp.mean(err, axis=-1)) if err.ndim else 0.5 * err


def _adamw(w, g, m, v):
    m = ADAM_B1 * m + (1.0 - ADAM_B1) * g
    v = ADAM_B2 * v + (1.0 - ADAM_B2) * _jnp.square(g)
    m_hat = m / (1.0 - ADAM_B1 ** ADAM_STEP)
    v_hat = v / (1.0 - ADAM_B2 ** ADAM_STEP)
    delta = -ADAM_LR * (m_hat / (_jnp.sqrt(v_hat) + ADAM_EPS) + ADAM_WD * w)
    return delta, m, v


def reference(x, ffn1_w_gate, ffn1_w_up, ffn1_w_down, ln1_g, ln1_b, w_in, conv_w, conv_b, ssm_a_re, ssm_a_im, ssm_log_dt, ssm_b_re, ssm_b_im, ssm_c_re, ssm_c_im, ssm_d, ssm_w_glu, attn_sinks, w_br_conv, w_br_ssm, w_br_attn, w_out, ln2_g, ln2_b, ffn2_w_gate, ffn2_w_up, ffn2_w_down, ln3_g, ln3_b, loss_target, m_ffn1_w_gate, m_ffn1_w_up, m_ffn1_w_down, m_ln1_g, m_ln1_b, m_w_in, m_conv_w, m_conv_b, m_ssm_a_re, m_ssm_a_im, m_ssm_log_dt, m_ssm_b_re, m_ssm_b_im, m_ssm_c_re, m_ssm_c_im, m_ssm_d, m_ssm_w_glu, m_attn_sinks, m_w_br_conv, m_w_br_ssm, m_w_br_attn, m_w_out, m_ln2_g, m_ln2_b, m_ffn2_w_gate, m_ffn2_w_up, m_ffn2_w_down, m_ln3_g, m_ln3_b, v_ffn1_w_gate, v_ffn1_w_up, v_ffn1_w_down, v_ln1_g, v_ln1_b, v_w_in, v_conv_w, v_conv_b, v_ssm_a_re, v_ssm_a_im, v_ssm_log_dt, v_ssm_b_re, v_ssm_b_im, v_ssm_c_re, v_ssm_c_im, v_ssm_d, v_ssm_w_glu, v_attn_sinks, v_w_br_conv, v_w_br_ssm, v_w_br_attn, v_w_out, v_ln2_g, v_ln2_b, v_ffn2_w_gate, v_ffn2_w_up, v_ffn2_w_down, v_ln3_g, v_ln3_b):
    given = dict(x=x, ffn1_w_gate=ffn1_w_gate, ffn1_w_up=ffn1_w_up, ffn1_w_down=ffn1_w_down, ln1_g=ln1_g, ln1_b=ln1_b, w_in=w_in, conv_w=conv_w, conv_b=conv_b, ssm_a_re=ssm_a_re, ssm_a_im=ssm_a_im, ssm_log_dt=ssm_log_dt, ssm_b_re=ssm_b_re, ssm_b_im=ssm_b_im, ssm_c_re=ssm_c_re, ssm_c_im=ssm_c_im, ssm_d=ssm_d, ssm_w_glu=ssm_w_glu, attn_sinks=attn_sinks, w_br_conv=w_br_conv, w_br_ssm=w_br_ssm, w_br_attn=w_br_attn, w_out=w_out, ln2_g=ln2_g, ln2_b=ln2_b, ffn2_w_gate=ffn2_w_gate, ffn2_w_up=ffn2_w_up, ffn2_w_down=ffn2_w_down, ln3_g=ln3_g, ln3_b=ln3_b, loss_target=loss_target, m_ffn1_w_gate=m_ffn1_w_gate, m_ffn1_w_up=m_ffn1_w_up, m_ffn1_w_down=m_ffn1_w_down, m_ln1_g=m_ln1_g, m_ln1_b=m_ln1_b, m_w_in=m_w_in, m_conv_w=m_conv_w, m_conv_b=m_conv_b, m_ssm_a_re=m_ssm_a_re, m_ssm_a_im=m_ssm_a_im, m_ssm_log_dt=m_ssm_log_dt, m_ssm_b_re=m_ssm_b_re, m_ssm_b_im=m_ssm_b_im, m_ssm_c_re=m_ssm_c_re, m_ssm_c_im=m_ssm_c_im, m_ssm_d=m_ssm_d, m_ssm_w_glu=m_ssm_w_glu, m_attn_sinks=m_attn_sinks, m_w_br_conv=m_w_br_conv, m_w_br_ssm=m_w_br_ssm, m_w_br_attn=m_w_br_attn, m_w_out=m_w_out, m_ln2_g=m_ln2_g, m_ln2_b=m_ln2_b, m_ffn2_w_gate=m_ffn2_w_gate, m_ffn2_w_up=m_ffn2_w_up, m_ffn2_w_down=m_ffn2_w_down, m_ln3_g=m_ln3_g, m_ln3_b=m_ln3_b, v_ffn1_w_gate=v_ffn1_w_gate, v_ffn1_w_up=v_ffn1_w_up, v_ffn1_w_down=v_ffn1_w_down, v_ln1_g=v_ln1_g, v_ln1_b=v_ln1_b, v_w_in=v_w_in, v_conv_w=v_conv_w, v_conv_b=v_conv_b, v_ssm_a_re=v_ssm_a_re, v_ssm_a_im=v_ssm_a_im, v_ssm_log_dt=v_ssm_log_dt, v_ssm_b_re=v_ssm_b_re, v_ssm_b_im=v_ssm_b_im, v_ssm_c_re=v_ssm_c_re, v_ssm_c_im=v_ssm_c_im, v_ssm_d=v_ssm_d, v_ssm_w_glu=v_ssm_w_glu, v_attn_sinks=v_attn_sinks, v_w_br_conv=v_w_br_conv, v_w_br_ssm=v_w_br_ssm, v_w_br_attn=v_w_br_attn, v_w_out=v_w_out, v_ln2_g=v_ln2_g, v_ln2_b=v_ln2_b, v_ffn2_w_gate=v_ffn2_w_gate, v_ffn2_w_up=v_ffn2_w_up, v_ffn2_w_down=v_ffn2_w_down, v_ln3_g=v_ln3_g, v_ln3_b=v_ln3_b)
    weights = {n: given[n] for n in TWIN_WEIGHTS}
    shared = {n: given[n] for n in SHARED_INPUTS}
    per_example = {n: given[n] for n in ['x']}
    grad_fn = _jax.value_and_grad(_loss, argnums=(0, 1))

    def one_microbatch(ex, loss_target):
        ex = dict(ex)
        diff = ex.pop(TWIN_DIFF_INPUT)
        return grad_fn(weights, diff, {**shared, **ex}, loss_target)

    if N_MICROBATCH == 1:
        loss, (grad_w, grad_x) = one_microbatch(per_example, given["loss_target"])
    else:
        def body(carry, xs):
            loss_sum, grad_sum = carry
            l_k, (gw_k, gx_k) = one_microbatch(xs[0], xs[1])
            with _jax.named_scope("update"):
                return (loss_sum + l_k, _jax.tree.map(_jnp.add, grad_sum, gw_k)), gx_k

        init = (_jnp.zeros((), _jnp.float32), _jax.tree.map(_jnp.zeros_like, weights))
        (loss, grad_w), grad_x = _jax.lax.scan(body, init, (per_example, given["loss_target"]))
    with _jax.named_scope("update"):
        delta_w, new_m, new_v = {}, {}, {}
        for n in TWIN_WEIGHTS:
            delta_w[n], new_m[n], new_v[n] = _adamw(weights[n], grad_w[n], given["m_" + n], given["v_" + n])
    return (loss, grad_x, *[grad_w[n] for n in TWIN_WEIGHTS], *[delta_w[n] for n in TWIN_WEIGHTS],
            *[new_m[n] for n in TWIN_WEIGHTS], *[new_v[n] for n in TWIN_WEIGHTS])
```

```python
import functools
import math

import jax
import jax.numpy as jnp
from jax import lax
from jax.experimental import pallas as pl
from jax.experimental.pallas import tpu as pltpu

F32 = jnp.float32
BF16 = jnp.bfloat16

LN_EPS = 1e-5
D_CONV = 256
D_SSM = 256
N_GROUPS = 16
SSM_GROUP = 16
SSM_STATE = 64
N_Q_HEADS = 8
HEAD_DIM = 64
D_ATTN = 512
D_KV = 128
BLOCK = 128
D_A = 3 * D_CONV + D_SSM + D_ATTN + 2 * D_KV
ADAM_LR = 0.001
ADAM_B1 = 0.9
ADAM_B2 = 0.999
ADAM_EPS = 1e-08
ADAM_WD = 0.01
ADAM_STEP = 10

VMEM_LIMIT_BYTES = 56 * 1024 * 1024
MESH = pl.DeviceIdType.MESH
N_CHIPS = 4
N_DEV = 8


def _params(sem=None):
    return pltpu.CompilerParams(dimension_semantics=sem, vmem_limit_bytes=VMEM_LIMIT_BYTES)


def _op(op, block, imap):
    if isinstance(op, tuple):
        arr, l = op
        return arr, pl.BlockSpec((None,) + block, lambda *g: (l,) + imap(*g))
    return op, pl.BlockSpec(block, imap)


def _mm(pairs, mode, m, n, k, out_dtypes, *, name, tm=512, tn=512, tk=512, add=None, a_off=0, b_off=0):
    tm, tn, tk = min(tm, m), min(tn, n), min(tk, k)
    assert m % tm == 0 and n % tn == 0 and k % tk == 0 and b_off % tn == 0, (name, m, n, k, tm, tn, tk)
    nk, npair, jo = k // tk, len(pairs), b_off // tn
    if mode == "nn":
        ao = a_off // tk
        ab, ai, bb, bi = (tm, tk), (lambda i, j, kk: (i, kk + ao)), (tk, tn), (lambda i, j, kk: (kk, j))
        dims = (((1,), (0,)), ((), ()))
    elif mode == "nt":
        ao = a_off // tk
        ab, ai, bb, bi = (tm, tk), (lambda i, j, kk: (i, kk + ao)), (tn, tk), (lambda i, j, kk: (j + jo, kk))
        dims = (((1,), (1,)), ((), ()))
    else:
        ao = a_off // tm
        ab, ai, bb, bi = (tk, tm), (lambda i, j, kk: (kk, i + ao)), (tk, tn), (lambda i, j, kk: (kk, j))
        dims = (((0,), (0,)), ((), ()))
    assert a_off % (tm if mode == "tn" else tk) == 0, (name, a_off)
    ops, specs = [], []
    for a, b in pairs:
        for o, blk, im in ((a, ab, ai), (b, bb, bi)):
            arr, sp = _op(o, blk, im)
            ops.append(arr)
            specs.append(sp)
    has_add = add is not None
    if has_add:
        ops.append(add)
        specs.append(pl.BlockSpec((tm, tn), lambda i, j, kk: (i, j)))
    nout = len(out_dtypes)

    def body(*refs):
        outs, acc = refs[2 * npair + has_add:2 * npair + has_add + nout], refs[-1]
        kk = pl.program_id(2)

        @pl.when(kk == 0)
        def _():
            acc[...] = jnp.zeros_like(acc)

        t = None
        for p in range(npair):
            d = lax.dot_general(refs[2 * p][...].astype(BF16), refs[2 * p + 1][...].astype(BF16), dims,
                                preferred_element_type=F32)
            t = d if t is None else t + d
        acc[...] += t

        @pl.when(kk == nk - 1)
        def _():
            r = acc[...]
            if has_add:
                r = r + refs[2 * npair][...]
            for o in outs:
                o[...] = r.astype(o.dtype)

    res = pl.pallas_call(
        body, grid=(m // tm, n // tn, nk), in_specs=specs,
        out_specs=[pl.BlockSpec((tm, tn), lambda i, j, kk: (i, j))] * nout,
        out_shape=[jax.ShapeDtypeStruct((m, n), dt) for dt in out_dtypes],
        scratch_shapes=[pltpu.VMEM((tm, tn), F32)],
        compiler_params=_params(("parallel", "parallel", "arbitrary")), name=name)(*ops)
    return res[0] if nout == 1 else res


def _rows(width, col=0, tm=None):
    return pl.BlockSpec((tm, width), lambda i: (i, col))


def _whole(shape):
    nd = len(shape)
    return pl.BlockSpec(shape, lambda i: (0,) * nd)


def _sigmoid(x):
    return 1.0 / (1.0 + jnp.exp(-x))


def _ln_fwd(x, f, g, b, s, alpha, name):
    S, Dm = x.shape
    tm = min(256, S)

    def body(x_ref, f_ref, g_ref, b_ref, y_ref, xh_ref, rs_ref):
        z = alpha * x_ref[...] + s * f_ref[...]
        mu = jnp.mean(z, axis=-1, keepdims=True)
        zc = z - mu
        var = jnp.mean(zc * zc, axis=-1, keepdims=True)
        rstd = lax.rsqrt(var + LN_EPS)
        xh = zc * rstd
        y_ref[...] = xh * g_ref[...] + b_ref[...]
        xh_ref[...] = xh
        rs_ref[...] = rstd

    return pl.pallas_call(
        body, grid=(S // tm,),
        in_specs=[_rows(Dm, tm=tm), _rows(Dm, tm=tm), _whole((1, Dm)), _whole((1, Dm))],
        out_specs=[_rows(Dm, tm=tm), _rows(Dm, tm=tm), _rows(1, tm=tm)],
        out_shape=[jax.ShapeDtypeStruct((S, Dm), F32), jax.ShapeDtypeStruct((S, Dm), F32),
                   jax.ShapeDtypeStruct((S, 1), F32)],
        compiler_params=_params(("parallel",)), name=name)(x, f, g, b)


def _ln_bwd(dy, xh, rs, g, s, alpha, name):
    S, Dm = dy.shape
    tm = min(256, S)

    def body(dy_ref, xh_ref, rs_ref, g_ref, dres_ref, dbr_ref, dg_ref, db_ref):
        @pl.when(pl.program_id(0) == 0)
        def _():
            dg_ref[...] = jnp.zeros_like(dg_ref)
            db_ref[...] = jnp.zeros_like(db_ref)

        dy, xh = dy_ref[...], xh_ref[...]
        dyg = dy * g_ref[...]
        m1 = jnp.mean(dyg, axis=-1, keepdims=True)
        m2 = jnp.mean(dyg * xh, axis=-1, keepdims=True)
        dz = rs_ref[...] * (dyg - m1 - xh * m2)
        dres_ref[...] = alpha * dz
        dbr_ref[...] = (s * dz).astype(BF16)
        dg_ref[...] += jnp.sum(dy * xh, axis=0, keepdims=True)
        db_ref[...] += jnp.sum(dy, axis=0, keepdims=True)

    return pl.pallas_call(
        body, grid=(S // tm,),
        in_specs=[_rows(Dm, tm=tm), _rows(Dm, tm=tm), _rows(1, tm=tm), _whole((1, Dm))],
        out_specs=[_rows(Dm, tm=tm), _rows(Dm, tm=tm), _whole((1, Dm)), _whole((1, Dm))],
        out_shape=[jax.ShapeDtypeStruct((S, Dm), F32), jax.ShapeDtypeStruct((S, Dm), BF16),
                   jax.ShapeDtypeStruct((1, Dm), F32), jax.ShapeDtypeStruct((1, Dm), F32)],
        compiler_params=_params(("arbitrary",)), name=name)(dy, xh, rs, g)


def _ffn_up(x, wg, wu, dff, name):
    S, Dm = x.shape
    tm, tn = min(1024, S), 256

    def body(x_ref, wg_ref, wu_ref, a_ref, b_ref, h_ref):
        xb = x_ref[...].astype(BF16)
        dims = (((1,), (1,)), ((), ()))
        a = lax.dot_general(xb, wg_ref[...], dims, preferred_element_type=F32)
        b = lax.dot_general(xb, wu_ref[...], dims, preferred_element_type=F32)
        a_ref[...] = a
        b_ref[...] = b
        h_ref[...] = (a * _sigmoid(a) * b).astype(BF16)

    wga, wgs = _op(wg, (tn, Dm), lambda i, j: (j, 0))
    wua, wus = _op(wu, (tn, Dm), lambda i, j: (j, 0))
    ob = pl.BlockSpec((tm, tn), lambda i, j: (i, j))
    return pl.pallas_call(
        body, grid=(S // tm, dff // tn),
        in_specs=[pl.BlockSpec((tm, Dm), lambda i, j: (i, 0)), wgs, wus], out_specs=[ob, ob, ob],
        out_shape=[jax.ShapeDtypeStruct((S, dff), F32), jax.ShapeDtypeStruct((S, dff), F32),
                   jax.ShapeDtypeStruct((S, dff), BF16)],
        compiler_params=_params(("parallel", "parallel")), name=name)(x, wga, wua)


def _ffn_dh(df, wd, a, b, name):
    S, Dm = df.shape
    dff = a.shape[1]
    tm, tn = min(1024, S), 256

    def body(df_ref, wd_ref, a_ref, b_ref, da_ref, db_ref):
        dh = lax.dot_general(df_ref[...], wd_ref[...], (((1,), (1,)), ((), ())), preferred_element_type=F32)
        a, b = a_ref[...], b_ref[...]
        sg = _sigmoid(a)
        da_ref[...] = (dh * b * (sg * (1.0 + a * (1.0 - sg)))).astype(BF16)
        db_ref[...] = (dh * (a * sg)).astype(BF16)

    wda, wds = _op(wd, (tn, Dm), lambda i, j: (j, 0))
    ob = pl.BlockSpec((tm, tn), lambda i, j: (i, j))
    return pl.pallas_call(
        body, grid=(S // tm, dff // tn),
        in_specs=[pl.BlockSpec((tm, Dm), lambda i, j: (i, 0)), wds, ob, ob], out_specs=[ob, ob],
        out_shape=[jax.ShapeDtypeStruct((S, dff), BF16), jax.ShapeDtypeStruct((S, dff), BF16)],
        compiler_params=_params(("parallel", "parallel")), name=name)(df, wda, a, b)


def _halo_prev(width, col, tm):
    return pl.BlockSpec((8, width), lambda i: (jnp.maximum(i * (tm // 8) - 1, 0), col))


def _halo_next(width, col, tm, S):
    return pl.BlockSpec((8, width), lambda i: (jnp.minimum((i + 1) * (tm // 8), S // 8 - 1), col))


def _shift_down(prev8, cur, n):
    ext = jnp.concatenate([prev8, cur], axis=0)
    return pltpu.roll(ext, n, axis=0)[8:]


def _shift_up(cur, next8, n):
    ext = jnp.concatenate([cur, next8], axis=0)
    return pltpu.roll(ext, ext.shape[0] - n, axis=0)[:cur.shape[0]]


def _conv_fwd(proj_a, conv_w, conv_b, name):
    S = proj_a.shape[0]
    tm = min(512, S)
    C = D_CONV

    def body(bg_ref, cg_ref, h_ref, cgp_ref, hp_ref, w_ref, cb_ref, out_ref):
        z = cg_ref[...] * h_ref[...]
        zp = jnp.where(pl.program_id(0) > 0, cgp_ref[...] * hp_ref[...], 0.0)
        w = w_ref[...]
        y = w[2:3] * z + w[1:2] * _shift_down(zp, z, 1) + w[0:1] * _shift_down(zp, z, 2) + cb_ref[...]
        out_ref[...] = (bg_ref[...] * y).astype(BF16)

    return pl.pallas_call(
        body, grid=(S // tm,),
        in_specs=[_rows(C, 0, tm), _rows(C, 1, tm), _rows(C, 2, tm), _halo_prev(C, 1, tm), _halo_prev(C, 2, tm),
                  _whole((3, C)), _whole((1, C))],
        out_specs=_rows(C, 0, tm), out_shape=jax.ShapeDtypeStruct((S, C), BF16),
        compiler_params=_params(("parallel",)), name=name)(proj_a, proj_a, proj_a, proj_a, proj_a, conv_w, conv_b)


def _conv_bwd(proj_a, dmc, conv_w, conv_b, name):
    S = proj_a.shape[0]
    tm = min(512, S)
    C = D_CONV
    nblk = S // tm

    def body(bg_ref, cg_ref, h_ref, cgp_ref, hp_ref, bgn_ref, d_ref, dn_ref, w_ref, cb_ref, out_ref, dw_ref,
             dcb_ref):
        i = pl.program_id(0)

        @pl.when(i == 0)
        def _():
            dw_ref[...] = jnp.zeros_like(dw_ref)
            dcb_ref[...] = jnp.zeros_like(dcb_ref)

        bg, cg, h, d = bg_ref[...], cg_ref[...], h_ref[...], d_ref[...]
        z = cg * h
        zp = jnp.where(i > 0, cgp_ref[...] * hp_ref[...], 0.0)
        w = w_ref[...]
        z1, z2 = _shift_down(zp, z, 1), _shift_down(zp, z, 2)
        y = w[2:3] * z + w[1:2] * z1 + w[0:1] * z2 + cb_ref[...]
        dy = d * bg
        dyn = jnp.where(i < nblk - 1, dn_ref[...] * bgn_ref[...], 0.0)
        dz = w[2:3] * dy + w[1:2] * _shift_up(dy, dyn, 1) + w[0:1] * _shift_up(dy, dyn, 2)
        out_ref[:, 0:C] = (d * y).astype(BF16)
        out_ref[:, C:2 * C] = (dz * h).astype(BF16)
        out_ref[:, 2 * C:3 * C] = (dz * cg).astype(BF16)
        dw_ref[0:1, :] += jnp.sum(dy * z2, axis=0, keepdims=True)
        dw_ref[1:2, :] += jnp.sum(dy * z1, axis=0, keepdims=True)
        dw_ref[2:3, :] += jnp.sum(dy * z, axis=0, keepdims=True)
        dcb_ref[...] += jnp.sum(dy, axis=0, keepdims=True)

    return pl.pallas_call(
        body, grid=(nblk,),
        in_specs=[_rows(C, 0, tm), _rows(C, 1, tm), _rows(C, 2, tm), _halo_prev(C, 1, tm), _halo_prev(C, 2, tm),
                  _halo_next(C, 0, tm, S), _rows(C, 0, tm), _halo_next(C, 0, tm, S), _whole((3, C)), _whole((1, C))],
        out_specs=[_rows(3 * C, 0, tm), _whole((3, C)), _whole((1, C))],
        out_shape=[jax.ShapeDtypeStruct((S, 3 * C), BF16), jax.ShapeDtypeStruct((3, C), F32),
                   jax.ShapeDtypeStruct((1, C), F32)],
        compiler_params=_params(("arbitrary",)), name=name)(
            proj_a, proj_a, proj_a, proj_a, proj_a, proj_a, dmc, dmc, conv_w, conv_b)


def _disc_math(lr, li, ldt, br, bi):
    dt = jnp.exp(ldt)
    mag = jnp.exp(lr * dt)
    ang = li * dt
    are = mag * jnp.cos(ang)
    aim = mag * jnp.sin(ang)
    nr = are - 1.0
    den = lr * lr + li * li
    cre = (nr * lr + aim * li) / den
    cim = (aim * lr - nr * li) / den
    return are, aim, cre * br - cim * bi, cre * bi + cim * br


def _disc_fwd(lr, li, ldt, br, bi):
    shapes = [lr.shape, lr.shape, br.shape, br.shape]

    def body(lr_ref, li_ref, ldt_ref, br_ref, bi_ref, *outs):
        for o, v in zip(outs, _disc_math(lr_ref[...], li_ref[...], ldt_ref[...], br_ref[...], bi_ref[...])):
            o[...] = v

    return pl.pallas_call(body, out_shape=[jax.ShapeDtypeStruct(s, F32) for s in shapes],
                          compiler_params=_params(), name="ssm_disc")(lr, li, ldt, br, bi)


def _disc_bwd(lr, li, ldt, br, bi, cts):
    shapes = [lr.shape, lr.shape, ldt.shape, br.shape, br.shape]

    def body(lr_ref, li_ref, ldt_ref, br_ref, bi_ref, c0, c1, c2, c3, *outs):
        _, vjp = jax.vjp(_disc_math, lr_ref[...], li_ref[...], ldt_ref[...], br_ref[...], bi_ref[...])
        for o, v in zip(outs, vjp((c0[...], c1[...], c2[...], c3[...]))):
            o[...] = v

    return pl.pallas_call(body, out_shape=[jax.ShapeDtypeStruct(s, F32) for s in shapes],
                          compiler_params=_params(), name="ssm_disc_bwd")(lr, li, ldt, br, bi, *cts)


def _scan_fwd(bu, abar, name):
    S = bu.shape[0]
    tb = min(256, S)

    def body(bu_ref, a_ref, xs_ref, st_ref):
        @pl.when(pl.program_id(0) == 0)
        def _():
            st_ref[...] = jnp.zeros_like(st_ref)

        ar, ai = a_ref[0:8, :], a_ref[8:16, :]

        def step(t, c):
            xr, xi = c
            nr = ar * xr - ai * xi + bu_ref[t, 0:8, :]
            ni = ar * xi + ai * xr + bu_ref[t, 8:16, :]
            xs_ref[t, 0:8, :] = nr
            xs_ref[t, 8:16, :] = ni
            return nr, ni

        xr, xi = lax.fori_loop(0, tb, step, (st_ref[0:8, :], st_ref[8:16, :]), unroll=8)
        st_ref[0:8, :] = xr
        st_ref[8:16, :] = xi

    blk = pl.BlockSpec((tb, 16, 128), lambda i: (i, 0, 0))
    return pl.pallas_call(
        body, grid=(S // tb,), in_specs=[blk, _whole((16, 128))], out_specs=blk,
        out_shape=jax.ShapeDtypeStruct((S, 16, 128), F32), scratch_shapes=[pltpu.VMEM((16, 128), F32)],
        compiler_params=_params(("arbitrary",)), name=name)(bu, abar)


def _scan_bwd(dxs, xs, abar, name):
    S = dxs.shape[0]
    tb = min(256, S)
    nblk = S // tb

    def body(d_ref, x_ref, xp_ref, a_ref, lam_ref, da_ref, st_ref):
        i = pl.program_id(0)

        @pl.when(i == 0)
        def _():
            st_ref[...] = jnp.zeros_like(st_ref)
            da_ref[...] = jnp.zeros_like(da_ref)

        ar, ai = a_ref[0:8, :], a_ref[8:16, :]

        def one(t, c, pr, pi):
            lr, li, gr, gi = c
            nr = d_ref[t, 0:8, :] + ar * lr + ai * li
            ni = d_ref[t, 8:16, :] - ai * lr + ar * li
            lam_ref[t, 0:8, :] = nr
            lam_ref[t, 8:16, :] = ni
            return nr, ni, gr + nr * pr + ni * pi, gi - nr * pi + ni * pr

        def step(s, c):
            t = tb - 1 - s
            return one(t, c, x_ref[t - 1, 0:8, :], x_ref[t - 1, 8:16, :])

        c = (st_ref[0:8, :], st_ref[8:16, :], jnp.zeros((8, 128), F32), jnp.zeros((8, 128), F32))
        c = lax.fori_loop(0, tb - 1, step, c, unroll=8)
        first = i == nblk - 1
        pr = jnp.where(first, 0.0, xp_ref[0, 0:8, :])
        pi = jnp.where(first, 0.0, xp_ref[0, 8:16, :])
        lr, li, gr, gi = one(0, c, pr, pi)
        st_ref[0:8, :] = lr
        st_ref[8:16, :] = li
        da_ref[0:8, :] += gr
        da_ref[8:16, :] += gi

    blk = pl.BlockSpec((tb, 16, 128), lambda i: (nblk - 1 - i, 0, 0))
    prev = pl.BlockSpec((1, 16, 128), lambda i: (jnp.maximum((nblk - 1 - i) * tb - 1, 0), 0, 0))
    return pl.pallas_call(
        body, grid=(nblk,), in_specs=[blk, blk, prev, _whole((16, 128))], out_specs=[blk, _whole((16, 128))],
        out_shape=[jax.ShapeDtypeStruct((S, 16, 128), F32), jax.ShapeDtypeStruct((16, 128), F32)],
        scratch_shapes=[pltpu.VMEM((16, 128), F32)],
        compiler_params=_params(("arbitrary",)), name=name)(dxs, xs, xs, abar)


def _gelu(x):
    return 0.5 * x * (1.0 + jnp.tanh(0.7978845608028654 * (x + 0.044715 * x * x * x)))


def _gelu_grad(x):
    t = jnp.tanh(0.7978845608028654 * (x + 0.044715 * x * x * x))
    return 0.5 * (1.0 + t) + 0.5 * x * (1.0 - t * t) * 0.7978845608028654 * (1.0 + 3.0 * 0.044715 * x * x)


def _ssm_out(xs, proj_a, cfull, dskip, wglu, name):
    S = xs.shape[0]
    tm = min(256, S)
    C = D_SSM

    def body(xs_ref, u_ref, c_ref, d_ref, wg_ref, y1_ref, ms_ref):
        y1 = jnp.dot(xs_ref[...].astype(BF16), c_ref[...], preferred_element_type=F32) + d_ref[...] * u_ref[...]
        y2 = _gelu(y1)
        gl = jnp.dot(y2.astype(BF16), wg_ref[...], preferred_element_type=F32)
        y1_ref[...] = y1
        ms_ref[...] = (y2 * _sigmoid(gl)).astype(BF16)

    wga, wgs = _op(wglu, (C, C), lambda i: (0, 0))
    return pl.pallas_call(
        body, grid=(S // tm,),
        in_specs=[_rows(2 * 1024, 0, tm), _rows(C, 3, tm), _whole((2 * 1024, C)), _whole((1, C)), wgs],
        out_specs=[_rows(C, 0, tm), _rows(C, 0, tm)],
        out_shape=[jax.ShapeDtypeStruct((S, C), F32), jax.ShapeDtypeStruct((S, C), BF16)],
        compiler_params=_params(("parallel",)), name=name)(xs, proj_a, cfull, dskip, wga)


def _ssm_out_bwd(dms, y1, proj_a, cfull, dskip, wglu, name):
    S = y1.shape[0]
    tm = min(256, S)
    C = D_SSM

    def body(dms_ref, y1_ref, u_ref, c_ref, d_ref, wg_ref, dy1_ref, y2_ref, dgl_ref, dxs_ref, du_ref, dd_ref):
        @pl.when(pl.program_id(0) == 0)
        def _():
            dd_ref[...] = jnp.zeros_like(dd_ref)

        dms, y1 = dms_ref[...], y1_ref[...]
        y2 = _gelu(y1)
        y2b = y2.astype(BF16)
        sg = _sigmoid(jnp.dot(y2b, wg_ref[...], preferred_element_type=F32))
        dgl = (dms * y2 * sg * (1.0 - sg)).astype(BF16)
        dy2 = dms * sg + lax.dot_general(dgl, wg_ref[...], (((1,), (1,)), ((), ())), preferred_element_type=F32)
        dy1 = dy2 * _gelu_grad(y1)
        dy1b = dy1.astype(BF16)
        dy1_ref[...] = dy1b
        y2_ref[...] = y2b
        dgl_ref[...] = dgl
        dxs_ref[...] = lax.dot_general(dy1b, c_ref[...], (((1,), (1,)), ((), ())), preferred_element_type=F32)
        du_ref[...] = d_ref[...] * dy1
        dd_ref[...] += jnp.sum(dy1 * u_ref[...], axis=0, keepdims=True)

    wga, wgs = _op(wglu, (C, C), lambda i: (0, 0))
    rc = _rows(C, 0, tm)
    return pl.pallas_call(
        body, grid=(S // tm,),
        in_specs=[rc, rc, _rows(C, 3, tm), _whole((2 * 1024, C)), _whole((1, C)), wgs],
        out_specs=[rc, rc, rc, _rows(2 * 1024, 0, tm), rc, _whole((1, C))],
        out_shape=[jax.ShapeDtypeStruct((S, C), BF16), jax.ShapeDtypeStruct((S, C), BF16),
                   jax.ShapeDtypeStruct((S, C), BF16), jax.ShapeDtypeStruct((S, 2 * 1024), F32),
                   jax.ShapeDtypeStruct((S, C), F32), jax.ShapeDtypeStruct((1, C), F32)],
        compiler_params=_params(("arbitrary",)), name=name)(dms, y1, proj_a, cfull, dskip, wga)


def _attn_mask(i):
    row = lax.broadcasted_iota(jnp.int32, (BLOCK, 2 * BLOCK), 0)
    col = lax.broadcasted_iota(jnp.int32, (BLOCK, 2 * BLOCK), 1)
    return (col > row) & (col <= row + BLOCK) & ((col >= BLOCK) | (i > 0))


def _attn_probs(qm, ksel, mask, sink):
    s = lax.dot_general(qm, ksel, (((1,), (1,)), ((), ())), preferred_element_type=F32)
    s = jnp.where(mask, s, -1e30)
    m = jnp.maximum(jnp.max(s, axis=1, keepdims=True), sink)
    p = jnp.exp(s - m)
    es = jnp.exp(sink - m)
    den = jnp.sum(p, axis=1, keepdims=True) + es
    return p / den, es / den


def _attn_fwd(proj_a, sinks, name):
    S = proj_a.shape[0]
    nb = S // BLOCK

    def body(q_ref, kp_ref, kc_ref, vp_ref, vc_ref, s_ref, out_ref):
        i = pl.program_id(0)
        mask = _attn_mask(i)
        lane = lax.broadcasted_iota(jnp.int32, (BLOCK, 128), 1)
        kk = jnp.concatenate([kp_ref[...], kc_ref[...]], axis=0).astype(BF16)
        vv = jnp.concatenate([vp_ref[...], vc_ref[...]], axis=0).astype(BF16)
        kk_r, vv_r = pltpu.roll(kk, 64, axis=1), pltpu.roll(vv, 64, axis=1)
        for j in range(4):
            qj = (q_ref[:, 128 * j:128 * (j + 1)] * (HEAD_DIM ** -0.5)).astype(BF16)
            outj = jnp.zeros((BLOCK, 128), F32)
            for e in range(2):
                hq = 2 * j + e
                own = (lane >= 64) if e else (lane < 64)
                aligned = e == hq // 4
                qm = jnp.where(own, qj, jnp.zeros_like(qj))
                pn, _ = _attn_probs(qm, kk if aligned else kk_r, mask, s_ref[0, hq])
                o = jnp.dot(pn.astype(BF16), vv if aligned else vv_r, preferred_element_type=F32)
                outj = jnp.where(own, o, outj)
            out_ref[:, 128 * j:128 * (j + 1)] = outj.astype(BF16)

    prev = lambda c: pl.BlockSpec((BLOCK, 128), lambda i: (jnp.maximum(i - 1, 0), c))
    cur = lambda c: pl.BlockSpec((BLOCK, 128), lambda i: (i, c))
    return pl.pallas_call(
        body, grid=(nb,),
        in_specs=[_rows(D_ATTN, 2, BLOCK), prev(12), cur(12), prev(13), cur(13),
                  pl.BlockSpec(memory_space=pltpu.SMEM)],
        out_specs=_rows(D_ATTN, 0, BLOCK), out_shape=jax.ShapeDtypeStruct((S, D_ATTN), BF16),
        compiler_params=_params(("parallel",)), name=name)(proj_a, proj_a, proj_a, proj_a, proj_a, sinks)


def _attn_bwd(proj_a, dout, sinks, name):
    S = proj_a.shape[0]
    nb = S // BLOCK

    def body(q_ref, kp_ref, kc_ref, vp_ref, vc_ref, do_ref, s_ref, out_ref, dk_ref, dv_ref, ds_ref, ck_ref, cv_ref):
        i = pl.program_id(0)

        @pl.when(i == 0)
        def _():
            ds_ref[...] = jnp.zeros_like(ds_ref)
            ck_ref[...] = jnp.zeros_like(ck_ref)
            cv_ref[...] = jnp.zeros_like(cv_ref)

        @pl.when(i < nb)
        def _():
            mask = _attn_mask(i)
            lane = lax.broadcasted_iota(jnp.int32, (BLOCK, 128), 1)
            kk = jnp.concatenate([kp_ref[...], kc_ref[...]], axis=0).astype(BF16)
            vv = jnp.concatenate([vp_ref[...], vc_ref[...]], axis=0).astype(BF16)
            kk_r, vv_r = pltpu.roll(kk, 64, axis=1), pltpu.roll(vv, 64, axis=1)
            dkk = jnp.zeros((2 * BLOCK, 128), F32)
            dvv = jnp.zeros((2 * BLOCK, 128), F32)
            for j in range(4):
                qj = (q_ref[:, 128 * j:128 * (j + 1)] * (HEAD_DIM ** -0.5)).astype(BF16)
                doj = do_ref[:, 128 * j:128 * (j + 1)].astype(BF16)
                dqj = jnp.zeros((BLOCK, 128), F32)
                for e in range(2):
                    hq = 2 * j + e
                    own = (lane >= 64) if e else (lane < 64)
                    aligned = e == hq // 4
                    ksel, vsel = (kk, vv) if aligned else (kk_r, vv_r)
                    qm = jnp.where(own, qj, jnp.zeros_like(qj))
                    dom = jnp.where(own, doj, jnp.zeros_like(doj))
                    pn, psink = _attn_probs(qm, ksel, mask, s_ref[0, hq])
                    dp = lax.dot_general(dom, vsel, (((1,), (1,)), ((), ())), preferred_element_type=F32)
                    delta = jnp.sum(pn * dp, axis=1, keepdims=True)
                    dsb = (pn * (dp - delta)).astype(BF16)
                    pnb = pn.astype(BF16)
                    ds_ref[hq:hq + 1, :] += jnp.broadcast_to(-jnp.sum(psink * delta), (1, 128))
                    dq = jnp.dot(dsb, ksel, preferred_element_type=F32) * (HEAD_DIM ** -0.5)
                    dqj = jnp.where(own, dq, dqj)
                    tn_dims = (((0,), (0,)), ((), ()))
                    dk = lax.dot_general(dsb, qm, tn_dims, preferred_element_type=F32)
                    dv = lax.dot_general(pnb, dom, tn_dims, preferred_element_type=F32)
                    dkk = dkk + (dk if aligned else pltpu.roll(dk, 64, axis=1))
                    dvv = dvv + (dv if aligned else pltpu.roll(dv, 64, axis=1))
                out_ref[:, 128 * j:128 * (j + 1)] = dqj.astype(BF16)
            ck_ref[0:BLOCK, :] = ck_ref[BLOCK:, :] + dkk[0:BLOCK]
            cv_ref[0:BLOCK, :] = cv_ref[BLOCK:, :] + dvv[0:BLOCK]
            ck_ref[BLOCK:, :] = dkk[BLOCK:]
            cv_ref[BLOCK:, :] = dvv[BLOCK:]

        @pl.when(i == nb)
        def _():
            ck_ref[0:BLOCK, :] = ck_ref[BLOCK:, :]
            cv_ref[0:BLOCK, :] = cv_ref[BLOCK:, :]

        dk_ref[...] = ck_ref[0:BLOCK, :].astype(BF16)
        dv_ref[...] = cv_ref[0:BLOCK, :].astype(BF16)

    last = nb - 1
    prev = lambda c: pl.BlockSpec((BLOCK, 128), lambda i: (jnp.clip(i - 1, 0, last), c))
    cur = lambda c: pl.BlockSpec((BLOCK, 128), lambda i: (jnp.minimum(i, last), c))
    qrow = lambda w, c: pl.BlockSpec((BLOCK, w), lambda i: (jnp.minimum(i, last), c))

    dq, dk, dv, ds = pl.pallas_call(
        body, grid=(nb + 1,),
        in_specs=[qrow(D_ATTN, 2), prev(12), cur(12), prev(13), cur(13), qrow(D_ATTN, 0),
                  pl.BlockSpec(memory_space=pltpu.SMEM)],
        out_specs=[qrow(D_ATTN, 0), prev(0), prev(0), _whole((N_Q_HEADS, 128))],
        out_shape=[jax.ShapeDtypeStruct((S, D_ATTN), BF16), jax.ShapeDtypeStruct((S, D_KV), BF16),
                   jax.ShapeDtypeStruct((S, D_KV), BF16), jax.ShapeDtypeStruct((N_Q_HEADS, 128), F32)],
        scratch_shapes=[pltpu.VMEM((2 * BLOCK, 128), F32), pltpu.VMEM((2 * BLOCK, 128), F32)],
        compiler_params=_params(("arbitrary",)), name=name)(proj_a, proj_a, proj_a, proj_a, proj_a, dout, sinks)
    return dq, dk, dv, ds


_BR = ((0, D_CONV), (D_CONV, D_CONV + D_SSM), (D_CONV + D_SSM, D_CONV + D_SSM + D_ATTN))


def _branches(m_refs, wbr_ref):
    nt = (((1,), (1,)), ((), ()))
    return [lax.dot_general(m[...], wbr_ref[:, lo:hi], nt, preferred_element_type=F32)
            for m, (lo, hi) in zip(m_refs, _BR)]


def _merge_fwd(mc, ms, ma, proj_g, wbr_t, name):
    S, Dm = mc.shape[0], proj_g.shape[1] // 3
    tm = min(256, S)

    def body(mc_ref, ms_ref, ma_ref, g_ref, w_ref, out_ref):
        ys = _branches((mc_ref, ms_ref, ma_ref), w_ref)
        acc = None
        for b in range(3):
            t = _sigmoid(g_ref[:, b * Dm:(b + 1) * Dm]) * ys[b]
            acc = t if acc is None else acc + t
        out_ref[...] = acc.astype(BF16)

    wa, ws = _op(wbr_t, (Dm, Dm), lambda i: (0, 0))
    return pl.pallas_call(
        body, grid=(S // tm,),
        in_specs=[_rows(D_CONV, 0, tm), _rows(D_SSM, 0, tm), _rows(D_ATTN, 0, tm), _rows(3 * Dm, 0, tm), ws],
        out_specs=_rows(Dm, 0, tm), out_shape=jax.ShapeDtypeStruct((S, Dm), BF16),
        compiler_params=_params(("parallel",)), name=name)(mc, ms, ma, proj_g, wa)


def _merge_bwd(dmerged, mc, ms, ma, proj_g, wbr_t, name):
    S, Dm = mc.shape[0], proj_g.shape[1] // 3
    tm = min(256, S)

    def body(d_ref, mc_ref, ms_ref, ma_ref, g_ref, w_ref, dg_ref, dy_ref, dmc_ref, dms_ref, dma_ref):
        ys = _branches((mc_ref, ms_ref, ma_ref), w_ref)
        d = d_ref[...]
        for b, (o_ref, (lo, hi)) in enumerate(zip((dmc_ref, dms_ref, dma_ref), _BR)):
            g = _sigmoid(g_ref[:, b * Dm:(b + 1) * Dm])
            dg_ref[:, b * Dm:(b + 1) * Dm] = (d * ys[b] * g * (1.0 - g)).astype(BF16)
            dyb = (g * d).astype(BF16)
            dy_ref[:, b * Dm:(b + 1) * Dm] = dyb
            o_ref[...] = jnp.dot(dyb, w_ref[:, lo:hi], preferred_element_type=F32)

    wa, ws = _op(wbr_t, (Dm, Dm), lambda i: (0, 0))
    return pl.pallas_call(
        body, grid=(S // tm,),
        in_specs=[_rows(Dm, 0, tm), _rows(D_CONV, 0, tm), _rows(D_SSM, 0, tm), _rows(D_ATTN, 0, tm),
                  _rows(3 * Dm, 0, tm), ws],
        out_specs=[_rows(3 * Dm, 0, tm), _rows(3 * Dm, 0, tm), _rows(D_CONV, 0, tm), _rows(D_SSM, 0, tm),
                   _rows(D_ATTN, 0, tm)],
        out_shape=[jax.ShapeDtypeStruct((S, 3 * Dm), BF16), jax.ShapeDtypeStruct((S, 3 * Dm), BF16),
                   jax.ShapeDtypeStruct((S, D_CONV), F32), jax.ShapeDtypeStruct((S, D_SSM), F32),
                   jax.ShapeDtypeStruct((S, D_ATTN), F32)],
        compiler_params=_params(("parallel",)), name=name)(dmerged, mc, ms, ma, proj_g, wa)


def _loss_head(y, target):
    S, Dm = y.shape
    tm = min(512, S)

    def body(y_ref, t_ref, dy_ref, l_ref):
        @pl.when(pl.program_id(0) == 0)
        def _():
            l_ref[...] = jnp.zeros_like(l_ref)

        e = y_ref[...] - t_ref[...]
        dy_ref[...] = e * (1.0 / Dm)
        l_ref[...] += jnp.broadcast_to(0.5 * jnp.sum(jnp.sum(e * e, axis=1, keepdims=True) * (1.0 / Dm)), (1, 128))

    return pl.pallas_call(
        body, grid=(S // tm,), in_specs=[_rows(Dm, 0, tm), _rows(Dm, 0, tm)],
        out_specs=[_rows(Dm, 0, tm), _whole((1, 128))],
        out_shape=[jax.ShapeDtypeStruct((S, Dm), F32), jax.ShapeDtypeStruct((1, 128), F32)],
        compiler_params=_params(("arbitrary",)), name="loss_head")(y, target)


def _view2d(shape):
    n = math.prod(shape)
    if shape[-1] % 128 == 0:
        return (n // shape[-1], shape[-1])
    if n >= (1 << 16) and len(shape) == 3:
        return (shape[0] * shape[1], shape[2])
    if n % 128 == 0:
        return (n // 128, 128)
    return (1, n)


def _adamw(w, g, m, v, name):
    shape = w.shape
    R, C = _view2d(shape)
    tm = R
    for cand in (512, 352, 256):
        if R > cand and R % cand == 0:
            tm = cand
            break
    c1 = 1.0 - ADAM_B1 ** ADAM_STEP
    c2 = 1.0 - ADAM_B2 ** ADAM_STEP

    def body(w_ref, g_ref, m_ref, v_ref, d_ref, nm_ref, nv_ref):
        g = g_ref[...]
        nm = ADAM_B1 * m_ref[...] + (1.0 - ADAM_B1) * g
        nv = ADAM_B2 * v_ref[...] + (1.0 - ADAM_B2) * (g * g)
        d_ref[...] = -ADAM_LR * ((nm / c1) / (jnp.sqrt(nv / c2) + ADAM_EPS) + ADAM_WD * w_ref[...])
        nm_ref[...] = nm
        nv_ref[...] = nv

    blk = _rows(C, 0, tm)
    outs = pl.pallas_call(
        body, grid=(R // tm,), in_specs=[blk] * 4, out_specs=[blk] * 3,
        out_shape=[jax.ShapeDtypeStruct((R, C), F32)] * 3,
        compiler_params=_params(("parallel",)), name=name)(*[t.reshape(R, C) for t in (w, g, m, v)])
    return [o.reshape(shape) for o in outs]


def _coords():
    return lax.axis_index("x"), lax.axis_index("y"), lax.axis_index("c")


def _other_chips(x, y):
    return [((1 - x, y), 2 * (1 - x) + y), ((x, 1 - y), 2 * x + 1 - y), ((1 - x, 1 - y), 2 * (1 - x) + 1 - y)]


def _exchange(build, n_in, out_shapes, n_sem, aliases, name):
    def body(*refs):
        ins, outs = refs[:n_in], refs[n_in:n_in + len(out_shapes)]
        send_sems, recv_sems, local_sems = refs[n_in + len(out_shapes):]
        sends, recvs, locals_ = build(ins, outs, send_sems, recv_sems, local_sems)
        for cp in locals_:
            cp.start()
        for cp in sends:
            cp.start()
        for cp in recvs:
            cp.wait_recv()
        for cp in sends:
            cp.wait_send()
        for cp in locals_:
            cp.wait()

    hbm = pl.BlockSpec(memory_space=pltpu.HBM)
    return pl.pallas_call(
        body, in_specs=[hbm] * n_in, out_specs=[hbm] * len(out_shapes), out_shape=out_shapes,
        scratch_shapes=[pltpu.SemaphoreType.DMA((n_sem,)), pltpu.SemaphoreType.DMA((n_sem,)),
                        pltpu.SemaphoreType.DMA((n_sem,))],
        input_output_aliases=aliases,
        compiler_params=pltpu.CompilerParams(has_side_effects=True), name=name)


def _remote(src, dst, send_sems, recv_sems, k, dev):
    return pltpu.make_async_remote_copy(src_ref=src, dst_ref=dst, send_sem=send_sems.at[k], recv_sem=recv_sems.at[k],
                                        device_id=dev, device_id_type=MESH)


def _gather_weights(shards):
    n = len(shards)
    fulls = [jax.ShapeDtypeStruct((s.shape[0], N_CHIPS * s.shape[1], s.shape[2]), s.dtype) for s in shards]

    def half(ref, chip, c, rp):
        return ref.at[:, pl.ds(pl.multiple_of(chip * rp + c * (rp // 2), 16), rp // 2), :]

    def step1(ins, outs, send_sems, recv_sems, local_sems):
        x, y, c = _coords()
        me = 2 * x + y
        sends, recvs, locals_ = [], [], []
        for a in range(n):
            rp = ins[a].shape[1]
            locals_.append(pltpu.make_async_copy(
                ins[a], outs[a].at[:, pl.ds(pl.multiple_of(me * rp, 16), rp), :], local_sems.at[a]))
            src = ins[a].at[:, pl.ds(pl.multiple_of(c * (rp // 2), 16), rp // 2), :]
            for r, ((px, py), chip) in enumerate(_other_chips(x, y)):
                k = 3 * a + r
                sends.append(_remote(src, half(outs[a], me, c, rp), send_sems, recv_sems, k, (px, py, c)))
                recvs.append(_remote(src, half(outs[a], chip, c, rp), send_sems, recv_sems, k, (px, py, c)))
        return sends, recvs, locals_

    got = _exchange(step1, n, fulls, 3 * n, {}, "gather_weights_ici")(*shards)

    def step2(ins, outs, send_sems, recv_sems, local_sems):
        x, y, c = _coords()
        sends, recvs = [], []
        for a in range(n):
            rp = ins[a].shape[1] // N_CHIPS
            for r, (_, chip) in enumerate(_other_chips(x, y)):
                k = 3 * a + r
                mine, theirs = half(outs[a], chip, c, rp), half(outs[a], chip, 1 - c, rp)
                sends.append(_remote(mine, mine, send_sems, recv_sems, k, (x, y, 1 - c)))
                recvs.append(_remote(theirs, theirs, send_sems, recv_sems, k, (x, y, 1 - c)))
        return sends, recvs, []

    return _exchange(step2, n, fulls, 3 * n, {a: a for a in range(n)}, "gather_weights_d2d")(*got)


def _pair_sum(a, b, c_idx, half_rows, out_dtype, name):
    n4, rp, W = a.shape
    tr = 32
    nblk = half_rows // tr

    def body(c_ref, a_ref, b_ref, o_ref):
        o_ref[...] = (a_ref[...] + b_ref[...]).astype(o_ref.dtype)

    return pl.pallas_call(
        body,
        grid_spec=pltpu.PrefetchScalarGridSpec(
            num_scalar_prefetch=1, grid=(nblk,),
            in_specs=[pl.BlockSpec((n4, tr, W), lambda i, c: (0, c[0] * nblk + i, 0)),
                      pl.BlockSpec((n4, tr, W), lambda i, c: (0, i, 0))],
            out_specs=pl.BlockSpec((n4, tr, W), lambda i, c: (0, i, 0))),
        out_shape=jax.ShapeDtypeStruct((n4, half_rows, W), out_dtype),
        compiler_params=_params(("parallel",)), name=name)(c_idx, a, b)


def _sum4(m, name):
    _, R, W = m.shape
    tr = 32

    def body(m_ref, o_ref):
        v = m_ref[...].astype(F32)
        o_ref[...] = ((v[0] + v[1]) + v[2]) + v[3]

    return pl.pallas_call(
        body, grid=(R // tr,), in_specs=[pl.BlockSpec((N_CHIPS, tr, W), lambda i: (0, i, 0))],
        out_specs=pl.BlockSpec((tr, W), lambda i: (i, 0)), out_shape=jax.ShapeDtypeStruct((R, W), F32),
        compiler_params=_params(("parallel",)), name=name)(m)


def _reduce_scatter(grads):
    n = len(grads)
    x, y, c = _coords()
    g4 = [g.reshape(N_CHIPS, g.shape[0] // N_CHIPS, g.shape[1]) for g in grads]
    hr = [g.shape[1] // 2 for g in g4]

    def step1(ins, outs, send_sems, recv_sems, local_sems):
        x, y, c = _coords()
        sends, recvs = [], []
        for a in range(n):
            src = ins[a].at[:, pl.ds(pl.multiple_of((1 - c) * hr[a], 8), hr[a]), :]
            sends.append(_remote(src, outs[a], send_sems, recv_sems, a, (x, y, 1 - c)))
            recvs.append(_remote(src, outs[a], send_sems, recv_sems, a, (x, y, 1 - c)))
        return sends, recvs, []

    lands = [jax.ShapeDtypeStruct((N_CHIPS, hr[a], g4[a].shape[2]), F32) for a in range(n)]
    got = _exchange(step1, n, lands, n, {}, "reduce_d2d")(*g4)
    c_idx = jnp.reshape(c, (1,)).astype(jnp.int32)
    q = [_pair_sum(g4[a], got[a], c_idx, hr[a], BF16, "reduce_pair_sum") for a in range(n)]

    def step2(ins, outs, send_sems, recv_sems, local_sems):
        x, y, c = _coords()
        me = 2 * x + y
        sends, recvs, locals_ = [], [], []
        for a in range(n):
            locals_.append(pltpu.make_async_copy(ins[a].at[me], outs[a].at[me], local_sems.at[a]))
            for r, ((px, py), chip) in enumerate(_other_chips(x, y)):
                k = 3 * a + r
                sends.append(_remote(ins[a].at[chip], outs[a].at[me], send_sems, recv_sems, k, (px, py, c)))
                recvs.append(_remote(ins[a].at[chip], outs[a].at[chip], send_sems, recv_sems, k, (px, py, c)))
        return sends, recvs, locals_

    lands = [jax.ShapeDtypeStruct(t.shape, BF16) for t in q]
    got = _exchange(step2, n, lands, 3 * n, {}, "reduce_ici")(*q)
    r = [_sum4(got[a], "reduce_sum4") for a in range(n)]

    def step3(ins, outs, send_sems, recv_sems, local_sems):
        x, y, c = _coords()
        sends, recvs, locals_ = [], [], []
        for a in range(n):
            locals_.append(pltpu.make_async_copy(ins[a], outs[a].at[c], local_sems.at[a]))
            sends.append(_remote(ins[a], outs[a].at[c], send_sems, recv_sems, a, (x, y, 1 - c)))
            recvs.append(_remote(ins[a], outs[a].at[1 - c], send_sems, recv_sems, a, (x, y, 1 - c)))
        return sends, recvs, locals_

    lands = [jax.ShapeDtypeStruct((2,) + t.shape, F32) for t in r]
    got = _exchange(step3, n, lands, n, {}, "reduce_share")(*r)
    return [t.reshape(2 * t.shape[1], t.shape[2]) for t in got]


def _all_reduce_small(buf, name):
    R = buf.shape[0]

    def step(ins, outs, send_sems, recv_sems, local_sems):
        x, y, c = _coords()
        me = 4 * x + 2 * y + c
        sends, recvs = [], []
        k = 0
        for fx in range(2):
            for fy in range(2):
                for fc in range(2):
                    if fx + fy + fc == 0:
                        continue
                    px, py, pc = x ^ fx, y ^ fy, c ^ fc
                    sends.append(_remote(ins[0], outs[0].at[me], send_sems, recv_sems, k, (px, py, pc)))
                    recvs.append(_remote(ins[0], outs[0].at[4 * px + 2 * py + pc], send_sems, recv_sems, k,
                                         (px, py, pc)))
                    k += 1
        return sends, recvs, [pltpu.make_async_copy(ins[0], outs[0].at[me], local_sems.at[0])]

    got = _exchange(step, 1, [jax.ShapeDtypeStruct((N_DEV, R, 128), F32)], N_DEV - 1, {}, name + "_gather")(buf)[0]

    def body(m_ref, o_ref):
        acc = m_ref[0]
        for d in range(1, N_DEV):
            acc = acc + m_ref[d]
        o_ref[...] = acc

    tr = R
    for cand in (512, 256, 128, 64, 32, 16, 8):
        if R % cand == 0:
            tr = cand
            break
    return pl.pallas_call(
        body, grid=(R // tr,), in_specs=[pl.BlockSpec((N_DEV, tr, 128), lambda i: (0, i, 0))],
        out_specs=pl.BlockSpec((tr, 128), lambda i: (i, 0)), out_shape=jax.ShapeDtypeStruct((R, 128), F32),
        compiler_params=_params(("parallel",)), name=name + "_sum")(got)


def _ssm_layouts(p, L):
    G = L * N_GROUPS
    lr = p["ssm_a_re"].reshape(G, 1, SSM_STATE)
    li = p["ssm_a_im"].reshape(G, 1, SSM_STATE)
    ldt = p["ssm_log_dt"].reshape(G, 1, 1)
    br = jnp.swapaxes(p["ssm_b_re"], 2, 3).reshape(G, SSM_GROUP, SSM_STATE)
    bi = jnp.swapaxes(p["ssm_b_im"], 2, 3).reshape(G, SSM_GROUP, SSM_STATE)
    are, aim, bre, bim = _disc_fwd(lr, li, ldt, br, bi)
    eye = jnp.eye(N_GROUPS, dtype=F32)
    abar = jnp.concatenate([are.reshape(L, 8, 128), aim.reshape(L, 8, 128)], axis=1)

    def b_blk(t):
        return jnp.einsum("lgcp,gh->lgchp", t.reshape(L, N_GROUPS, SSM_GROUP, SSM_STATE), eye).reshape(L, 256, 1024)

    def c_blk(t):
        return jnp.einsum("lgcp,gh->lgphc", t, eye).reshape(L, 1024, 256)

    bfull = jnp.concatenate([b_blk(bre), b_blk(bim)], axis=2).astype(BF16)
    cfull = jnp.concatenate([c_blk(p["ssm_c_re"]), -c_blk(p["ssm_c_im"])], axis=1).astype(BF16)
    return (lr, li, ldt, br, bi), abar, bfull, cfull


def _local_step(x, target, W, p, L):
    S, Dm = x.shape
    dff = W["wg1"].shape[1]
    alpha = (2.0 * L) ** 0.25
    disc_in, abar, bfull, cfull = _ssm_layouts(p, L)
    row = lambda t, l: t[l][None]
    saved = []
    h = x
    for l in range(L):
        sv = {"x0": h}
        a1, b1, h1 = _ffn_up(h, (W["wg1"], l), (W["wu1"], l), dff, "ffn_up")
        f1 = _mm([(h1, (W["wd1"], l))], "nn", S, Dm, dff, [F32], name="ffn_down", tm=1024, tn=1024, tk=256)
        x1, xh1, rs1 = _ln_fwd(h, f1, row(p["ln1_g"], l), row(p["ln1_b"], l), 0.5, alpha, "ln_fwd")
        sv.update(a1=a1, b1=b1, h1=h1, x1=x1, xh1=xh1, rs1=rs1)
        proj_a = _mm([(x1, (W["win"], l))], "nt", S, D_A, Dm, [F32], name="proj_a", tm=1024, tn=256, tk=1024)
        proj_g = _mm([(x1, (W["win"], l))], "nt", S, 3 * Dm, Dm, [F32], name="proj_g", tm=1024, tn=256, tk=1024,
                     b_off=D_A)
        mc = _conv_fwd(proj_a, p["conv_w"][l], row(p["conv_b"], l), "conv_fwd")
        bu = _mm([(proj_a, (bfull, l))], "nn", S, 2048, D_SSM, [F32], name="ssm_bu", tm=1024, tn=1024, tk=256,
                 a_off=3 * D_CONV)
        xs = _scan_fwd(bu.reshape(S, 16, 128), abar[l], "scan_fwd").reshape(S, 2048)
        y1, ms = _ssm_out(xs, proj_a, cfull[l], row(p["ssm_d"], l), (W["wglu"], l), "ssm_out")
        sinks = p["attn_sinks"][l][None]
        ma = _attn_fwd(proj_a, sinks, "attn_fwd")
        merged = _merge_fwd(mc, ms, ma, proj_g, (W["wbr"], l), "merge_fwd")
        mix = _mm([(merged, (W["wout"], l))], "nn", S, Dm, Dm, [F32], name="mix_out", tm=1024, tn=1024, tk=512)
        x2, xh2, rs2 = _ln_fwd(x1, mix, row(p["ln2_g"], l), row(p["ln2_b"], l), 1.0, alpha, "ln_fwd")
        sv.update(proj_a=proj_a, proj_g=proj_g, mc=mc, ms=ms, ma=ma, xs=xs, y1=y1, merged=merged, x2=x2, xh2=xh2,
                  rs2=rs2)
        a2, b2, h2 = _ffn_up(x2, (W["wg2"], l), (W["wu2"], l), dff, "ffn_up")
        f2 = _mm([(h2, (W["wd2"], l))], "nn", S, Dm, dff, [F32], name="ffn_down", tm=1024, tn=1024, tk=256)
        x3, xh3, rs3 = _ln_fwd(x2, f2, row(p["ln3_g"], l), row(p["ln3_b"], l), 0.5, alpha, "ln_fwd")
        sv.update(a2=a2, b2=b2, h2=h2, xh3=xh3, rs3=rs3)
        saved.append(sv)
        h = x3

    dy, loss = _loss_head(h, target)
    big = [None] * L
    small = {k: [None] * L for k in ("ln1_g", "ln1_b", "ln2_g", "ln2_b", "ln3_g", "ln3_b", "conv_w", "conv_b", "ssm_d",
                                     "attn_sinks", "wglu", "dabar", "dbfull", "dcfull")}

    def ffn_bwd(dy_out, x_in, a, b, hh, xh, rs, g, wg, wu, wd, l):
        dres, df, dg, db = _ln_bwd(dy_out, xh, rs, g, 0.5, alpha, "ln_bwd")
        da, dbb = _ffn_dh(df, (wd, l), a, b, "ffn_dh")
        dx = _mm([(da, (wg, l)), (dbb, (wu, l))], "nn", S, Dm, dff, [F32], name="ffn_dx", tm=1024, tn=1024, tk=256,
                 add=dres)
        tn_kw = dict(tm=1408, tn=1024, tk=512)
        dwg = _mm([(da, x_in)], "tn", dff, Dm, S, [F32], name="ffn_dw_up", **tn_kw)
        dwu = _mm([(dbb, x_in)], "tn", dff, Dm, S, [F32], name="ffn_dw_up", **tn_kw)
        dwd = _mm([(hh, df)], "tn", dff, Dm, S, [F32], name="ffn_dw_down", **tn_kw)
        return dx, dwg, dwu, dwd, dg, db

    for l in reversed(range(L)):
        sv = saved[l]
        dx2, dwg2, dwu2, dwd2, small["ln3_g"][l], small["ln3_b"][l] = ffn_bwd(
            dy, sv["x2"], sv["a2"], sv["b2"], sv["h2"], sv["xh3"], sv["rs3"], row(p["ln3_g"], l), W["wg2"], W["wu2"],
            W["wd2"], l)
        dres2, dmix, small["ln2_g"][l], small["ln2_b"][l] = _ln_bwd(dx2, sv["xh2"], sv["rs2"], row(p["ln2_g"], l), 1.0,
                                                                   alpha, "ln_bwd")
        dwout = _mm([(sv["merged"], dmix)], "tn", Dm, Dm, S, [F32], name="dw_out", tm=512, tn=1024, tk=512)
        dmerged = _mm([(dmix, (W["wout"], l))], "nt", S, Dm, Dm, [F32], name="d_merged", tm=1024, tn=1024, tk=512)
        dgates, dyb, dmc, dms, dma = _merge_bwd(dmerged, sv["mc"], sv["ms"], sv["ma"], sv["proj_g"], (W["wbr"], l),
                                                "merge_bwd")
        dwbr = jnp.concatenate([
            _mm([(dyb, m)], "tn", Dm, hi - lo, S, [F32], name="dw_br", tm=512, tn=512, tk=512, a_off=b * Dm)
            for b, ((lo, hi), m) in enumerate(zip(_BR, (sv["mc"], sv["ms"], sv["ma"])))], axis=1)
        proj_a = sv["proj_a"]
        d_conv, small["conv_w"][l], small["conv_b"][l] = _conv_bwd(proj_a, dmc, p["conv_w"][l], row(p["conv_b"], l),
                                                                  "conv_bwd")
        dy1, y2, dgl, dxs, du_skip, small["ssm_d"][l] = _ssm_out_bwd(dms, sv["y1"], proj_a, cfull[l],
                                                                    row(p["ssm_d"], l), (W["wglu"], l), "ssm_out_bwd")
        small["wglu"][l] = _mm([(y2, dgl)], "tn", D_SSM, D_SSM, S, [F32], name="dw_glu", tk=512)
        small["dcfull"][l] = _mm([(sv["xs"], dy1)], "tn", 2048, D_SSM, S, [F32], name="d_cfull", tm=1024, tk=512)
        lam, small["dabar"][l] = _scan_bwd(dxs.reshape(S, 16, 128), sv["xs"].reshape(S, 16, 128), abar[l], "scan_bwd")
        lam = lam.reshape(S, 2048)
        du = _mm([(lam, (bfull, l))], "nt", S, D_SSM, 2048, [BF16], name="ssm_du", tm=1024, tk=512, add=du_skip)
        small["dbfull"][l] = _mm([(proj_a, lam)], "tn", D_SSM, 2048, S, [F32], name="d_bfull", tm=256, tn=1024,
                                 tk=512, a_off=3 * D_CONV)
        dq, dk, dv, dsk = _attn_bwd(proj_a, dma, p["attn_sinks"][l][None], "attn_bwd")
        small["attn_sinks"][l] = dsk[:, 0]
        dproj = jnp.concatenate([d_conv, du, dq, dk, dv, dgates], axis=1)
        dx1 = _mm([(dproj, (W["win"], l))], "nn", S, Dm, D_A + 3 * Dm, [F32], name="d_x1", tm=1024, tn=1024, tk=256,
                  add=dres2)
        dwin = _mm([(dproj, sv["x1"])], "tn", D_A + 3 * Dm, Dm, S, [F32], name="dw_in", tm=2432, tn=1024, tk=512)
        dx0, dwg1, dwu1, dwd1, small["ln1_g"][l], small["ln1_b"][l] = ffn_bwd(
            dx1, sv["x0"], sv["a1"], sv["b1"], sv["h1"], sv["xh1"], sv["rs1"], row(p["ln1_g"], l), W["wg1"], W["wu1"],
            W["wd1"], l)
        big[l] = dict(wg1=dwg1, wu1=dwu1, wd1=dwd1, win=dwin, wbr=dwbr, wout=dwout, wg2=dwg2, wu2=dwu2, wd2=dwd2)
        dy = dx0

    small = {k: jnp.stack(v) for k, v in small.items()}
    eye = jnp.eye(N_GROUPS, dtype=F32)
    dabar = small.pop("dabar")
    dbf = small.pop("dbfull").reshape(L, N_GROUPS, SSM_GROUP, 2, N_GROUPS, SSM_STATE)
    dbbar = jnp.einsum("lgcrhp,gh->rlgcp", dbf, eye).reshape(2, L * N_GROUPS, SSM_GROUP, SSM_STATE)
    dcf = small.pop("dcfull").reshape(L, 2, N_GROUPS, SSM_STATE, N_GROUPS, SSM_GROUP)
    dc = jnp.einsum("lrgphc,gh->rlgcp", dcf, eye)
    G = L * N_GROUPS
    cts = (dabar[:, 0:8].reshape(G, 1, SSM_STATE), dabar[:, 8:16].reshape(G, 1, SSM_STATE), dbbar[0], dbbar[1])
    dlr, dli, dldt, dbr, dbi = _disc_bwd(*disc_in, cts)
    shp_b = (L, N_GROUPS, SSM_GROUP, SSM_STATE)
    small.update(
        ssm_a_re=dlr.reshape(L, N_GROUPS, SSM_STATE), ssm_a_im=dli.reshape(L, N_GROUPS, SSM_STATE),
        ssm_log_dt=dldt.reshape(L, N_GROUPS), ssm_b_re=jnp.swapaxes(dbr.reshape(shp_b), 2, 3),
        ssm_b_im=jnp.swapaxes(dbi.reshape(shp_b), 2, 3), ssm_c_re=dc[0], ssm_c_im=-dc[1],
        ln1_g=small["ln1_g"][:, 0], ln1_b=small["ln1_b"][:, 0], ln2_g=small["ln2_g"][:, 0],
        ln2_b=small["ln2_b"][:, 0], ln3_g=small["ln3_g"][:, 0], ln3_b=small["ln3_b"][:, 0],
        conv_b=small["conv_b"][:, 0], ssm_d=small["ssm_d"][:, 0])
    return loss, dy, big, small


_SMALL_ORDER = ("ln1_g", "ln1_b", "ln2_g", "ln2_b", "ln3_g", "ln3_b", "conv_w", "conv_b", "ssm_a_re", "ssm_a_im",
                "ssm_log_dt", "ssm_b_re", "ssm_b_im", "ssm_c_re", "ssm_c_im", "ssm_d", "attn_sinks", "wglu")
_BIG_ORDER = ("wg1", "wu1", "wd1", "win", "wbr", "wout", "wg2", "wu2", "wd2")
_WEIGHTS = ("ffn1_w_gate", "ffn1_w_up", "ffn1_w_down", "ln1_g", "ln1_b", "w_in", "conv_w", "conv_b", "ssm_a_re",
            "ssm_a_im", "ssm_log_dt", "ssm_b_re", "ssm_b_im", "ssm_c_re", "ssm_c_im", "ssm_d", "ssm_w_glu",
            "attn_sinks", "w_br_conv", "w_br_ssm", "w_br_attn", "w_out", "ln2_g", "ln2_b", "ffn2_w_gate",
            "ffn2_w_up", "ffn2_w_down", "ln3_g", "ln3_b")


def _weight_shards(w):
    t = lambda a: jnp.swapaxes(a, 1, 2).astype(BF16)
    wbr = jnp.concatenate([t(w["w_br_conv"]), t(w["w_br_ssm"]), t(w["w_br_attn"])], axis=2)
    return dict(wg1=t(w["ffn1_w_gate"]), wu1=t(w["ffn1_w_up"]), wd1=w["ffn1_w_down"].astype(BF16), win=t(w["w_in"]),
                wbr=wbr, wout=w["w_out"].astype(BF16), wg2=t(w["ffn2_w_gate"]), wu2=t(w["ffn2_w_up"]),
                wd2=w["ffn2_w_down"].astype(BF16))


def _gather_small_shards(conv_w, w_glu, chip):
    L = conv_w.shape[0]
    part = jnp.concatenate([conv_w.reshape(1, -1), w_glu.reshape(1, -1)], axis=1)
    n = part.shape[1]
    slots = lax.dynamic_update_slice(jnp.zeros((N_CHIPS, n), F32), part, (chip, 0))
    got = (_all_reduce_small(slots.reshape(-1, 128), "param_gather") * 0.5).reshape(N_CHIPS, n)
    cw = jnp.transpose(got[:, :L * 3 * 64].reshape(N_CHIPS, L, 3, 64), (1, 2, 0, 3)).reshape(L, 3, N_CHIPS * 64)
    wg = jnp.transpose(got[:, L * 3 * 64:].reshape(N_CHIPS, L, 64, D_SSM), (1, 0, 2, 3)).reshape(L, N_CHIPS * 64, D_SSM)
    return cw, wg.astype(BF16)


def kernel(x, ffn1_w_gate, ffn1_w_up, ffn1_w_down, ln1_g, ln1_b, w_in, conv_w, conv_b, ssm_a_re, ssm_a_im, ssm_log_dt, ssm_b_re, ssm_b_im, ssm_c_re, ssm_c_im, ssm_d, ssm_w_glu, attn_sinks, w_br_conv, w_br_ssm, w_br_attn, w_out, ln2_g, ln2_b, ffn2_w_gate, ffn2_w_up, ffn2_w_down, ln3_g, ln3_b, loss_target, m_ffn1_w_gate, m_ffn1_w_up, m_ffn1_w_down, m_ln1_g, m_ln1_b, m_w_in, m_conv_w, m_conv_b, m_ssm_a_re, m_ssm_a_im, m_ssm_log_dt, m_ssm_b_re, m_ssm_b_im, m_ssm_c_re, m_ssm_c_im, m_ssm_d, m_ssm_w_glu, m_attn_sinks, m_w_br_conv, m_w_br_ssm, m_w_br_attn, m_w_out, m_ln2_g, m_ln2_b, m_ffn2_w_gate, m_ffn2_w_up, m_ffn2_w_down, m_ln3_g, m_ln3_b, v_ffn1_w_gate, v_ffn1_w_up, v_ffn1_w_down, v_ln1_g, v_ln1_b, v_w_in, v_conv_w, v_conv_b, v_ssm_a_re, v_ssm_a_im, v_ssm_log_dt, v_ssm_b_re, v_ssm_b_im, v_ssm_c_re, v_ssm_c_im, v_ssm_d, v_ssm_w_glu, v_attn_sinks, v_w_br_conv, v_w_br_ssm, v_w_br_attn, v_w_out, v_ln2_g, v_ln2_b, v_ffn2_w_gate, v_ffn2_w_up, v_ffn2_w_down, v_ln3_g, v_ln3_b):
    args = dict(locals())
    w = {k: args[k] for k in _WEIGHTS}
    L = ln1_g.shape[0]
    cx, cy, cc = _coords()
    chip = 2 * cx + cy

    shards = _weight_shards(w)
    names = list(shards)
    full = dict(zip(names, _gather_weights([shards[k] for k in names])))
    p = {k: w[k] for k in ("ln1_g", "ln1_b", "ln2_g", "ln2_b", "ln3_g", "ln3_b", "conv_b", "ssm_a_re", "ssm_a_im",
                           "ssm_log_dt", "ssm_b_re", "ssm_b_im", "ssm_c_re", "ssm_c_im", "ssm_d", "attn_sinks")}
    p["conv_w"], full["wglu"] = _gather_small_shards(conv_w, ssm_w_glu, chip)

    loss, grad_x, big, small = _local_step(x[0], loss_target[0], full, p, L)
    loss = lax.psum(loss[0, 0], ("x", "y", "c"))

    sizes = [math.prod(small[k].shape) for k in _SMALL_ORDER]
    pad = (-sum(sizes)) % 1024
    flat = jnp.concatenate([small[k].reshape(-1) for k in _SMALL_ORDER] + [jnp.zeros((pad,), F32)])
    flat = _all_reduce_small(flat.reshape(-1, 128), "small_grads").reshape(-1)
    sm, off = {}, 0
    for k, n in zip(_SMALL_ORDER, sizes):
        sm[k] = flat[off:off + n].reshape(small[k].shape)
        off += n
    red = _reduce_scatter([big[l][k] for l in range(L) for k in _BIG_ORDER])
    red = {k: jnp.stack([red[l * len(_BIG_ORDER) + i] for l in range(L)]) for i, k in enumerate(_BIG_ORDER)}
    tr = lambda a: jnp.swapaxes(a, 1, 2)
    grads = dict(sm)
    grads.update(
        ffn1_w_gate=tr(red["wg1"]), ffn1_w_up=tr(red["wu1"]), ffn1_w_down=red["wd1"], w_in=tr(red["win"]),
        w_br_conv=tr(red["wbr"][:, :, _BR[0][0]:_BR[0][1]]), w_br_ssm=tr(red["wbr"][:, :, _BR[1][0]:_BR[1][1]]),
        w_br_attn=tr(red["wbr"][:, :, _BR[2][0]:_BR[2][1]]), w_out=red["wout"], ffn2_w_gate=tr(red["wg2"]),
        ffn2_w_up=tr(red["wu2"]), ffn2_w_down=red["wd2"],
        ssm_w_glu=lax.dynamic_slice_in_dim(sm["wglu"], chip * 64, 64, axis=1),
        conv_w=lax.dynamic_slice_in_dim(sm["conv_w"], chip * 64, 64, axis=2))

    outs = [[], [], [], []]
    for k in _WEIGHTS:
        d, nm, nv = _adamw(w[k], grads[k], args["m_" + k], args["v_" + k], "adamw")
        for lst, val in zip(outs, (grads[k], d, nm, nv)):
            lst.append(val)
    return (loss, grad_x[None], *outs[0], *outs[1], *outs[2], *outs[3])
```

```python
import functools
import math

import jax
import jax.numpy as jnp
from jax import lax
from jax.experimental import pallas as pl
from jax.experimental.pallas import tpu as pltpu

F32 = jnp.float32
BF16 = jnp.bfloat16

LN_EPS = 1e-5
D_CONV = 256
D_SSM = 256
N_GROUPS = 16
SSM_GROUP = 16
SSM_STATE = 64
N_Q_HEADS = 8
HEAD_DIM = 64
D_ATTN = 512
D_KV = 128
BLOCK = 128
D_A = 3 * D_CONV + D_SSM + D_ATTN + 2 * D_KV
ADAM_LR = 0.001
ADAM_B1 = 0.9
ADAM_B2 = 0.999
ADAM_EPS = 1e-08
ADAM_WD = 0.01
ADAM_STEP = 10

VMEM_LIMIT_BYTES = 56 * 1024 * 1024
MESH = pl.DeviceIdType.MESH
N_CHIPS = 4
N_DEV = 8


def _params(sem=None):
    return pltpu.CompilerParams(dimension_semantics=sem, vmem_limit_bytes=VMEM_LIMIT_BYTES)


def _op(op, block, imap):
    if isinstance(op, tuple):
        arr, l = op
        return arr, pl.BlockSpec((None,) + block, lambda *g: (l,) + imap(*g))
    return op, pl.BlockSpec(block, imap)


def _mm(pairs, mode, m, n, k, out_dtypes, *, name, tm=512, tn=512, tk=512, add=None, a_off=0, b_off=0):
    tm, tn, tk = min(tm, m), min(tn, n), min(tk, k)
    assert m % tm == 0 and n % tn == 0 and k % tk == 0 and b_off % tn == 0, (name, m, n, k, tm, tn, tk)
    nk, npair, jo = k // tk, len(pairs), b_off // tn
    if mode == "nn":
        ao = a_off // tk
        ab, ai, bb, bi = (tm, tk), (lambda i, j, kk: (i, kk + ao)), (tk, tn), (lambda i, j, kk: (kk, j))
        dims = (((1,), (0,)), ((), ()))
    elif mode == "nt":
        ao = a_off // tk
        ab, ai, bb, bi = (tm, tk), (lambda i, j, kk: (i, kk + ao)), (tn, tk), (lambda i, j, kk: (j + jo, kk))
        dims = (((1,), (1,)), ((), ()))
    else:
        ao = a_off // tm
        ab, ai, bb, bi = (tk, tm), (lambda i, j, kk: (kk, i + ao)), (tk, tn), (lambda i, j, kk: (kk, j))
        dims = (((0,), (0,)), ((), ()))
    assert a_off % (tm if mode == "tn" else tk) == 0, (name, a_off)
    ops, specs = [], []
    for a, b in pairs:
        for o, blk, im in ((a, ab, ai), (b, bb, bi)):
            arr, sp = _op(o, blk, im)
            ops.append(arr)
            specs.append(sp)
    has_add = add is not None
    if has_add:
        ops.append(add)
        specs.append(pl.BlockSpec((tm, tn), lambda i, j, kk: (i, j)))
    nout = len(out_dtypes)

    def body(*refs):
        outs, acc = refs[2 * npair + has_add:2 * npair + has_add + nout], refs[-1]
        kk = pl.program_id(2)

        @pl.when(kk == 0)
        def _():
            acc[...] = jnp.zeros_like(acc)

        t = None
        for p in range(npair):
            d = lax.dot_general(refs[2 * p][...].astype(BF16), refs[2 * p + 1][...].astype(BF16), dims,
                                preferred_element_type=F32)
            t = d if t is None else t + d
        acc[...] += t

        @pl.when(kk == nk - 1)
        def _():
            r = acc[...]
            if has_add:
                r = r + refs[2 * npair][...]
            for o in outs:
                o[...] = r.astype(o.dtype)

    res = pl.pallas_call(
        body, grid=(m // tm, n // tn, nk), in_specs=specs,
        out_specs=[pl.BlockSpec((tm, tn), lambda i, j, kk: (i, j))] * nout,
        out_shape=[jax.ShapeDtypeStruct((m, n), dt) for dt in out_dtypes],
        scratch_shapes=[pltpu.VMEM((tm, tn), F32)],
        compiler_params=_params(("parallel", "parallel", "arbitrary")), name=name)(*ops)
    return res[0] if nout == 1 else res


def _rows(width, col=0, tm=None):
    return pl.BlockSpec((tm, width), lambda i: (i, col))


def _whole(shape):
    nd = len(shape)
    return pl.BlockSpec(shape, lambda i: (0,) * nd)


def _sigmoid(x):
    return 1.0 / (1.0 + jnp.exp(-x))


def _mm_ln(a, w, x, g, b, s, alpha, name, tk):
    S, K = a.shape
    Dm = x.shape[1]
    tm, tk = min(512, S), min(tk, K)
    nk = K // tk

    def body(a_ref, w_ref, x_ref, g_ref, b_ref, y_ref, yb_ref, xh_ref, rs_ref, acc):
        kk = pl.program_id(1)

        @pl.when(kk == 0)
        def _():
            acc[...] = jnp.zeros_like(acc)

        acc[...] += jnp.dot(a_ref[...], w_ref[...], preferred_element_type=F32)

        @pl.when(kk == nk - 1)
        def _():
            z = alpha * x_ref[...] + s * acc[...]
            mu = jnp.mean(z, axis=-1, keepdims=True)
            zc = z - mu
            var = jnp.mean(zc * zc, axis=-1, keepdims=True)
            rstd = lax.rsqrt(var + LN_EPS)
            xh = zc * rstd
            y = xh * g_ref[...] + b_ref[...]
            y_ref[...] = y
            yb_ref[...] = y.astype(BF16)
            xh_ref[...] = xh
            rs_ref[...] = rstd

    wa, ws = _op(w, (tk, Dm), lambda i, kk: (kk, 0))
    row = pl.BlockSpec((tm, Dm), lambda i, kk: (i, 0))
    vec = pl.BlockSpec((1, Dm), lambda i, kk: (0, 0))
    return pl.pallas_call(
        body, grid=(S // tm, nk),
        in_specs=[pl.BlockSpec((tm, tk), lambda i, kk: (i, kk)), ws, row, vec, vec],
        out_specs=[row, row, row, pl.BlockSpec((tm, 1), lambda i, kk: (i, 0))],
        out_shape=[jax.ShapeDtypeStruct((S, Dm), F32), jax.ShapeDtypeStruct((S, Dm), BF16),
                   jax.ShapeDtypeStruct((S, Dm), F32), jax.ShapeDtypeStruct((S, 1), F32)],
        scratch_shapes=[pltpu.VMEM((tm, Dm), F32)],
        compiler_params=_params(("parallel", "arbitrary")), name=name)(a, wa, x, g, b)


def _ln_bwd(dy, xh, rs, g, s, alpha, name):
    S, Dm = dy.shape
    tm = min(256, S)

    def body(dy_ref, xh_ref, rs_ref, g_ref, dres_ref, dbr_ref, dg_ref, db_ref):
        @pl.when(pl.program_id(0) == 0)
        def _():
            dg_ref[...] = jnp.zeros_like(dg_ref)
            db_ref[...] = jnp.zeros_like(db_ref)

        dy, xh = dy_ref[...], xh_ref[...]
        dyg = dy * g_ref[...]
        m1 = jnp.mean(dyg, axis=-1, keepdims=True)
        m2 = jnp.mean(dyg * xh, axis=-1, keepdims=True)
        dz = rs_ref[...] * (dyg - m1 - xh * m2)
        dres_ref[...] = alpha * dz
        dbr_ref[...] = (s * dz).astype(BF16)
        dg_ref[...] += jnp.sum(dy * xh, axis=0, keepdims=True)
        db_ref[...] += jnp.sum(dy, axis=0, keepdims=True)

    return pl.pallas_call(
        body, grid=(S // tm,),
        in_specs=[_rows(Dm, tm=tm), _rows(Dm, tm=tm), _rows(1, tm=tm), _whole((1, Dm))],
        out_specs=[_rows(Dm, tm=tm), _rows(Dm, tm=tm), _whole((1, Dm)), _whole((1, Dm))],
        out_shape=[jax.ShapeDtypeStruct((S, Dm), F32), jax.ShapeDtypeStruct((S, Dm), BF16),
                   jax.ShapeDtypeStruct((1, Dm), F32), jax.ShapeDtypeStruct((1, Dm), F32)],
        compiler_params=_params(("arbitrary",)), name=name)(dy, xh, rs, g)


def _ffn_up(x, wg, wu, dff, name):
    S, Dm = x.shape
    tm, tn = min(1024, S), 256

    def body(x_ref, wg_ref, wu_ref, a_ref, b_ref, h_ref):
        xb = x_ref[...]
        dims = (((1,), (1,)), ((), ()))
        a = lax.dot_general(xb, wg_ref[...], dims, preferred_element_type=F32)
        b = lax.dot_general(xb, wu_ref[...], dims, preferred_element_type=F32)
        a_ref[...] = a.astype(BF16)
        b_ref[...] = b.astype(BF16)
        h_ref[...] = (a * _sigmoid(a) * b).astype(BF16)

    wga, wgs = _op(wg, (tn, Dm), lambda i, j: (j, 0))
    wua, wus = _op(wu, (tn, Dm), lambda i, j: (j, 0))
    ob = pl.BlockSpec((tm, tn), lambda i, j: (i, j))
    return pl.pallas_call(
        body, grid=(S // tm, dff // tn),
        in_specs=[pl.BlockSpec((tm, Dm), lambda i, j: (i, 0)), wgs, wus], out_specs=[ob, ob, ob],
        out_shape=[jax.ShapeDtypeStruct((S, dff), BF16)] * 3,
        compiler_params=_params(("parallel", "parallel")), name=name)(x, wga, wua)


def _ffn_dh(df, wd, a, b, name):
    S, Dm = df.shape
    dff = a.shape[1]
    tm, tn = min(1024, S), 256

    def body(df_ref, wd_ref, a_ref, b_ref, da_ref, db_ref):
        dh = lax.dot_general(df_ref[...], wd_ref[...], (((1,), (1,)), ((), ())), preferred_element_type=F32)
        a, b = a_ref[...].astype(F32), b_ref[...].astype(F32)
        sg = _sigmoid(a)
        da_ref[...] = (dh * b * (sg * (1.0 + a * (1.0 - sg)))).astype(BF16)
        db_ref[...] = (dh * (a * sg)).astype(BF16)

    wda, wds = _op(wd, (tn, Dm), lambda i, j: (j, 0))
    ob = pl.BlockSpec((tm, tn), lambda i, j: (i, j))
    return pl.pallas_call(
        body, grid=(S // tm, dff // tn),
        in_specs=[pl.BlockSpec((tm, Dm), lambda i, j: (i, 0)), wds, ob, ob], out_specs=[ob, ob],
        out_shape=[jax.ShapeDtypeStruct((S, dff), BF16), jax.ShapeDtypeStruct((S, dff), BF16)],
        compiler_params=_params(("parallel", "parallel")), name=name)(df, wda, a, b)


def _halo_prev(width, col, tm):
    return pl.BlockSpec((8, width), lambda i: (jnp.maximum(i * (tm // 8) - 1, 0), col))


def _halo_next(width, col, tm, S):
    return pl.BlockSpec((8, width), lambda i: (jnp.minimum((i + 1) * (tm // 8), S // 8 - 1), col))


def _shift_down(prev8, cur, n):
    ext = jnp.concatenate([prev8, cur], axis=0)
    return pltpu.roll(ext, n, axis=0)[8:]


def _shift_up(cur, next8, n):
    ext = jnp.concatenate([cur, next8], axis=0)
    return pltpu.roll(ext, ext.shape[0] - n, axis=0)[:cur.shape[0]]


def _conv_fwd(proj_a, conv_w, conv_b, name):
    S = proj_a.shape[0]
    tm = min(512, S)
    C = D_CONV

    def body(bg_ref, cg_ref, h_ref, cgp_ref, hp_ref, w_ref, cb_ref, out_ref):
        z = cg_ref[...] * h_ref[...]
        zp = jnp.where(pl.program_id(0) > 0, cgp_ref[...] * hp_ref[...], 0.0)
        w = w_ref[...]
        y = w[2:3] * z + w[1:2] * _shift_down(zp, z, 1) + w[0:1] * _shift_down(zp, z, 2) + cb_ref[...]
        out_ref[...] = (bg_ref[...] * y).astype(BF16)

    return pl.pallas_call(
        body, grid=(S // tm,),
        in_specs=[_rows(C, 0, tm), _rows(C, 1, tm), _rows(C, 2, tm), _halo_prev(C, 1, tm), _halo_prev(C, 2, tm),
                  _whole((3, C)), _whole((1, C))],
        out_specs=_rows(C, 0, tm), out_shape=jax.ShapeDtypeStruct((S, C), BF16),
        compiler_params=_params(("parallel",)), name=name)(proj_a, proj_a, proj_a, proj_a, proj_a, conv_w, conv_b)


def _conv_bwd(proj_a, dmc, conv_w, conv_b, name):
    S = proj_a.shape[0]
    tm = min(512, S)
    C = D_CONV
    nblk = S // tm

    def body(bg_ref, cg_ref, h_ref, cgp_ref, hp_ref, bgn_ref, d_ref, dn_ref, w_ref, cb_ref, out_ref, dw_ref,
             dcb_ref):
        i = pl.program_id(0)

        @pl.when(i == 0)
        def _():
            dw_ref[...] = jnp.zeros_like(dw_ref)
            dcb_ref[...] = jnp.zeros_like(dcb_ref)

        bg, cg, h, d = bg_ref[...], cg_ref[...], h_ref[...], d_ref[...]
        z = cg * h
        zp = jnp.where(i > 0, cgp_ref[...] * hp_ref[...], 0.0)
        w = w_ref[...]
        z1, z2 = _shift_down(zp, z, 1), _shift_down(zp, z, 2)
        y = w[2:3] * z + w[1:2] * z1 + w[0:1] * z2 + cb_ref[...]
        dy = d * bg
        dyn = jnp.where(i < nblk - 1, dn_ref[...] * bgn_ref[...], 0.0)
        dz = w[2:3] * dy + w[1:2] * _shift_up(dy, dyn, 1) + w[0:1] * _shift_up(dy, dyn, 2)
        out_ref[:, 0:C] = (d * y).astype(BF16)
        out_ref[:, C:2 * C] = (dz * h).astype(BF16)
        out_ref[:, 2 * C:3 * C] = (dz * cg).astype(BF16)
        dw_ref[0:1, :] += jnp.sum(dy * z2, axis=0, keepdims=True)
        dw_ref[1:2, :] += jnp.sum(dy * z1, axis=0, keepdims=True)
        dw_ref[2:3, :] += jnp.sum(dy * z, axis=0, keepdims=True)
        dcb_ref[...] += jnp.sum(dy, axis=0, keepdims=True)

    return pl.pallas_call(
        body, grid=(nblk,),
        in_specs=[_rows(C, 0, tm), _rows(C, 1, tm), _rows(C, 2, tm), _halo_prev(C, 1, tm), _halo_prev(C, 2, tm),
                  _halo_next(C, 0, tm, S), _rows(C, 0, tm), _halo_next(C, 0, tm, S), _whole((3, C)), _whole((1, C))],
        out_specs=[_rows(3 * C, 0, tm), _whole((3, C)), _whole((1, C))],
        out_shape=[jax.ShapeDtypeStruct((S, 3 * C), BF16), jax.ShapeDtypeStruct((3, C), F32),
                   jax.ShapeDtypeStruct((1, C), F32)],
        compiler_params=_params(("arbitrary",)), name=name)(
            proj_a, proj_a, proj_a, proj_a, proj_a, proj_a, dmc, dmc, conv_w, conv_b)


def _disc_math(lr, li, ldt, br, bi):
    dt = jnp.exp(ldt)
    mag = jnp.exp(lr * dt)
    ang = li * dt
    are = mag * jnp.cos(ang)
    aim = mag * jnp.sin(ang)
    nr = are - 1.0
    den = lr * lr + li * li
    cre = (nr * lr + aim * li) / den
    cim = (aim * lr - nr * li) / den
    return are, aim, cre * br - cim * bi, cre * bi + cim * br


def _disc_fwd(lr, li, ldt, br, bi):
    shapes = [lr.shape, lr.shape, br.shape, br.shape]

    def body(lr_ref, li_ref, ldt_ref, br_ref, bi_ref, *outs):
        for o, v in zip(outs, _disc_math(lr_ref[...], li_ref[...], ldt_ref[...], br_ref[...], bi_ref[...])):
            o[...] = v

    return pl.pallas_call(body, out_shape=[jax.ShapeDtypeStruct(s, F32) for s in shapes],
                          compiler_params=_params(), name="ssm_disc")(lr, li, ldt, br, bi)


def _disc_bwd(lr, li, ldt, br, bi, cts):
    shapes = [lr.shape, lr.shape, ldt.shape, br.shape, br.shape]

    def body(lr_ref, li_ref, ldt_ref, br_ref, bi_ref, c0, c1, c2, c3, *outs):
        _, vjp = jax.vjp(_disc_math, lr_ref[...], li_ref[...], ldt_ref[...], br_ref[...], bi_ref[...])
        for o, v in zip(outs, vjp((c0[...], c1[...], c2[...], c3[...]))):
            o[...] = v

    return pl.pallas_call(body, out_shape=[jax.ShapeDtypeStruct(s, F32) for s in shapes],
                          compiler_params=_params(), name="ssm_disc_bwd")(lr, li, ldt, br, bi, *cts)


def _scan_fwd(bu, abar, name):
    S = bu.shape[0]
    tb = min(256, S)

    def body(bu_ref, a_ref, xs_ref, st_ref):
        @pl.when(pl.program_id(0) == 0)
        def _():
            st_ref[...] = jnp.zeros_like(st_ref)

        ar, ai = a_ref[0:8, :], a_ref[8:16, :]

        def step(t, c):
            xr, xi = c
            nr = ar * xr - ai * xi + bu_ref[t, 0:8, :]
            ni = ar * xi + ai * xr + bu_ref[t, 8:16, :]
            xs_ref[t, 0:8, :] = nr
            xs_ref[t, 8:16, :] = ni
            return nr, ni

        xr, xi = lax.fori_loop(0, tb, step, (st_ref[0:8, :], st_ref[8:16, :]), unroll=8)
        st_ref[0:8, :] = xr
        st_ref[8:16, :] = xi

    blk = pl.BlockSpec((tb, 16, 128), lambda i: (i, 0, 0))
    return pl.pallas_call(
        body, grid=(S // tb,), in_specs=[blk, _whole((16, 128))], out_specs=blk,
        out_shape=jax.ShapeDtypeStruct((S, 16, 128), F32), scratch_shapes=[pltpu.VMEM((16, 128), F32)],
        compiler_params=_params(("arbitrary",)), name=name)(bu, abar)


def _scan_bwd(dxs, xs, abar, name):
    S = dxs.shape[0]
    tb = min(256, S)
    nblk = S // tb

    def body(d_ref, x_ref, xp_ref, a_ref, lam_ref, da_ref, st_ref):
        i = pl.program_id(0)

        @pl.when(i == 0)
        def _():
            st_ref[...] = jnp.zeros_like(st_ref)
            da_ref[...] = jnp.zeros_like(da_ref)

        ar, ai = a_ref[0:8, :], a_ref[8:16, :]

        def one(t, c, pr, pi):
            lr, li, gr, gi = c
            nr = d_ref[t, 0:8, :] + ar * lr + ai * li
            ni = d_ref[t, 8:16, :] - ai * lr + ar * li
            lam_ref[t, 0:8, :] = nr
            lam_ref[t, 8:16, :] = ni
            return nr, ni, gr + nr * pr + ni * pi, gi - nr * pi + ni * pr

        def step(s, c):
            t = tb - 1 - s
            return one(t, c, x_ref[t - 1, 0:8, :], x_ref[t - 1, 8:16, :])

        c = (st_ref[0:8, :], st_ref[8:16, :], jnp.zeros((8, 128), F32), jnp.zeros((8, 128), F32))
        c = lax.fori_loop(0, tb - 1, step, c, unroll=8)
        first = i == nblk - 1
        pr = jnp.where(first, 0.0, xp_ref[0, 0:8, :])
        pi = jnp.where(first, 0.0, xp_ref[0, 8:16, :])
        lr, li, gr, gi = one(0, c, pr, pi)
        st_ref[0:8, :] = lr
        st_ref[8:16, :] = li
        da_ref[0:8, :] += gr
        da_ref[8:16, :] += gi

    blk = pl.BlockSpec((tb, 16, 128), lambda i: (nblk - 1 - i, 0, 0))
    prev = pl.BlockSpec((1, 16, 128), lambda i: (jnp.maximum((nblk - 1 - i) * tb - 1, 0), 0, 0))
    return pl.pallas_call(
        body, grid=(nblk,), in_specs=[blk, blk, prev, _whole((16, 128))], out_specs=[blk, _whole((16, 128))],
        out_shape=[jax.ShapeDtypeStruct((S, 16, 128), F32), jax.ShapeDtypeStruct((16, 128), F32)],
        scratch_shapes=[pltpu.VMEM((16, 128), F32)],
        compiler_params=_params(("arbitrary",)), name=name)(dxs, xs, xs, abar)


def _gelu(x):
    return 0.5 * x * (1.0 + jnp.tanh(0.7978845608028654 * (x + 0.044715 * x * x * x)))


def _gelu_grad(x):
    t = jnp.tanh(0.7978845608028654 * (x + 0.044715 * x * x * x))
    return 0.5 * (1.0 + t) + 0.5 * x * (1.0 - t * t) * 0.7978845608028654 * (1.0 + 3.0 * 0.044715 * x * x)


def _ssm_out(xs, proj_a, cfull, dskip, wglu, name):
    S = xs.shape[0]
    tm = min(256, S)
    C = D_SSM

    def body(xs_ref, u_ref, c_ref, d_ref, wg_ref, y1_ref, ms_ref):
        y1 = jnp.dot(xs_ref[...].astype(BF16), c_ref[...], preferred_element_type=F32) + d_ref[...] * u_ref[...]
        y2 = _gelu(y1)
        gl = jnp.dot(y2.astype(BF16), wg_ref[...], preferred_element_type=F32)
        y1_ref[...] = y1
        ms_ref[...] = (y2 * _sigmoid(gl)).astype(BF16)

    wga, wgs = _op(wglu, (C, C), lambda i: (0, 0))
    return pl.pallas_call(
        body, grid=(S // tm,),
        in_specs=[_rows(2 * 1024, 0, tm), _rows(C, 3, tm), _whole((2 * 1024, C)), _whole((1, C)), wgs],
        out_specs=[_rows(C, 0, tm), _rows(C, 0, tm)],
        out_shape=[jax.ShapeDtypeStruct((S, C), F32), jax.ShapeDtypeStruct((S, C), BF16)],
        compiler_params=_params(("parallel",)), name=name)(xs, proj_a, cfull, dskip, wga)


def _ssm_out_bwd(dms, y1, proj_a, cfull, dskip, wglu, name):
    S = y1.shape[0]
    tm = min(256, S)
    C = D_SSM

    def body(dms_ref, y1_ref, u_ref, c_ref, d_ref, wg_ref, dy1_ref, y2_ref, dgl_ref, dxs_ref, du_ref, dd_ref):
        @pl.when(pl.program_id(0) == 0)
        def _():
            dd_ref[...] = jnp.zeros_like(dd_ref)

        dms, y1 = dms_ref[...], y1_ref[...]
        y2 = _gelu(y1)
        y2b = y2.astype(BF16)
        sg = _sigmoid(jnp.dot(y2b, wg_ref[...], preferred_element_type=F32))
        dgl = (dms * y2 * sg * (1.0 - sg)).astype(BF16)
        dy2 = dms * sg + lax.dot_general(dgl, wg_ref[...], (((1,), (1,)), ((), ())), preferred_element_type=F32)
        dy1 = dy2 * _gelu_grad(y1)
        dy1b = dy1.astype(BF16)
        dy1_ref[...] = dy1b
        y2_ref[...] = y2b
        dgl_ref[...] = dgl
        dxs_ref[...] = lax.dot_general(dy1b, c_ref[...], (((1,), (1,)), ((), ())), preferred_element_type=F32)
        du_ref[...] = d_ref[...] * dy1
        dd_ref[...] += jnp.sum(dy1 * u_ref[...], axis=0, keepdims=True)

    wga, wgs = _op(wglu, (C, C), lambda i: (0, 0))
    rc = _rows(C, 0, tm)
    return pl.pallas_call(
        body, grid=(S // tm,),
        in_specs=[rc, rc, _rows(C, 3, tm), _whole((2 * 1024, C)), _whole((1, C)), wgs],
        out_specs=[rc, rc, rc, _rows(2 * 1024, 0, tm), rc, _whole((1, C))],
        out_shape=[jax.ShapeDtypeStruct((S, C), BF16), jax.ShapeDtypeStruct((S, C), BF16),
                   jax.ShapeDtypeStruct((S, C), BF16), jax.ShapeDtypeStruct((S, 2 * 1024), F32),
                   jax.ShapeDtypeStruct((S, C), F32), jax.ShapeDtypeStruct((1, C), F32)],
        compiler_params=_params(("arbitrary",)), name=name)(dms, y1, proj_a, cfull, dskip, wga)


def _attn_mask(i):
    row = lax.broadcasted_iota(jnp.int32, (BLOCK, 2 * BLOCK), 0)
    col = lax.broadcasted_iota(jnp.int32, (BLOCK, 2 * BLOCK), 1)
    return (col > row) & (col <= row + BLOCK) & ((col >= BLOCK) | (i > 0))


def _attn_probs(qm, ksel, mask, sink):
    s = lax.dot_general(qm, ksel, (((1,), (1,)), ((), ())), preferred_element_type=F32)
    s = jnp.where(mask, s, -1e30)
    m = jnp.maximum(jnp.max(s, axis=1, keepdims=True), sink)
    p = jnp.exp(s - m)
    es = jnp.exp(sink - m)
    den = jnp.sum(p, axis=1, keepdims=True) + es
    return p / den, es / den


def _attn_fwd(proj_a, sinks, name):
    S = proj_a.shape[0]
    nb = S // BLOCK

    def body(q_ref, kp_ref, kc_ref, vp_ref, vc_ref, s_ref, out_ref):
        i = pl.program_id(0)
        mask = _attn_mask(i)
        lane = lax.broadcasted_iota(jnp.int32, (BLOCK, 128), 1)
        kk = jnp.concatenate([kp_ref[...], kc_ref[...]], axis=0).astype(BF16)
        vv = jnp.concatenate([vp_ref[...], vc_ref[...]], axis=0).astype(BF16)
        kk_r, vv_r = pltpu.roll(kk, 64, axis=1), pltpu.roll(vv, 64, axis=1)
        for j in range(4):
            qj = (q_ref[:, 128 * j:128 * (j + 1)] * (HEAD_DIM ** -0.5)).astype(BF16)
            outj = jnp.zeros((BLOCK, 128), F32)
            for e in range(2):
                hq = 2 * j + e
                own = (lane >= 64) if e else (lane < 64)
                aligned = e == hq // 4
                qm = jnp.where(own, qj, jnp.zeros_like(qj))
                pn, _ = _attn_probs(qm, kk if aligned else kk_r, mask, s_ref[0, hq])
                o = jnp.dot(pn.astype(BF16), vv if aligned else vv_r, preferred_element_type=F32)
                outj = jnp.where(own, o, outj)
            out_ref[:, 128 * j:128 * (j + 1)] = outj.astype(BF16)

    prev = lambda c: pl.BlockSpec((BLOCK, 128), lambda i: (jnp.maximum(i - 1, 0), c))
    cur = lambda c: pl.BlockSpec((BLOCK, 128), lambda i: (i, c))
    return pl.pallas_call(
        body, grid=(nb,),
        in_specs=[_rows(D_ATTN, 2, BLOCK), prev(12), cur(12), prev(13), cur(13),
                  pl.BlockSpec(memory_space=pltpu.SMEM)],
        out_specs=_rows(D_ATTN, 0, BLOCK), out_shape=jax.ShapeDtypeStruct((S, D_ATTN), BF16),
        compiler_params=_params(("parallel",)), name=name)(proj_a, proj_a, proj_a, proj_a, proj_a, sinks)


def _attn_bwd(proj_a, dout, sinks, name):
    S = proj_a.shape[0]
    nb = S // BLOCK

    def body(q_ref, kp_ref, kc_ref, vp_ref, vc_ref, do_ref, s_ref, out_ref, dk_ref, dv_ref, ds_ref, ck_ref, cv_ref):
        i = pl.program_id(0)

        @pl.when(i == 0)
        def _():
            ds_ref[...] = jnp.zeros_like(ds_ref)
            ck_ref[...] = jnp.zeros_like(ck_ref)
            cv_ref[...] = jnp.zeros_like(cv_ref)

        @pl.when(i < nb)
        def _():
            mask = _attn_mask(i)
            lane = lax.broadcasted_iota(jnp.int32, (BLOCK, 128), 1)
            kk = jnp.concatenate([kp_ref[...], kc_ref[...]], axis=0).astype(BF16)
            vv = jnp.concatenate([vp_ref[...], vc_ref[...]], axis=0).astype(BF16)
            kk_r, vv_r = pltpu.roll(kk, 64, axis=1), pltpu.roll(vv, 64, axis=1)
            dkk = jnp.zeros((2 * BLOCK, 128), F32)
            dvv = jnp.zeros((2 * BLOCK, 128), F32)
            for j in range(4):
                qj = (q_ref[:, 128 * j:128 * (j + 1)] * (HEAD_DIM ** -0.5)).astype(BF16)
                doj = do_ref[:, 128 * j:128 * (j + 1)].astype(BF16)
                dqj = jnp.zeros((BLOCK, 128), F32)
                for e in range(2):
                    hq = 2 * j + e
                    own = (lane >= 64) if e else (lane < 64)
                    aligned = e == hq // 4
                    ksel, vsel = (kk, vv) if aligned else (kk_r, vv_r)
                    qm = jnp.where(own, qj, jnp.zeros_like(qj))
                    dom = jnp.where(own, doj, jnp.zeros_like(doj))
                    pn, psink = _attn_probs(qm, ksel, mask, s_ref[0, hq])
                    dp = lax.dot_general(dom, vsel, (((1,), (1,)), ((), ())), preferred_element_type=F32)
                    delta = jnp.sum(pn * dp, axis=1, keepdims=True)
                    dsb = (pn * (dp - delta)).astype(BF16)
                    pnb = pn.astype(BF16)
                    ds_ref[hq:hq + 1, :] += jnp.broadcast_to(-jnp.sum(psink * delta), (1, 128))
                    dq = jnp.dot(dsb, ksel, preferred_element_type=F32) * (HEAD_DIM ** -0.5)
                    dqj = jnp.where(own, dq, dqj)
                    tn_dims = (((0,), (0,)), ((), ()))
                    dk = lax.dot_general(dsb, qm, tn_dims, preferred_element_type=F32)
                    dv = lax.dot_general(pnb, dom, tn_dims, preferred_element_type=F32)
                    dkk = dkk + (dk if aligned else pltpu.roll(dk, 64, axis=1))
                    dvv = dvv + (dv if aligned else pltpu.roll(dv, 64, axis=1))
                out_ref[:, 128 * j:128 * (j + 1)] = dqj.astype(BF16)
            ck_ref[0:BLOCK, :] = ck_ref[BLOCK:, :] + dkk[0:BLOCK]
            cv_ref[0:BLOCK, :] = cv_ref[BLOCK:, :] + dvv[0:BLOCK]
            ck_ref[BLOCK:, :] = dkk[BLOCK:]
            cv_ref[BLOCK:, :] = dvv[BLOCK:]

        @pl.when(i == nb)
        def _():
            ck_ref[0:BLOCK, :] = ck_ref[BLOCK:, :]
            cv_ref[0:BLOCK, :] = cv_ref[BLOCK:, :]

        dk_ref[...] = ck_ref[0:BLOCK, :].astype(BF16)
        dv_ref[...] = cv_ref[0:BLOCK, :].astype(BF16)

    last = nb - 1
    prev = lambda c: pl.BlockSpec((BLOCK, 128), lambda i: (jnp.clip(i - 1, 0, last), c))
    cur = lambda c: pl.BlockSpec((BLOCK, 128), lambda i: (jnp.minimum(i, last), c))
    qrow = lambda w, c: pl.BlockSpec((BLOCK, w), lambda i: (jnp.minimum(i, last), c))

    dq, dk, dv, ds = pl.pallas_call(
        body, grid=(nb + 1,),
        in_specs=[qrow(D_ATTN, 2), prev(12), cur(12), prev(13), cur(13), qrow(D_ATTN, 0),
                  pl.BlockSpec(memory_space=pltpu.SMEM)],
        out_specs=[qrow(D_ATTN, 0), prev(0), prev(0), _whole((N_Q_HEADS, 128))],
        out_shape=[jax.ShapeDtypeStruct((S, D_ATTN), BF16), jax.ShapeDtypeStruct((S, D_KV), BF16),
                   jax.ShapeDtypeStruct((S, D_KV), BF16), jax.ShapeDtypeStruct((N_Q_HEADS, 128), F32)],
        scratch_shapes=[pltpu.VMEM((2 * BLOCK, 128), F32), pltpu.VMEM((2 * BLOCK, 128), F32)],
        compiler_params=_params(("arbitrary",)), name=name)(proj_a, proj_a, proj_a, proj_a, proj_a, dout, sinks)
    return dq, dk, dv, ds


_BR = ((0, D_CONV), (D_CONV, D_CONV + D_SSM), (D_CONV + D_SSM, D_CONV + D_SSM + D_ATTN))


def _branches(m_refs, wbr_ref):
    nt = (((1,), (1,)), ((), ()))
    return [lax.dot_general(m[...], wbr_ref[:, lo:hi], nt, preferred_element_type=F32)
            for m, (lo, hi) in zip(m_refs, _BR)]


def _merge_fwd(mc, ms, ma, proj_g, wbr_t, name):
    S, Dm = mc.shape[0], proj_g.shape[1] // 3
    tm = min(256, S)

    def body(mc_ref, ms_ref, ma_ref, g_ref, w_ref, out_ref):
        ys = _branches((mc_ref, ms_ref, ma_ref), w_ref)
        acc = None
        for b in range(3):
            t = _sigmoid(g_ref[:, b * Dm:(b + 1) * Dm]) * ys[b]
            acc = t if acc is None else acc + t
        out_ref[...] = acc.astype(BF16)

    wa, ws = _op(wbr_t, (Dm, Dm), lambda i: (0, 0))
    return pl.pallas_call(
        body, grid=(S // tm,),
        in_specs=[_rows(D_CONV, 0, tm), _rows(D_SSM, 0, tm), _rows(D_ATTN, 0, tm), _rows(3 * Dm, 0, tm), ws],
        out_specs=_rows(Dm, 0, tm), out_shape=jax.ShapeDtypeStruct((S, Dm), BF16),
        compiler_params=_params(("parallel",)), name=name)(mc, ms, ma, proj_g, wa)


def _merge_bwd(dmerged, mc, ms, ma, proj_g, wbr_t, name):
    S, Dm = mc.shape[0], proj_g.shape[1] // 3
    tm = min(256, S)

    def body(d_ref, mc_ref, ms_ref, ma_ref, g_ref, w_ref, dg_ref, dy_ref, dmc_ref, dms_ref, dma_ref):
        ys = _branches((mc_ref, ms_ref, ma_ref), w_ref)
        d = d_ref[...]
        for b, (o_ref, (lo, hi)) in enumerate(zip((dmc_ref, dms_ref, dma_ref), _BR)):
            g = _sigmoid(g_ref[:, b * Dm:(b + 1) * Dm])
            dg_ref[:, b * Dm:(b + 1) * Dm] = (d * ys[b] * g * (1.0 - g)).astype(BF16)
            dyb = (g * d).astype(BF16)
            dy_ref[:, b * Dm:(b + 1) * Dm] = dyb
            o_ref[...] = jnp.dot(dyb, w_ref[:, lo:hi], preferred_element_type=F32)

    wa, ws = _op(wbr_t, (Dm, Dm), lambda i: (0, 0))
    return pl.pallas_call(
        body, grid=(S // tm,),
        in_specs=[_rows(Dm, 0, tm), _rows(D_CONV, 0, tm), _rows(D_SSM, 0, tm), _rows(D_ATTN, 0, tm),
                  _rows(3 * Dm, 0, tm), ws],
        out_specs=[_rows(3 * Dm, 0, tm), _rows(3 * Dm, 0, tm), _rows(D_CONV, 0, tm), _rows(D_SSM, 0, tm),
                   _rows(D_ATTN, 0, tm)],
        out_shape=[jax.ShapeDtypeStruct((S, 3 * Dm), BF16), jax.ShapeDtypeStruct((S, 3 * Dm), BF16),
                   jax.ShapeDtypeStruct((S, D_CONV), F32), jax.ShapeDtypeStruct((S, D_SSM), F32),
                   jax.ShapeDtypeStruct((S, D_ATTN), F32)],
        compiler_params=_params(("parallel",)), name=name)(dmerged, mc, ms, ma, proj_g, wa)


def _loss_head(y, target):
    S, Dm = y.shape
    tm = min(512, S)

    def body(y_ref, t_ref, dy_ref, l_ref):
        @pl.when(pl.program_id(0) == 0)
        def _():
            l_ref[...] = jnp.zeros_like(l_ref)

        e = y_ref[...] - t_ref[...]
        dy_ref[...] = e * (1.0 / Dm)
        l_ref[...] += jnp.broadcast_to(0.5 * jnp.sum(jnp.sum(e * e, axis=1, keepdims=True) * (1.0 / Dm)), (1, 128))

    return pl.pallas_call(
        body, grid=(S // tm,), in_specs=[_rows(Dm, 0, tm), _rows(Dm, 0, tm)],
        out_specs=[_rows(Dm, 0, tm), _whole((1, 128))],
        out_shape=[jax.ShapeDtypeStruct((S, Dm), F32), jax.ShapeDtypeStruct((1, 128), F32)],
        compiler_params=_params(("arbitrary",)), name="loss_head")(y, target)


def _view2d(shape):
    n = math.prod(shape)
    if shape[-1] % 128 == 0:
        return (n // shape[-1], shape[-1])
    if n >= (1 << 16) and len(shape) == 3:
        return (shape[0] * shape[1], shape[2])
    if n % 128 == 0:
        return (n // 128, 128)
    return (1, n)


def _adamw(w, g, m, v, name):
    shape = w.shape
    R, C = _view2d(shape)
    tm = R
    for cand in (512, 352, 256):
        if R > cand and R % cand == 0:
            tm = cand
            break
    c1 = 1.0 - ADAM_B1 ** ADAM_STEP
    c2 = 1.0 - ADAM_B2 ** ADAM_STEP

    def body(w_ref, g_ref, m_ref, v_ref, d_ref, nm_ref, nv_ref):
        g = g_ref[...]
        nm = ADAM_B1 * m_ref[...] + (1.0 - ADAM_B1) * g
        nv = ADAM_B2 * v_ref[...] + (1.0 - ADAM_B2) * (g * g)
        d_ref[...] = -ADAM_LR * ((nm / c1) / (jnp.sqrt(nv / c2) + ADAM_EPS) + ADAM_WD * w_ref[...])
        nm_ref[...] = nm
        nv_ref[...] = nv

    blk = _rows(C, 0, tm)
    outs = pl.pallas_call(
        body, grid=(R // tm,), in_specs=[blk] * 4, out_specs=[blk] * 3,
        out_shape=[jax.ShapeDtypeStruct((R, C), F32)] * 3,
        compiler_params=_params(("parallel",)), name=name)(*[t.reshape(R, C) for t in (w, g, m, v)])
    return [o.reshape(shape) for o in outs]


def _coords():
    return lax.axis_index("x"), lax.axis_index("y"), lax.axis_index("c")


def _other_chips(x, y):
    return [((1 - x, y), 2 * (1 - x) + y), ((x, 1 - y), 2 * x + 1 - y), ((1 - x, 1 - y), 2 * (1 - x) + 1 - y)]


def _exchange(build, n_in, out_shapes, n_sem, aliases, name):
    def body(*refs):
        ins, outs = refs[:n_in], refs[n_in:n_in + len(out_shapes)]
        send_sems, recv_sems, local_sems = refs[n_in + len(out_shapes):]
        sends, recvs, locals_ = build(ins, outs, send_sems, recv_sems, local_sems)
        for cp in locals_:
            cp.start()
        for cp in sends:
            cp.start()
        for cp in recvs:
            cp.wait_recv()
        for cp in sends:
            cp.wait_send()
        for cp in locals_:
            cp.wait()

    hbm = pl.BlockSpec(memory_space=pltpu.HBM)
    return pl.pallas_call(
        body, in_specs=[hbm] * n_in, out_specs=[hbm] * len(out_shapes), out_shape=out_shapes,
        scratch_shapes=[pltpu.SemaphoreType.DMA((n_sem,)), pltpu.SemaphoreType.DMA((n_sem,)),
                        pltpu.SemaphoreType.DMA((n_sem,))],
        input_output_aliases=aliases,
        compiler_params=pltpu.CompilerParams(has_side_effects=True), name=name)


def _remote(src, dst, send_sems, recv_sems, k, dev):
    return pltpu.make_async_remote_copy(src_ref=src, dst_ref=dst, send_sem=send_sems.at[k], recv_sem=recv_sems.at[k],
                                        device_id=dev, device_id_type=MESH)


def _gather_weights(shards):
    n = len(shards)
    fulls = [jax.ShapeDtypeStruct((s.shape[0], N_CHIPS * s.shape[1], s.shape[2]), s.dtype) for s in shards]

    def half(ref, chip, c, rp):
        return ref.at[:, pl.ds(pl.multiple_of(chip * rp + c * (rp // 2), 16), rp // 2), :]

    def step1(ins, outs, send_sems, recv_sems, local_sems):
        x, y, c = _coords()
        me = 2 * x + y
        sends, recvs, locals_ = [], [], []
        for a in range(n):
            rp = ins[a].shape[1]
            locals_.append(pltpu.make_async_copy(
                ins[a], outs[a].at[:, pl.ds(pl.multiple_of(me * rp, 16), rp), :], local_sems.at[a]))
            src = ins[a].at[:, pl.ds(pl.multiple_of(c * (rp // 2), 16), rp // 2), :]
            for r, ((px, py), chip) in enumerate(_other_chips(x, y)):
                k = 3 * a + r
                sends.append(_remote(src, half(outs[a], me, c, rp), send_sems, recv_sems, k, (px, py, c)))
                recvs.append(_remote(src, half(outs[a], chip, c, rp), send_sems, recv_sems, k, (px, py, c)))
        return sends, recvs, locals_

    got = _exchange(step1, n, fulls, 3 * n, {}, "gather_weights_ici")(*shards)

    def step2(ins, outs, send_sems, recv_sems, local_sems):
        x, y, c = _coords()
        sends, recvs = [], []
        for a in range(n):
            rp = ins[a].shape[1] // N_CHIPS
            for r, (_, chip) in enumerate(_other_chips(x, y)):
                k = 3 * a + r
                mine, theirs = half(outs[a], chip, c, rp), half(outs[a], chip, 1 - c, rp)
                sends.append(_remote(mine, mine, send_sems, recv_sems, k, (x, y, 1 - c)))
                recvs.append(_remote(theirs, theirs, send_sems, recv_sems, k, (x, y, 1 - c)))
        return sends, recvs, []

    return _exchange(step2, n, fulls, 3 * n, {a: a for a in range(n)}, "gather_weights_d2d")(*got)


def _pair_sum(a, b, c_idx, half_rows, out_dtype, name):
    n4, rp, W = a.shape
    tr = half_rows // 2
    nblk = half_rows // tr

    def body(c_ref, a_ref, b_ref, o_ref):
        o_ref[...] = (a_ref[...] + b_ref[...]).astype(o_ref.dtype)

    return pl.pallas_call(
        body,
        grid_spec=pltpu.PrefetchScalarGridSpec(
            num_scalar_prefetch=1, grid=(nblk,),
            in_specs=[pl.BlockSpec((n4, tr, W), lambda i, c: (0, c[0] * nblk + i, 0)),
                      pl.BlockSpec((n4, tr, W), lambda i, c: (0, i, 0))],
            out_specs=pl.BlockSpec((n4, tr, W), lambda i, c: (0, i, 0))),
        out_shape=jax.ShapeDtypeStruct((n4, half_rows, W), out_dtype),
        compiler_params=_params(("parallel",)), name=name)(c_idx, a, b)


def _sum4(m, c_idx, name):
    _, R, W = m.shape
    tr = R // 2

    def body(c_ref, m_ref, o_ref):
        v = m_ref[...].astype(F32)
        o_ref[...] = ((v[0] + v[1]) + v[2]) + v[3]

    return pl.pallas_call(
        body,
        grid_spec=pltpu.PrefetchScalarGridSpec(
            num_scalar_prefetch=1, grid=(R // tr,),
            in_specs=[pl.BlockSpec((N_CHIPS, tr, W), lambda i, c: (0, i, 0))],
            out_specs=pl.BlockSpec((None, tr, W), lambda i, c: (c[0], i, 0))),
        out_shape=jax.ShapeDtypeStruct((2, R, W), F32),
        compiler_params=_params(("parallel",)), name=name)(c_idx, m)


def _reduce_scatter(grads):
    n = len(grads)
    x, y, c = _coords()
    g4 = [g.reshape(N_CHIPS, g.shape[0] // N_CHIPS, g.shape[1]) for g in grads]
    hr = [g.shape[1] // 2 for g in g4]

    def step1(ins, outs, send_sems, recv_sems, local_sems):
        x, y, c = _coords()
        sends, recvs = [], []
        for a in range(n):
            src = ins[a].at[:, pl.ds(pl.multiple_of((1 - c) * hr[a], 8), hr[a]), :]
            sends.append(_remote(src, outs[a], send_sems, recv_sems, a, (x, y, 1 - c)))
            recvs.append(_remote(src, outs[a], send_sems, recv_sems, a, (x, y, 1 - c)))
        return sends, recvs, []

    lands = [jax.ShapeDtypeStruct((N_CHIPS, hr[a], g4[a].shape[2]), F32) for a in range(n)]
    got = _exchange(step1, n, lands, n, {}, "reduce_d2d")(*g4)
    c_idx = jnp.reshape(c, (1,)).astype(jnp.int32)
    q = [_pair_sum(g4[a], got[a], c_idx, hr[a], BF16, "reduce_pair_sum") for a in range(n)]

    def step2(ins, outs, send_sems, recv_sems, local_sems):
        x, y, c = _coords()
        me = 2 * x + y
        sends, recvs, locals_ = [], [], []
        for a in range(n):
            locals_.append(pltpu.make_async_copy(ins[a].at[me], outs[a].at[me], local_sems.at[a]))
            for r, ((px, py), chip) in enumerate(_other_chips(x, y)):
                k = 3 * a + r
                sends.append(_remote(ins[a].at[chip], outs[a].at[me], send_sems, recv_sems, k, (px, py, c)))
                recvs.append(_remote(ins[a].at[chip], outs[a].at[chip], send_sems, recv_sems, k, (px, py, c)))
        return sends, recvs, locals_

    lands = [jax.ShapeDtypeStruct(t.shape, BF16) for t in q]
    got = _exchange(step2, n, lands, 3 * n, {}, "reduce_ici")(*q)
    r = [_sum4(got[a], c_idx, "reduce_sum4") for a in range(n)]

    def step3(ins, outs, send_sems, recv_sems, local_sems):
        x, y, c = _coords()
        sends, recvs = [], []
        for a in range(n):
            sends.append(_remote(outs[a].at[c], outs[a].at[c], send_sems, recv_sems, a, (x, y, 1 - c)))
            recvs.append(_remote(outs[a].at[c], outs[a].at[1 - c], send_sems, recv_sems, a, (x, y, 1 - c)))
        return sends, recvs, []

    lands = [jax.ShapeDtypeStruct(t.shape, F32) for t in r]
    got = _exchange(step3, n, lands, n, {a: a for a in range(n)}, "reduce_share")(*r)
    return [t.reshape(2 * t.shape[1], t.shape[2]) for t in got]


def _all_reduce_small(buf, name):
    R = buf.shape[0]

    def step(ins, outs, send_sems, recv_sems, local_sems):
        x, y, c = _coords()
        me = 4 * x + 2 * y + c
        sends, recvs = [], []
        k = 0
        for fx in range(2):
            for fy in range(2):
                for fc in range(2):
                    if fx + fy + fc == 0:
                        continue
                    px, py, pc = x ^ fx, y ^ fy, c ^ fc
                    sends.append(_remote(ins[0], outs[0].at[me], send_sems, recv_sems, k, (px, py, pc)))
                    recvs.append(_remote(ins[0], outs[0].at[4 * px + 2 * py + pc], send_sems, recv_sems, k,
                                         (px, py, pc)))
                    k += 1
        return sends, recvs, [pltpu.make_async_copy(ins[0], outs[0].at[me], local_sems.at[0])]

    got = _exchange(step, 1, [jax.ShapeDtypeStruct((N_DEV, R, 128), F32)], N_DEV - 1, {}, name + "_gather")(buf)[0]

    def body(m_ref, o_ref):
        acc = m_ref[0]
        for d in range(1, N_DEV):
            acc = acc + m_ref[d]
        o_ref[...] = acc

    tr = R
    for cand in (512, 256, 128, 64, 32, 16, 8):
        if R % cand == 0:
            tr = cand
            break
    return pl.pallas_call(
        body, grid=(R // tr,), in_specs=[pl.BlockSpec((N_DEV, tr, 128), lambda i: (0, i, 0))],
        out_specs=pl.BlockSpec((tr, 128), lambda i: (i, 0)), out_shape=jax.ShapeDtypeStruct((R, 128), F32),
        compiler_params=_params(("parallel",)), name=name + "_sum")(got)


def _ssm_layouts(p, L):
    G = L * N_GROUPS
    lr = p["ssm_a_re"].reshape(G, 1, SSM_STATE)
    li = p["ssm_a_im"].reshape(G, 1, SSM_STATE)
    ldt = p["ssm_log_dt"].reshape(G, 1, 1)
    br = jnp.swapaxes(p["ssm_b_re"], 2, 3).reshape(G, SSM_GROUP, SSM_STATE)
    bi = jnp.swapaxes(p["ssm_b_im"], 2, 3).reshape(G, SSM_GROUP, SSM_STATE)
    are, aim, bre, bim = _disc_fwd(lr, li, ldt, br, bi)
    eye = jnp.eye(N_GROUPS, dtype=F32)
    abar = jnp.concatenate([are.reshape(L, 8, 128), aim.reshape(L, 8, 128)], axis=1)

    def b_blk(t):
        return jnp.einsum("lgcp,gh->lgchp", t.reshape(L, N_GROUPS, SSM_GROUP, SSM_STATE), eye).reshape(L, 256, 1024)

    def c_blk(t):
        return jnp.einsum("lgcp,gh->lgphc", t, eye).reshape(L, 1024, 256)

    bfull = jnp.concatenate([b_blk(bre), b_blk(bim)], axis=2).astype(BF16)
    cfull = jnp.concatenate([c_blk(p["ssm_c_re"]), -c_blk(p["ssm_c_im"])], axis=1).astype(BF16)
    return (lr, li, ldt, br, bi), abar, bfull, cfull


def _local_step(x, target, W, p, L):
    S, Dm = x.shape
    dff = W["wg1"].shape[1]
    alpha = (2.0 * L) ** 0.25
    disc_in, abar, bfull, cfull = _ssm_layouts(p, L)
    row = lambda t, l: t[l][None]
    saved = []
    h, hb = x, x.astype(BF16)
    for l in range(L):
        sv = {"x0": hb}
        a1, b1, h1 = _ffn_up(hb, (W["wg1"], l), (W["wu1"], l), dff, "ffn_up")
        x1, x1b, xh1, rs1 = _mm_ln(h1, (W["wd1"], l), h, row(p["ln1_g"], l), row(p["ln1_b"], l), 0.5, alpha,
                                   "ffn_down_ln", 1408)
        sv.update(a1=a1, b1=b1, h1=h1, x1=x1b, xh1=xh1, rs1=rs1)
        proj_a = _mm([(x1b, (W["win"], l))], "nt", S, D_A, Dm, [F32], name="proj_a", tm=1024, tn=256, tk=1024)
        proj_g = _mm([(x1b, (W["win"], l))], "nt", S, 3 * Dm, Dm, [F32], name="proj_g", tm=1024, tn=256, tk=1024,
                     b_off=D_A)
        mc = _conv_fwd(proj_a, p["conv_w"][l], row(p["conv_b"], l), "conv_fwd")
        bu = _mm([(proj_a, (bfull, l))], "nn", S, 2048, D_SSM, [F32], name="ssm_bu", tm=1024, tn=1024, tk=256,
                 a_off=3 * D_CONV)
        xs = _scan_fwd(bu.reshape(S, 16, 128), abar[l], "scan_fwd").reshape(S, 2048)
        y1, ms = _ssm_out(xs, proj_a, cfull[l], row(p["ssm_d"], l), (W["wglu"], l), "ssm_out")
        sinks = p["attn_sinks"][l][None]
        ma = _attn_fwd(proj_a, sinks, "attn_fwd")
        merged = _merge_fwd(mc, ms, ma, proj_g, (W["wbr"], l), "merge_fwd")
        x2, x2b, xh2, rs2 = _mm_ln(merged, (W["wout"], l), x1, row(p["ln2_g"], l), row(p["ln2_b"], l), 1.0, alpha,
                                   "mix_out_ln", 1024)
        sv.update(proj_a=proj_a, proj_g=proj_g, mc=mc, ms=ms, ma=ma, xs=xs, y1=y1, merged=merged, x2=x2b, xh2=xh2,
                  rs2=rs2)
        a2, b2, h2 = _ffn_up(x2b, (W["wg2"], l), (W["wu2"], l), dff, "ffn_up")
        h, hb, xh3, rs3 = _mm_ln(h2, (W["wd2"], l), x2, row(p["ln3_g"], l), row(p["ln3_b"], l), 0.5, alpha,
                                 "ffn_down_ln", 1408)
        sv.update(a2=a2, b2=b2, h2=h2, xh3=xh3, rs3=rs3)
        saved.append(sv)

    dy, loss = _loss_head(h, target)
    big = [None] * L
    small = {k: [None] * L for k in ("ln1_g", "ln1_b", "ln2_g", "ln2_b", "ln3_g", "ln3_b", "conv_w", "conv_b", "ssm_d",
                                     "attn_sinks", "wglu", "dabar", "dbfull", "dcfull")}

    def ffn_bwd(dy_out, x_in, a, b, hh, xh, rs, g, wg, wu, wd, l):
        dres, df, dg, db = _ln_bwd(dy_out, xh, rs, g, 0.5, alpha, "ln_bwd")
        da, dbb = _ffn_dh(df, (wd, l), a, b, "ffn_dh")
        dx = _mm([(da, (wg, l)), (dbb, (wu, l))], "nn", S, Dm, dff, [F32], name="ffn_dx", tm=1024, tn=1024, tk=256,
                 add=dres)
        tn_kw = dict(tm=1408, tn=1024, tk=512)
        dwg = _mm([(da, x_in)], "tn", dff, Dm, S, [F32], name="ffn_dw_up", **tn_kw)
        dwu = _mm([(dbb, x_in)], "tn", dff, Dm, S, [F32], name="ffn_dw_up", **tn_kw)
        dwd = _mm([(hh, df)], "tn", dff, Dm, S, [F32], name="ffn_dw_down", **tn_kw)
        return dx, dwg, dwu, dwd, dg, db

    for l in reversed(range(L)):
        sv = saved[l]
        dx2, dwg2, dwu2, dwd2, small["ln3_g"][l], small["ln3_b"][l] = ffn_bwd(
            dy, sv["x2"], sv["a2"], sv["b2"], sv["h2"], sv["xh3"], sv["rs3"], row(p["ln3_g"], l), W["wg2"], W["wu2"],
            W["wd2"], l)
        dres2, dmix, small["ln2_g"][l], small["ln2_b"][l] = _ln_bwd(dx2, sv["xh2"], sv["rs2"], row(p["ln2_g"], l), 1.0,
                                                                   alpha, "ln_bwd")
        dwout = _mm([(sv["merged"], dmix)], "tn", Dm, Dm, S, [F32], name="dw_out", tm=512, tn=1024, tk=512)
        dmerged = _mm([(dmix, (W["wout"], l))], "nt", S, Dm, Dm, [F32], name="d_merged", tm=1024, tn=1024, tk=512)
        dgates, dyb, dmc, dms, dma = _merge_bwd(dmerged, sv["mc"], sv["ms"], sv["ma"], sv["proj_g"], (W["wbr"], l),
                                                "merge_bwd")
        dwbr = jnp.concatenate([
            _mm([(dyb, m)], "tn", Dm, hi - lo, S, [F32], name="dw_br", tm=512, tn=512, tk=512, a_off=b * Dm)
            for b, ((lo, hi), m) in enumerate(zip(_BR, (sv["mc"], sv["ms"], sv["ma"])))], axis=1)
        proj_a = sv["proj_a"]
        d_conv, small["conv_w"][l], small["conv_b"][l] = _conv_bwd(proj_a, dmc, p["conv_w"][l], row(p["conv_b"], l),
                                                                  "conv_bwd")
        dy1, y2, dgl, dxs, du_skip, small["ssm_d"][l] = _ssm_out_bwd(dms, sv["y1"], proj_a, cfull[l],
                                                                    row(p["ssm_d"], l), (W["wglu"], l), "ssm_out_bwd")
        small["wglu"][l] = _mm([(y2, dgl)], "tn", D_SSM, D_SSM, S, [F32], name="dw_glu", tk=512)
        small["dcfull"][l] = _mm([(sv["xs"], dy1)], "tn", 2048, D_SSM, S, [F32], name="d_cfull", tm=1024, tk=512)
        lam, small["dabar"][l] = _scan_bwd(dxs.reshape(S, 16, 128), sv["xs"].reshape(S, 16, 128), abar[l], "scan_bwd")
        lam = lam.reshape(S, 2048)
        du = _mm([(lam, (bfull, l))], "nt", S, D_SSM, 2048, [BF16], name="ssm_du", tm=1024, tk=512, add=du_skip)
        small["dbfull"][l] = _mm([(proj_a, lam)], "tn", D_SSM, 2048, S, [F32], name="d_bfull", tm=256, tn=1024,
                                 tk=512, a_off=3 * D_CONV)
        dq, dk, dv, dsk = _attn_bwd(proj_a, dma, p["attn_sinks"][l][None], "attn_bwd")
        small["attn_sinks"][l] = dsk[:, 0]
        dproj = jnp.concatenate([d_conv, du, dq, dk, dv, dgates], axis=1)
        dx1 = _mm([(dproj, (W["win"], l))], "nn", S, Dm, D_A + 3 * Dm, [F32], name="d_x1", tm=1024, tn=1024, tk=256,
                  add=dres2)
        dwin = _mm([(dproj, sv["x1"])], "tn", D_A + 3 * Dm, Dm, S, [F32], name="dw_in", tm=2432, tn=1024, tk=512)
        dx0, dwg1, dwu1, dwd1, small["ln1_g"][l], small["ln1_b"][l] = ffn_bwd(
            dx1, sv["x0"], sv["a1"], sv["b1"], sv["h1"], sv["xh1"], sv["rs1"], row(p["ln1_g"], l), W["wg1"], W["wu1"],
            W["wd1"], l)
        big[l] = dict(wg1=dwg1, wu1=dwu1, wd1=dwd1, win=dwin, wbr=dwbr, wout=dwout, wg2=dwg2, wu2=dwu2, wd2=dwd2)
        dy = dx0

    small = {k: jnp.stack(v) for k, v in small.items()}
    eye = jnp.eye(N_GROUPS, dtype=F32)
    dabar = small.pop("dabar")
    dbf = small.pop("dbfull").reshape(L, N_GROUPS, SSM_GROUP, 2, N_GROUPS, SSM_STATE)
    dbbar = jnp.einsum("lgcrhp,gh->rlgcp", dbf, eye).reshape(2, L * N_GROUPS, SSM_GROUP, SSM_STATE)
    dcf = small.pop("dcfull").reshape(L, 2, N_GROUPS, SSM_STATE, N_GROUPS, SSM_GROUP)
    dc = jnp.einsum("lrgphc,gh->rlgcp", dcf, eye)
    G = L * N_GROUPS
    cts = (dabar[:, 0:8].reshape(G, 1, SSM_STATE), dabar[:, 8:16].reshape(G, 1, SSM_STATE), dbbar[0], dbbar[1])
    dlr, dli, dldt, dbr, dbi = _disc_bwd(*disc_in, cts)
    shp_b = (L, N_GROUPS, SSM_GROUP, SSM_STATE)
    small.update(
        ssm_a_re=dlr.reshape(L, N_GROUPS, SSM_STATE), ssm_a_im=dli.reshape(L, N_GROUPS, SSM_STATE),
        ssm_log_dt=dldt.reshape(L, N_GROUPS), ssm_b_re=jnp.swapaxes(dbr.reshape(shp_b), 2, 3),
        ssm_b_im=jnp.swapaxes(dbi.reshape(shp_b), 2, 3), ssm_c_re=dc[0], ssm_c_im=-dc[1],
        ln1_g=small["ln1_g"][:, 0], ln1_b=small["ln1_b"][:, 0], ln2_g=small["ln2_g"][:, 0],
        ln2_b=small["ln2_b"][:, 0], ln3_g=small["ln3_g"][:, 0], ln3_b=small["ln3_b"][:, 0],
        conv_b=small["conv_b"][:, 0], ssm_d=small["ssm_d"][:, 0])
    return loss, dy, big, small


_SMALL_ORDER = ("ln1_g", "ln1_b", "ln2_g", "ln2_b", "ln3_g", "ln3_b", "conv_w", "conv_b", "ssm_a_re", "ssm_a_im",
                "ssm_log_dt", "ssm_b_re", "ssm_b_im", "ssm_c_re", "ssm_c_im", "ssm_d", "attn_sinks", "wglu")
_BIG_ORDER = ("wg1", "wu1", "wd1", "win", "wbr", "wout", "wg2", "wu2", "wd2")
_WEIGHTS = ("ffn1_w_gate", "ffn1_w_up", "ffn1_w_down", "ln1_g", "ln1_b", "w_in", "conv_w", "conv_b", "ssm_a_re",
            "ssm_a_im", "ssm_log_dt", "ssm_b_re", "ssm_b_im", "ssm_c_re", "ssm_c_im", "ssm_d", "ssm_w_glu",
            "attn_sinks", "w_br_conv", "w_br_ssm", "w_br_attn", "w_out", "ln2_g", "ln2_b", "ffn2_w_gate",
            "ffn2_w_up", "ffn2_w_down", "ln3_g", "ln3_b")


def _weight_shards(w):
    t = lambda a: jnp.swapaxes(a, 1, 2).astype(BF16)
    wbr = jnp.concatenate([t(w["w_br_conv"]), t(w["w_br_ssm"]), t(w["w_br_attn"])], axis=2)
    return dict(wg1=t(w["ffn1_w_gate"]), wu1=t(w["ffn1_w_up"]), wd1=w["ffn1_w_down"].astype(BF16), win=t(w["w_in"]),
                wbr=wbr, wout=w["w_out"].astype(BF16), wg2=t(w["ffn2_w_gate"]), wu2=t(w["ffn2_w_up"]),
                wd2=w["ffn2_w_down"].astype(BF16))


def _gather_small_shards(conv_w, w_glu, chip):
    L = conv_w.shape[0]
    part = jnp.concatenate([conv_w.reshape(1, -1), w_glu.reshape(1, -1)], axis=1)
    n = part.shape[1]
    slots = lax.dynamic_update_slice(jnp.zeros((N_CHIPS, n), F32), part, (chip, 0))
    got = (_all_reduce_small(slots.reshape(-1, 128), "param_gather") * 0.5).reshape(N_CHIPS, n)
    cw = jnp.transpose(got[:, :L * 3 * 64].reshape(N_CHIPS, L, 3, 64), (1, 2, 0, 3)).reshape(L, 3, N_CHIPS * 64)
    wg = jnp.transpose(got[:, L * 3 * 64:].reshape(N_CHIPS, L, 64, D_SSM), (1, 0, 2, 3)).reshape(L, N_CHIPS * 64, D_SSM)
    return cw, wg.astype(BF16)


def kernel(x, ffn1_w_gate, ffn1_w_up, ffn1_w_down, ln1_g, ln1_b, w_in, conv_w, conv_b, ssm_a_re, ssm_a_im, ssm_log_dt, ssm_b_re, ssm_b_im, ssm_c_re, ssm_c_im, ssm_d, ssm_w_glu, attn_sinks, w_br_conv, w_br_ssm, w_br_attn, w_out, ln2_g, ln2_b, ffn2_w_gate, ffn2_w_up, ffn2_w_down, ln3_g, ln3_b, loss_target, m_ffn1_w_gate, m_ffn1_w_up, m_ffn1_w_down, m_ln1_g, m_ln1_b, m_w_in, m_conv_w, m_conv_b, m_ssm_a_re, m_ssm_a_im, m_ssm_log_dt, m_ssm_b_re, m_ssm_b_im, m_ssm_c_re, m_ssm_c_im, m_ssm_d, m_ssm_w_glu, m_attn_sinks, m_w_br_conv, m_w_br_ssm, m_w_br_attn, m_w_out, m_ln2_g, m_ln2_b, m_ffn2_w_gate, m_ffn2_w_up, m_ffn2_w_down, m_ln3_g, m_ln3_b, v_ffn1_w_gate, v_ffn1_w_up, v_ffn1_w_down, v_ln1_g, v_ln1_b, v_w_in, v_conv_w, v_conv_b, v_ssm_a_re, v_ssm_a_im, v_ssm_log_dt, v_ssm_b_re, v_ssm_b_im, v_ssm_c_re, v_ssm_c_im, v_ssm_d, v_ssm_w_glu, v_attn_sinks, v_w_br_conv, v_w_br_ssm, v_w_br_attn, v_w_out, v_ln2_g, v_ln2_b, v_ffn2_w_gate, v_ffn2_w_up, v_ffn2_w_down, v_ln3_g, v_ln3_b):
    args = dict(locals())
    w = {k: args[k] for k in _WEIGHTS}
    L = ln1_g.shape[0]
    cx, cy, cc = _coords()
    chip = 2 * cx + cy

    shards = _weight_shards(w)
    names = list(shards)
    full = dict(zip(names, _gather_weights([shards[k] for k in names])))
    p = {k: w[k] for k in ("ln1_g", "ln1_b", "ln2_g", "ln2_b", "ln3_g", "ln3_b", "conv_b", "ssm_a_re", "ssm_a_im",
                           "ssm_log_dt", "ssm_b_re", "ssm_b_im", "ssm_c_re", "ssm_c_im", "ssm_d", "attn_sinks")}
    p["conv_w"], full["wglu"] = _gather_small_shards(conv_w, ssm_w_glu, chip)

    loss, grad_x, big, small = _local_step(x[0], loss_target[0], full, p, L)
    loss = lax.psum(loss[0, 0], ("x", "y", "c"))

    sizes = [math.prod(small[k].shape) for k in _SMALL_ORDER]
    pad = (-sum(sizes)) % 1024
    flat = jnp.concatenate([small[k].reshape(-1) for k in _SMALL_ORDER] + [jnp.zeros((pad,), F32)])
    flat = _all_reduce_small(flat.reshape(-1, 128), "small_grads").reshape(-1)
    sm, off = {}, 0
    for k, n in zip(_SMALL_ORDER, sizes):
        sm[k] = flat[off:off + n].reshape(small[k].shape)
        off += n
    red = _reduce_scatter([big[l][k] for l in range(L) for k in _BIG_ORDER])
    red = {k: jnp.stack([red[l * len(_BIG_ORDER) + i] for l in range(L)]) for i, k in enumerate(_BIG_ORDER)}
    tr = lambda a: jnp.swapaxes(a, 1, 2)
    grads = dict(sm)
    grads.update(
        ffn1_w_gate=tr(red["wg1"]), ffn1_w_up=tr(red["wu1"]), ffn1_w_down=red["wd1"], w_in=tr(red["win"]),
        w_br_conv=tr(red["wbr"][:, :, _BR[0][0]:_BR[0][1]]), w_br_ssm=tr(red["wbr"][:, :, _BR[1][0]:_BR[1][1]]),
        w_br_attn=tr(red["wbr"][:, :, _BR[2][0]:_BR[2][1]]), w_out=red["wout"], ffn2_w_gate=tr(red["wg2"]),
        ffn2_w_up=tr(red["wu2"]), ffn2_w_down=red["wd2"],
        ssm_w_glu=lax.dynamic_slice_in_dim(sm["wglu"], chip * 64, 64, axis=1),
        conv_w=lax.dynamic_slice_in_dim(sm["conv_w"], chip * 64, 64, axis=2))

    outs = [[], [], [], []]
    for k in _WEIGHTS:
        d, nm, nv = _adamw(w[k], grads[k], args["m_" + k], args["v_" + k], "adamw")
        for lst, val in zip(outs, (grads[k], d, nm, nv)):
            lst.append(val)
    return (loss, grad_x[None], *outs[0], *outs[1], *outs[2], *outs[3])
```

```python
import functools
import math

import jax
import jax.numpy as jnp
from jax import lax
from jax.experimental import pallas as pl
from jax.experimental.pallas import tpu as pltpu

F32 = jnp.float32
BF16 = jnp.bfloat16

LN_EPS = 1e-5
D_CONV = 256
D_SSM = 256
N_GROUPS = 16
SSM_GROUP = 16
SSM_STATE = 64
N_Q_HEADS = 8
HEAD_DIM = 64
D_ATTN = 512
D_KV = 128
BLOCK = 128
D_A = 3 * D_CONV + D_SSM + D_ATTN + 2 * D_KV
ADAM_LR = 0.001
ADAM_B1 = 0.9
ADAM_B2 = 0.999
ADAM_EPS = 1e-08
ADAM_WD = 0.01
ADAM_STEP = 10

VMEM_LIMIT_BYTES = 56 * 1024 * 1024
MESH = pl.DeviceIdType.MESH
N_CHIPS = 4
N_DEV = 8


def _params(sem=None):
    return pltpu.CompilerParams(dimension_semantics=sem, vmem_limit_bytes=VMEM_LIMIT_BYTES)


def _op(op, block, imap):
    if isinstance(op, tuple):
        arr, l = op
        return arr, pl.BlockSpec((None,) + block, lambda *g: (l,) + imap(*g))
    return op, pl.BlockSpec(block, imap)


def _mm(pairs, mode, m, n, k, out_dtypes, *, name, tm=512, tn=512, tk=512, add=None, a_off=0, b_off=0):
    tm, tn, tk = min(tm, m), min(tn, n), min(tk, k)
    assert m % tm == 0 and n % tn == 0 and k % tk == 0 and b_off % tn == 0, (name, m, n, k, tm, tn, tk)
    nk, npair, jo = k // tk, len(pairs), b_off // tn
    if mode == "nn":
        ao = a_off // tk
        ab, ai, bb, bi = (tm, tk), (lambda i, j, kk: (i, kk + ao)), (tk, tn), (lambda i, j, kk: (kk, j))
        dims = (((1,), (0,)), ((), ()))
    elif mode == "nt":
        ao = a_off // tk
        ab, ai, bb, bi = (tm, tk), (lambda i, j, kk: (i, kk + ao)), (tn, tk), (lambda i, j, kk: (j + jo, kk))
        dims = (((1,), (1,)), ((), ()))
    else:
        ao = a_off // tm
        ab, ai, bb, bi = (tk, tm), (lambda i, j, kk: (kk, i + ao)), (tk, tn), (lambda i, j, kk: (kk, j))
        dims = (((0,), (0,)), ((), ()))
    assert a_off % (tm if mode == "tn" else tk) == 0, (name, a_off)
    ops, specs = [], []
    for a, b in pairs:
        for o, blk, im in ((a, ab, ai), (b, bb, bi)):
            arr, sp = _op(o, blk, im)
            ops.append(arr)
            specs.append(sp)
    has_add = add is not None
    if has_add:
        ops.append(add)
        specs.append(pl.BlockSpec((tm, tn), lambda i, j, kk: (i, j)))
    nout = len(out_dtypes)

    def body(*refs):
        outs, acc = refs[2 * npair + has_add:2 * npair + has_add + nout], refs[-1]
        kk = pl.program_id(2)
        t = None
        for p in range(npair):
            d = lax.dot_general(refs[2 * p][...].astype(BF16), refs[2 * p + 1][...].astype(BF16), dims,
                                preferred_element_type=F32)
            t = d if t is None else t + d

        def finish(r):
            if has_add:
                r = r + refs[2 * npair][...]
            for o in outs:
                o[...] = r.astype(o.dtype)

        if nk == 1:
            finish(t)
            return

        @pl.when(kk == 0)
        def _():
            acc[...] = t

        @pl.when((kk > 0) & (kk < nk - 1))
        def _():
            acc[...] += t

        @pl.when(kk == nk - 1)
        def _():
            finish(acc[...] + t)

    res = pl.pallas_call(
        body, grid=(m // tm, n // tn, nk), in_specs=specs,
        out_specs=[pl.BlockSpec((tm, tn), lambda i, j, kk: (i, j))] * nout,
        out_shape=[jax.ShapeDtypeStruct((m, n), dt) for dt in out_dtypes],
        scratch_shapes=[pltpu.VMEM((tm, tn), F32)],
        compiler_params=_params(("parallel", "parallel", "arbitrary")), name=name)(*ops)
    return res[0] if nout == 1 else res


def _rows(width, col=0, tm=None):
    return pl.BlockSpec((tm, width), lambda i: (i, col))


def _whole(shape):
    nd = len(shape)
    return pl.BlockSpec(shape, lambda i: (0,) * nd)


def _sigmoid(x):
    return 0.5 * jnp.tanh(0.5 * x) + 0.5


def _mm_ln(a, w, x, g, b, s, alpha, name, tk):
    S, K = a.shape
    Dm = x.shape[1]
    tm, tk = min(512, S), min(tk, K)
    nk = K // tk

    def body(a_ref, w_ref, x_ref, g_ref, b_ref, y_ref, yb_ref, xh_ref, rs_ref, acc):
        kk = pl.program_id(1)
        d = jnp.dot(a_ref[...], w_ref[...], preferred_element_type=F32)

        if nk > 1:
            @pl.when(kk == 0)
            def _():
                acc[...] = d

            @pl.when((kk > 0) & (kk < nk - 1))
            def _():
                acc[...] += d

        @pl.when(kk == nk - 1)
        def _():
            z = alpha * x_ref[...] + s * (d if nk == 1 else acc[...] + d)
            mu = jnp.mean(z, axis=-1, keepdims=True)
            zc = z - mu
            var = jnp.mean(zc * zc, axis=-1, keepdims=True)
            rstd = lax.rsqrt(var + LN_EPS)
            xh = zc * rstd
            y = xh * g_ref[...] + b_ref[...]
            y_ref[...] = y
            yb_ref[...] = y.astype(BF16)
            xh_ref[...] = xh
            rs_ref[...] = rstd

    wa, ws = _op(w, (tk, Dm), lambda i, kk: (kk, 0))
    row = pl.BlockSpec((tm, Dm), lambda i, kk: (i, 0))
    vec = pl.BlockSpec((1, Dm), lambda i, kk: (0, 0))
    return pl.pallas_call(
        body, grid=(S // tm, nk),
        in_specs=[pl.BlockSpec((tm, tk), lambda i, kk: (i, kk)), ws, row, vec, vec],
        out_specs=[row, row, row, pl.BlockSpec((tm, 1), lambda i, kk: (i, 0))],
        out_shape=[jax.ShapeDtypeStruct((S, Dm), F32), jax.ShapeDtypeStruct((S, Dm), BF16),
                   jax.ShapeDtypeStruct((S, Dm), F32), jax.ShapeDtypeStruct((S, 1), F32)],
        scratch_shapes=[pltpu.VMEM((tm, Dm), F32)],
        compiler_params=_params(("parallel", "arbitrary")), name=name)(a, wa, x, g, b)


def _ln_bwd(dy, xh, rs, g, s, alpha, name):
    S, Dm = dy.shape
    tm = min(256, S)

    def body(dy_ref, xh_ref, rs_ref, g_ref, dres_ref, dbr_ref, dg_ref, db_ref):
        @pl.when(pl.program_id(0) == 0)
        def _():
            dg_ref[...] = jnp.zeros_like(dg_ref)
            db_ref[...] = jnp.zeros_like(db_ref)

        dy, xh = dy_ref[...], xh_ref[...]
        dyg = dy * g_ref[...]
        m1 = jnp.mean(dyg, axis=-1, keepdims=True)
        m2 = jnp.mean(dyg * xh, axis=-1, keepdims=True)
        dz = rs_ref[...] * (dyg - m1 - xh * m2)
        dres_ref[...] = alpha * dz
        dbr_ref[...] = (s * dz).astype(BF16)
        dg_ref[...] += jnp.sum(dy * xh, axis=0, keepdims=True)
        db_ref[...] += jnp.sum(dy, axis=0, keepdims=True)

    return pl.pallas_call(
        body, grid=(S // tm,),
        in_specs=[_rows(Dm, tm=tm), _rows(Dm, tm=tm), _rows(1, tm=tm), _whole((1, Dm))],
        out_specs=[_rows(Dm, tm=tm), _rows(Dm, tm=tm), _whole((1, Dm)), _whole((1, Dm))],
        out_shape=[jax.ShapeDtypeStruct((S, Dm), F32), jax.ShapeDtypeStruct((S, Dm), BF16),
                   jax.ShapeDtypeStruct((1, Dm), F32), jax.ShapeDtypeStruct((1, Dm), F32)],
        compiler_params=_params(("arbitrary",)), name=name)(dy, xh, rs, g)


def _ffn_up(x, wg, wu, dff, name):
    S, Dm = x.shape
    tm, tn = min(512, S), dff // 2

    def body(x_ref, wg_ref, wu_ref, a_ref, b_ref, h_ref):
        xb = x_ref[...]
        dims = (((1,), (1,)), ((), ()))
        a = lax.dot_general(xb, wg_ref[...], dims, preferred_element_type=F32)
        b = lax.dot_general(xb, wu_ref[...], dims, preferred_element_type=F32)
        a_ref[...] = a.astype(BF16)
        b_ref[...] = b.astype(BF16)
        h_ref[...] = (a * _sigmoid(a) * b).astype(BF16)

    wga, wgs = _op(wg, (tn, Dm), lambda i, j: (j, 0))
    wua, wus = _op(wu, (tn, Dm), lambda i, j: (j, 0))
    ob = pl.BlockSpec((tm, tn), lambda i, j: (i, j))
    return pl.pallas_call(
        body, grid=(S // tm, dff // tn),
        in_specs=[pl.BlockSpec((tm, Dm), lambda i, j: (i, 0)), wgs, wus], out_specs=[ob, ob, ob],
        out_shape=[jax.ShapeDtypeStruct((S, dff), BF16)] * 3,
        compiler_params=_params(("parallel", "parallel")), name=name)(x, wga, wua)


def _ffn_dh(df, wd, a, b, name):
    S, Dm = df.shape
    dff = a.shape[1]
    tm, tn = min(512, S), dff // 2

    def body(df_ref, wd_ref, a_ref, b_ref, da_ref, db_ref):
        dh = lax.dot_general(df_ref[...], wd_ref[...], (((1,), (1,)), ((), ())), preferred_element_type=F32)
        a, b = a_ref[...].astype(F32), b_ref[...].astype(F32)
        sg = _sigmoid(a)
        da_ref[...] = (dh * b * (sg * (1.0 + a * (1.0 - sg)))).astype(BF16)
        db_ref[...] = (dh * (a * sg)).astype(BF16)

    wda, wds = _op(wd, (tn, Dm), lambda i, j: (j, 0))
    ob = pl.BlockSpec((tm, tn), lambda i, j: (i, j))
    return pl.pallas_call(
        body, grid=(S // tm, dff // tn),
        in_specs=[pl.BlockSpec((tm, Dm), lambda i, j: (i, 0)), wds, ob, ob], out_specs=[ob, ob],
        out_shape=[jax.ShapeDtypeStruct((S, dff), BF16), jax.ShapeDtypeStruct((S, dff), BF16)],
        compiler_params=_params(("parallel", "parallel")), name=name)(df, wda, a, b)


def _halo_prev(width, col, tm):
    return pl.BlockSpec((8, width), lambda i: (jnp.maximum(i * (tm // 8) - 1, 0), col))


def _halo_next(width, col, tm, S):
    return pl.BlockSpec((8, width), lambda i: (jnp.minimum((i + 1) * (tm // 8), S // 8 - 1), col))


def _shift_down(prev8, cur, n):
    ext = jnp.concatenate([prev8, cur], axis=0)
    return pltpu.roll(ext, n, axis=0)[8:]


def _shift_up(cur, next8, n):
    ext = jnp.concatenate([cur, next8], axis=0)
    return pltpu.roll(ext, ext.shape[0] - n, axis=0)[:cur.shape[0]]


def _conv_fwd(proj_a, conv_w, conv_b, name):
    S = proj_a.shape[0]
    tm = min(512, S)
    C = D_CONV

    def body(bg_ref, cg_ref, h_ref, cgp_ref, hp_ref, w_ref, cb_ref, out_ref):
        z = cg_ref[...] * h_ref[...]
        zp = jnp.where(pl.program_id(0) > 0, cgp_ref[...] * hp_ref[...], 0.0)
        w = w_ref[...]
        y = w[2:3] * z + w[1:2] * _shift_down(zp, z, 1) + w[0:1] * _shift_down(zp, z, 2) + cb_ref[...]
        out_ref[...] = (bg_ref[...] * y).astype(BF16)

    return pl.pallas_call(
        body, grid=(S // tm,),
        in_specs=[_rows(C, 0, tm), _rows(C, 1, tm), _rows(C, 2, tm), _halo_prev(C, 1, tm), _halo_prev(C, 2, tm),
                  _whole((3, C)), _whole((1, C))],
        out_specs=_rows(C, 0, tm), out_shape=jax.ShapeDtypeStruct((S, C), BF16),
        compiler_params=_params(("parallel",)), name=name)(proj_a, proj_a, proj_a, proj_a, proj_a, conv_w, conv_b)


def _conv_bwd(proj_a, dmc, conv_w, conv_b, name):
    S = proj_a.shape[0]
    tm = min(512, S)
    C = D_CONV
    nblk = S // tm

    def body(bg_ref, cg_ref, h_ref, cgp_ref, hp_ref, bgn_ref, d_ref, dn_ref, w_ref, cb_ref, out_ref, dw_ref,
             dcb_ref):
        i = pl.program_id(0)

        @pl.when(i == 0)
        def _():
            dw_ref[...] = jnp.zeros_like(dw_ref)
            dcb_ref[...] = jnp.zeros_like(dcb_ref)

        bg, cg, h, d = bg_ref[...], cg_ref[...], h_ref[...], d_ref[...]
        z = cg * h
        zp = jnp.where(i > 0, cgp_ref[...] * hp_ref[...], 0.0)
        w = w_ref[...]
        z1, z2 = _shift_down(zp, z, 1), _shift_down(zp, z, 2)
        y = w[2:3] * z + w[1:2] * z1 + w[0:1] * z2 + cb_ref[...]
        dy = d * bg
        dyn = jnp.where(i < nblk - 1, dn_ref[...] * bgn_ref[...], 0.0)
        dz = w[2:3] * dy + w[1:2] * _shift_up(dy, dyn, 1) + w[0:1] * _shift_up(dy, dyn, 2)
        out_ref[:, 0:C] = (d * y).astype(BF16)
        out_ref[:, C:2 * C] = (dz * h).astype(BF16)
        out_ref[:, 2 * C:3 * C] = (dz * cg).astype(BF16)
        dw_ref[0:1, :] += jnp.sum(dy * z2, axis=0, keepdims=True)
        dw_ref[1:2, :] += jnp.sum(dy * z1, axis=0, keepdims=True)
        dw_ref[2:3, :] += jnp.sum(dy * z, axis=0, keepdims=True)
        dcb_ref[...] += jnp.sum(dy, axis=0, keepdims=True)

    return pl.pallas_call(
        body, grid=(nblk,),
        in_specs=[_rows(C, 0, tm), _rows(C, 1, tm), _rows(C, 2, tm), _halo_prev(C, 1, tm), _halo_prev(C, 2, tm),
                  _halo_next(C, 0, tm, S), _rows(C, 0, tm), _halo_next(C, 0, tm, S), _whole((3, C)), _whole((1, C))],
        out_specs=[_rows(3 * C, 0, tm), _whole((3, C)), _whole((1, C))],
        out_shape=[jax.ShapeDtypeStruct((S, 3 * C), BF16), jax.ShapeDtypeStruct((3, C), F32),
                   jax.ShapeDtypeStruct((1, C), F32)],
        compiler_params=_params(("arbitrary",)), name=name)(
            proj_a, proj_a, proj_a, proj_a, proj_a, proj_a, dmc, dmc, conv_w, conv_b)


def _disc_math(lr, li, ldt, br, bi):
    dt = jnp.exp(ldt)
    mag = jnp.exp(lr * dt)
    ang = li * dt
    are = mag * jnp.cos(ang)
    aim = mag * jnp.sin(ang)
    nr = are - 1.0
    den = lr * lr + li * li
    cre = (nr * lr + aim * li) / den
    cim = (aim * lr - nr * li) / den
    return are, aim, cre * br - cim * bi, cre * bi + cim * br


def _disc_fwd(lr, li, ldt, br, bi):
    shapes = [lr.shape, lr.shape, br.shape, br.shape]

    def body(lr_ref, li_ref, ldt_ref, br_ref, bi_ref, *outs):
        for o, v in zip(outs, _disc_math(lr_ref[...], li_ref[...], ldt_ref[...], br_ref[...], bi_ref[...])):
            o[...] = v

    return pl.pallas_call(body, out_shape=[jax.ShapeDtypeStruct(s, F32) for s in shapes],
                          compiler_params=_params(), name="ssm_disc")(lr, li, ldt, br, bi)


def _disc_bwd(lr, li, ldt, br, bi, cts):
    shapes = [lr.shape, lr.shape, ldt.shape, br.shape, br.shape]

    def body(lr_ref, li_ref, ldt_ref, br_ref, bi_ref, c0, c1, c2, c3, *outs):
        _, vjp = jax.vjp(_disc_math, lr_ref[...], li_ref[...], ldt_ref[...], br_ref[...], bi_ref[...])
        for o, v in zip(outs, vjp((c0[...], c1[...], c2[...], c3[...]))):
            o[...] = v

    return pl.pallas_call(body, out_shape=[jax.ShapeDtypeStruct(s, F32) for s in shapes],
                          compiler_params=_params(), name="ssm_disc_bwd")(lr, li, ldt, br, bi, *cts)


def _scan_fwd(bu, abar, name):
    S = bu.shape[0]
    tb = min(256, S)

    def body(bu_ref, a_ref, xs_ref, st_ref):
        @pl.when(pl.program_id(0) == 0)
        def _():
            st_ref[...] = jnp.zeros_like(st_ref)

        ar, ai = a_ref[0:8, :], a_ref[8:16, :]

        def step(t, c):
            xr, xi = c
            nr = ar * xr - ai * xi + bu_ref[t, 0:8, :]
            ni = ar * xi + ai * xr + bu_ref[t, 8:16, :]
            xs_ref[t, 0:8, :] = nr
            xs_ref[t, 8:16, :] = ni
            return nr, ni

        xr, xi = lax.fori_loop(0, tb, step, (st_ref[0:8, :], st_ref[8:16, :]), unroll=8)
        st_ref[0:8, :] = xr
        st_ref[8:16, :] = xi

    blk = pl.BlockSpec((tb, 16, 128), lambda i: (i, 0, 0))
    return pl.pallas_call(
        body, grid=(S // tb,), in_specs=[blk, _whole((16, 128))], out_specs=blk,
        out_shape=jax.ShapeDtypeStruct((S, 16, 128), F32), scratch_shapes=[pltpu.VMEM((16, 128), F32)],
        compiler_params=_params(("arbitrary",)), name=name)(bu, abar)


def _scan_bwd(dxs, xs, abar, name):
    S = dxs.shape[0]
    tb = min(256, S)
    nblk = S // tb

    def body(d_ref, x_ref, xp_ref, a_ref, lam_ref, da_ref, st_ref):
        i = pl.program_id(0)

        @pl.when(i == 0)
        def _():
            st_ref[...] = jnp.zeros_like(st_ref)
            da_ref[...] = jnp.zeros_like(da_ref)

        ar, ai = a_ref[0:8, :], a_ref[8:16, :]

        def one(t, c, pr, pi):
            lr, li, gr, gi = c
            nr = d_ref[t, 0:8, :] + ar * lr + ai * li
            ni = d_ref[t, 8:16, :] - ai * lr + ar * li
            lam_ref[t, 0:8, :] = nr
            lam_ref[t, 8:16, :] = ni
            return nr, ni, gr + nr * pr + ni * pi, gi - nr * pi + ni * pr

        def step(s, c):
            t = tb - 1 - s
            return one(t, c, x_ref[t - 1, 0:8, :], x_ref[t - 1, 8:16, :])

        c = (st_ref[0:8, :], st_ref[8:16, :], jnp.zeros((8, 128), F32), jnp.zeros((8, 128), F32))
        c = lax.fori_loop(0, tb - 1, step, c, unroll=8)
        first = i == nblk - 1
        pr = jnp.where(first, 0.0, xp_ref[0, 0:8, :])
        pi = jnp.where(first, 0.0, xp_ref[0, 8:16, :])
        lr, li, gr, gi = one(0, c, pr, pi)
        st_ref[0:8, :] = lr
        st_ref[8:16, :] = li
        da_ref[0:8, :] += gr
        da_ref[8:16, :] += gi

    blk = pl.BlockSpec((tb, 16, 128), lambda i: (nblk - 1 - i, 0, 0))
    prev = pl.BlockSpec((1, 16, 128), lambda i: (jnp.maximum((nblk - 1 - i) * tb - 1, 0), 0, 0))
    return pl.pallas_call(
        body, grid=(nblk,), in_specs=[blk, blk, prev, _whole((16, 128))], out_specs=[blk, _whole((16, 128))],
        out_shape=[jax.ShapeDtypeStruct((S, 16, 128), F32), jax.ShapeDtypeStruct((16, 128), F32)],
        scratch_shapes=[pltpu.VMEM((16, 128), F32)],
        compiler_params=_params(("arbitrary",)), name=name)(dxs, xs, xs, abar)


def _gelu(x):
    return 0.5 * x * (1.0 + jnp.tanh(0.7978845608028654 * (x + 0.044715 * x * x * x)))


def _gelu_grad(x):
    t = jnp.tanh(0.7978845608028654 * (x + 0.044715 * x * x * x))
    return 0.5 * (1.0 + t) + 0.5 * x * (1.0 - t * t) * 0.7978845608028654 * (1.0 + 3.0 * 0.044715 * x * x)


def _ssm_out(xs, proj_a, cfull, dskip, wglu, name):
    S = xs.shape[0]
    tm = min(256, S)
    C = D_SSM

    def body(xs_ref, u_ref, c_ref, d_ref, wg_ref, y1_ref, ms_ref):
        y1 = jnp.dot(xs_ref[...].astype(BF16), c_ref[...], preferred_element_type=F32) + d_ref[...] * u_ref[...]
        y2 = _gelu(y1)
        gl = jnp.dot(y2.astype(BF16), wg_ref[...], preferred_element_type=F32)
        y1_ref[...] = y1
        ms_ref[...] = (y2 * _sigmoid(gl)).astype(BF16)

    wga, wgs = _op(wglu, (C, C), lambda i: (0, 0))
    return pl.pallas_call(
        body, grid=(S // tm,),
        in_specs=[_rows(2 * 1024, 0, tm), _rows(C, 3, tm), _whole((2 * 1024, C)), _whole((1, C)), wgs],
        out_specs=[_rows(C, 0, tm), _rows(C, 0, tm)],
        out_shape=[jax.ShapeDtypeStruct((S, C), F32), jax.ShapeDtypeStruct((S, C), BF16)],
        compiler_params=_params(("parallel",)), name=name)(xs, proj_a, cfull, dskip, wga)


def _ssm_out_bwd(dms, y1, proj_a, cfull, dskip, wglu, name):
    S = y1.shape[0]
    tm = min(256, S)
    C = D_SSM

    def body(dms_ref, y1_ref, u_ref, c_ref, d_ref, wg_ref, dy1_ref, y2_ref, dgl_ref, dxs_ref, du_ref, dd_ref):
        @pl.when(pl.program_id(0) == 0)
        def _():
            dd_ref[...] = jnp.zeros_like(dd_ref)

        dms, y1 = dms_ref[...], y1_ref[...]
        y2 = _gelu(y1)
        y2b = y2.astype(BF16)
        sg = _sigmoid(jnp.dot(y2b, wg_ref[...], preferred_element_type=F32))
        dgl = (dms * y2 * sg * (1.0 - sg)).astype(BF16)
        dy2 = dms * sg + lax.dot_general(dgl, wg_ref[...], (((1,), (1,)), ((), ())), preferred_element_type=F32)
        dy1 = dy2 * _gelu_grad(y1)
        dy1b = dy1.astype(BF16)
        dy1_ref[...] = dy1b
        y2_ref[...] = y2b
        dgl_ref[...] = dgl
        dxs_ref[...] = lax.dot_general(dy1b, c_ref[...], (((1,), (1,)), ((), ())), preferred_element_type=F32)
        du_ref[...] = d_ref[...] * dy1
        dd_ref[...] += jnp.sum(dy1 * u_ref[...], axis=0, keepdims=True)

    wga, wgs = _op(wglu, (C, C), lambda i: (0, 0))
    rc = _rows(C, 0, tm)
    return pl.pallas_call(
        body, grid=(S // tm,),
        in_specs=[rc, rc, _rows(C, 3, tm), _whole((2 * 1024, C)), _whole((1, C)), wgs],
        out_specs=[rc, rc, rc, _rows(2 * 1024, 0, tm), rc, _whole((1, C))],
        out_shape=[jax.ShapeDtypeStruct((S, C), BF16), jax.ShapeDtypeStruct((S, C), BF16),
                   jax.ShapeDtypeStruct((S, C), BF16), jax.ShapeDtypeStruct((S, 2 * 1024), F32),
                   jax.ShapeDtypeStruct((S, C), F32), jax.ShapeDtypeStruct((1, C), F32)],
        compiler_params=_params(("arbitrary",)), name=name)(dms, y1, proj_a, cfull, dskip, wga)


_NT = (((1,), (1,)), ((), ()))
_TN = (((0,), (0,)), ((), ()))


def _attn_heads(q_ref, kp_ref, kc_ref, vp_ref, vc_ref):
    lane = lax.broadcasted_iota(jnp.int32, (BLOCK, 128), 1)
    kk = jnp.concatenate([kp_ref[...], kc_ref[...]], axis=0).astype(BF16)
    vv = jnp.concatenate([vp_ref[...], vc_ref[...]], axis=0).astype(BF16)
    kk_r, vv_r = pltpu.roll(kk, 64, axis=1), pltpu.roll(vv, 64, axis=1)
    heads = []
    for hq in range(N_Q_HEADS):
        j, e = hq // 2, hq % 2
        qj = (q_ref[:, 128 * j:128 * (j + 1)] * (HEAD_DIM ** -0.5)).astype(BF16)
        own = (lane >= 64) if e else (lane < 64)
        aligned = e == hq // 4
        heads.append((own, jnp.where(own, qj, jnp.zeros_like(qj)), kk if aligned else kk_r, vv if aligned else vv_r,
                      aligned))
    return heads


def _attn_probs(i, heads, s_ref):
    n = N_Q_HEADS * BLOCK
    s = jnp.concatenate([lax.dot_general(qm, ks, _NT, preferred_element_type=F32) for _, qm, ks, _, _ in heads],
                        axis=0)
    row = lax.broadcasted_iota(jnp.int32, (n, 2 * BLOCK), 0) & (BLOCK - 1)
    col = lax.broadcasted_iota(jnp.int32, (n, 2 * BLOCK), 1)
    mask = (col > row) & (col <= row + BLOCK) & ((col >= BLOCK) | (i > 0))
    s = jnp.where(mask, s, -1e30)
    sink = jnp.concatenate([jnp.full((BLOCK, 1), s_ref[0, hq], F32) for hq in range(N_Q_HEADS)], axis=0)
    m = jnp.maximum(jnp.max(s, axis=1, keepdims=True), sink)
    p = jnp.exp(s - m)
    es = jnp.exp(sink - m)
    inv = 1.0 / (jnp.sum(p, axis=1, keepdims=True) + es)
    return p * inv, es * inv


def _attn_fwd(proj_a, sinks, name):
    S = proj_a.shape[0]
    nb = S // BLOCK

    def body(q_ref, kp_ref, kc_ref, vp_ref, vc_ref, s_ref, out_ref):
        heads = _attn_heads(q_ref, kp_ref, kc_ref, vp_ref, vc_ref)
        pn, _ = _attn_probs(pl.program_id(0), heads, s_ref)
        pnb = pn.astype(BF16)
        outs = [jnp.dot(pnb[BLOCK * hq:BLOCK * (hq + 1)], vs, preferred_element_type=F32)
                for hq, (_, _, _, vs, _) in enumerate(heads)]
        for j in range(4):
            out_ref[:, 128 * j:128 * (j + 1)] = jnp.where(heads[2 * j][0], outs[2 * j], outs[2 * j + 1]).astype(BF16)

    prev = lambda c: pl.BlockSpec((BLOCK, 128), lambda i: (jnp.maximum(i - 1, 0), c))
    cur = lambda c: pl.BlockSpec((BLOCK, 128), lambda i: (i, c))
    return pl.pallas_call(
        body, grid=(nb,),
        in_specs=[_rows(D_ATTN, 2, BLOCK), prev(12), cur(12), prev(13), cur(13),
                  pl.BlockSpec(memory_space=pltpu.SMEM)],
        out_specs=_rows(D_ATTN, 0, BLOCK), out_shape=jax.ShapeDtypeStruct((S, D_ATTN), BF16),
        compiler_params=_params(("parallel",)), name=name)(proj_a, proj_a, proj_a, proj_a, proj_a, sinks)


def _attn_bwd(proj_a, dout, sinks, name):
    S = proj_a.shape[0]
    nb = S // BLOCK

    def body(q_ref, kp_ref, kc_ref, vp_ref, vc_ref, do_ref, s_ref, out_ref, dk_ref, dv_ref, ds_ref, ck_ref, cv_ref):
        i = pl.program_id(0)

        @pl.when(i == 0)
        def _():
            ds_ref[...] = jnp.zeros_like(ds_ref)
            ck_ref[...] = jnp.zeros_like(ck_ref)
            cv_ref[...] = jnp.zeros_like(cv_ref)

        @pl.when(i < nb)
        def _():
            heads = _attn_heads(q_ref, kp_ref, kc_ref, vp_ref, vc_ref)
            pn, psink = _attn_probs(i, heads, s_ref)
            doms = []
            for hq, (own, _, _, _, _) in enumerate(heads):
                doj = do_ref[:, 128 * (hq // 2):128 * (hq // 2 + 1)].astype(BF16)
                doms.append(jnp.where(own, doj, jnp.zeros_like(doj)))
            dp = jnp.concatenate([lax.dot_general(dom, vs, _NT, preferred_element_type=F32)
                                  for dom, (_, _, _, vs, _) in zip(doms, heads)], axis=0)
            delta = jnp.sum(pn * dp, axis=1, keepdims=True)
            dsb = (pn * (dp - delta)).astype(BF16)
            pnb = pn.astype(BF16)
            sk = psink * delta
            dkk = jnp.zeros((2 * BLOCK, 128), F32)
            dvv = jnp.zeros((2 * BLOCK, 128), F32)
            dqs = []
            for hq, (own, qm, ks, vs, aligned) in enumerate(heads):
                rows = slice(BLOCK * hq, BLOCK * (hq + 1))
                ds_ref[hq:hq + 1, :] += jnp.broadcast_to(-jnp.sum(sk[rows]), (1, 128))
                dqs.append(jnp.dot(dsb[rows], ks, preferred_element_type=F32) * (HEAD_DIM ** -0.5))
                dk = lax.dot_general(dsb[rows], qm, _TN, preferred_element_type=F32)
                dv = lax.dot_general(pnb[rows], doms[hq], _TN, preferred_element_type=F32)
                dkk = dkk + (dk if aligned else pltpu.roll(dk, 64, axis=1))
                dvv = dvv + (dv if aligned else pltpu.roll(dv, 64, axis=1))
            for j in range(4):
                out_ref[:, 128 * j:128 * (j + 1)] = jnp.where(heads[2 * j][0], dqs[2 * j], dqs[2 * j + 1]).astype(BF16)
            ck_ref[0:BLOCK, :] = ck_ref[BLOCK:, :] + dkk[0:BLOCK]
            cv_ref[0:BLOCK, :] = cv_ref[BLOCK:, :] + dvv[0:BLOCK]
            ck_ref[BLOCK:, :] = dkk[BLOCK:]
            cv_ref[BLOCK:, :] = dvv[BLOCK:]

        @pl.when(i == nb)
        def _():
            ck_ref[0:BLOCK, :] = ck_ref[BLOCK:, :]
            cv_ref[0:BLOCK, :] = cv_ref[BLOCK:, :]

        dk_ref[...] = ck_ref[0:BLOCK, :].astype(BF16)
        dv_ref[...] = cv_ref[0:BLOCK, :].astype(BF16)

    last = nb - 1
    prev = lambda c: pl.BlockSpec((BLOCK, 128), lambda i: (jnp.clip(i - 1, 0, last), c))
    cur = lambda c: pl.BlockSpec((BLOCK, 128), lambda i: (jnp.minimum(i, last), c))
    qrow = lambda w, c: pl.BlockSpec((BLOCK, w), lambda i: (jnp.minimum(i, last), c))

    dq, dk, dv, ds = pl.pallas_call(
        body, grid=(nb + 1,),
        in_specs=[qrow(D_ATTN, 2), prev(12), cur(12), prev(13), cur(13), qrow(D_ATTN, 0),
                  pl.BlockSpec(memory_space=pltpu.SMEM)],
        out_specs=[qrow(D_ATTN, 0), prev(0), prev(0), _whole((N_Q_HEADS, 128))],
        out_shape=[jax.ShapeDtypeStruct((S, D_ATTN), BF16), jax.ShapeDtypeStruct((S, D_KV), BF16),
                   jax.ShapeDtypeStruct((S, D_KV), BF16), jax.ShapeDtypeStruct((N_Q_HEADS, 128), F32)],
        scratch_shapes=[pltpu.VMEM((2 * BLOCK, 128), F32), pltpu.VMEM((2 * BLOCK, 128), F32)],
        compiler_params=_params(("arbitrary",)), name=name)(proj_a, proj_a, proj_a, proj_a, proj_a, dout, sinks)
    return dq, dk, dv, ds


_BR = ((0, D_CONV), (D_CONV, D_CONV + D_SSM), (D_CONV + D_SSM, D_CONV + D_SSM + D_ATTN))


def _branches(m_refs, wbr_ref):
    nt = (((1,), (1,)), ((), ()))
    return [lax.dot_general(m[...], wbr_ref[:, lo:hi], nt, preferred_element_type=F32)
            for m, (lo, hi) in zip(m_refs, _BR)]


def _merge_fwd(mc, ms, ma, proj_g, wbr_t, name):
    S, Dm = mc.shape[0], proj_g.shape[1] // 3
    tm = min(256, S)

    def body(mc_ref, ms_ref, ma_ref, g_ref, w_ref, out_ref):
        ys = _branches((mc_ref, ms_ref, ma_ref), w_ref)
        acc = None
        for b in range(3):
            t = _sigmoid(g_ref[:, b * Dm:(b + 1) * Dm]) * ys[b]
            acc = t if acc is None else acc + t
        out_ref[...] = acc.astype(BF16)

    wa, ws = _op(wbr_t, (Dm, Dm), lambda i: (0, 0))
    return pl.pallas_call(
        body, grid=(S // tm,),
        in_specs=[_rows(D_CONV, 0, tm), _rows(D_SSM, 0, tm), _rows(D_ATTN, 0, tm), _rows(3 * Dm, 0, tm), ws],
        out_specs=_rows(Dm, 0, tm), out_shape=jax.ShapeDtypeStruct((S, Dm), BF16),
        compiler_params=_params(("parallel",)), name=name)(mc, ms, ma, proj_g, wa)


def _merge_bwd(dmerged, mc, ms, ma, proj_g, wbr_t, name):
    S, Dm = mc.shape[0], proj_g.shape[1] // 3
    tm = min(256, S)

    def body(d_ref, mc_ref, ms_ref, ma_ref, g_ref, w_ref, dg_ref, dy_ref, dmc_ref, dms_ref, dma_ref):
        ys = _branches((mc_ref, ms_ref, ma_ref), w_ref)
        d = d_ref[...]
        for b, (o_ref, (lo, hi)) in enumerate(zip((dmc_ref, dms_ref, dma_ref), _BR)):
            g = _sigmoid(g_ref[:, b * Dm:(b + 1) * Dm])
            dg_ref[:, b * Dm:(b + 1) * Dm] = (d * ys[b] * g * (1.0 - g)).astype(BF16)
            dyb = (g * d).astype(BF16)
            dy_ref[:, b * Dm:(b + 1) * Dm] = dyb
            o_ref[...] = jnp.dot(dyb, w_ref[:, lo:hi], preferred_element_type=F32)

    wa, ws = _op(wbr_t, (Dm, Dm), lambda i: (0, 0))
    return pl.pallas_call(
        body, grid=(S // tm,),
        in_specs=[_rows(Dm, 0, tm), _rows(D_CONV, 0, tm), _rows(D_SSM, 0, tm), _rows(D_ATTN, 0, tm),
                  _rows(3 * Dm, 0, tm), ws],
        out_specs=[_rows(3 * Dm, 0, tm), _rows(3 * Dm, 0, tm), _rows(D_CONV, 0, tm), _rows(D_SSM, 0, tm),
                   _rows(D_ATTN, 0, tm)],
        out_shape=[jax.ShapeDtypeStruct((S, 3 * Dm), BF16), jax.ShapeDtypeStruct((S, 3 * Dm), BF16),
                   jax.ShapeDtypeStruct((S, D_CONV), F32), jax.ShapeDtypeStruct((S, D_SSM), F32),
                   jax.ShapeDtypeStruct((S, D_ATTN), F32)],
        compiler_params=_params(("parallel",)), name=name)(dmerged, mc, ms, ma, proj_g, wa)


def _loss_head(y, target):
    S, Dm = y.shape
    tm = min(512, S)

    def body(y_ref, t_ref, dy_ref, l_ref):
        @pl.when(pl.program_id(0) == 0)
        def _():
            l_ref[...] = jnp.zeros_like(l_ref)

        e = y_ref[...] - t_ref[...]
        dy_ref[...] = e * (1.0 / Dm)
        l_ref[...] += jnp.broadcast_to(0.5 * jnp.sum(jnp.sum(e * e, axis=1, keepdims=True) * (1.0 / Dm)), (1, 128))

    return pl.pallas_call(
        body, grid=(S // tm,), in_specs=[_rows(Dm, 0, tm), _rows(Dm, 0, tm)],
        out_specs=[_rows(Dm, 0, tm), _whole((1, 128))],
        out_shape=[jax.ShapeDtypeStruct((S, Dm), F32), jax.ShapeDtypeStruct((1, 128), F32)],
        compiler_params=_params(("arbitrary",)), name="loss_head")(y, target)


def _view2d(shape):
    n = math.prod(shape)
    if shape[-1] % 128 == 0:
        return (n // shape[-1], shape[-1])
    if n >= (1 << 16) and len(shape) == 3:
        return (shape[0] * shape[1], shape[2])
    if n % 128 == 0:
        return (n // 128, 128)
    return (1, n)


def _adamw(w, g, m, v, name):
    shape = w.shape
    R, C = _view2d(shape)
    tm = R
    for cand in (512, 352, 256):
        if R > cand and R % cand == 0:
            tm = cand
            break
    c1 = 1.0 - ADAM_B1 ** ADAM_STEP
    c2 = 1.0 - ADAM_B2 ** ADAM_STEP

    def body(w_ref, g_ref, m_ref, v_ref, d_ref, nm_ref, nv_ref):
        g = g_ref[...]
        nm = ADAM_B1 * m_ref[...] + (1.0 - ADAM_B1) * g
        nv = ADAM_B2 * v_ref[...] + (1.0 - ADAM_B2) * (g * g)
        d_ref[...] = -ADAM_LR * ((nm / c1) / (jnp.sqrt(nv / c2) + ADAM_EPS) + ADAM_WD * w_ref[...])
        nm_ref[...] = nm
        nv_ref[...] = nv

    blk = _rows(C, 0, tm)
    outs = pl.pallas_call(
        body, grid=(R // tm,), in_specs=[blk] * 4, out_specs=[blk] * 3,
        out_shape=[jax.ShapeDtypeStruct((R, C), F32)] * 3,
        compiler_params=_params(("parallel",)), name=name)(*[t.reshape(R, C) for t in (w, g, m, v)])
    return [o.reshape(shape) for o in outs]


def _coords():
    return lax.axis_index("x"), lax.axis_index("y"), lax.axis_index("c")


def _other_chips(x, y):
    return [((1 - x, y), 2 * (1 - x) + y), ((x, 1 - y), 2 * x + 1 - y), ((1 - x, 1 - y), 2 * (1 - x) + 1 - y)]


def _exchange(build, n_in, out_shapes, n_sem, aliases, name):
    def body(*refs):
        ins, outs = refs[:n_in], refs[n_in:n_in + len(out_shapes)]
        send_sems, recv_sems, local_sems = refs[n_in + len(out_shapes):]
        sends, recvs, locals_ = build(ins, outs, send_sems, recv_sems, local_sems)
        for cp in locals_:
            cp.start()
        for cp in sends:
            cp.start()
        for cp in recvs:
            cp.wait_recv()
        for cp in sends:
            cp.wait_send()
        for cp in locals_:
            cp.wait()

    hbm = pl.BlockSpec(memory_space=pltpu.HBM)
    return pl.pallas_call(
        body, in_specs=[hbm] * n_in, out_specs=[hbm] * len(out_shapes), out_shape=out_shapes,
        scratch_shapes=[pltpu.SemaphoreType.DMA((n_sem,)), pltpu.SemaphoreType.DMA((n_sem,)),
                        pltpu.SemaphoreType.DMA((n_sem,))],
        input_output_aliases=aliases,
        compiler_params=pltpu.CompilerParams(has_side_effects=True), name=name)


def _remote(src, dst, send_sems, recv_sems, k, dev):
    return pltpu.make_async_remote_copy(src_ref=src, dst_ref=dst, send_sem=send_sems.at[k], recv_sem=recv_sems.at[k],
                                        device_id=dev, device_id_type=MESH)


def _gather_weights(shards):
    n = len(shards)
    fulls = [jax.ShapeDtypeStruct((s.shape[0], N_CHIPS * s.shape[1], s.shape[2]), s.dtype) for s in shards]

    def half(ref, chip, c, rp):
        return ref.at[:, pl.ds(pl.multiple_of(chip * rp + c * (rp // 2), 16), rp // 2), :]

    def step1(ins, outs, send_sems, recv_sems, local_sems):
        x, y, c = _coords()
        me = 2 * x + y
        sends, recvs = [], []
        for a in range(n):
            rp = ins[a].shape[1]
            src = ins[a].at[:, pl.ds(pl.multiple_of(c * (rp // 2), 16), rp // 2), :]
            for r, ((px, py), chip) in enumerate(_other_chips(x, y)):
                k = 3 * a + r
                sends.append(_remote(src, half(outs[a], me, c, rp), send_sems, recv_sems, k, (px, py, c)))
                recvs.append(_remote(src, half(outs[a], chip, c, rp), send_sems, recv_sems, k, (px, py, c)))
        return sends, recvs, []

    cx, cy, _ = _coords()
    placed = [lax.dynamic_update_slice(lax.empty(f.shape, f.dtype), s, (0, (2 * cx + cy) * s.shape[1], 0))
              for f, s in zip(fulls, shards)]
    got = _exchange(step1, 2 * n, fulls, 3 * n, {n + a: a for a in range(n)}, "gather_weights_ici")(*shards, *placed)

    def step2(ins, outs, send_sems, recv_sems, local_sems):
        x, y, c = _coords()
        sends, recvs = [], []
        for a in range(n):
            rp = ins[a].shape[1] // N_CHIPS
            for r, (_, chip) in enumerate(_other_chips(x, y)):
                k = 3 * a + r
                mine, theirs = half(outs[a], chip, c, rp), half(outs[a], chip, 1 - c, rp)
                sends.append(_remote(mine, mine, send_sems, recv_sems, k, (x, y, 1 - c)))
                recvs.append(_remote(theirs, theirs, send_sems, recv_sems, k, (x, y, 1 - c)))
        return sends, recvs, []

    return _exchange(step2, n, fulls, 3 * n, {a: a for a in range(n)}, "gather_weights_d2d")(*got)


def _pair_sum(a, b, c_idx, half_rows, out_dtype, name):
    n4, rp, W = a.shape
    tr = half_rows // 2
    nblk = half_rows // tr

    def body(c_ref, a_ref, b_ref, o_ref):
        o_ref[...] = (a_ref[...] + b_ref[...]).astype(o_ref.dtype)

    return pl.pallas_call(
        body,
        grid_spec=pltpu.PrefetchScalarGridSpec(
            num_scalar_prefetch=1, grid=(nblk,),
            in_specs=[pl.BlockSpec((n4, tr, W), lambda i, c: (0, c[0] * nblk + i, 0)),
                      pl.BlockSpec((n4, tr, W), lambda i, c: (0, i, 0))],
            out_specs=pl.BlockSpec((n4, tr, W), lambda i, c: (0, i, 0))),
        out_shape=jax.ShapeDtypeStruct((n4, half_rows, W), out_dtype),
        compiler_params=_params(("parallel",)), name=name)(c_idx, a, b)


def _sum4(m, c_idx, name):
    _, R, W = m.shape
    tr = R // 2

    def body(c_ref, m_ref, o_ref):
        v = m_ref[...].astype(F32)
        o_ref[...] = ((v[0] + v[1]) + v[2]) + v[3]

    return pl.pallas_call(
        body,
        grid_spec=pltpu.PrefetchScalarGridSpec(
            num_scalar_prefetch=1, grid=(R // tr,),
            in_specs=[pl.BlockSpec((N_CHIPS, tr, W), lambda i, c: (0, i, 0))],
            out_specs=pl.BlockSpec((None, tr, W), lambda i, c: (c[0], i, 0))),
        out_shape=jax.ShapeDtypeStruct((2, R, W), F32),
        compiler_params=_params(("parallel",)), name=name)(c_idx, m)


def _reduce_scatter(grads):
    n = len(grads)
    x, y, c = _coords()
    g4 = [g.reshape(N_CHIPS, g.shape[0] // N_CHIPS, g.shape[1]) for g in grads]
    hr = [g.shape[1] // 2 for g in g4]

    def step1(ins, outs, send_sems, recv_sems, local_sems):
        x, y, c = _coords()
        sends, recvs = [], []
        for a in range(n):
            src = ins[a].at[:, pl.ds(pl.multiple_of((1 - c) * hr[a], 8), hr[a]), :]
            sends.append(_remote(src, outs[a], send_sems, recv_sems, a, (x, y, 1 - c)))
            recvs.append(_remote(src, outs[a], send_sems, recv_sems, a, (x, y, 1 - c)))
        return sends, recvs, []

    lands = [jax.ShapeDtypeStruct((N_CHIPS, hr[a], g4[a].shape[2]), F32) for a in range(n)]
    got = _exchange(step1, n, lands, n, {}, "reduce_d2d")(*g4)
    c_idx = jnp.reshape(c, (1,)).astype(jnp.int32)
    q = [_pair_sum(g4[a], got[a], c_idx, hr[a], BF16, "reduce_pair_sum") for a in range(n)]

    def step2(ins, outs, send_sems, recv_sems, local_sems):
        x, y, c = _coords()
        me = 2 * x + y
        sends, recvs, locals_ = [], [], []
        for a in range(n):
            locals_.append(pltpu.make_async_copy(ins[a].at[me], outs[a].at[me], local_sems.at[a]))
            for r, ((px, py), chip) in enumerate(_other_chips(x, y)):
                k = 3 * a + r
                sends.append(_remote(ins[a].at[chip], outs[a].at[me], send_sems, recv_sems, k, (px, py, c)))
                recvs.append(_remote(ins[a].at[chip], outs[a].at[chip], send_sems, recv_sems, k, (px, py, c)))
        return sends, recvs, locals_

    lands = [jax.ShapeDtypeStruct(t.shape, BF16) for t in q]
    got = _exchange(step2, n, lands, 3 * n, {}, "reduce_ici")(*q)
    r = [_sum4(got[a], c_idx, "reduce_sum4") for a in range(n)]

    def step3(ins, outs, send_sems, recv_sems, local_sems):
        x, y, c = _coords()
        sends, recvs = [], []
        for a in range(n):
            sends.append(_remote(outs[a].at[c], outs[a].at[c], send_sems, recv_sems, a, (x, y, 1 - c)))
            recvs.append(_remote(outs[a].at[c], outs[a].at[1 - c], send_sems, recv_sems, a, (x, y, 1 - c)))
        return sends, recvs, []

    lands = [jax.ShapeDtypeStruct(t.shape, F32) for t in r]
    got = _exchange(step3, n, lands, n, {a: a for a in range(n)}, "reduce_share")(*r)
    return [t.reshape(2 * t.shape[1], t.shape[2]) for t in got]


def _all_reduce_small(buf, name):
    R = buf.shape[0]

    def step(ins, outs, send_sems, recv_sems, local_sems):
        x, y, c = _coords()
        me = 4 * x + 2 * y + c
        sends, recvs = [], []
        k = 0
        for fx in range(2):
            for fy in range(2):
                for fc in range(2):
                    if fx + fy + fc == 0:
                        continue
                    px, py, pc = x ^ fx, y ^ fy, c ^ fc
                    sends.append(_remote(ins[0], outs[0].at[me], send_sems, recv_sems, k, (px, py, pc)))
                    recvs.append(_remote(ins[0], outs[0].at[4 * px + 2 * py + pc], send_sems, recv_sems, k,
                                         (px, py, pc)))
                    k += 1
        return sends, recvs, [pltpu.make_async_copy(ins[0], outs[0].at[me], local_sems.at[0])]

    got = _exchange(step, 1, [jax.ShapeDtypeStruct((N_DEV, R, 128), F32)], N_DEV - 1, {}, name + "_gather")(buf)[0]

    def body(m_ref, o_ref):
        acc = m_ref[0]
        for d in range(1, N_DEV):
            acc = acc + m_ref[d]
        o_ref[...] = acc

    tr = max([t for t in range(8, min(R, 1024) + 1, 8) if R % t == 0], default=R)
    return pl.pallas_call(
        body, grid=(R // tr,), in_specs=[pl.BlockSpec((N_DEV, tr, 128), lambda i: (0, i, 0))],
        out_specs=pl.BlockSpec((tr, 128), lambda i: (i, 0)), out_shape=jax.ShapeDtypeStruct((R, 128), F32),
        compiler_params=_params(("parallel",)), name=name + "_sum")(got)


def _ssm_layouts(p, L):
    G = L * N_GROUPS
    lr = p["ssm_a_re"].reshape(G, 1, SSM_STATE)
    li = p["ssm_a_im"].reshape(G, 1, SSM_STATE)
    ldt = p["ssm_log_dt"].reshape(G, 1, 1)
    br = jnp.swapaxes(p["ssm_b_re"], 2, 3).reshape(G, SSM_GROUP, SSM_STATE)
    bi = jnp.swapaxes(p["ssm_b_im"], 2, 3).reshape(G, SSM_GROUP, SSM_STATE)
    are, aim, bre, bim = _disc_fwd(lr, li, ldt, br, bi)
    eye = jnp.eye(N_GROUPS, dtype=F32)
    abar = jnp.concatenate([are.reshape(L, 8, 128), aim.reshape(L, 8, 128)], axis=1)

    def b_blk(t):
        return jnp.einsum("lgcp,gh->lgchp", t.reshape(L, N_GROUPS, SSM_GROUP, SSM_STATE), eye).reshape(L, 256, 1024)

    def c_blk(t):
        return jnp.einsum("lgcp,gh->lgphc", t, eye).reshape(L, 1024, 256)

    bfull = jnp.concatenate([b_blk(bre), b_blk(bim)], axis=2).astype(BF16)
    cfull = jnp.concatenate([c_blk(p["ssm_c_re"]), -c_blk(p["ssm_c_im"])], axis=1).astype(BF16)
    return (lr, li, ldt, br, bi), abar, bfull, cfull


def _local_step(x, target, W, p, L):
    S, Dm = x.shape
    dff = W["wg1"].shape[1]
    alpha = (2.0 * L) ** 0.25
    disc_in, abar, bfull, cfull = _ssm_layouts(p, L)
    row = lambda t, l: t[l][None]
    saved = []
    h, hb = x, x.astype(BF16)
    for l in range(L):
        sv = {"x0": hb}
        a1, b1, h1 = _ffn_up(hb, (W["wg1"], l), (W["wu1"], l), dff, "ffn_up")
        x1, x1b, xh1, rs1 = _mm_ln(h1, (W["wd1"], l), h, row(p["ln1_g"], l), row(p["ln1_b"], l), 0.5, alpha,
                                   "ffn_down_ln", 1408)
        sv.update(a1=a1, b1=b1, h1=h1, x1=x1b, xh1=xh1, rs1=rs1)
        proj_a = _mm([(x1b, (W["win"], l))], "nt", S, D_A, Dm, [F32], name="proj_a", tm=1024, tn=896, tk=1024)
        proj_g = _mm([(x1b, (W["win"], l))], "nt", S, 3 * Dm, Dm, [F32], name="proj_g", tm=2048, tn=256, tk=1024,
                     b_off=D_A)
        mc = _conv_fwd(proj_a, p["conv_w"][l], row(p["conv_b"], l), "conv_fwd")
        bu = _mm([(proj_a, (bfull, l))], "nn", S, 2048, D_SSM, [F32], name="ssm_bu", tm=1024, tn=1024, tk=256,
                 a_off=3 * D_CONV)
        xs = _scan_fwd(bu.reshape(S, 16, 128), abar[l], "scan_fwd").reshape(S, 2048)
        y1, ms = _ssm_out(xs, proj_a, cfull[l], row(p["ssm_d"], l), (W["wglu"], l), "ssm_out")
        sinks = p["attn_sinks"][l][None]
        ma = _attn_fwd(proj_a, sinks, "attn_fwd")
        merged = _merge_fwd(mc, ms, ma, proj_g, (W["wbr"], l), "merge_fwd")
        x2, x2b, xh2, rs2 = _mm_ln(merged, (W["wout"], l), x1, row(p["ln2_g"], l), row(p["ln2_b"], l), 1.0, alpha,
                                   "mix_out_ln", 1024)
        sv.update(proj_a=proj_a, proj_g=proj_g, mc=mc, ms=ms, ma=ma, xs=xs, y1=y1, merged=merged, x2=x2b, xh2=xh2,
                  rs2=rs2)
        a2, b2, h2 = _ffn_up(x2b, (W["wg2"], l), (W["wu2"], l), dff, "ffn_up")
        h, hb, xh3, rs3 = _mm_ln(h2, (W["wd2"], l), x2, row(p["ln3_g"], l), row(p["ln3_b"], l), 0.5, alpha,
                                 "ffn_down_ln", 1408)
        sv.update(a2=a2, b2=b2, h2=h2, xh3=xh3, rs3=rs3)
        saved.append(sv)

    dy, loss = _loss_head(h, target)
    big = [None] * L
    small = {k: [None] * L for k in ("ln1_g", "ln1_b", "ln2_g", "ln2_b", "ln3_g", "ln3_b", "conv_w", "conv_b", "ssm_d",
                                     "attn_sinks", "wglu", "dabar", "dbfull", "dcfull")}

    def ffn_bwd(dy_out, x_in, a, b, hh, xh, rs, g, wg, wu, wd, l):
        dres, df, dg, db = _ln_bwd(dy_out, xh, rs, g, 0.5, alpha, "ln_bwd")
        da, dbb = _ffn_dh(df, (wd, l), a, b, "ffn_dh")
        dx = _mm([(da, (wg, l)), (dbb, (wu, l))], "nn", S, Dm, dff, [F32], name="ffn_dx", tm=1024, tn=1024,
                 tk=dff // 2, add=dres)
        tn_kw = dict(tm=1408, tn=1024, tk=512)
        dwg = _mm([(da, x_in)], "tn", dff, Dm, S, [F32], name="ffn_dw_up", **tn_kw)
        dwu = _mm([(dbb, x_in)], "tn", dff, Dm, S, [F32], name="ffn_dw_up", **tn_kw)
        dwd = _mm([(hh, df)], "tn", dff, Dm, S, [F32], name="ffn_dw_down", **tn_kw)
        return dx, dwg, dwu, dwd, dg, db

    for l in reversed(range(L)):
        sv = saved[l]
        dx2, dwg2, dwu2, dwd2, small["ln3_g"][l], small["ln3_b"][l] = ffn_bwd(
            dy, sv["x2"], sv["a2"], sv["b2"], sv["h2"], sv["xh3"], sv["rs3"], row(p["ln3_g"], l), W["wg2"], W["wu2"],
            W["wd2"], l)
        dres2, dmix, small["ln2_g"][l], small["ln2_b"][l] = _ln_bwd(dx2, sv["xh2"], sv["rs2"], row(p["ln2_g"], l), 1.0,
                                                                   alpha, "ln_bwd")
        dwout = _mm([(sv["merged"], dmix)], "tn", Dm, Dm, S, [F32], name="dw_out", tm=512, tn=1024, tk=512)
        dmerged = _mm([(dmix, (W["wout"], l))], "nt", S, Dm, Dm, [F32], name="d_merged", tm=1024, tn=1024, tk=512)
        dgates, dyb, dmc, dms, dma = _merge_bwd(dmerged, sv["mc"], sv["ms"], sv["ma"], sv["proj_g"], (W["wbr"], l),
                                                "merge_bwd")
        dwbr = jnp.concatenate([
            _mm([(dyb, m)], "tn", Dm, hi - lo, S, [F32], name="dw_br", tm=512, tn=512, tk=512, a_off=b * Dm)
            for b, ((lo, hi), m) in enumerate(zip(_BR, (sv["mc"], sv["ms"], sv["ma"])))], axis=1)
        proj_a = sv["proj_a"]
        d_conv, small["conv_w"][l], small["conv_b"][l] = _conv_bwd(proj_a, dmc, p["conv_w"][l], row(p["conv_b"], l),
                                                                  "conv_bwd")
        dy1, y2, dgl, dxs, du_skip, small["ssm_d"][l] = _ssm_out_bwd(dms, sv["y1"], proj_a, cfull[l],
                                                                    row(p["ssm_d"], l), (W["wglu"], l), "ssm_out_bwd")
        small["wglu"][l] = _mm([(y2, dgl)], "tn", D_SSM, D_SSM, S, [F32], name="dw_glu", tk=512)
        small["dcfull"][l] = _mm([(sv["xs"], dy1)], "tn", 2048, D_SSM, S, [F32], name="d_cfull", tm=1024, tk=512)
        lam, small["dabar"][l] = _scan_bwd(dxs.reshape(S, 16, 128), sv["xs"].reshape(S, 16, 128), abar[l], "scan_bwd")
        lam = lam.reshape(S, 2048)
        du = _mm([(lam, (bfull, l))], "nt", S, D_SSM, 2048, [BF16], name="ssm_du", tm=1024, tk=512, add=du_skip)
        small["dbfull"][l] = _mm([(proj_a, lam)], "tn", D_SSM, 2048, S, [F32], name="d_bfull", tm=256, tn=1024,
                                 tk=512, a_off=3 * D_CONV)
        dq, dk, dv, dsk = _attn_bwd(proj_a, dma, p["attn_sinks"][l][None], "attn_bwd")
        small["attn_sinks"][l] = dsk[:, 0]
        dproj = jnp.concatenate([d_conv, du, dq, dk, dv, dgates], axis=1)
        dx1 = _mm([(dproj, (W["win"], l))], "nn", S, Dm, D_A + 3 * Dm, [F32], name="d_x1", tm=1024, tn=1024,
                  tk=(D_A + 3 * Dm) // 2, add=dres2)
        dwin = _mm([(dproj, sv["x1"])], "tn", D_A + 3 * Dm, Dm, S, [F32], name="dw_in", tm=2432, tn=1024, tk=512)
        dx0, dwg1, dwu1, dwd1, small["ln1_g"][l], small["ln1_b"][l] = ffn_bwd(
            dx1, sv["x0"], sv["a1"], sv["b1"], sv["h1"], sv["xh1"], sv["rs1"], row(p["ln1_g"], l), W["wg1"], W["wu1"],
            W["wd1"], l)
        big[l] = dict(wg1=dwg1, wu1=dwu1, wd1=dwd1, win=dwin, wbr=dwbr, wout=dwout, wg2=dwg2, wu2=dwu2, wd2=dwd2)
        dy = dx0

    small = {k: jnp.stack(v) for k, v in small.items()}
    eye = jnp.eye(N_GROUPS, dtype=F32)
    dabar = small.pop("dabar")
    dbf = small.pop("dbfull").reshape(L, N_GROUPS, SSM_GROUP, 2, N_GROUPS, SSM_STATE)
    dbbar = jnp.einsum("lgcrhp,gh->rlgcp", dbf, eye).reshape(2, L * N_GROUPS, SSM_GROUP, SSM_STATE)
    dcf = small.pop("dcfull").reshape(L, 2, N_GROUPS, SSM_STATE, N_GROUPS, SSM_GROUP)
    dc = jnp.einsum("lrgphc,gh->rlgcp", dcf, eye)
    G = L * N_GROUPS
    cts = (dabar[:, 0:8].reshape(G, 1, SSM_STATE), dabar[:, 8:16].reshape(G, 1, SSM_STATE), dbbar[0], dbbar[1])
    dlr, dli, dldt, dbr, dbi = _disc_bwd(*disc_in, cts)
    shp_b = (L, N_GROUPS, SSM_GROUP, SSM_STATE)
    small.update(
        ssm_a_re=dlr.reshape(L, N_GROUPS, SSM_STATE), ssm_a_im=dli.reshape(L, N_GROUPS, SSM_STATE),
        ssm_log_dt=dldt.reshape(L, N_GROUPS), ssm_b_re=jnp.swapaxes(dbr.reshape(shp_b), 2, 3),
        ssm_b_im=jnp.swapaxes(dbi.reshape(shp_b), 2, 3), ssm_c_re=dc[0], ssm_c_im=-dc[1],
        ln1_g=small["ln1_g"][:, 0], ln1_b=small["ln1_b"][:, 0], ln2_g=small["ln2_g"][:, 0],
        ln2_b=small["ln2_b"][:, 0], ln3_g=small["ln3_g"][:, 0], ln3_b=small["ln3_b"][:, 0],
        conv_b=small["conv_b"][:, 0], ssm_d=small["ssm_d"][:, 0])
    return loss, dy, big, small


_SMALL_ORDER = ("ln1_g", "ln1_b", "ln2_g", "ln2_b", "ln3_g", "ln3_b", "conv_w", "conv_b", "ssm_a_re", "ssm_a_im",
                "ssm_log_dt", "ssm_b_re", "ssm_b_im", "ssm_c_re", "ssm_c_im", "ssm_d", "attn_sinks", "wglu")
_BIG_ORDER = ("wg1", "wu1", "wd1", "win", "wbr", "wout", "wg2", "wu2", "wd2")
_WEIGHTS = ("ffn1_w_gate", "ffn1_w_up", "ffn1_w_down", "ln1_g", "ln1_b", "w_in", "conv_w", "conv_b", "ssm_a_re",
            "ssm_a_im", "ssm_log_dt", "ssm_b_re", "ssm_b_im", "ssm_c_re", "ssm_c_im", "ssm_d", "ssm_w_glu",
            "attn_sinks", "w_br_conv", "w_br_ssm", "w_br_attn", "w_out", "ln2_g", "ln2_b", "ffn2_w_gate",
            "ffn2_w_up", "ffn2_w_down", "ln3_g", "ln3_b")


def _weight_shards(w):
    t = lambda a: jnp.swapaxes(a, 1, 2).astype(BF16)
    wbr = jnp.concatenate([t(w["w_br_conv"]), t(w["w_br_ssm"]), t(w["w_br_attn"])], axis=2)
    return dict(wg1=t(w["ffn1_w_gate"]), wu1=t(w["ffn1_w_up"]), wd1=w["ffn1_w_down"].astype(BF16), win=t(w["w_in"]),
                wbr=wbr, wout=w["w_out"].astype(BF16), wg2=t(w["ffn2_w_gate"]), wu2=t(w["ffn2_w_up"]),
                wd2=w["ffn2_w_down"].astype(BF16))


def _gather_small_shards(conv_w, w_glu, chip):
    L = conv_w.shape[0]
    part = jnp.concatenate([conv_w.reshape(1, -1), w_glu.reshape(1, -1)], axis=1)
    n = part.shape[1]
    slots = lax.dynamic_update_slice(jnp.zeros((N_CHIPS, n), F32), part, (chip, 0))
    got = (_all_reduce_small(slots.reshape(-1, 128), "param_gather") * 0.5).reshape(N_CHIPS, n)
    cw = jnp.transpose(got[:, :L * 3 * 64].reshape(N_CHIPS, L, 3, 64), (1, 2, 0, 3)).reshape(L, 3, N_CHIPS * 64)
    wg = jnp.transpose(got[:, L * 3 * 64:].reshape(N_CHIPS, L, 64, D_SSM), (1, 0, 2, 3)).reshape(L, N_CHIPS * 64, D_SSM)
    return cw, wg.astype(BF16)


def kernel(x, ffn1_w_gate, ffn1_w_up, ffn1_w_down, ln1_g, ln1_b, w_in, conv_w, conv_b, ssm_a_re, ssm_a_im, ssm_log_dt, ssm_b_re, ssm_b_im, ssm_c_re, ssm_c_im, ssm_d, ssm_w_glu, attn_sinks, w_br_conv, w_br_ssm, w_br_attn, w_out, ln2_g, ln2_b, ffn2_w_gate, ffn2_w_up, ffn2_w_down, ln3_g, ln3_b, loss_target, m_ffn1_w_gate, m_ffn1_w_up, m_ffn1_w_down, m_ln1_g, m_ln1_b, m_w_in, m_conv_w, m_conv_b, m_ssm_a_re, m_ssm_a_im, m_ssm_log_dt, m_ssm_b_re, m_ssm_b_im, m_ssm_c_re, m_ssm_c_im, m_ssm_d, m_ssm_w_glu, m_attn_sinks, m_w_br_conv, m_w_br_ssm, m_w_br_attn, m_w_out, m_ln2_g, m_ln2_b, m_ffn2_w_gate, m_ffn2_w_up, m_ffn2_w_down, m_ln3_g, m_ln3_b, v_ffn1_w_gate, v_ffn1_w_up, v_ffn1_w_down, v_ln1_g, v_ln1_b, v_w_in, v_conv_w, v_conv_b, v_ssm_a_re, v_ssm_a_im, v_ssm_log_dt, v_ssm_b_re, v_ssm_b_im, v_ssm_c_re, v_ssm_c_im, v_ssm_d, v_ssm_w_glu, v_attn_sinks, v_w_br_conv, v_w_br_ssm, v_w_br_attn, v_w_out, v_ln2_g, v_ln2_b, v_ffn2_w_gate, v_ffn2_w_up, v_ffn2_w_down, v_ln3_g, v_ln3_b):
    args = dict(locals())
    w = {k: args[k] for k in _WEIGHTS}
    L = ln1_g.shape[0]
    cx, cy, cc = _coords()
    chip = 2 * cx + cy

    shards = _weight_shards(w)
    names = list(shards)
    full = dict(zip(names, _gather_weights([shards[k] for k in names])))
    p = {k: w[k] for k in ("ln1_g", "ln1_b", "ln2_g", "ln2_b", "ln3_g", "ln3_b", "conv_b", "ssm_a_re", "ssm_a_im",
                           "ssm_log_dt", "ssm_b_re", "ssm_b_im", "ssm_c_re", "ssm_c_im", "ssm_d", "attn_sinks")}
    p["conv_w"], full["wglu"] = _gather_small_shards(conv_w, ssm_w_glu, chip)

    loss, grad_x, big, small = _local_step(x[0], loss_target[0], full, p, L)
    loss = lax.psum(loss[0, 0], ("x", "y", "c"))

    sizes = [math.prod(small[k].shape) for k in _SMALL_ORDER]
    pad = (-sum(sizes)) % 1024
    flat = jnp.concatenate([small[k].reshape(-1) for k in _SMALL_ORDER] + [jnp.zeros((pad,), F32)])
    flat = _all_reduce_small(flat.reshape(-1, 128), "small_grads").reshape(-1)
    sm, off = {}, 0
    for k, n in zip(_SMALL_ORDER, sizes):
        sm[k] = flat[off:off + n].reshape(small[k].shape)
        off += n
    red = _reduce_scatter([big[l][k] for l in range(L) for k in _BIG_ORDER])
    red = {k: jnp.stack([red[l * len(_BIG_ORDER) + i] for l in range(L)]) for i, k in enumerate(_BIG_ORDER)}
    tr = lambda a: jnp.swapaxes(a, 1, 2)
    grads = dict(sm)
    grads.update(
        ffn1_w_gate=tr(red["wg1"]), ffn1_w_up=tr(red["wu1"]), ffn1_w_down=red["wd1"], w_in=tr(red["win"]),
        w_br_conv=tr(red["wbr"][:, :, _BR[0][0]:_BR[0][1]]), w_br_ssm=tr(red["wbr"][:, :, _BR[1][0]:_BR[1][1]]),
        w_br_attn=tr(red["wbr"][:, :, _BR[2][0]:_BR[2][1]]), w_out=red["wout"], ffn2_w_gate=tr(red["wg2"]),
        ffn2_w_up=tr(red["wu2"]), ffn2_w_down=red["wd2"],
        ssm_w_glu=lax.dynamic_slice_in_dim(sm["wglu"], chip * 64, 64, axis=1),
        conv_w=lax.dynamic_slice_in_dim(sm["conv_w"], chip * 64, 64, axis=2))

    outs = [[], [], [], []]
    for k in _WEIGHTS:
        d, nm, nv = _adamw(w[k], grads[k], args["m_" + k], args["v_" + k], "adamw")
        for lst, val in zip(outs, (grads[k], d, nm, nv)):
            lst.append(val)
    return (loss, grad_x[None], *outs[0], *outs[1], *outs[2], *outs[3])
```

```python
import functools
import math

import jax
import jax.numpy as jnp
from jax import lax
from jax.experimental import pallas as pl
from jax.experimental.pallas import tpu as pltpu

F32 = jnp.float32
BF16 = jnp.bfloat16

LN_EPS = 1e-5
D_CONV = 256
D_SSM = 256
N_GROUPS = 16
SSM_GROUP = 16
SSM_STATE = 64
N_Q_HEADS = 8
HEAD_DIM = 64
D_ATTN = 512
D_KV = 128
BLOCK = 128
D_A = 3 * D_CONV + D_SSM + D_ATTN + 2 * D_KV
ADAM_LR = 0.001
ADAM_B1 = 0.9
ADAM_B2 = 0.999
ADAM_EPS = 1e-08
ADAM_WD = 0.01
ADAM_STEP = 10

VMEM_LIMIT_BYTES = 56 * 1024 * 1024
MESH = pl.DeviceIdType.MESH
N_CHIPS = 4
N_DEV = 8


def _params(sem=None):
    return pltpu.CompilerParams(dimension_semantics=sem, vmem_limit_bytes=VMEM_LIMIT_BYTES)


def _op(op, block, imap):
    if isinstance(op, tuple):
        arr, l = op
        return arr, pl.BlockSpec((None,) + block, lambda *g: (l,) + imap(*g))
    return op, pl.BlockSpec(block, imap)


def _mm(pairs, mode, m, n, k, out_dtypes, *, name, tm=512, tn=512, tk=512, add=None, a_off=0, b_off=0):
    tm, tn, tk = min(tm, m), min(tn, n), min(tk, k)
    assert m % tm == 0 and n % tn == 0 and k % tk == 0 and b_off % tn == 0, (name, m, n, k, tm, tn, tk)
    nk, npair, jo = k // tk, len(pairs), b_off // tn
    if mode == "nn":
        ao = a_off // tk
        ab, ai, bb, bi = (tm, tk), (lambda i, j, kk: (i, kk + ao)), (tk, tn), (lambda i, j, kk: (kk, j))
        dims = (((1,), (0,)), ((), ()))
    elif mode == "nt":
        ao = a_off // tk
        ab, ai, bb, bi = (tm, tk), (lambda i, j, kk: (i, kk + ao)), (tn, tk), (lambda i, j, kk: (j + jo, kk))
        dims = (((1,), (1,)), ((), ()))
    else:
        ao = a_off // tm
        ab, ai, bb, bi = (tk, tm), (lambda i, j, kk: (kk, i + ao)), (tk, tn), (lambda i, j, kk: (kk, j))
        dims = (((0,), (0,)), ((), ()))
    assert a_off % (tm if mode == "tn" else tk) == 0, (name, a_off)
    ops, specs = [], []
    for a, b in pairs:
        for o, blk, im in ((a, ab, ai), (b, bb, bi)):
            arr, sp = _op(o, blk, im)
            ops.append(arr)
            specs.append(sp)
    has_add = add is not None
    if has_add:
        ops.append(add)
        specs.append(pl.BlockSpec((tm, tn), lambda i, j, kk: (i, j)))
    nout = len(out_dtypes)

    def body(*refs):
        outs, acc = refs[2 * npair + has_add:2 * npair + has_add + nout], refs[-1]
        kk = pl.program_id(2)
        t = None
        for p in range(npair):
            d = lax.dot_general(refs[2 * p][...].astype(BF16), refs[2 * p + 1][...].astype(BF16), dims,
                                preferred_element_type=F32)
            t = d if t is None else t + d

        def finish(r):
            if has_add:
                r = r + refs[2 * npair][...]
            for o in outs:
                o[...] = r.astype(o.dtype)

        if nk == 1:
            finish(t)
            return

        @pl.when(kk == 0)
        def _():
            acc[...] = jnp.zeros_like(acc)

        acc[...] += t

        @pl.when(kk == nk - 1)
        def _():
            finish(acc[...])

    res = pl.pallas_call(
        body, grid=(m // tm, n // tn, nk), in_specs=specs,
        out_specs=[pl.BlockSpec((tm, tn), lambda i, j, kk: (i, j))] * nout,
        out_shape=[jax.ShapeDtypeStruct((m, n), dt) for dt in out_dtypes],
        scratch_shapes=[pltpu.VMEM((tm, tn), F32)],
        compiler_params=_params(("parallel", "parallel", "arbitrary")), name=name)(*ops)
    return res[0] if nout == 1 else res


def _rows(width, col=0, tm=None):
    return pl.BlockSpec((tm, width), lambda i: (i, col))


def _whole(shape):
    nd = len(shape)
    return pl.BlockSpec(shape, lambda i: (0,) * nd)


def _sigmoid(x):
    return 0.5 * jnp.tanh(0.5 * x) + 0.5


def _mm_ln(a, w, x, g, b, s, alpha, name, tk):
    S, K = a.shape
    Dm = x.shape[1]
    tm, tk = min(512, S), min(tk, K)
    nk = K // tk

    def body(a_ref, w_ref, x_ref, g_ref, b_ref, y_ref, yb_ref, xh_ref, rs_ref, acc):
        kk = pl.program_id(1)
        d = jnp.dot(a_ref[...], w_ref[...], preferred_element_type=F32)

        if nk > 1:
            @pl.when(kk == 0)
            def _():
                acc[...] = d

            @pl.when((kk > 0) & (kk < nk - 1))
            def _():
                acc[...] += d

        @pl.when(kk == nk - 1)
        def _():
            z = alpha * x_ref[...] + s * (d if nk == 1 else acc[...] + d)
            mu = jnp.mean(z, axis=-1, keepdims=True)
            zc = z - mu
            var = jnp.mean(zc * zc, axis=-1, keepdims=True)
            rstd = lax.rsqrt(var + LN_EPS)
            xh = zc * rstd
            y = xh * g_ref[...] + b_ref[...]
            y_ref[...] = y
            yb_ref[...] = y.astype(BF16)
            xh_ref[...] = xh
            rs_ref[...] = rstd

    wa, ws = _op(w, (tk, Dm), lambda i, kk: (kk, 0))
    row = pl.BlockSpec((tm, Dm), lambda i, kk: (i, 0))
    vec = pl.BlockSpec((1, Dm), lambda i, kk: (0, 0))
    return pl.pallas_call(
        body, grid=(S // tm, nk),
        in_specs=[pl.BlockSpec((tm, tk), lambda i, kk: (i, kk)), ws, row, vec, vec],
        out_specs=[row, row, row, pl.BlockSpec((tm, 1), lambda i, kk: (i, 0))],
        out_shape=[jax.ShapeDtypeStruct((S, Dm), F32), jax.ShapeDtypeStruct((S, Dm), BF16),
                   jax.ShapeDtypeStruct((S, Dm), F32), jax.ShapeDtypeStruct((S, 1), F32)],
        scratch_shapes=[pltpu.VMEM((tm, Dm), F32)],
        compiler_params=_params(("parallel", "arbitrary")), name=name)(a, wa, x, g, b)


def _ln_bwd(dy, xh, rs, g, s, alpha, name):
    S, Dm = dy.shape
    tm = min(256, S)

    def body(dy_ref, xh_ref, rs_ref, g_ref, dres_ref, dbr_ref, dg_ref, db_ref):
        @pl.when(pl.program_id(0) == 0)
        def _():
            dg_ref[...] = jnp.zeros_like(dg_ref)
            db_ref[...] = jnp.zeros_like(db_ref)

        dy, xh = dy_ref[...], xh_ref[...]
        dyg = dy * g_ref[...]
        m1 = jnp.mean(dyg, axis=-1, keepdims=True)
        m2 = jnp.mean(dyg * xh, axis=-1, keepdims=True)
        dz = rs_ref[...] * (dyg - m1 - xh * m2)
        dres_ref[...] = alpha * dz
        dbr_ref[...] = (s * dz).astype(BF16)
        dg_ref[...] += jnp.sum(dy * xh, axis=0, keepdims=True)
        db_ref[...] += jnp.sum(dy, axis=0, keepdims=True)

    return pl.pallas_call(
        body, grid=(S // tm,),
        in_specs=[_rows(Dm, tm=tm), _rows(Dm, tm=tm), _rows(1, tm=tm), _whole((1, Dm))],
        out_specs=[_rows(Dm, tm=tm), _rows(Dm, tm=tm), _whole((1, Dm)), _whole((1, Dm))],
        out_shape=[jax.ShapeDtypeStruct((S, Dm), F32), jax.ShapeDtypeStruct((S, Dm), BF16),
                   jax.ShapeDtypeStruct((1, Dm), F32), jax.ShapeDtypeStruct((1, Dm), F32)],
        compiler_params=_params(("arbitrary",)), name=name)(dy, xh, rs, g)


def _ffn_up(x, wg, wu, dff, name):
    S, Dm = x.shape
    tm, tn = min(512, S), dff // 2

    def body(x_ref, wg_ref, wu_ref, a_ref, b_ref, h_ref):
        xb = x_ref[...]
        dims = (((1,), (1,)), ((), ()))
        a = lax.dot_general(xb, wg_ref[...], dims, preferred_element_type=F32)
        b = lax.dot_general(xb, wu_ref[...], dims, preferred_element_type=F32)
        a_ref[...] = a.astype(BF16)
        b_ref[...] = b.astype(BF16)
        h_ref[...] = (a * _sigmoid(a) * b).astype(BF16)

    wga, wgs = _op(wg, (tn, Dm), lambda i, j: (j, 0))
    wua, wus = _op(wu, (tn, Dm), lambda i, j: (j, 0))
    ob = pl.BlockSpec((tm, tn), lambda i, j: (i, j))
    return pl.pallas_call(
        body, grid=(S // tm, dff // tn),
        in_specs=[pl.BlockSpec((tm, Dm), lambda i, j: (i, 0)), wgs, wus], out_specs=[ob, ob, ob],
        out_shape=[jax.ShapeDtypeStruct((S, dff), BF16)] * 3,
        compiler_params=_params(("parallel", "parallel")), name=name)(x, wga, wua)


def _ffn_dh(df, wd, a, b, name):
    S, Dm = df.shape
    dff = a.shape[1]
    tm, tn = min(512, S), dff // 2

    def body(df_ref, wd_ref, a_ref, b_ref, da_ref, db_ref):
        dh = lax.dot_general(df_ref[...], wd_ref[...], (((1,), (1,)), ((), ())), preferred_element_type=F32)
        a, b = a_ref[...].astype(F32), b_ref[...].astype(F32)
        sg = _sigmoid(a)
        da_ref[...] = (dh * b * (sg * (1.0 + a * (1.0 - sg)))).astype(BF16)
        db_ref[...] = (dh * (a * sg)).astype(BF16)

    wda, wds = _op(wd, (tn, Dm), lambda i, j: (j, 0))
    ob = pl.BlockSpec((tm, tn), lambda i, j: (i, j))
    return pl.pallas_call(
        body, grid=(S // tm, dff // tn),
        in_specs=[pl.BlockSpec((tm, Dm), lambda i, j: (i, 0)), wds, ob, ob], out_specs=[ob, ob],
        out_shape=[jax.ShapeDtypeStruct((S, dff), BF16), jax.ShapeDtypeStruct((S, dff), BF16)],
        compiler_params=_params(("parallel", "parallel")), name=name)(df, wda, a, b)


def _halo_prev(width, col, tm):
    return pl.BlockSpec((8, width), lambda i: (jnp.maximum(i * (tm // 8) - 1, 0), col))


def _halo_next(width, col, tm, S):
    return pl.BlockSpec((8, width), lambda i: (jnp.minimum((i + 1) * (tm // 8), S // 8 - 1), col))


def _shift_down(prev8, cur, n):
    ext = jnp.concatenate([prev8, cur], axis=0)
    return pltpu.roll(ext, n, axis=0)[8:]


def _shift_up(cur, next8, n):
    ext = jnp.concatenate([cur, next8], axis=0)
    return pltpu.roll(ext, ext.shape[0] - n, axis=0)[:cur.shape[0]]


def _conv_fwd(proj_a, conv_w, conv_b, name):
    S = proj_a.shape[0]
    tm = min(512, S)
    C = D_CONV

    def body(bg_ref, cg_ref, h_ref, cgp_ref, hp_ref, w_ref, cb_ref, out_ref):
        z = cg_ref[...] * h_ref[...]
        zp = jnp.where(pl.program_id(0) > 0, cgp_ref[...] * hp_ref[...], 0.0)
        w = w_ref[...]
        y = w[2:3] * z + w[1:2] * _shift_down(zp, z, 1) + w[0:1] * _shift_down(zp, z, 2) + cb_ref[...]
        out_ref[...] = (bg_ref[...] * y).astype(BF16)

    return pl.pallas_call(
        body, grid=(S // tm,),
        in_specs=[_rows(C, 0, tm), _rows(C, 1, tm), _rows(C, 2, tm), _halo_prev(C, 1, tm), _halo_prev(C, 2, tm),
                  _whole((3, C)), _whole((1, C))],
        out_specs=_rows(C, 0, tm), out_shape=jax.ShapeDtypeStruct((S, C), BF16),
        compiler_params=_params(("parallel",)), name=name)(proj_a, proj_a, proj_a, proj_a, proj_a, conv_w, conv_b)


def _conv_bwd(proj_a, dmc, conv_w, conv_b, name):
    S = proj_a.shape[0]
    tm = min(512, S)
    C = D_CONV
    nblk = S // tm

    def body(bg_ref, cg_ref, h_ref, cgp_ref, hp_ref, bgn_ref, d_ref, dn_ref, w_ref, cb_ref, out_ref, dw_ref,
             dcb_ref):
        i = pl.program_id(0)

        @pl.when(i == 0)
        def _():
            dw_ref[...] = jnp.zeros_like(dw_ref)
            dcb_ref[...] = jnp.zeros_like(dcb_ref)

        bg, cg, h, d = bg_ref[...], cg_ref[...], h_ref[...], d_ref[...]
        z = cg * h
        zp = jnp.where(i > 0, cgp_ref[...] * hp_ref[...], 0.0)
        w = w_ref[...]
        z1, z2 = _shift_down(zp, z, 1), _shift_down(zp, z, 2)
        y = w[2:3] * z + w[1:2] * z1 + w[0:1] * z2 + cb_ref[...]
        dy = d * bg
        dyn = jnp.where(i < nblk - 1, dn_ref[...] * bgn_ref[...], 0.0)
        dz = w[2:3] * dy + w[1:2] * _shift_up(dy, dyn, 1) + w[0:1] * _shift_up(dy, dyn, 2)
        out_ref[:, 0:C] = (d * y).astype(BF16)
        out_ref[:, C:2 * C] = (dz * h).astype(BF16)
        out_ref[:, 2 * C:3 * C] = (dz * cg).astype(BF16)
        dw_ref[0:1, :] += jnp.sum(dy * z2, axis=0, keepdims=True)
        dw_ref[1:2, :] += jnp.sum(dy * z1, axis=0, keepdims=True)
        dw_ref[2:3, :] += jnp.sum(dy * z, axis=0, keepdims=True)
        dcb_ref[...] += jnp.sum(dy, axis=0, keepdims=True)

    return pl.pallas_call(
        body, grid=(nblk,),
        in_specs=[_rows(C, 0, tm), _rows(C, 1, tm), _rows(C, 2, tm), _halo_prev(C, 1, tm), _halo_prev(C, 2, tm),
                  _halo_next(C, 0, tm, S), _rows(C, 0, tm), _halo_next(C, 0, tm, S), _whole((3, C)), _whole((1, C))],
        out_specs=[_rows(3 * C, 0, tm), _whole((3, C)), _whole((1, C))],
        out_shape=[jax.ShapeDtypeStruct((S, 3 * C), BF16), jax.ShapeDtypeStruct((3, C), F32),
                   jax.ShapeDtypeStruct((1, C), F32)],
        compiler_params=_params(("arbitrary",)), name=name)(
            proj_a, proj_a, proj_a, proj_a, proj_a, proj_a, dmc, dmc, conv_w, conv_b)


def _disc_math(lr, li, ldt, br, bi):
    dt = jnp.exp(ldt)
    mag = jnp.exp(lr * dt)
    ang = li * dt
    are = mag * jnp.cos(ang)
    aim = mag * jnp.sin(ang)
    nr = are - 1.0
    den = lr * lr + li * li
    cre = (nr * lr + aim * li) / den
    cim = (aim * lr - nr * li) / den
    return are, aim, cre * br - cim * bi, cre * bi + cim * br


def _disc_fwd(lr, li, ldt, br, bi):
    shapes = [lr.shape, lr.shape, br.shape, br.shape]

    def body(lr_ref, li_ref, ldt_ref, br_ref, bi_ref, *outs):
        for o, v in zip(outs, _disc_math(lr_ref[...], li_ref[...], ldt_ref[...], br_ref[...], bi_ref[...])):
            o[...] = v

    return pl.pallas_call(body, out_shape=[jax.ShapeDtypeStruct(s, F32) for s in shapes],
                          compiler_params=_params(), name="ssm_disc")(lr, li, ldt, br, bi)


def _disc_bwd(lr, li, ldt, br, bi, cts):
    shapes = [lr.shape, lr.shape, ldt.shape, br.shape, br.shape]

    def body(lr_ref, li_ref, ldt_ref, br_ref, bi_ref, c0, c1, c2, c3, *outs):
        _, vjp = jax.vjp(_disc_math, lr_ref[...], li_ref[...], ldt_ref[...], br_ref[...], bi_ref[...])
        for o, v in zip(outs, vjp((c0[...], c1[...], c2[...], c3[...]))):
            o[...] = v

    return pl.pallas_call(body, out_shape=[jax.ShapeDtypeStruct(s, F32) for s in shapes],
                          compiler_params=_params(), name="ssm_disc_bwd")(lr, li, ldt, br, bi, *cts)


def _scan_fwd(bu, abar, name):
    S = bu.shape[0]
    tb = min(256, S)

    def body(bu_ref, a_ref, xs_ref, st_ref):
        @pl.when(pl.program_id(0) == 0)
        def _():
            st_ref[...] = jnp.zeros_like(st_ref)

        ar, ai = a_ref[0:8, :], a_ref[8:16, :]

        def step(t, c):
            xr, xi = c
            nr = ar * xr - ai * xi + bu_ref[t, 0:8, :]
            ni = ar * xi + ai * xr + bu_ref[t, 8:16, :]
            xs_ref[t, 0:8, :] = nr
            xs_ref[t, 8:16, :] = ni
            return nr, ni

        xr, xi = lax.fori_loop(0, tb, step, (st_ref[0:8, :], st_ref[8:16, :]), unroll=8)
        st_ref[0:8, :] = xr
        st_ref[8:16, :] = xi

    blk = pl.BlockSpec((tb, 16, 128), lambda i: (i, 0, 0))
    return pl.pallas_call(
        body, grid=(S // tb,), in_specs=[blk, _whole((16, 128))], out_specs=blk,
        out_shape=jax.ShapeDtypeStruct((S, 16, 128), F32), scratch_shapes=[pltpu.VMEM((16, 128), F32)],
        compiler_params=_params(("arbitrary",)), name=name)(bu, abar)


def _scan_bwd(dxs, xs, abar, name):
    S = dxs.shape[0]
    tb = min(256, S)
    nblk = S // tb

    def body(d_ref, x_ref, xp_ref, a_ref, lam_ref, da_ref, st_ref):
        i = pl.program_id(0)

        @pl.when(i == 0)
        def _():
            st_ref[...] = jnp.zeros_like(st_ref)
            da_ref[...] = jnp.zeros_like(da_ref)

        ar, ai = a_ref[0:8, :], a_ref[8:16, :]

        def one(t, c, pr, pi):
            lr, li, gr, gi = c
            nr = d_ref[t, 0:8, :] + ar * lr + ai * li
            ni = d_ref[t, 8:16, :] - ai * lr + ar * li
            lam_ref[t, 0:8, :] = nr
            lam_ref[t, 8:16, :] = ni
            return nr, ni, gr + nr * pr + ni * pi, gi - nr * pi + ni * pr

        def step(s, c):
            t = tb - 1 - s
            return one(t, c, x_ref[t - 1, 0:8, :], x_ref[t - 1, 8:16, :])

        c = (st_ref[0:8, :], st_ref[8:16, :], jnp.zeros((8, 128), F32), jnp.zeros((8, 128), F32))
        c = lax.fori_loop(0, tb - 1, step, c, unroll=8)
        first = i == nblk - 1
        pr = jnp.where(first, 0.0, xp_ref[0, 0:8, :])
        pi = jnp.where(first, 0.0, xp_ref[0, 8:16, :])
        lr, li, gr, gi = one(0, c, pr, pi)
        st_ref[0:8, :] = lr
        st_ref[8:16, :] = li
        da_ref[0:8, :] += gr
        da_ref[8:16, :] += gi

    blk = pl.BlockSpec((tb, 16, 128), lambda i: (nblk - 1 - i, 0, 0))
    prev = pl.BlockSpec((1, 16, 128), lambda i: (jnp.maximum((nblk - 1 - i) * tb - 1, 0), 0, 0))
    return pl.pallas_call(
        body, grid=(nblk,), in_specs=[blk, blk, prev, _whole((16, 128))], out_specs=[blk, _whole((16, 128))],
        out_shape=[jax.ShapeDtypeStruct((S, 16, 128), F32), jax.ShapeDtypeStruct((16, 128), F32)],
        scratch_shapes=[pltpu.VMEM((16, 128), F32)],
        compiler_params=_params(("arbitrary",)), name=name)(dxs, xs, xs, abar)


def _gelu(x):
    return 0.5 * x * (1.0 + jnp.tanh(0.7978845608028654 * (x + 0.044715 * x * x * x)))


def _gelu_grad(x):
    t = jnp.tanh(0.7978845608028654 * (x + 0.044715 * x * x * x))
    return 0.5 * (1.0 + t) + 0.5 * x * (1.0 - t * t) * 0.7978845608028654 * (1.0 + 3.0 * 0.044715 * x * x)


def _ssm_out(xs, proj_a, cfull, dskip, wglu, name):
    S = xs.shape[0]
    tm = min(256, S)
    C = D_SSM

    def body(xs_ref, u_ref, c_ref, d_ref, wg_ref, y1_ref, ms_ref):
        y1 = jnp.dot(xs_ref[...].astype(BF16), c_ref[...], preferred_element_type=F32) + d_ref[...] * u_ref[...]
        y2 = _gelu(y1)
        gl = jnp.dot(y2.astype(BF16), wg_ref[...], preferred_element_type=F32)
        y1_ref[...] = y1
        ms_ref[...] = (y2 * _sigmoid(gl)).astype(BF16)

    wga, wgs = _op(wglu, (C, C), lambda i: (0, 0))
    return pl.pallas_call(
        body, grid=(S // tm,),
        in_specs=[_rows(2 * 1024, 0, tm), _rows(C, 3, tm), _whole((2 * 1024, C)), _whole((1, C)), wgs],
        out_specs=[_rows(C, 0, tm), _rows(C, 0, tm)],
        out_shape=[jax.ShapeDtypeStruct((S, C), F32), jax.ShapeDtypeStruct((S, C), BF16)],
        compiler_params=_params(("parallel",)), name=name)(xs, proj_a, cfull, dskip, wga)


def _ssm_out_bwd(dms, y1, proj_a, cfull, dskip, wglu, name):
    S = y1.shape[0]
    tm = min(256, S)
    C = D_SSM

    def body(dms_ref, y1_ref, u_ref, c_ref, d_ref, wg_ref, dy1_ref, y2_ref, dgl_ref, dxs_ref, du_ref, dd_ref):
        @pl.when(pl.program_id(0) == 0)
        def _():
            dd_ref[...] = jnp.zeros_like(dd_ref)

        dms, y1 = dms_ref[...], y1_ref[...]
        y2 = _gelu(y1)
        y2b = y2.astype(BF16)
        sg = _sigmoid(jnp.dot(y2b, wg_ref[...], preferred_element_type=F32))
        dgl = (dms * y2 * sg * (1.0 - sg)).astype(BF16)
        dy2 = dms * sg + lax.dot_general(dgl, wg_ref[...], (((1,), (1,)), ((), ())), preferred_element_type=F32)
        dy1 = dy2 * _gelu_grad(y1)
        dy1b = dy1.astype(BF16)
        dy1_ref[...] = dy1b
        y2_ref[...] = y2b
        dgl_ref[...] = dgl
        dxs_ref[...] = lax.dot_general(dy1b, c_ref[...], (((1,), (1,)), ((), ())), preferred_element_type=F32)
        du_ref[...] = d_ref[...] * dy1
        dd_ref[...] += jnp.sum(dy1 * u_ref[...], axis=0, keepdims=True)

    wga, wgs = _op(wglu, (C, C), lambda i: (0, 0))
    rc = _rows(C, 0, tm)
    return pl.pallas_call(
        body, grid=(S // tm,),
        in_specs=[rc, rc, _rows(C, 3, tm), _whole((2 * 1024, C)), _whole((1, C)), wgs],
        out_specs=[rc, rc, rc, _rows(2 * 1024, 0, tm), rc, _whole((1, C))],
        out_shape=[jax.ShapeDtypeStruct((S, C), BF16), jax.ShapeDtypeStruct((S, C), BF16),
                   jax.ShapeDtypeStruct((S, C), BF16), jax.ShapeDtypeStruct((S, 2 * 1024), F32),
                   jax.ShapeDtypeStruct((S, C), F32), jax.ShapeDtypeStruct((1, C), F32)],
        compiler_params=_params(("arbitrary",)), name=name)(dms, y1, proj_a, cfull, dskip, wga)


_NT = (((1,), (1,)), ((), ()))
_TN = (((0,), (0,)), ((), ()))


def _attn_heads(q_ref, kp_ref, kc_ref, vp_ref, vc_ref):
    lane = lax.broadcasted_iota(jnp.int32, (BLOCK, 128), 1)
    kk = jnp.concatenate([kp_ref[...], kc_ref[...]], axis=0).astype(BF16)
    vv = jnp.concatenate([vp_ref[...], vc_ref[...]], axis=0).astype(BF16)
    kk_r, vv_r = pltpu.roll(kk, 64, axis=1), pltpu.roll(vv, 64, axis=1)
    heads = []
    for hq in range(N_Q_HEADS):
        j, e = hq // 2, hq % 2
        qj = (q_ref[:, 128 * j:128 * (j + 1)] * (HEAD_DIM ** -0.5)).astype(BF16)
        own = (lane >= 64) if e else (lane < 64)
        aligned = e == hq // 4
        heads.append((own, jnp.where(own, qj, jnp.zeros_like(qj)), kk if aligned else kk_r, vv if aligned else vv_r,
                      aligned))
    return heads


def _attn_probs(i, heads, s_ref):
    n = N_Q_HEADS * BLOCK
    s = jnp.concatenate([lax.dot_general(qm, ks, _NT, preferred_element_type=F32) for _, qm, ks, _, _ in heads],
                        axis=0)
    row = lax.broadcasted_iota(jnp.int32, (n, 2 * BLOCK), 0) & (BLOCK - 1)
    col = lax.broadcasted_iota(jnp.int32, (n, 2 * BLOCK), 1)
    mask = (col > row) & (col <= row + BLOCK) & ((col >= BLOCK) | (i > 0))
    s = jnp.where(mask, s, -1e30)
    sink = jnp.concatenate([jnp.full((BLOCK, 1), s_ref[0, hq], F32) for hq in range(N_Q_HEADS)], axis=0)
    m = jnp.maximum(jnp.max(s, axis=1, keepdims=True), sink)
    p = jnp.exp(s - m)
    es = jnp.exp(sink - m)
    inv = 1.0 / (jnp.sum(p, axis=1, keepdims=True) + es)
    return p * inv, es * inv


def _attn_fwd(proj_a, sinks, name):
    S = proj_a.shape[0]
    nb = S // BLOCK

    def body(q_ref, kp_ref, kc_ref, vp_ref, vc_ref, s_ref, out_ref):
        i = pl.program_id(0)
        heads = _attn_heads(q_ref, kp_ref, kc_ref, vp_ref, vc_ref)
        row = lax.broadcasted_iota(jnp.int32, (BLOCK, 2 * BLOCK), 0)
        col = lax.broadcasted_iota(jnp.int32, (BLOCK, 2 * BLOCK), 1)
        mask = (col > row) & (col <= row + BLOCK) & ((col >= BLOCK) | (i > 0))
        outs = []
        for hq, (_, qm, ks, vs, _) in enumerate(heads):
            s = jnp.where(mask, lax.dot_general(qm, ks, _NT, preferred_element_type=F32), -1e30)
            sink = s_ref[0, hq]
            m = jnp.maximum(jnp.max(s, axis=1, keepdims=True), sink)
            p = jnp.exp(s - m)
            inv = 1.0 / (jnp.sum(p, axis=1, keepdims=True) + jnp.exp(sink - m))
            outs.append(jnp.dot((p * inv).astype(BF16), vs, preferred_element_type=F32))
        for j in range(4):
            out_ref[:, 128 * j:128 * (j + 1)] = jnp.where(heads[2 * j][0], outs[2 * j], outs[2 * j + 1]).astype(BF16)

    prev = lambda c: pl.BlockSpec((BLOCK, 128), lambda i: (jnp.maximum(i - 1, 0), c))
    cur = lambda c: pl.BlockSpec((BLOCK, 128), lambda i: (i, c))
    return pl.pallas_call(
        body, grid=(nb,),
        in_specs=[_rows(D_ATTN, 2, BLOCK), prev(12), cur(12), prev(13), cur(13),
                  pl.BlockSpec(memory_space=pltpu.SMEM)],
        out_specs=_rows(D_ATTN, 0, BLOCK), out_shape=jax.ShapeDtypeStruct((S, D_ATTN), BF16),
        compiler_params=_params(("parallel",)), name=name)(proj_a, proj_a, proj_a, proj_a, proj_a, sinks)


def _attn_bwd(proj_a, dout, sinks, name):
    S = proj_a.shape[0]
    nb = S // BLOCK

    def body(q_ref, kp_ref, kc_ref, vp_ref, vc_ref, do_ref, s_ref, out_ref, dk_ref, dv_ref, ds_ref, ck_ref, cv_ref):
        i = pl.program_id(0)

        @pl.when(i == 0)
        def _():
            ds_ref[...] = jnp.zeros_like(ds_ref)
            ck_ref[...] = jnp.zeros_like(ck_ref)
            cv_ref[...] = jnp.zeros_like(cv_ref)

        @pl.when(i < nb)
        def _():
            heads = _attn_heads(q_ref, kp_ref, kc_ref, vp_ref, vc_ref)
            pn, psink = _attn_probs(i, heads, s_ref)
            doms = []
            for hq, (own, _, _, _, _) in enumerate(heads):
                doj = do_ref[:, 128 * (hq // 2):128 * (hq // 2 + 1)].astype(BF16)
                doms.append(jnp.where(own, doj, jnp.zeros_like(doj)))
            dp = jnp.concatenate([lax.dot_general(dom, vs, _NT, preferred_element_type=F32)
                                  for dom, (_, _, _, vs, _) in zip(doms, heads)], axis=0)
            delta = jnp.sum(pn * dp, axis=1, keepdims=True)
            dsb = (pn * (dp - delta)).astype(BF16)
            pnb = pn.astype(BF16)
            sk = psink * delta
            dkk = jnp.zeros((2 * BLOCK, 128), F32)
            dvv = jnp.zeros((2 * BLOCK, 128), F32)
            dqs = []
            for hq, (own, qm, ks, vs, aligned) in enumerate(heads):
                rows = slice(BLOCK * hq, BLOCK * (hq + 1))
                ds_ref[hq:hq + 1, :] += jnp.broadcast_to(-jnp.sum(sk[rows]), (1, 128))
                dqs.append(jnp.dot(dsb[rows], ks, preferred_element_type=F32) * (HEAD_DIM ** -0.5))
                dk = lax.dot_general(dsb[rows], qm, _TN, preferred_element_type=F32)
                dv = lax.dot_general(pnb[rows], doms[hq], _TN, preferred_element_type=F32)
                dkk = dkk + (dk if aligned else pltpu.roll(dk, 64, axis=1))
                dvv = dvv + (dv if aligned else pltpu.roll(dv, 64, axis=1))
            for j in range(4):
                out_ref[:, 128 * j:128 * (j + 1)] = jnp.where(heads[2 * j][0], dqs[2 * j], dqs[2 * j + 1]).astype(BF16)
            ck_ref[0:BLOCK, :] = ck_ref[BLOCK:, :] + dkk[0:BLOCK]
            cv_ref[0:BLOCK, :] = cv_ref[BLOCK:, :] + dvv[0:BLOCK]
            ck_ref[BLOCK:, :] = dkk[BLOCK:]
            cv_ref[BLOCK:, :] = dvv[BLOCK:]

        @pl.when(i == nb)
        def _():
            ck_ref[0:BLOCK, :] = ck_ref[BLOCK:, :]
            cv_ref[0:BLOCK, :] = cv_ref[BLOCK:, :]

        dk_ref[...] = ck_ref[0:BLOCK, :].astype(BF16)
        dv_ref[...] = cv_ref[0:BLOCK, :].astype(BF16)

    last = nb - 1
    prev = lambda c: pl.BlockSpec((BLOCK, 128), lambda i: (jnp.clip(i - 1, 0, last), c))
    cur = lambda c: pl.BlockSpec((BLOCK, 128), lambda i: (jnp.minimum(i, last), c))
    qrow = lambda w, c: pl.BlockSpec((BLOCK, w), lambda i: (jnp.minimum(i, last), c))

    dq, dk, dv, ds = pl.pallas_call(
        body, grid=(nb + 1,),
        in_specs=[qrow(D_ATTN, 2), prev(12), cur(12), prev(13), cur(13), qrow(D_ATTN, 0),
                  pl.BlockSpec(memory_space=pltpu.SMEM)],
        out_specs=[qrow(D_ATTN, 0), prev(0), prev(0), _whole((N_Q_HEADS, 128))],
        out_shape=[jax.ShapeDtypeStruct((S, D_ATTN), BF16), jax.ShapeDtypeStruct((S, D_KV), BF16),
                   jax.ShapeDtypeStruct((S, D_KV), BF16), jax.ShapeDtypeStruct((N_Q_HEADS, 128), F32)],
        scratch_shapes=[pltpu.VMEM((2 * BLOCK, 128), F32), pltpu.VMEM((2 * BLOCK, 128), F32)],
        compiler_params=_params(("arbitrary",)), name=name)(proj_a, proj_a, proj_a, proj_a, proj_a, dout, sinks)
    return dq, dk, dv, ds


_BR = ((0, D_CONV), (D_CONV, D_CONV + D_SSM), (D_CONV + D_SSM, D_CONV + D_SSM + D_ATTN))


def _branches(m_refs, wbr_ref):
    nt = (((1,), (1,)), ((), ()))
    return [lax.dot_general(m[...], wbr_ref[:, lo:hi], nt, preferred_element_type=F32)
            for m, (lo, hi) in zip(m_refs, _BR)]


def _merge_fwd(mc, ms, ma, proj_g, wbr_t, name):
    S, Dm = mc.shape[0], proj_g.shape[1] // 3
    tm = min(256, S)

    def body(mc_ref, ms_ref, ma_ref, g_ref, w_ref, out_ref):
        ys = _branches((mc_ref, ms_ref, ma_ref), w_ref)
        acc = None
        for b in range(3):
            t = _sigmoid(g_ref[:, b * Dm:(b + 1) * Dm]) * ys[b]
            acc = t if acc is None else acc + t
        out_ref[...] = acc.astype(BF16)

    wa, ws = _op(wbr_t, (Dm, Dm), lambda i: (0, 0))
    return pl.pallas_call(
        body, grid=(S // tm,),
        in_specs=[_rows(D_CONV, 0, tm), _rows(D_SSM, 0, tm), _rows(D_ATTN, 0, tm), _rows(3 * Dm, 0, tm), ws],
        out_specs=_rows(Dm, 0, tm), out_shape=jax.ShapeDtypeStruct((S, Dm), BF16),
        compiler_params=_params(("parallel",)), name=name)(mc, ms, ma, proj_g, wa)


def _merge_bwd(dmerged, mc, ms, ma, proj_g, wbr_t, name):
    S, Dm = mc.shape[0], proj_g.shape[1] // 3
    tm = min(256, S)

    def body(d_ref, mc_ref, ms_ref, ma_ref, g_ref, w_ref, dg_ref, dy_ref, dmc_ref, dms_ref, dma_ref):
        ys = _branches((mc_ref, ms_ref, ma_ref), w_ref)
        d = d_ref[...]
        for b, (o_ref, (lo, hi)) in enumerate(zip((dmc_ref, dms_ref, dma_ref), _BR)):
            g = _sigmoid(g_ref[:, b * Dm:(b + 1) * Dm])
            dg_ref[:, b * Dm:(b + 1) * Dm] = (d * ys[b] * g * (1.0 - g)).astype(BF16)
            dyb = (g * d).astype(BF16)
            dy_ref[:, b * Dm:(b + 1) * Dm] = dyb
            o_ref[...] = jnp.dot(dyb, w_ref[:, lo:hi], preferred_element_type=F32)

    wa, ws = _op(wbr_t, (Dm, Dm), lambda i: (0, 0))
    return pl.pallas_call(
        body, grid=(S // tm,),
        in_specs=[_rows(Dm, 0, tm), _rows(D_CONV, 0, tm), _rows(D_SSM, 0, tm), _rows(D_ATTN, 0, tm),
                  _rows(3 * Dm, 0, tm), ws],
        out_specs=[_rows(3 * Dm, 0, tm), _rows(3 * Dm, 0, tm), _rows(D_CONV, 0, tm), _rows(D_SSM, 0, tm),
                   _rows(D_ATTN, 0, tm)],
        out_shape=[jax.ShapeDtypeStruct((S, 3 * Dm), BF16), jax.ShapeDtypeStruct((S, 3 * Dm), BF16),
                   jax.ShapeDtypeStruct((S, D_CONV), F32), jax.ShapeDtypeStruct((S, D_SSM), F32),
                   jax.ShapeDtypeStruct((S, D_ATTN), F32)],
        compiler_params=_params(("parallel",)), name=name)(dmerged, mc, ms, ma, proj_g, wa)


def _loss_head(y, target):
    S, Dm = y.shape
    tm = min(512, S)

    def body(y_ref, t_ref, dy_ref, l_ref):
        @pl.when(pl.program_id(0) == 0)
        def _():
            l_ref[...] = jnp.zeros_like(l_ref)

        e = y_ref[...] - t_ref[...]
        dy_ref[...] = e * (1.0 / Dm)
        l_ref[...] += jnp.broadcast_to(0.5 * jnp.sum(jnp.sum(e * e, axis=1, keepdims=True) * (1.0 / Dm)), (1, 128))

    return pl.pallas_call(
        body, grid=(S // tm,), in_specs=[_rows(Dm, 0, tm), _rows(Dm, 0, tm)],
        out_specs=[_rows(Dm, 0, tm), _whole((1, 128))],
        out_shape=[jax.ShapeDtypeStruct((S, Dm), F32), jax.ShapeDtypeStruct((1, 128), F32)],
        compiler_params=_params(("arbitrary",)), name="loss_head")(y, target)


def _view2d(shape):
    n = math.prod(shape)
    if shape[-1] % 128 == 0:
        return (n // shape[-1], shape[-1])
    if n >= (1 << 16) and len(shape) == 3:
        return (shape[0] * shape[1], shape[2])
    if n % 128 == 0:
        return (n // 128, 128)
    return (1, n)


def _adamw(w, g, m, v, name):
    shape = w.shape
    R, C = _view2d(shape)
    tm = R
    for cand in (512, 352, 256):
        if R > cand and R % cand == 0:
            tm = cand
            break
    c1 = 1.0 - ADAM_B1 ** ADAM_STEP
    c2 = 1.0 - ADAM_B2 ** ADAM_STEP

    def body(w_ref, g_ref, m_ref, v_ref, d_ref, nm_ref, nv_ref):
        g = g_ref[...]
        nm = ADAM_B1 * m_ref[...] + (1.0 - ADAM_B1) * g
        nv = ADAM_B2 * v_ref[...] + (1.0 - ADAM_B2) * (g * g)
        d_ref[...] = -ADAM_LR * ((nm / c1) / (jnp.sqrt(nv / c2) + ADAM_EPS) + ADAM_WD * w_ref[...])
        nm_ref[...] = nm
        nv_ref[...] = nv

    blk = _rows(C, 0, tm)
    outs = pl.pallas_call(
        body, grid=(R // tm,), in_specs=[blk] * 4, out_specs=[blk] * 3,
        out_shape=[jax.ShapeDtypeStruct((R, C), F32)] * 3,
        compiler_params=_params(("parallel",)), name=name)(*[t.reshape(R, C) for t in (w, g, m, v)])
    return [o.reshape(shape) for o in outs]


def _coords():
    return lax.axis_index("x"), lax.axis_index("y"), lax.axis_index("c")


def _other_chips(x, y):
    return [((1 - x, y), 2 * (1 - x) + y), ((x, 1 - y), 2 * x + 1 - y), ((1 - x, 1 - y), 2 * (1 - x) + 1 - y)]


def _exchange(build, n_in, out_shapes, n_sem, aliases, name):
    def body(*refs):
        ins, outs = refs[:n_in], refs[n_in:n_in + len(out_shapes)]
        send_sems, recv_sems, local_sems = refs[n_in + len(out_shapes):]
        sends, recvs, locals_ = build(ins, outs, send_sems, recv_sems, local_sems)
        for cp in locals_:
            cp.start()
        for cp in sends:
            cp.start()
        for cp in recvs:
            cp.wait_recv()
        for cp in sends:
            cp.wait_send()
        for cp in locals_:
            cp.wait()

    hbm = pl.BlockSpec(memory_space=pltpu.HBM)
    return pl.pallas_call(
        body, in_specs=[hbm] * n_in, out_specs=[hbm] * len(out_shapes), out_shape=out_shapes,
        scratch_shapes=[pltpu.SemaphoreType.DMA((n_sem,)), pltpu.SemaphoreType.DMA((n_sem,)),
                        pltpu.SemaphoreType.DMA((n_sem,))],
        input_output_aliases=aliases,
        compiler_params=pltpu.CompilerParams(has_side_effects=True), name=name)


def _remote(src, dst, send_sems, recv_sems, k, dev):
    return pltpu.make_async_remote_copy(src_ref=src, dst_ref=dst, send_sem=send_sems.at[k], recv_sem=recv_sems.at[k],
                                        device_id=dev, device_id_type=MESH)


def _gather_weights(shards):
    n = len(shards)
    fulls = [jax.ShapeDtypeStruct((s.shape[0], N_CHIPS * s.shape[1], s.shape[2]), s.dtype) for s in shards]

    def half(ref, chip, c, rp):
        return ref.at[:, pl.ds(pl.multiple_of(chip * rp + c * (rp // 2), 16), rp // 2), :]

    def step1(ins, outs, send_sems, recv_sems, local_sems):
        x, y, c = _coords()
        me = 2 * x + y
        sends, recvs = [], []
        for a in range(n):
            rp = ins[a].shape[1]
            src = ins[a].at[:, pl.ds(pl.multiple_of(c * (rp // 2), 16), rp // 2), :]
            for r, ((px, py), chip) in enumerate(_other_chips(x, y)):
                k = 3 * a + r
                sends.append(_remote(src, half(outs[a], me, c, rp), send_sems, recv_sems, k, (px, py, c)))
                recvs.append(_remote(src, half(outs[a], chip, c, rp), send_sems, recv_sems, k, (px, py, c)))
        return sends, recvs, []

    cx, cy, _ = _coords()
    placed = [lax.dynamic_update_slice(lax.empty(f.shape, f.dtype), s, (0, (2 * cx + cy) * s.shape[1], 0))
              for f, s in zip(fulls, shards)]
    got = _exchange(step1, 2 * n, fulls, 3 * n, {n + a: a for a in range(n)}, "gather_weights_ici")(*shards, *placed)

    def step2(ins, outs, send_sems, recv_sems, local_sems):
        x, y, c = _coords()
        sends, recvs = [], []
        for a in range(n):
            rp = ins[a].shape[1] // N_CHIPS
            for r, (_, chip) in enumerate(_other_chips(x, y)):
                k = 3 * a + r
                mine, theirs = half(outs[a], chip, c, rp), half(outs[a], chip, 1 - c, rp)
                sends.append(_remote(mine, mine, send_sems, recv_sems, k, (x, y, 1 - c)))
                recvs.append(_remote(theirs, theirs, send_sems, recv_sems, k, (x, y, 1 - c)))
        return sends, recvs, []

    return _exchange(step2, n, fulls, 3 * n, {a: a for a in range(n)}, "gather_weights_d2d")(*got)


def _pair_sum(a, b, c_idx, half_rows, out_dtype, name):
    n4, rp, W = a.shape
    tr = half_rows // 2
    nblk = half_rows // tr

    def body(c_ref, a_ref, b_ref, o_ref):
        o_ref[...] = (a_ref[...] + b_ref[...]).astype(o_ref.dtype)

    return pl.pallas_call(
        body,
        grid_spec=pltpu.PrefetchScalarGridSpec(
            num_scalar_prefetch=1, grid=(nblk,),
            in_specs=[pl.BlockSpec((n4, tr, W), lambda i, c: (0, c[0] * nblk + i, 0)),
                      pl.BlockSpec((n4, tr, W), lambda i, c: (0, i, 0))],
            out_specs=pl.BlockSpec((n4, tr, W), lambda i, c: (0, i, 0))),
        out_shape=jax.ShapeDtypeStruct((n4, half_rows, W), out_dtype),
        compiler_params=_params(("parallel",)), name=name)(c_idx, a, b)


def _sum4(m, c_idx, name):
    _, R, W = m.shape
    tr = R // 2

    def body(c_ref, m_ref, o_ref):
        v = m_ref[...].astype(F32)
        o_ref[...] = ((v[0] + v[1]) + v[2]) + v[3]

    return pl.pallas_call(
        body,
        grid_spec=pltpu.PrefetchScalarGridSpec(
            num_scalar_prefetch=1, grid=(R // tr,),
            in_specs=[pl.BlockSpec((N_CHIPS, tr, W), lambda i, c: (0, i, 0))],
            out_specs=pl.BlockSpec((None, tr, W), lambda i, c: (c[0], i, 0))),
        out_shape=jax.ShapeDtypeStruct((2, R, W), F32),
        compiler_params=_params(("parallel",)), name=name)(c_idx, m)


def _reduce_scatter(grads):
    n = len(grads)
    x, y, c = _coords()
    g4 = [g.reshape(N_CHIPS, g.shape[0] // N_CHIPS, g.shape[1]) for g in grads]
    hr = [g.shape[1] // 2 for g in g4]

    def step1(ins, outs, send_sems, recv_sems, local_sems):
        x, y, c = _coords()
        sends, recvs = [], []
        for a in range(n):
            src = ins[a].at[:, pl.ds(pl.multiple_of((1 - c) * hr[a], 8), hr[a]), :]
            sends.append(_remote(src, outs[a], send_sems, recv_sems, a, (x, y, 1 - c)))
            recvs.append(_remote(src, outs[a], send_sems, recv_sems, a, (x, y, 1 - c)))
        return sends, recvs, []

    lands = [jax.ShapeDtypeStruct((N_CHIPS, hr[a], g4[a].shape[2]), F32) for a in range(n)]
    got = _exchange(step1, n, lands, n, {}, "reduce_d2d")(*g4)
    c_idx = jnp.reshape(c, (1,)).astype(jnp.int32)
    q = [_pair_sum(g4[a], got[a], c_idx, hr[a], BF16, "reduce_pair_sum") for a in range(n)]

    def step2(ins, outs, send_sems, recv_sems, local_sems):
        x, y, c = _coords()
        me = 2 * x + y
        sends, recvs, locals_ = [], [], []
        for a in range(n):
            locals_.append(pltpu.make_async_copy(ins[a].at[me], outs[a].at[me], local_sems.at[a]))
            for r, ((px, py), chip) in enumerate(_other_chips(x, y)):
                k = 3 * a + r
                sends.append(_remote(ins[a].at[chip], outs[a].at[me], send_sems, recv_sems, k, (px, py, c)))
                recvs.append(_remote(ins[a].at[chip], outs[a].at[chip], send_sems, recv_sems, k, (px, py, c)))
        return sends, recvs, locals_

    lands = [jax.ShapeDtypeStruct(t.shape, BF16) for t in q]
    got = _exchange(step2, n, lands, 3 * n, {}, "reduce_ici")(*q)
    r = [_sum4(got[a], c_idx, "reduce_sum4") for a in range(n)]

    def step3(ins, outs, send_sems, recv_sems, local_sems):
        x, y, c = _coords()
        sends, recvs = [], []
        for a in range(n):
            sends.append(_remote(outs[a].at[c], outs[a].at[c], send_sems, recv_sems, a, (x, y, 1 - c)))
            recvs.append(_remote(outs[a].at[c], outs[a].at[1 - c], send_sems, recv_sems, a, (x, y, 1 - c)))
        return sends, recvs, []

    lands = [jax.ShapeDtypeStruct(t.shape, F32) for t in r]
    got = _exchange(step3, n, lands, n, {a: a for a in range(n)}, "reduce_share")(*r)
    return [t.reshape(2 * t.shape[1], t.shape[2]) for t in got]


def _all_reduce_small(buf, name):
    R = buf.shape[0]

    def step(ins, outs, send_sems, recv_sems, local_sems):
        x, y, c = _coords()
        me = 4 * x + 2 * y + c
        sends, recvs = [], []
        k = 0
        for fx in range(2):
            for fy in range(2):
                for fc in range(2):
                    if fx + fy + fc == 0:
                        continue
                    px, py, pc = x ^ fx, y ^ fy, c ^ fc
                    sends.append(_remote(ins[0], outs[0].at[me], send_sems, recv_sems, k, (px, py, pc)))
                    recvs.append(_remote(ins[0], outs[0].at[4 * px + 2 * py + pc], send_sems, recv_sems, k,
                                         (px, py, pc)))
                    k += 1
        return sends, recvs, [pltpu.make_async_copy(ins[0], outs[0].at[me], local_sems.at[0])]

    got = _exchange(step, 1, [jax.ShapeDtypeStruct((N_DEV, R, 128), F32)], N_DEV - 1, {}, name + "_gather")(buf)[0]

    def body(m_ref, o_ref):
        acc = m_ref[0]
        for d in range(1, N_DEV):
            acc = acc + m_ref[d]
        o_ref[...] = acc

    tr = max([t for t in range(8, min(R, 1024) + 1, 8) if R % t == 0], default=R)
    return pl.pallas_call(
        body, grid=(R // tr,), in_specs=[pl.BlockSpec((N_DEV, tr, 128), lambda i: (0, i, 0))],
        out_specs=pl.BlockSpec((tr, 128), lambda i: (i, 0)), out_shape=jax.ShapeDtypeStruct((R, 128), F32),
        compiler_params=_params(("parallel",)), name=name + "_sum")(got)


def _ssm_layouts(p, L):
    G = L * N_GROUPS
    lr = p["ssm_a_re"].reshape(G, 1, SSM_STATE)
    li = p["ssm_a_im"].reshape(G, 1, SSM_STATE)
    ldt = p["ssm_log_dt"].reshape(G, 1, 1)
    br = jnp.swapaxes(p["ssm_b_re"], 2, 3).reshape(G, SSM_GROUP, SSM_STATE)
    bi = jnp.swapaxes(p["ssm_b_im"], 2, 3).reshape(G, SSM_GROUP, SSM_STATE)
    are, aim, bre, bim = _disc_fwd(lr, li, ldt, br, bi)
    eye = jnp.eye(N_GROUPS, dtype=F32)
    abar = jnp.concatenate([are.reshape(L, 8, 128), aim.reshape(L, 8, 128)], axis=1)

    def b_blk(t):
        return jnp.einsum("lgcp,gh->lgchp", t.reshape(L, N_GROUPS, SSM_GROUP, SSM_STATE), eye).reshape(L, 256, 1024)

    def c_blk(t):
        return jnp.einsum("lgcp,gh->lgphc", t, eye).reshape(L, 1024, 256)

    bfull = jnp.concatenate([b_blk(bre), b_blk(bim)], axis=2).astype(BF16)
    cfull = jnp.concatenate([c_blk(p["ssm_c_re"]), -c_blk(p["ssm_c_im"])], axis=1).astype(BF16)
    return (lr, li, ldt, br, bi), abar, bfull, cfull


def _local_step(x, target, W, p, L):
    S, Dm = x.shape
    dff = W["wg1"].shape[1]
    alpha = (2.0 * L) ** 0.25
    disc_in, abar, bfull, cfull = _ssm_layouts(p, L)
    row = lambda t, l: t[l][None]
    saved = []
    h, hb = x, x.astype(BF16)
    for l in range(L):
        sv = {"x0": hb}
        a1, b1, h1 = _ffn_up(hb, (W["wg1"], l), (W["wu1"], l), dff, "ffn_up")
        x1, x1b, xh1, rs1 = _mm_ln(h1, (W["wd1"], l), h, row(p["ln1_g"], l), row(p["ln1_b"], l), 0.5, alpha,
                                   "ffn_down_ln", 1408)
        sv.update(a1=a1, b1=b1, h1=h1, x1=x1b, xh1=xh1, rs1=rs1)
        proj_a = _mm([(x1b, (W["win"], l))], "nt", S, D_A, Dm, [F32], name="proj_a", tm=1024, tn=896, tk=1024)
        proj_g = _mm([(x1b, (W["win"], l))], "nt", S, 3 * Dm, Dm, [F32], name="proj_g", tm=2048, tn=256, tk=1024,
                     b_off=D_A)
        mc = _conv_fwd(proj_a, p["conv_w"][l], row(p["conv_b"], l), "conv_fwd")
        bu = _mm([(proj_a, (bfull, l))], "nn", S, 2048, D_SSM, [F32], name="ssm_bu", tm=1024, tn=1024, tk=256,
                 a_off=3 * D_CONV)
        xs = _scan_fwd(bu.reshape(S, 16, 128), abar[l], "scan_fwd").reshape(S, 2048)
        y1, ms = _ssm_out(xs, proj_a, cfull[l], row(p["ssm_d"], l), (W["wglu"], l), "ssm_out")
        sinks = p["attn_sinks"][l][None]
        ma = _attn_fwd(proj_a, sinks, "attn_fwd")
        merged = _merge_fwd(mc, ms, ma, proj_g, (W["wbr"], l), "merge_fwd")
        x2, x2b, xh2, rs2 = _mm_ln(merged, (W["wout"], l), x1, row(p["ln2_g"], l), row(p["ln2_b"], l), 1.0, alpha,
                                   "mix_out_ln", 1024)
        sv.update(proj_a=proj_a, proj_g=proj_g, mc=mc, ms=ms, ma=ma, xs=xs, y1=y1, merged=merged, x2=x2b, xh2=xh2,
                  rs2=rs2)
        a2, b2, h2 = _ffn_up(x2b, (W["wg2"], l), (W["wu2"], l), dff, "ffn_up")
        h, hb, xh3, rs3 = _mm_ln(h2, (W["wd2"], l), x2, row(p["ln3_g"], l), row(p["ln3_b"], l), 0.5, alpha,
                                 "ffn_down_ln", 1408)
        sv.update(a2=a2, b2=b2, h2=h2, xh3=xh3, rs3=rs3)
        saved.append(sv)

    dy, loss = _loss_head(h, target)
    big = [None] * L
    small = {k: [None] * L for k in ("ln1_g", "ln1_b", "ln2_g", "ln2_b", "ln3_g", "ln3_b", "conv_w", "conv_b", "ssm_d",
                                     "attn_sinks", "wglu", "dabar", "dbfull", "dcfull")}

    def ffn_bwd(dy_out, x_in, a, b, hh, xh, rs, g, wg, wu, wd, l):
        dres, df, dg, db = _ln_bwd(dy_out, xh, rs, g, 0.5, alpha, "ln_bwd")
        da, dbb = _ffn_dh(df, (wd, l), a, b, "ffn_dh")
        dx = _mm([(da, (wg, l)), (dbb, (wu, l))], "nn", S, Dm, dff, [F32], name="ffn_dx", tm=1024, tn=1024,
                 tk=dff // 2, add=dres)
        tn_kw = dict(tm=1408, tn=1024, tk=512)
        dwg = _mm([(da, x_in)], "tn", dff, Dm, S, [F32], name="ffn_dw_up", **tn_kw)
        dwu = _mm([(dbb, x_in)], "tn", dff, Dm, S, [F32], name="ffn_dw_up", **tn_kw)
        dwd = _mm([(hh, df)], "tn", dff, Dm, S, [F32], name="ffn_dw_down", **tn_kw)
        return dx, dwg, dwu, dwd, dg, db

    for l in reversed(range(L)):
        sv = saved[l]
        dx2, dwg2, dwu2, dwd2, small["ln3_g"][l], small["ln3_b"][l] = ffn_bwd(
            dy, sv["x2"], sv["a2"], sv["b2"], sv["h2"], sv["xh3"], sv["rs3"], row(p["ln3_g"], l), W["wg2"], W["wu2"],
            W["wd2"], l)
        dres2, dmix, small["ln2_g"][l], small["ln2_b"][l] = _ln_bwd(dx2, sv["xh2"], sv["rs2"], row(p["ln2_g"], l), 1.0,
                                                                   alpha, "ln_bwd")
        dwout = _mm([(sv["merged"], dmix)], "tn", Dm, Dm, S, [F32], name="dw_out", tm=512, tn=1024, tk=512)
        dmerged = _mm([(dmix, (W["wout"], l))], "nt", S, Dm, Dm, [F32], name="d_merged", tm=1024, tn=1024, tk=512)
        dgates, dyb, dmc, dms, dma = _merge_bwd(dmerged, sv["mc"], sv["ms"], sv["ma"], sv["proj_g"], (W["wbr"], l),
                                                "merge_bwd")
        dwbr = jnp.concatenate([
            _mm([(dyb, m)], "tn", Dm, hi - lo, S, [F32], name="dw_br", tm=512, tn=512, tk=512, a_off=b * Dm)
            for b, ((lo, hi), m) in enumerate(zip(_BR, (sv["mc"], sv["ms"], sv["ma"])))], axis=1)
        proj_a = sv["proj_a"]
        d_conv, small["conv_w"][l], small["conv_b"][l] = _conv_bwd(proj_a, dmc, p["conv_w"][l], row(p["conv_b"], l),
                                                                  "conv_bwd")
        dy1, y2, dgl, dxs, du_skip, small["ssm_d"][l] = _ssm_out_bwd(dms, sv["y1"], proj_a, cfull[l],
                                                                    row(p["ssm_d"], l), (W["wglu"], l), "ssm_out_bwd")
        small["wglu"][l] = _mm([(y2, dgl)], "tn", D_SSM, D_SSM, S, [F32], name="dw_glu", tk=512)
        small["dcfull"][l] = _mm([(sv["xs"], dy1)], "tn", 2048, D_SSM, S, [F32], name="d_cfull", tm=1024, tk=512)
        lam, small["dabar"][l] = _scan_bwd(dxs.reshape(S, 16, 128), sv["xs"].reshape(S, 16, 128), abar[l], "scan_bwd")
        lam = lam.reshape(S, 2048)
        du = _mm([(lam, (bfull, l))], "nt", S, D_SSM, 2048, [BF16], name="ssm_du", tm=1024, tk=512, add=du_skip)
        small["dbfull"][l] = _mm([(proj_a, lam)], "tn", D_SSM, 2048, S, [F32], name="d_bfull", tm=256, tn=1024,
                                 tk=512, a_off=3 * D_CONV)
        dq, dk, dv, dsk = _attn_bwd(proj_a, dma, p["attn_sinks"][l][None], "attn_bwd")
        small["attn_sinks"][l] = dsk[:, 0]
        dproj = jnp.concatenate([d_conv, du, dq, dk, dv, dgates], axis=1)
        dx1 = _mm([(dproj, (W["win"], l))], "nn", S, Dm, D_A + 3 * Dm, [F32], name="d_x1", tm=1024, tn=1024,
                  tk=(D_A + 3 * Dm) // 2, add=dres2)
        dwin = _mm([(dproj, sv["x1"])], "tn", D_A + 3 * Dm, Dm, S, [F32], name="dw_in", tm=2432, tn=1024, tk=512)
        dx0, dwg1, dwu1, dwd1, small["ln1_g"][l], small["ln1_b"][l] = ffn_bwd(
            dx1, sv["x0"], sv["a1"], sv["b1"], sv["h1"], sv["xh1"], sv["rs1"], row(p["ln1_g"], l), W["wg1"], W["wu1"],
            W["wd1"], l)
        big[l] = dict(wg1=dwg1, wu1=dwu1, wd1=dwd1, win=dwin, wbr=dwbr, wout=dwout, wg2=dwg2, wu2=dwu2, wd2=dwd2)
        dy = dx0

    small = {k: jnp.stack(v) for k, v in small.items()}
    eye = jnp.eye(N_GROUPS, dtype=F32)
    dabar = small.pop("dabar")
    dbf = small.pop("dbfull").reshape(L, N_GROUPS, SSM_GROUP, 2, N_GROUPS, SSM_STATE)
    dbbar = jnp.einsum("lgcrhp,gh->rlgcp", dbf, eye).reshape(2, L * N_GROUPS, SSM_GROUP, SSM_STATE)
    dcf = small.pop("dcfull").reshape(L, 2, N_GROUPS, SSM_STATE, N_GROUPS, SSM_GROUP)
    dc = jnp.einsum("lrgphc,gh->rlgcp", dcf, eye)
    G = L * N_GROUPS
    cts = (dabar[:, 0:8].reshape(G, 1, SSM_STATE), dabar[:, 8:16].reshape(G, 1, SSM_STATE), dbbar[0], dbbar[1])
    dlr, dli, dldt, dbr, dbi = _disc_bwd(*disc_in, cts)
    shp_b = (L, N_GROUPS, SSM_GROUP, SSM_STATE)
    small.update(
        ssm_a_re=dlr.reshape(L, N_GROUPS, SSM_STATE), ssm_a_im=dli.reshape(L, N_GROUPS, SSM_STATE),
        ssm_log_dt=dldt.reshape(L, N_GROUPS), ssm_b_re=jnp.swapaxes(dbr.reshape(shp_b), 2, 3),
        ssm_b_im=jnp.swapaxes(dbi.reshape(shp_b), 2, 3), ssm_c_re=dc[0], ssm_c_im=-dc[1],
        ln1_g=small["ln1_g"][:, 0], ln1_b=small["ln1_b"][:, 0], ln2_g=small["ln2_g"][:, 0],
        ln2_b=small["ln2_b"][:, 0], ln3_g=small["ln3_g"][:, 0], ln3_b=small["ln3_b"][:, 0],
        conv_b=small["conv_b"][:, 0], ssm_d=small["ssm_d"][:, 0])
    return loss, dy, big, small


_SMALL_ORDER = ("ln1_g", "ln1_b", "ln2_g", "ln2_b", "ln3_g", "ln3_b", "conv_w", "conv_b", "ssm_a_re", "ssm_a_im",
                "ssm_log_dt", "ssm_b_re", "ssm_b_im", "ssm_c_re", "ssm_c_im", "ssm_d", "attn_sinks", "wglu")
_BIG_ORDER = ("wg1", "wu1", "wd1", "win", "wbr", "wout", "wg2", "wu2", "wd2")
_WEIGHTS = ("ffn1_w_gate", "ffn1_w_up", "ffn1_w_down", "ln1_g", "ln1_b", "w_in", "conv_w", "conv_b", "ssm_a_re",
            "ssm_a_im", "ssm_log_dt", "ssm_b_re", "ssm_b_im", "ssm_c_re", "ssm_c_im", "ssm_d", "ssm_w_glu",
            "attn_sinks", "w_br_conv", "w_br_ssm", "w_br_attn", "w_out", "ln2_g", "ln2_b", "ffn2_w_gate",
            "ffn2_w_up", "ffn2_w_down", "ln3_g", "ln3_b")


def _weight_shards(w):
    t = lambda a: jnp.swapaxes(a, 1, 2).astype(BF16)
    wbr = jnp.concatenate([t(w["w_br_conv"]), t(w["w_br_ssm"]), t(w["w_br_attn"])], axis=2)
    return dict(wg1=t(w["ffn1_w_gate"]), wu1=t(w["ffn1_w_up"]), wd1=w["ffn1_w_down"].astype(BF16), win=t(w["w_in"]),
                wbr=wbr, wout=w["w_out"].astype(BF16), wg2=t(w["ffn2_w_gate"]), wu2=t(w["ffn2_w_up"]),
                wd2=w["ffn2_w_down"].astype(BF16))


def _gather_small_shards(conv_w, w_glu, chip):
    L = conv_w.shape[0]
    part = jnp.concatenate([conv_w.reshape(1, -1), w_glu.reshape(1, -1)], axis=1)
    n = part.shape[1]
    slots = lax.dynamic_update_slice(jnp.zeros((N_CHIPS, n), F32), part, (chip, 0))
    got = (_all_reduce_small(slots.reshape(-1, 128), "param_gather") * 0.5).reshape(N_CHIPS, n)
    cw = jnp.transpose(got[:, :L * 3 * 64].reshape(N_CHIPS, L, 3, 64), (1, 2, 0, 3)).reshape(L, 3, N_CHIPS * 64)
    wg = jnp.transpose(got[:, L * 3 * 64:].reshape(N_CHIPS, L, 64, D_SSM), (1, 0, 2, 3)).reshape(L, N_CHIPS * 64, D_SSM)
    return cw, wg.astype(BF16)


def kernel(x, ffn1_w_gate, ffn1_w_up, ffn1_w_down, ln1_g, ln1_b, w_in, conv_w, conv_b, ssm_a_re, ssm_a_im, ssm_log_dt, ssm_b_re, ssm_b_im, ssm_c_re, ssm_c_im, ssm_d, ssm_w_glu, attn_sinks, w_br_conv, w_br_ssm, w_br_attn, w_out, ln2_g, ln2_b, ffn2_w_gate, ffn2_w_up, ffn2_w_down, ln3_g, ln3_b, loss_target, m_ffn1_w_gate, m_ffn1_w_up, m_ffn1_w_down, m_ln1_g, m_ln1_b, m_w_in, m_conv_w, m_conv_b, m_ssm_a_re, m_ssm_a_im, m_ssm_log_dt, m_ssm_b_re, m_ssm_b_im, m_ssm_c_re, m_ssm_c_im, m_ssm_d, m_ssm_w_glu, m_attn_sinks, m_w_br_conv, m_w_br_ssm, m_w_br_attn, m_w_out, m_ln2_g, m_ln2_b, m_ffn2_w_gate, m_ffn2_w_up, m_ffn2_w_down, m_ln3_g, m_ln3_b, v_ffn1_w_gate, v_ffn1_w_up, v_ffn1_w_down, v_ln1_g, v_ln1_b, v_w_in, v_conv_w, v_conv_b, v_ssm_a_re, v_ssm_a_im, v_ssm_log_dt, v_ssm_b_re, v_ssm_b_im, v_ssm_c_re, v_ssm_c_im, v_ssm_d, v_ssm_w_glu, v_attn_sinks, v_w_br_conv, v_w_br_ssm, v_w_br_attn, v_w_out, v_ln2_g, v_ln2_b, v_ffn2_w_gate, v_ffn2_w_up, v_ffn2_w_down, v_ln3_g, v_ln3_b):
    args = dict(locals())
    w = {k: args[k] for k in _WEIGHTS}
    L = ln1_g.shape[0]
    cx, cy, cc = _coords()
    chip = 2 * cx + cy

    shards = _weight_shards(w)
    names = list(shards)
    full = dict(zip(names, _gather_weights([shards[k] for k in names])))
    p = {k: w[k] for k in ("ln1_g", "ln1_b", "ln2_g", "ln2_b", "ln3_g", "ln3_b", "conv_b", "ssm_a_re", "ssm_a_im",
                           "ssm_log_dt", "ssm_b_re", "ssm_b_im", "ssm_c_re", "ssm_c_im", "ssm_d", "attn_sinks")}
    p["conv_w"], full["wglu"] = _gather_small_shards(conv_w, ssm_w_glu, chip)

    loss, grad_x, big, small = _local_step(x[0], loss_target[0], full, p, L)
    loss = lax.psum(loss[0, 0], ("x", "y", "c"))

    sizes = [math.prod(small[k].shape) for k in _SMALL_ORDER]
    pad = (-sum(sizes)) % 1024
    flat = jnp.concatenate([small[k].reshape(-1) for k in _SMALL_ORDER] + [jnp.zeros((pad,), F32)])
    flat = _all_reduce_small(flat.reshape(-1, 128), "small_grads").reshape(-1)
    sm, off = {}, 0
    for k, n in zip(_SMALL_ORDER, sizes):
        sm[k] = flat[off:off + n].reshape(small[k].shape)
        off += n
    red = _reduce_scatter([big[l][k] for l in range(L) for k in _BIG_ORDER])
    red = {k: jnp.stack([red[l * len(_BIG_ORDER) + i] for l in range(L)]) for i, k in enumerate(_BIG_ORDER)}
    tr = lambda a: jnp.swapaxes(a, 1, 2)
    grads = dict(sm)
    grads.update(
        ffn1_w_gate=tr(red["wg1"]), ffn1_w_up=tr(red["wu1"]), ffn1_w_down=red["wd1"], w_in=tr(red["win"]),
        w_br_conv=tr(red["wbr"][:, :, _BR[0][0]:_BR[0][1]]), w_br_ssm=tr(red["wbr"][:, :, _BR[1][0]:_BR[1][1]]),
        w_br_attn=tr(red["wbr"][:, :, _BR[2][0]:_BR[2][1]]), w_out=red["wout"], ffn2_w_gate=tr(red["wg2"]),
        ffn2_w_up=tr(red["wu2"]), ffn2_w_down=red["wd2"],
        ssm_w_glu=lax.dynamic_slice_in_dim(sm["wglu"], chip * 64, 64, axis=1),
        conv_w=lax.dynamic_slice_in_dim(sm["conv_w"], chip * 64, 64, axis=2))

    outs = [[], [], [], []]
    for k in _WEIGHTS:
        d, nm, nv = _adamw(w[k], grads[k], args["m_" + k], args["v_" + k], "adamw")
        for lst, val in zip(outs, (grads[k], d, nm, nv)):
            lst.append(val)
    return (loss, grad_x[None], *outs[0], *outs[1], *outs[2], *outs[3])
```

```python
import functools
import math

import jax
import jax.numpy as jnp
from jax import lax
from jax.experimental import pallas as pl
from jax.experimental.pallas import tpu as pltpu

F32 = jnp.float32
BF16 = jnp.bfloat16

LN_EPS = 1e-5
D_CONV = 256
D_SSM = 256
N_GROUPS = 16
SSM_GROUP = 16
SSM_STATE = 64
N_Q_HEADS = 8
HEAD_DIM = 64
D_ATTN = 512
D_KV = 128
BLOCK = 128
D_A = 3 * D_CONV + D_SSM + D_ATTN + 2 * D_KV
ADAM_LR = 0.001
ADAM_B1 = 0.9
ADAM_B2 = 0.999
ADAM_EPS = 1e-08
ADAM_WD = 0.01
ADAM_STEP = 10

VMEM_LIMIT_BYTES = 56 * 1024 * 1024
MESH = pl.DeviceIdType.MESH
N_CHIPS = 4
N_DEV = 8


def _params(sem=None):
    return pltpu.CompilerParams(dimension_semantics=sem, vmem_limit_bytes=VMEM_LIMIT_BYTES)


def _op(op, block, imap):
    if isinstance(op, tuple):
        arr, l = op
        return arr, pl.BlockSpec((None,) + block, lambda *g: (l,) + imap(*g))
    return op, pl.BlockSpec(block, imap)


def _call(body, *, grid, in_specs, out_specs, out_shape, scratch=(), sem=None, name, ops, comm=None):
    if comm is None:
        return pl.pallas_call(body, grid=grid, in_specs=in_specs, out_specs=out_specs, out_shape=out_shape,
                              scratch_shapes=list(scratch), compiler_params=_params(sem), name=name)(*ops), []
    build, cins, couts, n_sem = comm
    n_in, n_out, n_scr, nci, nco, ng = len(in_specs), len(out_specs), len(scratch), len(cins), len(couts), len(grid)
    hbm = pl.BlockSpec(memory_space=pltpu.HBM)
    aliases = {n_in + o: n_out + j for j, o in enumerate(couts) if isinstance(o, int)}
    cshapes = [jax.ShapeDtypeStruct(cins[o].shape, cins[o].dtype) if isinstance(o, int) else o for o in couts]

    def hosted(*refs):
        refs = list(refs)
        main = refs[:n_in] + refs[n_in + nci:n_in + nci + n_out] + refs[n_in + nci + n_out + nco:-2]
        crefs, co = refs[n_in:n_in + nci], refs[n_in + nci + n_out:n_in + nci + n_out + nco]
        send_sems, recv_sems = refs[-2:]
        fresh = []
        for j, o in enumerate(couts):
            if isinstance(o, int):
                crefs[o] = co[j]
            else:
                fresh.append(co[j])

        def start():
            for cp in build(crefs, fresh, send_sems, recv_sems)[0]:
                cp.start()

        def finish():
            sends, recvs = build(crefs, fresh, send_sems, recv_sems)
            for cp in recvs:
                cp.wait_recv()
            for cp in sends:
                cp.wait_send()

        if ng == 0:
            start()
            finish()
            return
        ids = [pl.program_id(d) for d in range(ng)]
        first = functools.reduce(jnp.logical_and, [ids[d] == 0 for d in range(ng)])
        last = functools.reduce(jnp.logical_and, [ids[d] == grid[d] - 1 for d in range(ng)])
        pl.when(first)(start)
        body(*main)
        pl.when(last)(finish)

    res = pl.pallas_call(
        hosted, grid=grid, in_specs=list(in_specs) + [hbm] * nci, out_specs=list(out_specs) + [hbm] * nco,
        out_shape=list(out_shape) + cshapes,
        scratch_shapes=list(scratch) + [pltpu.SemaphoreType.DMA((n_sem,)), pltpu.SemaphoreType.DMA((n_sem,))],
        input_output_aliases=aliases,
        compiler_params=pltpu.CompilerParams(dimension_semantics=("arbitrary",) * ng if ng else None,
                                             vmem_limit_bytes=VMEM_LIMIT_BYTES, has_side_effects=True),
        name=name)(*ops, *cins)
    return res[:n_out], res[n_out:]


def _mm(pairs, mode, m, n, k, out_dtypes, *, name, tm=512, tn=512, tk=512, add=None, a_off=0, b_off=0, comm=None):
    tm, tn, tk = min(tm, m), min(tn, n), min(tk, k)
    assert m % tm == 0 and n % tn == 0 and k % tk == 0 and b_off % tn == 0, (name, m, n, k, tm, tn, tk)
    nk, npair, jo = k // tk, len(pairs), b_off // tn
    if mode == "nn":
        ao = a_off // tk
        ab, ai, bb, bi = (tm, tk), (lambda i, j, kk: (i, kk + ao)), (tk, tn), (lambda i, j, kk: (kk, j))
        dims = (((1,), (0,)), ((), ()))
    elif mode == "nt":
        ao = a_off // tk
        ab, ai, bb, bi = (tm, tk), (lambda i, j, kk: (i, kk + ao)), (tn, tk), (lambda i, j, kk: (j + jo, kk))
        dims = (((1,), (1,)), ((), ()))
    else:
        ao = a_off // tm
        ab, ai, bb, bi = (tk, tm), (lambda i, j, kk: (kk, i + ao)), (tk, tn), (lambda i, j, kk: (kk, j))
        dims = (((0,), (0,)), ((), ()))
    assert a_off % (tm if mode == "tn" else tk) == 0, (name, a_off)
    ops, specs = [], []
    for a, b in pairs:
        for o, blk, im in ((a, ab, ai), (b, bb, bi)):
            arr, sp = _op(o, blk, im)
            ops.append(arr)
            specs.append(sp)
    has_add = add is not None
    if has_add:
        ops.append(add)
        specs.append(pl.BlockSpec((tm, tn), lambda i, j, kk: (i, j)))
    nout = len(out_dtypes)

    def body(*refs):
        outs, acc = refs[2 * npair + has_add:2 * npair + has_add + nout], refs[-1]
        kk = pl.program_id(2)
        t = None
        for p in range(npair):
            d = lax.dot_general(refs[2 * p][...].astype(BF16), refs[2 * p + 1][...].astype(BF16), dims,
                                preferred_element_type=F32)
            t = d if t is None else t + d

        def finish(r):
            if has_add:
                r = r + refs[2 * npair][...]
            for o in outs:
                o[...] = r.astype(o.dtype)

        if nk == 1:
            finish(t)
            return

        @pl.when(kk == 0)
        def _():
            acc[...] = jnp.zeros_like(acc)

        acc[...] += t

        @pl.when(kk == nk - 1)
        def _():
            finish(acc[...])

    res, couts = _call(
        body, grid=(m // tm, n // tn, nk), in_specs=specs,
        out_specs=[pl.BlockSpec((tm, tn), lambda i, j, kk: (i, j))] * nout,
        out_shape=[jax.ShapeDtypeStruct((m, n), dt) for dt in out_dtypes],
        scratch=[pltpu.VMEM((tm, tn), F32)], sem=("parallel", "parallel", "arbitrary"), name=name, ops=ops, comm=comm)
    res = res[0] if nout == 1 else res
    return res if comm is None else (res, couts)


def _rows(width, col=0, tm=None):
    return pl.BlockSpec((tm, width), lambda i: (i, col))


def _whole(shape):
    nd = len(shape)
    return pl.BlockSpec(shape, lambda i: (0,) * nd)


def _sigmoid(x):
    return 0.5 * jnp.tanh(0.5 * x) + 0.5


def _mm_ln(a, w, x, g, b, s, alpha, name, tk, comm=None):
    S, K = a.shape
    Dm = x.shape[1]
    tm, tk = min(512, S), min(tk, K)
    nk = K // tk

    def body(a_ref, w_ref, x_ref, g_ref, b_ref, y_ref, yb_ref, xh_ref, rs_ref, acc):
        kk = pl.program_id(1)
        d = jnp.dot(a_ref[...], w_ref[...], preferred_element_type=F32)

        if nk > 1:
            @pl.when(kk == 0)
            def _():
                acc[...] = d

            @pl.when((kk > 0) & (kk < nk - 1))
            def _():
                acc[...] += d

        @pl.when(kk == nk - 1)
        def _():
            z = alpha * x_ref[...] + s * (d if nk == 1 else acc[...] + d)
            mu = jnp.mean(z, axis=-1, keepdims=True)
            zc = z - mu
            var = jnp.mean(zc * zc, axis=-1, keepdims=True)
            rstd = lax.rsqrt(var + LN_EPS)
            xh = zc * rstd
            y = xh * g_ref[...] + b_ref[...]
            y_ref[...] = y
            yb_ref[...] = y.astype(BF16)
            xh_ref[...] = xh
            rs_ref[...] = rstd

    wa, ws = _op(w, (tk, Dm), lambda i, kk: (kk, 0))
    row = pl.BlockSpec((tm, Dm), lambda i, kk: (i, 0))
    vec = pl.BlockSpec((1, Dm), lambda i, kk: (0, 0))
    res, couts = _call(
        body, grid=(S // tm, nk),
        in_specs=[pl.BlockSpec((tm, tk), lambda i, kk: (i, kk)), ws, row, vec, vec],
        out_specs=[row, row, row, pl.BlockSpec((tm, 1), lambda i, kk: (i, 0))],
        out_shape=[jax.ShapeDtypeStruct((S, Dm), F32), jax.ShapeDtypeStruct((S, Dm), BF16),
                   jax.ShapeDtypeStruct((S, Dm), F32), jax.ShapeDtypeStruct((S, 1), F32)],
        scratch=[pltpu.VMEM((tm, Dm), F32)], sem=("parallel", "arbitrary"), name=name, ops=[a, wa, x, g, b],
        comm=comm)
    return res if comm is None else (res, couts)


def _ln_bwd(dy, xh, rs, g, s, alpha, name):
    S, Dm = dy.shape
    tm = min(256, S)

    def body(dy_ref, xh_ref, rs_ref, g_ref, dres_ref, dbr_ref, dg_ref, db_ref):
        @pl.when(pl.program_id(0) == 0)
        def _():
            dg_ref[...] = jnp.zeros_like(dg_ref)
            db_ref[...] = jnp.zeros_like(db_ref)

        dy, xh = dy_ref[...], xh_ref[...]
        dyg = dy * g_ref[...]
        m1 = jnp.mean(dyg, axis=-1, keepdims=True)
        m2 = jnp.mean(dyg * xh, axis=-1, keepdims=True)
        dz = rs_ref[...] * (dyg - m1 - xh * m2)
        dres_ref[...] = alpha * dz
        dbr_ref[...] = (s * dz).astype(BF16)
        dg_ref[...] += jnp.sum(dy * xh, axis=0, keepdims=True)
        db_ref[...] += jnp.sum(dy, axis=0, keepdims=True)

    return pl.pallas_call(
        body, grid=(S // tm,),
        in_specs=[_rows(Dm, tm=tm), _rows(Dm, tm=tm), _rows(1, tm=tm), _whole((1, Dm))],
        out_specs=[_rows(Dm, tm=tm), _rows(Dm, tm=tm), _whole((1, Dm)), _whole((1, Dm))],
        out_shape=[jax.ShapeDtypeStruct((S, Dm), F32), jax.ShapeDtypeStruct((S, Dm), BF16),
                   jax.ShapeDtypeStruct((1, Dm), F32), jax.ShapeDtypeStruct((1, Dm), F32)],
        compiler_params=_params(("arbitrary",)), name=name)(dy, xh, rs, g)


def _ffn_up(x, wg, wu, dff, name, comm=None):
    S, Dm = x.shape
    tm, tn = min(512, S), dff // 2

    def body(x_ref, wg_ref, wu_ref, a_ref, b_ref, h_ref):
        xb = x_ref[...]
        dims = (((1,), (1,)), ((), ()))
        a = lax.dot_general(xb, wg_ref[...], dims, preferred_element_type=F32)
        b = lax.dot_general(xb, wu_ref[...], dims, preferred_element_type=F32)
        a_ref[...] = a.astype(BF16)
        b_ref[...] = b.astype(BF16)
        h_ref[...] = (a * _sigmoid(a) * b).astype(BF16)

    wga, wgs = _op(wg, (tn, Dm), lambda i, j: (j, 0))
    wua, wus = _op(wu, (tn, Dm), lambda i, j: (j, 0))
    ob = pl.BlockSpec((tm, tn), lambda i, j: (i, j))
    res, couts = _call(
        body, grid=(S // tm, dff // tn),
        in_specs=[pl.BlockSpec((tm, Dm), lambda i, j: (i, 0)), wgs, wus], out_specs=[ob, ob, ob],
        out_shape=[jax.ShapeDtypeStruct((S, dff), BF16)] * 3, sem=("parallel", "parallel"), name=name,
        ops=[x, wga, wua], comm=comm)
    return res if comm is None else (res, couts)


def _ffn_dh(df, wd, a, b, name):
    S, Dm = df.shape
    dff = a.shape[1]
    tm, tn = min(512, S), dff // 2

    def body(df_ref, wd_ref, a_ref, b_ref, da_ref, db_ref):
        dh = lax.dot_general(df_ref[...], wd_ref[...], (((1,), (1,)), ((), ())), preferred_element_type=F32)
        a, b = a_ref[...].astype(F32), b_ref[...].astype(F32)
        sg = _sigmoid(a)
        da_ref[...] = (dh * b * (sg * (1.0 + a * (1.0 - sg)))).astype(BF16)
        db_ref[...] = (dh * (a * sg)).astype(BF16)

    wda, wds = _op(wd, (tn, Dm), lambda i, j: (j, 0))
    ob = pl.BlockSpec((tm, tn), lambda i, j: (i, j))
    return pl.pallas_call(
        body, grid=(S // tm, dff // tn),
        in_specs=[pl.BlockSpec((tm, Dm), lambda i, j: (i, 0)), wds, ob, ob], out_specs=[ob, ob],
        out_shape=[jax.ShapeDtypeStruct((S, dff), BF16), jax.ShapeDtypeStruct((S, dff), BF16)],
        compiler_params=_params(("parallel", "parallel")), name=name)(df, wda, a, b)


def _halo_prev(width, col, tm):
    return pl.BlockSpec((8, width), lambda i: (jnp.maximum(i * (tm // 8) - 1, 0), col))


def _halo_next(width, col, tm, S):
    return pl.BlockSpec((8, width), lambda i: (jnp.minimum((i + 1) * (tm // 8), S // 8 - 1), col))


def _shift_down(prev8, cur, n):
    ext = jnp.concatenate([prev8, cur], axis=0)
    return pltpu.roll(ext, n, axis=0)[8:]


def _shift_up(cur, next8, n):
    ext = jnp.concatenate([cur, next8], axis=0)
    return pltpu.roll(ext, ext.shape[0] - n, axis=0)[:cur.shape[0]]


def _conv_fwd(proj_a, conv_w, conv_b, name):
    S = proj_a.shape[0]
    tm = min(512, S)
    C = D_CONV

    def body(bg_ref, cg_ref, h_ref, cgp_ref, hp_ref, w_ref, cb_ref, out_ref):
        z = cg_ref[...] * h_ref[...]
        zp = jnp.where(pl.program_id(0) > 0, cgp_ref[...] * hp_ref[...], 0.0)
        w = w_ref[...]
        y = w[2:3] * z + w[1:2] * _shift_down(zp, z, 1) + w[0:1] * _shift_down(zp, z, 2) + cb_ref[...]
        out_ref[...] = (bg_ref[...] * y).astype(BF16)

    return pl.pallas_call(
        body, grid=(S // tm,),
        in_specs=[_rows(C, 0, tm), _rows(C, 1, tm), _rows(C, 2, tm), _halo_prev(C, 1, tm), _halo_prev(C, 2, tm),
                  _whole((3, C)), _whole((1, C))],
        out_specs=_rows(C, 0, tm), out_shape=jax.ShapeDtypeStruct((S, C), BF16),
        compiler_params=_params(("parallel",)), name=name)(proj_a, proj_a, proj_a, proj_a, proj_a, conv_w, conv_b)


def _conv_bwd(proj_a, dmc, conv_w, conv_b, name):
    S = proj_a.shape[0]
    tm = min(512, S)
    C = D_CONV
    nblk = S // tm

    def body(bg_ref, cg_ref, h_ref, cgp_ref, hp_ref, bgn_ref, d_ref, dn_ref, w_ref, cb_ref, out_ref, dw_ref,
             dcb_ref):
        i = pl.program_id(0)

        @pl.when(i == 0)
        def _():
            dw_ref[...] = jnp.zeros_like(dw_ref)
            dcb_ref[...] = jnp.zeros_like(dcb_ref)

        bg, cg, h, d = bg_ref[...], cg_ref[...], h_ref[...], d_ref[...]
        z = cg * h
        zp = jnp.where(i > 0, cgp_ref[...] * hp_ref[...], 0.0)
        w = w_ref[...]
        z1, z2 = _shift_down(zp, z, 1), _shift_down(zp, z, 2)
        y = w[2:3] * z + w[1:2] * z1 + w[0:1] * z2 + cb_ref[...]
        dy = d * bg
        dyn = jnp.where(i < nblk - 1, dn_ref[...] * bgn_ref[...], 0.0)
        dz = w[2:3] * dy + w[1:2] * _shift_up(dy, dyn, 1) + w[0:1] * _shift_up(dy, dyn, 2)
        out_ref[:, 0:C] = (d * y).astype(BF16)
        out_ref[:, C:2 * C] = (dz * h).astype(BF16)
        out_ref[:, 2 * C:3 * C] = (dz * cg).astype(BF16)
        dw_ref[0:1, :] += jnp.sum(dy * z2, axis=0, keepdims=True)
        dw_ref[1:2, :] += jnp.sum(dy * z1, axis=0, keepdims=True)
        dw_ref[2:3, :] += jnp.sum(dy * z, axis=0, keepdims=True)
        dcb_ref[...] += jnp.sum(dy, axis=0, keepdims=True)

    return pl.pallas_call(
        body, grid=(nblk,),
        in_specs=[_rows(C, 0, tm), _rows(C, 1, tm), _rows(C, 2, tm), _halo_prev(C, 1, tm), _halo_prev(C, 2, tm),
                  _halo_next(C, 0, tm, S), _rows(C, 0, tm), _halo_next(C, 0, tm, S), _whole((3, C)), _whole((1, C))],
        out_specs=[_rows(3 * C, 0, tm), _whole((3, C)), _whole((1, C))],
        out_shape=[jax.ShapeDtypeStruct((S, 3 * C), BF16), jax.ShapeDtypeStruct((3, C), F32),
                   jax.ShapeDtypeStruct((1, C), F32)],
        compiler_params=_params(("arbitrary",)), name=name)(
            proj_a, proj_a, proj_a, proj_a, proj_a, proj_a, dmc, dmc, conv_w, conv_b)


def _disc_math(lr, li, ldt, br, bi):
    dt = jnp.exp(ldt)
    mag = jnp.exp(lr * dt)
    ang = li * dt
    are = mag * jnp.cos(ang)
    aim = mag * jnp.sin(ang)
    nr = are - 1.0
    den = lr * lr + li * li
    cre = (nr * lr + aim * li) / den
    cim = (aim * lr - nr * li) / den
    return are, aim, cre * br - cim * bi, cre * bi + cim * br


def _disc_fwd(lr, li, ldt, br, bi):
    shapes = [lr.shape, lr.shape, br.shape, br.shape]

    def body(lr_ref, li_ref, ldt_ref, br_ref, bi_ref, *outs):
        for o, v in zip(outs, _disc_math(lr_ref[...], li_ref[...], ldt_ref[...], br_ref[...], bi_ref[...])):
            o[...] = v

    return pl.pallas_call(body, out_shape=[jax.ShapeDtypeStruct(s, F32) for s in shapes],
                          compiler_params=_params(), name="ssm_disc")(lr, li, ldt, br, bi)


def _disc_bwd(lr, li, ldt, br, bi, cts):
    shapes = [lr.shape, lr.shape, ldt.shape, br.shape, br.shape]

    def body(lr_ref, li_ref, ldt_ref, br_ref, bi_ref, c0, c1, c2, c3, *outs):
        _, vjp = jax.vjp(_disc_math, lr_ref[...], li_ref[...], ldt_ref[...], br_ref[...], bi_ref[...])
        for o, v in zip(outs, vjp((c0[...], c1[...], c2[...], c3[...]))):
            o[...] = v

    return pl.pallas_call(body, out_shape=[jax.ShapeDtypeStruct(s, F32) for s in shapes],
                          compiler_params=_params(), name="ssm_disc_bwd")(lr, li, ldt, br, bi, *cts)


def _scan_fwd(bu, abar, name):
    S = bu.shape[0]
    tb = min(256, S)

    def body(bu_ref, a_ref, xs_ref, st_ref):
        @pl.when(pl.program_id(0) == 0)
        def _():
            st_ref[...] = jnp.zeros_like(st_ref)

        ar, ai = a_ref[0:8, :], a_ref[8:16, :]

        def step(t, c):
            xr, xi = c
            nr = ar * xr - ai * xi + bu_ref[t, 0:8, :]
            ni = ar * xi + ai * xr + bu_ref[t, 8:16, :]
            xs_ref[t, 0:8, :] = nr
            xs_ref[t, 8:16, :] = ni
            return nr, ni

        xr, xi = lax.fori_loop(0, tb, step, (st_ref[0:8, :], st_ref[8:16, :]), unroll=8)
        st_ref[0:8, :] = xr
        st_ref[8:16, :] = xi

    blk = pl.BlockSpec((tb, 16, 128), lambda i: (i, 0, 0))
    return pl.pallas_call(
        body, grid=(S // tb,), in_specs=[blk, _whole((16, 128))], out_specs=blk,
        out_shape=jax.ShapeDtypeStruct((S, 16, 128), F32), scratch_shapes=[pltpu.VMEM((16, 128), F32)],
        compiler_params=_params(("arbitrary",)), name=name)(bu, abar)


def _scan_bwd(dxs, xs, abar, name):
    S = dxs.shape[0]
    tb = min(256, S)
    nblk = S // tb

    def body(d_ref, x_ref, xp_ref, a_ref, lam_ref, da_ref, st_ref):
        i = pl.program_id(0)

        @pl.when(i == 0)
        def _():
            st_ref[...] = jnp.zeros_like(st_ref)
            da_ref[...] = jnp.zeros_like(da_ref)

        ar, ai = a_ref[0:8, :], a_ref[8:16, :]

        def one(t, c, pr, pi):
            lr, li, gr, gi = c
            nr = d_ref[t, 0:8, :] + ar * lr + ai * li
            ni = d_ref[t, 8:16, :] - ai * lr + ar * li
            lam_ref[t, 0:8, :] = nr
            lam_ref[t, 8:16, :] = ni
            return nr, ni, gr + nr * pr + ni * pi, gi - nr * pi + ni * pr

        def step(s, c):
            t = tb - 1 - s
            return one(t, c, x_ref[t - 1, 0:8, :], x_ref[t - 1, 8:16, :])

        c = (st_ref[0:8, :], st_ref[8:16, :], jnp.zeros((8, 128), F32), jnp.zeros((8, 128), F32))
        c = lax.fori_loop(0, tb - 1, step, c, unroll=8)
        first = i == nblk - 1
        pr = jnp.where(first, 0.0, xp_ref[0, 0:8, :])
        pi = jnp.where(first, 0.0, xp_ref[0, 8:16, :])
        lr, li, gr, gi = one(0, c, pr, pi)
        st_ref[0:8, :] = lr
        st_ref[8:16, :] = li
        da_ref[0:8, :] += gr
        da_ref[8:16, :] += gi

    blk = pl.BlockSpec((tb, 16, 128), lambda i: (nblk - 1 - i, 0, 0))
    prev = pl.BlockSpec((1, 16, 128), lambda i: (jnp.maximum((nblk - 1 - i) * tb - 1, 0), 0, 0))
    return pl.pallas_call(
        body, grid=(nblk,), in_specs=[blk, blk, prev, _whole((16, 128))], out_specs=[blk, _whole((16, 128))],
        out_shape=[jax.ShapeDtypeStruct((S, 16, 128), F32), jax.ShapeDtypeStruct((16, 128), F32)],
        scratch_shapes=[pltpu.VMEM((16, 128), F32)],
        compiler_params=_params(("arbitrary",)), name=name)(dxs, xs, xs, abar)


def _gelu(x):
    return 0.5 * x * (1.0 + jnp.tanh(0.7978845608028654 * (x + 0.044715 * x * x * x)))


def _gelu_grad(x):
    t = jnp.tanh(0.7978845608028654 * (x + 0.044715 * x * x * x))
    return 0.5 * (1.0 + t) + 0.5 * x * (1.0 - t * t) * 0.7978845608028654 * (1.0 + 3.0 * 0.044715 * x * x)


def _ssm_out(xs, proj_a, cfull, dskip, wglu, name):
    S = xs.shape[0]
    tm = min(256, S)
    C = D_SSM

    def body(xs_ref, u_ref, c_ref, d_ref, wg_ref, y1_ref, ms_ref):
        y1 = jnp.dot(xs_ref[...].astype(BF16), c_ref[...], preferred_element_type=F32) + d_ref[...] * u_ref[...]
        y2 = _gelu(y1)
        gl = jnp.dot(y2.astype(BF16), wg_ref[...], preferred_element_type=F32)
        y1_ref[...] = y1
        ms_ref[...] = (y2 * _sigmoid(gl)).astype(BF16)

    wga, wgs = _op(wglu, (C, C), lambda i: (0, 0))
    return pl.pallas_call(
        body, grid=(S // tm,),
        in_specs=[_rows(2 * 1024, 0, tm), _rows(C, 3, tm), _whole((2 * 1024, C)), _whole((1, C)), wgs],
        out_specs=[_rows(C, 0, tm), _rows(C, 0, tm)],
        out_shape=[jax.ShapeDtypeStruct((S, C), F32), jax.ShapeDtypeStruct((S, C), BF16)],
        compiler_params=_params(("parallel",)), name=name)(xs, proj_a, cfull, dskip, wga)


def _ssm_out_bwd(dms, y1, proj_a, cfull, dskip, wglu, name):
    S = y1.shape[0]
    tm = min(256, S)
    C = D_SSM

    def body(dms_ref, y1_ref, u_ref, c_ref, d_ref, wg_ref, dy1_ref, y2_ref, dgl_ref, dxs_ref, du_ref, dd_ref):
        @pl.when(pl.program_id(0) == 0)
        def _():
            dd_ref[...] = jnp.zeros_like(dd_ref)

        dms, y1 = dms_ref[...], y1_ref[...]
        y2 = _gelu(y1)
        y2b = y2.astype(BF16)
        sg = _sigmoid(jnp.dot(y2b, wg_ref[...], preferred_element_type=F32))
        dgl = (dms * y2 * sg * (1.0 - sg)).astype(BF16)
        dy2 = dms * sg + lax.dot_general(dgl, wg_ref[...], (((1,), (1,)), ((), ())), preferred_element_type=F32)
        dy1 = dy2 * _gelu_grad(y1)
        dy1b = dy1.astype(BF16)
        dy1_ref[...] = dy1b
        y2_ref[...] = y2b
        dgl_ref[...] = dgl
        dxs_ref[...] = lax.dot_general(dy1b, c_ref[...], (((1,), (1,)), ((), ())), preferred_element_type=F32)
        du_ref[...] = d_ref[...] * dy1
        dd_ref[...] += jnp.sum(dy1 * u_ref[...], axis=0, keepdims=True)

    wga, wgs = _op(wglu, (C, C), lambda i: (0, 0))
    rc = _rows(C, 0, tm)
    return pl.pallas_call(
        body, grid=(S // tm,),
        in_specs=[rc, rc, _rows(C, 3, tm), _whole((2 * 1024, C)), _whole((1, C)), wgs],
        out_specs=[rc, rc, rc, _rows(2 * 1024, 0, tm), rc, _whole((1, C))],
        out_shape=[jax.ShapeDtypeStruct((S, C), BF16), jax.ShapeDtypeStruct((S, C), BF16),
                   jax.ShapeDtypeStruct((S, C), BF16), jax.ShapeDtypeStruct((S, 2 * 1024), F32),
                   jax.ShapeDtypeStruct((S, C), F32), jax.ShapeDtypeStruct((1, C), F32)],
        compiler_params=_params(("arbitrary",)), name=name)(dms, y1, proj_a, cfull, dskip, wga)


_NT = (((1,), (1,)), ((), ()))
_TN = (((0,), (0,)), ((), ()))


def _attn_heads(q_ref, kp_ref, kc_ref, vp_ref, vc_ref):
    lane = lax.broadcasted_iota(jnp.int32, (BLOCK, 128), 1)
    kk = jnp.concatenate([kp_ref[...], kc_ref[...]], axis=0).astype(BF16)
    vv = jnp.concatenate([vp_ref[...], vc_ref[...]], axis=0).astype(BF16)
    kk_r, vv_r = pltpu.roll(kk, 64, axis=1), pltpu.roll(vv, 64, axis=1)
    heads = []
    for hq in range(N_Q_HEADS):
        j, e = hq // 2, hq % 2
        qj = (q_ref[:, 128 * j:128 * (j + 1)] * (HEAD_DIM ** -0.5)).astype(BF16)
        own = (lane >= 64) if e else (lane < 64)
        aligned = e == hq // 4
        heads.append((own, jnp.where(own, qj, jnp.zeros_like(qj)), kk if aligned else kk_r, vv if aligned else vv_r,
                      aligned))
    return heads


def _attn_probs(i, heads, s_ref):
    n = N_Q_HEADS * BLOCK
    s = jnp.concatenate([lax.dot_general(qm, ks, _NT, preferred_element_type=F32) for _, qm, ks, _, _ in heads],
                        axis=0)
    row = lax.broadcasted_iota(jnp.int32, (n, 2 * BLOCK), 0) & (BLOCK - 1)
    col = lax.broadcasted_iota(jnp.int32, (n, 2 * BLOCK), 1)
    mask = (col > row) & (col <= row + BLOCK) & ((col >= BLOCK) | (i > 0))
    s = jnp.where(mask, s, -1e30)
    sink = jnp.concatenate([jnp.full((BLOCK, 1), s_ref[0, hq], F32) for hq in range(N_Q_HEADS)], axis=0)
    m = jnp.maximum(jnp.max(s, axis=1, keepdims=True), sink)
    p = jnp.exp(s - m)
    es = jnp.exp(sink - m)
    inv = 1.0 / (jnp.sum(p, axis=1, keepdims=True) + es)
    return p * inv, es * inv


def _attn_fwd(proj_a, sinks, name):
    S = proj_a.shape[0]
    nb = S // BLOCK

    def body(q_ref, kp_ref, kc_ref, vp_ref, vc_ref, s_ref, out_ref):
        i = pl.program_id(0)
        heads = _attn_heads(q_ref, kp_ref, kc_ref, vp_ref, vc_ref)
        row = lax.broadcasted_iota(jnp.int32, (BLOCK, 2 * BLOCK), 0)
        col = lax.broadcasted_iota(jnp.int32, (BLOCK, 2 * BLOCK), 1)
        mask = (col > row) & (col <= row + BLOCK) & ((col >= BLOCK) | (i > 0))
        outs = []
        for hq, (_, qm, ks, vs, _) in enumerate(heads):
            s = jnp.where(mask, lax.dot_general(qm, ks, _NT, preferred_element_type=F32), -1e30)
            sink = s_ref[0, hq]
            m = jnp.maximum(jnp.max(s, axis=1, keepdims=True), sink)
            p = jnp.exp(s - m)
            inv = 1.0 / (jnp.sum(p, axis=1, keepdims=True) + jnp.exp(sink - m))
            outs.append(jnp.dot((p * inv).astype(BF16), vs, preferred_element_type=F32))
        for j in range(4):
            out_ref[:, 128 * j:128 * (j + 1)] = jnp.where(heads[2 * j][0], outs[2 * j], outs[2 * j + 1]).astype(BF16)

    prev = lambda c: pl.BlockSpec((BLOCK, 128), lambda i: (jnp.maximum(i - 1, 0), c))
    cur = lambda c: pl.BlockSpec((BLOCK, 128), lambda i: (i, c))
    return pl.pallas_call(
        body, grid=(nb,),
        in_specs=[_rows(D_ATTN, 2, BLOCK), prev(12), cur(12), prev(13), cur(13),
                  pl.BlockSpec(memory_space=pltpu.SMEM)],
        out_specs=_rows(D_ATTN, 0, BLOCK), out_shape=jax.ShapeDtypeStruct((S, D_ATTN), BF16),
        compiler_params=_params(("parallel",)), name=name)(proj_a, proj_a, proj_a, proj_a, proj_a, sinks)


def _attn_bwd(proj_a, dout, sinks, name):
    S = proj_a.shape[0]
    nb = S // BLOCK

    def body(q_ref, kp_ref, kc_ref, vp_ref, vc_ref, do_ref, s_ref, out_ref, dk_ref, dv_ref, ds_ref, ck_ref, cv_ref):
        i = pl.program_id(0)

        @pl.when(i == 0)
        def _():
            ds_ref[...] = jnp.zeros_like(ds_ref)
            ck_ref[...] = jnp.zeros_like(ck_ref)
            cv_ref[...] = jnp.zeros_like(cv_ref)

        @pl.when(i < nb)
        def _():
            heads = _attn_heads(q_ref, kp_ref, kc_ref, vp_ref, vc_ref)
            pn, psink = _attn_probs(i, heads, s_ref)
            doms = []
            for hq, (own, _, _, _, _) in enumerate(heads):
                doj = do_ref[:, 128 * (hq // 2):128 * (hq // 2 + 1)].astype(BF16)
                doms.append(jnp.where(own, doj, jnp.zeros_like(doj)))
            dp = jnp.concatenate([lax.dot_general(dom, vs, _NT, preferred_element_type=F32)
                                  for dom, (_, _, _, vs, _) in zip(doms, heads)], axis=0)
            delta = jnp.sum(pn * dp, axis=1, keepdims=True)
            dsb = (pn * (dp - delta)).astype(BF16)
            pnb = pn.astype(BF16)
            sk = psink * delta
            dkk = jnp.zeros((2 * BLOCK, 128), F32)
            dvv = jnp.zeros((2 * BLOCK, 128), F32)
            dqs = []
            for hq, (own, qm, ks, vs, aligned) in enumerate(heads):
                rows = slice(BLOCK * hq, BLOCK * (hq + 1))
                ds_ref[hq:hq + 1, :] += jnp.broadcast_to(-jnp.sum(sk[rows]), (1, 128))
                dqs.append(jnp.dot(dsb[rows], ks, preferred_element_type=F32) * (HEAD_DIM ** -0.5))
                dk = lax.dot_general(dsb[rows], qm, _TN, preferred_element_type=F32)
                dv = lax.dot_general(pnb[rows], doms[hq], _TN, preferred_element_type=F32)
                dkk = dkk + (dk if aligned else pltpu.roll(dk, 64, axis=1))
                dvv = dvv + (dv if aligned else pltpu.roll(dv, 64, axis=1))
            for j in range(4):
                out_ref[:, 128 * j:128 * (j + 1)] = jnp.where(heads[2 * j][0], dqs[2 * j], dqs[2 * j + 1]).astype(BF16)
            ck_ref[0:BLOCK, :] = ck_ref[BLOCK:, :] + dkk[0:BLOCK]
            cv_ref[0:BLOCK, :] = cv_ref[BLOCK:, :] + dvv[0:BLOCK]
            ck_ref[BLOCK:, :] = dkk[BLOCK:]
            cv_ref[BLOCK:, :] = dvv[BLOCK:]

        @pl.when(i == nb)
        def _():
            ck_ref[0:BLOCK, :] = ck_ref[BLOCK:, :]
            cv_ref[0:BLOCK, :] = cv_ref[BLOCK:, :]

        dk_ref[...] = ck_ref[0:BLOCK, :].astype(BF16)
        dv_ref[...] = cv_ref[0:BLOCK, :].astype(BF16)

    last = nb - 1
    prev = lambda c: pl.BlockSpec((BLOCK, 128), lambda i: (jnp.clip(i - 1, 0, last), c))
    cur = lambda c: pl.BlockSpec((BLOCK, 128), lambda i: (jnp.minimum(i, last), c))
    qrow = lambda w, c: pl.BlockSpec((BLOCK, w), lambda i: (jnp.minimum(i, last), c))

    dq, dk, dv, ds = pl.pallas_call(
        body, grid=(nb + 1,),
        in_specs=[qrow(D_ATTN, 2), prev(12), cur(12), prev(13), cur(13), qrow(D_ATTN, 0),
                  pl.BlockSpec(memory_space=pltpu.SMEM)],
        out_specs=[qrow(D_ATTN, 0), prev(0), prev(0), _whole((N_Q_HEADS, 128))],
        out_shape=[jax.ShapeDtypeStruct((S, D_ATTN), BF16), jax.ShapeDtypeStruct((S, D_KV), BF16),
                   jax.ShapeDtypeStruct((S, D_KV), BF16), jax.ShapeDtypeStruct((N_Q_HEADS, 128), F32)],
        scratch_shapes=[pltpu.VMEM((2 * BLOCK, 128), F32), pltpu.VMEM((2 * BLOCK, 128), F32)],
        compiler_params=_params(("arbitrary",)), name=name)(proj_a, proj_a, proj_a, proj_a, proj_a, dout, sinks)
    return dq, dk, dv, ds


_BR = ((0, D_CONV), (D_CONV, D_CONV + D_SSM), (D_CONV + D_SSM, D_CONV + D_SSM + D_ATTN))


def _branches(m_refs, wbr_ref):
    nt = (((1,), (1,)), ((), ()))
    return [lax.dot_general(m[...], wbr_ref[:, lo:hi], nt, preferred_element_type=F32)
            for m, (lo, hi) in zip(m_refs, _BR)]


def _merge_fwd(mc, ms, ma, proj_g, wbr_t, name):
    S, Dm = mc.shape[0], proj_g.shape[1] // 3
    tm = min(256, S)

    def body(mc_ref, ms_ref, ma_ref, g_ref, w_ref, out_ref):
        ys = _branches((mc_ref, ms_ref, ma_ref), w_ref)
        acc = None
        for b in range(3):
            t = _sigmoid(g_ref[:, b * Dm:(b + 1) * Dm]) * ys[b]
            acc = t if acc is None else acc + t
        out_ref[...] = acc.astype(BF16)

    wa, ws = _op(wbr_t, (Dm, Dm), lambda i: (0, 0))
    return pl.pallas_call(
        body, grid=(S // tm,),
        in_specs=[_rows(D_CONV, 0, tm), _rows(D_SSM, 0, tm), _rows(D_ATTN, 0, tm), _rows(3 * Dm, 0, tm), ws],
        out_specs=_rows(Dm, 0, tm), out_shape=jax.ShapeDtypeStruct((S, Dm), BF16),
        compiler_params=_params(("parallel",)), name=name)(mc, ms, ma, proj_g, wa)


def _merge_bwd(dmerged, mc, ms, ma, proj_g, wbr_t, name):
    S, Dm = mc.shape[0], proj_g.shape[1] // 3
    tm = min(256, S)

    def body(d_ref, mc_ref, ms_ref, ma_ref, g_ref, w_ref, dg_ref, dy_ref, dmc_ref, dms_ref, dma_ref):
        ys = _branches((mc_ref, ms_ref, ma_ref), w_ref)
        d = d_ref[...]
        for b, (o_ref, (lo, hi)) in enumerate(zip((dmc_ref, dms_ref, dma_ref), _BR)):
            g = _sigmoid(g_ref[:, b * Dm:(b + 1) * Dm])
            dg_ref[:, b * Dm:(b + 1) * Dm] = (d * ys[b] * g * (1.0 - g)).astype(BF16)
            dyb = (g * d).astype(BF16)
            dy_ref[:, b * Dm:(b + 1) * Dm] = dyb
            o_ref[...] = jnp.dot(dyb, w_ref[:, lo:hi], preferred_element_type=F32)

    wa, ws = _op(wbr_t, (Dm, Dm), lambda i: (0, 0))
    return pl.pallas_call(
        body, grid=(S // tm,),
        in_specs=[_rows(Dm, 0, tm), _rows(D_CONV, 0, tm), _rows(D_SSM, 0, tm), _rows(D_ATTN, 0, tm),
                  _rows(3 * Dm, 0, tm), ws],
        out_specs=[_rows(3 * Dm, 0, tm), _rows(3 * Dm, 0, tm), _rows(D_CONV, 0, tm), _rows(D_SSM, 0, tm),
                   _rows(D_ATTN, 0, tm)],
        out_shape=[jax.ShapeDtypeStruct((S, 3 * Dm), BF16), jax.ShapeDtypeStruct((S, 3 * Dm), BF16),
                   jax.ShapeDtypeStruct((S, D_CONV), F32), jax.ShapeDtypeStruct((S, D_SSM), F32),
                   jax.ShapeDtypeStruct((S, D_ATTN), F32)],
        compiler_params=_params(("parallel",)), name=name)(dmerged, mc, ms, ma, proj_g, wa)


def _loss_head(y, target):
    S, Dm = y.shape
    tm = min(512, S)

    def body(y_ref, t_ref, dy_ref, l_ref):
        @pl.when(pl.program_id(0) == 0)
        def _():
            l_ref[...] = jnp.zeros_like(l_ref)

        e = y_ref[...] - t_ref[...]
        dy_ref[...] = e * (1.0 / Dm)
        l_ref[...] += jnp.broadcast_to(0.5 * jnp.sum(jnp.sum(e * e, axis=1, keepdims=True) * (1.0 / Dm)), (1, 128))

    return pl.pallas_call(
        body, grid=(S // tm,), in_specs=[_rows(Dm, 0, tm), _rows(Dm, 0, tm)],
        out_specs=[_rows(Dm, 0, tm), _whole((1, 128))],
        out_shape=[jax.ShapeDtypeStruct((S, Dm), F32), jax.ShapeDtypeStruct((1, 128), F32)],
        compiler_params=_params(("arbitrary",)), name="loss_head")(y, target)


def _view2d(shape):
    n = math.prod(shape)
    if shape[-1] % 128 == 0:
        return (n // shape[-1], shape[-1])
    if n >= (1 << 16) and len(shape) == 3:
        return (shape[0] * shape[1], shape[2])
    if n % 128 == 0:
        return (n // 128, 128)
    return (1, n)


def _adamw(w, g, m, v, name):
    shape = w.shape
    R, C = _view2d(shape)
    tm = R
    for cand in (512, 352, 256):
        if R > cand and R % cand == 0:
            tm = cand
            break
    c1 = 1.0 - ADAM_B1 ** ADAM_STEP
    c2 = 1.0 - ADAM_B2 ** ADAM_STEP

    def body(w_ref, g_ref, m_ref, v_ref, d_ref, nm_ref, nv_ref):
        g = g_ref[...]
        nm = ADAM_B1 * m_ref[...] + (1.0 - ADAM_B1) * g
        nv = ADAM_B2 * v_ref[...] + (1.0 - ADAM_B2) * (g * g)
        d_ref[...] = -ADAM_LR * ((nm / c1) / (jnp.sqrt(nv / c2) + ADAM_EPS) + ADAM_WD * w_ref[...])
        nm_ref[...] = nm
        nv_ref[...] = nv

    blk = _rows(C, 0, tm)
    outs = pl.pallas_call(
        body, grid=(R // tm,), in_specs=[blk] * 4, out_specs=[blk] * 3,
        out_shape=[jax.ShapeDtypeStruct((R, C), F32)] * 3,
        compiler_params=_params(("parallel",)), name=name)(*[t.reshape(R, C) for t in (w, g, m, v)])
    return [o.reshape(shape) for o in outs]


def _coords():
    return lax.axis_index("x"), lax.axis_index("y"), lax.axis_index("c")


def _other_chips(x, y):
    return [((1 - x, y), 2 * (1 - x) + y), ((x, 1 - y), 2 * x + 1 - y), ((1 - x, 1 - y), 2 * (1 - x) + 1 - y)]


def _comm_call(comm, name):
    return _call(None, grid=(), in_specs=[], out_specs=[], out_shape=[], name=name, ops=[], comm=comm)[1]


def _remote(src, dst, send_sems, recv_sems, k, dev):
    return pltpu.make_async_remote_copy(src_ref=src, dst_ref=dst, send_sem=send_sems.at[k], recv_sem=recv_sems.at[k],
                                        device_id=dev, device_id_type=MESH)


def _half(ref, chip, c, rp):
    return ref.at[pl.ds(pl.multiple_of(chip * rp + c * (rp // 2), 16), rp // 2), :]


def _gather_ici(shards, fulls, l):
    n = len(shards)

    def build(refs, fresh, send_sems, recv_sems):
        x, y, c = _coords()
        me = 2 * x + y
        sends, recvs = [], []
        for a in range(n):
            rp = shards[a].shape[1]
            src = refs[a].at[l].at[pl.ds(pl.multiple_of(c * (rp // 2), 16), rp // 2), :]
            for r, ((px, py), chip) in enumerate(_other_chips(x, y)):
                k = 3 * a + r
                sends.append(_remote(src, _half(refs[n + a], me, c, rp), send_sems, recv_sems, k, (px, py, c)))
                recvs.append(_remote(src, _half(refs[n + a], chip, c, rp), send_sems, recv_sems, k, (px, py, c)))
        return sends, recvs

    return build, list(shards) + list(fulls), [n + a for a in range(n)], 3 * n


def _gather_d2d(fulls):
    n = len(fulls)

    def build(refs, fresh, send_sems, recv_sems):
        x, y, c = _coords()
        sends, recvs = [], []
        for a in range(n):
            rp = fulls[a].shape[0] // N_CHIPS
            for r, (_, chip) in enumerate(_other_chips(x, y)):
                k = 3 * a + r
                mine, theirs = _half(refs[a], chip, c, rp), _half(refs[a], chip, 1 - c, rp)
                sends.append(_remote(mine, mine, send_sems, recv_sems, k, (x, y, 1 - c)))
                recvs.append(_remote(theirs, theirs, send_sems, recv_sems, k, (x, y, 1 - c)))
        return sends, recvs

    return build, list(fulls), list(range(n)), 3 * n


def _pair_sum(a, b, c_idx, half_rows, out_dtype, name):
    n4, rp, W = a.shape
    tr = half_rows // 2
    nblk = half_rows // tr

    def body(c_ref, a_ref, b_ref, o_ref):
        o_ref[...] = (a_ref[...] + b_ref[...]).astype(o_ref.dtype)

    return pl.pallas_call(
        body,
        grid_spec=pltpu.PrefetchScalarGridSpec(
            num_scalar_prefetch=1, grid=(nblk,),
            in_specs=[pl.BlockSpec((n4, tr, W), lambda i, c: (0, c[0] * nblk + i, 0)),
                      pl.BlockSpec((n4, tr, W), lambda i, c: (0, i, 0))],
            out_specs=pl.BlockSpec((n4, tr, W), lambda i, c: (0, i, 0))),
        out_shape=jax.ShapeDtypeStruct((n4, half_rows, W), out_dtype),
        compiler_params=_params(("parallel",)), name=name)(c_idx, a, b)


def _sum4(land, own, me_c, name):
    _, R, W = land.shape
    tr = R // 2

    def body(s_ref, l_ref, o_ref, out_ref):
        me = s_ref[0]
        acc = None
        for i in range(N_CHIPS):
            t = jnp.where(me == i, o_ref[...], l_ref[i]).astype(F32)
            acc = t if acc is None else acc + t
        out_ref[...] = acc

    return pl.pallas_call(
        body,
        grid_spec=pltpu.PrefetchScalarGridSpec(
            num_scalar_prefetch=1, grid=(R // tr,),
            in_specs=[pl.BlockSpec((N_CHIPS, tr, W), lambda i, s: (0, i, 0)),
                      pl.BlockSpec((None, tr, W), lambda i, s: (s[0], i, 0))],
            out_specs=pl.BlockSpec((None, tr, W), lambda i, s: (s[1], i, 0))),
        out_shape=jax.ShapeDtypeStruct((2, R, W), F32),
        compiler_params=_params(("parallel",)), name=name)(me_c, land, own)


def _reduce_d2d(g4):
    n = len(g4)

    def build(refs, fresh, send_sems, recv_sems):
        x, y, c = _coords()
        sends, recvs = [], []
        for a in range(n):
            hr = g4[a].shape[1] // 2
            src = refs[a].at[:, pl.ds(pl.multiple_of((1 - c) * hr, 8), hr), :]
            sends.append(_remote(src, fresh[a], send_sems, recv_sems, a, (x, y, 1 - c)))
            recvs.append(_remote(src, fresh[a], send_sems, recv_sems, a, (x, y, 1 - c)))
        return sends, recvs

    outs = [jax.ShapeDtypeStruct((N_CHIPS, g.shape[1] // 2, g.shape[2]), F32) for g in g4]
    return build, list(g4), outs, n


def _reduce_ici(q, lands):
    n = len(q)

    def build(refs, fresh, send_sems, recv_sems):
        x, y, c = _coords()
        me = 2 * x + y
        sends, recvs = [], []
        for a in range(n):
            for r, ((px, py), chip) in enumerate(_other_chips(x, y)):
                k = 3 * a + r
                sends.append(_remote(refs[a].at[chip], refs[n + a].at[me], send_sems, recv_sems, k, (px, py, c)))
                recvs.append(_remote(refs[a].at[chip], refs[n + a].at[chip], send_sems, recv_sems, k, (px, py, c)))
        return sends, recvs

    return build, list(q) + list(lands), [n + a for a in range(n)], 3 * n


def _reduce_share(r):
    n = len(r)

    def build(refs, fresh, send_sems, recv_sems):
        x, y, c = _coords()
        sends, recvs = [], []
        for a in range(n):
            sends.append(_remote(refs[a].at[c], refs[a].at[c], send_sems, recv_sems, a, (x, y, 1 - c)))
            recvs.append(_remote(refs[a].at[c], refs[a].at[1 - c], send_sems, recv_sems, a, (x, y, 1 - c)))
        return sends, recvs

    return build, list(r), list(range(n)), n


class _Weave:
    GATHER = {"up1": ("wg1", "wu1", "wd1"), "down1": ("win", "wbr", "wout"), "proj_g": ("wg2", "wu2", "wd2")}
    REDUCE = {"dwg2": ("wg1", "wu1"), "dwu2": ("wd1", "wg2"), "dwd2": ("wu2", "wd2"), "dw_out": ("wbr", "wout"),
              "d_x1": ("win",)}

    def __init__(self, shards, W, L):
        self.shards, self.W, self.L = shards, W, L
        x, y, c = _coords()
        self.c_idx = jnp.reshape(c, (1,)).astype(jnp.int32)
        self.me_c = jnp.stack([2 * x + y, c]).astype(jnp.int32)
        self.final = {k: [None] * L for k in _BIG_ORDER}
        self.st = None
        self.keys = ()

    def gather_first(self):
        got = _comm_call(_gather_ici([self.shards[k] for k in _BIG_ORDER], [self.W[k][0] for k in _BIG_ORDER], 0),
                         "gather0_ici")
        got = _comm_call(_gather_d2d(got), "gather0_d2d")
        for k, t in zip(_BIG_ORDER, got):
            self.W[k][0] = t

    def take(self, site, l):
        self.keys = ()
        if site in self.GATHER or site == "up2":
            nxt = l + 1
            if nxt >= self.L:
                return None
            if site == "up2":
                self.keys = _BIG_ORDER
                return _gather_d2d([self.W[k][nxt] for k in _BIG_ORDER])
            self.keys = self.GATHER[site]
            return _gather_ici([self.shards[k] for k in self.keys], [self.W[k][nxt] for k in self.keys], nxt)
        st = self.st
        if st is None:
            return None
        if site == "dx2":
            self.keys = _BIG_ORDER
            return _reduce_d2d([st["g4"][k] for k in _BIG_ORDER])
        if site in self.REDUCE:
            self.keys = self.REDUCE[site]
            return _reduce_ici([st["q"][k] for k in self.keys],
                               [jnp.zeros(st["q"][k].shape, BF16) for k in self.keys])
        if site == "dw_in":
            self.keys = _BIG_ORDER
            return _reduce_share([st["r"][k] for k in _BIG_ORDER])
        return None

    def done(self, site, l, outs):
        st = self.st
        if site in self.GATHER or site == "up2":
            for k, t in zip(self.keys, outs):
                self.W[k][l + 1] = t
        elif site == "dx2":
            st["q"] = {k: _pair_sum(st["g4"][k], t, self.c_idx, t.shape[1], BF16, "reduce_pair_sum")
                       for k, t in zip(self.keys, outs)}
            st["r"] = {}
        elif site in self.REDUCE:
            for k, t in zip(self.keys, outs):
                st["r"][k] = _sum4(t, st["q"][k], self.me_c, "reduce_sum4")
        elif site == "dw_in":
            for k, t in zip(self.keys, outs):
                self.final[k][st["layer"]] = t.reshape(2 * t.shape[1], t.shape[2])
            self.st = None

    def grads(self, l, big):
        assert self.st is None
        self.st = dict(layer=l, g4={k: g.reshape(N_CHIPS, g.shape[0] // N_CHIPS, g.shape[1]) for k, g in big.items()})

    def flush(self):
        for site, name in (("dx2", "reduce_d2d"), ("dwg2", "reduce_ici"), ("dwu2", "reduce_ici"),
                           ("dwd2", "reduce_ici"), ("dw_out", "reduce_ici"), ("d_x1", "reduce_ici"),
                           ("dw_in", "reduce_share")):
            self.done(site, None, _comm_call(self.take(site, None), name))


def _all_reduce_small(buf, name):
    R = buf.shape[0]

    def build(refs, fresh, send_sems, recv_sems):
        x, y, c = _coords()
        me = 4 * x + 2 * y + c
        sends, recvs = [], []
        k = 0
        for fx in range(2):
            for fy in range(2):
                for fc in range(2):
                    if fx + fy + fc == 0:
                        continue
                    px, py, pc = x ^ fx, y ^ fy, c ^ fc
                    sends.append(_remote(refs[0], refs[1].at[me], send_sems, recv_sems, k, (px, py, pc)))
                    recvs.append(_remote(refs[0], refs[1].at[4 * px + 2 * py + pc], send_sems, recv_sems, k,
                                         (px, py, pc)))
                    k += 1
        return sends, recvs

    got = _comm_call((build, [buf, jnp.zeros((N_DEV, R, 128), F32)], [1], N_DEV - 1), name + "_gather")[0]
    x, y, c = _coords()
    me = jnp.reshape(4 * x + 2 * y + c, (1,)).astype(jnp.int32)

    def body(s_ref, m_ref, own_ref, o_ref):
        acc = None
        for d in range(N_DEV):
            t = jnp.where(s_ref[0] == d, own_ref[...], m_ref[d])
            acc = t if acc is None else acc + t
        o_ref[...] = acc

    tr = max([t for t in range(8, min(R, 1024) + 1, 8) if R % t == 0], default=R)
    return pl.pallas_call(
        body,
        grid_spec=pltpu.PrefetchScalarGridSpec(
            num_scalar_prefetch=1, grid=(R // tr,),
            in_specs=[pl.BlockSpec((N_DEV, tr, 128), lambda i, s: (0, i, 0)),
                      pl.BlockSpec((tr, 128), lambda i, s: (i, 0))],
            out_specs=pl.BlockSpec((tr, 128), lambda i, s: (i, 0))),
        out_shape=jax.ShapeDtypeStruct((R, 128), F32),
        compiler_params=_params(("parallel",)), name=name + "_sum")(me, got, buf)


def _ssm_layouts(p, L):
    G = L * N_GROUPS
    lr = p["ssm_a_re"].reshape(G, 1, SSM_STATE)
    li = p["ssm_a_im"].reshape(G, 1, SSM_STATE)
    ldt = p["ssm_log_dt"].reshape(G, 1, 1)
    br = jnp.swapaxes(p["ssm_b_re"], 2, 3).reshape(G, SSM_GROUP, SSM_STATE)
    bi = jnp.swapaxes(p["ssm_b_im"], 2, 3).reshape(G, SSM_GROUP, SSM_STATE)
    are, aim, bre, bim = _disc_fwd(lr, li, ldt, br, bi)
    eye = jnp.eye(N_GROUPS, dtype=F32)
    abar = jnp.concatenate([are.reshape(L, 8, 128), aim.reshape(L, 8, 128)], axis=1)

    def b_blk(t):
        return jnp.einsum("lgcp,gh->lgchp", t.reshape(L, N_GROUPS, SSM_GROUP, SSM_STATE), eye).reshape(L, 256, 1024)

    def c_blk(t):
        return jnp.einsum("lgcp,gh->lgphc", t, eye).reshape(L, 1024, 256)

    bfull = jnp.concatenate([b_blk(bre), b_blk(bim)], axis=2).astype(BF16)
    cfull = jnp.concatenate([c_blk(p["ssm_c_re"]), -c_blk(p["ssm_c_im"])], axis=1).astype(BF16)
    return (lr, li, ldt, br, bi), abar, bfull, cfull


def _local_step(x, target, W, p, L, weave=None):
    S, Dm = x.shape
    dff = W["wg1"][0].shape[0]
    alpha = (2.0 * L) ** 0.25
    disc_in, abar, bfull, cfull = _ssm_layouts(p, L)
    row = lambda t, l: t[l][None]

    def carry(fn, site, l, *args, **kw):
        comm = weave.take(site, l) if weave is not None else None
        if comm is None:
            return fn(*args, **kw)
        res, couts = fn(*args, comm=comm, **kw)
        weave.done(site, l, couts)
        return res

    saved = []
    h, hb = x, x.astype(BF16)
    for l in range(L):
        sv = {"x0": hb}
        a1, b1, h1 = carry(_ffn_up, "up1", l, hb, W["wg1"][l], W["wu1"][l], dff, "ffn_up")
        x1, x1b, xh1, rs1 = carry(_mm_ln, "down1", l, h1, W["wd1"][l], h, row(p["ln1_g"], l), row(p["ln1_b"], l), 0.5,
                                  alpha, "ffn_down_ln", 1408)
        sv.update(a1=a1, b1=b1, h1=h1, x1=x1b, xh1=xh1, rs1=rs1)
        proj_a = _mm([(x1b, W["win"][l])], "nt", S, D_A, Dm, [F32], name="proj_a", tm=1024, tn=896, tk=1024)
        proj_g = carry(_mm, "proj_g", l, [(x1b, W["win"][l])], "nt", S, 3 * Dm, Dm, [F32], name="proj_g", tm=2048,
                       tn=256, tk=1024, b_off=D_A)
        mc = _conv_fwd(proj_a, p["conv_w"][l], row(p["conv_b"], l), "conv_fwd")
        bu = _mm([(proj_a, (bfull, l))], "nn", S, 2048, D_SSM, [F32], name="ssm_bu", tm=1024, tn=1024, tk=256,
                 a_off=3 * D_CONV)
        xs = _scan_fwd(bu.reshape(S, 16, 128), abar[l], "scan_fwd").reshape(S, 2048)
        y1, ms = _ssm_out(xs, proj_a, cfull[l], row(p["ssm_d"], l), W["wglu"][l], "ssm_out")
        sinks = p["attn_sinks"][l][None]
        ma = _attn_fwd(proj_a, sinks, "attn_fwd")
        merged = _merge_fwd(mc, ms, ma, proj_g, W["wbr"][l], "merge_fwd")
        x2, x2b, xh2, rs2 = _mm_ln(merged, W["wout"][l], x1, row(p["ln2_g"], l), row(p["ln2_b"], l), 1.0, alpha,
                                   "mix_out_ln", 1024)
        sv.update(proj_a=proj_a, proj_g=proj_g, mc=mc, ms=ms, ma=ma, xs=xs, y1=y1, merged=merged, x2=x2b, xh2=xh2,
                  rs2=rs2)
        a2, b2, h2 = carry(_ffn_up, "up2", l, x2b, W["wg2"][l], W["wu2"][l], dff, "ffn_up")
        h, hb, xh3, rs3 = _mm_ln(h2, W["wd2"][l], x2, row(p["ln3_g"], l), row(p["ln3_b"], l), 0.5, alpha,
                                 "ffn_down_ln", 1408)
        sv.update(a2=a2, b2=b2, h2=h2, xh3=xh3, rs3=rs3)
        saved.append(sv)

    dy, loss = _loss_head(h, target)
    big = [None] * L
    small = {k: [None] * L for k in ("ln1_g", "ln1_b", "ln2_g", "ln2_b", "ln3_g", "ln3_b", "conv_w", "conv_b", "ssm_d",
                                     "attn_sinks", "wglu", "dabar", "dbfull", "dcfull")}

    def ffn_bwd(dy_out, x_in, a, b, hh, xh, rs, g, wg, wu, wd, l, sites):
        dres, df, dg, db = _ln_bwd(dy_out, xh, rs, g, 0.5, alpha, "ln_bwd")
        da, dbb = _ffn_dh(df, wd[l], a, b, "ffn_dh")
        dx = carry(_mm, sites[0], l, [(da, wg[l]), (dbb, wu[l])], "nn", S, Dm, dff, [F32], name="ffn_dx", tm=1024,
                   tn=1024, tk=dff // 2, add=dres)
        tn_kw = dict(tm=1408, tn=1024, tk=512)
        dwg = carry(_mm, sites[1], l, [(da, x_in)], "tn", dff, Dm, S, [F32], name="ffn_dw_up", **tn_kw)
        dwu = carry(_mm, sites[2], l, [(dbb, x_in)], "tn", dff, Dm, S, [F32], name="ffn_dw_up", **tn_kw)
        dwd = carry(_mm, sites[3], l, [(hh, df)], "tn", dff, Dm, S, [F32], name="ffn_dw_down", **tn_kw)
        return dx, dwg, dwu, dwd, dg, db

    for l in reversed(range(L)):
        sv = saved[l]
        dx2, dwg2, dwu2, dwd2, small["ln3_g"][l], small["ln3_b"][l] = ffn_bwd(
            dy, sv["x2"], sv["a2"], sv["b2"], sv["h2"], sv["xh3"], sv["rs3"], row(p["ln3_g"], l), W["wg2"], W["wu2"],
            W["wd2"], l, ("dx2", "dwg2", "dwu2", "dwd2"))
        dres2, dmix, small["ln2_g"][l], small["ln2_b"][l] = _ln_bwd(dx2, sv["xh2"], sv["rs2"], row(p["ln2_g"], l), 1.0,
                                                                   alpha, "ln_bwd")
        dwout = carry(_mm, "dw_out", l, [(sv["merged"], dmix)], "tn", Dm, Dm, S, [F32], name="dw_out", tm=512, tn=1024,
                      tk=512)
        dmerged = _mm([(dmix, W["wout"][l])], "nt", S, Dm, Dm, [F32], name="d_merged", tm=1024, tn=1024, tk=512)
        dgates, dyb, dmc, dms, dma = _merge_bwd(dmerged, sv["mc"], sv["ms"], sv["ma"], sv["proj_g"], W["wbr"][l],
                                                "merge_bwd")
        dwbr = jnp.concatenate([
            _mm([(dyb, m)], "tn", Dm, hi - lo, S, [F32], name="dw_br", tm=512, tn=512, tk=512, a_off=b * Dm)
            for b, ((lo, hi), m) in enumerate(zip(_BR, (sv["mc"], sv["ms"], sv["ma"])))], axis=1)
        proj_a = sv["proj_a"]
        d_conv, small["conv_w"][l], small["conv_b"][l] = _conv_bwd(proj_a, dmc, p["conv_w"][l], row(p["conv_b"], l),
                                                                  "conv_bwd")
        dy1, y2, dgl, dxs, du_skip, small["ssm_d"][l] = _ssm_out_bwd(dms, sv["y1"], proj_a, cfull[l],
                                                                    row(p["ssm_d"], l), W["wglu"][l], "ssm_out_bwd")
        small["wglu"][l] = _mm([(y2, dgl)], "tn", D_SSM, D_SSM, S, [F32], name="dw_glu", tk=512)
        small["dcfull"][l] = _mm([(sv["xs"], dy1)], "tn", 2048, D_SSM, S, [F32], name="d_cfull", tm=1024, tk=512)
        lam, small["dabar"][l] = _scan_bwd(dxs.reshape(S, 16, 128), sv["xs"].reshape(S, 16, 128), abar[l], "scan_bwd")
        lam = lam.reshape(S, 2048)
        du = _mm([(lam, (bfull, l))], "nt", S, D_SSM, 2048, [BF16], name="ssm_du", tm=1024, tk=512, add=du_skip)
        small["dbfull"][l] = _mm([(proj_a, lam)], "tn", D_SSM, 2048, S, [F32], name="d_bfull", tm=256, tn=1024,
                                 tk=512, a_off=3 * D_CONV)
        dq, dk, dv, dsk = _attn_bwd(proj_a, dma, p["attn_sinks"][l][None], "attn_bwd")
        small["attn_sinks"][l] = dsk[:, 0]
        dproj = jnp.concatenate([d_conv, du, dq, dk, dv, dgates], axis=1)
        dx1 = carry(_mm, "d_x1", l, [(dproj, W["win"][l])], "nn", S, Dm, D_A + 3 * Dm, [F32], name="d_x1", tm=1024,
                    tn=1024, tk=(D_A + 3 * Dm) // 2, add=dres2)
        dwin = carry(_mm, "dw_in", l, [(dproj, sv["x1"])], "tn", D_A + 3 * Dm, Dm, S, [F32], name="dw_in", tm=2432,
                     tn=1024, tk=512)
        dx0, dwg1, dwu1, dwd1, small["ln1_g"][l], small["ln1_b"][l] = ffn_bwd(
            dx1, sv["x0"], sv["a1"], sv["b1"], sv["h1"], sv["xh1"], sv["rs1"], row(p["ln1_g"], l), W["wg1"], W["wu1"],
            W["wd1"], l, ("dx1", "dwg1", "dwu1", "dwd1"))
        big[l] = dict(wg1=dwg1, wu1=dwu1, wd1=dwd1, win=dwin, wbr=dwbr, wout=dwout, wg2=dwg2, wu2=dwu2, wd2=dwd2)
        if weave is not None:
            weave.grads(l, big[l])
        dy = dx0

    small = {k: jnp.stack(v) for k, v in small.items()}
    eye = jnp.eye(N_GROUPS, dtype=F32)
    dabar = small.pop("dabar")
    dbf = small.pop("dbfull").reshape(L, N_GROUPS, SSM_GROUP, 2, N_GROUPS, SSM_STATE)
    dbbar = jnp.einsum("lgcrhp,gh->rlgcp", dbf, eye).reshape(2, L * N_GROUPS, SSM_GROUP, SSM_STATE)
    dcf = small.pop("dcfull").reshape(L, 2, N_GROUPS, SSM_STATE, N_GROUPS, SSM_GROUP)
    dc = jnp.einsum("lrgphc,gh->rlgcp", dcf, eye)
    G = L * N_GROUPS
    cts = (dabar[:, 0:8].reshape(G, 1, SSM_STATE), dabar[:, 8:16].reshape(G, 1, SSM_STATE), dbbar[0], dbbar[1])
    dlr, dli, dldt, dbr, dbi = _disc_bwd(*disc_in, cts)
    shp_b = (L, N_GROUPS, SSM_GROUP, SSM_STATE)
    small.update(
        ssm_a_re=dlr.reshape(L, N_GROUPS, SSM_STATE), ssm_a_im=dli.reshape(L, N_GROUPS, SSM_STATE),
        ssm_log_dt=dldt.reshape(L, N_GROUPS), ssm_b_re=jnp.swapaxes(dbr.reshape(shp_b), 2, 3),
        ssm_b_im=jnp.swapaxes(dbi.reshape(shp_b), 2, 3), ssm_c_re=dc[0], ssm_c_im=-dc[1],
        ln1_g=small["ln1_g"][:, 0], ln1_b=small["ln1_b"][:, 0], ln2_g=small["ln2_g"][:, 0],
        ln2_b=small["ln2_b"][:, 0], ln3_g=small["ln3_g"][:, 0], ln3_b=small["ln3_b"][:, 0],
        conv_b=small["conv_b"][:, 0], ssm_d=small["ssm_d"][:, 0])
    return loss, dy, big, small


_SMALL_ORDER = ("ln1_g", "ln1_b", "ln2_g", "ln2_b", "ln3_g", "ln3_b", "conv_w", "conv_b", "ssm_a_re", "ssm_a_im",
                "ssm_log_dt", "ssm_b_re", "ssm_b_im", "ssm_c_re", "ssm_c_im", "ssm_d", "attn_sinks", "wglu")
_BIG_ORDER = ("wg1", "wu1", "wd1", "win", "wbr", "wout", "wg2", "wu2", "wd2")
_WEIGHTS = ("ffn1_w_gate", "ffn1_w_up", "ffn1_w_down", "ln1_g", "ln1_b", "w_in", "conv_w", "conv_b", "ssm_a_re",
            "ssm_a_im", "ssm_log_dt", "ssm_b_re", "ssm_b_im", "ssm_c_re", "ssm_c_im", "ssm_d", "ssm_w_glu",
            "attn_sinks", "w_br_conv", "w_br_ssm", "w_br_attn", "w_out", "ln2_g", "ln2_b", "ffn2_w_gate",
            "ffn2_w_up", "ffn2_w_down", "ln3_g", "ln3_b")


def _weight_shards(w):
    t = lambda a: jnp.swapaxes(a, 1, 2).astype(BF16)
    wbr = jnp.concatenate([t(w["w_br_conv"]), t(w["w_br_ssm"]), t(w["w_br_attn"])], axis=2)
    return dict(wg1=t(w["ffn1_w_gate"]), wu1=t(w["ffn1_w_up"]), wd1=w["ffn1_w_down"].astype(BF16), win=t(w["w_in"]),
                wbr=wbr, wout=w["w_out"].astype(BF16), wg2=t(w["ffn2_w_gate"]), wu2=t(w["ffn2_w_up"]),
                wd2=w["ffn2_w_down"].astype(BF16))


def _gather_small_shards(conv_w, w_glu, chip):
    L = conv_w.shape[0]
    part = jnp.concatenate([conv_w.reshape(1, -1), w_glu.reshape(1, -1)], axis=1)
    n = part.shape[1]
    slots = lax.dynamic_update_slice(jnp.zeros((N_CHIPS, n), F32), part, (chip, 0))
    got = (_all_reduce_small(slots.reshape(-1, 128), "param_gather") * 0.5).reshape(N_CHIPS, n)
    cw = jnp.transpose(got[:, :L * 3 * 64].reshape(N_CHIPS, L, 3, 64), (1, 2, 0, 3)).reshape(L, 3, N_CHIPS * 64)
    wg = jnp.transpose(got[:, L * 3 * 64:].reshape(N_CHIPS, L, 64, D_SSM), (1, 0, 2, 3)).reshape(L, N_CHIPS * 64, D_SSM)
    return cw, wg.astype(BF16)


def kernel(x, ffn1_w_gate, ffn1_w_up, ffn1_w_down, ln1_g, ln1_b, w_in, conv_w, conv_b, ssm_a_re, ssm_a_im, ssm_log_dt, ssm_b_re, ssm_b_im, ssm_c_re, ssm_c_im, ssm_d, ssm_w_glu, attn_sinks, w_br_conv, w_br_ssm, w_br_attn, w_out, ln2_g, ln2_b, ffn2_w_gate, ffn2_w_up, ffn2_w_down, ln3_g, ln3_b, loss_target, m_ffn1_w_gate, m_ffn1_w_up, m_ffn1_w_down, m_ln1_g, m_ln1_b, m_w_in, m_conv_w, m_conv_b, m_ssm_a_re, m_ssm_a_im, m_ssm_log_dt, m_ssm_b_re, m_ssm_b_im, m_ssm_c_re, m_ssm_c_im, m_ssm_d, m_ssm_w_glu, m_attn_sinks, m_w_br_conv, m_w_br_ssm, m_w_br_attn, m_w_out, m_ln2_g, m_ln2_b, m_ffn2_w_gate, m_ffn2_w_up, m_ffn2_w_down, m_ln3_g, m_ln3_b, v_ffn1_w_gate, v_ffn1_w_up, v_ffn1_w_down, v_ln1_g, v_ln1_b, v_w_in, v_conv_w, v_conv_b, v_ssm_a_re, v_ssm_a_im, v_ssm_log_dt, v_ssm_b_re, v_ssm_b_im, v_ssm_c_re, v_ssm_c_im, v_ssm_d, v_ssm_w_glu, v_attn_sinks, v_w_br_conv, v_w_br_ssm, v_w_br_attn, v_w_out, v_ln2_g, v_ln2_b, v_ffn2_w_gate, v_ffn2_w_up, v_ffn2_w_down, v_ln3_g, v_ln3_b):
    args = dict(locals())
    w = {k: args[k] for k in _WEIGHTS}
    L = ln1_g.shape[0]
    cx, cy, cc = _coords()
    chip = 2 * cx + cy

    shards = _weight_shards(w)
    full = {k: [lax.dynamic_update_slice(lax.empty((N_CHIPS * s.shape[1], s.shape[2]), BF16), s[l],
                                         (chip * s.shape[1], 0)) for l in range(L)] for k, s in shards.items()}
    weave = _Weave(shards, full, L)
    weave.gather_first()
    p = {k: w[k] for k in ("ln1_g", "ln1_b", "ln2_g", "ln2_b", "ln3_g", "ln3_b", "conv_b", "ssm_a_re", "ssm_a_im",
                           "ssm_log_dt", "ssm_b_re", "ssm_b_im", "ssm_c_re", "ssm_c_im", "ssm_d", "attn_sinks")}
    p["conv_w"], full["wglu"] = _gather_small_shards(conv_w, ssm_w_glu, chip)

    loss, grad_x, _, small = _local_step(x[0], loss_target[0], full, p, L, weave)
    weave.flush()
    loss = lax.psum(loss[0, 0], ("x", "y", "c"))

    sizes = [math.prod(small[k].shape) for k in _SMALL_ORDER]
    pad = (-sum(sizes)) % 1024
    flat = jnp.concatenate([small[k].reshape(-1) for k in _SMALL_ORDER] + [jnp.zeros((pad,), F32)])
    flat = _all_reduce_small(flat.reshape(-1, 128), "small_grads").reshape(-1)
    sm, off = {}, 0
    for k, n in zip(_SMALL_ORDER, sizes):
        sm[k] = flat[off:off + n].reshape(small[k].shape)
        off += n
    red = {k: jnp.stack(weave.final[k]) for k in _BIG_ORDER}
    tr = lambda a: jnp.swapaxes(a, 1, 2)
    grads = dict(sm)
    grads.update(
        ffn1_w_gate=tr(red["wg1"]), ffn1_w_up=tr(red["wu1"]), ffn1_w_down=red["wd1"], w_in=tr(red["win"]),
        w_br_conv=tr(red["wbr"][:, :, _BR[0][0]:_BR[0][1]]), w_br_ssm=tr(red["wbr"][:, :, _BR[1][0]:_BR[1][1]]),
        w_br_attn=tr(red["wbr"][:, :, _BR[2][0]:_BR[2][1]]), w_out=red["wout"], ffn2_w_gate=tr(red["wg2"]),
        ffn2_w_up=tr(red["wu2"]), ffn2_w_down=red["wd2"],
        ssm_w_glu=lax.dynamic_slice_in_dim(sm["wglu"], chip * 64, 64, axis=1),
        conv_w=lax.dynamic_slice_in_dim(sm["conv_w"], chip * 64, 64, axis=2))

    outs = [[], [], [], []]
    for k in _WEIGHTS:
        d, nm, nv = _adamw(w[k], grads[k], args["m_" + k], args["v_" + k], "adamw")
        for lst, val in zip(outs, (grads[k], d, nm, nv)):
            lst.append(val)
    return (loss, grad_x[None], *outs[0], *outs[1], *outs[2], *outs[3])
```

```python
import functools
import math

import jax
import jax.numpy as jnp
from jax import lax
from jax.experimental import pallas as pl
from jax.experimental.pallas import tpu as pltpu

F32 = jnp.float32
BF16 = jnp.bfloat16

LN_EPS = 1e-5
D_CONV = 256
D_SSM = 256
N_GROUPS = 16
SSM_GROUP = 16
SSM_STATE = 64
N_Q_HEADS = 8
HEAD_DIM = 64
D_ATTN = 512
D_KV = 128
BLOCK = 128
D_A = 3 * D_CONV + D_SSM + D_ATTN + 2 * D_KV
ADAM_LR = 0.001
ADAM_B1 = 0.9
ADAM_B2 = 0.999
ADAM_EPS = 1e-08
ADAM_WD = 0.01
ADAM_STEP = 10

VMEM_LIMIT_BYTES = 56 * 1024 * 1024
MESH = pl.DeviceIdType.MESH
N_CHIPS = 4
N_DEV = 8


def _params(sem=None):
    return pltpu.CompilerParams(dimension_semantics=sem, vmem_limit_bytes=VMEM_LIMIT_BYTES)


def _op(op, block, imap):
    if isinstance(op, tuple):
        arr, l = op
        return arr, pl.BlockSpec((None,) + block, lambda *g: (l,) + imap(*g))
    return op, pl.BlockSpec(block, imap)


def _call(body, *, grid, in_specs, out_specs, out_shape, scratch=(), sem=None, name, ops, comm=None):
    if comm is None:
        return pl.pallas_call(body, grid=grid, in_specs=in_specs, out_specs=out_specs, out_shape=out_shape,
                              scratch_shapes=list(scratch), compiler_params=_params(sem), name=name)(*ops), []
    build, cins, couts, n_sem = comm
    n_in, n_out, n_scr, nci, nco, ng = len(in_specs), len(out_specs), len(scratch), len(cins), len(couts), len(grid)
    hbm = pl.BlockSpec(memory_space=pltpu.HBM)
    aliases = {n_in + o: n_out + j for j, o in enumerate(couts) if isinstance(o, int)}
    cshapes = [jax.ShapeDtypeStruct(cins[o].shape, cins[o].dtype) if isinstance(o, int) else o for o in couts]

    def hosted(*refs):
        refs = list(refs)
        main = refs[:n_in] + refs[n_in + nci:n_in + nci + n_out] + refs[n_in + nci + n_out + nco:-2]
        crefs, co = refs[n_in:n_in + nci], refs[n_in + nci + n_out:n_in + nci + n_out + nco]
        send_sems, recv_sems = refs[-2:]
        fresh = []
        for j, o in enumerate(couts):
            if isinstance(o, int):
                crefs[o] = co[j]
            else:
                fresh.append(co[j])

        def start():
            for cp in build(crefs, fresh, send_sems, recv_sems)[0]:
                cp.start()

        def finish():
            sends, recvs = build(crefs, fresh, send_sems, recv_sems)
            for cp in recvs:
                cp.wait_recv()
            for cp in sends:
                cp.wait_send()

        if ng == 0:
            start()
            finish()
            return
        ids = [pl.program_id(d) for d in range(ng)]
        first = functools.reduce(jnp.logical_and, [ids[d] == 0 for d in range(ng)])
        last = functools.reduce(jnp.logical_and, [ids[d] == grid[d] - 1 for d in range(ng)])
        pl.when(first)(start)
        body(*main)
        pl.when(last)(finish)

    res = pl.pallas_call(
        hosted, grid=grid, in_specs=list(in_specs) + [hbm] * nci, out_specs=list(out_specs) + [hbm] * nco,
        out_shape=list(out_shape) + cshapes,
        scratch_shapes=list(scratch) + [pltpu.SemaphoreType.DMA((n_sem,)), pltpu.SemaphoreType.DMA((n_sem,))],
        input_output_aliases=aliases,
        compiler_params=pltpu.CompilerParams(dimension_semantics=("arbitrary",) * ng if ng else None,
                                             vmem_limit_bytes=VMEM_LIMIT_BYTES, has_side_effects=True),
        name=name)(*ops, *cins)
    return res[:n_out], res[n_out:]


def _mm(pairs, mode, m, n, k, out_dtypes, *, name, tm=512, tn=512, tk=512, add=None, a_off=0, b_off=0, comm=None):
    tm, tn, tk = min(tm, m), min(tn, n), min(tk, k)
    assert m % tm == 0 and n % tn == 0 and k % tk == 0 and b_off % tn == 0, (name, m, n, k, tm, tn, tk)
    nk, npair, jo = k // tk, len(pairs), b_off // tn
    if mode == "nn":
        ao = a_off // tk
        ab, ai, bb, bi = (tm, tk), (lambda i, j, kk: (i, kk + ao)), (tk, tn), (lambda i, j, kk: (kk, j))
        dims = (((1,), (0,)), ((), ()))
    elif mode == "nt":
        ao = a_off // tk
        ab, ai, bb, bi = (tm, tk), (lambda i, j, kk: (i, kk + ao)), (tn, tk), (lambda i, j, kk: (j + jo, kk))
        dims = (((1,), (1,)), ((), ()))
    else:
        ao = a_off // tm
        ab, ai, bb, bi = (tk, tm), (lambda i, j, kk: (kk, i + ao)), (tk, tn), (lambda i, j, kk: (kk, j))
        dims = (((0,), (0,)), ((), ()))
    assert a_off % (tm if mode == "tn" else tk) == 0, (name, a_off)
    ops, specs = [], []
    for a, b in pairs:
        for o, blk, im in ((a, ab, ai), (b, bb, bi)):
            arr, sp = _op(o, blk, im)
            ops.append(arr)
            specs.append(sp)
    has_add = add is not None
    if has_add:
        ops.append(add)
        specs.append(pl.BlockSpec((tm, tn), lambda i, j, kk: (i, j)))
    nout = len(out_dtypes)

    def body(*refs):
        outs, acc = refs[2 * npair + has_add:2 * npair + has_add + nout], refs[-1]
        kk = pl.program_id(2)
        t = None
        for p in range(npair):
            d = lax.dot_general(refs[2 * p][...].astype(BF16), refs[2 * p + 1][...].astype(BF16), dims,
                                preferred_element_type=F32)
            t = d if t is None else t + d

        def finish(r):
            if has_add:
                r = r + refs[2 * npair][...]
            for o in outs:
                o[...] = r.astype(o.dtype)

        if nk == 1:
            finish(t)
            return

        @pl.when(kk == 0)
        def _():
            acc[...] = jnp.zeros_like(acc)

        acc[...] += t

        @pl.when(kk == nk - 1)
        def _():
            finish(acc[...])

    res, couts = _call(
        body, grid=(m // tm, n // tn, nk), in_specs=specs,
        out_specs=[pl.BlockSpec((tm, tn), lambda i, j, kk: (i, j))] * nout,
        out_shape=[jax.ShapeDtypeStruct((m, n), dt) for dt in out_dtypes],
        scratch=[pltpu.VMEM((tm, tn), F32)], sem=("parallel", "parallel", "arbitrary"), name=name, ops=ops, comm=comm)
    res = res[0] if nout == 1 else res
    return res if comm is None else (res, couts)


def _rows(width, col=0, tm=None):
    return pl.BlockSpec((tm, width), lambda i: (i, col))


def _whole(shape):
    nd = len(shape)
    return pl.BlockSpec(shape, lambda i: (0,) * nd)


def _sigmoid(x):
    return 0.5 * jnp.tanh(0.5 * x) + 0.5


def _mm_ln(a, w, x, g, b, s, alpha, name, tk, comm=None):
    S, K = a.shape
    Dm = x.shape[1]
    tm, tk = min(1024, S), min(tk, K)
    nk = K // tk

    def body(a_ref, w_ref, x_ref, g_ref, b_ref, y_ref, yb_ref, xh_ref, rs_ref, acc):
        kk = pl.program_id(1)
        d = jnp.dot(a_ref[...], w_ref[...], preferred_element_type=F32)

        if nk > 1:
            @pl.when(kk == 0)
            def _():
                acc[...] = d

            @pl.when((kk > 0) & (kk < nk - 1))
            def _():
                acc[...] += d

        @pl.when(kk == nk - 1)
        def _():
            z = alpha * x_ref[...] + s * (d if nk == 1 else acc[...] + d)
            mu = jnp.mean(z, axis=-1, keepdims=True)
            zc = z - mu
            var = jnp.mean(zc * zc, axis=-1, keepdims=True)
            rstd = lax.rsqrt(var + LN_EPS)
            xh = zc * rstd
            y = xh * g_ref[...] + b_ref[...]
            y_ref[...] = y
            yb_ref[...] = y.astype(BF16)
            xh_ref[...] = xh
            rs_ref[...] = rstd

    wa, ws = _op(w, (tk, Dm), lambda i, kk: (kk, 0))
    row = pl.BlockSpec((tm, Dm), lambda i, kk: (i, 0))
    vec = pl.BlockSpec((1, Dm), lambda i, kk: (0, 0))
    res, couts = _call(
        body, grid=(S // tm, nk),
        in_specs=[pl.BlockSpec((tm, tk), lambda i, kk: (i, kk)), ws, row, vec, vec],
        out_specs=[row, row, row, pl.BlockSpec((tm, 1), lambda i, kk: (i, 0))],
        out_shape=[jax.ShapeDtypeStruct((S, Dm), F32), jax.ShapeDtypeStruct((S, Dm), BF16),
                   jax.ShapeDtypeStruct((S, Dm), F32), jax.ShapeDtypeStruct((S, 1), F32)],
        scratch=[pltpu.VMEM((tm, Dm), F32)], sem=("parallel", "arbitrary"), name=name, ops=[a, wa, x, g, b],
        comm=comm)
    return res if comm is None else (res, couts)


def _ln_bwd(dy, xh, rs, g, s, alpha, name):
    S, Dm = dy.shape
    tm = min(256, S)

    def body(dy_ref, xh_ref, rs_ref, g_ref, dres_ref, dbr_ref, dg_ref, db_ref):
        @pl.when(pl.program_id(0) == 0)
        def _():
            dg_ref[...] = jnp.zeros_like(dg_ref)
            db_ref[...] = jnp.zeros_like(db_ref)

        dy, xh = dy_ref[...], xh_ref[...]
        dyg = dy * g_ref[...]
        m1 = jnp.mean(dyg, axis=-1, keepdims=True)
        m2 = jnp.mean(dyg * xh, axis=-1, keepdims=True)
        dz = rs_ref[...] * (dyg - m1 - xh * m2)
        dres_ref[...] = alpha * dz
        dbr_ref[...] = (s * dz).astype(BF16)
        dg_ref[...] += jnp.sum(dy * xh, axis=0, keepdims=True)
        db_ref[...] += jnp.sum(dy, axis=0, keepdims=True)

    return pl.pallas_call(
        body, grid=(S // tm,),
        in_specs=[_rows(Dm, tm=tm), _rows(Dm, tm=tm), _rows(1, tm=tm), _whole((1, Dm))],
        out_specs=[_rows(Dm, tm=tm), _rows(Dm, tm=tm), _whole((1, Dm)), _whole((1, Dm))],
        out_shape=[jax.ShapeDtypeStruct((S, Dm), F32), jax.ShapeDtypeStruct((S, Dm), BF16),
                   jax.ShapeDtypeStruct((1, Dm), F32), jax.ShapeDtypeStruct((1, Dm), F32)],
        compiler_params=_params(("arbitrary",)), name=name)(dy, xh, rs, g)


def _ffn_up(x, wg, wu, dff, name, comm=None):
    S, Dm = x.shape
    tm, tn = min(512, S), dff // 2

    def body(x_ref, wg_ref, wu_ref, a_ref, b_ref, h_ref):
        xb = x_ref[...]
        dims = (((1,), (1,)), ((), ()))
        a = lax.dot_general(xb, wg_ref[...], dims, preferred_element_type=F32)
        b = lax.dot_general(xb, wu_ref[...], dims, preferred_element_type=F32)
        a_ref[...] = a.astype(BF16)
        b_ref[...] = b.astype(BF16)
        h_ref[...] = (a * _sigmoid(a) * b).astype(BF16)

    wga, wgs = _op(wg, (tn, Dm), lambda i, j: (j, 0))
    wua, wus = _op(wu, (tn, Dm), lambda i, j: (j, 0))
    ob = pl.BlockSpec((tm, tn), lambda i, j: (i, j))
    res, couts = _call(
        body, grid=(S // tm, dff // tn),
        in_specs=[pl.BlockSpec((tm, Dm), lambda i, j: (i, 0)), wgs, wus], out_specs=[ob, ob, ob],
        out_shape=[jax.ShapeDtypeStruct((S, dff), BF16)] * 3, sem=("parallel", "parallel"), name=name,
        ops=[x, wga, wua], comm=comm)
    return res if comm is None else (res, couts)


def _ffn_dh(df, wd, a, b, name):
    S, Dm = df.shape
    dff = a.shape[1]
    tm, tn = min(512, S), dff // 2

    def body(df_ref, wd_ref, a_ref, b_ref, da_ref, db_ref):
        dh = lax.dot_general(df_ref[...], wd_ref[...], (((1,), (1,)), ((), ())), preferred_element_type=F32)
        a, b = a_ref[...].astype(F32), b_ref[...].astype(F32)
        sg = _sigmoid(a)
        da_ref[...] = (dh * b * (sg * (1.0 + a * (1.0 - sg)))).astype(BF16)
        db_ref[...] = (dh * (a * sg)).astype(BF16)

    wda, wds = _op(wd, (tn, Dm), lambda i, j: (j, 0))
    ob = pl.BlockSpec((tm, tn), lambda i, j: (i, j))
    return pl.pallas_call(
        body, grid=(S // tm, dff // tn),
        in_specs=[pl.BlockSpec((tm, Dm), lambda i, j: (i, 0)), wds, ob, ob], out_specs=[ob, ob],
        out_shape=[jax.ShapeDtypeStruct((S, dff), BF16), jax.ShapeDtypeStruct((S, dff), BF16)],
        compiler_params=_params(("parallel", "parallel")), name=name)(df, wda, a, b)


def _halo_prev(width, col, tm):
    return pl.BlockSpec((8, width), lambda i: (jnp.maximum(i * (tm // 8) - 1, 0), col))


def _halo_next(width, col, tm, S):
    return pl.BlockSpec((8, width), lambda i: (jnp.minimum((i + 1) * (tm // 8), S // 8 - 1), col))


def _shift_down(prev8, cur, n):
    ext = jnp.concatenate([prev8, cur], axis=0)
    return pltpu.roll(ext, n, axis=0)[8:]


def _shift_up(cur, next8, n):
    ext = jnp.concatenate([cur, next8], axis=0)
    return pltpu.roll(ext, ext.shape[0] - n, axis=0)[:cur.shape[0]]


def _conv_fwd(proj_a, conv_w, conv_b, name):
    S = proj_a.shape[0]
    tm = min(512, S)
    C = D_CONV

    def body(bg_ref, cg_ref, h_ref, cgp_ref, hp_ref, w_ref, cb_ref, out_ref):
        z = cg_ref[...] * h_ref[...]
        zp = jnp.where(pl.program_id(0) > 0, cgp_ref[...] * hp_ref[...], 0.0)
        w = w_ref[...]
        y = w[2:3] * z + w[1:2] * _shift_down(zp, z, 1) + w[0:1] * _shift_down(zp, z, 2) + cb_ref[...]
        out_ref[...] = (bg_ref[...] * y).astype(BF16)

    return pl.pallas_call(
        body, grid=(S // tm,),
        in_specs=[_rows(C, 0, tm), _rows(C, 1, tm), _rows(C, 2, tm), _halo_prev(C, 1, tm), _halo_prev(C, 2, tm),
                  _whole((3, C)), _whole((1, C))],
        out_specs=_rows(C, 0, tm), out_shape=jax.ShapeDtypeStruct((S, C), BF16),
        compiler_params=_params(("parallel",)), name=name)(proj_a, proj_a, proj_a, proj_a, proj_a, conv_w, conv_b)


def _conv_bwd(proj_a, dmc, conv_w, conv_b, name):
    S = proj_a.shape[0]
    tm = min(512, S)
    C = D_CONV
    nblk = S // tm

    def body(bg_ref, cg_ref, h_ref, cgp_ref, hp_ref, bgn_ref, d_ref, dn_ref, w_ref, cb_ref, out_ref, dw_ref,
             dcb_ref):
        i = pl.program_id(0)

        @pl.when(i == 0)
        def _():
            dw_ref[...] = jnp.zeros_like(dw_ref)
            dcb_ref[...] = jnp.zeros_like(dcb_ref)

        bg, cg, h, d = bg_ref[...], cg_ref[...], h_ref[...], d_ref[...]
        z = cg * h
        zp = jnp.where(i > 0, cgp_ref[...] * hp_ref[...], 0.0)
        w = w_ref[...]
        z1, z2 = _shift_down(zp, z, 1), _shift_down(zp, z, 2)
        y = w[2:3] * z + w[1:2] * z1 + w[0:1] * z2 + cb_ref[...]
        dy = d * bg
        dyn = jnp.where(i < nblk - 1, dn_ref[...] * bgn_ref[...], 0.0)
        dz = w[2:3] * dy + w[1:2] * _shift_up(dy, dyn, 1) + w[0:1] * _shift_up(dy, dyn, 2)
        out_ref[:, 0:C] = (d * y).astype(BF16)
        out_ref[:, C:2 * C] = (dz * h).astype(BF16)
        out_ref[:, 2 * C:3 * C] = (dz * cg).astype(BF16)
        dw_ref[0:1, :] += jnp.sum(dy * z2, axis=0, keepdims=True)
        dw_ref[1:2, :] += jnp.sum(dy * z1, axis=0, keepdims=True)
        dw_ref[2:3, :] += jnp.sum(dy * z, axis=0, keepdims=True)
        dcb_ref[...] += jnp.sum(dy, axis=0, keepdims=True)

    return pl.pallas_call(
        body, grid=(nblk,),
        in_specs=[_rows(C, 0, tm), _rows(C, 1, tm), _rows(C, 2, tm), _halo_prev(C, 1, tm), _halo_prev(C, 2, tm),
                  _halo_next(C, 0, tm, S), _rows(C, 0, tm), _halo_next(C, 0, tm, S), _whole((3, C)), _whole((1, C))],
        out_specs=[_rows(3 * C, 0, tm), _whole((3, C)), _whole((1, C))],
        out_shape=[jax.ShapeDtypeStruct((S, 3 * C), BF16), jax.ShapeDtypeStruct((3, C), F32),
                   jax.ShapeDtypeStruct((1, C), F32)],
        compiler_params=_params(("arbitrary",)), name=name)(
            proj_a, proj_a, proj_a, proj_a, proj_a, proj_a, dmc, dmc, conv_w, conv_b)


def _disc_math(lr, li, ldt, br, bi):
    dt = jnp.exp(ldt)
    mag = jnp.exp(lr * dt)
    ang = li * dt
    are = mag * jnp.cos(ang)
    aim = mag * jnp.sin(ang)
    nr = are - 1.0
    den = lr * lr + li * li
    cre = (nr * lr + aim * li) / den
    cim = (aim * lr - nr * li) / den
    return are, aim, cre * br - cim * bi, cre * bi + cim * br


def _disc_fwd(lr, li, ldt, br, bi):
    shapes = [lr.shape, lr.shape, br.shape, br.shape]

    def body(lr_ref, li_ref, ldt_ref, br_ref, bi_ref, *outs):
        for o, v in zip(outs, _disc_math(lr_ref[...], li_ref[...], ldt_ref[...], br_ref[...], bi_ref[...])):
            o[...] = v

    return pl.pallas_call(body, out_shape=[jax.ShapeDtypeStruct(s, F32) for s in shapes],
                          compiler_params=_params(), name="ssm_disc")(lr, li, ldt, br, bi)


def _disc_bwd(lr, li, ldt, br, bi, cts):
    shapes = [lr.shape, lr.shape, ldt.shape, br.shape, br.shape]

    def body(lr_ref, li_ref, ldt_ref, br_ref, bi_ref, c0, c1, c2, c3, *outs):
        _, vjp = jax.vjp(_disc_math, lr_ref[...], li_ref[...], ldt_ref[...], br_ref[...], bi_ref[...])
        for o, v in zip(outs, vjp((c0[...], c1[...], c2[...], c3[...]))):
            o[...] = v

    return pl.pallas_call(body, out_shape=[jax.ShapeDtypeStruct(s, F32) for s in shapes],
                          compiler_params=_params(), name="ssm_disc_bwd")(lr, li, ldt, br, bi, *cts)


def _scan_fwd(bu, abar, name):
    S = bu.shape[0]
    tb = min(256, S)

    def body(bu_ref, a_ref, xs_ref, st_ref):
        @pl.when(pl.program_id(0) == 0)
        def _():
            st_ref[...] = jnp.zeros_like(st_ref)

        ar, ai = a_ref[0:8, :], a_ref[8:16, :]

        def step(t, c):
            xr, xi = c
            nr = ar * xr - ai * xi + bu_ref[t, 0:8, :]
            ni = ar * xi + ai * xr + bu_ref[t, 8:16, :]
            xs_ref[t, 0:8, :] = nr
            xs_ref[t, 8:16, :] = ni
            return nr, ni

        xr, xi = lax.fori_loop(0, tb, step, (st_ref[0:8, :], st_ref[8:16, :]), unroll=8)
        st_ref[0:8, :] = xr
        st_ref[8:16, :] = xi

    blk = pl.BlockSpec((tb, 16, 128), lambda i: (i, 0, 0))
    return pl.pallas_call(
        body, grid=(S // tb,), in_specs=[blk, _whole((16, 128))], out_specs=blk,
        out_shape=jax.ShapeDtypeStruct((S, 16, 128), F32), scratch_shapes=[pltpu.VMEM((16, 128), F32)],
        compiler_params=_params(("arbitrary",)), name=name)(bu, abar)


def _scan_bwd(dxs, xs, abar, name):
    S = dxs.shape[0]
    tb = min(256, S)
    nblk = S // tb

    def body(d_ref, x_ref, xp_ref, a_ref, lam_ref, da_ref, st_ref):
        i = pl.program_id(0)

        @pl.when(i == 0)
        def _():
            st_ref[...] = jnp.zeros_like(st_ref)
            da_ref[...] = jnp.zeros_like(da_ref)

        ar, ai = a_ref[0:8, :], a_ref[8:16, :]

        def one(t, c, pr, pi):
            lr, li, gr, gi = c
            nr = d_ref[t, 0:8, :] + ar * lr + ai * li
            ni = d_ref[t, 8:16, :] - ai * lr + ar * li
            lam_ref[t, 0:8, :] = nr
            lam_ref[t, 8:16, :] = ni
            return nr, ni, gr + nr * pr + ni * pi, gi - nr * pi + ni * pr

        def step(s, c):
            t = tb - 1 - s
            return one(t, c, x_ref[t - 1, 0:8, :], x_ref[t - 1, 8:16, :])

        c = (st_ref[0:8, :], st_ref[8:16, :], jnp.zeros((8, 128), F32), jnp.zeros((8, 128), F32))
        c = lax.fori_loop(0, tb - 1, step, c, unroll=8)
        first = i == nblk - 1
        pr = jnp.where(first, 0.0, xp_ref[0, 0:8, :])
        pi = jnp.where(first, 0.0, xp_ref[0, 8:16, :])
        lr, li, gr, gi = one(0, c, pr, pi)
        st_ref[0:8, :] = lr
        st_ref[8:16, :] = li
        da_ref[0:8, :] += gr
        da_ref[8:16, :] += gi

    blk = pl.BlockSpec((tb, 16, 128), lambda i: (nblk - 1 - i, 0, 0))
    prev = pl.BlockSpec((1, 16, 128), lambda i: (jnp.maximum((nblk - 1 - i) * tb - 1, 0), 0, 0))
    return pl.pallas_call(
        body, grid=(nblk,), in_specs=[blk, blk, prev, _whole((16, 128))], out_specs=[blk, _whole((16, 128))],
        out_shape=[jax.ShapeDtypeStruct((S, 16, 128), F32), jax.ShapeDtypeStruct((16, 128), F32)],
        scratch_shapes=[pltpu.VMEM((16, 128), F32)],
        compiler_params=_params(("arbitrary",)), name=name)(dxs, xs, xs, abar)


def _gelu(x):
    return 0.5 * x * (1.0 + jnp.tanh(0.7978845608028654 * (x + 0.044715 * x * x * x)))


def _gelu_grad(x):
    t = jnp.tanh(0.7978845608028654 * (x + 0.044715 * x * x * x))
    return 0.5 * (1.0 + t) + 0.5 * x * (1.0 - t * t) * 0.7978845608028654 * (1.0 + 3.0 * 0.044715 * x * x)


def _ssm_out(xs, proj_a, cfull, dskip, wglu, name):
    S = xs.shape[0]
    tm = min(256, S)
    C = D_SSM

    def body(xs_ref, u_ref, c_ref, d_ref, wg_ref, y1_ref, ms_ref):
        y1 = jnp.dot(xs_ref[...].astype(BF16), c_ref[...], preferred_element_type=F32) + d_ref[...] * u_ref[...]
        y2 = _gelu(y1)
        gl = jnp.dot(y2.astype(BF16), wg_ref[...], preferred_element_type=F32)
        y1_ref[...] = y1
        ms_ref[...] = (y2 * _sigmoid(gl)).astype(BF16)

    wga, wgs = _op(wglu, (C, C), lambda i: (0, 0))
    return pl.pallas_call(
        body, grid=(S // tm,),
        in_specs=[_rows(2 * 1024, 0, tm), _rows(C, 3, tm), _whole((2 * 1024, C)), _whole((1, C)), wgs],
        out_specs=[_rows(C, 0, tm), _rows(C, 0, tm)],
        out_shape=[jax.ShapeDtypeStruct((S, C), F32), jax.ShapeDtypeStruct((S, C), BF16)],
        compiler_params=_params(("parallel",)), name=name)(xs, proj_a, cfull, dskip, wga)


def _ssm_out_bwd(dms, y1, proj_a, cfull, dskip, wglu, name):
    S = y1.shape[0]
    tm = min(256, S)
    C = D_SSM

    def body(dms_ref, y1_ref, u_ref, c_ref, d_ref, wg_ref, dy1_ref, y2_ref, dgl_ref, dxs_ref, du_ref, dd_ref):
        @pl.when(pl.program_id(0) == 0)
        def _():
            dd_ref[...] = jnp.zeros_like(dd_ref)

        dms, y1 = dms_ref[...], y1_ref[...]
        y2 = _gelu(y1)
        y2b = y2.astype(BF16)
        sg = _sigmoid(jnp.dot(y2b, wg_ref[...], preferred_element_type=F32))
        dgl = (dms * y2 * sg * (1.0 - sg)).astype(BF16)
        dy2 = dms * sg + lax.dot_general(dgl, wg_ref[...], (((1,), (1,)), ((), ())), preferred_element_type=F32)
        dy1 = dy2 * _gelu_grad(y1)
        dy1b = dy1.astype(BF16)
        dy1_ref[...] = dy1b
        y2_ref[...] = y2b
        dgl_ref[...] = dgl
        dxs_ref[...] = lax.dot_general(dy1b, c_ref[...], (((1,), (1,)), ((), ())), preferred_element_type=F32)
        du_ref[...] = d_ref[...] * dy1
        dd_ref[...] += jnp.sum(dy1 * u_ref[...], axis=0, keepdims=True)

    wga, wgs = _op(wglu, (C, C), lambda i: (0, 0))
    rc = _rows(C, 0, tm)
    return pl.pallas_call(
        body, grid=(S // tm,),
        in_specs=[rc, rc, _rows(C, 3, tm), _whole((2 * 1024, C)), _whole((1, C)), wgs],
        out_specs=[rc, rc, rc, _rows(2 * 1024, 0, tm), rc, _whole((1, C))],
        out_shape=[jax.ShapeDtypeStruct((S, C), BF16), jax.ShapeDtypeStruct((S, C), BF16),
                   jax.ShapeDtypeStruct((S, C), BF16), jax.ShapeDtypeStruct((S, 2 * 1024), F32),
                   jax.ShapeDtypeStruct((S, C), F32), jax.ShapeDtypeStruct((1, C), F32)],
        compiler_params=_params(("arbitrary",)), name=name)(dms, y1, proj_a, cfull, dskip, wga)


_NT = (((1,), (1,)), ((), ()))
_TN = (((0,), (0,)), ((), ()))


def _attn_heads(q_ref, kp_ref, kc_ref, vp_ref, vc_ref):
    lane = lax.broadcasted_iota(jnp.int32, (BLOCK, 128), 1)
    kk = jnp.concatenate([kp_ref[...], kc_ref[...]], axis=0).astype(BF16)
    vv = jnp.concatenate([vp_ref[...], vc_ref[...]], axis=0).astype(BF16)
    kk_r, vv_r = pltpu.roll(kk, 64, axis=1), pltpu.roll(vv, 64, axis=1)
    heads = []
    for hq in range(N_Q_HEADS):
        j, e = hq // 2, hq % 2
        qj = (q_ref[:, 128 * j:128 * (j + 1)] * (HEAD_DIM ** -0.5)).astype(BF16)
        own = (lane >= 64) if e else (lane < 64)
        aligned = e == hq // 4
        heads.append((own, jnp.where(own, qj, jnp.zeros_like(qj)), kk if aligned else kk_r, vv if aligned else vv_r,
                      aligned))
    return heads


def _attn_probs(i, heads, s_ref):
    n = N_Q_HEADS * BLOCK
    s = jnp.concatenate([lax.dot_general(qm, ks, _NT, preferred_element_type=F32) for _, qm, ks, _, _ in heads],
                        axis=0)
    row = lax.broadcasted_iota(jnp.int32, (n, 2 * BLOCK), 0) & (BLOCK - 1)
    col = lax.broadcasted_iota(jnp.int32, (n, 2 * BLOCK), 1)
    mask = (col > row) & (col <= row + BLOCK) & ((col >= BLOCK) | (i > 0))
    s = jnp.where(mask, s, -1e30)
    sink = jnp.concatenate([jnp.full((BLOCK, 1), s_ref[0, hq], F32) for hq in range(N_Q_HEADS)], axis=0)
    m = jnp.maximum(jnp.max(s, axis=1, keepdims=True), sink)
    p = jnp.exp(s - m)
    es = jnp.exp(sink - m)
    inv = 1.0 / (jnp.sum(p, axis=1, keepdims=True) + es)
    return p * inv, es * inv


def _attn_fwd(proj_a, sinks, name):
    S = proj_a.shape[0]
    nb = S // BLOCK

    def body(q_ref, kp_ref, kc_ref, vp_ref, vc_ref, s_ref, out_ref):
        i = pl.program_id(0)
        heads = _attn_heads(q_ref, kp_ref, kc_ref, vp_ref, vc_ref)
        row = lax.broadcasted_iota(jnp.int32, (BLOCK, 2 * BLOCK), 0)
        col = lax.broadcasted_iota(jnp.int32, (BLOCK, 2 * BLOCK), 1)
        mask = (col > row) & (col <= row + BLOCK) & ((col >= BLOCK) | (i > 0))
        outs = []
        for hq, (_, qm, ks, vs, _) in enumerate(heads):
            s = jnp.where(mask, lax.dot_general(qm, ks, _NT, preferred_element_type=F32), -1e30)
            sink = s_ref[0, hq]
            m = jnp.maximum(jnp.max(s, axis=1, keepdims=True), sink)
            p = jnp.exp(s - m)
            inv = 1.0 / (jnp.sum(p, axis=1, keepdims=True) + jnp.exp(sink - m))
            outs.append(jnp.dot((p * inv).astype(BF16), vs, preferred_element_type=F32))
        for j in range(4):
            out_ref[:, 128 * j:128 * (j + 1)] = jnp.where(heads[2 * j][0], outs[2 * j], outs[2 * j + 1]).astype(BF16)

    prev = lambda c: pl.BlockSpec((BLOCK, 128), lambda i: (jnp.maximum(i - 1, 0), c))
    cur = lambda c: pl.BlockSpec((BLOCK, 128), lambda i: (i, c))
    return pl.pallas_call(
        body, grid=(nb,),
        in_specs=[_rows(D_ATTN, 2, BLOCK), prev(12), cur(12), prev(13), cur(13),
                  pl.BlockSpec(memory_space=pltpu.SMEM)],
        out_specs=_rows(D_ATTN, 0, BLOCK), out_shape=jax.ShapeDtypeStruct((S, D_ATTN), BF16),
        compiler_params=_params(("parallel",)), name=name)(proj_a, proj_a, proj_a, proj_a, proj_a, sinks)


def _attn_bwd(proj_a, dout, sinks, name):
    S = proj_a.shape[0]
    nb = S // BLOCK

    def body(q_ref, kp_ref, kc_ref, vp_ref, vc_ref, do_ref, s_ref, out_ref, dk_ref, dv_ref, ds_ref, ck_ref, cv_ref):
        i = pl.program_id(0)

        @pl.when(i == 0)
        def _():
            ds_ref[...] = jnp.zeros_like(ds_ref)
            ck_ref[...] = jnp.zeros_like(ck_ref)
            cv_ref[...] = jnp.zeros_like(cv_ref)

        @pl.when(i < nb)
        def _():
            heads = _attn_heads(q_ref, kp_ref, kc_ref, vp_ref, vc_ref)
            pn, psink = _attn_probs(i, heads, s_ref)
            doms = []
            for hq, (own, _, _, _, _) in enumerate(heads):
                doj = do_ref[:, 128 * (hq // 2):128 * (hq // 2 + 1)].astype(BF16)
                doms.append(jnp.where(own, doj, jnp.zeros_like(doj)))
            dp = jnp.concatenate([lax.dot_general(dom, vs, _NT, preferred_element_type=F32)
                                  for dom, (_, _, _, vs, _) in zip(doms, heads)], axis=0)
            delta = jnp.sum(pn * dp, axis=1, keepdims=True)
            dsb = (pn * (dp - delta)).astype(BF16)
            pnb = pn.astype(BF16)
            sk = psink * delta
            dkk = jnp.zeros((2 * BLOCK, 128), F32)
            dvv = jnp.zeros((2 * BLOCK, 128), F32)
            dqs = []
            for hq, (own, qm, ks, vs, aligned) in enumerate(heads):
                rows = slice(BLOCK * hq, BLOCK * (hq + 1))
                ds_ref[hq:hq + 1, :] += jnp.broadcast_to(-jnp.sum(sk[rows]), (1, 128))
                dqs.append(jnp.dot(dsb[rows], ks, preferred_element_type=F32) * (HEAD_DIM ** -0.5))
                dk = lax.dot_general(dsb[rows], qm, _TN, preferred_element_type=F32)
                dv = lax.dot_general(pnb[rows], doms[hq], _TN, preferred_element_type=F32)
                dkk = dkk + (dk if aligned else pltpu.roll(dk, 64, axis=1))
                dvv = dvv + (dv if aligned else pltpu.roll(dv, 64, axis=1))
            for j in range(4):
                out_ref[:, 128 * j:128 * (j + 1)] = jnp.where(heads[2 * j][0], dqs[2 * j], dqs[2 * j + 1]).astype(BF16)
            ck_ref[0:BLOCK, :] = ck_ref[BLOCK:, :] + dkk[0:BLOCK]
            cv_ref[0:BLOCK, :] = cv_ref[BLOCK:, :] + dvv[0:BLOCK]
            ck_ref[BLOCK:, :] = dkk[BLOCK:]
            cv_ref[BLOCK:, :] = dvv[BLOCK:]

        @pl.when(i == nb)
        def _():
            ck_ref[0:BLOCK, :] = ck_ref[BLOCK:, :]
            cv_ref[0:BLOCK, :] = cv_ref[BLOCK:, :]

        dk_ref[...] = ck_ref[0:BLOCK, :].astype(BF16)
        dv_ref[...] = cv_ref[0:BLOCK, :].astype(BF16)

    last = nb - 1
    prev = lambda c: pl.BlockSpec((BLOCK, 128), lambda i: (jnp.clip(i - 1, 0, last), c))
    cur = lambda c: pl.BlockSpec((BLOCK, 128), lambda i: (jnp.minimum(i, last), c))
    qrow = lambda w, c: pl.BlockSpec((BLOCK, w), lambda i: (jnp.minimum(i, last), c))

    dq, dk, dv, ds = pl.pallas_call(
        body, grid=(nb + 1,),
        in_specs=[qrow(D_ATTN, 2), prev(12), cur(12), prev(13), cur(13), qrow(D_ATTN, 0),
                  pl.BlockSpec(memory_space=pltpu.SMEM)],
        out_specs=[qrow(D_ATTN, 0), prev(0), prev(0), _whole((N_Q_HEADS, 128))],
        out_shape=[jax.ShapeDtypeStruct((S, D_ATTN), BF16), jax.ShapeDtypeStruct((S, D_KV), BF16),
                   jax.ShapeDtypeStruct((S, D_KV), BF16), jax.ShapeDtypeStruct((N_Q_HEADS, 128), F32)],
        scratch_shapes=[pltpu.VMEM((2 * BLOCK, 128), F32), pltpu.VMEM((2 * BLOCK, 128), F32)],
        compiler_params=_params(("arbitrary",)), name=name)(proj_a, proj_a, proj_a, proj_a, proj_a, dout, sinks)
    return dq, dk, dv, ds


_BR = ((0, D_CONV), (D_CONV, D_CONV + D_SSM), (D_CONV + D_SSM, D_CONV + D_SSM + D_ATTN))


def _branches(m_refs, wbr_ref):
    nt = (((1,), (1,)), ((), ()))
    return [lax.dot_general(m[...], wbr_ref[:, lo:hi], nt, preferred_element_type=F32)
            for m, (lo, hi) in zip(m_refs, _BR)]


def _merge_fwd(mc, ms, ma, proj_g, wbr_t, name):
    S, Dm = mc.shape[0], proj_g.shape[1] // 3
    tm = min(256, S)

    def body(mc_ref, ms_ref, ma_ref, g_ref, w_ref, out_ref):
        ys = _branches((mc_ref, ms_ref, ma_ref), w_ref)
        acc = None
        for b in range(3):
            t = _sigmoid(g_ref[:, b * Dm:(b + 1) * Dm]) * ys[b]
            acc = t if acc is None else acc + t
        out_ref[...] = acc.astype(BF16)

    wa, ws = _op(wbr_t, (Dm, Dm), lambda i: (0, 0))
    return pl.pallas_call(
        body, grid=(S // tm,),
        in_specs=[_rows(D_CONV, 0, tm), _rows(D_SSM, 0, tm), _rows(D_ATTN, 0, tm), _rows(3 * Dm, 0, tm), ws],
        out_specs=_rows(Dm, 0, tm), out_shape=jax.ShapeDtypeStruct((S, Dm), BF16),
        compiler_params=_params(("parallel",)), name=name)(mc, ms, ma, proj_g, wa)


def _merge_bwd(dmerged, mc, ms, ma, proj_g, wbr_t, name):
    S, Dm = mc.shape[0], proj_g.shape[1] // 3
    tm = min(256, S)

    def body(d_ref, mc_ref, ms_ref, ma_ref, g_ref, w_ref, dg_ref, dy_ref, dmc_ref, dms_ref, dma_ref):
        ys = _branches((mc_ref, ms_ref, ma_ref), w_ref)
        d = d_ref[...]
        for b, (o_ref, (lo, hi)) in enumerate(zip((dmc_ref, dms_ref, dma_ref), _BR)):
            g = _sigmoid(g_ref[:, b * Dm:(b + 1) * Dm])
            dg_ref[:, b * Dm:(b + 1) * Dm] = (d * ys[b] * g * (1.0 - g)).astype(BF16)
            dyb = (g * d).astype(BF16)
            dy_ref[:, b * Dm:(b + 1) * Dm] = dyb
            o_ref[...] = jnp.dot(dyb, w_ref[:, lo:hi], preferred_element_type=F32)

    wa, ws = _op(wbr_t, (Dm, Dm), lambda i: (0, 0))
    return pl.pallas_call(
        body, grid=(S // tm,),
        in_specs=[_rows(Dm, 0, tm), _rows(D_CONV, 0, tm), _rows(D_SSM, 0, tm), _rows(D_ATTN, 0, tm),
                  _rows(3 * Dm, 0, tm), ws],
        out_specs=[_rows(3 * Dm, 0, tm), _rows(3 * Dm, 0, tm), _rows(D_CONV, 0, tm), _rows(D_SSM, 0, tm),
                   _rows(D_ATTN, 0, tm)],
        out_shape=[jax.ShapeDtypeStruct((S, 3 * Dm), BF16), jax.ShapeDtypeStruct((S, 3 * Dm), BF16),
                   jax.ShapeDtypeStruct((S, D_CONV), F32), jax.ShapeDtypeStruct((S, D_SSM), F32),
                   jax.ShapeDtypeStruct((S, D_ATTN), F32)],
        compiler_params=_params(("parallel",)), name=name)(dmerged, mc, ms, ma, proj_g, wa)


def _loss_head(y, target):
    S, Dm = y.shape
    tm = min(512, S)

    def body(y_ref, t_ref, dy_ref, l_ref):
        @pl.when(pl.program_id(0) == 0)
        def _():
            l_ref[...] = jnp.zeros_like(l_ref)

        e = y_ref[...] - t_ref[...]
        dy_ref[...] = e * (1.0 / Dm)
        l_ref[...] += jnp.broadcast_to(0.5 * jnp.sum(jnp.sum(e * e, axis=1, keepdims=True) * (1.0 / Dm)), (1, 128))

    return pl.pallas_call(
        body, grid=(S // tm,), in_specs=[_rows(Dm, 0, tm), _rows(Dm, 0, tm)],
        out_specs=[_rows(Dm, 0, tm), _whole((1, 128))],
        out_shape=[jax.ShapeDtypeStruct((S, Dm), F32), jax.ShapeDtypeStruct((1, 128), F32)],
        compiler_params=_params(("arbitrary",)), name="loss_head")(y, target)


def _view2d(shape):
    n = math.prod(shape)
    if shape[-1] % 128 == 0:
        return (n // shape[-1], shape[-1])
    if n >= (1 << 16) and len(shape) == 3:
        return (shape[0] * shape[1], shape[2])
    if n % 128 == 0:
        return (n // 128, 128)
    return (1, n)


def _adamw(w, g, m, v, name):
    shape = w.shape
    R, C = _view2d(shape)
    tm = R
    for cand in (512, 352, 256):
        if R > cand and R % cand == 0:
            tm = cand
            break
    c1 = 1.0 - ADAM_B1 ** ADAM_STEP
    c2 = 1.0 - ADAM_B2 ** ADAM_STEP

    def body(w_ref, g_ref, m_ref, v_ref, d_ref, nm_ref, nv_ref):
        g = g_ref[...]
        nm = ADAM_B1 * m_ref[...] + (1.0 - ADAM_B1) * g
        nv = ADAM_B2 * v_ref[...] + (1.0 - ADAM_B2) * (g * g)
        d_ref[...] = -ADAM_LR * ((nm / c1) / (jnp.sqrt(nv / c2) + ADAM_EPS) + ADAM_WD * w_ref[...])
        nm_ref[...] = nm
        nv_ref[...] = nv

    blk = _rows(C, 0, tm)
    outs = pl.pallas_call(
        body, grid=(R // tm,), in_specs=[blk] * 4, out_specs=[blk] * 3,
        out_shape=[jax.ShapeDtypeStruct((R, C), F32)] * 3,
        compiler_params=_params(("parallel",)), name=name)(*[t.reshape(R, C) for t in (w, g, m, v)])
    return [o.reshape(shape) for o in outs]


def _coords():
    return lax.axis_index("x"), lax.axis_index("y"), lax.axis_index("c")


def _other_chips(x, y):
    return [((1 - x, y), 2 * (1 - x) + y), ((x, 1 - y), 2 * x + 1 - y), ((1 - x, 1 - y), 2 * (1 - x) + 1 - y)]


def _comm_call(comm, name):
    return _call(None, grid=(), in_specs=[], out_specs=[], out_shape=[], name=name, ops=[], comm=comm)[1]


def _remote(src, dst, send_sems, recv_sems, k, dev):
    return pltpu.make_async_remote_copy(src_ref=src, dst_ref=dst, send_sem=send_sems.at[k], recv_sem=recv_sems.at[k],
                                        device_id=dev, device_id_type=MESH)


def _half(ref, chip, c, rp):
    return ref.at[pl.ds(pl.multiple_of(chip * rp + c * (rp // 2), 16), rp // 2), :]


def _gather_ici(shards, fulls, l):
    n = len(shards)

    def build(refs, fresh, send_sems, recv_sems):
        x, y, c = _coords()
        me = 2 * x + y
        sends, recvs = [], []
        for a in range(n):
            rp = shards[a].shape[1]
            src = refs[a].at[l].at[pl.ds(pl.multiple_of(c * (rp // 2), 16), rp // 2), :]
            for r, ((px, py), chip) in enumerate(_other_chips(x, y)):
                k = 3 * a + r
                sends.append(_remote(src, _half(refs[n + a], me, c, rp), send_sems, recv_sems, k, (px, py, c)))
                recvs.append(_remote(src, _half(refs[n + a], chip, c, rp), send_sems, recv_sems, k, (px, py, c)))
        return sends, recvs

    return build, list(shards) + list(fulls), [n + a for a in range(n)], 3 * n


def _gather_d2d(fulls, shards, l):
    n = len(fulls)

    def build(refs, fresh, send_sems, recv_sems):
        x, y, c = _coords()
        me = 2 * x + y
        sends, recvs = [], []
        for a in range(n):
            rp = fulls[a].shape[0] // N_CHIPS
            for r, (_, chip) in enumerate(_other_chips(x, y)):
                k = 4 * a + r
                mine, theirs = _half(refs[a], chip, c, rp), _half(refs[a], chip, 1 - c, rp)
                sends.append(_remote(mine, mine, send_sems, recv_sems, k, (x, y, 1 - c)))
                recvs.append(_remote(theirs, theirs, send_sems, recv_sems, k, (x, y, 1 - c)))
            own = refs[a].at[pl.ds(pl.multiple_of(me * rp, 16), rp), :]
            sends.append(_remote(refs[n + a].at[l], own, send_sems, recv_sems, 4 * a + 3, (x, y, 1 - c)))
            recvs.append(_remote(refs[n + a].at[l], own, send_sems, recv_sems, 4 * a + 3, (x, y, 1 - c)))
        return sends, recvs

    return build, list(fulls) + list(shards), list(range(n)), 4 * n


def _pair_sum(a, b, c_idx, half_rows, out_dtype, name):
    n4, rp, W = a.shape
    tr = half_rows // 2
    nblk = half_rows // tr

    def body(c_ref, a_ref, b_ref, o_ref):
        o_ref[...] = (a_ref[...] + b_ref[...]).astype(o_ref.dtype)

    return pl.pallas_call(
        body,
        grid_spec=pltpu.PrefetchScalarGridSpec(
            num_scalar_prefetch=1, grid=(nblk,),
            in_specs=[pl.BlockSpec((n4, tr, W), lambda i, c: (0, c[0] * nblk + i, 0)),
                      pl.BlockSpec((n4, tr, W), lambda i, c: (0, i, 0))],
            out_specs=pl.BlockSpec((n4, tr, W), lambda i, c: (0, i, 0))),
        out_shape=jax.ShapeDtypeStruct((n4, half_rows, W), out_dtype),
        compiler_params=_params(("parallel",)), name=name)(c_idx, a, b)


def _sum4(land, own, me_c, stacked, l, name):
    _, R, W = land.shape
    tr = R // 2

    def body(s_ref, l_ref, o_ref, stacked_ref, out_ref):
        me = s_ref[0]
        acc = None
        for i in range(N_CHIPS):
            t = jnp.where(me == i, o_ref[...], l_ref[i]).astype(F32)
            acc = t if acc is None else acc + t
        out_ref[...] = acc

    return pl.pallas_call(
        body,
        grid_spec=pltpu.PrefetchScalarGridSpec(
            num_scalar_prefetch=1, grid=(R // tr,),
            in_specs=[pl.BlockSpec((N_CHIPS, tr, W), lambda i, s: (0, i, 0)),
                      pl.BlockSpec((None, tr, W), lambda i, s: (s[0], i, 0)),
                      pl.BlockSpec(memory_space=pltpu.HBM)],
            out_specs=pl.BlockSpec((None, None, tr, W), lambda i, s: (l, s[1], i, 0))),
        out_shape=jax.ShapeDtypeStruct(stacked.shape, F32), input_output_aliases={3: 0},
        compiler_params=_params(("parallel",)), name=name)(me_c, land, own, stacked)


def _reduce_d2d(g4):
    n = len(g4)

    def build(refs, fresh, send_sems, recv_sems):
        x, y, c = _coords()
        sends, recvs = [], []
        for a in range(n):
            hr = g4[a].shape[1] // 2
            src = refs[a].at[:, pl.ds(pl.multiple_of((1 - c) * hr, 8), hr), :]
            sends.append(_remote(src, fresh[a], send_sems, recv_sems, a, (x, y, 1 - c)))
            recvs.append(_remote(src, fresh[a], send_sems, recv_sems, a, (x, y, 1 - c)))
        return sends, recvs

    outs = [jax.ShapeDtypeStruct((N_CHIPS, g.shape[1] // 2, g.shape[2]), F32) for g in g4]
    return build, list(g4), outs, n


def _reduce_ici(q, lands):
    n = len(q)

    def build(refs, fresh, send_sems, recv_sems):
        x, y, c = _coords()
        me = 2 * x + y
        sends, recvs = [], []
        for a in range(n):
            for r, ((px, py), chip) in enumerate(_other_chips(x, y)):
                k = 3 * a + r
                sends.append(_remote(refs[a].at[chip], refs[n + a].at[me], send_sems, recv_sems, k, (px, py, c)))
                recvs.append(_remote(refs[a].at[chip], refs[n + a].at[chip], send_sems, recv_sems, k, (px, py, c)))
        return sends, recvs

    return build, list(q) + list(lands), [n + a for a in range(n)], 3 * n


def _reduce_share(r, l):
    n = len(r)

    def build(refs, fresh, send_sems, recv_sems):
        x, y, c = _coords()
        sends, recvs = [], []
        for a in range(n):
            mine, theirs = refs[a].at[l].at[c], refs[a].at[l].at[1 - c]
            sends.append(_remote(mine, mine, send_sems, recv_sems, a, (x, y, 1 - c)))
            recvs.append(_remote(mine, theirs, send_sems, recv_sems, a, (x, y, 1 - c)))
        return sends, recvs

    return build, list(r), list(range(n)), n


class _Weave:
    GATHER = {"up1": ("wg1", "wu1", "wd1"), "down1": ("win", "wbr", "wout", "wglu"), "proj_g": ("wg2", "wu2", "wd2")}
    REDUCE = {"dwg2": ("wg1", "wu1"), "dwu2": ("wd1", "wg2"), "dwd2": ("wu2", "wd2"),
              "dw_out": ("wbr", "wout", "wglu"), "d_x1": ("win",)}

    def __init__(self, shards, W, L):
        self.shards, self.W, self.L = shards, W, L
        x, y, c = _coords()
        self.c_idx = jnp.reshape(c, (1,)).astype(jnp.int32)
        self.me_c = jnp.stack([2 * x + y, c]).astype(jnp.int32)
        self.final = {k: lax.empty((L, 2, s.shape[1] // 2, s.shape[2]), F32) for k, s in shards.items()}
        self.st = None
        self.keys = ()

    def gather_first(self):
        got = _comm_call(_gather_ici([self.shards[k] for k in _BIG_ORDER], [self.W[k][0] for k in _BIG_ORDER], 0),
                         "gather0_ici")
        got = _comm_call(_gather_d2d(got, [self.shards[k] for k in _BIG_ORDER], 0), "gather0_d2d")
        for k, t in zip(_BIG_ORDER, got):
            self.W[k][0] = t

    def take(self, site, l):
        self.keys = ()
        if site in self.GATHER or site == "up2":
            nxt = l + 1
            if nxt >= self.L:
                return None
            if site == "up2":
                self.keys = _BIG_ORDER
                return _gather_d2d([self.W[k][nxt] for k in _BIG_ORDER], [self.shards[k] for k in _BIG_ORDER], nxt)
            self.keys = self.GATHER[site]
            return _gather_ici([self.shards[k] for k in self.keys], [self.W[k][nxt] for k in self.keys], nxt)
        st = self.st
        if st is None:
            return None
        if site == "dx2":
            self.keys = _BIG_ORDER
            return _reduce_d2d([st["g4"][k] for k in _BIG_ORDER])
        if site in self.REDUCE:
            self.keys = self.REDUCE[site]
            return _reduce_ici([st["q"][k] for k in self.keys],
                               [jnp.zeros(st["q"][k].shape, BF16) for k in self.keys])
        if site == "dw_in":
            self.keys = _BIG_ORDER
            return _reduce_share([self.final[k] for k in _BIG_ORDER], st["layer"])
        return None

    def done(self, site, l, outs):
        st = self.st
        if site in self.GATHER or site == "up2":
            for k, t in zip(self.keys, outs):
                self.W[k][l + 1] = t
        elif site == "dx2":
            st["q"] = {k: _pair_sum(st["g4"][k], t, self.c_idx, t.shape[1], BF16, "reduce_pair_sum")
                       for k, t in zip(self.keys, outs)}
        elif site in self.REDUCE:
            for k, t in zip(self.keys, outs):
                self.final[k] = _sum4(t, st["q"][k], self.me_c, self.final[k], st["layer"], "reduce_sum4")
        elif site == "dw_in":
            for k, t in zip(self.keys, outs):
                self.final[k] = t
            self.st = None

    def grads(self, l, big):
        assert self.st is None
        self.st = dict(layer=l, g4={k: g.reshape(N_CHIPS, g.shape[0] // N_CHIPS, g.shape[1]) for k, g in big.items()})

    def flush(self):
        for site, name in (("dx2", "reduce_d2d"), ("dwg2", "reduce_ici"), ("dwu2", "reduce_ici"),
                           ("dwd2", "reduce_ici"), ("dw_out", "reduce_ici"), ("d_x1", "reduce_ici"),
                           ("dw_in", "reduce_share")):
            self.done(site, None, _comm_call(self.take(site, None), name))


def _all_reduce_small(buf, name):
    R = buf.shape[0]

    def build(refs, fresh, send_sems, recv_sems):
        x, y, c = _coords()
        me = 4 * x + 2 * y + c
        sends, recvs = [], []
        k = 0
        for fx in range(2):
            for fy in range(2):
                for fc in range(2):
                    if fx + fy + fc == 0:
                        continue
                    px, py, pc = x ^ fx, y ^ fy, c ^ fc
                    sends.append(_remote(refs[0], refs[1].at[me], send_sems, recv_sems, k, (px, py, pc)))
                    recvs.append(_remote(refs[0], refs[1].at[4 * px + 2 * py + pc], send_sems, recv_sems, k,
                                         (px, py, pc)))
                    k += 1
        return sends, recvs

    got = _comm_call((build, [buf, jnp.zeros((N_DEV, R, 128), F32)], [1], N_DEV - 1), name + "_gather")[0]
    x, y, c = _coords()
    me = jnp.reshape(4 * x + 2 * y + c, (1,)).astype(jnp.int32)

    def body(s_ref, m_ref, own_ref, o_ref):
        acc = None
        for d in range(N_DEV):
            t = jnp.where(s_ref[0] == d, own_ref[...], m_ref[d])
            acc = t if acc is None else acc + t
        o_ref[...] = acc

    tr = max([t for t in range(8, min(R, 1024) + 1, 8) if R % t == 0], default=R)
    return pl.pallas_call(
        body,
        grid_spec=pltpu.PrefetchScalarGridSpec(
            num_scalar_prefetch=1, grid=(R // tr,),
            in_specs=[pl.BlockSpec((N_DEV, tr, 128), lambda i, s: (0, i, 0)),
                      pl.BlockSpec((tr, 128), lambda i, s: (i, 0))],
            out_specs=pl.BlockSpec((tr, 128), lambda i, s: (i, 0))),
        out_shape=jax.ShapeDtypeStruct((R, 128), F32),
        compiler_params=_params(("parallel",)), name=name + "_sum")(me, got, buf)


def _ssm_layouts(p, L):
    G = L * N_GROUPS
    lr = p["ssm_a_re"].reshape(G, 1, SSM_STATE)
    li = p["ssm_a_im"].reshape(G, 1, SSM_STATE)
    ldt = p["ssm_log_dt"].reshape(G, 1, 1)
    br = jnp.swapaxes(p["ssm_b_re"], 2, 3).reshape(G, SSM_GROUP, SSM_STATE)
    bi = jnp.swapaxes(p["ssm_b_im"], 2, 3).reshape(G, SSM_GROUP, SSM_STATE)
    are, aim, bre, bim = _disc_fwd(lr, li, ldt, br, bi)
    eye = jnp.eye(N_GROUPS, dtype=F32)
    abar = jnp.concatenate([are.reshape(L, 8, 128), aim.reshape(L, 8, 128)], axis=1)

    def b_blk(t):
        return jnp.einsum("lgcp,gh->lgchp", t.reshape(L, N_GROUPS, SSM_GROUP, SSM_STATE), eye).reshape(L, 256, 1024)

    def c_blk(t):
        return jnp.einsum("lgcp,gh->lgphc", t, eye).reshape(L, 1024, 256)

    bfull = jnp.concatenate([b_blk(bre), b_blk(bim)], axis=2).astype(BF16)
    cfull = jnp.concatenate([c_blk(p["ssm_c_re"]), -c_blk(p["ssm_c_im"])], axis=1).astype(BF16)
    return (lr, li, ldt, br, bi), abar, bfull, cfull


def _local_step(x, target, W, p, L, weave=None):
    S, Dm = x.shape
    dff = W["wg1"][0].shape[0]
    alpha = (2.0 * L) ** 0.25
    disc_in, abar, bfull, cfull = _ssm_layouts(p, L)
    row = lambda t, l: t[l][None]

    def carry(fn, site, l, *args, **kw):
        comm = weave.take(site, l) if weave is not None else None
        if comm is None:
            return fn(*args, **kw)
        res, couts = fn(*args, comm=comm, **kw)
        weave.done(site, l, couts)
        return res

    saved = []
    h, hb = x, x.astype(BF16)
    for l in range(L):
        sv = {"x0": hb}
        a1, b1, h1 = carry(_ffn_up, "up1", l, hb, W["wg1"][l], W["wu1"][l], dff, "ffn_up")
        x1, x1b, xh1, rs1 = carry(_mm_ln, "down1", l, h1, W["wd1"][l], h, row(p["ln1_g"], l), row(p["ln1_b"], l), 0.5,
                                  alpha, "ffn_down_ln", 1408)
        sv.update(a1=a1, b1=b1, h1=h1, x1=x1b, xh1=xh1, rs1=rs1)
        proj_a = _mm([(x1b, W["win"][l])], "nt", S, D_A, Dm, [F32], name="proj_a", tm=1024, tn=896, tk=1024)
        proj_g = carry(_mm, "proj_g", l, [(x1b, W["win"][l])], "nt", S, 3 * Dm, Dm, [F32], name="proj_g", tm=2048,
                       tn=256, tk=1024, b_off=D_A)
        mc = _conv_fwd(proj_a, p["conv_w"][l], row(p["conv_b"], l), "conv_fwd")
        bu = _mm([(proj_a, (bfull, l))], "nn", S, 2048, D_SSM, [F32], name="ssm_bu", tm=1024, tn=1024, tk=256,
                 a_off=3 * D_CONV)
        xs = _scan_fwd(bu.reshape(S, 16, 128), abar[l], "scan_fwd").reshape(S, 2048)
        y1, ms = _ssm_out(xs, proj_a, cfull[l], row(p["ssm_d"], l), W["wglu"][l], "ssm_out")
        sinks = p["attn_sinks"][l][None]
        ma = _attn_fwd(proj_a, sinks, "attn_fwd")
        merged = _merge_fwd(mc, ms, ma, proj_g, W["wbr"][l], "merge_fwd")
        x2, x2b, xh2, rs2 = _mm_ln(merged, W["wout"][l], x1, row(p["ln2_g"], l), row(p["ln2_b"], l), 1.0, alpha,
                                   "mix_out_ln", 1024)
        sv.update(proj_a=proj_a, proj_g=proj_g, mc=mc, ms=ms, ma=ma, xs=xs, y1=y1, merged=merged, x2=x2b, xh2=xh2,
                  rs2=rs2)
        a2, b2, h2 = carry(_ffn_up, "up2", l, x2b, W["wg2"][l], W["wu2"][l], dff, "ffn_up")
        h, hb, xh3, rs3 = _mm_ln(h2, W["wd2"][l], x2, row(p["ln3_g"], l), row(p["ln3_b"], l), 0.5, alpha,
                                 "ffn_down_ln", 1408)
        sv.update(a2=a2, b2=b2, h2=h2, xh3=xh3, rs3=rs3)
        saved.append(sv)

    dy, loss = _loss_head(h, target)
    big = [None] * L
    small = {k: [None] * L for k in ("ln1_g", "ln1_b", "ln2_g", "ln2_b", "ln3_g", "ln3_b", "conv_w", "conv_b", "ssm_d",
                                     "attn_sinks", "dabar", "dbfull", "dcfull")}

    def ffn_bwd(dy_out, x_in, a, b, hh, xh, rs, g, wg, wu, wd, l, sites):
        dres, df, dg, db = _ln_bwd(dy_out, xh, rs, g, 0.5, alpha, "ln_bwd")
        da, dbb = _ffn_dh(df, wd[l], a, b, "ffn_dh")
        dx = carry(_mm, sites[0], l, [(da, wg[l]), (dbb, wu[l])], "nn", S, Dm, dff, [F32], name="ffn_dx", tm=1024,
                   tn=1024, tk=dff // 2, add=dres)
        tn_kw = dict(tm=1408, tn=1024, tk=512)
        dwg = carry(_mm, sites[1], l, [(da, x_in)], "tn", dff, Dm, S, [F32], name="ffn_dw_up", **tn_kw)
        dwu = carry(_mm, sites[2], l, [(dbb, x_in)], "tn", dff, Dm, S, [F32], name="ffn_dw_up", **tn_kw)
        dwd = carry(_mm, sites[3], l, [(hh, df)], "tn", dff, Dm, S, [F32], name="ffn_dw_down", **tn_kw)
        return dx, dwg, dwu, dwd, dg, db

    for l in reversed(range(L)):
        sv = saved[l]
        dx2, dwg2, dwu2, dwd2, small["ln3_g"][l], small["ln3_b"][l] = ffn_bwd(
            dy, sv["x2"], sv["a2"], sv["b2"], sv["h2"], sv["xh3"], sv["rs3"], row(p["ln3_g"], l), W["wg2"], W["wu2"],
            W["wd2"], l, ("dx2", "dwg2", "dwu2", "dwd2"))
        dres2, dmix, small["ln2_g"][l], small["ln2_b"][l] = _ln_bwd(dx2, sv["xh2"], sv["rs2"], row(p["ln2_g"], l), 1.0,
                                                                   alpha, "ln_bwd")
        dwout = carry(_mm, "dw_out", l, [(sv["merged"], dmix)], "tn", Dm, Dm, S, [F32], name="dw_out", tm=512, tn=1024,
                      tk=512)
        dmerged = _mm([(dmix, W["wout"][l])], "nt", S, Dm, Dm, [F32], name="d_merged", tm=1024, tn=1024, tk=512)
        dgates, dyb, dmc, dms, dma = _merge_bwd(dmerged, sv["mc"], sv["ms"], sv["ma"], sv["proj_g"], W["wbr"][l],
                                                "merge_bwd")
        dwbr = jnp.concatenate([
            _mm([(dyb, m)], "tn", Dm, hi - lo, S, [F32], name="dw_br", tm=512, tn=512, tk=512, a_off=b * Dm)
            for b, ((lo, hi), m) in enumerate(zip(_BR, (sv["mc"], sv["ms"], sv["ma"])))], axis=1)
        proj_a = sv["proj_a"]
        d_conv, small["conv_w"][l], small["conv_b"][l] = _conv_bwd(proj_a, dmc, p["conv_w"][l], row(p["conv_b"], l),
                                                                  "conv_bwd")
        dy1, y2, dgl, dxs, du_skip, small["ssm_d"][l] = _ssm_out_bwd(dms, sv["y1"], proj_a, cfull[l],
                                                                    row(p["ssm_d"], l), W["wglu"][l], "ssm_out_bwd")
        dwglu = _mm([(y2, dgl)], "tn", D_SSM, D_SSM, S, [F32], name="dw_glu", tk=512)
        small["dcfull"][l] = _mm([(sv["xs"], dy1)], "tn", 2048, D_SSM, S, [F32], name="d_cfull", tm=1024, tk=512)
        lam, small["dabar"][l] = _scan_bwd(dxs.reshape(S, 16, 128), sv["xs"].reshape(S, 16, 128), abar[l], "scan_bwd")
        lam = lam.reshape(S, 2048)
        du = _mm([(lam, (bfull, l))], "nt", S, D_SSM, 2048, [BF16], name="ssm_du", tm=1024, tk=512, add=du_skip)
        small["dbfull"][l] = _mm([(proj_a, lam)], "tn", D_SSM, 2048, S, [F32], name="d_bfull", tm=256, tn=1024,
                                 tk=512, a_off=3 * D_CONV)
        dq, dk, dv, dsk = _attn_bwd(proj_a, dma, p["attn_sinks"][l][None], "attn_bwd")
        small["attn_sinks"][l] = dsk[:, 0]
        dproj = jnp.concatenate([d_conv, du, dq, dk, dv, dgates], axis=1)
        dx1 = carry(_mm, "d_x1", l, [(dproj, W["win"][l])], "nn", S, Dm, D_A + 3 * Dm, [F32], name="d_x1", tm=1024,
                    tn=1024, tk=(D_A + 3 * Dm) // 2, add=dres2)
        dwin = carry(_mm, "dw_in", l, [(dproj, sv["x1"])], "tn", D_A + 3 * Dm, Dm, S, [F32], name="dw_in", tm=2432,
                     tn=1024, tk=512)
        dx0, dwg1, dwu1, dwd1, small["ln1_g"][l], small["ln1_b"][l] = ffn_bwd(
            dx1, sv["x0"], sv["a1"], sv["b1"], sv["h1"], sv["xh1"], sv["rs1"], row(p["ln1_g"], l), W["wg1"], W["wu1"],
            W["wd1"], l, ("dx1", "dwg1", "dwu1", "dwd1"))
        big[l] = dict(wg1=dwg1, wu1=dwu1, wd1=dwd1, win=dwin, wbr=dwbr, wout=dwout, wg2=dwg2, wu2=dwu2, wd2=dwd2,
                      wglu=dwglu)
        if weave is not None:
            weave.grads(l, big[l])
        dy = dx0

    small = {k: jnp.stack(v) for k, v in small.items()}
    eye = jnp.eye(N_GROUPS, dtype=F32)
    dabar = small.pop("dabar")
    dbf = small.pop("dbfull").reshape(L, N_GROUPS, SSM_GROUP, 2, N_GROUPS, SSM_STATE)
    dbbar = jnp.einsum("lgcrhp,gh->rlgcp", dbf, eye).reshape(2, L * N_GROUPS, SSM_GROUP, SSM_STATE)
    dcf = small.pop("dcfull").reshape(L, 2, N_GROUPS, SSM_STATE, N_GROUPS, SSM_GROUP)
    dc = jnp.einsum("lrgphc,gh->rlgcp", dcf, eye)
    G = L * N_GROUPS
    cts = (dabar[:, 0:8].reshape(G, 1, SSM_STATE), dabar[:, 8:16].reshape(G, 1, SSM_STATE), dbbar[0], dbbar[1])
    dlr, dli, dldt, dbr, dbi = _disc_bwd(*disc_in, cts)
    shp_b = (L, N_GROUPS, SSM_GROUP, SSM_STATE)
    small.update(
        ssm_a_re=dlr.reshape(L, N_GROUPS, SSM_STATE), ssm_a_im=dli.reshape(L, N_GROUPS, SSM_STATE),
        ssm_log_dt=dldt.reshape(L, N_GROUPS), ssm_b_re=jnp.swapaxes(dbr.reshape(shp_b), 2, 3),
        ssm_b_im=jnp.swapaxes(dbi.reshape(shp_b), 2, 3), ssm_c_re=dc[0], ssm_c_im=-dc[1],
        ln1_g=small["ln1_g"][:, 0], ln1_b=small["ln1_b"][:, 0], ln2_g=small["ln2_g"][:, 0],
        ln2_b=small["ln2_b"][:, 0], ln3_g=small["ln3_g"][:, 0], ln3_b=small["ln3_b"][:, 0],
        conv_b=small["conv_b"][:, 0], ssm_d=small["ssm_d"][:, 0])
    return loss, dy, big, small


_SMALL_ORDER = ("ln1_g", "ln1_b", "ln2_g", "ln2_b", "ln3_g", "ln3_b", "conv_w", "conv_b", "ssm_a_re", "ssm_a_im",
                "ssm_log_dt", "ssm_b_re", "ssm_b_im", "ssm_c_re", "ssm_c_im", "ssm_d", "attn_sinks")
_BIG_ORDER = ("wg1", "wu1", "wd1", "win", "wbr", "wout", "wg2", "wu2", "wd2", "wglu")
_WEIGHTS = ("ffn1_w_gate", "ffn1_w_up", "ffn1_w_down", "ln1_g", "ln1_b", "w_in", "conv_w", "conv_b", "ssm_a_re",
            "ssm_a_im", "ssm_log_dt", "ssm_b_re", "ssm_b_im", "ssm_c_re", "ssm_c_im", "ssm_d", "ssm_w_glu",
            "attn_sinks", "w_br_conv", "w_br_ssm", "w_br_attn", "w_out", "ln2_g", "ln2_b", "ffn2_w_gate",
            "ffn2_w_up", "ffn2_w_down", "ln3_g", "ln3_b")


def _weight_shards(w):
    t = lambda a: jnp.swapaxes(a, 1, 2).astype(BF16)
    wbr = jnp.concatenate([t(w["w_br_conv"]), t(w["w_br_ssm"]), t(w["w_br_attn"])], axis=2)
    return dict(wg1=t(w["ffn1_w_gate"]), wu1=t(w["ffn1_w_up"]), wd1=w["ffn1_w_down"].astype(BF16), win=t(w["w_in"]),
                wbr=wbr, wout=w["w_out"].astype(BF16), wg2=t(w["ffn2_w_gate"]), wu2=t(w["ffn2_w_up"]),
                wd2=w["ffn2_w_down"].astype(BF16), wglu=w["ssm_w_glu"].astype(BF16))


def _gather_conv_w(conv_w, chip):
    L = conv_w.shape[0]
    n = L * 3 * 64
    slots = lax.dynamic_update_slice(jnp.zeros((N_CHIPS, n), F32), conv_w.reshape(1, n), (chip, 0))
    got = (_all_reduce_small(slots.reshape(-1, 128), "conv_w_gather") * 0.5).reshape(N_CHIPS, L, 3, 64)
    return jnp.transpose(got, (1, 2, 0, 3)).reshape(L, 3, N_CHIPS * 64)


def kernel(x, ffn1_w_gate, ffn1_w_up, ffn1_w_down, ln1_g, ln1_b, w_in, conv_w, conv_b, ssm_a_re, ssm_a_im, ssm_log_dt, ssm_b_re, ssm_b_im, ssm_c_re, ssm_c_im, ssm_d, ssm_w_glu, attn_sinks, w_br_conv, w_br_ssm, w_br_attn, w_out, ln2_g, ln2_b, ffn2_w_gate, ffn2_w_up, ffn2_w_down, ln3_g, ln3_b, loss_target, m_ffn1_w_gate, m_ffn1_w_up, m_ffn1_w_down, m_ln1_g, m_ln1_b, m_w_in, m_conv_w, m_conv_b, m_ssm_a_re, m_ssm_a_im, m_ssm_log_dt, m_ssm_b_re, m_ssm_b_im, m_ssm_c_re, m_ssm_c_im, m_ssm_d, m_ssm_w_glu, m_attn_sinks, m_w_br_conv, m_w_br_ssm, m_w_br_attn, m_w_out, m_ln2_g, m_ln2_b, m_ffn2_w_gate, m_ffn2_w_up, m_ffn2_w_down, m_ln3_g, m_ln3_b, v_ffn1_w_gate, v_ffn1_w_up, v_ffn1_w_down, v_ln1_g, v_ln1_b, v_w_in, v_conv_w, v_conv_b, v_ssm_a_re, v_ssm_a_im, v_ssm_log_dt, v_ssm_b_re, v_ssm_b_im, v_ssm_c_re, v_ssm_c_im, v_ssm_d, v_ssm_w_glu, v_attn_sinks, v_w_br_conv, v_w_br_ssm, v_w_br_attn, v_w_out, v_ln2_g, v_ln2_b, v_ffn2_w_gate, v_ffn2_w_up, v_ffn2_w_down, v_ln3_g, v_ln3_b):
    args = dict(locals())
    w = {k: args[k] for k in _WEIGHTS}
    L = ln1_g.shape[0]
    cx, cy, cc = _coords()
    chip = 2 * cx + cy

    shards = _weight_shards(w)
    full = {k: [lax.empty((N_CHIPS * s.shape[1], s.shape[2]), BF16) for _ in range(L)] for k, s in shards.items()}
    weave = _Weave(shards, full, L)
    weave.gather_first()
    p = {k: w[k] for k in ("ln1_g", "ln1_b", "ln2_g", "ln2_b", "ln3_g", "ln3_b", "conv_b", "ssm_a_re", "ssm_a_im",
                           "ssm_log_dt", "ssm_b_re", "ssm_b_im", "ssm_c_re", "ssm_c_im", "ssm_d", "attn_sinks")}
    p["conv_w"] = _gather_conv_w(conv_w, chip)

    loss, grad_x, _, small = _local_step(x[0], loss_target[0], full, p, L, weave)
    weave.flush()
    loss = lax.psum(loss[0, 0], ("x", "y", "c"))

    sizes = [math.prod(small[k].shape) for k in _SMALL_ORDER]
    pad = (-sum(sizes)) % 1024
    flat = jnp.concatenate([small[k].reshape(-1) for k in _SMALL_ORDER] + [jnp.zeros((pad,), F32)])
    flat = _all_reduce_small(flat.reshape(-1, 128), "small_grads").reshape(-1)
    sm, off = {}, 0
    for k, n in zip(_SMALL_ORDER, sizes):
        sm[k] = flat[off:off + n].reshape(small[k].shape)
        off += n
    red = {k: t.reshape(L, 2 * t.shape[2], t.shape[3]) for k, t in weave.final.items()}
    tr = lambda a: jnp.swapaxes(a, 1, 2)
    grads = dict(sm)
    grads.update(
        ffn1_w_gate=tr(red["wg1"]), ffn1_w_up=tr(red["wu1"]), ffn1_w_down=red["wd1"], w_in=tr(red["win"]),
        w_br_conv=tr(red["wbr"][:, :, _BR[0][0]:_BR[0][1]]), w_br_ssm=tr(red["wbr"][:, :, _BR[1][0]:_BR[1][1]]),
        w_br_attn=tr(red["wbr"][:, :, _BR[2][0]:_BR[2][1]]), w_out=red["wout"], ffn2_w_gate=tr(red["wg2"]),
        ffn2_w_up=tr(red["wu2"]), ffn2_w_down=red["wd2"], ssm_w_glu=red["wglu"],
        conv_w=lax.dynamic_slice_in_dim(sm["conv_w"], chip * 64, 64, axis=2))

    outs = [[], [], [], []]
    for k in _WEIGHTS:
        d, nm, nv = _adamw(w[k], grads[k], args["m_" + k], args["v_" + k], "adamw")
        for lst, val in zip(outs, (grads[k], d, nm, nv)):
            lst.append(val)
    return (loss, grad_x[None], *outs[0], *outs[1], *outs[2], *outs[3])
```

```python
import functools
import math

import jax
import jax.numpy as jnp
from jax import lax
from jax.experimental import pallas as pl
from jax.experimental.pallas import tpu as pltpu

F32 = jnp.float32
BF16 = jnp.bfloat16

LN_EPS = 1e-5
D_CONV = 256
D_SSM = 256
N_GROUPS = 16
SSM_GROUP = 16
SSM_STATE = 64
N_Q_HEADS = 8
HEAD_DIM = 64
D_ATTN = 512
D_KV = 128
BLOCK = 128
D_A = 3 * D_CONV + D_SSM + D_ATTN + 2 * D_KV
ADAM_LR = 0.001
ADAM_B1 = 0.9
ADAM_B2 = 0.999
ADAM_EPS = 1e-08
ADAM_WD = 0.01
ADAM_STEP = 10

VMEM_LIMIT_BYTES = 56 * 1024 * 1024
MESH = pl.DeviceIdType.MESH
N_CHIPS = 4
N_DEV = 8


def _params(sem=None):
    return pltpu.CompilerParams(dimension_semantics=sem, vmem_limit_bytes=VMEM_LIMIT_BYTES)


def _op(op, block, imap):
    if isinstance(op, tuple):
        arr, l = op
        return arr, pl.BlockSpec((None,) + block, lambda *g: (l,) + imap(*g))
    return op, pl.BlockSpec(block, imap)


def _call(body, *, grid, in_specs, out_specs, out_shape, scratch=(), sem=None, name, ops, comm=None):
    if comm is None:
        return pl.pallas_call(body, grid=grid, in_specs=in_specs, out_specs=out_specs, out_shape=out_shape,
                              scratch_shapes=list(scratch), compiler_params=_params(sem), name=name)(*ops), []
    build, cins, couts, n_sem = comm
    n_in, n_out, n_scr, nci, nco, ng = len(in_specs), len(out_specs), len(scratch), len(cins), len(couts), len(grid)
    hbm = pl.BlockSpec(memory_space=pltpu.HBM)
    aliases = {n_in + o: n_out + j for j, o in enumerate(couts) if isinstance(o, int)}
    cshapes = [jax.ShapeDtypeStruct(cins[o].shape, cins[o].dtype) if isinstance(o, int) else o for o in couts]

    def hosted(*refs):
        refs = list(refs)
        main = refs[:n_in] + refs[n_in + nci:n_in + nci + n_out] + refs[n_in + nci + n_out + nco:-2]
        crefs, co = refs[n_in:n_in + nci], refs[n_in + nci + n_out:n_in + nci + n_out + nco]
        send_sems, recv_sems = refs[-2:]
        fresh = []
        for j, o in enumerate(couts):
            if isinstance(o, int):
                crefs[o] = co[j]
            else:
                fresh.append(co[j])

        def start():
            for cp in build(crefs, fresh, send_sems, recv_sems)[0]:
                cp.start()

        def finish():
            sends, recvs = build(crefs, fresh, send_sems, recv_sems)
            for cp in recvs:
                cp.wait_recv()
            for cp in sends:
                cp.wait_send()

        if ng == 0:
            start()
            finish()
            return
        ids = [pl.program_id(d) for d in range(ng)]
        first = functools.reduce(jnp.logical_and, [ids[d] == 0 for d in range(ng)])
        last = functools.reduce(jnp.logical_and, [ids[d] == grid[d] - 1 for d in range(ng)])
        pl.when(first)(start)
        body(*main)
        pl.when(last)(finish)

    res = pl.pallas_call(
        hosted, grid=grid, in_specs=list(in_specs) + [hbm] * nci, out_specs=list(out_specs) + [hbm] * nco,
        out_shape=list(out_shape) + cshapes,
        scratch_shapes=list(scratch) + [pltpu.SemaphoreType.DMA((n_sem,)), pltpu.SemaphoreType.DMA((n_sem,))],
        input_output_aliases=aliases,
        compiler_params=pltpu.CompilerParams(dimension_semantics=("arbitrary",) * ng if ng else None,
                                             vmem_limit_bytes=VMEM_LIMIT_BYTES, has_side_effects=True),
        name=name)(*ops, *cins)
    return res[:n_out], res[n_out:]


def _mm(pairs, mode, m, n, k, out_dtypes, *, name, tm=512, tn=512, tk=512, add=None, a_off=0, b_off=0, comm=None):
    tm, tn, tk = min(tm, m), min(tn, n), min(tk, k)
    assert m % tm == 0 and n % tn == 0 and k % tk == 0 and b_off % tn == 0, (name, m, n, k, tm, tn, tk)
    nk, npair, jo = k // tk, len(pairs), b_off // tn
    if mode == "nn":
        ao = a_off // tk
        ab, ai, bb, bi = (tm, tk), (lambda i, j, kk: (i, kk + ao)), (tk, tn), (lambda i, j, kk: (kk, j))
        dims = (((1,), (0,)), ((), ()))
    elif mode == "nt":
        ao = a_off // tk
        ab, ai, bb, bi = (tm, tk), (lambda i, j, kk: (i, kk + ao)), (tn, tk), (lambda i, j, kk: (j + jo, kk))
        dims = (((1,), (1,)), ((), ()))
    else:
        ao = a_off // tm
        ab, ai, bb, bi = (tk, tm), (lambda i, j, kk: (kk, i + ao)), (tk, tn), (lambda i, j, kk: (kk, j))
        dims = (((0,), (0,)), ((), ()))
    assert a_off % (tm if mode == "tn" else tk) == 0, (name, a_off)
    ops, specs = [], []
    for a, b in pairs:
        for o, blk, im in ((a, ab, ai), (b, bb, bi)):
            arr, sp = _op(o, blk, im)
            ops.append(arr)
            specs.append(sp)
    has_add = add is not None
    if has_add:
        ops.append(add)
        specs.append(pl.BlockSpec((tm, tn), lambda i, j, kk: (i, j)))
    nout = len(out_dtypes)

    def body(*refs):
        outs, acc = refs[2 * npair + has_add:2 * npair + has_add + nout], refs[-1]
        kk = pl.program_id(2)
        t = None
        for p in range(npair):
            d = lax.dot_general(refs[2 * p][...].astype(BF16), refs[2 * p + 1][...].astype(BF16), dims,
                                preferred_element_type=F32)
            t = d if t is None else t + d

        def finish(r):
            if has_add:
                r = r + refs[2 * npair][...]
            for o in outs:
                o[...] = r.astype(o.dtype)

        if nk == 1:
            finish(t)
            return

        @pl.when(kk == 0)
        def _():
            acc[...] = jnp.zeros_like(acc)

        acc[...] += t

        @pl.when(kk == nk - 1)
        def _():
            finish(acc[...])

    res, couts = _call(
        body, grid=(m // tm, n // tn, nk), in_specs=specs,
        out_specs=[pl.BlockSpec((tm, tn), lambda i, j, kk: (i, j))] * nout,
        out_shape=[jax.ShapeDtypeStruct((m, n), dt) for dt in out_dtypes],
        scratch=[pltpu.VMEM((tm, tn), F32)], sem=("parallel", "parallel", "arbitrary"), name=name, ops=ops, comm=comm)
    res = res[0] if nout == 1 else res
    return res if comm is None else (res, couts)


def _rows(width, col=0, tm=None):
    return pl.BlockSpec((tm, width), lambda i: (i, col))


def _whole(shape):
    nd = len(shape)
    return pl.BlockSpec(shape, lambda i: (0,) * nd)


def _sigmoid(x):
    return 0.5 * jnp.tanh(0.5 * x) + 0.5


def _mm_ln(a, w, x, g, b, s, alpha, name, tk, comm=None):
    S, K = a.shape
    Dm = x.shape[1]
    tm, tk = min(1024, S), min(tk, K)
    nk = K // tk

    def body(a_ref, w_ref, x_ref, g_ref, b_ref, y_ref, yb_ref, xh_ref, rs_ref, acc):
        kk = pl.program_id(1)
        d = jnp.dot(a_ref[...], w_ref[...], preferred_element_type=F32)

        if nk > 1:
            @pl.when(kk == 0)
            def _():
                acc[...] = d

            @pl.when((kk > 0) & (kk < nk - 1))
            def _():
                acc[...] += d

        @pl.when(kk == nk - 1)
        def _():
            z = alpha * x_ref[...] + s * (d if nk == 1 else acc[...] + d)
            mu = jnp.mean(z, axis=-1, keepdims=True)
            zc = z - mu
            var = jnp.mean(zc * zc, axis=-1, keepdims=True)
            rstd = lax.rsqrt(var + LN_EPS)
            xh = zc * rstd
            y = xh * g_ref[...] + b_ref[...]
            y_ref[...] = y
            yb_ref[...] = y.astype(BF16)
            xh_ref[...] = xh
            rs_ref[...] = rstd

    wa, ws = _op(w, (tk, Dm), lambda i, kk: (kk, 0))
    row = pl.BlockSpec((tm, Dm), lambda i, kk: (i, 0))
    vec = pl.BlockSpec((1, Dm), lambda i, kk: (0, 0))
    res, couts = _call(
        body, grid=(S // tm, nk),
        in_specs=[pl.BlockSpec((tm, tk), lambda i, kk: (i, kk)), ws, row, vec, vec],
        out_specs=[row, row, row, pl.BlockSpec((tm, 1), lambda i, kk: (i, 0))],
        out_shape=[jax.ShapeDtypeStruct((S, Dm), F32), jax.ShapeDtypeStruct((S, Dm), BF16),
                   jax.ShapeDtypeStruct((S, Dm), F32), jax.ShapeDtypeStruct((S, 1), F32)],
        scratch=[pltpu.VMEM((tm, Dm), F32)], sem=("parallel", "arbitrary"), name=name, ops=[a, wa, x, g, b],
        comm=comm)
    return res if comm is None else (res, couts)


def _ln_bwd(dy, xh, rs, g, s, alpha, name):
    S, Dm = dy.shape
    tm = min(256, S)

    def body(dy_ref, xh_ref, rs_ref, g_ref, dres_ref, dbr_ref, dg_ref, db_ref):
        @pl.when(pl.program_id(0) == 0)
        def _():
            dg_ref[...] = jnp.zeros_like(dg_ref)
            db_ref[...] = jnp.zeros_like(db_ref)

        dy, xh = dy_ref[...], xh_ref[...]
        dyg = dy * g_ref[...]
        m1 = jnp.mean(dyg, axis=-1, keepdims=True)
        m2 = jnp.mean(dyg * xh, axis=-1, keepdims=True)
        dz = rs_ref[...] * (dyg - m1 - xh * m2)
        dres_ref[...] = alpha * dz
        dbr_ref[...] = (s * dz).astype(BF16)
        dg_ref[...] += jnp.sum(dy * xh, axis=0, keepdims=True)
        db_ref[...] += jnp.sum(dy, axis=0, keepdims=True)

    return pl.pallas_call(
        body, grid=(S // tm,),
        in_specs=[_rows(Dm, tm=tm), _rows(Dm, tm=tm), _rows(1, tm=tm), _whole((1, Dm))],
        out_specs=[_rows(Dm, tm=tm), _rows(Dm, tm=tm), _whole((1, Dm)), _whole((1, Dm))],
        out_shape=[jax.ShapeDtypeStruct((S, Dm), F32), jax.ShapeDtypeStruct((S, Dm), BF16),
                   jax.ShapeDtypeStruct((1, Dm), F32), jax.ShapeDtypeStruct((1, Dm), F32)],
        compiler_params=_params(("arbitrary",)), name=name)(dy, xh, rs, g)


def _ffn_up(x, wg, wu, dff, name, comm=None):
    S, Dm = x.shape
    tm, tn = min(512, S), dff // 2

    def body(x_ref, wg_ref, wu_ref, a_ref, b_ref, h_ref):
        xb = x_ref[...]
        dims = (((1,), (1,)), ((), ()))
        a = lax.dot_general(xb, wg_ref[...], dims, preferred_element_type=F32)
        b = lax.dot_general(xb, wu_ref[...], dims, preferred_element_type=F32)
        a_ref[...] = a.astype(BF16)
        b_ref[...] = b.astype(BF16)
        h_ref[...] = (a * _sigmoid(a) * b).astype(BF16)

    wga, wgs = _op(wg, (tn, Dm), lambda i, j: (j, 0))
    wua, wus = _op(wu, (tn, Dm), lambda i, j: (j, 0))
    ob = pl.BlockSpec((tm, tn), lambda i, j: (i, j))
    res, couts = _call(
        body, grid=(S // tm, dff // tn),
        in_specs=[pl.BlockSpec((tm, Dm), lambda i, j: (i, 0)), wgs, wus], out_specs=[ob, ob, ob],
        out_shape=[jax.ShapeDtypeStruct((S, dff), BF16)] * 3, sem=("parallel", "parallel"), name=name,
        ops=[x, wga, wua], comm=comm)
    return res if comm is None else (res, couts)


def _ffn_dh(df, wd, a, b, name):
    S, Dm = df.shape
    dff = a.shape[1]
    tm, tn = min(512, S), dff // 2

    def body(df_ref, wd_ref, a_ref, b_ref, da_ref, db_ref):
        dh = lax.dot_general(df_ref[...], wd_ref[...], (((1,), (1,)), ((), ())), preferred_element_type=F32)
        a, b = a_ref[...].astype(F32), b_ref[...].astype(F32)
        sg = _sigmoid(a)
        da_ref[...] = (dh * b * (sg * (1.0 + a * (1.0 - sg)))).astype(BF16)
        db_ref[...] = (dh * (a * sg)).astype(BF16)

    wda, wds = _op(wd, (tn, Dm), lambda i, j: (j, 0))
    ob = pl.BlockSpec((tm, tn), lambda i, j: (i, j))
    return pl.pallas_call(
        body, grid=(S // tm, dff // tn),
        in_specs=[pl.BlockSpec((tm, Dm), lambda i, j: (i, 0)), wds, ob, ob], out_specs=[ob, ob],
        out_shape=[jax.ShapeDtypeStruct((S, dff), BF16), jax.ShapeDtypeStruct((S, dff), BF16)],
        compiler_params=_params(("parallel", "parallel")), name=name)(df, wda, a, b)


def _halo_prev(width, col, tm):
    return pl.BlockSpec((8, width), lambda i: (jnp.maximum(i * (tm // 8) - 1, 0), col))


def _halo_next(width, col, tm, S):
    return pl.BlockSpec((8, width), lambda i: (jnp.minimum((i + 1) * (tm // 8), S // 8 - 1), col))


def _shift_down(prev8, cur, n):
    ext = jnp.concatenate([prev8, cur], axis=0)
    return pltpu.roll(ext, n, axis=0)[8:]


def _shift_up(cur, next8, n):
    ext = jnp.concatenate([cur, next8], axis=0)
    return pltpu.roll(ext, ext.shape[0] - n, axis=0)[:cur.shape[0]]


def _conv_fwd(proj_a, conv_w, conv_b, name):
    S = proj_a.shape[0]
    tm = min(512, S)
    C = D_CONV

    def body(bg_ref, cg_ref, h_ref, cgp_ref, hp_ref, w_ref, cb_ref, out_ref):
        z = cg_ref[...] * h_ref[...]
        zp = jnp.where(pl.program_id(0) > 0, cgp_ref[...] * hp_ref[...], 0.0)
        w = w_ref[...]
        y = w[2:3] * z + w[1:2] * _shift_down(zp, z, 1) + w[0:1] * _shift_down(zp, z, 2) + cb_ref[...]
        out_ref[...] = (bg_ref[...] * y).astype(BF16)

    return pl.pallas_call(
        body, grid=(S // tm,),
        in_specs=[_rows(C, 0, tm), _rows(C, 1, tm), _rows(C, 2, tm), _halo_prev(C, 1, tm), _halo_prev(C, 2, tm),
                  _whole((3, C)), _whole((1, C))],
        out_specs=_rows(C, 0, tm), out_shape=jax.ShapeDtypeStruct((S, C), BF16),
        compiler_params=_params(("parallel",)), name=name)(proj_a, proj_a, proj_a, proj_a, proj_a, conv_w, conv_b)


def _conv_bwd(proj_a, dmc, conv_w, conv_b, name):
    S = proj_a.shape[0]
    tm = min(512, S)
    C = D_CONV
    nblk = S // tm

    def body(bg_ref, cg_ref, h_ref, cgp_ref, hp_ref, bgn_ref, d_ref, dn_ref, w_ref, cb_ref, out_ref, dw_ref,
             dcb_ref):
        i = pl.program_id(0)

        @pl.when(i == 0)
        def _():
            dw_ref[...] = jnp.zeros_like(dw_ref)
            dcb_ref[...] = jnp.zeros_like(dcb_ref)

        bg, cg, h, d = bg_ref[...], cg_ref[...], h_ref[...], d_ref[...]
        z = cg * h
        zp = jnp.where(i > 0, cgp_ref[...] * hp_ref[...], 0.0)
        w = w_ref[...]
        z1, z2 = _shift_down(zp, z, 1), _shift_down(zp, z, 2)
        y = w[2:3] * z + w[1:2] * z1 + w[0:1] * z2 + cb_ref[...]
        dy = d * bg
        dyn = jnp.where(i < nblk - 1, dn_ref[...] * bgn_ref[...], 0.0)
        dz = w[2:3] * dy + w[1:2] * _shift_up(dy, dyn, 1) + w[0:1] * _shift_up(dy, dyn, 2)
        out_ref[:, 0:C] = (d * y).astype(BF16)
        out_ref[:, C:2 * C] = (dz * h).astype(BF16)
        out_ref[:, 2 * C:3 * C] = (dz * cg).astype(BF16)
        dw_ref[0:1, :] += jnp.sum(dy * z2, axis=0, keepdims=True)
        dw_ref[1:2, :] += jnp.sum(dy * z1, axis=0, keepdims=True)
        dw_ref[2:3, :] += jnp.sum(dy * z, axis=0, keepdims=True)
        dcb_ref[...] += jnp.sum(dy, axis=0, keepdims=True)

    return pl.pallas_call(
        body, grid=(nblk,),
        in_specs=[_rows(C, 0, tm), _rows(C, 1, tm), _rows(C, 2, tm), _halo_prev(C, 1, tm), _halo_prev(C, 2, tm),
                  _halo_next(C, 0, tm, S), _rows(C, 0, tm), _halo_next(C, 0, tm, S), _whole((3, C)), _whole((1, C))],
        out_specs=[_rows(3 * C, 0, tm), _whole((3, C)), _whole((1, C))],
        out_shape=[jax.ShapeDtypeStruct((S, 3 * C), BF16), jax.ShapeDtypeStruct((3, C), F32),
                   jax.ShapeDtypeStruct((1, C), F32)],
        compiler_params=_params(("arbitrary",)), name=name)(
            proj_a, proj_a, proj_a, proj_a, proj_a, proj_a, dmc, dmc, conv_w, conv_b)


def _disc_math(lr, li, ldt, br, bi):
    dt = jnp.exp(ldt)
    mag = jnp.exp(lr * dt)
    ang = li * dt
    are = mag * jnp.cos(ang)
    aim = mag * jnp.sin(ang)
    nr = are - 1.0
    den = lr * lr + li * li
    cre = (nr * lr + aim * li) / den
    cim = (aim * lr - nr * li) / den
    return are, aim, cre * br - cim * bi, cre * bi + cim * br


def _disc_fwd(lr, li, ldt, br, bi):
    shapes = [lr.shape, lr.shape, br.shape, br.shape]

    def body(lr_ref, li_ref, ldt_ref, br_ref, bi_ref, *outs):
        for o, v in zip(outs, _disc_math(lr_ref[...], li_ref[...], ldt_ref[...], br_ref[...], bi_ref[...])):
            o[...] = v

    return pl.pallas_call(body, out_shape=[jax.ShapeDtypeStruct(s, F32) for s in shapes],
                          compiler_params=_params(), name="ssm_disc")(lr, li, ldt, br, bi)


def _disc_bwd(lr, li, ldt, br, bi, cts):
    shapes = [lr.shape, lr.shape, ldt.shape, br.shape, br.shape]

    def body(lr_ref, li_ref, ldt_ref, br_ref, bi_ref, c0, c1, c2, c3, *outs):
        _, vjp = jax.vjp(_disc_math, lr_ref[...], li_ref[...], ldt_ref[...], br_ref[...], bi_ref[...])
        for o, v in zip(outs, vjp((c0[...], c1[...], c2[...], c3[...]))):
            o[...] = v

    return pl.pallas_call(body, out_shape=[jax.ShapeDtypeStruct(s, F32) for s in shapes],
                          compiler_params=_params(), name="ssm_disc_bwd")(lr, li, ldt, br, bi, *cts)


def _scan_fwd(bu, abar, name):
    S = bu.shape[0]
    tb = min(256, S)

    def body(bu_ref, a_ref, xs_ref, st_ref):
        @pl.when(pl.program_id(0) == 0)
        def _():
            st_ref[...] = jnp.zeros_like(st_ref)

        ar, ai = a_ref[0:8, :], a_ref[8:16, :]

        def step(t, c):
            xr, xi = c
            nr = ar * xr - ai * xi + bu_ref[t, 0:8, :]
            ni = ar * xi + ai * xr + bu_ref[t, 8:16, :]
            xs_ref[t, 0:8, :] = nr
            xs_ref[t, 8:16, :] = ni
            return nr, ni

        xr, xi = lax.fori_loop(0, tb, step, (st_ref[0:8, :], st_ref[8:16, :]), unroll=8)
        st_ref[0:8, :] = xr
        st_ref[8:16, :] = xi

    blk = pl.BlockSpec((tb, 16, 128), lambda i: (i, 0, 0))
    return pl.pallas_call(
        body, grid=(S // tb,), in_specs=[blk, _whole((16, 128))], out_specs=blk,
        out_shape=jax.ShapeDtypeStruct((S, 16, 128), F32), scratch_shapes=[pltpu.VMEM((16, 128), F32)],
        compiler_params=_params(("arbitrary",)), name=name)(bu, abar)


def _scan_bwd(dxs, xs, abar, name):
    S = dxs.shape[0]
    tb = min(256, S)
    nblk = S // tb

    def body(d_ref, x_ref, xp_ref, a_ref, lam_ref, da_ref, st_ref):
        i = pl.program_id(0)

        @pl.when(i == 0)
        def _():
            st_ref[...] = jnp.zeros_like(st_ref)
            da_ref[...] = jnp.zeros_like(da_ref)

        ar, ai = a_ref[0:8, :], a_ref[8:16, :]

        def one(t, c, pr, pi):
            lr, li, gr, gi = c
            nr = d_ref[t, 0:8, :] + ar * lr + ai * li
            ni = d_ref[t, 8:16, :] - ai * lr + ar * li
            lam_ref[t, 0:8, :] = nr
            lam_ref[t, 8:16, :] = ni
            return nr, ni, gr + nr * pr + ni * pi, gi - nr * pi + ni * pr

        def step(s, c):
            t = tb - 1 - s
            return one(t, c, x_ref[t - 1, 0:8, :], x_ref[t - 1, 8:16, :])

        c = (st_ref[0:8, :], st_ref[8:16, :], jnp.zeros((8, 128), F32), jnp.zeros((8, 128), F32))
        c = lax.fori_loop(0, tb - 1, step, c, unroll=8)
        first = i == nblk - 1
        pr = jnp.where(first, 0.0, xp_ref[0, 0:8, :])
        pi = jnp.where(first, 0.0, xp_ref[0, 8:16, :])
        lr, li, gr, gi = one(0, c, pr, pi)
        st_ref[0:8, :] = lr
        st_ref[8:16, :] = li
        da_ref[0:8, :] += gr
        da_ref[8:16, :] += gi

    blk = pl.BlockSpec((tb, 16, 128), lambda i: (nblk - 1 - i, 0, 0))
    prev = pl.BlockSpec((1, 16, 128), lambda i: (jnp.maximum((nblk - 1 - i) * tb - 1, 0), 0, 0))
    return pl.pallas_call(
        body, grid=(nblk,), in_specs=[blk, blk, prev, _whole((16, 128))], out_specs=[blk, _whole((16, 128))],
        out_shape=[jax.ShapeDtypeStruct((S, 16, 128), F32), jax.ShapeDtypeStruct((16, 128), F32)],
        scratch_shapes=[pltpu.VMEM((16, 128), F32)],
        compiler_params=_params(("arbitrary",)), name=name)(dxs, xs, xs, abar)


def _gelu(x):
    return 0.5 * x * (1.0 + jnp.tanh(0.7978845608028654 * (x + 0.044715 * x * x * x)))


def _gelu_grad(x):
    t = jnp.tanh(0.7978845608028654 * (x + 0.044715 * x * x * x))
    return 0.5 * (1.0 + t) + 0.5 * x * (1.0 - t * t) * 0.7978845608028654 * (1.0 + 3.0 * 0.044715 * x * x)


def _ssm_out(xs, proj_a, cfull, dskip, wglu, name):
    S = xs.shape[0]
    tm = min(256, S)
    C = D_SSM

    def body(xs_ref, u_ref, c_ref, d_ref, wg_ref, y1_ref, ms_ref):
        y1 = jnp.dot(xs_ref[...].astype(BF16), c_ref[...], preferred_element_type=F32) + d_ref[...] * u_ref[...]
        y2 = _gelu(y1)
        gl = jnp.dot(y2.astype(BF16), wg_ref[...], preferred_element_type=F32)
        y1_ref[...] = y1
        ms_ref[...] = (y2 * _sigmoid(gl)).astype(BF16)

    wga, wgs = _op(wglu, (C, C), lambda i: (0, 0))
    return pl.pallas_call(
        body, grid=(S // tm,),
        in_specs=[_rows(2 * 1024, 0, tm), _rows(C, 3, tm), _whole((2 * 1024, C)), _whole((1, C)), wgs],
        out_specs=[_rows(C, 0, tm), _rows(C, 0, tm)],
        out_shape=[jax.ShapeDtypeStruct((S, C), F32), jax.ShapeDtypeStruct((S, C), BF16)],
        compiler_params=_params(("parallel",)), name=name)(xs, proj_a, cfull, dskip, wga)


def _ssm_out_bwd(dms, y1, proj_a, cfull, dskip, wglu, name):
    S = y1.shape[0]
    tm = min(256, S)
    C = D_SSM

    def body(dms_ref, y1_ref, u_ref, c_ref, d_ref, wg_ref, dy1_ref, y2_ref, dgl_ref, dxs_ref, du_ref, dd_ref):
        @pl.when(pl.program_id(0) == 0)
        def _():
            dd_ref[...] = jnp.zeros_like(dd_ref)

        dms, y1 = dms_ref[...], y1_ref[...]
        y2 = _gelu(y1)
        y2b = y2.astype(BF16)
        sg = _sigmoid(jnp.dot(y2b, wg_ref[...], preferred_element_type=F32))
        dgl = (dms * y2 * sg * (1.0 - sg)).astype(BF16)
        dy2 = dms * sg + lax.dot_general(dgl, wg_ref[...], (((1,), (1,)), ((), ())), preferred_element_type=F32)
        dy1 = dy2 * _gelu_grad(y1)
        dy1b = dy1.astype(BF16)
        dy1_ref[...] = dy1b
        y2_ref[...] = y2b
        dgl_ref[...] = dgl
        dxs_ref[...] = lax.dot_general(dy1b, c_ref[...], (((1,), (1,)), ((), ())), preferred_element_type=F32)
        du_ref[...] = d_ref[...] * dy1
        dd_ref[...] += jnp.sum(dy1 * u_ref[...], axis=0, keepdims=True)

    wga, wgs = _op(wglu, (C, C), lambda i: (0, 0))
    rc = _rows(C, 0, tm)
    return pl.pallas_call(
        body, grid=(S // tm,),
        in_specs=[rc, rc, _rows(C, 3, tm), _whole((2 * 1024, C)), _whole((1, C)), wgs],
        out_specs=[rc, rc, rc, _rows(2 * 1024, 0, tm), rc, _whole((1, C))],
        out_shape=[jax.ShapeDtypeStruct((S, C), BF16), jax.ShapeDtypeStruct((S, C), BF16),
                   jax.ShapeDtypeStruct((S, C), BF16), jax.ShapeDtypeStruct((S, 2 * 1024), F32),
                   jax.ShapeDtypeStruct((S, C), F32), jax.ShapeDtypeStruct((1, C), F32)],
        compiler_params=_params(("arbitrary",)), name=name)(dms, y1, proj_a, cfull, dskip, wga)


_NT = (((1,), (1,)), ((), ()))
_TN = (((0,), (0,)), ((), ()))


def _attn_heads(q_ref, kp_ref, kc_ref, vp_ref, vc_ref):
    lane = lax.broadcasted_iota(jnp.int32, (BLOCK, 128), 1)
    kk = jnp.concatenate([kp_ref[...], kc_ref[...]], axis=0).astype(BF16)
    vv = jnp.concatenate([vp_ref[...], vc_ref[...]], axis=0).astype(BF16)
    kk_r, vv_r = pltpu.roll(kk, 64, axis=1), pltpu.roll(vv, 64, axis=1)
    heads = []
    for hq in range(N_Q_HEADS):
        j, e = hq // 2, hq % 2
        qj = (q_ref[:, 128 * j:128 * (j + 1)] * (HEAD_DIM ** -0.5)).astype(BF16)
        own = (lane >= 64) if e else (lane < 64)
        aligned = e == hq // 4
        heads.append((own, jnp.where(own, qj, jnp.zeros_like(qj)), kk if aligned else kk_r, vv if aligned else vv_r,
                      aligned))
    return heads


def _attn_probs(i, heads, s_ref):
    n = N_Q_HEADS * BLOCK
    s = jnp.concatenate([lax.dot_general(qm, ks, _NT, preferred_element_type=F32) for _, qm, ks, _, _ in heads],
                        axis=0)
    row = lax.broadcasted_iota(jnp.int32, (n, 2 * BLOCK), 0) & (BLOCK - 1)
    col = lax.broadcasted_iota(jnp.int32, (n, 2 * BLOCK), 1)
    mask = (col > row) & (col <= row + BLOCK) & ((col >= BLOCK) | (i > 0))
    s = jnp.where(mask, s, -1e30)
    sink = jnp.concatenate([jnp.full((BLOCK, 1), s_ref[0, hq], F32) for hq in range(N_Q_HEADS)], axis=0)
    m = jnp.maximum(jnp.max(s, axis=1, keepdims=True), sink)
    p = jnp.exp(s - m)
    es = jnp.exp(sink - m)
    inv = 1.0 / (jnp.sum(p, axis=1, keepdims=True) + es)
    return p * inv, es * inv


def _attn_fwd(proj_a, sinks, name):
    S = proj_a.shape[0]
    nb = S // BLOCK

    def body(q_ref, kp_ref, kc_ref, vp_ref, vc_ref, s_ref, out_ref):
        i = pl.program_id(0)
        heads = _attn_heads(q_ref, kp_ref, kc_ref, vp_ref, vc_ref)
        row = lax.broadcasted_iota(jnp.int32, (BLOCK, 2 * BLOCK), 0)
        col = lax.broadcasted_iota(jnp.int32, (BLOCK, 2 * BLOCK), 1)
        mask = (col > row) & (col <= row + BLOCK) & ((col >= BLOCK) | (i > 0))
        outs = []
        for hq, (_, qm, ks, vs, _) in enumerate(heads):
            s = jnp.where(mask, lax.dot_general(qm, ks, _NT, preferred_element_type=F32), -1e30)
            sink = s_ref[0, hq]
            m = jnp.maximum(jnp.max(s, axis=1, keepdims=True), sink)
            p = jnp.exp(s - m)
            inv = 1.0 / (jnp.sum(p, axis=1, keepdims=True) + jnp.exp(sink - m))
            outs.append(jnp.dot((p * inv).astype(BF16), vs, preferred_element_type=F32))
        for j in range(4):
            out_ref[:, 128 * j:128 * (j + 1)] = jnp.where(heads[2 * j][0], outs[2 * j], outs[2 * j + 1]).astype(BF16)

    prev = lambda c: pl.BlockSpec((BLOCK, 128), lambda i: (jnp.maximum(i - 1, 0), c))
    cur = lambda c: pl.BlockSpec((BLOCK, 128), lambda i: (i, c))
    return pl.pallas_call(
        body, grid=(nb,),
        in_specs=[_rows(D_ATTN, 2, BLOCK), prev(12), cur(12), prev(13), cur(13),
                  pl.BlockSpec(memory_space=pltpu.SMEM)],
        out_specs=_rows(D_ATTN, 0, BLOCK), out_shape=jax.ShapeDtypeStruct((S, D_ATTN), BF16),
        compiler_params=_params(("parallel",)), name=name)(proj_a, proj_a, proj_a, proj_a, proj_a, sinks)


def _attn_bwd(proj_a, dout, sinks, name):
    S = proj_a.shape[0]
    nb = S // BLOCK

    def body(q_ref, kp_ref, kc_ref, vp_ref, vc_ref, do_ref, s_ref, out_ref, dk_ref, dv_ref, ds_ref, ck_ref, cv_ref):
        i = pl.program_id(0)

        @pl.when(i == 0)
        def _():
            ds_ref[...] = jnp.zeros_like(ds_ref)
            ck_ref[...] = jnp.zeros_like(ck_ref)
            cv_ref[...] = jnp.zeros_like(cv_ref)

        @pl.when(i < nb)
        def _():
            heads = _attn_heads(q_ref, kp_ref, kc_ref, vp_ref, vc_ref)
            pn, psink = _attn_probs(i, heads, s_ref)
            doms = []
            for hq, (own, _, _, _, _) in enumerate(heads):
                doj = do_ref[:, 128 * (hq // 2):128 * (hq // 2 + 1)].astype(BF16)
                doms.append(jnp.where(own, doj, jnp.zeros_like(doj)))
            dp = jnp.concatenate([lax.dot_general(dom, vs, _NT, preferred_element_type=F32)
                                  for dom, (_, _, _, vs, _) in zip(doms, heads)], axis=0)
            delta = jnp.sum(pn * dp, axis=1, keepdims=True)
            dsb = (pn * (dp - delta)).astype(BF16)
            pnb = pn.astype(BF16)
            sk = psink * delta
            dkk = jnp.zeros((2 * BLOCK, 128), F32)
            dvv = jnp.zeros((2 * BLOCK, 128), F32)
            dqs = []
            for hq, (own, qm, ks, vs, aligned) in enumerate(heads):
                rows = slice(BLOCK * hq, BLOCK * (hq + 1))
                ds_ref[hq:hq + 1, :] += jnp.broadcast_to(-jnp.sum(sk[rows]), (1, 128))
                dqs.append(jnp.dot(dsb[rows], ks, preferred_element_type=F32) * (HEAD_DIM ** -0.5))
                dk = lax.dot_general(dsb[rows], qm, _TN, preferred_element_type=F32)
                dv = lax.dot_general(pnb[rows], doms[hq], _TN, preferred_element_type=F32)
                dkk = dkk + (dk if aligned else pltpu.roll(dk, 64, axis=1))
                dvv = dvv + (dv if aligned else pltpu.roll(dv, 64, axis=1))
            for j in range(4):
                out_ref[:, 128 * j:128 * (j + 1)] = jnp.where(heads[2 * j][0], dqs[2 * j], dqs[2 * j + 1]).astype(BF16)
            ck_ref[0:BLOCK, :] = ck_ref[BLOCK:, :] + dkk[0:BLOCK]
            cv_ref[0:BLOCK, :] = cv_ref[BLOCK:, :] + dvv[0:BLOCK]
            ck_ref[BLOCK:, :] = dkk[BLOCK:]
            cv_ref[BLOCK:, :] = dvv[BLOCK:]

        @pl.when(i == nb)
        def _():
            ck_ref[0:BLOCK, :] = ck_ref[BLOCK:, :]
            cv_ref[0:BLOCK, :] = cv_ref[BLOCK:, :]

        dk_ref[...] = ck_ref[0:BLOCK, :].astype(BF16)
        dv_ref[...] = cv_ref[0:BLOCK, :].astype(BF16)

    last = nb - 1
    prev = lambda c: pl.BlockSpec((BLOCK, 128), lambda i: (jnp.clip(i - 1, 0, last), c))
    cur = lambda c: pl.BlockSpec((BLOCK, 128), lambda i: (jnp.minimum(i, last), c))
    qrow = lambda w, c: pl.BlockSpec((BLOCK, w), lambda i: (jnp.minimum(i, last), c))

    dq, dk, dv, ds = pl.pallas_call(
        body, grid=(nb + 1,),
        in_specs=[qrow(D_ATTN, 2), prev(12), cur(12), prev(13), cur(13), qrow(D_ATTN, 0),
                  pl.BlockSpec(memory_space=pltpu.SMEM)],
        out_specs=[qrow(D_ATTN, 0), prev(0), prev(0), _whole((N_Q_HEADS, 128))],
        out_shape=[jax.ShapeDtypeStruct((S, D_ATTN), BF16), jax.ShapeDtypeStruct((S, D_KV), BF16),
                   jax.ShapeDtypeStruct((S, D_KV), BF16), jax.ShapeDtypeStruct((N_Q_HEADS, 128), F32)],
        scratch_shapes=[pltpu.VMEM((2 * BLOCK, 128), F32), pltpu.VMEM((2 * BLOCK, 128), F32)],
        compiler_params=_params(("arbitrary",)), name=name)(proj_a, proj_a, proj_a, proj_a, proj_a, dout, sinks)
    return dq, dk, dv, ds


_BR = ((0, D_CONV), (D_CONV, D_CONV + D_SSM), (D_CONV + D_SSM, D_CONV + D_SSM + D_ATTN))


def _branches(m_refs, wbr_ref):
    nt = (((1,), (1,)), ((), ()))
    return [lax.dot_general(m[...], wbr_ref[:, lo:hi], nt, preferred_element_type=F32)
            for m, (lo, hi) in zip(m_refs, _BR)]


def _merge_fwd(mc, ms, ma, proj_g, wbr_t, name):
    S, Dm = mc.shape[0], proj_g.shape[1] // 3
    tm = min(256, S)

    def body(mc_ref, ms_ref, ma_ref, g_ref, w_ref, out_ref):
        ys = _branches((mc_ref, ms_ref, ma_ref), w_ref)
        acc = None
        for b in range(3):
            t = _sigmoid(g_ref[:, b * Dm:(b + 1) * Dm]) * ys[b]
            acc = t if acc is None else acc + t
        out_ref[...] = acc.astype(BF16)

    wa, ws = _op(wbr_t, (Dm, Dm), lambda i: (0, 0))
    return pl.pallas_call(
        body, grid=(S // tm,),
        in_specs=[_rows(D_CONV, 0, tm), _rows(D_SSM, 0, tm), _rows(D_ATTN, 0, tm), _rows(3 * Dm, 0, tm), ws],
        out_specs=_rows(Dm, 0, tm), out_shape=jax.ShapeDtypeStruct((S, Dm), BF16),
        compiler_params=_params(("parallel",)), name=name)(mc, ms, ma, proj_g, wa)


def _merge_bwd(dmerged, mc, ms, ma, proj_g, wbr_t, name):
    S, Dm = mc.shape[0], proj_g.shape[1] // 3
    tm = min(256, S)

    def body(d_ref, mc_ref, ms_ref, ma_ref, g_ref, w_ref, dg_ref, dy_ref, dmc_ref, dms_ref, dma_ref):
        ys = _branches((mc_ref, ms_ref, ma_ref), w_ref)
        d = d_ref[...]
        for b, (o_ref, (lo, hi)) in enumerate(zip((dmc_ref, dms_ref, dma_ref), _BR)):
            g = _sigmoid(g_ref[:, b * Dm:(b + 1) * Dm])
            dg_ref[:, b * Dm:(b + 1) * Dm] = (d * ys[b] * g * (1.0 - g)).astype(BF16)
            dyb = (g * d).astype(BF16)
            dy_ref[:, b * Dm:(b + 1) * Dm] = dyb
            o_ref[...] = jnp.dot(dyb, w_ref[:, lo:hi], preferred_element_type=F32)

    wa, ws = _op(wbr_t, (Dm, Dm), lambda i: (0, 0))
    return pl.pallas_call(
        body, grid=(S // tm,),
        in_specs=[_rows(Dm, 0, tm), _rows(D_CONV, 0, tm), _rows(D_SSM, 0, tm), _rows(D_ATTN, 0, tm),
                  _rows(3 * Dm, 0, tm), ws],
        out_specs=[_rows(3 * Dm, 0, tm), _rows(3 * Dm, 0, tm), _rows(D_CONV, 0, tm), _rows(D_SSM, 0, tm),
                   _rows(D_ATTN, 0, tm)],
        out_shape=[jax.ShapeDtypeStruct((S, 3 * Dm), BF16), jax.ShapeDtypeStruct((S, 3 * Dm), BF16),
                   jax.ShapeDtypeStruct((S, D_CONV), F32), jax.ShapeDtypeStruct((S, D_SSM), F32),
                   jax.ShapeDtypeStruct((S, D_ATTN), F32)],
        compiler_params=_params(("parallel",)), name=name)(dmerged, mc, ms, ma, proj_g, wa)


def _loss_head(y, target):
    S, Dm = y.shape
    tm = min(512, S)

    def body(y_ref, t_ref, dy_ref, l_ref):
        @pl.when(pl.program_id(0) == 0)
        def _():
            l_ref[...] = jnp.zeros_like(l_ref)

        e = y_ref[...] - t_ref[...]
        dy_ref[...] = e * (1.0 / Dm)
        l_ref[...] += jnp.broadcast_to(0.5 * jnp.sum(jnp.sum(e * e, axis=1, keepdims=True) * (1.0 / Dm)), (1, 128))

    return pl.pallas_call(
        body, grid=(S // tm,), in_specs=[_rows(Dm, 0, tm), _rows(Dm, 0, tm)],
        out_specs=[_rows(Dm, 0, tm), _whole((1, 128))],
        out_shape=[jax.ShapeDtypeStruct((S, Dm), F32), jax.ShapeDtypeStruct((1, 128), F32)],
        compiler_params=_params(("arbitrary",)), name="loss_head")(y, target)


def _view2d(shape):
    n = math.prod(shape)
    if shape[-1] % 128 == 0:
        return (n // shape[-1], shape[-1])
    if n >= (1 << 16) and len(shape) == 3:
        return (shape[0] * shape[1], shape[2])
    if n % 128 == 0:
        return (n // 128, 128)
    return (1, n)


def _adamw(w, g, m, v, name):
    shape = w.shape
    R, C = _view2d(shape)
    tm = R
    for cand in (512, 352, 256):
        if R > cand and R % cand == 0:
            tm = cand
            break
    c1 = 1.0 - ADAM_B1 ** ADAM_STEP
    c2 = 1.0 - ADAM_B2 ** ADAM_STEP

    def body(w_ref, g_ref, m_ref, v_ref, d_ref, nm_ref, nv_ref):
        g = g_ref[...]
        nm = ADAM_B1 * m_ref[...] + (1.0 - ADAM_B1) * g
        nv = ADAM_B2 * v_ref[...] + (1.0 - ADAM_B2) * (g * g)
        d_ref[...] = -ADAM_LR * ((nm / c1) / (jnp.sqrt(nv / c2) + ADAM_EPS) + ADAM_WD * w_ref[...])
        nm_ref[...] = nm
        nv_ref[...] = nv

    blk = _rows(C, 0, tm)
    outs = pl.pallas_call(
        body, grid=(R // tm,), in_specs=[blk] * 4, out_specs=[blk] * 3,
        out_shape=[jax.ShapeDtypeStruct((R, C), F32)] * 3,
        compiler_params=_params(("parallel",)), name=name)(*[t.reshape(R, C) for t in (w, g, m, v)])
    return [o.reshape(shape) for o in outs]


def _coords():
    return lax.axis_index("x"), lax.axis_index("y"), lax.axis_index("c")


def _other_chips(x, y):
    return [((1 - x, y), 2 * (1 - x) + y), ((x, 1 - y), 2 * x + 1 - y), ((1 - x, 1 - y), 2 * (1 - x) + 1 - y)]


def _comm_call(comm, name):
    return _call(None, grid=(), in_specs=[], out_specs=[], out_shape=[], name=name, ops=[], comm=comm)[1]


def _remote(src, dst, send_sems, recv_sems, k, dev):
    return pltpu.make_async_remote_copy(src_ref=src, dst_ref=dst, send_sem=send_sems.at[k], recv_sem=recv_sems.at[k],
                                        device_id=dev, device_id_type=MESH)


def _half(ref, chip, c, rp):
    return ref.at[pl.ds(pl.multiple_of(chip * rp + c * (rp // 2), 16), rp // 2), :]


def _gather_ici(shards, fulls, l):
    n = len(shards)

    def build(refs, fresh, send_sems, recv_sems):
        x, y, c = _coords()
        me = 2 * x + y
        sends, recvs = [], []
        for a in range(n):
            rp = shards[a].shape[1]
            src = refs[a].at[l].at[pl.ds(pl.multiple_of(c * (rp // 2), 16), rp // 2), :]
            for r, ((px, py), chip) in enumerate(_other_chips(x, y)):
                k = 3 * a + r
                sends.append(_remote(src, _half(refs[n + a], me, c, rp), send_sems, recv_sems, k, (px, py, c)))
                recvs.append(_remote(src, _half(refs[n + a], chip, c, rp), send_sems, recv_sems, k, (px, py, c)))
        return sends, recvs

    return build, list(shards) + list(fulls), [n + a for a in range(n)], 3 * n


def _gather_d2d(fulls, shards, l):
    n = len(fulls)

    def build(refs, fresh, send_sems, recv_sems):
        x, y, c = _coords()
        me = 2 * x + y
        sends, recvs = [], []
        for a in range(n):
            rp = fulls[a].shape[0] // N_CHIPS
            for r, (_, chip) in enumerate(_other_chips(x, y)):
                k = 4 * a + r
                mine, theirs = _half(refs[a], chip, c, rp), _half(refs[a], chip, 1 - c, rp)
                sends.append(_remote(mine, mine, send_sems, recv_sems, k, (x, y, 1 - c)))
                recvs.append(_remote(theirs, theirs, send_sems, recv_sems, k, (x, y, 1 - c)))
            own = refs[a].at[pl.ds(pl.multiple_of(me * rp, 16), rp), :]
            sends.append(_remote(refs[n + a].at[l], own, send_sems, recv_sems, 4 * a + 3, (x, y, 1 - c)))
            recvs.append(_remote(refs[n + a].at[l], own, send_sems, recv_sems, 4 * a + 3, (x, y, 1 - c)))
        return sends, recvs

    return build, list(fulls) + list(shards), list(range(n)), 4 * n


def _pair_sum(a, b, c_idx, half_rows, out_dtype, name):
    n4, rp, W = a.shape
    tr = half_rows // 2
    nblk = half_rows // tr

    def body(c_ref, a_ref, b_ref, o_ref):
        o_ref[...] = (a_ref[...] + b_ref[...]).astype(o_ref.dtype)

    return pl.pallas_call(
        body,
        grid_spec=pltpu.PrefetchScalarGridSpec(
            num_scalar_prefetch=1, grid=(nblk,),
            in_specs=[pl.BlockSpec((n4, tr, W), lambda i, c: (0, c[0] * nblk + i, 0)),
                      pl.BlockSpec((n4, tr, W), lambda i, c: (0, i, 0))],
            out_specs=pl.BlockSpec((n4, tr, W), lambda i, c: (0, i, 0))),
        out_shape=jax.ShapeDtypeStruct((n4, half_rows, W), out_dtype),
        compiler_params=_params(("parallel",)), name=name)(c_idx, a, b)


def _sum4(land, own, me_c, stacked, l, name):
    _, R, W = land.shape
    tr = R // 2

    def body(s_ref, l_ref, o_ref, stacked_ref, out_ref):
        me = s_ref[0]
        acc = None
        for i in range(N_CHIPS):
            t = jnp.where(me == i, o_ref[...], l_ref[i]).astype(F32)
            acc = t if acc is None else acc + t
        out_ref[...] = acc

    return pl.pallas_call(
        body,
        grid_spec=pltpu.PrefetchScalarGridSpec(
            num_scalar_prefetch=1, grid=(R // tr,),
            in_specs=[pl.BlockSpec((N_CHIPS, tr, W), lambda i, s: (0, i, 0)),
                      pl.BlockSpec((None, tr, W), lambda i, s: (s[0], i, 0)),
                      pl.BlockSpec(memory_space=pltpu.HBM)],
            out_specs=pl.BlockSpec((None, None, tr, W), lambda i, s: (l, s[1], i, 0))),
        out_shape=jax.ShapeDtypeStruct(stacked.shape, F32), input_output_aliases={3: 0},
        compiler_params=_params(("parallel",)), name=name)(me_c, land, own, stacked)


def _reduce_d2d(g4):
    n = len(g4)

    def build(refs, fresh, send_sems, recv_sems):
        x, y, c = _coords()
        sends, recvs = [], []
        for a in range(n):
            hr = g4[a].shape[1] // 2
            src = refs[a].at[:, pl.ds(pl.multiple_of((1 - c) * hr, 8), hr), :]
            sends.append(_remote(src, fresh[a], send_sems, recv_sems, a, (x, y, 1 - c)))
            recvs.append(_remote(src, fresh[a], send_sems, recv_sems, a, (x, y, 1 - c)))
        return sends, recvs

    outs = [jax.ShapeDtypeStruct((N_CHIPS, g.shape[1] // 2, g.shape[2]), F32) for g in g4]
    return build, list(g4), outs, n


def _reduce_ici(q, lands):
    n = len(q)

    def build(refs, fresh, send_sems, recv_sems):
        x, y, c = _coords()
        me = 2 * x + y
        sends, recvs = [], []
        for a in range(n):
            for r, ((px, py), chip) in enumerate(_other_chips(x, y)):
                k = 3 * a + r
                sends.append(_remote(refs[a].at[chip], refs[n + a].at[me], send_sems, recv_sems, k, (px, py, c)))
                recvs.append(_remote(refs[a].at[chip], refs[n + a].at[chip], send_sems, recv_sems, k, (px, py, c)))
        return sends, recvs

    return build, list(q) + list(lands), [n + a for a in range(n)], 3 * n


def _reduce_share(r, l):
    n = len(r)

    def build(refs, fresh, send_sems, recv_sems):
        x, y, c = _coords()
        sends, recvs = [], []
        for a in range(n):
            mine, theirs = refs[a].at[l].at[c], refs[a].at[l].at[1 - c]
            sends.append(_remote(mine, mine, send_sems, recv_sems, a, (x, y, 1 - c)))
            recvs.append(_remote(mine, theirs, send_sems, recv_sems, a, (x, y, 1 - c)))
        return sends, recvs

    return build, list(r), list(range(n)), n


class _Weave:
    GATHER = {"up1": ("wg1", "wu1", "wd1"), "down1": ("win", "wbr", "wout", "wglu"), "proj_g": ("wg2", "wu2", "wd2")}
    REDUCE = {"dwg2": ("wg1", "wu1"), "dwu2": ("wd1", "wg2"), "dwd2": ("wu2", "wd2"),
              "dw_out": ("wbr", "wout", "wglu"), "d_x1": ("win",)}

    def __init__(self, shards, W, L):
        self.shards, self.W, self.L = shards, W, L
        x, y, c = _coords()
        self.c_idx = jnp.reshape(c, (1,)).astype(jnp.int32)
        self.me_c = jnp.stack([2 * x + y, c]).astype(jnp.int32)
        self.final = {k: lax.empty((L, 2, s.shape[1] // 2, s.shape[2]), F32) for k, s in shards.items()}
        self.st = None
        self.keys = ()

    def gather_first(self):
        got = _comm_call(_gather_ici([self.shards[k] for k in _BIG_ORDER], [self.W[k][0] for k in _BIG_ORDER], 0),
                         "gather0_ici")
        got = _comm_call(_gather_d2d(got, [self.shards[k] for k in _BIG_ORDER], 0), "gather0_d2d")
        for k, t in zip(_BIG_ORDER, got):
            self.W[k][0] = t

    def take(self, site, l):
        self.keys = ()
        if site in self.GATHER or site == "up2":
            nxt = l + 1
            if nxt >= self.L:
                return None
            if site == "up2":
                self.keys = _BIG_ORDER
                return _gather_d2d([self.W[k][nxt] for k in _BIG_ORDER], [self.shards[k] for k in _BIG_ORDER], nxt)
            self.keys = self.GATHER[site]
            return _gather_ici([self.shards[k] for k in self.keys], [self.W[k][nxt] for k in self.keys], nxt)
        st = self.st
        if st is None:
            return None
        if site == "dx2":
            self.keys = _BIG_ORDER
            return _reduce_d2d([st["g4"][k] for k in _BIG_ORDER])
        if site in self.REDUCE:
            self.keys = self.REDUCE[site]
            return _reduce_ici([st["q"][k] for k in self.keys],
                               [jnp.zeros(st["q"][k].shape, BF16) for k in self.keys])
        if site == "dw_in":
            self.keys = _BIG_ORDER
            return _reduce_share([self.final[k] for k in _BIG_ORDER], st["layer"])
        return None

    def done(self, site, l, outs):
        st = self.st
        if site in self.GATHER or site == "up2":
            for k, t in zip(self.keys, outs):
                self.W[k][l + 1] = t
        elif site == "dx2":
            st["q"] = {k: _pair_sum(st["g4"][k], t, self.c_idx, t.shape[1], BF16, "reduce_pair_sum")
                       for k, t in zip(self.keys, outs)}
        elif site in self.REDUCE:
            for k, t in zip(self.keys, outs):
                self.final[k] = _sum4(t, st["q"][k], self.me_c, self.final[k], st["layer"], "reduce_sum4")
        elif site == "dw_in":
            for k, t in zip(self.keys, outs):
                self.final[k] = t
            self.st = None

    def grads(self, l, big):
        assert self.st is None
        self.st = dict(layer=l, g4={k: g.reshape(N_CHIPS, g.shape[0] // N_CHIPS, g.shape[1]) for k, g in big.items()})

    def flush(self):
        for site, name in (("dx2", "reduce_d2d"), ("dwg2", "reduce_ici"), ("dwu2", "reduce_ici"),
                           ("dwd2", "reduce_ici"), ("dw_out", "reduce_ici"), ("d_x1", "reduce_ici"),
                           ("dw_in", "reduce_share")):
            self.done(site, None, _comm_call(self.take(site, None), name))


def _all_reduce_small(buf, name):
    R = buf.shape[0]

    def build(refs, fresh, send_sems, recv_sems):
        x, y, c = _coords()
        me = 4 * x + 2 * y + c
        sends, recvs = [], []
        k = 0
        for fx in range(2):
            for fy in range(2):
                for fc in range(2):
                    if fx + fy + fc == 0:
                        continue
                    px, py, pc = x ^ fx, y ^ fy, c ^ fc
                    sends.append(_remote(refs[0], refs[1].at[me], send_sems, recv_sems, k, (px, py, pc)))
                    recvs.append(_remote(refs[0], refs[1].at[4 * px + 2 * py + pc], send_sems, recv_sems, k,
                                         (px, py, pc)))
                    k += 1
        return sends, recvs

    got = _comm_call((build, [buf, jnp.zeros((N_DEV, R, 128), F32)], [1], N_DEV - 1), name + "_gather")[0]
    x, y, c = _coords()
    me = jnp.reshape(4 * x + 2 * y + c, (1,)).astype(jnp.int32)

    def body(s_ref, m_ref, own_ref, o_ref):
        acc = None
        for d in range(N_DEV):
            t = jnp.where(s_ref[0] == d, own_ref[...], m_ref[d])
            acc = t if acc is None else acc + t
        o_ref[...] = acc

    tr = max([t for t in range(8, min(R, 1024) + 1, 8) if R % t == 0], default=R)
    return pl.pallas_call(
        body,
        grid_spec=pltpu.PrefetchScalarGridSpec(
            num_scalar_prefetch=1, grid=(R // tr,),
            in_specs=[pl.BlockSpec((N_DEV, tr, 128), lambda i, s: (0, i, 0)),
                      pl.BlockSpec((tr, 128), lambda i, s: (i, 0))],
            out_specs=pl.BlockSpec((tr, 128), lambda i, s: (i, 0))),
        out_shape=jax.ShapeDtypeStruct((R, 128), F32),
        compiler_params=_params(("parallel",)), name=name + "_sum")(me, got, buf)


def _ssm_layouts(p, L):
    G = L * N_GROUPS
    lr = p["ssm_a_re"].reshape(G, 1, SSM_STATE)
    li = p["ssm_a_im"].reshape(G, 1, SSM_STATE)
    ldt = p["ssm_log_dt"].reshape(G, 1, 1)
    br = jnp.swapaxes(p["ssm_b_re"], 2, 3).reshape(G, SSM_GROUP, SSM_STATE)
    bi = jnp.swapaxes(p["ssm_b_im"], 2, 3).reshape(G, SSM_GROUP, SSM_STATE)
    are, aim, bre, bim = _disc_fwd(lr, li, ldt, br, bi)
    eye = jnp.eye(N_GROUPS, dtype=F32)
    abar = jnp.concatenate([are.reshape(L, 8, 128), aim.reshape(L, 8, 128)], axis=1)

    def b_blk(t):
        return jnp.einsum("lgcp,gh->lgchp", t.reshape(L, N_GROUPS, SSM_GROUP, SSM_STATE), eye).reshape(L, 256, 1024)

    def c_blk(t):
        return jnp.einsum("lgcp,gh->lgphc", t, eye).reshape(L, 1024, 256)

    bfull = jnp.concatenate([b_blk(bre), b_blk(bim)], axis=2).astype(BF16)
    cfull = jnp.concatenate([c_blk(p["ssm_c_re"]), -c_blk(p["ssm_c_im"])], axis=1).astype(BF16)
    return (lr, li, ldt, br, bi), abar, bfull, cfull


def _local_step(x, target, W, p, L, weave=None):
    S, Dm = x.shape
    dff = W["wg1"][0].shape[0]
    alpha = (2.0 * L) ** 0.25
    disc_in, abar, bfull, cfull = _ssm_layouts(p, L)
    row = lambda t, l: t[l][None]

    def carry(fn, site, l, *args, **kw):
        comm = weave.take(site, l) if weave is not None else None
        if comm is None:
            return fn(*args, **kw)
        res, couts = fn(*args, comm=comm, **kw)
        weave.done(site, l, couts)
        return res

    saved = []
    h, hb = x, x.astype(BF16)
    for l in range(L):
        sv = {"x0": hb}
        a1, b1, h1 = carry(_ffn_up, "up1", l, hb, W["wg1"][l], W["wu1"][l], dff, "ffn_up")
        x1, x1b, xh1, rs1 = carry(_mm_ln, "down1", l, h1, W["wd1"][l], h, row(p["ln1_g"], l), row(p["ln1_b"], l), 0.5,
                                  alpha, "ffn_down_ln", 1408)
        sv.update(a1=a1, b1=b1, h1=h1, x1=x1b, xh1=xh1, rs1=rs1)
        proj_a = _mm([(x1b, W["win"][l])], "nt", S, D_A, Dm, [F32], name="proj_a", tm=1024, tn=896, tk=1024)
        proj_g = carry(_mm, "proj_g", l, [(x1b, W["win"][l])], "nt", S, 3 * Dm, Dm, [F32], name="proj_g", tm=2048,
                       tn=256, tk=1024, b_off=D_A)
        mc = _conv_fwd(proj_a, p["conv_w"][l], row(p["conv_b"], l), "conv_fwd")
        bu = _mm([(proj_a, (bfull, l))], "nn", S, 2048, D_SSM, [F32], name="ssm_bu", tm=1024, tn=1024, tk=256,
                 a_off=3 * D_CONV)
        xs = _scan_fwd(bu.reshape(S, 16, 128), abar[l], "scan_fwd").reshape(S, 2048)
        y1, ms = _ssm_out(xs, proj_a, cfull[l], row(p["ssm_d"], l), W["wglu"][l], "ssm_out")
        sinks = p["attn_sinks"][l][None]
        ma = _attn_fwd(proj_a, sinks, "attn_fwd")
        merged = _merge_fwd(mc, ms, ma, proj_g, W["wbr"][l], "merge_fwd")
        x2, x2b, xh2, rs2 = _mm_ln(merged, W["wout"][l], x1, row(p["ln2_g"], l), row(p["ln2_b"], l), 1.0, alpha,
                                   "mix_out_ln", 1024)
        sv.update(proj_a=proj_a, proj_g=proj_g, mc=mc, ms=ms, ma=ma, xs=xs, y1=y1, merged=merged, x2=x2b, xh2=xh2,
                  rs2=rs2)
        a2, b2, h2 = carry(_ffn_up, "up2", l, x2b, W["wg2"][l], W["wu2"][l], dff, "ffn_up")
        h, hb, xh3, rs3 = _mm_ln(h2, W["wd2"][l], x2, row(p["ln3_g"], l), row(p["ln3_b"], l), 0.5, alpha,
                                 "ffn_down_ln", 1408)
        sv.update(a2=a2, b2=b2, h2=h2, xh3=xh3, rs3=rs3)
        saved.append(sv)

    dy, loss = _loss_head(h, target)
    big = [None] * L
    small = {k: [None] * L for k in ("ln1_g", "ln1_b", "ln2_g", "ln2_b", "ln3_g", "ln3_b", "conv_w", "conv_b", "ssm_d",
                                     "attn_sinks", "dabar", "dbfull", "dcfull")}

    def ffn_bwd(dy_out, x_in, a, b, hh, xh, rs, g, wg, wu, wd, l, sites):
        dres, df, dg, db = _ln_bwd(dy_out, xh, rs, g, 0.5, alpha, "ln_bwd")
        da, dbb = _ffn_dh(df, wd[l], a, b, "ffn_dh")
        dx = carry(_mm, sites[0], l, [(da, wg[l]), (dbb, wu[l])], "nn", S, Dm, dff, [F32], name="ffn_dx", tm=1024,
                   tn=1024, tk=dff // 2, add=dres)
        tn_kw = dict(tm=1408, tn=1024, tk=2048)
        dwg = carry(_mm, sites[1], l, [(da, x_in)], "tn", dff, Dm, S, [F32], name="ffn_dw_up", **tn_kw)
        dwu = carry(_mm, sites[2], l, [(dbb, x_in)], "tn", dff, Dm, S, [F32], name="ffn_dw_up", **tn_kw)
        dwd = carry(_mm, sites[3], l, [(hh, df)], "tn", dff, Dm, S, [F32], name="ffn_dw_down", **tn_kw)
        return dx, dwg, dwu, dwd, dg, db

    for l in reversed(range(L)):
        sv = saved[l]
        dx2, dwg2, dwu2, dwd2, small["ln3_g"][l], small["ln3_b"][l] = ffn_bwd(
            dy, sv["x2"], sv["a2"], sv["b2"], sv["h2"], sv["xh3"], sv["rs3"], row(p["ln3_g"], l), W["wg2"], W["wu2"],
            W["wd2"], l, ("dx2", "dwg2", "dwu2", "dwd2"))
        dres2, dmix, small["ln2_g"][l], small["ln2_b"][l] = _ln_bwd(dx2, sv["xh2"], sv["rs2"], row(p["ln2_g"], l), 1.0,
                                                                   alpha, "ln_bwd")
        dwout = carry(_mm, "dw_out", l, [(sv["merged"], dmix)], "tn", Dm, Dm, S, [F32], name="dw_out", tm=1024, tn=1024,
                      tk=2048)
        dmerged = _mm([(dmix, W["wout"][l])], "nt", S, Dm, Dm, [F32], name="d_merged", tm=1024, tn=1024, tk=512)
        dgates, dyb, dmc, dms, dma = _merge_bwd(dmerged, sv["mc"], sv["ms"], sv["ma"], sv["proj_g"], W["wbr"][l],
                                                "merge_bwd")
        dwbr = jnp.concatenate([
            _mm([(dyb, m)], "tn", Dm, hi - lo, S, [F32], name="dw_br", tm=1024, tn=512, tk=2048, a_off=b * Dm)
            for b, ((lo, hi), m) in enumerate(zip(_BR, (sv["mc"], sv["ms"], sv["ma"])))], axis=1)
        proj_a = sv["proj_a"]
        d_conv, small["conv_w"][l], small["conv_b"][l] = _conv_bwd(proj_a, dmc, p["conv_w"][l], row(p["conv_b"], l),
                                                                  "conv_bwd")
        dy1, y2, dgl, dxs, du_skip, small["ssm_d"][l] = _ssm_out_bwd(dms, sv["y1"], proj_a, cfull[l],
                                                                    row(p["ssm_d"], l), W["wglu"][l], "ssm_out_bwd")
        dwglu = _mm([(y2, dgl)], "tn", D_SSM, D_SSM, S, [F32], name="dw_glu", tk=2048)
        small["dcfull"][l] = _mm([(sv["xs"], dy1)], "tn", 2048, D_SSM, S, [F32], name="d_cfull", tm=1024, tk=1024)
        lam, small["dabar"][l] = _scan_bwd(dxs.reshape(S, 16, 128), sv["xs"].reshape(S, 16, 128), abar[l], "scan_bwd")
        lam = lam.reshape(S, 2048)
        du = _mm([(lam, (bfull, l))], "nt", S, D_SSM, 2048, [BF16], name="ssm_du", tm=1024, tk=512, add=du_skip)
        small["dbfull"][l] = _mm([(proj_a, lam)], "tn", D_SSM, 2048, S, [F32], name="d_bfull", tm=256, tn=1024,
                                 tk=1024, a_off=3 * D_CONV)
        dq, dk, dv, dsk = _attn_bwd(proj_a, dma, p["attn_sinks"][l][None], "attn_bwd")
        small["attn_sinks"][l] = dsk[:, 0]
        dproj = jnp.concatenate([d_conv, du, dq, dk, dv, dgates], axis=1)
        dx1 = carry(_mm, "d_x1", l, [(dproj, W["win"][l])], "nn", S, Dm, D_A + 3 * Dm, [F32], name="d_x1", tm=1024,
                    tn=1024, tk=(D_A + 3 * Dm) // 2, add=dres2)
        dwin = carry(_mm, "dw_in", l, [(dproj, sv["x1"])], "tn", D_A + 3 * Dm, Dm, S, [F32], name="dw_in", tm=2432,
                     tn=1024, tk=1024)
        dx0, dwg1, dwu1, dwd1, small["ln1_g"][l], small["ln1_b"][l] = ffn_bwd(
            dx1, sv["x0"], sv["a1"], sv["b1"], sv["h1"], sv["xh1"], sv["rs1"], row(p["ln1_g"], l), W["wg1"], W["wu1"],
            W["wd1"], l, ("dx1", "dwg1", "dwu1", "dwd1"))
        big[l] = dict(wg1=dwg1, wu1=dwu1, wd1=dwd1, win=dwin, wbr=dwbr, wout=dwout, wg2=dwg2, wu2=dwu2, wd2=dwd2,
                      wglu=dwglu)
        if weave is not None:
            weave.grads(l, big[l])
        dy = dx0

    small = {k: jnp.stack(v) for k, v in small.items()}
    eye = jnp.eye(N_GROUPS, dtype=F32)
    dabar = small.pop("dabar")
    dbf = small.pop("dbfull").reshape(L, N_GROUPS, SSM_GROUP, 2, N_GROUPS, SSM_STATE)
    dbbar = jnp.einsum("lgcrhp,gh->rlgcp", dbf, eye).reshape(2, L * N_GROUPS, SSM_GROUP, SSM_STATE)
    dcf = small.pop("dcfull").reshape(L, 2, N_GROUPS, SSM_STATE, N_GROUPS, SSM_GROUP)
    dc = jnp.einsum("lrgphc,gh->rlgcp", dcf, eye)
    G = L * N_GROUPS
    cts = (dabar[:, 0:8].reshape(G, 1, SSM_STATE), dabar[:, 8:16].reshape(G, 1, SSM_STATE), dbbar[0], dbbar[1])
    dlr, dli, dldt, dbr, dbi = _disc_bwd(*disc_in, cts)
    shp_b = (L, N_GROUPS, SSM_GROUP, SSM_STATE)
    small.update(
        ssm_a_re=dlr.reshape(L, N_GROUPS, SSM_STATE), ssm_a_im=dli.reshape(L, N_GROUPS, SSM_STATE),
        ssm_log_dt=dldt.reshape(L, N_GROUPS), ssm_b_re=jnp.swapaxes(dbr.reshape(shp_b), 2, 3),
        ssm_b_im=jnp.swapaxes(dbi.reshape(shp_b), 2, 3), ssm_c_re=dc[0], ssm_c_im=-dc[1],
        ln1_g=small["ln1_g"][:, 0], ln1_b=small["ln1_b"][:, 0], ln2_g=small["ln2_g"][:, 0],
        ln2_b=small["ln2_b"][:, 0], ln3_g=small["ln3_g"][:, 0], ln3_b=small["ln3_b"][:, 0],
        conv_b=small["conv_b"][:, 0], ssm_d=small["ssm_d"][:, 0])
    return loss, dy, big, small


_SMALL_ORDER = ("ln1_g", "ln1_b", "ln2_g", "ln2_b", "ln3_g", "ln3_b", "conv_w", "conv_b", "ssm_a_re", "ssm_a_im",
                "ssm_log_dt", "ssm_b_re", "ssm_b_im", "ssm_c_re", "ssm_c_im", "ssm_d", "attn_sinks")
_BIG_ORDER = ("wg1", "wu1", "wd1", "win", "wbr", "wout", "wg2", "wu2", "wd2", "wglu")
_WEIGHTS = ("ffn1_w_gate", "ffn1_w_up", "ffn1_w_down", "ln1_g", "ln1_b", "w_in", "conv_w", "conv_b", "ssm_a_re",
            "ssm_a_im", "ssm_log_dt", "ssm_b_re", "ssm_b_im", "ssm_c_re", "ssm_c_im", "ssm_d", "ssm_w_glu",
            "attn_sinks", "w_br_conv", "w_br_ssm", "w_br_attn", "w_out", "ln2_g", "ln2_b", "ffn2_w_gate",
            "ffn2_w_up", "ffn2_w_down", "ln3_g", "ln3_b")


def _weight_shards(w):
    t = lambda a: jnp.swapaxes(a, 1, 2).astype(BF16)
    wbr = jnp.concatenate([t(w["w_br_conv"]), t(w["w_br_ssm"]), t(w["w_br_attn"])], axis=2)
    return dict(wg1=t(w["ffn1_w_gate"]), wu1=t(w["ffn1_w_up"]), wd1=w["ffn1_w_down"].astype(BF16), win=t(w["w_in"]),
                wbr=wbr, wout=w["w_out"].astype(BF16), wg2=t(w["ffn2_w_gate"]), wu2=t(w["ffn2_w_up"]),
                wd2=w["ffn2_w_down"].astype(BF16), wglu=w["ssm_w_glu"].astype(BF16))


def _gather_conv_w(conv_w, chip):
    L = conv_w.shape[0]
    n = L * 3 * 64
    slots = lax.dynamic_update_slice(jnp.zeros((N_CHIPS, n), F32), conv_w.reshape(1, n), (chip, 0))
    got = (_all_reduce_small(slots.reshape(-1, 128), "conv_w_gather") * 0.5).reshape(N_CHIPS, L, 3, 64)
    return jnp.transpose(got, (1, 2, 0, 3)).reshape(L, 3, N_CHIPS * 64)


def kernel(x, ffn1_w_gate, ffn1_w_up, ffn1_w_down, ln1_g, ln1_b, w_in, conv_w, conv_b, ssm_a_re, ssm_a_im, ssm_log_dt, ssm_b_re, ssm_b_im, ssm_c_re, ssm_c_im, ssm_d, ssm_w_glu, attn_sinks, w_br_conv, w_br_ssm, w_br_attn, w_out, ln2_g, ln2_b, ffn2_w_gate, ffn2_w_up, ffn2_w_down, ln3_g, ln3_b, loss_target, m_ffn1_w_gate, m_ffn1_w_up, m_ffn1_w_down, m_ln1_g, m_ln1_b, m_w_in, m_conv_w, m_conv_b, m_ssm_a_re, m_ssm_a_im, m_ssm_log_dt, m_ssm_b_re, m_ssm_b_im, m_ssm_c_re, m_ssm_c_im, m_ssm_d, m_ssm_w_glu, m_attn_sinks, m_w_br_conv, m_w_br_ssm, m_w_br_attn, m_w_out, m_ln2_g, m_ln2_b, m_ffn2_w_gate, m_ffn2_w_up, m_ffn2_w_down, m_ln3_g, m_ln3_b, v_ffn1_w_gate, v_ffn1_w_up, v_ffn1_w_down, v_ln1_g, v_ln1_b, v_w_in, v_conv_w, v_conv_b, v_ssm_a_re, v_ssm_a_im, v_ssm_log_dt, v_ssm_b_re, v_ssm_b_im, v_ssm_c_re, v_ssm_c_im, v_ssm_d, v_ssm_w_glu, v_attn_sinks, v_w_br_conv, v_w_br_ssm, v_w_br_attn, v_w_out, v_ln2_g, v_ln2_b, v_ffn2_w_gate, v_ffn2_w_up, v_ffn2_w_down, v_ln3_g, v_ln3_b):
    args = dict(locals())
    w = {k: args[k] for k in _WEIGHTS}
    L = ln1_g.shape[0]
    cx, cy, cc = _coords()
    chip = 2 * cx + cy

    shards = _weight_shards(w)
    full = {k: [lax.empty((N_CHIPS * s.shape[1], s.shape[2]), BF16) for _ in range(L)] for k, s in shards.items()}
    weave = _Weave(shards, full, L)
    weave.gather_first()
    p = {k: w[k] for k in ("ln1_g", "ln1_b", "ln2_g", "ln2_b", "ln3_g", "ln3_b", "conv_b", "ssm_a_re", "ssm_a_im",
                           "ssm_log_dt", "ssm_b_re", "ssm_b_im", "ssm_c_re", "ssm_c_im", "ssm_d", "attn_sinks")}
    p["conv_w"] = _gather_conv_w(conv_w, chip)

    loss, grad_x, _, small = _local_step(x[0], loss_target[0], full, p, L, weave)
    weave.flush()
    loss = lax.psum(loss[0, 0], ("x", "y", "c"))

    sizes = [math.prod(small[k].shape) for k in _SMALL_ORDER]
    pad = (-sum(sizes)) % 1024
    flat = jnp.concatenate([small[k].reshape(-1) for k in _SMALL_ORDER] + [jnp.zeros((pad,), F32)])
    flat = _all_reduce_small(flat.reshape(-1, 128), "small_grads").reshape(-1)
    sm, off = {}, 0
    for k, n in zip(_SMALL_ORDER, sizes):
        sm[k] = flat[off:off + n].reshape(small[k].shape)
        off += n
    red = {k: t.reshape(L, 2 * t.shape[2], t.shape[3]) for k, t in weave.final.items()}
    tr = lambda a: jnp.swapaxes(a, 1, 2)
    grads = dict(sm)
    grads.update(
        ffn1_w_gate=tr(red["wg1"]), ffn1_w_up=tr(red["wu1"]), ffn1_w_down=red["wd1"], w_in=tr(red["win"]),
        w_br_conv=tr(red["wbr"][:, :, _BR[0][0]:_BR[0][1]]), w_br_ssm=tr(red["wbr"][:, :, _BR[1][0]:_BR[1][1]]),
        w_br_attn=tr(red["wbr"][:, :, _BR[2][0]:_BR[2][1]]), w_out=red["wout"], ffn2_w_gate=tr(red["wg2"]),
        ffn2_w_up=tr(red["wu2"]), ffn2_w_down=red["wd2"], ssm_w_glu=red["wglu"],
        conv_w=lax.dynamic_slice_in_dim(sm["conv_w"], chip * 64, 64, axis=2))

    outs = [[], [], [], []]
    for k in _WEIGHTS:
        d, nm, nv = _adamw(w[k], grads[k], args["m_" + k], args["v_" + k], "adamw")
        for lst, val in zip(outs, (grads[k], d, nm, nv)):
            lst.append(val)
    return (loss, grad_x[None], *outs[0], *outs[1], *outs[2], *outs[3])
```

```python
import functools
import math

import jax
import jax.numpy as jnp
from jax import lax
from jax.experimental import pallas as pl
from jax.experimental.pallas import tpu as pltpu

F32 = jnp.float32
BF16 = jnp.bfloat16

LN_EPS = 1e-5
D_CONV = 256
D_SSM = 256
N_GROUPS = 16
SSM_GROUP = 16
SSM_STATE = 64
N_Q_HEADS = 8
HEAD_DIM = 64
D_ATTN = 512
D_KV = 128
BLOCK = 128
D_A = 3 * D_CONV + D_SSM + D_ATTN + 2 * D_KV
ADAM_LR = 0.001
ADAM_B1 = 0.9
ADAM_B2 = 0.999
ADAM_EPS = 1e-08
ADAM_WD = 0.01
ADAM_STEP = 10

VMEM_LIMIT_BYTES = 56 * 1024 * 1024
MESH = pl.DeviceIdType.MESH
N_CHIPS = 4
N_DEV = 8


def _params(sem=None):
    return pltpu.CompilerParams(dimension_semantics=sem, vmem_limit_bytes=VMEM_LIMIT_BYTES)


def _op(op, block, imap):
    if isinstance(op, tuple):
        arr, l = op
        return arr, pl.BlockSpec((None,) + block, lambda *g: (l,) + imap(*g))
    return op, pl.BlockSpec(block, imap)


def _call(body, *, grid, in_specs, out_specs, out_shape, scratch=(), sem=None, name, ops, comm=None):
    if comm is None:
        return pl.pallas_call(body, grid=grid, in_specs=in_specs, out_specs=out_specs, out_shape=out_shape,
                              scratch_shapes=list(scratch), compiler_params=_params(sem), name=name)(*ops), []
    build, cins, couts, n_sem = comm
    n_in, n_out, n_scr, nci, nco, ng = len(in_specs), len(out_specs), len(scratch), len(cins), len(couts), len(grid)
    hbm = pl.BlockSpec(memory_space=pltpu.HBM)
    aliases = {n_in + o: n_out + j for j, o in enumerate(couts) if isinstance(o, int)}
    cshapes = [jax.ShapeDtypeStruct(cins[o].shape, cins[o].dtype) if isinstance(o, int) else o for o in couts]

    def hosted(*refs):
        refs = list(refs)
        main = refs[:n_in] + refs[n_in + nci:n_in + nci + n_out] + refs[n_in + nci + n_out + nco:-2]
        crefs, co = refs[n_in:n_in + nci], refs[n_in + nci + n_out:n_in + nci + n_out + nco]
        send_sems, recv_sems = refs[-2:]
        fresh = []
        for j, o in enumerate(couts):
            if isinstance(o, int):
                crefs[o] = co[j]
            else:
                fresh.append(co[j])

        def start():
            for cp in build(crefs, fresh, send_sems, recv_sems)[0]:
                cp.start()

        def finish():
            sends, recvs = build(crefs, fresh, send_sems, recv_sems)
            for cp in recvs:
                cp.wait_recv()
            for cp in sends:
                cp.wait_send()

        if ng == 0:
            start()
            finish()
            return
        ids = [pl.program_id(d) for d in range(ng)]
        first = functools.reduce(jnp.logical_and, [ids[d] == 0 for d in range(ng)])
        last = functools.reduce(jnp.logical_and, [ids[d] == grid[d] - 1 for d in range(ng)])
        pl.when(first)(start)
        body(*main)
        pl.when(last)(finish)

    res = pl.pallas_call(
        hosted, grid=grid, in_specs=list(in_specs) + [hbm] * nci, out_specs=list(out_specs) + [hbm] * nco,
        out_shape=list(out_shape) + cshapes,
        scratch_shapes=list(scratch) + [pltpu.SemaphoreType.DMA((n_sem,)), pltpu.SemaphoreType.DMA((n_sem,))],
        input_output_aliases=aliases,
        compiler_params=pltpu.CompilerParams(dimension_semantics=("arbitrary",) * ng if ng else None,
                                             vmem_limit_bytes=VMEM_LIMIT_BYTES, has_side_effects=True),
        name=name)(*ops, *cins)
    return res[:n_out], res[n_out:]


def _mm(pairs, mode, m, n, k, out_dtypes, *, name, tm=512, tn=512, tk=512, add=None, a_off=0, b_off=0, comm=None):
    tm, tn, tk = min(tm, m), min(tn, n), min(tk, k)
    assert m % tm == 0 and n % tn == 0 and k % tk == 0 and b_off % tn == 0, (name, m, n, k, tm, tn, tk)
    nk, npair, jo = k // tk, len(pairs), b_off // tn
    if mode == "nn":
        ao = a_off // tk
        ab, ai, bb, bi = (tm, tk), (lambda i, j, kk: (i, kk + ao)), (tk, tn), (lambda i, j, kk: (kk, j))
        dims = (((1,), (0,)), ((), ()))
    elif mode == "nt":
        ao = a_off // tk
        ab, ai, bb, bi = (tm, tk), (lambda i, j, kk: (i, kk + ao)), (tn, tk), (lambda i, j, kk: (j + jo, kk))
        dims = (((1,), (1,)), ((), ()))
    else:
        ao = a_off // tm
        ab, ai, bb, bi = (tk, tm), (lambda i, j, kk: (kk, i + ao)), (tk, tn), (lambda i, j, kk: (kk, j))
        dims = (((0,), (0,)), ((), ()))
    assert a_off % (tm if mode == "tn" else tk) == 0, (name, a_off)
    ops, specs = [], []
    for a, b in pairs:
        for o, blk, im in ((a, ab, ai), (b, bb, bi)):
            arr, sp = _op(o, blk, im)
            ops.append(arr)
            specs.append(sp)
    has_add = add is not None
    if has_add:
        ops.append(add)
        specs.append(pl.BlockSpec((tm, tn), lambda i, j, kk: (i, j)))
    nout = len(out_dtypes)

    def body(*refs):
        outs, acc = refs[2 * npair + has_add:2 * npair + has_add + nout], refs[-1]
        kk = pl.program_id(2)
        t = None
        for p in range(npair):
            d = lax.dot_general(refs[2 * p][...].astype(BF16), refs[2 * p + 1][...].astype(BF16), dims,
                                preferred_element_type=F32)
            t = d if t is None else t + d

        def finish(r):
            if has_add:
                r = r + refs[2 * npair][...]
            for o in outs:
                o[...] = r.astype(o.dtype)

        if nk == 1:
            finish(t)
            return

        @pl.when(kk == 0)
        def _():
            acc[...] = jnp.zeros_like(acc)

        acc[...] += t

        @pl.when(kk == nk - 1)
        def _():
            finish(acc[...])

    res, couts = _call(
        body, grid=(m // tm, n // tn, nk), in_specs=specs,
        out_specs=[pl.BlockSpec((tm, tn), lambda i, j, kk: (i, j))] * nout,
        out_shape=[jax.ShapeDtypeStruct((m, n), dt) for dt in out_dtypes],
        scratch=[pltpu.VMEM((tm, tn), F32)], sem=("parallel", "parallel", "arbitrary"), name=name, ops=ops, comm=comm)
    res = res[0] if nout == 1 else res
    return res if comm is None else (res, couts)


def _rows(width, col=0, tm=None):
    return pl.BlockSpec((tm, width), lambda i: (i, col))


def _whole(shape):
    nd = len(shape)
    return pl.BlockSpec(shape, lambda i: (0,) * nd)


def _sigmoid(x):
    return 0.5 * jnp.tanh(0.5 * x) + 0.5


def _mm_ln(a, w, x, g, b, s, alpha, name, tk, comm=None):
    S, K = a.shape
    Dm = x.shape[1]
    tm, tk = min(1024, S), min(tk, K)
    nk = K // tk

    def body(a_ref, w_ref, x_ref, g_ref, b_ref, y_ref, yb_ref, xh_ref, rs_ref, acc):
        kk = pl.program_id(1)
        d = jnp.dot(a_ref[...], w_ref[...], preferred_element_type=F32)

        if nk > 1:
            @pl.when(kk == 0)
            def _():
                acc[...] = d

            @pl.when((kk > 0) & (kk < nk - 1))
            def _():
                acc[...] += d

        @pl.when(kk == nk - 1)
        def _():
            z = alpha * x_ref[...] + s * (d if nk == 1 else acc[...] + d)
            mu = jnp.mean(z, axis=-1, keepdims=True)
            zc = z - mu
            var = jnp.mean(zc * zc, axis=-1, keepdims=True)
            rstd = lax.rsqrt(var + LN_EPS)
            xh = zc * rstd
            y = xh * g_ref[...] + b_ref[...]
            y_ref[...] = y
            yb_ref[...] = y.astype(BF16)
            xh_ref[...] = xh
            rs_ref[...] = rstd

    wa, ws = _op(w, (tk, Dm), lambda i, kk: (kk, 0))
    row = pl.BlockSpec((tm, Dm), lambda i, kk: (i, 0))
    vec = pl.BlockSpec((1, Dm), lambda i, kk: (0, 0))
    res, couts = _call(
        body, grid=(S // tm, nk),
        in_specs=[pl.BlockSpec((tm, tk), lambda i, kk: (i, kk)), ws, row, vec, vec],
        out_specs=[row, row, row, pl.BlockSpec((tm, 1), lambda i, kk: (i, 0))],
        out_shape=[jax.ShapeDtypeStruct((S, Dm), F32), jax.ShapeDtypeStruct((S, Dm), BF16),
                   jax.ShapeDtypeStruct((S, Dm), F32), jax.ShapeDtypeStruct((S, 1), F32)],
        scratch=[pltpu.VMEM((tm, Dm), F32)], sem=("parallel", "arbitrary"), name=name, ops=[a, wa, x, g, b],
        comm=comm)
    return res if comm is None else (res, couts)


def _ln_bwd(dy, xh, rs, g, s, alpha, name):
    S, Dm = dy.shape
    tm = min(256, S)

    def body(dy_ref, xh_ref, rs_ref, g_ref, dres_ref, dbr_ref, dg_ref, db_ref):
        @pl.when(pl.program_id(0) == 0)
        def _():
            dg_ref[...] = jnp.zeros_like(dg_ref)
            db_ref[...] = jnp.zeros_like(db_ref)

        dy, xh = dy_ref[...], xh_ref[...]
        dyg = dy * g_ref[...]
        m1 = jnp.mean(dyg, axis=-1, keepdims=True)
        m2 = jnp.mean(dyg * xh, axis=-1, keepdims=True)
        dz = rs_ref[...] * (dyg - m1 - xh * m2)
        dres_ref[...] = alpha * dz
        dbr_ref[...] = (s * dz).astype(BF16)
        dg_ref[...] += jnp.sum(dy * xh, axis=0, keepdims=True)
        db_ref[...] += jnp.sum(dy, axis=0, keepdims=True)

    return pl.pallas_call(
        body, grid=(S // tm,),
        in_specs=[_rows(Dm, tm=tm), _rows(Dm, tm=tm), _rows(1, tm=tm), _whole((1, Dm))],
        out_specs=[_rows(Dm, tm=tm), _rows(Dm, tm=tm), _whole((1, Dm)), _whole((1, Dm))],
        out_shape=[jax.ShapeDtypeStruct((S, Dm), F32), jax.ShapeDtypeStruct((S, Dm), BF16),
                   jax.ShapeDtypeStruct((1, Dm), F32), jax.ShapeDtypeStruct((1, Dm), F32)],
        compiler_params=_params(("arbitrary",)), name=name)(dy, xh, rs, g)


def _ffn_up(x, wg, wu, dff, name, comm=None):
    S, Dm = x.shape
    tm, tn = min(512, S), dff // 2

    def body(x_ref, wg_ref, wu_ref, a_ref, b_ref, h_ref):
        xb = x_ref[...]
        dims = (((1,), (1,)), ((), ()))
        a = lax.dot_general(xb, wg_ref[...], dims, preferred_element_type=F32)
        b = lax.dot_general(xb, wu_ref[...], dims, preferred_element_type=F32)
        a_ref[...] = a.astype(BF16)
        b_ref[...] = b.astype(BF16)
        h_ref[...] = (a * _sigmoid(a) * b).astype(BF16)

    wga, wgs = _op(wg, (tn, Dm), lambda i, j: (j, 0))
    wua, wus = _op(wu, (tn, Dm), lambda i, j: (j, 0))
    ob = pl.BlockSpec((tm, tn), lambda i, j: (i, j))
    res, couts = _call(
        body, grid=(S // tm, dff // tn),
        in_specs=[pl.BlockSpec((tm, Dm), lambda i, j: (i, 0)), wgs, wus], out_specs=[ob, ob, ob],
        out_shape=[jax.ShapeDtypeStruct((S, dff), BF16)] * 3, sem=("parallel", "parallel"), name=name,
        ops=[x, wga, wua], comm=comm)
    return res if comm is None else (res, couts)


def _ffn_dh(df, wd, a, b, name):
    S, Dm = df.shape
    dff = a.shape[1]
    tm, tn = min(512, S), dff // 2

    def body(df_ref, wd_ref, a_ref, b_ref, da_ref, db_ref):
        dh = lax.dot_general(df_ref[...], wd_ref[...], (((1,), (1,)), ((), ())), preferred_element_type=F32)
        a, b = a_ref[...].astype(F32), b_ref[...].astype(F32)
        sg = _sigmoid(a)
        da_ref[...] = (dh * b * (sg * (1.0 + a * (1.0 - sg)))).astype(BF16)
        db_ref[...] = (dh * (a * sg)).astype(BF16)

    wda, wds = _op(wd, (tn, Dm), lambda i, j: (j, 0))
    ob = pl.BlockSpec((tm, tn), lambda i, j: (i, j))
    return pl.pallas_call(
        body, grid=(S // tm, dff // tn),
        in_specs=[pl.BlockSpec((tm, Dm), lambda i, j: (i, 0)), wds, ob, ob], out_specs=[ob, ob],
        out_shape=[jax.ShapeDtypeStruct((S, dff), BF16), jax.ShapeDtypeStruct((S, dff), BF16)],
        compiler_params=_params(("parallel", "parallel")), name=name)(df, wda, a, b)


def _halo_prev(width, col, tm):
    return pl.BlockSpec((8, width), lambda i: (jnp.maximum(i * (tm // 8) - 1, 0), col))


def _halo_next(width, col, tm, S):
    return pl.BlockSpec((8, width), lambda i: (jnp.minimum((i + 1) * (tm // 8), S // 8 - 1), col))


def _shift_down(prev8, cur, n):
    ext = jnp.concatenate([prev8, cur], axis=0)
    return pltpu.roll(ext, n, axis=0)[8:]


def _shift_up(cur, next8, n):
    ext = jnp.concatenate([cur, next8], axis=0)
    return pltpu.roll(ext, ext.shape[0] - n, axis=0)[:cur.shape[0]]


def _conv_fwd(proj_a, conv_w, conv_b, name):
    S = proj_a.shape[0]
    tm = min(512, S)
    C = D_CONV

    def body(bg_ref, cg_ref, h_ref, cgp_ref, hp_ref, w_ref, cb_ref, out_ref):
        z = cg_ref[...] * h_ref[...]
        zp = jnp.where(pl.program_id(0) > 0, cgp_ref[...] * hp_ref[...], 0.0)
        w = w_ref[...]
        y = w[2:3] * z + w[1:2] * _shift_down(zp, z, 1) + w[0:1] * _shift_down(zp, z, 2) + cb_ref[...]
        out_ref[...] = (bg_ref[...] * y).astype(BF16)

    return pl.pallas_call(
        body, grid=(S // tm,),
        in_specs=[_rows(C, 0, tm), _rows(C, 1, tm), _rows(C, 2, tm), _halo_prev(C, 1, tm), _halo_prev(C, 2, tm),
                  _whole((3, C)), _whole((1, C))],
        out_specs=_rows(C, 0, tm), out_shape=jax.ShapeDtypeStruct((S, C), BF16),
        compiler_params=_params(("parallel",)), name=name)(proj_a, proj_a, proj_a, proj_a, proj_a, conv_w, conv_b)


def _conv_bwd(proj_a, dmc, conv_w, conv_b, name):
    S = proj_a.shape[0]
    tm = min(512, S)
    C = D_CONV
    nblk = S // tm

    def body(bg_ref, cg_ref, h_ref, cgp_ref, hp_ref, bgn_ref, d_ref, dn_ref, w_ref, cb_ref, out_ref, dw_ref,
             dcb_ref):
        i = pl.program_id(0)

        @pl.when(i == 0)
        def _():
            dw_ref[...] = jnp.zeros_like(dw_ref)
            dcb_ref[...] = jnp.zeros_like(dcb_ref)

        bg, cg, h, d = bg_ref[...], cg_ref[...], h_ref[...], d_ref[...]
        z = cg * h
        zp = jnp.where(i > 0, cgp_ref[...] * hp_ref[...], 0.0)
        w = w_ref[...]
        z1, z2 = _shift_down(zp, z, 1), _shift_down(zp, z, 2)
        y = w[2:3] * z + w[1:2] * z1 + w[0:1] * z2 + cb_ref[...]
        dy = d * bg
        dyn = jnp.where(i < nblk - 1, dn_ref[...] * bgn_ref[...], 0.0)
        dz = w[2:3] * dy + w[1:2] * _shift_up(dy, dyn, 1) + w[0:1] * _shift_up(dy, dyn, 2)
        out_ref[:, 0:C] = (d * y).astype(BF16)
        out_ref[:, C:2 * C] = (dz * h).astype(BF16)
        out_ref[:, 2 * C:3 * C] = (dz * cg).astype(BF16)
        dw_ref[0:1, :] += jnp.sum(dy * z2, axis=0, keepdims=True)
        dw_ref[1:2, :] += jnp.sum(dy * z1, axis=0, keepdims=True)
        dw_ref[2:3, :] += jnp.sum(dy * z, axis=0, keepdims=True)
        dcb_ref[...] += jnp.sum(dy, axis=0, keepdims=True)

    return pl.pallas_call(
        body, grid=(nblk,),
        in_specs=[_rows(C, 0, tm), _rows(C, 1, tm), _rows(C, 2, tm), _halo_prev(C, 1, tm), _halo_prev(C, 2, tm),
                  _halo_next(C, 0, tm, S), _rows(C, 0, tm), _halo_next(C, 0, tm, S), _whole((3, C)), _whole((1, C))],
        out_specs=[_rows(3 * C, 0, tm), _whole((3, C)), _whole((1, C))],
        out_shape=[jax.ShapeDtypeStruct((S, 3 * C), BF16), jax.ShapeDtypeStruct((3, C), F32),
                   jax.ShapeDtypeStruct((1, C), F32)],
        compiler_params=_params(("arbitrary",)), name=name)(
            proj_a, proj_a, proj_a, proj_a, proj_a, proj_a, dmc, dmc, conv_w, conv_b)


def _disc_math(lr, li, ldt, br, bi):
    dt = jnp.exp(ldt)
    mag = jnp.exp(lr * dt)
    ang = li * dt
    are = mag * jnp.cos(ang)
    aim = mag * jnp.sin(ang)
    nr = are - 1.0
    den = lr * lr + li * li
    cre = (nr * lr + aim * li) / den
    cim = (aim * lr - nr * li) / den
    return are, aim, cre * br - cim * bi, cre * bi + cim * br


def _disc_fwd(lr, li, ldt, br, bi):
    shapes = [lr.shape, lr.shape, br.shape, br.shape]

    def body(lr_ref, li_ref, ldt_ref, br_ref, bi_ref, *outs):
        for o, v in zip(outs, _disc_math(lr_ref[...], li_ref[...], ldt_ref[...], br_ref[...], bi_ref[...])):
            o[...] = v

    return pl.pallas_call(body, out_shape=[jax.ShapeDtypeStruct(s, F32) for s in shapes],
                          compiler_params=_params(), name="ssm_disc")(lr, li, ldt, br, bi)


def _disc_bwd(lr, li, ldt, br, bi, cts):
    shapes = [lr.shape, lr.shape, ldt.shape, br.shape, br.shape]

    def body(lr_ref, li_ref, ldt_ref, br_ref, bi_ref, c0, c1, c2, c3, *outs):
        _, vjp = jax.vjp(_disc_math, lr_ref[...], li_ref[...], ldt_ref[...], br_ref[...], bi_ref[...])
        for o, v in zip(outs, vjp((c0[...], c1[...], c2[...], c3[...]))):
            o[...] = v

    return pl.pallas_call(body, out_shape=[jax.ShapeDtypeStruct(s, F32) for s in shapes],
                          compiler_params=_params(), name="ssm_disc_bwd")(lr, li, ldt, br, bi, *cts)


def _scan_fwd(bu, abar, name):
    S = bu.shape[0]
    tb = min(256, S)

    def body(bu_ref, a_ref, xs_ref, st_ref):
        @pl.when(pl.program_id(0) == 0)
        def _():
            st_ref[...] = jnp.zeros_like(st_ref)

        ar, ai = a_ref[0:8, :], a_ref[8:16, :]

        def step(t, c):
            xr, xi = c
            nr = ar * xr - ai * xi + bu_ref[t, 0:8, :]
            ni = ar * xi + ai * xr + bu_ref[t, 8:16, :]
            xs_ref[t, 0:8, :] = nr
            xs_ref[t, 8:16, :] = ni
            return nr, ni

        xr, xi = lax.fori_loop(0, tb, step, (st_ref[0:8, :], st_ref[8:16, :]), unroll=8)
        st_ref[0:8, :] = xr
        st_ref[8:16, :] = xi

    blk = pl.BlockSpec((tb, 16, 128), lambda i: (i, 0, 0))
    return pl.pallas_call(
        body, grid=(S // tb,), in_specs=[blk, _whole((16, 128))], out_specs=blk,
        out_shape=jax.ShapeDtypeStruct((S, 16, 128), F32), scratch_shapes=[pltpu.VMEM((16, 128), F32)],
        compiler_params=_params(("arbitrary",)), name=name)(bu, abar)


def _scan_bwd(dxs, xs, abar, name):
    S = dxs.shape[0]
    tb = min(256, S)
    nblk = S // tb

    def body(d_ref, x_ref, xp_ref, a_ref, lam_ref, da_ref, st_ref):
        i = pl.program_id(0)

        @pl.when(i == 0)
        def _():
            st_ref[...] = jnp.zeros_like(st_ref)
            da_ref[...] = jnp.zeros_like(da_ref)

        ar, ai = a_ref[0:8, :], a_ref[8:16, :]

        def one(t, c, pr, pi):
            lr, li, gr, gi = c
            nr = d_ref[t, 0:8, :] + ar * lr + ai * li
            ni = d_ref[t, 8:16, :] - ai * lr + ar * li
            lam_ref[t, 0:8, :] = nr
            lam_ref[t, 8:16, :] = ni
            return nr, ni, gr + nr * pr + ni * pi, gi - nr * pi + ni * pr

        def step(s, c):
            t = tb - 1 - s
            return one(t, c, x_ref[t - 1, 0:8, :], x_ref[t - 1, 8:16, :])

        c = (st_ref[0:8, :], st_ref[8:16, :], jnp.zeros((8, 128), F32), jnp.zeros((8, 128), F32))
        c = lax.fori_loop(0, tb - 1, step, c, unroll=8)
        first = i == nblk - 1
        pr = jnp.where(first, 0.0, xp_ref[0, 0:8, :])
        pi = jnp.where(first, 0.0, xp_ref[0, 8:16, :])
        lr, li, gr, gi = one(0, c, pr, pi)
        st_ref[0:8, :] = lr
        st_ref[8:16, :] = li
        da_ref[0:8, :] += gr
        da_ref[8:16, :] += gi

    blk = pl.BlockSpec((tb, 16, 128), lambda i: (nblk - 1 - i, 0, 0))
    prev = pl.BlockSpec((1, 16, 128), lambda i: (jnp.maximum((nblk - 1 - i) * tb - 1, 0), 0, 0))
    return pl.pallas_call(
        body, grid=(nblk,), in_specs=[blk, blk, prev, _whole((16, 128))], out_specs=[blk, _whole((16, 128))],
        out_shape=[jax.ShapeDtypeStruct((S, 16, 128), F32), jax.ShapeDtypeStruct((16, 128), F32)],
        scratch_shapes=[pltpu.VMEM((16, 128), F32)],
        compiler_params=_params(("arbitrary",)), name=name)(dxs, xs, xs, abar)


def _gelu(x):
    return 0.5 * x * (1.0 + jnp.tanh(0.7978845608028654 * (x + 0.044715 * x * x * x)))


def _gelu_grad(x):
    t = jnp.tanh(0.7978845608028654 * (x + 0.044715 * x * x * x))
    return 0.5 * (1.0 + t) + 0.5 * x * (1.0 - t * t) * 0.7978845608028654 * (1.0 + 3.0 * 0.044715 * x * x)


def _ssm_out(xs, proj_a, cfull, dskip, wglu, name):
    S = xs.shape[0]
    tm = min(256, S)
    C = D_SSM

    def body(xs_ref, u_ref, c_ref, d_ref, wg_ref, y1_ref, ms_ref):
        y1 = jnp.dot(xs_ref[...].astype(BF16), c_ref[...], preferred_element_type=F32) + d_ref[...] * u_ref[...]
        y2 = _gelu(y1)
        gl = jnp.dot(y2.astype(BF16), wg_ref[...], preferred_element_type=F32)
        y1_ref[...] = y1
        ms_ref[...] = (y2 * _sigmoid(gl)).astype(BF16)

    wga, wgs = _op(wglu, (C, C), lambda i: (0, 0))
    return pl.pallas_call(
        body, grid=(S // tm,),
        in_specs=[_rows(2 * 1024, 0, tm), _rows(C, 3, tm), _whole((2 * 1024, C)), _whole((1, C)), wgs],
        out_specs=[_rows(C, 0, tm), _rows(C, 0, tm)],
        out_shape=[jax.ShapeDtypeStruct((S, C), F32), jax.ShapeDtypeStruct((S, C), BF16)],
        compiler_params=_params(("parallel",)), name=name)(xs, proj_a, cfull, dskip, wga)


def _ssm_out_bwd(dms, y1, proj_a, cfull, dskip, wglu, name):
    S = y1.shape[0]
    tm = min(256, S)
    C = D_SSM

    def body(dms_ref, y1_ref, u_ref, c_ref, d_ref, wg_ref, dy1_ref, y2_ref, dgl_ref, dxs_ref, du_ref, dd_ref):
        @pl.when(pl.program_id(0) == 0)
        def _():
            dd_ref[...] = jnp.zeros_like(dd_ref)

        dms, y1 = dms_ref[...], y1_ref[...]
        y2 = _gelu(y1)
        y2b = y2.astype(BF16)
        sg = _sigmoid(jnp.dot(y2b, wg_ref[...], preferred_element_type=F32))
        dgl = (dms * y2 * sg * (1.0 - sg)).astype(BF16)
        dy2 = dms * sg + lax.dot_general(dgl, wg_ref[...], (((1,), (1,)), ((), ())), preferred_element_type=F32)
        dy1 = dy2 * _gelu_grad(y1)
        dy1b = dy1.astype(BF16)
        dy1_ref[...] = dy1b
        y2_ref[...] = y2b
        dgl_ref[...] = dgl
        dxs_ref[...] = lax.dot_general(dy1b, c_ref[...], (((1,), (1,)), ((), ())), preferred_element_type=F32)
        du_ref[...] = d_ref[...] * dy1
        dd_ref[...] += jnp.sum(dy1 * u_ref[...], axis=0, keepdims=True)

    wga, wgs = _op(wglu, (C, C), lambda i: (0, 0))
    rc = _rows(C, 0, tm)
    return pl.pallas_call(
        body, grid=(S // tm,),
        in_specs=[rc, rc, _rows(C, 3, tm), _whole((2 * 1024, C)), _whole((1, C)), wgs],
        out_specs=[rc, rc, rc, _rows(2 * 1024, 0, tm), rc, _whole((1, C))],
        out_shape=[jax.ShapeDtypeStruct((S, C), BF16), jax.ShapeDtypeStruct((S, C), BF16),
                   jax.ShapeDtypeStruct((S, C), BF16), jax.ShapeDtypeStruct((S, 2 * 1024), F32),
                   jax.ShapeDtypeStruct((S, C), F32), jax.ShapeDtypeStruct((1, C), F32)],
        compiler_params=_params(("arbitrary",)), name=name)(dms, y1, proj_a, cfull, dskip, wga)


_NT = (((1,), (1,)), ((), ()))
_TN = (((0,), (0,)), ((), ()))


def _attn_heads(q_ref, kp_ref, kc_ref, vp_ref, vc_ref):
    lane = lax.broadcasted_iota(jnp.int32, (BLOCK, 128), 1)
    kk = jnp.concatenate([kp_ref[...], kc_ref[...]], axis=0).astype(BF16)
    vv = jnp.concatenate([vp_ref[...], vc_ref[...]], axis=0).astype(BF16)
    kk_r, vv_r = pltpu.roll(kk, 64, axis=1), pltpu.roll(vv, 64, axis=1)
    heads = []
    for hq in range(N_Q_HEADS):
        j, e = hq // 2, hq % 2
        qj = (q_ref[:, 128 * j:128 * (j + 1)] * (HEAD_DIM ** -0.5)).astype(BF16)
        own = (lane >= 64) if e else (lane < 64)
        aligned = e == hq // 4
        heads.append((own, jnp.where(own, qj, jnp.zeros_like(qj)), kk if aligned else kk_r, vv if aligned else vv_r,
                      aligned))
    return heads


def _attn_probs(i, heads, s_ref):
    n = N_Q_HEADS * BLOCK
    s = jnp.concatenate([lax.dot_general(qm, ks, _NT, preferred_element_type=F32) for _, qm, ks, _, _ in heads],
                        axis=0)
    row = lax.broadcasted_iota(jnp.int32, (n, 2 * BLOCK), 0) & (BLOCK - 1)
    col = lax.broadcasted_iota(jnp.int32, (n, 2 * BLOCK), 1)
    mask = (col > row) & (col <= row + BLOCK) & ((col >= BLOCK) | (i > 0))
    s = jnp.where(mask, s, -1e30)
    sink = jnp.concatenate([jnp.full((BLOCK, 1), s_ref[0, hq], F32) for hq in range(N_Q_HEADS)], axis=0)
    m = jnp.maximum(jnp.max(s, axis=1, keepdims=True), sink)
    p = jnp.exp(s - m)
    es = jnp.exp(sink - m)
    inv = 1.0 / (jnp.sum(p, axis=1, keepdims=True) + es)
    return p * inv, es * inv


def _attn_fwd(proj_a, sinks, name):
    S = proj_a.shape[0]
    nb = S // BLOCK

    def body(q_ref, kp_ref, kc_ref, vp_ref, vc_ref, s_ref, out_ref):
        i = pl.program_id(0)
        heads = _attn_heads(q_ref, kp_ref, kc_ref, vp_ref, vc_ref)
        row = lax.broadcasted_iota(jnp.int32, (BLOCK, 2 * BLOCK), 0)
        col = lax.broadcasted_iota(jnp.int32, (BLOCK, 2 * BLOCK), 1)
        mask = (col > row) & (col <= row + BLOCK) & ((col >= BLOCK) | (i > 0))
        outs = []
        for hq, (_, qm, ks, vs, _) in enumerate(heads):
            s = jnp.where(mask, lax.dot_general(qm, ks, _NT, preferred_element_type=F32), -1e30)
            sink = s_ref[0, hq]
            m = jnp.maximum(jnp.max(s, axis=1, keepdims=True), sink)
            p = jnp.exp(s - m)
            inv = 1.0 / (jnp.sum(p, axis=1, keepdims=True) + jnp.exp(sink - m))
            outs.append(jnp.dot((p * inv).astype(BF16), vs, preferred_element_type=F32))
        for j in range(4):
            out_ref[:, 128 * j:128 * (j + 1)] = jnp.where(heads[2 * j][0], outs[2 * j], outs[2 * j + 1]).astype(BF16)

    prev = lambda c: pl.BlockSpec((BLOCK, 128), lambda i: (jnp.maximum(i - 1, 0), c))
    cur = lambda c: pl.BlockSpec((BLOCK, 128), lambda i: (i, c))
    return pl.pallas_call(
        body, grid=(nb,),
        in_specs=[_rows(D_ATTN, 2, BLOCK), prev(12), cur(12), prev(13), cur(13),
                  pl.BlockSpec(memory_space=pltpu.SMEM)],
        out_specs=_rows(D_ATTN, 0, BLOCK), out_shape=jax.ShapeDtypeStruct((S, D_ATTN), BF16),
        compiler_params=_params(("parallel",)), name=name)(proj_a, proj_a, proj_a, proj_a, proj_a, sinks)


def _attn_bwd(proj_a, dout, sinks, name):
    S = proj_a.shape[0]
    nb = S // BLOCK

    def body(q_ref, kp_ref, kc_ref, vp_ref, vc_ref, do_ref, s_ref, out_ref, dk_ref, dv_ref, ds_ref, ck_ref, cv_ref):
        i = pl.program_id(0)

        @pl.when(i == 0)
        def _():
            ds_ref[...] = jnp.zeros_like(ds_ref)
            ck_ref[...] = jnp.zeros_like(ck_ref)
            cv_ref[...] = jnp.zeros_like(cv_ref)

        @pl.when(i < nb)
        def _():
            heads = _attn_heads(q_ref, kp_ref, kc_ref, vp_ref, vc_ref)
            pn, psink = _attn_probs(i, heads, s_ref)
            doms = []
            for hq, (own, _, _, _, _) in enumerate(heads):
                doj = do_ref[:, 128 * (hq // 2):128 * (hq // 2 + 1)].astype(BF16)
                doms.append(jnp.where(own, doj, jnp.zeros_like(doj)))
            dp = jnp.concatenate([lax.dot_general(dom, vs, _NT, preferred_element_type=F32)
                                  for dom, (_, _, _, vs, _) in zip(doms, heads)], axis=0)
            delta = jnp.sum(pn * dp, axis=1, keepdims=True)
            dsb = (pn * (dp - delta)).astype(BF16)
            pnb = pn.astype(BF16)
            sk = psink * delta
            dkk = jnp.zeros((2 * BLOCK, 128), F32)
            dvv = jnp.zeros((2 * BLOCK, 128), F32)
            dqs = []
            for hq, (own, qm, ks, vs, aligned) in enumerate(heads):
                rows = slice(BLOCK * hq, BLOCK * (hq + 1))
                ds_ref[hq:hq + 1, :] += jnp.broadcast_to(-jnp.sum(sk[rows]), (1, 128))
                dqs.append(jnp.dot(dsb[rows], ks, preferred_element_type=F32) * (HEAD_DIM ** -0.5))
                dk = lax.dot_general(dsb[rows], qm, _TN, preferred_element_type=F32)
                dv = lax.dot_general(pnb[rows], doms[hq], _TN, preferred_element_type=F32)
                dkk = dkk + (dk if aligned else pltpu.roll(dk, 64, axis=1))
                dvv = dvv + (dv if aligned else pltpu.roll(dv, 64, axis=1))
            for j in range(4):
                out_ref[:, 128 * j:128 * (j + 1)] = jnp.where(heads[2 * j][0], dqs[2 * j], dqs[2 * j + 1]).astype(BF16)
            ck_ref[0:BLOCK, :] = ck_ref[BLOCK:, :] + dkk[0:BLOCK]
            cv_ref[0:BLOCK, :] = cv_ref[BLOCK:, :] + dvv[0:BLOCK]
            ck_ref[BLOCK:, :] = dkk[BLOCK:]
            cv_ref[BLOCK:, :] = dvv[BLOCK:]

        @pl.when(i == nb)
        def _():
            ck_ref[0:BLOCK, :] = ck_ref[BLOCK:, :]
            cv_ref[0:BLOCK, :] = cv_ref[BLOCK:, :]

        dk_ref[...] = ck_ref[0:BLOCK, :].astype(BF16)
        dv_ref[...] = cv_ref[0:BLOCK, :].astype(BF16)

    last = nb - 1
    prev = lambda c: pl.BlockSpec((BLOCK, 128), lambda i: (jnp.clip(i - 1, 0, last), c))
    cur = lambda c: pl.BlockSpec((BLOCK, 128), lambda i: (jnp.minimum(i, last), c))
    qrow = lambda w, c: pl.BlockSpec((BLOCK, w), lambda i: (jnp.minimum(i, last), c))

    dq, dk, dv, ds = pl.pallas_call(
        body, grid=(nb + 1,),
        in_specs=[qrow(D_ATTN, 2), prev(12), cur(12), prev(13), cur(13), qrow(D_ATTN, 0),
                  pl.BlockSpec(memory_space=pltpu.SMEM)],
        out_specs=[qrow(D_ATTN, 0), prev(0), prev(0), _whole((N_Q_HEADS, 128))],
        out_shape=[jax.ShapeDtypeStruct((S, D_ATTN), BF16), jax.ShapeDtypeStruct((S, D_KV), BF16),
                   jax.ShapeDtypeStruct((S, D_KV), BF16), jax.ShapeDtypeStruct((N_Q_HEADS, 128), F32)],
        scratch_shapes=[pltpu.VMEM((2 * BLOCK, 128), F32), pltpu.VMEM((2 * BLOCK, 128), F32)],
        compiler_params=_params(("arbitrary",)), name=name)(proj_a, proj_a, proj_a, proj_a, proj_a, dout, sinks)
    return dq, dk, dv, ds


_BR = ((0, D_CONV), (D_CONV, D_CONV + D_SSM), (D_CONV + D_SSM, D_CONV + D_SSM + D_ATTN))


def _branches(m_refs, wbr_ref):
    nt = (((1,), (1,)), ((), ()))
    return [lax.dot_general(m[...], wbr_ref[:, lo:hi], nt, preferred_element_type=F32)
            for m, (lo, hi) in zip(m_refs, _BR)]


def _merge_fwd(mc, ms, ma, proj_g, wbr_t, name):
    S, Dm = mc.shape[0], proj_g.shape[1] // 3
    tm = min(256, S)

    def body(mc_ref, ms_ref, ma_ref, g_ref, w_ref, out_ref):
        ys = _branches((mc_ref, ms_ref, ma_ref), w_ref)
        acc = None
        for b in range(3):
            t = _sigmoid(g_ref[:, b * Dm:(b + 1) * Dm]) * ys[b]
            acc = t if acc is None else acc + t
        out_ref[...] = acc.astype(BF16)

    wa, ws = _op(wbr_t, (Dm, Dm), lambda i: (0, 0))
    return pl.pallas_call(
        body, grid=(S // tm,),
        in_specs=[_rows(D_CONV, 0, tm), _rows(D_SSM, 0, tm), _rows(D_ATTN, 0, tm), _rows(3 * Dm, 0, tm), ws],
        out_specs=_rows(Dm, 0, tm), out_shape=jax.ShapeDtypeStruct((S, Dm), BF16),
        compiler_params=_params(("parallel",)), name=name)(mc, ms, ma, proj_g, wa)


def _merge_bwd(dmerged, mc, ms, ma, proj_g, wbr_t, name):
    S, Dm = mc.shape[0], proj_g.shape[1] // 3
    tm = min(256, S)

    def body(d_ref, mc_ref, ms_ref, ma_ref, g_ref, w_ref, dg_ref, dy_ref, dmc_ref, dms_ref, dma_ref):
        ys = _branches((mc_ref, ms_ref, ma_ref), w_ref)
        d = d_ref[...]
        for b, (o_ref, (lo, hi)) in enumerate(zip((dmc_ref, dms_ref, dma_ref), _BR)):
            g = _sigmoid(g_ref[:, b * Dm:(b + 1) * Dm])
            dg_ref[:, b * Dm:(b + 1) * Dm] = (d * ys[b] * g * (1.0 - g)).astype(BF16)
            dyb = (g * d).astype(BF16)
            dy_ref[:, b * Dm:(b + 1) * Dm] = dyb
            o_ref[...] = jnp.dot(dyb, w_ref[:, lo:hi], preferred_element_type=F32)

    wa, ws = _op(wbr_t, (Dm, Dm), lambda i: (0, 0))
    return pl.pallas_call(
        body, grid=(S // tm,),
        in_specs=[_rows(Dm, 0, tm), _rows(D_CONV, 0, tm), _rows(D_SSM, 0, tm), _rows(D_ATTN, 0, tm),
                  _rows(3 * Dm, 0, tm), ws],
        out_specs=[_rows(3 * Dm, 0, tm), _rows(3 * Dm, 0, tm), _rows(D_CONV, 0, tm), _rows(D_SSM, 0, tm),
                   _rows(D_ATTN, 0, tm)],
        out_shape=[jax.ShapeDtypeStruct((S, 3 * Dm), BF16), jax.ShapeDtypeStruct((S, 3 * Dm), BF16),
                   jax.ShapeDtypeStruct((S, D_CONV), F32), jax.ShapeDtypeStruct((S, D_SSM), F32),
                   jax.ShapeDtypeStruct((S, D_ATTN), F32)],
        compiler_params=_params(("parallel",)), name=name)(dmerged, mc, ms, ma, proj_g, wa)


def _loss_head(y, target):
    S, Dm = y.shape
    tm = min(512, S)

    def body(y_ref, t_ref, dy_ref, l_ref):
        @pl.when(pl.program_id(0) == 0)
        def _():
            l_ref[...] = jnp.zeros_like(l_ref)

        e = y_ref[...] - t_ref[...]
        dy_ref[...] = e * (1.0 / Dm)
        l_ref[...] += jnp.broadcast_to(0.5 * jnp.sum(jnp.sum(e * e, axis=1, keepdims=True) * (1.0 / Dm)), (1, 128))

    return pl.pallas_call(
        body, grid=(S // tm,), in_specs=[_rows(Dm, 0, tm), _rows(Dm, 0, tm)],
        out_specs=[_rows(Dm, 0, tm), _whole((1, 128))],
        out_shape=[jax.ShapeDtypeStruct((S, Dm), F32), jax.ShapeDtypeStruct((1, 128), F32)],
        compiler_params=_params(("arbitrary",)), name="loss_head")(y, target)


def _view2d(shape):
    n = math.prod(shape)
    if shape[-1] % 128 == 0:
        return (n // shape[-1], shape[-1])
    if n >= (1 << 16) and len(shape) == 3:
        return (shape[0] * shape[1], shape[2])
    if n % 128 == 0:
        return (n // 128, 128)
    return (1, n)


def _adamw(w, g, m, v, name):
    shape = w.shape
    R, C = _view2d(shape)
    tm = R
    for cand in (512, 352, 256):
        if R > cand and R % cand == 0:
            tm = cand
            break
    c1 = 1.0 - ADAM_B1 ** ADAM_STEP
    c2 = 1.0 - ADAM_B2 ** ADAM_STEP

    def body(w_ref, g_ref, m_ref, v_ref, d_ref, nm_ref, nv_ref):
        g = g_ref[...]
        nm = ADAM_B1 * m_ref[...] + (1.0 - ADAM_B1) * g
        nv = ADAM_B2 * v_ref[...] + (1.0 - ADAM_B2) * (g * g)
        d_ref[...] = -ADAM_LR * ((nm / c1) / (jnp.sqrt(nv / c2) + ADAM_EPS) + ADAM_WD * w_ref[...])
        nm_ref[...] = nm
        nv_ref[...] = nv

    blk = _rows(C, 0, tm)
    outs = pl.pallas_call(
        body, grid=(R // tm,), in_specs=[blk] * 4, out_specs=[blk] * 3,
        out_shape=[jax.ShapeDtypeStruct((R, C), F32)] * 3,
        compiler_params=_params(("parallel",)), name=name)(*[t.reshape(R, C) for t in (w, g, m, v)])
    return [o.reshape(shape) for o in outs]


def _coords():
    return lax.axis_index("x"), lax.axis_index("y"), lax.axis_index("c")


def _other_chips(x, y):
    return [((1 - x, y), 2 * (1 - x) + y), ((x, 1 - y), 2 * x + 1 - y), ((1 - x, 1 - y), 2 * (1 - x) + 1 - y)]


def _comm_call(comm, name):
    return _call(None, grid=(), in_specs=[], out_specs=[], out_shape=[], name=name, ops=[], comm=comm)[1]


def _remote(src, dst, send_sems, recv_sems, k, dev):
    return pltpu.make_async_remote_copy(src_ref=src, dst_ref=dst, send_sem=send_sems.at[k], recv_sem=recv_sems.at[k],
                                        device_id=dev, device_id_type=MESH)


def _half(ref, chip, c, rp):
    return ref.at[pl.ds(pl.multiple_of(chip * rp + c * (rp // 2), 16), rp // 2), :]


def _gather_ici(shards, fulls, l):
    n = len(shards)

    def build(refs, fresh, send_sems, recv_sems):
        x, y, c = _coords()
        me = 2 * x + y
        sends, recvs = [], []
        for a in range(n):
            rp = shards[a].shape[1]
            src = refs[a].at[l].at[pl.ds(pl.multiple_of(c * (rp // 2), 16), rp // 2), :]
            for r, ((px, py), chip) in enumerate(_other_chips(x, y)):
                k = 3 * a + r
                sends.append(_remote(src, _half(refs[n + a], me, c, rp), send_sems, recv_sems, k, (px, py, c)))
                recvs.append(_remote(src, _half(refs[n + a], chip, c, rp), send_sems, recv_sems, k, (px, py, c)))
        return sends, recvs

    return build, list(shards) + list(fulls), [n + a for a in range(n)], 3 * n


def _gather_d2d(fulls, shards, l):
    n = len(fulls)

    def build(refs, fresh, send_sems, recv_sems):
        x, y, c = _coords()
        me = 2 * x + y
        sends, recvs = [], []
        for a in range(n):
            rp = fulls[a].shape[0] // N_CHIPS
            for r, (_, chip) in enumerate(_other_chips(x, y)):
                k = 4 * a + r
                mine, theirs = _half(refs[a], chip, c, rp), _half(refs[a], chip, 1 - c, rp)
                sends.append(_remote(mine, mine, send_sems, recv_sems, k, (x, y, 1 - c)))
                recvs.append(_remote(theirs, theirs, send_sems, recv_sems, k, (x, y, 1 - c)))
            own = refs[a].at[pl.ds(pl.multiple_of(me * rp, 16), rp), :]
            sends.append(_remote(refs[n + a].at[l], own, send_sems, recv_sems, 4 * a + 3, (x, y, 1 - c)))
            recvs.append(_remote(refs[n + a].at[l], own, send_sems, recv_sems, 4 * a + 3, (x, y, 1 - c)))
        return sends, recvs

    return build, list(fulls) + list(shards), list(range(n)), 4 * n


def _pair_sum(a, b, c_idx, half_rows, out_dtype, name):
    n4, rp, W = a.shape
    tr = half_rows // 2
    nblk = half_rows // tr

    def body(c_ref, a_ref, b_ref, o_ref):
        o_ref[...] = (a_ref[...] + b_ref[...]).astype(o_ref.dtype)

    return pl.pallas_call(
        body,
        grid_spec=pltpu.PrefetchScalarGridSpec(
            num_scalar_prefetch=1, grid=(nblk,),
            in_specs=[pl.BlockSpec((n4, tr, W), lambda i, c: (0, c[0] * nblk + i, 0)),
                      pl.BlockSpec((n4, tr, W), lambda i, c: (0, i, 0))],
            out_specs=pl.BlockSpec((n4, tr, W), lambda i, c: (0, i, 0))),
        out_shape=jax.ShapeDtypeStruct((n4, half_rows, W), out_dtype),
        compiler_params=_params(("parallel",)), name=name)(c_idx, a, b)


def _sum4(land, own, me_c, stacked, l, name):
    _, R, W = land.shape
    tr = R // 2

    def body(s_ref, l_ref, o_ref, stacked_ref, out_ref):
        me = s_ref[0]
        acc = None
        for i in range(N_CHIPS):
            t = jnp.where(me == i, o_ref[...], l_ref[i]).astype(F32)
            acc = t if acc is None else acc + t
        out_ref[...] = acc

    return pl.pallas_call(
        body,
        grid_spec=pltpu.PrefetchScalarGridSpec(
            num_scalar_prefetch=1, grid=(R // tr,),
            in_specs=[pl.BlockSpec((N_CHIPS, tr, W), lambda i, s: (0, i, 0)),
                      pl.BlockSpec((None, tr, W), lambda i, s: (s[0], i, 0)),
                      pl.BlockSpec(memory_space=pltpu.HBM)],
            out_specs=pl.BlockSpec((None, None, tr, W), lambda i, s: (l, s[1], i, 0))),
        out_shape=jax.ShapeDtypeStruct(stacked.shape, F32), input_output_aliases={3: 0},
        compiler_params=_params(("parallel",)), name=name)(me_c, land, own, stacked)


def _reduce_d2d(g4):
    n = len(g4)

    def build(refs, fresh, send_sems, recv_sems):
        x, y, c = _coords()
        sends, recvs = [], []
        for a in range(n):
            hr = g4[a].shape[1] // 2
            src = refs[a].at[:, pl.ds(pl.multiple_of((1 - c) * hr, 8), hr), :]
            sends.append(_remote(src, fresh[a], send_sems, recv_sems, a, (x, y, 1 - c)))
            recvs.append(_remote(src, fresh[a], send_sems, recv_sems, a, (x, y, 1 - c)))
        return sends, recvs

    outs = [jax.ShapeDtypeStruct((N_CHIPS, g.shape[1] // 2, g.shape[2]), F32) for g in g4]
    return build, list(g4), outs, n


def _reduce_ici(q, lands):
    n = len(q)

    def build(refs, fresh, send_sems, recv_sems):
        x, y, c = _coords()
        me = 2 * x + y
        sends, recvs = [], []
        for a in range(n):
            for r, ((px, py), chip) in enumerate(_other_chips(x, y)):
                k = 3 * a + r
                sends.append(_remote(refs[a].at[chip], refs[n + a].at[me], send_sems, recv_sems, k, (px, py, c)))
                recvs.append(_remote(refs[a].at[chip], refs[n + a].at[chip], send_sems, recv_sems, k, (px, py, c)))
        return sends, recvs

    return build, list(q) + list(lands), [n + a for a in range(n)], 3 * n


def _reduce_share(r, l):
    n = len(r)

    def build(refs, fresh, send_sems, recv_sems):
        x, y, c = _coords()
        sends, recvs = [], []
        for a in range(n):
            mine, theirs = refs[a].at[l].at[c], refs[a].at[l].at[1 - c]
            sends.append(_remote(mine, mine, send_sems, recv_sems, a, (x, y, 1 - c)))
            recvs.append(_remote(mine, theirs, send_sems, recv_sems, a, (x, y, 1 - c)))
        return sends, recvs

    return build, list(r), list(range(n)), n


class _SemOffset:
    def __init__(self, sems, off):
        self.sems, self.off = sems, off

    @property
    def at(self):
        return self

    def __getitem__(self, k):
        return self.sems.at[k + self.off]


def _merge(c1, c2):
    b1, i1, o1, s1 = c1
    b2, i2, o2, s2 = c2
    n1, f1 = len(i1), sum(not isinstance(o, int) for o in o1)

    def build(refs, fresh, send_sems, recv_sems):
        sa, ra = b1(refs[:n1], fresh[:f1], send_sems, recv_sems)
        sb, rb = b2(refs[n1:], fresh[f1:], _SemOffset(send_sems, s1), _SemOffset(recv_sems, s1))
        return sa + sb, ra + rb

    return build, list(i1) + list(i2), list(o1) + [o + n1 if isinstance(o, int) else o for o in o2], s1 + s2


class _Weave:
    A, B, C = ("wg1", "wu1", "wd1"), ("win", "wbr", "wout", "wglu"), ("wg2", "wu2", "wd2")
    X, Y = C + B, A
    ICI_X = {"dwg1": ("wg2", "wu2"), "dwu1": ("wd2", "wbr", "wout", "wglu"), "dwd1": ("win",)}
    ICI_Y = {"dwg2": ("wg1", "wu1"), "dwu2": ("wd1",)}

    def __init__(self, shards, W, L):
        self.shards, self.W, self.L = shards, W, L
        x, y, c = _coords()
        self.c_idx = jnp.reshape(c, (1,)).astype(jnp.int32)
        self.me_c = jnp.stack([2 * x + y, c]).astype(jnp.int32)
        self.final = {k: lax.empty((L, 2, s.shape[1] // 2, s.shape[2]), F32) for k, s in shards.items()}
        self.x = self.y = None
        self.parts = []

    def _ici(self, keys, l):
        def cb(outs):
            for k, t in zip(keys, outs):
                self.W[k][l] = t
        return _gather_ici([self.shards[k] for k in keys], [self.W[k][l] for k in keys], l), cb

    def _d2d(self, keys, l):
        def cb(outs):
            for k, t in zip(keys, outs):
                self.W[k][l] = t
        return _gather_d2d([self.W[k][l] for k in keys], [self.shards[k] for k in keys], l), cb

    def _r1(self, st, keys):
        def cb(outs):
            st["q"] = {k: _pair_sum(st["g4"][k], t, self.c_idx, t.shape[1], BF16, "reduce_pair_sum")
                       for k, t in zip(keys, outs)}
        return _reduce_d2d([st["g4"][k] for k in keys]), cb

    def _r2(self, st, keys):
        def cb(outs):
            for k, t in zip(keys, outs):
                self.final[k] = _sum4(t, st["q"][k], self.me_c, self.final[k], st["layer"], "reduce_sum4")
        return _reduce_ici([st["q"][k] for k in keys], [jnp.zeros(st["q"][k].shape, BF16) for k in keys]), cb

    def _r3(self, st, keys):
        def cb(outs):
            for k, t in zip(keys, outs):
                self.final[k] = t
        return _reduce_share([self.final[k] for k in keys], st["layer"]), cb

    def _pieces(self, site, l):
        nxt = l is not None and l + 1 < self.L
        if site == "up1":
            return [self._ici(self.B, l)]
        if site == "down1":
            return [self._d2d(self.B, l), self._ici(self.C, l)]
        if site == "proj_g":
            return [self._d2d(self.C, l)] + ([self._ici(self.A, l + 1)] if nxt else [])
        if site == "up2":
            return [self._d2d(self.A, l + 1)] if nxt else []
        if site == "dx1" and self.x is not None:
            return [self._r1(self.x, self.X)]
        if site in self.ICI_X and self.x is not None:
            return [self._r2(self.x, self.ICI_X[site])]
        if site == "dx2":
            ps = [self._r3(self.x, self.X)] if self.x is not None else []
            return ps + ([self._r1(self.y, self.Y)] if self.y is not None else [])
        if site in self.ICI_Y and self.y is not None:
            return [self._r2(self.y, self.ICI_Y[site])]
        if site == "dwd2" and self.y is not None:
            return [self._r3(self.y, self.Y)]
        return []

    def take(self, site, l):
        self.parts = self._pieces(site, l)
        if not self.parts:
            return None
        comm = self.parts[0][0]
        for c, _ in self.parts[1:]:
            comm = _merge(comm, c)
        return comm

    def done(self, site, l, outs):
        i = 0
        for comm, cb in self.parts:
            cb(outs[i:i + len(comm[2])])
            i += len(comm[2])
        if site == "dx2":
            self.x = None
        if site == "dwd2":
            self.y = None

    def gather_first(self):
        for piece, name in ((self._ici, "gather0_ici"), (self._d2d, "gather0_d2d")):
            comm, cb = piece(self.A, 0)
            cb(_comm_call(comm, name))

    def _state(self, l, grads):
        return dict(layer=l, g4={k: g.reshape(N_CHIPS, g.shape[0] // N_CHIPS, g.shape[1]) for k, g in grads.items()})

    def grads_x(self, l, grads):
        assert self.x is None
        self.x = self._state(l, grads)

    def grads_y(self, l, grads):
        assert self.y is None
        self.y = self._state(l, grads)

    def flush(self):
        for site, name in (("dx2", "reduce_tail_d2d"), ("dwg2", "reduce_tail_ici"), ("dwu2", "reduce_tail_ici"),
                           ("dwd2", "reduce_tail_share")):
            self.done(site, None, _comm_call(self.take(site, None), name))


def _all_reduce_small(buf, name):
    R = buf.shape[0]

    def build(refs, fresh, send_sems, recv_sems):
        x, y, c = _coords()
        me = 4 * x + 2 * y + c
        sends, recvs = [], []
        k = 0
        for fx in range(2):
            for fy in range(2):
                for fc in range(2):
                    if fx + fy + fc == 0:
                        continue
                    px, py, pc = x ^ fx, y ^ fy, c ^ fc
                    sends.append(_remote(refs[0], refs[1].at[me], send_sems, recv_sems, k, (px, py, pc)))
                    recvs.append(_remote(refs[0], refs[1].at[4 * px + 2 * py + pc], send_sems, recv_sems, k,
                                         (px, py, pc)))
                    k += 1
        return sends, recvs

    got = _comm_call((build, [buf, jnp.zeros((N_DEV, R, 128), F32)], [1], N_DEV - 1), name + "_gather")[0]
    x, y, c = _coords()
    me = jnp.reshape(4 * x + 2 * y + c, (1,)).astype(jnp.int32)

    def body(s_ref, m_ref, own_ref, o_ref):
        acc = None
        for d in range(N_DEV):
            t = jnp.where(s_ref[0] == d, own_ref[...], m_ref[d])
            acc = t if acc is None else acc + t
        o_ref[...] = acc

    tr = max([t for t in range(8, min(R, 1024) + 1, 8) if R % t == 0], default=R)
    return pl.pallas_call(
        body,
        grid_spec=pltpu.PrefetchScalarGridSpec(
            num_scalar_prefetch=1, grid=(R // tr,),
            in_specs=[pl.BlockSpec((N_DEV, tr, 128), lambda i, s: (0, i, 0)),
                      pl.BlockSpec((tr, 128), lambda i, s: (i, 0))],
            out_specs=pl.BlockSpec((tr, 128), lambda i, s: (i, 0))),
        out_shape=jax.ShapeDtypeStruct((R, 128), F32),
        compiler_params=_params(("parallel",)), name=name + "_sum")(me, got, buf)


def _ssm_layouts(p, L):
    G = L * N_GROUPS
    lr = p["ssm_a_re"].reshape(G, 1, SSM_STATE)
    li = p["ssm_a_im"].reshape(G, 1, SSM_STATE)
    ldt = p["ssm_log_dt"].reshape(G, 1, 1)
    br = jnp.swapaxes(p["ssm_b_re"], 2, 3).reshape(G, SSM_GROUP, SSM_STATE)
    bi = jnp.swapaxes(p["ssm_b_im"], 2, 3).reshape(G, SSM_GROUP, SSM_STATE)
    are, aim, bre, bim = _disc_fwd(lr, li, ldt, br, bi)
    eye = jnp.eye(N_GROUPS, dtype=F32)
    abar = jnp.concatenate([are.reshape(L, 8, 128), aim.reshape(L, 8, 128)], axis=1)

    def b_blk(t):
        return jnp.einsum("lgcp,gh->lgchp", t.reshape(L, N_GROUPS, SSM_GROUP, SSM_STATE), eye).reshape(L, 256, 1024)

    def c_blk(t):
        return jnp.einsum("lgcp,gh->lgphc", t, eye).reshape(L, 1024, 256)

    bfull = jnp.concatenate([b_blk(bre), b_blk(bim)], axis=2).astype(BF16)
    cfull = jnp.concatenate([c_blk(p["ssm_c_re"]), -c_blk(p["ssm_c_im"])], axis=1).astype(BF16)
    return (lr, li, ldt, br, bi), abar, bfull, cfull


def _local_step(x, target, W, p, L, weave=None):
    S, Dm = x.shape
    dff = W["wg1"][0].shape[0]
    alpha = (2.0 * L) ** 0.25
    disc_in, abar, bfull, cfull = _ssm_layouts(p, L)
    row = lambda t, l: t[l][None]

    def carry(fn, site, l, *args, **kw):
        comm = weave.take(site, l) if weave is not None else None
        if comm is None:
            return fn(*args, **kw)
        res, couts = fn(*args, comm=comm, **kw)
        weave.done(site, l, couts)
        return res

    saved = []
    h, hb = x, x.astype(BF16)
    for l in range(L):
        sv = {"x0": hb}
        a1, b1, h1 = carry(_ffn_up, "up1", l, hb, W["wg1"][l], W["wu1"][l], dff, "ffn_up")
        x1, x1b, xh1, rs1 = carry(_mm_ln, "down1", l, h1, W["wd1"][l], h, row(p["ln1_g"], l), row(p["ln1_b"], l), 0.5,
                                  alpha, "ffn_down_ln", 1408)
        sv.update(a1=a1, b1=b1, h1=h1, x1=x1b, xh1=xh1, rs1=rs1)
        proj_a = _mm([(x1b, W["win"][l])], "nt", S, D_A, Dm, [F32], name="proj_a", tm=1024, tn=896, tk=1024)
        proj_g = carry(_mm, "proj_g", l, [(x1b, W["win"][l])], "nt", S, 3 * Dm, Dm, [F32], name="proj_g", tm=2048,
                       tn=256, tk=1024, b_off=D_A)
        mc = _conv_fwd(proj_a, p["conv_w"][l], row(p["conv_b"], l), "conv_fwd")
        bu = _mm([(proj_a, (bfull, l))], "nn", S, 2048, D_SSM, [F32], name="ssm_bu", tm=1024, tn=1024, tk=256,
                 a_off=3 * D_CONV)
        xs3 = _scan_fwd(bu.reshape(S, 16, 128), abar[l], "scan_fwd")
        xs = xs3.reshape(S, 2048)
        y1, ms = _ssm_out(xs, proj_a, cfull[l], row(p["ssm_d"], l), W["wglu"][l], "ssm_out")
        sinks = p["attn_sinks"][l][None]
        ma = _attn_fwd(proj_a, sinks, "attn_fwd")
        merged = _merge_fwd(mc, ms, ma, proj_g, W["wbr"][l], "merge_fwd")
        x2, x2b, xh2, rs2 = _mm_ln(merged, W["wout"][l], x1, row(p["ln2_g"], l), row(p["ln2_b"], l), 1.0, alpha,
                                   "mix_out_ln", 1024)
        sv.update(proj_a=proj_a, proj_g=proj_g, mc=mc, ms=ms, ma=ma, xs=xs, xs3=xs3, y1=y1, merged=merged, x2=x2b,
                  xh2=xh2, rs2=rs2)
        a2, b2, h2 = carry(_ffn_up, "up2", l, x2b, W["wg2"][l], W["wu2"][l], dff, "ffn_up")
        h, hb, xh3, rs3 = _mm_ln(h2, W["wd2"][l], x2, row(p["ln3_g"], l), row(p["ln3_b"], l), 0.5, alpha,
                                 "ffn_down_ln", 1408)
        sv.update(a2=a2, b2=b2, h2=h2, xh3=xh3, rs3=rs3)
        saved.append(sv)

    dy, loss = _loss_head(h, target)
    big = [None] * L
    small = {k: [None] * L for k in ("ln1_g", "ln1_b", "ln2_g", "ln2_b", "ln3_g", "ln3_b", "conv_w", "conv_b", "ssm_d",
                                     "attn_sinks", "dabar", "dbfull", "dcfull")}

    def ffn_bwd(dy_out, x_in, a, b, hh, xh, rs, g, wg, wu, wd, l, sites):
        dres, df, dg, db = _ln_bwd(dy_out, xh, rs, g, 0.5, alpha, "ln_bwd")
        da, dbb = _ffn_dh(df, wd[l], a, b, "ffn_dh")
        dx = carry(_mm, sites[0], l, [(da, wg[l]), (dbb, wu[l])], "nn", S, Dm, dff, [F32], name="ffn_dx", tm=1024,
                   tn=1024, tk=dff // 2, add=dres)
        tn_kw = dict(tm=1408, tn=1024, tk=2048)
        dwg = carry(_mm, sites[1], l, [(da, x_in)], "tn", dff, Dm, S, [F32], name="ffn_dw_up", **tn_kw)
        dwu = carry(_mm, sites[2], l, [(dbb, x_in)], "tn", dff, Dm, S, [F32], name="ffn_dw_up", **tn_kw)
        dwd = carry(_mm, sites[3], l, [(hh, df)], "tn", dff, Dm, S, [F32], name="ffn_dw_down", **tn_kw)
        return dx, dwg, dwu, dwd, dg, db

    for l in reversed(range(L)):
        sv = saved[l]
        dx2, dwg2, dwu2, dwd2, small["ln3_g"][l], small["ln3_b"][l] = ffn_bwd(
            dy, sv["x2"], sv["a2"], sv["b2"], sv["h2"], sv["xh3"], sv["rs3"], row(p["ln3_g"], l), W["wg2"], W["wu2"],
            W["wd2"], l, ("dx2", "dwg2", "dwu2", "dwd2"))
        dres2, dmix, small["ln2_g"][l], small["ln2_b"][l] = _ln_bwd(dx2, sv["xh2"], sv["rs2"], row(p["ln2_g"], l), 1.0,
                                                                   alpha, "ln_bwd")
        dwout = carry(_mm, "dw_out", l, [(sv["merged"], dmix)], "tn", Dm, Dm, S, [F32], name="dw_out", tm=1024, tn=1024,
                      tk=2048)
        dmerged = _mm([(dmix, W["wout"][l])], "nt", S, Dm, Dm, [F32], name="d_merged", tm=1024, tn=1024, tk=512)
        dgates, dyb, dmc, dms, dma = _merge_bwd(dmerged, sv["mc"], sv["ms"], sv["ma"], sv["proj_g"], W["wbr"][l],
                                                "merge_bwd")
        dwbr = jnp.concatenate([
            _mm([(dyb, m)], "tn", Dm, hi - lo, S, [F32], name="dw_br", tm=1024, tn=512, tk=2048, a_off=b * Dm)
            for b, ((lo, hi), m) in enumerate(zip(_BR, (sv["mc"], sv["ms"], sv["ma"])))], axis=1)
        proj_a = sv["proj_a"]
        d_conv, small["conv_w"][l], small["conv_b"][l] = _conv_bwd(proj_a, dmc, p["conv_w"][l], row(p["conv_b"], l),
                                                                  "conv_bwd")
        dy1, y2, dgl, dxs, du_skip, small["ssm_d"][l] = _ssm_out_bwd(dms, sv["y1"], proj_a, cfull[l],
                                                                    row(p["ssm_d"], l), W["wglu"][l], "ssm_out_bwd")
        dwglu = _mm([(y2, dgl)], "tn", D_SSM, D_SSM, S, [F32], name="dw_glu", tk=2048)
        small["dcfull"][l] = _mm([(sv["xs"], dy1)], "tn", 2048, D_SSM, S, [F32], name="d_cfull", tm=1024, tk=1024)
        lam, small["dabar"][l] = _scan_bwd(dxs.reshape(S, 16, 128), sv["xs3"], abar[l], "scan_bwd")
        lam = lam.reshape(S, 2048)
        du = _mm([(lam, (bfull, l))], "nt", S, D_SSM, 2048, [BF16], name="ssm_du", tm=1024, tk=512, add=du_skip)
        small["dbfull"][l] = _mm([(proj_a, lam)], "tn", D_SSM, 2048, S, [F32], name="d_bfull", tm=256, tn=1024,
                                 tk=1024, a_off=3 * D_CONV)
        dq, dk, dv, dsk = _attn_bwd(proj_a, dma, p["attn_sinks"][l][None], "attn_bwd")
        small["attn_sinks"][l] = dsk[:, 0]
        dproj = jnp.concatenate([d_conv, du, dq, dk, dv, dgates], axis=1)
        dx1 = carry(_mm, "d_x1", l, [(dproj, W["win"][l])], "nn", S, Dm, D_A + 3 * Dm, [F32], name="d_x1", tm=1024,
                    tn=1024, tk=(D_A + 3 * Dm) // 2, add=dres2)
        dwin = carry(_mm, "dw_in", l, [(dproj, sv["x1"])], "tn", D_A + 3 * Dm, Dm, S, [F32], name="dw_in", tm=2432,
                     tn=1024, tk=1024)
        if weave is not None:
            weave.grads_x(l, dict(wg2=dwg2, wu2=dwu2, wd2=dwd2, win=dwin, wbr=dwbr, wout=dwout, wglu=dwglu))
        dx0, dwg1, dwu1, dwd1, small["ln1_g"][l], small["ln1_b"][l] = ffn_bwd(
            dx1, sv["x0"], sv["a1"], sv["b1"], sv["h1"], sv["xh1"], sv["rs1"], row(p["ln1_g"], l), W["wg1"], W["wu1"],
            W["wd1"], l, ("dx1", "dwg1", "dwu1", "dwd1"))
        big[l] = dict(wg1=dwg1, wu1=dwu1, wd1=dwd1, win=dwin, wbr=dwbr, wout=dwout, wg2=dwg2, wu2=dwu2, wd2=dwd2,
                      wglu=dwglu)
        if weave is not None:
            weave.grads_y(l, dict(wg1=dwg1, wu1=dwu1, wd1=dwd1))
        dy = dx0

    small = {k: jnp.stack(v) for k, v in small.items()}
    eye = jnp.eye(N_GROUPS, dtype=F32)
    dabar = small.pop("dabar")
    dbf = small.pop("dbfull").reshape(L, N_GROUPS, SSM_GROUP, 2, N_GROUPS, SSM_STATE)
    dbbar = jnp.einsum("lgcrhp,gh->rlgcp", dbf, eye).reshape(2, L * N_GROUPS, SSM_GROUP, SSM_STATE)
    dcf = small.pop("dcfull").reshape(L, 2, N_GROUPS, SSM_STATE, N_GROUPS, SSM_GROUP)
    dc = jnp.einsum("lrgphc,gh->rlgcp", dcf, eye)
    G = L * N_GROUPS
    cts = (dabar[:, 0:8].reshape(G, 1, SSM_STATE), dabar[:, 8:16].reshape(G, 1, SSM_STATE), dbbar[0], dbbar[1])
    dlr, dli, dldt, dbr, dbi = _disc_bwd(*disc_in, cts)
    shp_b = (L, N_GROUPS, SSM_GROUP, SSM_STATE)
    small.update(
        ssm_a_re=dlr.reshape(L, N_GROUPS, SSM_STATE), ssm_a_im=dli.reshape(L, N_GROUPS, SSM_STATE),
        ssm_log_dt=dldt.reshape(L, N_GROUPS), ssm_b_re=jnp.swapaxes(dbr.reshape(shp_b), 2, 3),
        ssm_b_im=jnp.swapaxes(dbi.reshape(shp_b), 2, 3), ssm_c_re=dc[0], ssm_c_im=-dc[1],
        ln1_g=small["ln1_g"][:, 0], ln1_b=small["ln1_b"][:, 0], ln2_g=small["ln2_g"][:, 0],
        ln2_b=small["ln2_b"][:, 0], ln3_g=small["ln3_g"][:, 0], ln3_b=small["ln3_b"][:, 0],
        conv_b=small["conv_b"][:, 0], ssm_d=small["ssm_d"][:, 0])
    return loss, dy, big, small


_SMALL_ORDER = ("ln1_g", "ln1_b", "ln2_g", "ln2_b", "ln3_g", "ln3_b", "conv_w", "conv_b", "ssm_a_re", "ssm_a_im",
                "ssm_log_dt", "ssm_b_re", "ssm_b_im", "ssm_c_re", "ssm_c_im", "ssm_d", "attn_sinks")
_BIG_ORDER = ("wg1", "wu1", "wd1", "win", "wbr", "wout", "wg2", "wu2", "wd2", "wglu")
_WEIGHTS = ("ffn1_w_gate", "ffn1_w_up", "ffn1_w_down", "ln1_g", "ln1_b", "w_in", "conv_w", "conv_b", "ssm_a_re",
            "ssm_a_im", "ssm_log_dt", "ssm_b_re", "ssm_b_im", "ssm_c_re", "ssm_c_im", "ssm_d", "ssm_w_glu",
            "attn_sinks", "w_br_conv", "w_br_ssm", "w_br_attn", "w_out", "ln2_g", "ln2_b", "ffn2_w_gate",
            "ffn2_w_up", "ffn2_w_down", "ln3_g", "ln3_b")


def _weight_shards(w):
    t = lambda a: jnp.swapaxes(a, 1, 2).astype(BF16)
    wbr = jnp.concatenate([t(w["w_br_conv"]), t(w["w_br_ssm"]), t(w["w_br_attn"])], axis=2)
    return dict(wg1=t(w["ffn1_w_gate"]), wu1=t(w["ffn1_w_up"]), wd1=w["ffn1_w_down"].astype(BF16), win=t(w["w_in"]),
                wbr=wbr, wout=w["w_out"].astype(BF16), wg2=t(w["ffn2_w_gate"]), wu2=t(w["ffn2_w_up"]),
                wd2=w["ffn2_w_down"].astype(BF16), wglu=w["ssm_w_glu"].astype(BF16))


def _gather_conv_w(conv_w, chip):
    L = conv_w.shape[0]
    n = L * 3 * 64
    slots = lax.dynamic_update_slice(jnp.zeros((N_CHIPS, n), F32), conv_w.reshape(1, n), (chip, 0))
    got = (_all_reduce_small(slots.reshape(-1, 128), "conv_w_gather") * 0.5).reshape(N_CHIPS, L, 3, 64)
    return jnp.transpose(got, (1, 2, 0, 3)).reshape(L, 3, N_CHIPS * 64)


def kernel(x, ffn1_w_gate, ffn1_w_up, ffn1_w_down, ln1_g, ln1_b, w_in, conv_w, conv_b, ssm_a_re, ssm_a_im, ssm_log_dt, ssm_b_re, ssm_b_im, ssm_c_re, ssm_c_im, ssm_d, ssm_w_glu, attn_sinks, w_br_conv, w_br_ssm, w_br_attn, w_out, ln2_g, ln2_b, ffn2_w_gate, ffn2_w_up, ffn2_w_down, ln3_g, ln3_b, loss_target, m_ffn1_w_gate, m_ffn1_w_up, m_ffn1_w_down, m_ln1_g, m_ln1_b, m_w_in, m_conv_w, m_conv_b, m_ssm_a_re, m_ssm_a_im, m_ssm_log_dt, m_ssm_b_re, m_ssm_b_im, m_ssm_c_re, m_ssm_c_im, m_ssm_d, m_ssm_w_glu, m_attn_sinks, m_w_br_conv, m_w_br_ssm, m_w_br_attn, m_w_out, m_ln2_g, m_ln2_b, m_ffn2_w_gate, m_ffn2_w_up, m_ffn2_w_down, m_ln3_g, m_ln3_b, v_ffn1_w_gate, v_ffn1_w_up, v_ffn1_w_down, v_ln1_g, v_ln1_b, v_w_in, v_conv_w, v_conv_b, v_ssm_a_re, v_ssm_a_im, v_ssm_log_dt, v_ssm_b_re, v_ssm_b_im, v_ssm_c_re, v_ssm_c_im, v_ssm_d, v_ssm_w_glu, v_attn_sinks, v_w_br_conv, v_w_br_ssm, v_w_br_attn, v_w_out, v_ln2_g, v_ln2_b, v_ffn2_w_gate, v_ffn2_w_up, v_ffn2_w_down, v_ln3_g, v_ln3_b):
    args = dict(locals())
    w = {k: args[k] for k in _WEIGHTS}
    L = ln1_g.shape[0]
    cx, cy, cc = _coords()
    chip = 2 * cx + cy

    shards = _weight_shards(w)
    full = {k: [lax.empty((N_CHIPS * s.shape[1], s.shape[2]), BF16) for _ in range(L)] for k, s in shards.items()}
    weave = _Weave(shards, full, L)
    weave.gather_first()
    p = {k: w[k] for k in ("ln1_g", "ln1_b", "ln2_g", "ln2_b", "ln3_g", "ln3_b", "conv_b", "ssm_a_re", "ssm_a_im",
                           "ssm_log_dt", "ssm_b_re", "ssm_b_im", "ssm_c_re", "ssm_c_im", "ssm_d", "attn_sinks")}
    p["conv_w"] = _gather_conv_w(conv_w, chip)

    loss, grad_x, _, small = _local_step(x[0], loss_target[0], full, p, L, weave)
    weave.flush()
    loss = lax.psum(loss[0, 0], ("x", "y", "c"))

    sizes = [math.prod(small[k].shape) for k in _SMALL_ORDER]
    pad = (-sum(sizes)) % 1024
    flat = jnp.concatenate([small[k].reshape(-1) for k in _SMALL_ORDER] + [jnp.zeros((pad,), F32)])
    flat = _all_reduce_small(flat.reshape(-1, 128), "small_grads").reshape(-1)
    sm, off = {}, 0
    for k, n in zip(_SMALL_ORDER, sizes):
        sm[k] = flat[off:off + n].reshape(small[k].shape)
        off += n
    red = {k: t.reshape(L, 2 * t.shape[2], t.shape[3]) for k, t in weave.final.items()}
    tr = lambda a: jnp.swapaxes(a, 1, 2)
    grads = dict(sm)
    grads.update(
        ffn1_w_gate=tr(red["wg1"]), ffn1_w_up=tr(red["wu1"]), ffn1_w_down=red["wd1"], w_in=tr(red["win"]),
        w_br_conv=tr(red["wbr"][:, :, _BR[0][0]:_BR[0][1]]), w_br_ssm=tr(red["wbr"][:, :, _BR[1][0]:_BR[1][1]]),
        w_br_attn=tr(red["wbr"][:, :, _BR[2][0]:_BR[2][1]]), w_out=red["wout"], ffn2_w_gate=tr(red["wg2"]),
        ffn2_w_up=tr(red["wu2"]), ffn2_w_down=red["wd2"], ssm_w_glu=red["wglu"],
        conv_w=lax.dynamic_slice_in_dim(sm["conv_w"], chip * 64, 64, axis=2))

    outs = [[], [], [], []]
    for k in _WEIGHTS:
        d, nm, nv = _adamw(w[k], grads[k], args["m_" + k], args["v_" + k], "adamw")
        for lst, val in zip(outs, (grads[k], d, nm, nv)):
            lst.append(val)
    return (loss, grad_x[None], *outs[0], *outs[1], *outs[2], *outs[3])
```

```python
import functools
import math

import jax
import jax.numpy as jnp
from jax import lax
from jax.experimental import pallas as pl
from jax.experimental.pallas import tpu as pltpu

F32 = jnp.float32
BF16 = jnp.bfloat16

LN_EPS = 1e-5
D_CONV = 256
D_SSM = 256
N_GROUPS = 16
SSM_GROUP = 16
SSM_STATE = 64
N_Q_HEADS = 8
HEAD_DIM = 64
D_ATTN = 512
D_KV = 128
BLOCK = 128
D_A = 3 * D_CONV + D_SSM + D_ATTN + 2 * D_KV
ADAM_LR = 0.001
ADAM_B1 = 0.9
ADAM_B2 = 0.999
ADAM_EPS = 1e-08
ADAM_WD = 0.01
ADAM_STEP = 10

VMEM_LIMIT_BYTES = 56 * 1024 * 1024
MESH = pl.DeviceIdType.MESH
N_CHIPS = 4
N_DEV = 8


def _params(sem=None):
    return pltpu.CompilerParams(dimension_semantics=sem, vmem_limit_bytes=VMEM_LIMIT_BYTES)


def _op(op, block, imap):
    if isinstance(op, tuple):
        arr, l = op
        return arr, pl.BlockSpec((None,) + block, lambda *g: (l,) + imap(*g))
    return op, pl.BlockSpec(block, imap)


def _call(body, *, grid, in_specs, out_specs, out_shape, scratch=(), sem=None, name, ops, comm=None):
    if comm is None:
        return pl.pallas_call(body, grid=grid, in_specs=in_specs, out_specs=out_specs, out_shape=out_shape,
                              scratch_shapes=list(scratch), compiler_params=_params(sem), name=name)(*ops), []
    build, cins, couts, n_sem = comm
    n_in, n_out, n_scr, nci, nco, ng = len(in_specs), len(out_specs), len(scratch), len(cins), len(couts), len(grid)
    hbm = pl.BlockSpec(memory_space=pltpu.HBM)
    aliases = {n_in + o: n_out + j for j, o in enumerate(couts) if isinstance(o, int)}
    cshapes = [jax.ShapeDtypeStruct(cins[o].shape, cins[o].dtype) if isinstance(o, int) else o for o in couts]

    def hosted(*refs):
        refs = list(refs)
        main = refs[:n_in] + refs[n_in + nci:n_in + nci + n_out] + refs[n_in + nci + n_out + nco:-2]
        crefs, co = refs[n_in:n_in + nci], refs[n_in + nci + n_out:n_in + nci + n_out + nco]
        send_sems, recv_sems = refs[-2:]
        fresh = []
        for j, o in enumerate(couts):
            if isinstance(o, int):
                crefs[o] = co[j]
            else:
                fresh.append(co[j])

        def start():
            for cp in build(crefs, fresh, send_sems, recv_sems)[0]:
                cp.start()

        def finish():
            sends, recvs = build(crefs, fresh, send_sems, recv_sems)
            for cp in recvs:
                cp.wait_recv()
            for cp in sends:
                cp.wait_send()

        if ng == 0:
            start()
            finish()
            return
        ids = [pl.program_id(d) for d in range(ng)]
        first = functools.reduce(jnp.logical_and, [ids[d] == 0 for d in range(ng)])
        last = functools.reduce(jnp.logical_and, [ids[d] == grid[d] - 1 for d in range(ng)])
        pl.when(first)(start)
        body(*main)
        pl.when(last)(finish)

    res = pl.pallas_call(
        hosted, grid=grid, in_specs=list(in_specs) + [hbm] * nci, out_specs=list(out_specs) + [hbm] * nco,
        out_shape=list(out_shape) + cshapes,
        scratch_shapes=list(scratch) + [pltpu.SemaphoreType.DMA((n_sem,)), pltpu.SemaphoreType.DMA((n_sem,))],
        input_output_aliases=aliases,
        compiler_params=pltpu.CompilerParams(dimension_semantics=("arbitrary",) * ng if ng else None,
                                             vmem_limit_bytes=VMEM_LIMIT_BYTES, has_side_effects=True),
        name=name)(*ops, *cins)
    return res[:n_out], res[n_out:]


def _mm(pairs, mode, m, n, k, out_dtypes, *, name, tm=512, tn=512, tk=512, add=None, a_off=0, b_off=0, comm=None):
    tm, tn, tk = min(tm, m), min(tn, n), min(tk, k)
    assert m % tm == 0 and n % tn == 0 and k % tk == 0 and b_off % tn == 0, (name, m, n, k, tm, tn, tk)
    nk, npair, jo = k // tk, len(pairs), b_off // tn
    if mode == "nn":
        ao = a_off // tk
        ab, ai, bb, bi = (tm, tk), (lambda i, j, kk: (i, kk + ao)), (tk, tn), (lambda i, j, kk: (kk, j))
        dims = (((1,), (0,)), ((), ()))
    elif mode == "nt":
        ao = a_off // tk
        ab, ai, bb, bi = (tm, tk), (lambda i, j, kk: (i, kk + ao)), (tn, tk), (lambda i, j, kk: (j + jo, kk))
        dims = (((1,), (1,)), ((), ()))
    else:
        ao = a_off // tm
        ab, ai, bb, bi = (tk, tm), (lambda i, j, kk: (kk, i + ao)), (tk, tn), (lambda i, j, kk: (kk, j))
        dims = (((0,), (0,)), ((), ()))
    assert a_off % (tm if mode == "tn" else tk) == 0, (name, a_off)
    ops, specs = [], []
    for a, b in pairs:
        for o, blk, im in ((a, ab, ai), (b, bb, bi)):
            arr, sp = _op(o, blk, im)
            ops.append(arr)
            specs.append(sp)
    has_add = add is not None
    if has_add:
        ops.append(add)
        specs.append(pl.BlockSpec((tm, tn), lambda i, j, kk: (i, j)))
    nout = len(out_dtypes)

    def body(*refs):
        outs, acc = refs[2 * npair + has_add:2 * npair + has_add + nout], refs[-1]
        kk = pl.program_id(2)
        t = None
        for p in range(npair):
            d = lax.dot_general(refs[2 * p][...].astype(BF16), refs[2 * p + 1][...].astype(BF16), dims,
                                preferred_element_type=F32)
            t = d if t is None else t + d

        def finish(r):
            if has_add:
                r = r + refs[2 * npair][...]
            for o in outs:
                o[...] = r.astype(o.dtype)

        if nk == 1:
            finish(t)
            return

        @pl.when(kk == 0)
        def _():
            acc[...] = jnp.zeros_like(acc)

        acc[...] += t

        @pl.when(kk == nk - 1)
        def _():
            finish(acc[...])

    res, couts = _call(
        body, grid=(m // tm, n // tn, nk), in_specs=specs,
        out_specs=[pl.BlockSpec((tm, tn), lambda i, j, kk: (i, j))] * nout,
        out_shape=[jax.ShapeDtypeStruct((m, n), dt) for dt in out_dtypes],
        scratch=[pltpu.VMEM((tm, tn), F32)], sem=("parallel", "parallel", "arbitrary"), name=name, ops=ops, comm=comm)
    res = res[0] if nout == 1 else res
    return res if comm is None else (res, couts)


def _rows(width, col=0, tm=None):
    return pl.BlockSpec((tm, width), lambda i: (i, col))


def _whole(shape):
    nd = len(shape)
    return pl.BlockSpec(shape, lambda i: (0,) * nd)


def _sigmoid(x):
    return 0.5 * jnp.tanh(0.5 * x) + 0.5


def _mm_ln(a, w, x, g, b, s, alpha, name, tk, comm=None):
    S, K = a.shape
    Dm = x.shape[1]
    tm, tk = min(1024, S), min(tk, K)
    nk = K // tk

    def body(a_ref, w_ref, x_ref, g_ref, b_ref, y_ref, yb_ref, xh_ref, rs_ref, acc):
        kk = pl.program_id(1)
        d = jnp.dot(a_ref[...], w_ref[...], preferred_element_type=F32)

        if nk > 1:
            @pl.when(kk == 0)
            def _():
                acc[...] = d

            @pl.when((kk > 0) & (kk < nk - 1))
            def _():
                acc[...] += d

        @pl.when(kk == nk - 1)
        def _():
            z = alpha * x_ref[...] + s * (d if nk == 1 else acc[...] + d)
            mu = jnp.mean(z, axis=-1, keepdims=True)
            zc = z - mu
            var = jnp.mean(zc * zc, axis=-1, keepdims=True)
            rstd = lax.rsqrt(var + LN_EPS)
            xh = zc * rstd
            y = xh * g_ref[...] + b_ref[...]
            y_ref[...] = y
            yb_ref[...] = y.astype(BF16)
            xh_ref[...] = xh
            rs_ref[...] = rstd

    wa, ws = _op(w, (tk, Dm), lambda i, kk: (kk, 0))
    row = pl.BlockSpec((tm, Dm), lambda i, kk: (i, 0))
    vec = pl.BlockSpec((1, Dm), lambda i, kk: (0, 0))
    res, couts = _call(
        body, grid=(S // tm, nk),
        in_specs=[pl.BlockSpec((tm, tk), lambda i, kk: (i, kk)), ws, row, vec, vec],
        out_specs=[row, row, row, pl.BlockSpec((tm, 1), lambda i, kk: (i, 0))],
        out_shape=[jax.ShapeDtypeStruct((S, Dm), F32), jax.ShapeDtypeStruct((S, Dm), BF16),
                   jax.ShapeDtypeStruct((S, Dm), F32), jax.ShapeDtypeStruct((S, 1), F32)],
        scratch=[pltpu.VMEM((tm, Dm), F32)], sem=("parallel", "arbitrary"), name=name, ops=[a, wa, x, g, b],
        comm=comm)
    return res if comm is None else (res, couts)


def _ln_bwd(dy, xh, rs, g, s, alpha, name):
    S, Dm = dy.shape
    tm = min(512, S)

    def body(dy_ref, xh_ref, rs_ref, g_ref, dres_ref, dbr_ref, dg_ref, db_ref):
        @pl.when(pl.program_id(0) == 0)
        def _():
            dg_ref[...] = jnp.zeros_like(dg_ref)
            db_ref[...] = jnp.zeros_like(db_ref)

        dy, xh = dy_ref[...], xh_ref[...]
        dyg = dy * g_ref[...]
        m1 = jnp.mean(dyg, axis=-1, keepdims=True)
        m2 = jnp.mean(dyg * xh, axis=-1, keepdims=True)
        dz = rs_ref[...] * (dyg - m1 - xh * m2)
        dres_ref[...] = alpha * dz
        dbr_ref[...] = (s * dz).astype(BF16)
        dg_ref[...] += jnp.sum(dy * xh, axis=0, keepdims=True)
        db_ref[...] += jnp.sum(dy, axis=0, keepdims=True)

    return pl.pallas_call(
        body, grid=(S // tm,),
        in_specs=[_rows(Dm, tm=tm), _rows(Dm, tm=tm), _rows(1, tm=tm), _whole((1, Dm))],
        out_specs=[_rows(Dm, tm=tm), _rows(Dm, tm=tm), _whole((1, Dm)), _whole((1, Dm))],
        out_shape=[jax.ShapeDtypeStruct((S, Dm), F32), jax.ShapeDtypeStruct((S, Dm), BF16),
                   jax.ShapeDtypeStruct((1, Dm), F32), jax.ShapeDtypeStruct((1, Dm), F32)],
        compiler_params=_params(("arbitrary",)), name=name)(dy, xh, rs, g)


def _ffn_up(x, wg, wu, dff, name, comm=None):
    S, Dm = x.shape
    tm, tn = min(512, S), dff // 2

    def body(x_ref, wg_ref, wu_ref, a_ref, b_ref, h_ref):
        xb = x_ref[...]
        dims = (((1,), (1,)), ((), ()))
        a = lax.dot_general(xb, wg_ref[...], dims, preferred_element_type=F32)
        b = lax.dot_general(xb, wu_ref[...], dims, preferred_element_type=F32)
        a_ref[...] = a.astype(BF16)
        b_ref[...] = b.astype(BF16)
        h_ref[...] = (a * _sigmoid(a) * b).astype(BF16)

    wga, wgs = _op(wg, (tn, Dm), lambda i, j: (j, 0))
    wua, wus = _op(wu, (tn, Dm), lambda i, j: (j, 0))
    ob = pl.BlockSpec((tm, tn), lambda i, j: (i, j))
    res, couts = _call(
        body, grid=(S // tm, dff // tn),
        in_specs=[pl.BlockSpec((tm, Dm), lambda i, j: (i, 0)), wgs, wus], out_specs=[ob, ob, ob],
        out_shape=[jax.ShapeDtypeStruct((S, dff), BF16)] * 3, sem=("parallel", "parallel"), name=name,
        ops=[x, wga, wua], comm=comm)
    return res if comm is None else (res, couts)


def _ffn_dh(df, wd, a, b, name):
    S, Dm = df.shape
    dff = a.shape[1]
    tm, tn = min(512, S), dff // 2

    def body(df_ref, wd_ref, a_ref, b_ref, da_ref, db_ref):
        dh = lax.dot_general(df_ref[...], wd_ref[...], (((1,), (1,)), ((), ())), preferred_element_type=F32)
        a, b = a_ref[...].astype(F32), b_ref[...].astype(F32)
        sg = _sigmoid(a)
        da_ref[...] = (dh * b * (sg * (1.0 + a * (1.0 - sg)))).astype(BF16)
        db_ref[...] = (dh * (a * sg)).astype(BF16)

    wda, wds = _op(wd, (tn, Dm), lambda i, j: (j, 0))
    ob = pl.BlockSpec((tm, tn), lambda i, j: (i, j))
    return pl.pallas_call(
        body, grid=(S // tm, dff // tn),
        in_specs=[pl.BlockSpec((tm, Dm), lambda i, j: (i, 0)), wds, ob, ob], out_specs=[ob, ob],
        out_shape=[jax.ShapeDtypeStruct((S, dff), BF16), jax.ShapeDtypeStruct((S, dff), BF16)],
        compiler_params=_params(("parallel", "parallel")), name=name)(df, wda, a, b)


def _halo_prev(width, col, tm):
    return pl.BlockSpec((8, width), lambda i: (jnp.maximum(i * (tm // 8) - 1, 0), col))


def _halo_next(width, col, tm, S):
    return pl.BlockSpec((8, width), lambda i: (jnp.minimum((i + 1) * (tm // 8), S // 8 - 1), col))


def _shift_down(prev8, cur, n):
    ext = jnp.concatenate([prev8, cur], axis=0)
    return pltpu.roll(ext, n, axis=0)[8:]


def _shift_up(cur, next8, n):
    ext = jnp.concatenate([cur, next8], axis=0)
    return pltpu.roll(ext, ext.shape[0] - n, axis=0)[:cur.shape[0]]


def _conv_fwd(proj_a, conv_w, conv_b, name):
    S = proj_a.shape[0]
    tm = min(512, S)
    C = D_CONV

    def body(bg_ref, cg_ref, h_ref, cgp_ref, hp_ref, w_ref, cb_ref, out_ref):
        z = cg_ref[...] * h_ref[...]
        zp = jnp.where(pl.program_id(0) > 0, cgp_ref[...] * hp_ref[...], 0.0)
        w = w_ref[...]
        y = w[2:3] * z + w[1:2] * _shift_down(zp, z, 1) + w[0:1] * _shift_down(zp, z, 2) + cb_ref[...]
        out_ref[...] = (bg_ref[...] * y).astype(BF16)

    return pl.pallas_call(
        body, grid=(S // tm,),
        in_specs=[_rows(C, 0, tm), _rows(C, 1, tm), _rows(C, 2, tm), _halo_prev(C, 1, tm), _halo_prev(C, 2, tm),
                  _whole((3, C)), _whole((1, C))],
        out_specs=_rows(C, 0, tm), out_shape=jax.ShapeDtypeStruct((S, C), BF16),
        compiler_params=_params(("parallel",)), name=name)(proj_a, proj_a, proj_a, proj_a, proj_a, conv_w, conv_b)


def _conv_bwd(proj_a, dmc, conv_w, conv_b, name):
    S = proj_a.shape[0]
    tm = min(512, S)
    C = D_CONV
    nblk = S // tm

    def body(bg_ref, cg_ref, h_ref, cgp_ref, hp_ref, bgn_ref, d_ref, dn_ref, w_ref, cb_ref, out_ref, dw_ref,
             dcb_ref):
        i = pl.program_id(0)

        @pl.when(i == 0)
        def _():
            dw_ref[...] = jnp.zeros_like(dw_ref)
            dcb_ref[...] = jnp.zeros_like(dcb_ref)

        bg, cg, h, d = bg_ref[...], cg_ref[...], h_ref[...], d_ref[...]
        z = cg * h
        zp = jnp.where(i > 0, cgp_ref[...] * hp_ref[...], 0.0)
        w = w_ref[...]
        z1, z2 = _shift_down(zp, z, 1), _shift_down(zp, z, 2)
        y = w[2:3] * z + w[1:2] * z1 + w[0:1] * z2 + cb_ref[...]
        dy = d * bg
        dyn = jnp.where(i < nblk - 1, dn_ref[...] * bgn_ref[...], 0.0)
        dz = w[2:3] * dy + w[1:2] * _shift_up(dy, dyn, 1) + w[0:1] * _shift_up(dy, dyn, 2)
        out_ref[:, 0:C] = (d * y).astype(BF16)
        out_ref[:, C:2 * C] = (dz * h).astype(BF16)
        out_ref[:, 2 * C:3 * C] = (dz * cg).astype(BF16)
        dw_ref[0:1, :] += jnp.sum(dy * z2, axis=0, keepdims=True)
        dw_ref[1:2, :] += jnp.sum(dy * z1, axis=0, keepdims=True)
        dw_ref[2:3, :] += jnp.sum(dy * z, axis=0, keepdims=True)
        dcb_ref[...] += jnp.sum(dy, axis=0, keepdims=True)

    return pl.pallas_call(
        body, grid=(nblk,),
        in_specs=[_rows(C, 0, tm), _rows(C, 1, tm), _rows(C, 2, tm), _halo_prev(C, 1, tm), _halo_prev(C, 2, tm),
                  _halo_next(C, 0, tm, S), _rows(C, 0, tm), _halo_next(C, 0, tm, S), _whole((3, C)), _whole((1, C))],
        out_specs=[_rows(3 * C, 0, tm), _whole((3, C)), _whole((1, C))],
        out_shape=[jax.ShapeDtypeStruct((S, 3 * C), BF16), jax.ShapeDtypeStruct((3, C), F32),
                   jax.ShapeDtypeStruct((1, C), F32)],
        compiler_params=_params(("arbitrary",)), name=name)(
            proj_a, proj_a, proj_a, proj_a, proj_a, proj_a, dmc, dmc, conv_w, conv_b)


def _disc_math(lr, li, ldt, br, bi):
    dt = jnp.exp(ldt)
    mag = jnp.exp(lr * dt)
    ang = li * dt
    are = mag * jnp.cos(ang)
    aim = mag * jnp.sin(ang)
    nr = are - 1.0
    den = lr * lr + li * li
    cre = (nr * lr + aim * li) / den
    cim = (aim * lr - nr * li) / den
    return are, aim, cre * br - cim * bi, cre * bi + cim * br


def _disc_fwd(lr, li, ldt, br, bi):
    shapes = [lr.shape, lr.shape, br.shape, br.shape]

    def body(lr_ref, li_ref, ldt_ref, br_ref, bi_ref, *outs):
        for o, v in zip(outs, _disc_math(lr_ref[...], li_ref[...], ldt_ref[...], br_ref[...], bi_ref[...])):
            o[...] = v

    return pl.pallas_call(body, out_shape=[jax.ShapeDtypeStruct(s, F32) for s in shapes],
                          compiler_params=_params(), name="ssm_disc")(lr, li, ldt, br, bi)


def _disc_bwd(lr, li, ldt, br, bi, cts):
    shapes = [lr.shape, lr.shape, ldt.shape, br.shape, br.shape]

    def body(lr_ref, li_ref, ldt_ref, br_ref, bi_ref, c0, c1, c2, c3, *outs):
        _, vjp = jax.vjp(_disc_math, lr_ref[...], li_ref[...], ldt_ref[...], br_ref[...], bi_ref[...])
        for o, v in zip(outs, vjp((c0[...], c1[...], c2[...], c3[...]))):
            o[...] = v

    return pl.pallas_call(body, out_shape=[jax.ShapeDtypeStruct(s, F32) for s in shapes],
                          compiler_params=_params(), name="ssm_disc_bwd")(lr, li, ldt, br, bi, *cts)


def _scan_fwd(bu, abar, name):
    S = bu.shape[0]
    tb = min(256, S)

    def body(bu_ref, a_ref, xs_ref, st_ref):
        @pl.when(pl.program_id(0) == 0)
        def _():
            st_ref[...] = jnp.zeros_like(st_ref)

        ar, ai = a_ref[0:8, :], a_ref[8:16, :]

        def step(t, c):
            xr, xi = c
            nr = ar * xr - ai * xi + bu_ref[t, 0:8, :]
            ni = ar * xi + ai * xr + bu_ref[t, 8:16, :]
            xs_ref[t, 0:8, :] = nr
            xs_ref[t, 8:16, :] = ni
            return nr, ni

        xr, xi = lax.fori_loop(0, tb, step, (st_ref[0:8, :], st_ref[8:16, :]), unroll=8)
        st_ref[0:8, :] = xr
        st_ref[8:16, :] = xi

    blk = pl.BlockSpec((tb, 16, 128), lambda i: (i, 0, 0))
    return pl.pallas_call(
        body, grid=(S // tb,), in_specs=[blk, _whole((16, 128))], out_specs=blk,
        out_shape=jax.ShapeDtypeStruct((S, 16, 128), F32), scratch_shapes=[pltpu.VMEM((16, 128), F32)],
        compiler_params=_params(("arbitrary",)), name=name)(bu, abar)


def _scan_bwd(dxs, xs, abar, name):
    S = dxs.shape[0]
    tb = min(256, S)
    nblk = S // tb

    def body(d_ref, x_ref, xp_ref, a_ref, lam_ref, da_ref, st_ref):
        i = pl.program_id(0)

        @pl.when(i == 0)
        def _():
            st_ref[...] = jnp.zeros_like(st_ref)
            da_ref[...] = jnp.zeros_like(da_ref)

        ar, ai = a_ref[0:8, :], a_ref[8:16, :]

        def one(t, c, pr, pi):
            lr, li, gr, gi = c
            nr = d_ref[t, 0:8, :] + ar * lr + ai * li
            ni = d_ref[t, 8:16, :] - ai * lr + ar * li
            lam_ref[t, 0:8, :] = nr
            lam_ref[t, 8:16, :] = ni
            return nr, ni, gr + nr * pr + ni * pi, gi - nr * pi + ni * pr

        def step(s, c):
            t = tb - 1 - s
            return one(t, c, x_ref[t - 1, 0:8, :], x_ref[t - 1, 8:16, :])

        c = (st_ref[0:8, :], st_ref[8:16, :], jnp.zeros((8, 128), F32), jnp.zeros((8, 128), F32))
        c = lax.fori_loop(0, tb - 1, step, c, unroll=8)
        first = i == nblk - 1
        pr = jnp.where(first, 0.0, xp_ref[0, 0:8, :])
        pi = jnp.where(first, 0.0, xp_ref[0, 8:16, :])
        lr, li, gr, gi = one(0, c, pr, pi)
        st_ref[0:8, :] = lr
        st_ref[8:16, :] = li
        da_ref[0:8, :] += gr
        da_ref[8:16, :] += gi

    blk = pl.BlockSpec((tb, 16, 128), lambda i: (nblk - 1 - i, 0, 0))
    prev = pl.BlockSpec((1, 16, 128), lambda i: (jnp.maximum((nblk - 1 - i) * tb - 1, 0), 0, 0))
    return pl.pallas_call(
        body, grid=(nblk,), in_specs=[blk, blk, prev, _whole((16, 128))], out_specs=[blk, _whole((16, 128))],
        out_shape=[jax.ShapeDtypeStruct((S, 16, 128), F32), jax.ShapeDtypeStruct((16, 128), F32)],
        scratch_shapes=[pltpu.VMEM((16, 128), F32)],
        compiler_params=_params(("arbitrary",)), name=name)(dxs, xs, xs, abar)


def _gelu(x):
    return 0.5 * x * (1.0 + jnp.tanh(0.7978845608028654 * (x + 0.044715 * x * x * x)))


def _gelu_grad(x):
    t = jnp.tanh(0.7978845608028654 * (x + 0.044715 * x * x * x))
    return 0.5 * (1.0 + t) + 0.5 * x * (1.0 - t * t) * 0.7978845608028654 * (1.0 + 3.0 * 0.044715 * x * x)


def _ssm_out(xs, proj_a, cfull, dskip, wglu, name):
    S = xs.shape[0]
    tm = min(512, S)
    C = D_SSM

    def body(xs_ref, u_ref, c_ref, d_ref, wg_ref, y1_ref, ms_ref):
        y1 = jnp.dot(xs_ref[...].astype(BF16), c_ref[...], preferred_element_type=F32) + d_ref[...] * u_ref[...]
        y2 = _gelu(y1)
        gl = jnp.dot(y2.astype(BF16), wg_ref[...], preferred_element_type=F32)
        y1_ref[...] = y1
        ms_ref[...] = (y2 * _sigmoid(gl)).astype(BF16)

    wga, wgs = _op(wglu, (C, C), lambda i: (0, 0))
    return pl.pallas_call(
        body, grid=(S // tm,),
        in_specs=[_rows(2 * 1024, 0, tm), _rows(C, 3, tm), _whole((2 * 1024, C)), _whole((1, C)), wgs],
        out_specs=[_rows(C, 0, tm), _rows(C, 0, tm)],
        out_shape=[jax.ShapeDtypeStruct((S, C), F32), jax.ShapeDtypeStruct((S, C), BF16)],
        compiler_params=_params(("parallel",)), name=name)(xs, proj_a, cfull, dskip, wga)


def _ssm_out_bwd(dms, y1, proj_a, cfull, dskip, wglu, name):
    S = y1.shape[0]
    tm = min(512, S)
    C = D_SSM

    def body(dms_ref, y1_ref, u_ref, c_ref, d_ref, wg_ref, dy1_ref, y2_ref, dgl_ref, dxs_ref, du_ref, dd_ref):
        @pl.when(pl.program_id(0) == 0)
        def _():
            dd_ref[...] = jnp.zeros_like(dd_ref)

        dms, y1 = dms_ref[...], y1_ref[...]
        y2 = _gelu(y1)
        y2b = y2.astype(BF16)
        sg = _sigmoid(jnp.dot(y2b, wg_ref[...], preferred_element_type=F32))
        dgl = (dms * y2 * sg * (1.0 - sg)).astype(BF16)
        dy2 = dms * sg + lax.dot_general(dgl, wg_ref[...], (((1,), (1,)), ((), ())), preferred_element_type=F32)
        dy1 = dy2 * _gelu_grad(y1)
        dy1b = dy1.astype(BF16)
        dy1_ref[...] = dy1b
        y2_ref[...] = y2b
        dgl_ref[...] = dgl
        dxs_ref[...] = lax.dot_general(dy1b, c_ref[...], (((1,), (1,)), ((), ())), preferred_element_type=F32)
        du_ref[...] = d_ref[...] * dy1
        dd_ref[...] += jnp.sum(dy1 * u_ref[...], axis=0, keepdims=True)

    wga, wgs = _op(wglu, (C, C), lambda i: (0, 0))
    rc = _rows(C, 0, tm)
    return pl.pallas_call(
        body, grid=(S // tm,),
        in_specs=[rc, rc, _rows(C, 3, tm), _whole((2 * 1024, C)), _whole((1, C)), wgs],
        out_specs=[rc, rc, rc, _rows(2 * 1024, 0, tm), rc, _whole((1, C))],
        out_shape=[jax.ShapeDtypeStruct((S, C), BF16), jax.ShapeDtypeStruct((S, C), BF16),
                   jax.ShapeDtypeStruct((S, C), BF16), jax.ShapeDtypeStruct((S, 2 * 1024), F32),
                   jax.ShapeDtypeStruct((S, C), F32), jax.ShapeDtypeStruct((1, C), F32)],
        compiler_params=_params(("arbitrary",)), name=name)(dms, y1, proj_a, cfull, dskip, wga)


_NT = (((1,), (1,)), ((), ()))
_TN = (((0,), (0,)), ((), ()))


def _attn_heads(q_ref, kp_ref, kc_ref, vp_ref, vc_ref):
    lane = lax.broadcasted_iota(jnp.int32, (BLOCK, 128), 1)
    kk = jnp.concatenate([kp_ref[...], kc_ref[...]], axis=0).astype(BF16)
    vv = jnp.concatenate([vp_ref[...], vc_ref[...]], axis=0).astype(BF16)
    kk_r, vv_r = pltpu.roll(kk, 64, axis=1), pltpu.roll(vv, 64, axis=1)
    heads = []
    for hq in range(N_Q_HEADS):
        j, e = hq // 2, hq % 2
        qj = (q_ref[:, 128 * j:128 * (j + 1)] * (HEAD_DIM ** -0.5)).astype(BF16)
        own = (lane >= 64) if e else (lane < 64)
        aligned = e == hq // 4
        heads.append((own, jnp.where(own, qj, jnp.zeros_like(qj)), kk if aligned else kk_r, vv if aligned else vv_r,
                      aligned))
    return heads


def _attn_probs(i, heads, s_ref):
    n = N_Q_HEADS * BLOCK
    s = jnp.concatenate([lax.dot_general(qm, ks, _NT, preferred_element_type=F32) for _, qm, ks, _, _ in heads],
                        axis=0)
    row = lax.broadcasted_iota(jnp.int32, (n, 2 * BLOCK), 0) & (BLOCK - 1)
    col = lax.broadcasted_iota(jnp.int32, (n, 2 * BLOCK), 1)
    mask = (col > row) & (col <= row + BLOCK) & ((col >= BLOCK) | (i > 0))
    s = jnp.where(mask, s, -1e30)
    sink = jnp.concatenate([jnp.full((BLOCK, 1), s_ref[0, hq], F32) for hq in range(N_Q_HEADS)], axis=0)
    m = jnp.maximum(jnp.max(s, axis=1, keepdims=True), sink)
    p = jnp.exp(s - m)
    es = jnp.exp(sink - m)
    inv = 1.0 / (jnp.sum(p, axis=1, keepdims=True) + es)
    return p * inv, es * inv


def _attn_fwd(proj_a, sinks, name):
    S = proj_a.shape[0]
    nb = S // BLOCK

    def body(q_ref, kp_ref, kc_ref, vp_ref, vc_ref, s_ref, out_ref):
        i = pl.program_id(0)
        heads = _attn_heads(q_ref, kp_ref, kc_ref, vp_ref, vc_ref)
        row = lax.broadcasted_iota(jnp.int32, (BLOCK, 2 * BLOCK), 0)
        col = lax.broadcasted_iota(jnp.int32, (BLOCK, 2 * BLOCK), 1)
        mask = (col > row) & (col <= row + BLOCK) & ((col >= BLOCK) | (i > 0))
        outs = []
        for hq, (_, qm, ks, vs, _) in enumerate(heads):
            s = jnp.where(mask, lax.dot_general(qm, ks, _NT, preferred_element_type=F32), -1e30)
            sink = s_ref[0, hq]
            m = jnp.maximum(jnp.max(s, axis=1, keepdims=True), sink)
            p = jnp.exp(s - m)
            inv = 1.0 / (jnp.sum(p, axis=1, keepdims=True) + jnp.exp(sink - m))
            outs.append(jnp.dot((p * inv).astype(BF16), vs, preferred_element_type=F32))
        for j in range(4):
            out_ref[:, 128 * j:128 * (j + 1)] = jnp.where(heads[2 * j][0], outs[2 * j], outs[2 * j + 1]).astype(BF16)

    prev = lambda c: pl.BlockSpec((BLOCK, 128), lambda i: (jnp.maximum(i - 1, 0), c))
    cur = lambda c: pl.BlockSpec((BLOCK, 128), lambda i: (i, c))
    return pl.pallas_call(
        body, grid=(nb,),
        in_specs=[_rows(D_ATTN, 2, BLOCK), prev(12), cur(12), prev(13), cur(13),
                  pl.BlockSpec(memory_space=pltpu.SMEM)],
        out_specs=_rows(D_ATTN, 0, BLOCK), out_shape=jax.ShapeDtypeStruct((S, D_ATTN), BF16),
        compiler_params=_params(("parallel",)), name=name)(proj_a, proj_a, proj_a, proj_a, proj_a, sinks)


def _attn_bwd(proj_a, dout, sinks, name):
    S = proj_a.shape[0]
    nb = S // BLOCK

    def body(q_ref, kp_ref, kc_ref, vp_ref, vc_ref, do_ref, s_ref, out_ref, dk_ref, dv_ref, ds_ref, ck_ref, cv_ref):
        i = pl.program_id(0)

        @pl.when(i == 0)
        def _():
            ds_ref[...] = jnp.zeros_like(ds_ref)
            ck_ref[...] = jnp.zeros_like(ck_ref)
            cv_ref[...] = jnp.zeros_like(cv_ref)

        @pl.when(i < nb)
        def _():
            heads = _attn_heads(q_ref, kp_ref, kc_ref, vp_ref, vc_ref)
            pn, psink = _attn_probs(i, heads, s_ref)
            doms = []
            for hq, (own, _, _, _, _) in enumerate(heads):
                doj = do_ref[:, 128 * (hq // 2):128 * (hq // 2 + 1)].astype(BF16)
                doms.append(jnp.where(own, doj, jnp.zeros_like(doj)))
            dp = jnp.concatenate([lax.dot_general(dom, vs, _NT, preferred_element_type=F32)
                                  for dom, (_, _, _, vs, _) in zip(doms, heads)], axis=0)
            delta = jnp.sum(pn * dp, axis=1, keepdims=True)
            dsb = (pn * (dp - delta)).astype(BF16)
            pnb = pn.astype(BF16)
            sk = psink * delta
            dkk = jnp.zeros((2 * BLOCK, 128), F32)
            dvv = jnp.zeros((2 * BLOCK, 128), F32)
            dqs = []
            for hq, (own, qm, ks, vs, aligned) in enumerate(heads):
                rows = slice(BLOCK * hq, BLOCK * (hq + 1))
                ds_ref[hq:hq + 1, :] += jnp.broadcast_to(-jnp.sum(sk[rows]), (1, 128))
                dqs.append(jnp.dot(dsb[rows], ks, preferred_element_type=F32) * (HEAD_DIM ** -0.5))
                dk = lax.dot_general(dsb[rows], qm, _TN, preferred_element_type=F32)
                dv = lax.dot_general(pnb[rows], doms[hq], _TN, preferred_element_type=F32)
                dkk = dkk + (dk if aligned else pltpu.roll(dk, 64, axis=1))
                dvv = dvv + (dv if aligned else pltpu.roll(dv, 64, axis=1))
            for j in range(4):
                out_ref[:, 128 * j:128 * (j + 1)] = jnp.where(heads[2 * j][0], dqs[2 * j], dqs[2 * j + 1]).astype(BF16)
            ck_ref[0:BLOCK, :] = ck_ref[BLOCK:, :] + dkk[0:BLOCK]
            cv_ref[0:BLOCK, :] = cv_ref[BLOCK:, :] + dvv[0:BLOCK]
            ck_ref[BLOCK:, :] = dkk[BLOCK:]
            cv_ref[BLOCK:, :] = dvv[BLOCK:]

        @pl.when(i == nb)
        def _():
            ck_ref[0:BLOCK, :] = ck_ref[BLOCK:, :]
            cv_ref[0:BLOCK, :] = cv_ref[BLOCK:, :]

        dk_ref[...] = ck_ref[0:BLOCK, :].astype(BF16)
        dv_ref[...] = cv_ref[0:BLOCK, :].astype(BF16)

    last = nb - 1
    prev = lambda c: pl.BlockSpec((BLOCK, 128), lambda i: (jnp.clip(i - 1, 0, last), c))
    cur = lambda c: pl.BlockSpec((BLOCK, 128), lambda i: (jnp.minimum(i, last), c))
    qrow = lambda w, c: pl.BlockSpec((BLOCK, w), lambda i: (jnp.minimum(i, last), c))

    dq, dk, dv, ds = pl.pallas_call(
        body, grid=(nb + 1,),
        in_specs=[qrow(D_ATTN, 2), prev(12), cur(12), prev(13), cur(13), qrow(D_ATTN, 0),
                  pl.BlockSpec(memory_space=pltpu.SMEM)],
        out_specs=[qrow(D_ATTN, 0), prev(0), prev(0), _whole((N_Q_HEADS, 128))],
        out_shape=[jax.ShapeDtypeStruct((S, D_ATTN), BF16), jax.ShapeDtypeStruct((S, D_KV), BF16),
                   jax.ShapeDtypeStruct((S, D_KV), BF16), jax.ShapeDtypeStruct((N_Q_HEADS, 128), F32)],
        scratch_shapes=[pltpu.VMEM((2 * BLOCK, 128), F32), pltpu.VMEM((2 * BLOCK, 128), F32)],
        compiler_params=_params(("arbitrary",)), name=name)(proj_a, proj_a, proj_a, proj_a, proj_a, dout, sinks)
    return dq, dk, dv, ds


_BR = ((0, D_CONV), (D_CONV, D_CONV + D_SSM), (D_CONV + D_SSM, D_CONV + D_SSM + D_ATTN))


def _branches(m_refs, wbr_ref):
    nt = (((1,), (1,)), ((), ()))
    return [lax.dot_general(m[...], wbr_ref[:, lo:hi], nt, preferred_element_type=F32)
            for m, (lo, hi) in zip(m_refs, _BR)]


def _merge_fwd(mc, ms, ma, proj_g, wbr_t, name):
    S, Dm = mc.shape[0], proj_g.shape[1] // 3
    tm = min(512, S)

    def body(mc_ref, ms_ref, ma_ref, g_ref, w_ref, out_ref):
        ys = _branches((mc_ref, ms_ref, ma_ref), w_ref)
        acc = None
        for b in range(3):
            t = _sigmoid(g_ref[:, b * Dm:(b + 1) * Dm].astype(F32)) * ys[b]
            acc = t if acc is None else acc + t
        out_ref[...] = acc.astype(BF16)

    wa, ws = _op(wbr_t, (Dm, Dm), lambda i: (0, 0))
    return pl.pallas_call(
        body, grid=(S // tm,),
        in_specs=[_rows(D_CONV, 0, tm), _rows(D_SSM, 0, tm), _rows(D_ATTN, 0, tm), _rows(3 * Dm, 0, tm), ws],
        out_specs=_rows(Dm, 0, tm), out_shape=jax.ShapeDtypeStruct((S, Dm), BF16),
        compiler_params=_params(("parallel",)), name=name)(mc, ms, ma, proj_g, wa)


def _merge_bwd(dmerged, mc, ms, ma, proj_g, wbr_t, name):
    S, Dm = mc.shape[0], proj_g.shape[1] // 3
    tm = min(256, S)

    def body(d_ref, mc_ref, ms_ref, ma_ref, g_ref, w_ref, dg_ref, dy_ref, dmc_ref, dms_ref, dma_ref):
        ys = _branches((mc_ref, ms_ref, ma_ref), w_ref)
        d = d_ref[...]
        for b, (o_ref, (lo, hi)) in enumerate(zip((dmc_ref, dms_ref, dma_ref), _BR)):
            g = _sigmoid(g_ref[:, b * Dm:(b + 1) * Dm].astype(F32))
            dg_ref[:, b * Dm:(b + 1) * Dm] = (d * ys[b] * g * (1.0 - g)).astype(BF16)
            dyb = (g * d).astype(BF16)
            dy_ref[:, b * Dm:(b + 1) * Dm] = dyb
            o_ref[...] = jnp.dot(dyb, w_ref[:, lo:hi], preferred_element_type=F32)

    wa, ws = _op(wbr_t, (Dm, Dm), lambda i: (0, 0))
    return pl.pallas_call(
        body, grid=(S // tm,),
        in_specs=[_rows(Dm, 0, tm), _rows(D_CONV, 0, tm), _rows(D_SSM, 0, tm), _rows(D_ATTN, 0, tm),
                  _rows(3 * Dm, 0, tm), ws],
        out_specs=[_rows(3 * Dm, 0, tm), _rows(3 * Dm, 0, tm), _rows(D_CONV, 0, tm), _rows(D_SSM, 0, tm),
                   _rows(D_ATTN, 0, tm)],
        out_shape=[jax.ShapeDtypeStruct((S, 3 * Dm), BF16), jax.ShapeDtypeStruct((S, 3 * Dm), BF16),
                   jax.ShapeDtypeStruct((S, D_CONV), F32), jax.ShapeDtypeStruct((S, D_SSM), F32),
                   jax.ShapeDtypeStruct((S, D_ATTN), F32)],
        compiler_params=_params(("parallel",)), name=name)(dmerged, mc, ms, ma, proj_g, wa)


def _loss_head(y, target):
    S, Dm = y.shape
    tm = min(512, S)

    def body(y_ref, t_ref, dy_ref, l_ref):
        @pl.when(pl.program_id(0) == 0)
        def _():
            l_ref[...] = jnp.zeros_like(l_ref)

        e = y_ref[...] - t_ref[...]
        dy_ref[...] = e * (1.0 / Dm)
        l_ref[...] += jnp.broadcast_to(0.5 * jnp.sum(jnp.sum(e * e, axis=1, keepdims=True) * (1.0 / Dm)), (1, 128))

    return pl.pallas_call(
        body, grid=(S // tm,), in_specs=[_rows(Dm, 0, tm), _rows(Dm, 0, tm)],
        out_specs=[_rows(Dm, 0, tm), _whole((1, 128))],
        out_shape=[jax.ShapeDtypeStruct((S, Dm), F32), jax.ShapeDtypeStruct((1, 128), F32)],
        compiler_params=_params(("arbitrary",)), name="loss_head")(y, target)


def _view2d(shape):
    n = math.prod(shape)
    if shape[-1] % 128 == 0:
        return (n // shape[-1], shape[-1])
    if n >= (1 << 16) and len(shape) == 3:
        return (shape[0] * shape[1], shape[2])
    if n % 128 == 0:
        return (n // 128, 128)
    return (1, n)


def _adamw(w, g, m, v, name):
    shape = w.shape
    c1 = 1.0 - ADAM_B1 ** ADAM_STEP
    c2 = 1.0 - ADAM_B2 ** ADAM_STEP

    def body(w_ref, g_ref, m_ref, v_ref, d_ref, nm_ref, nv_ref):
        g = g_ref[...]
        nm = ADAM_B1 * m_ref[...] + (1.0 - ADAM_B1) * g
        nv = ADAM_B2 * v_ref[...] + (1.0 - ADAM_B2) * (g * g)
        d_ref[...] = -ADAM_LR * ((nm / c1) / (jnp.sqrt(nv / c2) + ADAM_EPS) + ADAM_WD * w_ref[...])
        nm_ref[...] = nm
        nv_ref[...] = nv

    if len(shape) == 3 and shape[1] % 8 == 0 and shape[1] * shape[2] >= (1 << 16):
        Ls, R, C = shape
        tm = max(t for t in range(8, min(R, 512) + 1, 8) if R % t == 0)
        blk = pl.BlockSpec((None, tm, C), lambda l, i: (l, i, 0))
        return pl.pallas_call(
            body, grid=(Ls, R // tm), in_specs=[blk] * 4, out_specs=[blk] * 3,
            out_shape=[jax.ShapeDtypeStruct(shape, F32)] * 3,
            compiler_params=_params(("parallel", "parallel")), name=name)(w, g, m, v)
    R, C = _view2d(shape)
    tm = R
    for cand in (512, 352, 256):
        if R > cand and R % cand == 0:
            tm = cand
            break
    blk = _rows(C, 0, tm)
    outs = pl.pallas_call(
        body, grid=(R // tm,), in_specs=[blk] * 4, out_specs=[blk] * 3,
        out_shape=[jax.ShapeDtypeStruct((R, C), F32)] * 3,
        compiler_params=_params(("parallel",)), name=name)(*[t.reshape(R, C) for t in (w, g, m, v)])
    return [o.reshape(shape) for o in outs]


def _coords():
    return lax.axis_index("x"), lax.axis_index("y"), lax.axis_index("c")


def _other_chips(x, y):
    return [((1 - x, y), 2 * (1 - x) + y), ((x, 1 - y), 2 * x + 1 - y), ((1 - x, 1 - y), 2 * (1 - x) + 1 - y)]


def _comm_call(comm, name):
    return _call(None, grid=(), in_specs=[], out_specs=[], out_shape=[], name=name, ops=[], comm=comm)[1]


def _remote(src, dst, send_sems, recv_sems, k, dev):
    return pltpu.make_async_remote_copy(src_ref=src, dst_ref=dst, send_sem=send_sems.at[k], recv_sem=recv_sems.at[k],
                                        device_id=dev, device_id_type=MESH)


def _half(ref, chip, c, rp):
    return ref.at[pl.ds(pl.multiple_of(chip * rp + c * (rp // 2), 16), rp // 2), :]


def _gather_ici(shards, fulls, l):
    n = len(shards)

    def build(refs, fresh, send_sems, recv_sems):
        x, y, c = _coords()
        me = 2 * x + y
        sends, recvs = [], []
        for a in range(n):
            rp = shards[a].shape[1]
            src = refs[a].at[l].at[pl.ds(pl.multiple_of(c * (rp // 2), 16), rp // 2), :]
            for r, ((px, py), chip) in enumerate(_other_chips(x, y)):
                k = 3 * a + r
                sends.append(_remote(src, _half(refs[n + a], me, c, rp), send_sems, recv_sems, k, (px, py, c)))
                recvs.append(_remote(src, _half(refs[n + a], chip, c, rp), send_sems, recv_sems, k, (px, py, c)))
        return sends, recvs

    return build, list(shards) + list(fulls), [n + a for a in range(n)], 3 * n


def _gather_d2d(fulls, shards, l):
    n = len(fulls)

    def build(refs, fresh, send_sems, recv_sems):
        x, y, c = _coords()
        me = 2 * x + y
        sends, recvs = [], []
        for a in range(n):
            rp = fulls[a].shape[0] // N_CHIPS
            for r, (_, chip) in enumerate(_other_chips(x, y)):
                k = 4 * a + r
                mine, theirs = _half(refs[a], chip, c, rp), _half(refs[a], chip, 1 - c, rp)
                sends.append(_remote(mine, mine, send_sems, recv_sems, k, (x, y, 1 - c)))
                recvs.append(_remote(theirs, theirs, send_sems, recv_sems, k, (x, y, 1 - c)))
            own = refs[a].at[pl.ds(pl.multiple_of(me * rp, 16), rp), :]
            sends.append(_remote(refs[n + a].at[l], own, send_sems, recv_sems, 4 * a + 3, (x, y, 1 - c)))
            recvs.append(_remote(refs[n + a].at[l], own, send_sems, recv_sems, 4 * a + 3, (x, y, 1 - c)))
        return sends, recvs

    return build, list(fulls) + list(shards), list(range(n)), 4 * n


def _pair_sum(a, b, c_idx, half_rows, out_dtype, name):
    n4, rp, W = a.shape
    tr = half_rows // 2
    nblk = half_rows // tr

    def body(c_ref, a_ref, b_ref, o_ref):
        o_ref[...] = (a_ref[...] + b_ref[...]).astype(o_ref.dtype)

    return pl.pallas_call(
        body,
        grid_spec=pltpu.PrefetchScalarGridSpec(
            num_scalar_prefetch=1, grid=(nblk,),
            in_specs=[pl.BlockSpec((n4, tr, W), lambda i, c: (0, c[0] * nblk + i, 0)),
                      pl.BlockSpec((n4, tr, W), lambda i, c: (0, i, 0))],
            out_specs=pl.BlockSpec((n4, tr, W), lambda i, c: (0, i, 0))),
        out_shape=jax.ShapeDtypeStruct((n4, half_rows, W), out_dtype),
        compiler_params=_params(("parallel",)), name=name)(c_idx, a, b)


def _sum4(land, own, me_c, stacked, l, name):
    _, R, W = land.shape
    tr = R // 2

    def body(s_ref, l_ref, o_ref, stacked_ref, out_ref):
        me = s_ref[0]
        acc = None
        for i in range(N_CHIPS):
            t = jnp.where(me == i, o_ref[...], l_ref[i]).astype(F32)
            acc = t if acc is None else acc + t
        out_ref[...] = acc

    return pl.pallas_call(
        body,
        grid_spec=pltpu.PrefetchScalarGridSpec(
            num_scalar_prefetch=1, grid=(R // tr,),
            in_specs=[pl.BlockSpec((N_CHIPS, tr, W), lambda i, s: (0, i, 0)),
                      pl.BlockSpec((None, tr, W), lambda i, s: (s[0], i, 0)),
                      pl.BlockSpec(memory_space=pltpu.HBM)],
            out_specs=pl.BlockSpec((None, None, tr, W), lambda i, s: (l, s[1], i, 0))),
        out_shape=jax.ShapeDtypeStruct(stacked.shape, F32), input_output_aliases={3: 0},
        compiler_params=_params(("parallel",)), name=name)(me_c, land, own, stacked)


def _reduce_d2d(g4):
    n = len(g4)

    def build(refs, fresh, send_sems, recv_sems):
        x, y, c = _coords()
        sends, recvs = [], []
        for a in range(n):
            hr = g4[a].shape[1] // 2
            src = refs[a].at[:, pl.ds(pl.multiple_of((1 - c) * hr, 8), hr), :]
            sends.append(_remote(src, fresh[a], send_sems, recv_sems, a, (x, y, 1 - c)))
            recvs.append(_remote(src, fresh[a], send_sems, recv_sems, a, (x, y, 1 - c)))
        return sends, recvs

    outs = [jax.ShapeDtypeStruct((N_CHIPS, g.shape[1] // 2, g.shape[2]), F32) for g in g4]
    return build, list(g4), outs, n


def _reduce_ici(q, lands):
    n = len(q)

    def build(refs, fresh, send_sems, recv_sems):
        x, y, c = _coords()
        me = 2 * x + y
        sends, recvs = [], []
        for a in range(n):
            for r, ((px, py), chip) in enumerate(_other_chips(x, y)):
                k = 3 * a + r
                sends.append(_remote(refs[a].at[chip], refs[n + a].at[me], send_sems, recv_sems, k, (px, py, c)))
                recvs.append(_remote(refs[a].at[chip], refs[n + a].at[chip], send_sems, recv_sems, k, (px, py, c)))
        return sends, recvs

    return build, list(q) + list(lands), [n + a for a in range(n)], 3 * n


def _reduce_share(r, l):
    n = len(r)

    def build(refs, fresh, send_sems, recv_sems):
        x, y, c = _coords()
        sends, recvs = [], []
        for a in range(n):
            mine, theirs = refs[a].at[l].at[c], refs[a].at[l].at[1 - c]
            sends.append(_remote(mine, mine, send_sems, recv_sems, a, (x, y, 1 - c)))
            recvs.append(_remote(mine, theirs, send_sems, recv_sems, a, (x, y, 1 - c)))
        return sends, recvs

    return build, list(r), list(range(n)), n


class _SemOffset:
    def __init__(self, sems, off):
        self.sems, self.off = sems, off

    @property
    def at(self):
        return self

    def __getitem__(self, k):
        return self.sems.at[k + self.off]


def _merge(c1, c2):
    b1, i1, o1, s1 = c1
    b2, i2, o2, s2 = c2
    n1, f1 = len(i1), sum(not isinstance(o, int) for o in o1)

    def build(refs, fresh, send_sems, recv_sems):
        sa, ra = b1(refs[:n1], fresh[:f1], send_sems, recv_sems)
        sb, rb = b2(refs[n1:], fresh[f1:], _SemOffset(send_sems, s1), _SemOffset(recv_sems, s1))
        return sa + sb, ra + rb

    return build, list(i1) + list(i2), list(o1) + [o + n1 if isinstance(o, int) else o for o in o2], s1 + s2


class _Weave:
    A, B, C = ("wg1", "wu1", "wd1"), ("win", "wbr", "wout", "wglu"), ("wg2", "wu2", "wd2")
    X, Y = C + B, A
    ICI_X = {"dwg1": ("wg2", "wu2"), "dwu1": ("wd2", "wbr", "wout", "wglu"), "dwd1": ("win",)}
    ICI_Y = {"dwg2": ("wg1", "wu1"), "dwu2": ("wd1",)}

    def __init__(self, shards, W, L):
        self.shards, self.W, self.L = shards, W, L
        x, y, c = _coords()
        self.c_idx = jnp.reshape(c, (1,)).astype(jnp.int32)
        self.me_c = jnp.stack([2 * x + y, c]).astype(jnp.int32)
        self.final = {k: lax.empty((L, 2, s.shape[1] // 2, s.shape[2]), F32) for k, s in shards.items()}
        self.x = self.y = None
        self.parts = []

    def _ici(self, keys, l):
        def cb(outs):
            for k, t in zip(keys, outs):
                self.W[k][l] = t
        return _gather_ici([self.shards[k] for k in keys], [self.W[k][l] for k in keys], l), cb

    def _d2d(self, keys, l):
        def cb(outs):
            for k, t in zip(keys, outs):
                self.W[k][l] = t
        return _gather_d2d([self.W[k][l] for k in keys], [self.shards[k] for k in keys], l), cb

    def _r1(self, st, keys):
        def cb(outs):
            st["q"] = {k: _pair_sum(st["g4"][k], t, self.c_idx, t.shape[1], BF16, "reduce_pair_sum")
                       for k, t in zip(keys, outs)}
        return _reduce_d2d([st["g4"][k] for k in keys]), cb

    def _r2(self, st, keys):
        def cb(outs):
            for k, t in zip(keys, outs):
                self.final[k] = _sum4(t, st["q"][k], self.me_c, self.final[k], st["layer"], "reduce_sum4")
        return _reduce_ici([st["q"][k] for k in keys], [jnp.zeros(st["q"][k].shape, BF16) for k in keys]), cb

    def _r3(self, st, keys):
        def cb(outs):
            for k, t in zip(keys, outs):
                self.final[k] = t
        return _reduce_share([self.final[k] for k in keys], st["layer"]), cb

    def _pieces(self, site, l):
        nxt = l is not None and l + 1 < self.L
        if site == "up1":
            return [self._ici(self.B, l)]
        if site == "down1":
            return [self._d2d(self.B, l), self._ici(self.C, l)]
        if site == "proj_g":
            return [self._d2d(self.C, l)] + ([self._ici(self.A, l + 1)] if nxt else [])
        if site == "up2":
            return [self._d2d(self.A, l + 1)] if nxt else []
        if site == "dx1" and self.x is not None:
            return [self._r1(self.x, self.X)]
        if site in self.ICI_X and self.x is not None:
            return [self._r2(self.x, self.ICI_X[site])]
        if site == "dx2":
            ps = [self._r3(self.x, self.X)] if self.x is not None else []
            return ps + ([self._r1(self.y, self.Y)] if self.y is not None else [])
        if site in self.ICI_Y and self.y is not None:
            return [self._r2(self.y, self.ICI_Y[site])]
        if site == "dwd2" and self.y is not None:
            return [self._r3(self.y, self.Y)]
        return []

    def take(self, site, l):
        self.parts = self._pieces(site, l)
        if not self.parts:
            return None
        comm = self.parts[0][0]
        for c, _ in self.parts[1:]:
            comm = _merge(comm, c)
        return comm

    def done(self, site, l, outs):
        i = 0
        for comm, cb in self.parts:
            cb(outs[i:i + len(comm[2])])
            i += len(comm[2])
        if site == "dx2":
            self.x = None
        if site == "dwd2":
            self.y = None

    def gather_first(self):
        for piece, name in ((self._ici, "gather0_ici"), (self._d2d, "gather0_d2d")):
            comm, cb = piece(self.A, 0)
            cb(_comm_call(comm, name))

    def _state(self, l, grads):
        return dict(layer=l, g4={k: g.reshape(N_CHIPS, g.shape[0] // N_CHIPS, g.shape[1]) for k, g in grads.items()})

    def grads_x(self, l, grads):
        assert self.x is None
        self.x = self._state(l, grads)

    def grads_y(self, l, grads):
        assert self.y is None
        self.y = self._state(l, grads)

    def flush(self):
        for site, name in (("dx2", "reduce_tail_d2d"), ("dwg2", "reduce_tail_ici"), ("dwu2", "reduce_tail_ici"),
                           ("dwd2", "reduce_tail_share")):
            self.done(site, None, _comm_call(self.take(site, None), name))


def _all_reduce_small(buf, name):
    R = buf.shape[0]

    def build(refs, fresh, send_sems, recv_sems):
        x, y, c = _coords()
        me = 4 * x + 2 * y + c
        sends, recvs = [], []
        k = 0
        for fx in range(2):
            for fy in range(2):
                for fc in range(2):
                    if fx + fy + fc == 0:
                        continue
                    px, py, pc = x ^ fx, y ^ fy, c ^ fc
                    sends.append(_remote(refs[0], refs[1].at[me], send_sems, recv_sems, k, (px, py, pc)))
                    recvs.append(_remote(refs[0], refs[1].at[4 * px + 2 * py + pc], send_sems, recv_sems, k,
                                         (px, py, pc)))
                    k += 1
        return sends, recvs

    got = _comm_call((build, [buf, jnp.zeros((N_DEV, R, 128), F32)], [1], N_DEV - 1), name + "_gather")[0]
    x, y, c = _coords()
    me = jnp.reshape(4 * x + 2 * y + c, (1,)).astype(jnp.int32)

    def body(s_ref, m_ref, own_ref, o_ref):
        acc = None
        for d in range(N_DEV):
            t = jnp.where(s_ref[0] == d, own_ref[...], m_ref[d])
            acc = t if acc is None else acc + t
        o_ref[...] = acc

    tr = max([t for t in range(8, min(R, 1024) + 1, 8) if R % t == 0], default=R)
    return pl.pallas_call(
        body,
        grid_spec=pltpu.PrefetchScalarGridSpec(
            num_scalar_prefetch=1, grid=(R // tr,),
            in_specs=[pl.BlockSpec((N_DEV, tr, 128), lambda i, s: (0, i, 0)),
                      pl.BlockSpec((tr, 128), lambda i, s: (i, 0))],
            out_specs=pl.BlockSpec((tr, 128), lambda i, s: (i, 0))),
        out_shape=jax.ShapeDtypeStruct((R, 128), F32),
        compiler_params=_params(("parallel",)), name=name + "_sum")(me, got, buf)


def _ssm_layouts(p, L):
    G = L * N_GROUPS
    lr = p["ssm_a_re"].reshape(G, 1, SSM_STATE)
    li = p["ssm_a_im"].reshape(G, 1, SSM_STATE)
    ldt = p["ssm_log_dt"].reshape(G, 1, 1)
    br = jnp.swapaxes(p["ssm_b_re"], 2, 3).reshape(G, SSM_GROUP, SSM_STATE)
    bi = jnp.swapaxes(p["ssm_b_im"], 2, 3).reshape(G, SSM_GROUP, SSM_STATE)
    are, aim, bre, bim = _disc_fwd(lr, li, ldt, br, bi)
    eye = jnp.eye(N_GROUPS, dtype=F32)
    abar = jnp.concatenate([are.reshape(L, 8, 128), aim.reshape(L, 8, 128)], axis=1)

    def b_blk(t):
        return jnp.einsum("lgcp,gh->lgchp", t.reshape(L, N_GROUPS, SSM_GROUP, SSM_STATE), eye).reshape(L, 256, 1024)

    def c_blk(t):
        return jnp.einsum("lgcp,gh->lgphc", t, eye).reshape(L, 1024, 256)

    bfull = jnp.concatenate([b_blk(bre), b_blk(bim)], axis=2).astype(BF16)
    cfull = jnp.concatenate([c_blk(p["ssm_c_re"]), -c_blk(p["ssm_c_im"])], axis=1).astype(BF16)
    return (lr, li, ldt, br, bi), abar, bfull, cfull


def _local_step(x, target, W, p, L, weave=None):
    S, Dm = x.shape
    dff = W["wg1"][0].shape[0]
    alpha = (2.0 * L) ** 0.25
    disc_in, abar, bfull, cfull = _ssm_layouts(p, L)
    row = lambda t, l: t[l][None]

    def carry(fn, site, l, *args, **kw):
        comm = weave.take(site, l) if weave is not None else None
        if comm is None:
            return fn(*args, **kw)
        res, couts = fn(*args, comm=comm, **kw)
        weave.done(site, l, couts)
        return res

    saved = []
    h, hb = x, x.astype(BF16)
    for l in range(L):
        sv = {"x0": hb}
        a1, b1, h1 = carry(_ffn_up, "up1", l, hb, W["wg1"][l], W["wu1"][l], dff, "ffn_up")
        x1, x1b, xh1, rs1 = carry(_mm_ln, "down1", l, h1, W["wd1"][l], h, row(p["ln1_g"], l), row(p["ln1_b"], l), 0.5,
                                  alpha, "ffn_down_ln", 1408)
        sv.update(a1=a1, b1=b1, h1=h1, x1=x1b, xh1=xh1, rs1=rs1)
        proj_a = _mm([(x1b, W["win"][l])], "nt", S, D_A, Dm, [F32], name="proj_a", tm=1024, tn=896, tk=1024)
        proj_g = carry(_mm, "proj_g", l, [(x1b, W["win"][l])], "nt", S, 3 * Dm, Dm, [BF16], name="proj_g", tm=2048,
                       tn=256, tk=1024, b_off=D_A)
        mc = _conv_fwd(proj_a, p["conv_w"][l], row(p["conv_b"], l), "conv_fwd")
        bu = _mm([(proj_a, (bfull, l))], "nn", S, 2048, D_SSM, [F32], name="ssm_bu", tm=1024, tn=1024, tk=256,
                 a_off=3 * D_CONV)
        xs3 = _scan_fwd(bu.reshape(S, 16, 128), abar[l], "scan_fwd")
        xs = xs3.reshape(S, 2048)
        y1, ms = _ssm_out(xs, proj_a, cfull[l], row(p["ssm_d"], l), W["wglu"][l], "ssm_out")
        sinks = p["attn_sinks"][l][None]
        ma = _attn_fwd(proj_a, sinks, "attn_fwd")
        merged = _merge_fwd(mc, ms, ma, proj_g, W["wbr"][l], "merge_fwd")
        x2, x2b, xh2, rs2 = _mm_ln(merged, W["wout"][l], x1, row(p["ln2_g"], l), row(p["ln2_b"], l), 1.0, alpha,
                                   "mix_out_ln", 1024)
        sv.update(proj_a=proj_a, proj_g=proj_g, mc=mc, ms=ms, ma=ma, xs=xs, xs3=xs3, y1=y1, merged=merged, x2=x2b,
                  xh2=xh2, rs2=rs2)
        a2, b2, h2 = carry(_ffn_up, "up2", l, x2b, W["wg2"][l], W["wu2"][l], dff, "ffn_up")
        h, hb, xh3, rs3 = _mm_ln(h2, W["wd2"][l], x2, row(p["ln3_g"], l), row(p["ln3_b"], l), 0.5, alpha,
                                 "ffn_down_ln", 1408)
        sv.update(a2=a2, b2=b2, h2=h2, xh3=xh3, rs3=rs3)
        saved.append(sv)

    dy, loss = _loss_head(h, target)
    big = [None] * L
    small = {k: [None] * L for k in ("ln1_g", "ln1_b", "ln2_g", "ln2_b", "ln3_g", "ln3_b", "conv_w", "conv_b", "ssm_d",
                                     "attn_sinks", "dabar", "dbfull", "dcfull")}

    def ffn_bwd(dy_out, x_in, a, b, hh, xh, rs, g, wg, wu, wd, l, sites):
        dres, df, dg, db = _ln_bwd(dy_out, xh, rs, g, 0.5, alpha, "ln_bwd")
        da, dbb = _ffn_dh(df, wd[l], a, b, "ffn_dh")
        dx = carry(_mm, sites[0], l, [(da, wg[l]), (dbb, wu[l])], "nn", S, Dm, dff, [F32], name="ffn_dx", tm=1024,
                   tn=1024, tk=dff // 2, add=dres)
        tn_kw = dict(tm=1408, tn=1024, tk=2048)
        dwg = carry(_mm, sites[1], l, [(da, x_in)], "tn", dff, Dm, S, [F32], name="ffn_dw_up", **tn_kw)
        dwu = carry(_mm, sites[2], l, [(dbb, x_in)], "tn", dff, Dm, S, [F32], name="ffn_dw_up", **tn_kw)
        dwd = carry(_mm, sites[3], l, [(hh, df)], "tn", dff, Dm, S, [F32], name="ffn_dw_down", **tn_kw)
        return dx, dwg, dwu, dwd, dg, db

    for l in reversed(range(L)):
        sv = saved[l]
        dx2, dwg2, dwu2, dwd2, small["ln3_g"][l], small["ln3_b"][l] = ffn_bwd(
            dy, sv["x2"], sv["a2"], sv["b2"], sv["h2"], sv["xh3"], sv["rs3"], row(p["ln3_g"], l), W["wg2"], W["wu2"],
            W["wd2"], l, ("dx2", "dwg2", "dwu2", "dwd2"))
        dres2, dmix, small["ln2_g"][l], small["ln2_b"][l] = _ln_bwd(dx2, sv["xh2"], sv["rs2"], row(p["ln2_g"], l), 1.0,
                                                                   alpha, "ln_bwd")
        dwout = carry(_mm, "dw_out", l, [(sv["merged"], dmix)], "tn", Dm, Dm, S, [F32], name="dw_out", tm=1024, tn=1024,
                      tk=2048)
        dmerged = _mm([(dmix, W["wout"][l])], "nt", S, Dm, Dm, [F32], name="d_merged", tm=1024, tn=1024, tk=512)
        dgates, dyb, dmc, dms, dma = _merge_bwd(dmerged, sv["mc"], sv["ms"], sv["ma"], sv["proj_g"], W["wbr"][l],
                                                "merge_bwd")
        dwbr = jnp.concatenate([
            _mm([(dyb, m)], "tn", Dm, hi - lo, S, [F32], name="dw_br", tm=1024, tn=512, tk=2048, a_off=b * Dm)
            for b, ((lo, hi), m) in enumerate(zip(_BR, (sv["mc"], sv["ms"], sv["ma"])))], axis=1)
        proj_a = sv["proj_a"]
        d_conv, small["conv_w"][l], small["conv_b"][l] = _conv_bwd(proj_a, dmc, p["conv_w"][l], row(p["conv_b"], l),
                                                                  "conv_bwd")
        dy1, y2, dgl, dxs, du_skip, small["ssm_d"][l] = _ssm_out_bwd(dms, sv["y1"], proj_a, cfull[l],
                                                                    row(p["ssm_d"], l), W["wglu"][l], "ssm_out_bwd")
        dwglu = _mm([(y2, dgl)], "tn", D_SSM, D_SSM, S, [F32], name="dw_glu", tk=2048)
        small["dcfull"][l] = _mm([(sv["xs"], dy1)], "tn", 2048, D_SSM, S, [F32], name="d_cfull", tm=1024, tk=1024)
        lam, small["dabar"][l] = _scan_bwd(dxs.reshape(S, 16, 128), sv["xs3"], abar[l], "scan_bwd")
        lam = lam.reshape(S, 2048)
        du = _mm([(lam, (bfull, l))], "nt", S, D_SSM, 2048, [BF16], name="ssm_du", tm=1024, tk=512, add=du_skip)
        small["dbfull"][l] = _mm([(proj_a, lam)], "tn", D_SSM, 2048, S, [F32], name="d_bfull", tm=256, tn=1024,
                                 tk=1024, a_off=3 * D_CONV)
        dq, dk, dv, dsk = _attn_bwd(proj_a, dma, p["attn_sinks"][l][None], "attn_bwd")
        small["attn_sinks"][l] = dsk[:, 0]
        dproj = jnp.concatenate([d_conv, du, dq, dk, dv, dgates], axis=1)
        dx1 = carry(_mm, "d_x1", l, [(dproj, W["win"][l])], "nn", S, Dm, D_A + 3 * Dm, [F32], name="d_x1", tm=1024,
                    tn=1024, tk=(D_A + 3 * Dm) // 2, add=dres2)
        dwin = carry(_mm, "dw_in", l, [(dproj, sv["x1"])], "tn", D_A + 3 * Dm, Dm, S, [F32], name="dw_in", tm=2432,
                     tn=1024, tk=1024)
        if weave is not None:
            weave.grads_x(l, dict(wg2=dwg2, wu2=dwu2, wd2=dwd2, win=dwin, wbr=dwbr, wout=dwout, wglu=dwglu))
        dx0, dwg1, dwu1, dwd1, small["ln1_g"][l], small["ln1_b"][l] = ffn_bwd(
            dx1, sv["x0"], sv["a1"], sv["b1"], sv["h1"], sv["xh1"], sv["rs1"], row(p["ln1_g"], l), W["wg1"], W["wu1"],
            W["wd1"], l, ("dx1", "dwg1", "dwu1", "dwd1"))
        big[l] = dict(wg1=dwg1, wu1=dwu1, wd1=dwd1, win=dwin, wbr=dwbr, wout=dwout, wg2=dwg2, wu2=dwu2, wd2=dwd2,
                      wglu=dwglu)
        if weave is not None:
            weave.grads_y(l, dict(wg1=dwg1, wu1=dwu1, wd1=dwd1))
        dy = dx0

    small = {k: jnp.stack(v) for k, v in small.items()}
    eye = jnp.eye(N_GROUPS, dtype=F32)
    dabar = small.pop("dabar")
    dbf = small.pop("dbfull").reshape(L, N_GROUPS, SSM_GROUP, 2, N_GROUPS, SSM_STATE)
    dbbar = jnp.einsum("lgcrhp,gh->rlgcp", dbf, eye).reshape(2, L * N_GROUPS, SSM_GROUP, SSM_STATE)
    dcf = small.pop("dcfull").reshape(L, 2, N_GROUPS, SSM_STATE, N_GROUPS, SSM_GROUP)
    dc = jnp.einsum("lrgphc,gh->rlgcp", dcf, eye)
    G = L * N_GROUPS
    cts = (dabar[:, 0:8].reshape(G, 1, SSM_STATE), dabar[:, 8:16].reshape(G, 1, SSM_STATE), dbbar[0], dbbar[1])
    dlr, dli, dldt, dbr, dbi = _disc_bwd(*disc_in, cts)
    shp_b = (L, N_GROUPS, SSM_GROUP, SSM_STATE)
    small.update(
        ssm_a_re=dlr.reshape(L, N_GROUPS, SSM_STATE), ssm_a_im=dli.reshape(L, N_GROUPS, SSM_STATE),
        ssm_log_dt=dldt.reshape(L, N_GROUPS), ssm_b_re=jnp.swapaxes(dbr.reshape(shp_b), 2, 3),
        ssm_b_im=jnp.swapaxes(dbi.reshape(shp_b), 2, 3), ssm_c_re=dc[0], ssm_c_im=-dc[1],
        ln1_g=small["ln1_g"][:, 0], ln1_b=small["ln1_b"][:, 0], ln2_g=small["ln2_g"][:, 0],
        ln2_b=small["ln2_b"][:, 0], ln3_g=small["ln3_g"][:, 0], ln3_b=small["ln3_b"][:, 0],
        conv_b=small["conv_b"][:, 0], ssm_d=small["ssm_d"][:, 0])
    return loss, dy, big, small


_SMALL_ORDER = ("ln1_g", "ln1_b", "ln2_g", "ln2_b", "ln3_g", "ln3_b", "conv_w", "conv_b", "ssm_a_re", "ssm_a_im",
                "ssm_log_dt", "ssm_b_re", "ssm_b_im", "ssm_c_re", "ssm_c_im", "ssm_d", "attn_sinks")
_BIG_ORDER = ("wg1", "wu1", "wd1", "win", "wbr", "wout", "wg2", "wu2", "wd2", "wglu")
_WEIGHTS = ("ffn1_w_gate", "ffn1_w_up", "ffn1_w_down", "ln1_g", "ln1_b", "w_in", "conv_w", "conv_b", "ssm_a_re",
            "ssm_a_im", "ssm_log_dt", "ssm_b_re", "ssm_b_im", "ssm_c_re", "ssm_c_im", "ssm_d", "ssm_w_glu",
            "attn_sinks", "w_br_conv", "w_br_ssm", "w_br_attn", "w_out", "ln2_g", "ln2_b", "ffn2_w_gate",
            "ffn2_w_up", "ffn2_w_down", "ln3_g", "ln3_b")


def _weight_shards(w):
    t = lambda a: jnp.swapaxes(a, 1, 2).astype(BF16)
    wbr = jnp.concatenate([t(w["w_br_conv"]), t(w["w_br_ssm"]), t(w["w_br_attn"])], axis=2)
    return dict(wg1=t(w["ffn1_w_gate"]), wu1=t(w["ffn1_w_up"]), wd1=w["ffn1_w_down"].astype(BF16), win=t(w["w_in"]),
                wbr=wbr, wout=w["w_out"].astype(BF16), wg2=t(w["ffn2_w_gate"]), wu2=t(w["ffn2_w_up"]),
                wd2=w["ffn2_w_down"].astype(BF16), wglu=w["ssm_w_glu"].astype(BF16))


def _gather_conv_w(conv_w, chip):
    L = conv_w.shape[0]
    n = L * 3 * 64
    slots = lax.dynamic_update_slice(jnp.zeros((N_CHIPS, n), F32), conv_w.reshape(1, n), (chip, 0))
    got = (_all_reduce_small(slots.reshape(-1, 128), "conv_w_gather") * 0.5).reshape(N_CHIPS, L, 3, 64)
    return jnp.transpose(got, (1, 2, 0, 3)).reshape(L, 3, N_CHIPS * 64)


def kernel(x, ffn1_w_gate, ffn1_w_up, ffn1_w_down, ln1_g, ln1_b, w_in, conv_w, conv_b, ssm_a_re, ssm_a_im, ssm_log_dt, ssm_b_re, ssm_b_im, ssm_c_re, ssm_c_im, ssm_d, ssm_w_glu, attn_sinks, w_br_conv, w_br_ssm, w_br_attn, w_out, ln2_g, ln2_b, ffn2_w_gate, ffn2_w_up, ffn2_w_down, ln3_g, ln3_b, loss_target, m_ffn1_w_gate, m_ffn1_w_up, m_ffn1_w_down, m_ln1_g, m_ln1_b, m_w_in, m_conv_w, m_conv_b, m_ssm_a_re, m_ssm_a_im, m_ssm_log_dt, m_ssm_b_re, m_ssm_b_im, m_ssm_c_re, m_ssm_c_im, m_ssm_d, m_ssm_w_glu, m_attn_sinks, m_w_br_conv, m_w_br_ssm, m_w_br_attn, m_w_out, m_ln2_g, m_ln2_b, m_ffn2_w_gate, m_ffn2_w_up, m_ffn2_w_down, m_ln3_g, m_ln3_b, v_ffn1_w_gate, v_ffn1_w_up, v_ffn1_w_down, v_ln1_g, v_ln1_b, v_w_in, v_conv_w, v_conv_b, v_ssm_a_re, v_ssm_a_im, v_ssm_log_dt, v_ssm_b_re, v_ssm_b_im, v_ssm_c_re, v_ssm_c_im, v_ssm_d, v_ssm_w_glu, v_attn_sinks, v_w_br_conv, v_w_br_ssm, v_w_br_attn, v_w_out, v_ln2_g, v_ln2_b, v_ffn2_w_gate, v_ffn2_w_up, v_ffn2_w_down, v_ln3_g, v_ln3_b):
    args = dict(locals())
    w = {k: args[k] for k in _WEIGHTS}
    L = ln1_g.shape[0]
    cx, cy, cc = _coords()
    chip = 2 * cx + cy

    shards = _weight_shards(w)
    full = {k: [lax.empty((N_CHIPS * s.shape[1], s.shape[2]), BF16) for _ in range(L)] for k, s in shards.items()}
    weave = _Weave(shards, full, L)
    weave.gather_first()
    p = {k: w[k] for k in ("ln1_g", "ln1_b", "ln2_g", "ln2_b", "ln3_g", "ln3_b", "conv_b", "ssm_a_re", "ssm_a_im",
                           "ssm_log_dt", "ssm_b_re", "ssm_b_im", "ssm_c_re", "ssm_c_im", "ssm_d", "attn_sinks")}
    p["conv_w"] = _gather_conv_w(conv_w, chip)

    loss, grad_x, _, small = _local_step(x[0], loss_target[0], full, p, L, weave)
    weave.flush()
    loss = lax.psum(loss[0, 0], ("x", "y", "c"))

    sizes = [math.prod(small[k].shape) for k in _SMALL_ORDER]
    pad = (-sum(sizes)) % 1024
    flat = jnp.concatenate([small[k].reshape(-1) for k in _SMALL_ORDER] + [jnp.zeros((pad,), F32)])
    flat = _all_reduce_small(flat.reshape(-1, 128), "small_grads").reshape(-1)
    sm, off = {}, 0
    for k, n in zip(_SMALL_ORDER, sizes):
        sm[k] = flat[off:off + n].reshape(small[k].shape)
        off += n
    red = {k: t.reshape(L, 2 * t.shape[2], t.shape[3]) for k, t in weave.final.items()}
    tr = lambda a: jnp.swapaxes(a, 1, 2)
    grads = dict(sm)
    grads.update(
        ffn1_w_gate=tr(red["wg1"]), ffn1_w_up=tr(red["wu1"]), ffn1_w_down=red["wd1"], w_in=tr(red["win"]),
        w_br_conv=tr(red["wbr"][:, :, _BR[0][0]:_BR[0][1]]), w_br_ssm=tr(red["wbr"][:, :, _BR[1][0]:_BR[1][1]]),
        w_br_attn=tr(red["wbr"][:, :, _BR[2][0]:_BR[2][1]]), w_out=red["wout"], ffn2_w_gate=tr(red["wg2"]),
        ffn2_w_up=tr(red["wu2"]), ffn2_w_down=red["wd2"], ssm_w_glu=red["wglu"],
        conv_w=lax.dynamic_slice_in_dim(sm["conv_w"], chip * 64, 64, axis=2))

    outs = [[], [], [], []]
    for k in _WEIGHTS:
        d, nm, nv = _adamw(w[k], grads[k], args["m_" + k], args["v_" + k], "adamw")
        for lst, val in zip(outs, (grads[k], d, nm, nv)):
            lst.append(val)
    return (loss, grad_x[None], *outs[0], *outs[1], *outs[2], *outs[3])
```

```python
import functools
import math

import jax
import jax.numpy as jnp
from jax import lax
from jax.experimental import pallas as pl
from jax.experimental.pallas import tpu as pltpu

F32 = jnp.float32
BF16 = jnp.bfloat16

LN_EPS = 1e-5
D_CONV = 256
D_SSM = 256
N_GROUPS = 16
SSM_GROUP = 16
SSM_STATE = 64
N_Q_HEADS = 8
HEAD_DIM = 64
D_ATTN = 512
D_KV = 128
BLOCK = 128
D_A = 3 * D_CONV + D_SSM + D_ATTN + 2 * D_KV
ADAM_LR = 0.001
ADAM_B1 = 0.9
ADAM_B2 = 0.999
ADAM_EPS = 1e-08
ADAM_WD = 0.01
ADAM_STEP = 10

VMEM_LIMIT_BYTES = 56 * 1024 * 1024
MESH = pl.DeviceIdType.MESH
N_CHIPS = 4
N_DEV = 8


def _params(sem=None):
    return pltpu.CompilerParams(dimension_semantics=sem, vmem_limit_bytes=VMEM_LIMIT_BYTES)


def _op(op, block, imap):
    if isinstance(op, tuple):
        arr, l = op
        return arr, pl.BlockSpec((None,) + block, lambda *g: (l,) + imap(*g))
    return op, pl.BlockSpec(block, imap)


def _call(body, *, grid, in_specs, out_specs, out_shape, scratch=(), sem=None, name, ops, comm=None):
    if comm is None:
        return pl.pallas_call(body, grid=grid, in_specs=in_specs, out_specs=out_specs, out_shape=out_shape,
                              scratch_shapes=list(scratch), compiler_params=_params(sem), name=name)(*ops), []
    build, cins, couts, n_sem = comm
    n_in, n_out, n_scr, nci, nco, ng = len(in_specs), len(out_specs), len(scratch), len(cins), len(couts), len(grid)
    hbm = pl.BlockSpec(memory_space=pltpu.HBM)
    aliases = {n_in + o: n_out + j for j, o in enumerate(couts) if isinstance(o, int)}
    cshapes = [jax.ShapeDtypeStruct(cins[o].shape, cins[o].dtype) if isinstance(o, int) else o for o in couts]

    def hosted(*refs):
        refs = list(refs)
        main = refs[:n_in] + refs[n_in + nci:n_in + nci + n_out] + refs[n_in + nci + n_out + nco:-2]
        crefs, co = refs[n_in:n_in + nci], refs[n_in + nci + n_out:n_in + nci + n_out + nco]
        send_sems, recv_sems = refs[-2:]
        fresh = []
        for j, o in enumerate(couts):
            if isinstance(o, int):
                crefs[o] = co[j]
            else:
                fresh.append(co[j])

        def start():
            for cp in build(crefs, fresh, send_sems, recv_sems)[0]:
                cp.start()

        def finish():
            sends, recvs = build(crefs, fresh, send_sems, recv_sems)
            for cp in recvs:
                cp.wait_recv()
            for cp in sends:
                cp.wait_send()

        if ng == 0:
            start()
            finish()
            return
        ids = [pl.program_id(d) for d in range(ng)]
        first = functools.reduce(jnp.logical_and, [ids[d] == 0 for d in range(ng)])
        last = functools.reduce(jnp.logical_and, [ids[d] == grid[d] - 1 for d in range(ng)])
        pl.when(first)(start)
        body(*main)
        pl.when(last)(finish)

    res = pl.pallas_call(
        hosted, grid=grid, in_specs=list(in_specs) + [hbm] * nci, out_specs=list(out_specs) + [hbm] * nco,
        out_shape=list(out_shape) + cshapes,
        scratch_shapes=list(scratch) + [pltpu.SemaphoreType.DMA((n_sem,)), pltpu.SemaphoreType.DMA((n_sem,))],
        input_output_aliases=aliases,
        compiler_params=pltpu.CompilerParams(dimension_semantics=("arbitrary",) * ng if ng else None,
                                             vmem_limit_bytes=VMEM_LIMIT_BYTES, has_side_effects=True),
        name=name)(*ops, *cins)
    return res[:n_out], res[n_out:]


def _mm(pairs, mode, m, n, k, out_dtypes, *, name, tm=512, tn=512, tk=512, add=None, a_off=0, b_off=0, comm=None):
    tm, tn, tk = min(tm, m), min(tn, n), min(tk, k)
    assert m % tm == 0 and n % tn == 0 and k % tk == 0 and b_off % tn == 0, (name, m, n, k, tm, tn, tk)
    nk, npair, jo = k // tk, len(pairs), b_off // tn
    if mode == "nn":
        ao = a_off // tk
        ab, ai, bb, bi = (tm, tk), (lambda i, j, kk: (i, kk + ao)), (tk, tn), (lambda i, j, kk: (kk, j))
        dims = (((1,), (0,)), ((), ()))
    elif mode == "nt":
        ao = a_off // tk
        ab, ai, bb, bi = (tm, tk), (lambda i, j, kk: (i, kk + ao)), (tn, tk), (lambda i, j, kk: (j + jo, kk))
        dims = (((1,), (1,)), ((), ()))
    else:
        ao = a_off // tm
        ab, ai, bb, bi = (tk, tm), (lambda i, j, kk: (kk, i + ao)), (tk, tn), (lambda i, j, kk: (kk, j))
        dims = (((0,), (0,)), ((), ()))
    assert a_off % (tm if mode == "tn" else tk) == 0, (name, a_off)
    ops, specs = [], []
    for a, b in pairs:
        for o, blk, im in ((a, ab, ai), (b, bb, bi)):
            arr, sp = _op(o, blk, im)
            ops.append(arr)
            specs.append(sp)
    has_add = add is not None
    if has_add:
        ops.append(add)
        specs.append(pl.BlockSpec((tm, tn), lambda i, j, kk: (i, j)))
    nout = len(out_dtypes)

    def body(*refs):
        outs, acc = refs[2 * npair + has_add:2 * npair + has_add + nout], refs[-1]
        kk = pl.program_id(2)
        t = None
        for p in range(npair):
            d = lax.dot_general(refs[2 * p][...].astype(BF16), refs[2 * p + 1][...].astype(BF16), dims,
                                preferred_element_type=F32)
            t = d if t is None else t + d

        def finish(r):
            if has_add:
                r = r + refs[2 * npair][...]
            for o in outs:
                o[...] = r.astype(o.dtype)

        if nk == 1:
            finish(t)
            return

        @pl.when(kk == 0)
        def _():
            acc[...] = jnp.zeros_like(acc)

        acc[...] += t

        @pl.when(kk == nk - 1)
        def _():
            finish(acc[...])

    res, couts = _call(
        body, grid=(m // tm, n // tn, nk), in_specs=specs,
        out_specs=[pl.BlockSpec((tm, tn), lambda i, j, kk: (i, j))] * nout,
        out_shape=[jax.ShapeDtypeStruct((m, n), dt) for dt in out_dtypes],
        scratch=[pltpu.VMEM((tm, tn), F32)], sem=("parallel", "parallel", "arbitrary"), name=name, ops=ops, comm=comm)
    res = res[0] if nout == 1 else res
    return res if comm is None else (res, couts)


def _rows(width, col=0, tm=None):
    return pl.BlockSpec((tm, width), lambda i: (i, col))


def _whole(shape):
    nd = len(shape)
    return pl.BlockSpec(shape, lambda i: (0,) * nd)


def _sigmoid(x):
    return 0.5 * jnp.tanh(0.5 * x) + 0.5


def _mm_ln(a, w, x, g, b, s, alpha, name, tk, comm=None):
    S, K = a.shape
    Dm = x.shape[1]
    tm, tk = min(1024, S), min(tk, K)
    nk = K // tk

    def body(a_ref, w_ref, x_ref, g_ref, b_ref, y_ref, yb_ref, xh_ref, rs_ref, acc):
        kk = pl.program_id(1)
        d = jnp.dot(a_ref[...], w_ref[...], preferred_element_type=F32)

        if nk > 1:
            @pl.when(kk == 0)
            def _():
                acc[...] = d

            @pl.when((kk > 0) & (kk < nk - 1))
            def _():
                acc[...] += d

        @pl.when(kk == nk - 1)
        def _():
            z = alpha * x_ref[...] + s * (d if nk == 1 else acc[...] + d)
            mu = jnp.mean(z, axis=-1, keepdims=True)
            zc = z - mu
            var = jnp.mean(zc * zc, axis=-1, keepdims=True)
            rstd = lax.rsqrt(var + LN_EPS)
            xh = zc * rstd
            y = xh * g_ref[...] + b_ref[...]
            y_ref[...] = y
            yb_ref[...] = y.astype(BF16)
            xh_ref[...] = xh
            rs_ref[...] = rstd

    wa, ws = _op(w, (tk, Dm), lambda i, kk: (kk, 0))
    row = pl.BlockSpec((tm, Dm), lambda i, kk: (i, 0))
    vec = pl.BlockSpec((1, Dm), lambda i, kk: (0, 0))
    res, couts = _call(
        body, grid=(S // tm, nk),
        in_specs=[pl.BlockSpec((tm, tk), lambda i, kk: (i, kk)), ws, row, vec, vec],
        out_specs=[row, row, row, pl.BlockSpec((tm, 1), lambda i, kk: (i, 0))],
        out_shape=[jax.ShapeDtypeStruct((S, Dm), F32), jax.ShapeDtypeStruct((S, Dm), BF16),
                   jax.ShapeDtypeStruct((S, Dm), F32), jax.ShapeDtypeStruct((S, 1), F32)],
        scratch=[pltpu.VMEM((tm, Dm), F32)], sem=("parallel", "arbitrary"), name=name, ops=[a, wa, x, g, b],
        comm=comm)
    return res if comm is None else (res, couts)


def _ln_bwd(dy, xh, rs, g, s, alpha, name):
    S, Dm = dy.shape
    tm = min(512, S)

    def body(dy_ref, xh_ref, rs_ref, g_ref, dres_ref, dbr_ref, dg_ref, db_ref):
        @pl.when(pl.program_id(0) == 0)
        def _():
            dg_ref[...] = jnp.zeros_like(dg_ref)
            db_ref[...] = jnp.zeros_like(db_ref)

        dy, xh = dy_ref[...], xh_ref[...]
        dyg = dy * g_ref[...]
        m1 = jnp.mean(dyg, axis=-1, keepdims=True)
        m2 = jnp.mean(dyg * xh, axis=-1, keepdims=True)
        dz = rs_ref[...] * (dyg - m1 - xh * m2)
        dres_ref[...] = alpha * dz
        dbr_ref[...] = (s * dz).astype(BF16)
        dg_ref[...] += jnp.sum(dy * xh, axis=0, keepdims=True)
        db_ref[...] += jnp.sum(dy, axis=0, keepdims=True)

    return pl.pallas_call(
        body, grid=(S // tm,),
        in_specs=[_rows(Dm, tm=tm), _rows(Dm, tm=tm), _rows(1, tm=tm), _whole((1, Dm))],
        out_specs=[_rows(Dm, tm=tm), _rows(Dm, tm=tm), _whole((1, Dm)), _whole((1, Dm))],
        out_shape=[jax.ShapeDtypeStruct((S, Dm), F32), jax.ShapeDtypeStruct((S, Dm), BF16),
                   jax.ShapeDtypeStruct((1, Dm), F32), jax.ShapeDtypeStruct((1, Dm), F32)],
        compiler_params=_params(("arbitrary",)), name=name)(dy, xh, rs, g)


def _ffn_up(x, wg, wu, dff, name, comm=None):
    S, Dm = x.shape
    tm, tn = min(512, S), dff // 2

    def body(x_ref, wg_ref, wu_ref, a_ref, b_ref, h_ref):
        xb = x_ref[...]
        dims = (((1,), (1,)), ((), ()))
        a = lax.dot_general(xb, wg_ref[...], dims, preferred_element_type=F32)
        b = lax.dot_general(xb, wu_ref[...], dims, preferred_element_type=F32)
        a_ref[...] = a.astype(BF16)
        b_ref[...] = b.astype(BF16)
        h_ref[...] = (a * _sigmoid(a) * b).astype(BF16)

    wga, wgs = _op(wg, (tn, Dm), lambda i, j: (j, 0))
    wua, wus = _op(wu, (tn, Dm), lambda i, j: (j, 0))
    ob = pl.BlockSpec((tm, tn), lambda i, j: (i, j))
    res, couts = _call(
        body, grid=(S // tm, dff // tn),
        in_specs=[pl.BlockSpec((tm, Dm), lambda i, j: (i, 0)), wgs, wus], out_specs=[ob, ob, ob],
        out_shape=[jax.ShapeDtypeStruct((S, dff), BF16)] * 3, sem=("parallel", "parallel"), name=name,
        ops=[x, wga, wua], comm=comm)
    return res if comm is None else (res, couts)


def _ffn_dh(df, wd, a, b, name):
    S, Dm = df.shape
    dff = a.shape[1]
    tm, tn = min(512, S), dff // 2

    def body(df_ref, wd_ref, a_ref, b_ref, da_ref, db_ref):
        dh = lax.dot_general(df_ref[...], wd_ref[...], (((1,), (1,)), ((), ())), preferred_element_type=F32)
        a, b = a_ref[...].astype(F32), b_ref[...].astype(F32)
        sg = _sigmoid(a)
        da_ref[...] = (dh * b * (sg * (1.0 + a * (1.0 - sg)))).astype(BF16)
        db_ref[...] = (dh * (a * sg)).astype(BF16)

    wda, wds = _op(wd, (tn, Dm), lambda i, j: (j, 0))
    ob = pl.BlockSpec((tm, tn), lambda i, j: (i, j))
    return pl.pallas_call(
        body, grid=(S // tm, dff // tn),
        in_specs=[pl.BlockSpec((tm, Dm), lambda i, j: (i, 0)), wds, ob, ob], out_specs=[ob, ob],
        out_shape=[jax.ShapeDtypeStruct((S, dff), BF16), jax.ShapeDtypeStruct((S, dff), BF16)],
        compiler_params=_params(("parallel", "parallel")), name=name)(df, wda, a, b)


def _halo_prev(width, col, tm):
    return pl.BlockSpec((8, width), lambda i: (jnp.maximum(i * (tm // 8) - 1, 0), col))


def _halo_next(width, col, tm, S):
    return pl.BlockSpec((8, width), lambda i: (jnp.minimum((i + 1) * (tm // 8), S // 8 - 1), col))


def _shift_down(prev8, cur, n):
    ext = jnp.concatenate([prev8, cur], axis=0)
    return pltpu.roll(ext, n, axis=0)[8:]


def _shift_up(cur, next8, n):
    ext = jnp.concatenate([cur, next8], axis=0)
    return pltpu.roll(ext, ext.shape[0] - n, axis=0)[:cur.shape[0]]


def _conv_fwd(proj_a, conv_w, conv_b, name):
    S = proj_a.shape[0]
    tm = min(512, S)
    C = D_CONV

    def body(bg_ref, cg_ref, h_ref, cgp_ref, hp_ref, w_ref, cb_ref, out_ref):
        z = cg_ref[...] * h_ref[...]
        zp = jnp.where(pl.program_id(0) > 0, cgp_ref[...] * hp_ref[...], 0.0)
        w = w_ref[...]
        y = w[2:3] * z + w[1:2] * _shift_down(zp, z, 1) + w[0:1] * _shift_down(zp, z, 2) + cb_ref[...]
        out_ref[...] = (bg_ref[...] * y).astype(BF16)

    return pl.pallas_call(
        body, grid=(S // tm,),
        in_specs=[_rows(C, 0, tm), _rows(C, 1, tm), _rows(C, 2, tm), _halo_prev(C, 1, tm), _halo_prev(C, 2, tm),
                  _whole((3, C)), _whole((1, C))],
        out_specs=_rows(C, 0, tm), out_shape=jax.ShapeDtypeStruct((S, C), BF16),
        compiler_params=_params(("parallel",)), name=name)(proj_a, proj_a, proj_a, proj_a, proj_a, conv_w, conv_b)


def _conv_bwd(proj_a, dmc, conv_w, conv_b, name):
    S = proj_a.shape[0]
    tm = min(512, S)
    C = D_CONV
    nblk = S // tm

    def body(bg_ref, cg_ref, h_ref, cgp_ref, hp_ref, bgn_ref, d_ref, dn_ref, w_ref, cb_ref, out_ref, dw_ref,
             dcb_ref):
        i = pl.program_id(0)

        @pl.when(i == 0)
        def _():
            dw_ref[...] = jnp.zeros_like(dw_ref)
            dcb_ref[...] = jnp.zeros_like(dcb_ref)

        bg, cg, h, d = bg_ref[...], cg_ref[...], h_ref[...], d_ref[...]
        z = cg * h
        zp = jnp.where(i > 0, cgp_ref[...] * hp_ref[...], 0.0)
        w = w_ref[...]
        z1, z2 = _shift_down(zp, z, 1), _shift_down(zp, z, 2)
        y = w[2:3] * z + w[1:2] * z1 + w[0:1] * z2 + cb_ref[...]
        dy = d * bg
        dyn = jnp.where(i < nblk - 1, dn_ref[...] * bgn_ref[...], 0.0)
        dz = w[2:3] * dy + w[1:2] * _shift_up(dy, dyn, 1) + w[0:1] * _shift_up(dy, dyn, 2)
        out_ref[:, 0:C] = (d * y).astype(BF16)
        out_ref[:, C:2 * C] = (dz * h).astype(BF16)
        out_ref[:, 2 * C:3 * C] = (dz * cg).astype(BF16)
        dw_ref[0:1, :] += jnp.sum(dy * z2, axis=0, keepdims=True)
        dw_ref[1:2, :] += jnp.sum(dy * z1, axis=0, keepdims=True)
        dw_ref[2:3, :] += jnp.sum(dy * z, axis=0, keepdims=True)
        dcb_ref[...] += jnp.sum(dy, axis=0, keepdims=True)

    return pl.pallas_call(
        body, grid=(nblk,),
        in_specs=[_rows(C, 0, tm), _rows(C, 1, tm), _rows(C, 2, tm), _halo_prev(C, 1, tm), _halo_prev(C, 2, tm),
                  _halo_next(C, 0, tm, S), _rows(C, 0, tm), _halo_next(C, 0, tm, S), _whole((3, C)), _whole((1, C))],
        out_specs=[_rows(3 * C, 0, tm), _whole((3, C)), _whole((1, C))],
        out_shape=[jax.ShapeDtypeStruct((S, 3 * C), BF16), jax.ShapeDtypeStruct((3, C), F32),
                   jax.ShapeDtypeStruct((1, C), F32)],
        compiler_params=_params(("arbitrary",)), name=name)(
            proj_a, proj_a, proj_a, proj_a, proj_a, proj_a, dmc, dmc, conv_w, conv_b)


def _disc_math(lr, li, ldt, br, bi):
    dt = jnp.exp(ldt)
    mag = jnp.exp(lr * dt)
    ang = li * dt
    are = mag * jnp.cos(ang)
    aim = mag * jnp.sin(ang)
    nr = are - 1.0
    den = lr * lr + li * li
    cre = (nr * lr + aim * li) / den
    cim = (aim * lr - nr * li) / den
    return are, aim, cre * br - cim * bi, cre * bi + cim * br


def _disc_fwd(lr, li, ldt, br, bi):
    shapes = [lr.shape, lr.shape, br.shape, br.shape]

    def body(lr_ref, li_ref, ldt_ref, br_ref, bi_ref, *outs):
        for o, v in zip(outs, _disc_math(lr_ref[...], li_ref[...], ldt_ref[...], br_ref[...], bi_ref[...])):
            o[...] = v

    return pl.pallas_call(body, out_shape=[jax.ShapeDtypeStruct(s, F32) for s in shapes],
                          compiler_params=_params(), name="ssm_disc")(lr, li, ldt, br, bi)


def _disc_bwd(lr, li, ldt, br, bi, cts):
    shapes = [lr.shape, lr.shape, ldt.shape, br.shape, br.shape]

    def body(lr_ref, li_ref, ldt_ref, br_ref, bi_ref, c0, c1, c2, c3, *outs):
        _, vjp = jax.vjp(_disc_math, lr_ref[...], li_ref[...], ldt_ref[...], br_ref[...], bi_ref[...])
        for o, v in zip(outs, vjp((c0[...], c1[...], c2[...], c3[...]))):
            o[...] = v

    return pl.pallas_call(body, out_shape=[jax.ShapeDtypeStruct(s, F32) for s in shapes],
                          compiler_params=_params(), name="ssm_disc_bwd")(lr, li, ldt, br, bi, *cts)


def _scan_fwd(bu, abar, name):
    S = bu.shape[0]
    tb = min(256, S)

    def body(bu_ref, a_ref, xs_ref, st_ref):
        @pl.when(pl.program_id(0) == 0)
        def _():
            st_ref[...] = jnp.zeros_like(st_ref)

        ar, ai = a_ref[0:8, :], a_ref[8:16, :]

        def step(t, c):
            xr, xi = c
            nr = ar * xr - ai * xi + bu_ref[t, 0:8, :]
            ni = ar * xi + ai * xr + bu_ref[t, 8:16, :]
            xs_ref[t, 0:8, :] = nr
            xs_ref[t, 8:16, :] = ni
            return nr, ni

        xr, xi = lax.fori_loop(0, tb, step, (st_ref[0:8, :], st_ref[8:16, :]), unroll=8)
        st_ref[0:8, :] = xr
        st_ref[8:16, :] = xi

    blk = pl.BlockSpec((tb, 16, 128), lambda i: (i, 0, 0))
    return pl.pallas_call(
        body, grid=(S // tb,), in_specs=[blk, _whole((16, 128))], out_specs=blk,
        out_shape=jax.ShapeDtypeStruct((S, 16, 128), F32), scratch_shapes=[pltpu.VMEM((16, 128), F32)],
        compiler_params=_params(("arbitrary",)), name=name)(bu, abar)


def _scan_bwd(dxs, xs, abar, name):
    S = dxs.shape[0]
    tb = min(256, S)
    nblk = S // tb

    def body(d_ref, x_ref, xp_ref, a_ref, lam_ref, da_ref, st_ref):
        i = pl.program_id(0)

        @pl.when(i == 0)
        def _():
            st_ref[...] = jnp.zeros_like(st_ref)
            da_ref[...] = jnp.zeros_like(da_ref)

        ar, ai = a_ref[0:8, :], a_ref[8:16, :]

        def one(t, c, pr, pi):
            lr, li, gr, gi = c
            nr = d_ref[t, 0:8, :] + ar * lr + ai * li
            ni = d_ref[t, 8:16, :] - ai * lr + ar * li
            lam_ref[t, 0:8, :] = nr
            lam_ref[t, 8:16, :] = ni
            return nr, ni, gr + nr * pr + ni * pi, gi - nr * pi + ni * pr

        def step(s, c):
            t = tb - 1 - s
            return one(t, c, x_ref[t - 1, 0:8, :], x_ref[t - 1, 8:16, :])

        c = (st_ref[0:8, :], st_ref[8:16, :], jnp.zeros((8, 128), F32), jnp.zeros((8, 128), F32))
        c = lax.fori_loop(0, tb - 1, step, c, unroll=8)
        first = i == nblk - 1
        pr = jnp.where(first, 0.0, xp_ref[0, 0:8, :])
        pi = jnp.where(first, 0.0, xp_ref[0, 8:16, :])
        lr, li, gr, gi = one(0, c, pr, pi)
        st_ref[0:8, :] = lr
        st_ref[8:16, :] = li
        da_ref[0:8, :] += gr
        da_ref[8:16, :] += gi

    blk = pl.BlockSpec((tb, 16, 128), lambda i: (nblk - 1 - i, 0, 0))
    prev = pl.BlockSpec((1, 16, 128), lambda i: (jnp.maximum((nblk - 1 - i) * tb - 1, 0), 0, 0))
    return pl.pallas_call(
        body, grid=(nblk,), in_specs=[blk, blk, prev, _whole((16, 128))], out_specs=[blk, _whole((16, 128))],
        out_shape=[jax.ShapeDtypeStruct((S, 16, 128), F32), jax.ShapeDtypeStruct((16, 128), F32)],
        scratch_shapes=[pltpu.VMEM((16, 128), F32)],
        compiler_params=_params(("arbitrary",)), name=name)(dxs, xs, xs, abar)


def _gelu(x):
    return 0.5 * x * (1.0 + jnp.tanh(0.7978845608028654 * (x + 0.044715 * x * x * x)))


def _gelu_grad(x):
    t = jnp.tanh(0.7978845608028654 * (x + 0.044715 * x * x * x))
    return 0.5 * (1.0 + t) + 0.5 * x * (1.0 - t * t) * 0.7978845608028654 * (1.0 + 3.0 * 0.044715 * x * x)


def _ssm_out(xs, proj_a, cfull, dskip, wglu, name):
    S = xs.shape[0]
    tm = min(512, S)
    C = D_SSM

    def body(xs_ref, u_ref, c_ref, d_ref, wg_ref, y1_ref, ms_ref):
        y1 = jnp.dot(xs_ref[...].astype(BF16), c_ref[...], preferred_element_type=F32) + d_ref[...] * u_ref[...]
        y2 = _gelu(y1)
        gl = jnp.dot(y2.astype(BF16), wg_ref[...], preferred_element_type=F32)
        y1_ref[...] = y1
        ms_ref[...] = (y2 * _sigmoid(gl)).astype(BF16)

    wga, wgs = _op(wglu, (C, C), lambda i: (0, 0))
    return pl.pallas_call(
        body, grid=(S // tm,),
        in_specs=[_rows(2 * 1024, 0, tm), _rows(C, 3, tm), _whole((2 * 1024, C)), _whole((1, C)), wgs],
        out_specs=[_rows(C, 0, tm), _rows(C, 0, tm)],
        out_shape=[jax.ShapeDtypeStruct((S, C), F32), jax.ShapeDtypeStruct((S, C), BF16)],
        compiler_params=_params(("parallel",)), name=name)(xs, proj_a, cfull, dskip, wga)


def _ssm_out_bwd(dms, y1, proj_a, cfull, dskip, wglu, name):
    S = y1.shape[0]
    tm = min(512, S)
    C = D_SSM

    def body(dms_ref, y1_ref, u_ref, c_ref, d_ref, wg_ref, dy1_ref, y2_ref, dgl_ref, dxs_ref, du_ref, dd_ref):
        @pl.when(pl.program_id(0) == 0)
        def _():
            dd_ref[...] = jnp.zeros_like(dd_ref)

        dms, y1 = dms_ref[...], y1_ref[...]
        y2 = _gelu(y1)
        y2b = y2.astype(BF16)
        sg = _sigmoid(jnp.dot(y2b, wg_ref[...], preferred_element_type=F32))
        dgl = (dms * y2 * sg * (1.0 - sg)).astype(BF16)
        dy2 = dms * sg + lax.dot_general(dgl, wg_ref[...], (((1,), (1,)), ((), ())), preferred_element_type=F32)
        dy1 = dy2 * _gelu_grad(y1)
        dy1b = dy1.astype(BF16)
        dy1_ref[...] = dy1b
        y2_ref[...] = y2b
        dgl_ref[...] = dgl
        dxs_ref[...] = lax.dot_general(dy1b, c_ref[...], (((1,), (1,)), ((), ())), preferred_element_type=F32)
        du_ref[...] = d_ref[...] * dy1
        dd_ref[...] += jnp.sum(dy1 * u_ref[...], axis=0, keepdims=True)

    wga, wgs = _op(wglu, (C, C), lambda i: (0, 0))
    rc = _rows(C, 0, tm)
    return pl.pallas_call(
        body, grid=(S // tm,),
        in_specs=[rc, rc, _rows(C, 3, tm), _whole((2 * 1024, C)), _whole((1, C)), wgs],
        out_specs=[rc, rc, rc, _rows(2 * 1024, 0, tm), rc, _whole((1, C))],
        out_shape=[jax.ShapeDtypeStruct((S, C), BF16), jax.ShapeDtypeStruct((S, C), BF16),
                   jax.ShapeDtypeStruct((S, C), BF16), jax.ShapeDtypeStruct((S, 2 * 1024), F32),
                   jax.ShapeDtypeStruct((S, C), F32), jax.ShapeDtypeStruct((1, C), F32)],
        compiler_params=_params(("arbitrary",)), name=name)(dms, y1, proj_a, cfull, dskip, wga)


_NT = (((1,), (1,)), ((), ()))
_TN = (((0,), (0,)), ((), ()))


def _attn_heads(q_ref, kp_ref, kc_ref, vp_ref, vc_ref):
    lane = lax.broadcasted_iota(jnp.int32, (BLOCK, 128), 1)
    kk = jnp.concatenate([kp_ref[...], kc_ref[...]], axis=0).astype(BF16)
    vv = jnp.concatenate([vp_ref[...], vc_ref[...]], axis=0).astype(BF16)
    kk_r, vv_r = pltpu.roll(kk, 64, axis=1), pltpu.roll(vv, 64, axis=1)
    heads = []
    for hq in range(N_Q_HEADS):
        j, e = hq // 2, hq % 2
        qj = (q_ref[:, 128 * j:128 * (j + 1)] * (HEAD_DIM ** -0.5)).astype(BF16)
        own = (lane >= 64) if e else (lane < 64)
        aligned = e == hq // 4
        heads.append((own, jnp.where(own, qj, jnp.zeros_like(qj)), kk if aligned else kk_r, vv if aligned else vv_r,
                      aligned))
    return heads


def _attn_probs(i, heads, s_ref):
    n = N_Q_HEADS * BLOCK
    s = jnp.concatenate([lax.dot_general(qm, ks, _NT, preferred_element_type=F32) for _, qm, ks, _, _ in heads],
                        axis=0)
    row = lax.broadcasted_iota(jnp.int32, (n, 2 * BLOCK), 0) & (BLOCK - 1)
    col = lax.broadcasted_iota(jnp.int32, (n, 2 * BLOCK), 1)
    mask = (col > row) & (col <= row + BLOCK) & ((col >= BLOCK) | (i > 0))
    s = jnp.where(mask, s, -1e30)
    sink = jnp.concatenate([jnp.full((BLOCK, 1), s_ref[0, hq], F32) for hq in range(N_Q_HEADS)], axis=0)
    m = jnp.maximum(jnp.max(s, axis=1, keepdims=True), sink)
    p = jnp.exp(s - m)
    es = jnp.exp(sink - m)
    inv = 1.0 / (jnp.sum(p, axis=1, keepdims=True) + es)
    return p * inv, es * inv


def _attn_fwd(proj_a, sinks, name):
    S = proj_a.shape[0]
    nb = S // BLOCK

    def body(q_ref, kp_ref, kc_ref, vp_ref, vc_ref, s_ref, out_ref):
        i = pl.program_id(0)
        heads = _attn_heads(q_ref, kp_ref, kc_ref, vp_ref, vc_ref)
        row = lax.broadcasted_iota(jnp.int32, (BLOCK, 2 * BLOCK), 0)
        col = lax.broadcasted_iota(jnp.int32, (BLOCK, 2 * BLOCK), 1)
        mask = (col > row) & (col <= row + BLOCK) & ((col >= BLOCK) | (i > 0))
        outs = []
        for hq, (_, qm, ks, vs, _) in enumerate(heads):
            s = jnp.where(mask, lax.dot_general(qm, ks, _NT, preferred_element_type=F32), -1e30)
            sink = s_ref[0, hq]
            m = jnp.maximum(jnp.max(s, axis=1, keepdims=True), sink)
            p = jnp.exp(s - m)
            inv = 1.0 / (jnp.sum(p, axis=1, keepdims=True) + jnp.exp(sink - m))
            outs.append(jnp.dot((p * inv).astype(BF16), vs, preferred_element_type=F32))
        for j in range(4):
            out_ref[:, 128 * j:128 * (j + 1)] = jnp.where(heads[2 * j][0], outs[2 * j], outs[2 * j + 1]).astype(BF16)

    prev = lambda c: pl.BlockSpec((BLOCK, 128), lambda i: (jnp.maximum(i - 1, 0), c))
    cur = lambda c: pl.BlockSpec((BLOCK, 128), lambda i: (i, c))
    return pl.pallas_call(
        body, grid=(nb,),
        in_specs=[_rows(D_ATTN, 2, BLOCK), prev(12), cur(12), prev(13), cur(13),
                  pl.BlockSpec(memory_space=pltpu.SMEM)],
        out_specs=_rows(D_ATTN, 0, BLOCK), out_shape=jax.ShapeDtypeStruct((S, D_ATTN), BF16),
        compiler_params=_params(("parallel",)), name=name)(proj_a, proj_a, proj_a, proj_a, proj_a, sinks)


def _attn_bwd(proj_a, dout, sinks, name):
    S = proj_a.shape[0]
    nb = S // BLOCK

    def body(q_ref, kp_ref, kc_ref, vp_ref, vc_ref, do_ref, s_ref, out_ref, dk_ref, dv_ref, ds_ref, ck_ref, cv_ref):
        i = pl.program_id(0)

        @pl.when(i == 0)
        def _():
            ds_ref[...] = jnp.zeros_like(ds_ref)
            ck_ref[...] = jnp.zeros_like(ck_ref)
            cv_ref[...] = jnp.zeros_like(cv_ref)

        @pl.when(i < nb)
        def _():
            heads = _attn_heads(q_ref, kp_ref, kc_ref, vp_ref, vc_ref)
            pn, psink = _attn_probs(i, heads, s_ref)
            doms = []
            for hq, (own, _, _, _, _) in enumerate(heads):
                doj = do_ref[:, 128 * (hq // 2):128 * (hq // 2 + 1)].astype(BF16)
                doms.append(jnp.where(own, doj, jnp.zeros_like(doj)))
            dp = jnp.concatenate([lax.dot_general(dom, vs, _NT, preferred_element_type=F32)
                                  for dom, (_, _, _, vs, _) in zip(doms, heads)], axis=0)
            delta = jnp.sum(pn * dp, axis=1, keepdims=True)
            dsb = (pn * (dp - delta)).astype(BF16)
            pnb = pn.astype(BF16)
            sk = psink * delta
            dkk = jnp.zeros((2 * BLOCK, 128), F32)
            dvv = jnp.zeros((2 * BLOCK, 128), F32)
            dqs = []
            for hq, (own, qm, ks, vs, aligned) in enumerate(heads):
                rows = slice(BLOCK * hq, BLOCK * (hq + 1))
                ds_ref[hq:hq + 1, :] += jnp.broadcast_to(-jnp.sum(sk[rows]), (1, 128))
                dqs.append(jnp.dot(dsb[rows], ks, preferred_element_type=F32) * (HEAD_DIM ** -0.5))
                dk = lax.dot_general(dsb[rows], qm, _TN, preferred_element_type=F32)
                dv = lax.dot_general(pnb[rows], doms[hq], _TN, preferred_element_type=F32)
                dkk = dkk + (dk if aligned else pltpu.roll(dk, 64, axis=1))
                dvv = dvv + (dv if aligned else pltpu.roll(dv, 64, axis=1))
            for j in range(4):
                out_ref[:, 128 * j:128 * (j + 1)] = jnp.where(heads[2 * j][0], dqs[2 * j], dqs[2 * j + 1]).astype(BF16)
            ck_ref[0:BLOCK, :] = ck_ref[BLOCK:, :] + dkk[0:BLOCK]
            cv_ref[0:BLOCK, :] = cv_ref[BLOCK:, :] + dvv[0:BLOCK]
            ck_ref[BLOCK:, :] = dkk[BLOCK:]
            cv_ref[BLOCK:, :] = dvv[BLOCK:]

        @pl.when(i == nb)
        def _():
            ck_ref[0:BLOCK, :] = ck_ref[BLOCK:, :]
            cv_ref[0:BLOCK, :] = cv_ref[BLOCK:, :]

        dk_ref[...] = ck_ref[0:BLOCK, :].astype(BF16)
        dv_ref[...] = cv_ref[0:BLOCK, :].astype(BF16)

    last = nb - 1
    prev = lambda c: pl.BlockSpec((BLOCK, 128), lambda i: (jnp.clip(i - 1, 0, last), c))
    cur = lambda c: pl.BlockSpec((BLOCK, 128), lambda i: (jnp.minimum(i, last), c))
    qrow = lambda w, c: pl.BlockSpec((BLOCK, w), lambda i: (jnp.minimum(i, last), c))

    dq, dk, dv, ds = pl.pallas_call(
        body, grid=(nb + 1,),
        in_specs=[qrow(D_ATTN, 2), prev(12), cur(12), prev(13), cur(13), qrow(D_ATTN, 0),
                  pl.BlockSpec(memory_space=pltpu.SMEM)],
        out_specs=[qrow(D_ATTN, 0), prev(0), prev(0), _whole((N_Q_HEADS, 128))],
        out_shape=[jax.ShapeDtypeStruct((S, D_ATTN), BF16), jax.ShapeDtypeStruct((S, D_KV), BF16),
                   jax.ShapeDtypeStruct((S, D_KV), BF16), jax.ShapeDtypeStruct((N_Q_HEADS, 128), F32)],
        scratch_shapes=[pltpu.VMEM((2 * BLOCK, 128), F32), pltpu.VMEM((2 * BLOCK, 128), F32)],
        compiler_params=_params(("arbitrary",)), name=name)(proj_a, proj_a, proj_a, proj_a, proj_a, dout, sinks)
    return dq, dk, dv, ds


_BR = ((0, D_CONV), (D_CONV, D_CONV + D_SSM), (D_CONV + D_SSM, D_CONV + D_SSM + D_ATTN))


def _branches(m_refs, wbr_ref):
    nt = (((1,), (1,)), ((), ()))
    return [lax.dot_general(m[...], wbr_ref[:, lo:hi], nt, preferred_element_type=F32)
            for m, (lo, hi) in zip(m_refs, _BR)]


def _merge_fwd(mc, ms, ma, proj_g, wbr_t, name):
    S, Dm = mc.shape[0], proj_g.shape[1] // 3
    tm = min(512, S)

    def body(mc_ref, ms_ref, ma_ref, g_ref, w_ref, out_ref):
        ys = _branches((mc_ref, ms_ref, ma_ref), w_ref)
        acc = None
        for b in range(3):
            t = _sigmoid(g_ref[:, b * Dm:(b + 1) * Dm].astype(F32)) * ys[b]
            acc = t if acc is None else acc + t
        out_ref[...] = acc.astype(BF16)

    wa, ws = _op(wbr_t, (Dm, Dm), lambda i: (0, 0))
    return pl.pallas_call(
        body, grid=(S // tm,),
        in_specs=[_rows(D_CONV, 0, tm), _rows(D_SSM, 0, tm), _rows(D_ATTN, 0, tm), _rows(3 * Dm, 0, tm), ws],
        out_specs=_rows(Dm, 0, tm), out_shape=jax.ShapeDtypeStruct((S, Dm), BF16),
        compiler_params=_params(("parallel",)), name=name)(mc, ms, ma, proj_g, wa)


def _merge_bwd(dmerged, mc, ms, ma, proj_g, wbr_t, name):
    S, Dm = mc.shape[0], proj_g.shape[1] // 3
    tm = min(256, S)

    def body(d_ref, mc_ref, ms_ref, ma_ref, g_ref, w_ref, dg_ref, dy_ref, dmc_ref, dms_ref, dma_ref):
        ys = _branches((mc_ref, ms_ref, ma_ref), w_ref)
        d = d_ref[...]
        for b, (o_ref, (lo, hi)) in enumerate(zip((dmc_ref, dms_ref, dma_ref), _BR)):
            g = _sigmoid(g_ref[:, b * Dm:(b + 1) * Dm].astype(F32))
            dg_ref[:, b * Dm:(b + 1) * Dm] = (d * ys[b] * g * (1.0 - g)).astype(BF16)
            dyb = (g * d).astype(BF16)
            dy_ref[:, b * Dm:(b + 1) * Dm] = dyb
            o_ref[...] = jnp.dot(dyb, w_ref[:, lo:hi], preferred_element_type=F32)

    wa, ws = _op(wbr_t, (Dm, Dm), lambda i: (0, 0))
    return pl.pallas_call(
        body, grid=(S // tm,),
        in_specs=[_rows(Dm, 0, tm), _rows(D_CONV, 0, tm), _rows(D_SSM, 0, tm), _rows(D_ATTN, 0, tm),
                  _rows(3 * Dm, 0, tm), ws],
        out_specs=[_rows(3 * Dm, 0, tm), _rows(3 * Dm, 0, tm), _rows(D_CONV, 0, tm), _rows(D_SSM, 0, tm),
                   _rows(D_ATTN, 0, tm)],
        out_shape=[jax.ShapeDtypeStruct((S, 3 * Dm), BF16), jax.ShapeDtypeStruct((S, 3 * Dm), BF16),
                   jax.ShapeDtypeStruct((S, D_CONV), F32), jax.ShapeDtypeStruct((S, D_SSM), F32),
                   jax.ShapeDtypeStruct((S, D_ATTN), F32)],
        compiler_params=_params(("parallel",)), name=name)(dmerged, mc, ms, ma, proj_g, wa)


def _loss_head(y, target):
    S, Dm = y.shape
    tm = min(512, S)

    def body(y_ref, t_ref, dy_ref, l_ref):
        @pl.when(pl.program_id(0) == 0)
        def _():
            l_ref[...] = jnp.zeros_like(l_ref)

        e = y_ref[...] - t_ref[...]
        dy_ref[...] = e * (1.0 / Dm)
        l_ref[...] += jnp.broadcast_to(0.5 * jnp.sum(jnp.sum(e * e, axis=1, keepdims=True) * (1.0 / Dm)), (1, 128))

    return pl.pallas_call(
        body, grid=(S // tm,), in_specs=[_rows(Dm, 0, tm), _rows(Dm, 0, tm)],
        out_specs=[_rows(Dm, 0, tm), _whole((1, 128))],
        out_shape=[jax.ShapeDtypeStruct((S, Dm), F32), jax.ShapeDtypeStruct((1, 128), F32)],
        compiler_params=_params(("arbitrary",)), name="loss_head")(y, target)


def _view2d(shape):
    n = math.prod(shape)
    if shape[-1] % 128 == 0:
        return (n // shape[-1], shape[-1])
    if n >= (1 << 16) and len(shape) == 3:
        return (shape[0] * shape[1], shape[2])
    if n % 128 == 0:
        return (n // 128, 128)
    return (1, n)


def _adamw(w, g, m, v, name, blocks_3d=False):
    shape = w.shape
    c1 = 1.0 - ADAM_B1 ** ADAM_STEP
    c2 = 1.0 - ADAM_B2 ** ADAM_STEP

    def body(w_ref, g_ref, m_ref, v_ref, d_ref, nm_ref, nv_ref):
        g = g_ref[...]
        nm = ADAM_B1 * m_ref[...] + (1.0 - ADAM_B1) * g
        nv = ADAM_B2 * v_ref[...] + (1.0 - ADAM_B2) * (g * g)
        d_ref[...] = -ADAM_LR * ((nm / c1) / (jnp.sqrt(nv / c2) + ADAM_EPS) + ADAM_WD * w_ref[...])
        nm_ref[...] = nm
        nv_ref[...] = nv

    if blocks_3d:
        Ls, R, C = shape
        tm = max(t for t in range(8, min(R, 512) + 1, 8) if R % t == 0)
        blk = pl.BlockSpec((None, tm, C), lambda l, i: (l, i, 0))
        return pl.pallas_call(
            body, grid=(Ls, R // tm), in_specs=[blk] * 4, out_specs=[blk] * 3,
            out_shape=[jax.ShapeDtypeStruct(shape, F32)] * 3,
            compiler_params=_params(("parallel", "parallel")), name=name)(w, g, m, v)
    R, C = _view2d(shape)
    tm = R
    for cand in (512, 352, 256):
        if R > cand and R % cand == 0:
            tm = cand
            break
    blk = _rows(C, 0, tm)
    outs = pl.pallas_call(
        body, grid=(R // tm,), in_specs=[blk] * 4, out_specs=[blk] * 3,
        out_shape=[jax.ShapeDtypeStruct((R, C), F32)] * 3,
        compiler_params=_params(("parallel",)), name=name)(*[t.reshape(R, C) for t in (w, g, m, v)])
    return [o.reshape(shape) for o in outs]


def _coords():
    return lax.axis_index("x"), lax.axis_index("y"), lax.axis_index("c")


def _other_chips(x, y):
    return [((1 - x, y), 2 * (1 - x) + y), ((x, 1 - y), 2 * x + 1 - y), ((1 - x, 1 - y), 2 * (1 - x) + 1 - y)]


def _comm_call(comm, name):
    return _call(None, grid=(), in_specs=[], out_specs=[], out_shape=[], name=name, ops=[], comm=comm)[1]


def _remote(src, dst, send_sems, recv_sems, k, dev):
    return pltpu.make_async_remote_copy(src_ref=src, dst_ref=dst, send_sem=send_sems.at[k], recv_sem=recv_sems.at[k],
                                        device_id=dev, device_id_type=MESH)


def _half(ref, chip, c, rp):
    return ref.at[pl.ds(pl.multiple_of(chip * rp + c * (rp // 2), 16), rp // 2), :]


def _gather_ici(shards, fulls, l):
    n = len(shards)

    def build(refs, fresh, send_sems, recv_sems):
        x, y, c = _coords()
        me = 2 * x + y
        sends, recvs = [], []
        for a in range(n):
            rp = shards[a].shape[1]
            src = refs[a].at[l].at[pl.ds(pl.multiple_of(c * (rp // 2), 16), rp // 2), :]
            for r, ((px, py), chip) in enumerate(_other_chips(x, y)):
                k = 3 * a + r
                sends.append(_remote(src, _half(refs[n + a], me, c, rp), send_sems, recv_sems, k, (px, py, c)))
                recvs.append(_remote(src, _half(refs[n + a], chip, c, rp), send_sems, recv_sems, k, (px, py, c)))
        return sends, recvs

    return build, list(shards) + list(fulls), [n + a for a in range(n)], 3 * n


def _gather_d2d(fulls, shards, l):
    n = len(fulls)

    def build(refs, fresh, send_sems, recv_sems):
        x, y, c = _coords()
        me = 2 * x + y
        sends, recvs = [], []
        for a in range(n):
            rp = fulls[a].shape[0] // N_CHIPS
            for r, (_, chip) in enumerate(_other_chips(x, y)):
                k = 4 * a + r
                mine, theirs = _half(refs[a], chip, c, rp), _half(refs[a], chip, 1 - c, rp)
                sends.append(_remote(mine, mine, send_sems, recv_sems, k, (x, y, 1 - c)))
                recvs.append(_remote(theirs, theirs, send_sems, recv_sems, k, (x, y, 1 - c)))
            own = refs[a].at[pl.ds(pl.multiple_of(me * rp, 16), rp), :]
            sends.append(_remote(refs[n + a].at[l], own, send_sems, recv_sems, 4 * a + 3, (x, y, 1 - c)))
            recvs.append(_remote(refs[n + a].at[l], own, send_sems, recv_sems, 4 * a + 3, (x, y, 1 - c)))
        return sends, recvs

    return build, list(fulls) + list(shards), list(range(n)), 4 * n


def _pair_sum(a, b, c_idx, half_rows, out_dtype, name):
    n4, rp, W = a.shape
    tr = half_rows // 2
    nblk = half_rows // tr

    def body(c_ref, a_ref, b_ref, o_ref):
        o_ref[...] = (a_ref[...] + b_ref[...]).astype(o_ref.dtype)

    return pl.pallas_call(
        body,
        grid_spec=pltpu.PrefetchScalarGridSpec(
            num_scalar_prefetch=1, grid=(nblk,),
            in_specs=[pl.BlockSpec((n4, tr, W), lambda i, c: (0, c[0] * nblk + i, 0)),
                      pl.BlockSpec((n4, tr, W), lambda i, c: (0, i, 0))],
            out_specs=pl.BlockSpec((n4, tr, W), lambda i, c: (0, i, 0))),
        out_shape=jax.ShapeDtypeStruct((n4, half_rows, W), out_dtype),
        compiler_params=_params(("parallel",)), name=name)(c_idx, a, b)


def _sum4(land, own, me_c, stacked, l, name):
    _, R, W = land.shape
    tr = R // 2

    def body(s_ref, l_ref, o_ref, stacked_ref, out_ref):
        me = s_ref[0]
        acc = None
        for i in range(N_CHIPS):
            t = jnp.where(me == i, o_ref[...], l_ref[i]).astype(F32)
            acc = t if acc is None else acc + t
        out_ref[...] = acc

    return pl.pallas_call(
        body,
        grid_spec=pltpu.PrefetchScalarGridSpec(
            num_scalar_prefetch=1, grid=(R // tr,),
            in_specs=[pl.BlockSpec((N_CHIPS, tr, W), lambda i, s: (0, i, 0)),
                      pl.BlockSpec((None, tr, W), lambda i, s: (s[0], i, 0)),
                      pl.BlockSpec(memory_space=pltpu.HBM)],
            out_specs=pl.BlockSpec((None, None, tr, W), lambda i, s: (l, s[1], i, 0))),
        out_shape=jax.ShapeDtypeStruct(stacked.shape, F32), input_output_aliases={3: 0},
        compiler_params=_params(("parallel",)), name=name)(me_c, land, own, stacked)


def _reduce_d2d(g4):
    n = len(g4)

    def build(refs, fresh, send_sems, recv_sems):
        x, y, c = _coords()
        sends, recvs = [], []
        for a in range(n):
            hr = g4[a].shape[1] // 2
            src = refs[a].at[:, pl.ds(pl.multiple_of((1 - c) * hr, 8), hr), :]
            sends.append(_remote(src, fresh[a], send_sems, recv_sems, a, (x, y, 1 - c)))
            recvs.append(_remote(src, fresh[a], send_sems, recv_sems, a, (x, y, 1 - c)))
        return sends, recvs

    outs = [jax.ShapeDtypeStruct((N_CHIPS, g.shape[1] // 2, g.shape[2]), F32) for g in g4]
    return build, list(g4), outs, n


def _reduce_ici(q, lands):
    n = len(q)

    def build(refs, fresh, send_sems, recv_sems):
        x, y, c = _coords()
        me = 2 * x + y
        sends, recvs = [], []
        for a in range(n):
            for r, ((px, py), chip) in enumerate(_other_chips(x, y)):
                k = 3 * a + r
                sends.append(_remote(refs[a].at[chip], refs[n + a].at[me], send_sems, recv_sems, k, (px, py, c)))
                recvs.append(_remote(refs[a].at[chip], refs[n + a].at[chip], send_sems, recv_sems, k, (px, py, c)))
        return sends, recvs

    return build, list(q) + list(lands), [n + a for a in range(n)], 3 * n


def _reduce_share(r, l):
    n = len(r)

    def build(refs, fresh, send_sems, recv_sems):
        x, y, c = _coords()
        sends, recvs = [], []
        for a in range(n):
            mine, theirs = refs[a].at[l].at[c], refs[a].at[l].at[1 - c]
            sends.append(_remote(mine, mine, send_sems, recv_sems, a, (x, y, 1 - c)))
            recvs.append(_remote(mine, theirs, send_sems, recv_sems, a, (x, y, 1 - c)))
        return sends, recvs

    return build, list(r), list(range(n)), n


class _SemOffset:
    def __init__(self, sems, off):
        self.sems, self.off = sems, off

    @property
    def at(self):
        return self

    def __getitem__(self, k):
        return self.sems.at[k + self.off]


def _merge(c1, c2):
    b1, i1, o1, s1 = c1
    b2, i2, o2, s2 = c2
    n1, f1 = len(i1), sum(not isinstance(o, int) for o in o1)

    def build(refs, fresh, send_sems, recv_sems):
        sa, ra = b1(refs[:n1], fresh[:f1], send_sems, recv_sems)
        sb, rb = b2(refs[n1:], fresh[f1:], _SemOffset(send_sems, s1), _SemOffset(recv_sems, s1))
        return sa + sb, ra + rb

    return build, list(i1) + list(i2), list(o1) + [o + n1 if isinstance(o, int) else o for o in o2], s1 + s2


class _Weave:
    A, B, C = ("wg1", "wu1", "wd1"), ("win", "wbr", "wout", "wglu"), ("wg2", "wu2", "wd2")
    X, Y = C + B, A
    ICI_X = {"dwg1": ("wg2", "wu2"), "dwu1": ("wd2", "wbr", "wout", "wglu"), "dwd1": ("win",)}
    ICI_Y = {"dwg2": ("wg1", "wu1"), "dwu2": ("wd1",)}

    def __init__(self, shards, W, L):
        self.shards, self.W, self.L = shards, W, L
        x, y, c = _coords()
        self.c_idx = jnp.reshape(c, (1,)).astype(jnp.int32)
        self.me_c = jnp.stack([2 * x + y, c]).astype(jnp.int32)
        self.final = {k: lax.empty((L, 2, s.shape[1] // 2, s.shape[2]), F32) for k, s in shards.items()}
        self.x = self.y = None
        self.parts = []

    def _ici(self, keys, l):
        def cb(outs):
            for k, t in zip(keys, outs):
                self.W[k][l] = t
        return _gather_ici([self.shards[k] for k in keys], [self.W[k][l] for k in keys], l), cb

    def _d2d(self, keys, l):
        def cb(outs):
            for k, t in zip(keys, outs):
                self.W[k][l] = t
        return _gather_d2d([self.W[k][l] for k in keys], [self.shards[k] for k in keys], l), cb

    def _r1(self, st, keys):
        def cb(outs):
            st["q"] = {k: _pair_sum(st["g4"][k], t, self.c_idx, t.shape[1], BF16, "reduce_pair_sum")
                       for k, t in zip(keys, outs)}
        return _reduce_d2d([st["g4"][k] for k in keys]), cb

    def _r2(self, st, keys):
        def cb(outs):
            for k, t in zip(keys, outs):
                self.final[k] = _sum4(t, st["q"][k], self.me_c, self.final[k], st["layer"], "reduce_sum4")
        return _reduce_ici([st["q"][k] for k in keys], [jnp.zeros(st["q"][k].shape, BF16) for k in keys]), cb

    def _r3(self, st, keys):
        def cb(outs):
            for k, t in zip(keys, outs):
                self.final[k] = t
        return _reduce_share([self.final[k] for k in keys], st["layer"]), cb

    def _pieces(self, site, l):
        nxt = l is not None and l + 1 < self.L
        if site == "up1":
            return [self._ici(self.B, l)]
        if site == "down1":
            return [self._d2d(self.B, l), self._ici(self.C, l)]
        if site == "proj_g":
            return [self._d2d(self.C, l)] + ([self._ici(self.A, l + 1)] if nxt else [])
        if site == "up2":
            return [self._d2d(self.A, l + 1)] if nxt else []
        if site == "dx1" and self.x is not None:
            return [self._r1(self.x, self.X)]
        if site in self.ICI_X and self.x is not None:
            return [self._r2(self.x, self.ICI_X[site])]
        if site == "dx2":
            ps = [self._r3(self.x, self.X)] if self.x is not None else []
            return ps + ([self._r1(self.y, self.Y)] if self.y is not None else [])
        if site in self.ICI_Y and self.y is not None:
            return [self._r2(self.y, self.ICI_Y[site])]
        if site == "dwd2" and self.y is not None:
            return [self._r3(self.y, self.Y)]
        return []

    def take(self, site, l):
        self.parts = self._pieces(site, l)
        if not self.parts:
            return None
        comm = self.parts[0][0]
        for c, _ in self.parts[1:]:
            comm = _merge(comm, c)
        return comm

    def done(self, site, l, outs):
        i = 0
        for comm, cb in self.parts:
            cb(outs[i:i + len(comm[2])])
            i += len(comm[2])
        if site == "dx2":
            self.x = None
        if site == "dwd2":
            self.y = None

    def gather_first(self):
        for piece, name in ((self._ici, "gather0_ici"), (self._d2d, "gather0_d2d")):
            comm, cb = piece(self.A, 0)
            cb(_comm_call(comm, name))

    def _state(self, l, grads):
        return dict(layer=l, g4={k: g.reshape(N_CHIPS, g.shape[0] // N_CHIPS, g.shape[1]) for k, g in grads.items()})

    def grads_x(self, l, grads):
        assert self.x is None
        self.x = self._state(l, grads)

    def grads_y(self, l, grads):
        assert self.y is None
        self.y = self._state(l, grads)

    def flush(self):
        for site, name in (("dx2", "reduce_tail_d2d"), ("dwg2", "reduce_tail_ici"), ("dwu2", "reduce_tail_ici"),
                           ("dwd2", "reduce_tail_share")):
            self.done(site, None, _comm_call(self.take(site, None), name))


def _all_reduce_small(buf, name):
    R = buf.shape[0]

    def build(refs, fresh, send_sems, recv_sems):
        x, y, c = _coords()
        me = 4 * x + 2 * y + c
        sends, recvs = [], []
        k = 0
        for fx in range(2):
            for fy in range(2):
                for fc in range(2):
                    if fx + fy + fc == 0:
                        continue
                    px, py, pc = x ^ fx, y ^ fy, c ^ fc
                    sends.append(_remote(refs[0], refs[1].at[me], send_sems, recv_sems, k, (px, py, pc)))
                    recvs.append(_remote(refs[0], refs[1].at[4 * px + 2 * py + pc], send_sems, recv_sems, k,
                                         (px, py, pc)))
                    k += 1
        return sends, recvs

    got = _comm_call((build, [buf, jnp.zeros((N_DEV, R, 128), F32)], [1], N_DEV - 1), name + "_gather")[0]
    x, y, c = _coords()
    me = jnp.reshape(4 * x + 2 * y + c, (1,)).astype(jnp.int32)

    def body(s_ref, m_ref, own_ref, o_ref):
        acc = None
        for d in range(N_DEV):
            t = jnp.where(s_ref[0] == d, own_ref[...], m_ref[d])
            acc = t if acc is None else acc + t
        o_ref[...] = acc

    tr = max([t for t in range(8, min(R, 1024) + 1, 8) if R % t == 0], default=R)
    return pl.pallas_call(
        body,
        grid_spec=pltpu.PrefetchScalarGridSpec(
            num_scalar_prefetch=1, grid=(R // tr,),
            in_specs=[pl.BlockSpec((N_DEV, tr, 128), lambda i, s: (0, i, 0)),
                      pl.BlockSpec((tr, 128), lambda i, s: (i, 0))],
            out_specs=pl.BlockSpec((tr, 128), lambda i, s: (i, 0))),
        out_shape=jax.ShapeDtypeStruct((R, 128), F32),
        compiler_params=_params(("parallel",)), name=name + "_sum")(me, got, buf)


def _ssm_layouts(p, L):
    G = L * N_GROUPS
    lr = p["ssm_a_re"].reshape(G, 1, SSM_STATE)
    li = p["ssm_a_im"].reshape(G, 1, SSM_STATE)
    ldt = p["ssm_log_dt"].reshape(G, 1, 1)
    br = jnp.swapaxes(p["ssm_b_re"], 2, 3).reshape(G, SSM_GROUP, SSM_STATE)
    bi = jnp.swapaxes(p["ssm_b_im"], 2, 3).reshape(G, SSM_GROUP, SSM_STATE)
    are, aim, bre, bim = _disc_fwd(lr, li, ldt, br, bi)
    eye = jnp.eye(N_GROUPS, dtype=F32)
    abar = jnp.concatenate([are.reshape(L, 8, 128), aim.reshape(L, 8, 128)], axis=1)

    def b_blk(t):
        return jnp.einsum("lgcp,gh->lgchp", t.reshape(L, N_GROUPS, SSM_GROUP, SSM_STATE), eye).reshape(L, 256, 1024)

    def c_blk(t):
        return jnp.einsum("lgcp,gh->lgphc", t, eye).reshape(L, 1024, 256)

    bfull = jnp.concatenate([b_blk(bre), b_blk(bim)], axis=2).astype(BF16)
    cfull = jnp.concatenate([c_blk(p["ssm_c_re"]), -c_blk(p["ssm_c_im"])], axis=1).astype(BF16)
    return (lr, li, ldt, br, bi), abar, bfull, cfull


def _local_step(x, target, W, p, L, weave=None):
    S, Dm = x.shape
    dff = W["wg1"][0].shape[0]
    alpha = (2.0 * L) ** 0.25
    disc_in, abar, bfull, cfull = _ssm_layouts(p, L)
    row = lambda t, l: t[l][None]

    def carry(fn, site, l, *args, **kw):
        comm = weave.take(site, l) if weave is not None else None
        if comm is None:
            return fn(*args, **kw)
        res, couts = fn(*args, comm=comm, **kw)
        weave.done(site, l, couts)
        return res

    saved = []
    h, hb = x, x.astype(BF16)
    for l in range(L):
        sv = {"x0": hb}
        a1, b1, h1 = carry(_ffn_up, "up1", l, hb, W["wg1"][l], W["wu1"][l], dff, "ffn_up")
        x1, x1b, xh1, rs1 = carry(_mm_ln, "down1", l, h1, W["wd1"][l], h, row(p["ln1_g"], l), row(p["ln1_b"], l), 0.5,
                                  alpha, "ffn_down_ln", 1408)
        sv.update(a1=a1, b1=b1, h1=h1, x1=x1b, xh1=xh1, rs1=rs1)
        proj_a = _mm([(x1b, W["win"][l])], "nt", S, D_A, Dm, [F32], name="proj_a", tm=1024, tn=896, tk=1024)
        proj_g = carry(_mm, "proj_g", l, [(x1b, W["win"][l])], "nt", S, 3 * Dm, Dm, [BF16], name="proj_g", tm=2048,
                       tn=256, tk=1024, b_off=D_A)
        mc = _conv_fwd(proj_a, p["conv_w"][l], row(p["conv_b"], l), "conv_fwd")
        bu = _mm([(proj_a, (bfull, l))], "nn", S, 2048, D_SSM, [F32], name="ssm_bu", tm=1024, tn=1024, tk=256,
                 a_off=3 * D_CONV)
        xs3 = _scan_fwd(bu.reshape(S, 16, 128), abar[l], "scan_fwd")
        xs = xs3.reshape(S, 2048)
        y1, ms = _ssm_out(xs, proj_a, cfull[l], row(p["ssm_d"], l), W["wglu"][l], "ssm_out")
        sinks = p["attn_sinks"][l][None]
        ma = _attn_fwd(proj_a, sinks, "attn_fwd")
        merged = _merge_fwd(mc, ms, ma, proj_g, W["wbr"][l], "merge_fwd")
        x2, x2b, xh2, rs2 = _mm_ln(merged, W["wout"][l], x1, row(p["ln2_g"], l), row(p["ln2_b"], l), 1.0, alpha,
                                   "mix_out_ln", 1024)
        sv.update(proj_a=proj_a, proj_g=proj_g, mc=mc, ms=ms, ma=ma, xs=xs, xs3=xs3, y1=y1, merged=merged, x2=x2b,
                  xh2=xh2, rs2=rs2)
        a2, b2, h2 = carry(_ffn_up, "up2", l, x2b, W["wg2"][l], W["wu2"][l], dff, "ffn_up")
        h, hb, xh3, rs3 = _mm_ln(h2, W["wd2"][l], x2, row(p["ln3_g"], l), row(p["ln3_b"], l), 0.5, alpha,
                                 "ffn_down_ln", 1408)
        sv.update(a2=a2, b2=b2, h2=h2, xh3=xh3, rs3=rs3)
        saved.append(sv)

    dy, loss = _loss_head(h, target)
    big = [None] * L
    small = {k: [None] * L for k in ("ln1_g", "ln1_b", "ln2_g", "ln2_b", "ln3_g", "ln3_b", "conv_w", "conv_b", "ssm_d",
                                     "attn_sinks", "dabar", "dbfull", "dcfull")}

    def ffn_bwd(dy_out, x_in, a, b, hh, xh, rs, g, wg, wu, wd, l, sites):
        dres, df, dg, db = _ln_bwd(dy_out, xh, rs, g, 0.5, alpha, "ln_bwd")
        da, dbb = _ffn_dh(df, wd[l], a, b, "ffn_dh")
        dx = carry(_mm, sites[0], l, [(da, wg[l]), (dbb, wu[l])], "nn", S, Dm, dff, [F32], name="ffn_dx", tm=1024,
                   tn=1024, tk=dff // 2, add=dres)
        tn_kw = dict(tm=1408, tn=1024, tk=2048)
        dwg = carry(_mm, sites[1], l, [(da, x_in)], "tn", dff, Dm, S, [F32], name="ffn_dw_up", **tn_kw)
        dwu = carry(_mm, sites[2], l, [(dbb, x_in)], "tn", dff, Dm, S, [F32], name="ffn_dw_up", **tn_kw)
        dwd = carry(_mm, sites[3], l, [(hh, df)], "tn", dff, Dm, S, [F32], name="ffn_dw_down", **tn_kw)
        return dx, dwg, dwu, dwd, dg, db

    for l in reversed(range(L)):
        sv = saved[l]
        dx2, dwg2, dwu2, dwd2, small["ln3_g"][l], small["ln3_b"][l] = ffn_bwd(
            dy, sv["x2"], sv["a2"], sv["b2"], sv["h2"], sv["xh3"], sv["rs3"], row(p["ln3_g"], l), W["wg2"], W["wu2"],
            W["wd2"], l, ("dx2", "dwg2", "dwu2", "dwd2"))
        dres2, dmix, small["ln2_g"][l], small["ln2_b"][l] = _ln_bwd(dx2, sv["xh2"], sv["rs2"], row(p["ln2_g"], l), 1.0,
                                                                   alpha, "ln_bwd")
        dwout = carry(_mm, "dw_out", l, [(sv["merged"], dmix)], "tn", Dm, Dm, S, [F32], name="dw_out", tm=1024, tn=1024,
                      tk=2048)
        dmerged = _mm([(dmix, W["wout"][l])], "nt", S, Dm, Dm, [F32], name="d_merged", tm=1024, tn=1024, tk=512)
        dgates, dyb, dmc, dms, dma = _merge_bwd(dmerged, sv["mc"], sv["ms"], sv["ma"], sv["proj_g"], W["wbr"][l],
                                                "merge_bwd")
        dwbr = jnp.concatenate([
            _mm([(dyb, m)], "tn", Dm, hi - lo, S, [F32], name="dw_br", tm=1024, tn=512, tk=2048, a_off=b * Dm)
            for b, ((lo, hi), m) in enumerate(zip(_BR, (sv["mc"], sv["ms"], sv["ma"])))], axis=1)
        proj_a = sv["proj_a"]
        d_conv, small["conv_w"][l], small["conv_b"][l] = _conv_bwd(proj_a, dmc, p["conv_w"][l], row(p["conv_b"], l),
                                                                  "conv_bwd")
        dy1, y2, dgl, dxs, du_skip, small["ssm_d"][l] = _ssm_out_bwd(dms, sv["y1"], proj_a, cfull[l],
                                                                    row(p["ssm_d"], l), W["wglu"][l], "ssm_out_bwd")
        dwglu = _mm([(y2, dgl)], "tn", D_SSM, D_SSM, S, [F32], name="dw_glu", tk=2048)
        small["dcfull"][l] = _mm([(sv["xs"], dy1)], "tn", 2048, D_SSM, S, [F32], name="d_cfull", tm=1024, tk=1024)
        lam, small["dabar"][l] = _scan_bwd(dxs.reshape(S, 16, 128), sv["xs3"], abar[l], "scan_bwd")
        lam = lam.reshape(S, 2048)
        du = _mm([(lam, (bfull, l))], "nt", S, D_SSM, 2048, [BF16], name="ssm_du", tm=1024, tk=512, add=du_skip)
        small["dbfull"][l] = _mm([(proj_a, lam)], "tn", D_SSM, 2048, S, [F32], name="d_bfull", tm=256, tn=1024,
                                 tk=1024, a_off=3 * D_CONV)
        dq, dk, dv, dsk = _attn_bwd(proj_a, dma, p["attn_sinks"][l][None], "attn_bwd")
        small["attn_sinks"][l] = dsk[:, 0]
        dproj = jnp.concatenate([d_conv, du, dq, dk, dv, dgates], axis=1)
        dx1 = carry(_mm, "d_x1", l, [(dproj, W["win"][l])], "nn", S, Dm, D_A + 3 * Dm, [F32], name="d_x1", tm=1024,
                    tn=1024, tk=(D_A + 3 * Dm) // 2, add=dres2)
        dwin = carry(_mm, "dw_in", l, [(dproj, sv["x1"])], "tn", D_A + 3 * Dm, Dm, S, [F32], name="dw_in", tm=2432,
                     tn=1024, tk=1024)
        if weave is not None:
            weave.grads_x(l, dict(wg2=dwg2, wu2=dwu2, wd2=dwd2, win=dwin, wbr=dwbr, wout=dwout, wglu=dwglu))
        dx0, dwg1, dwu1, dwd1, small["ln1_g"][l], small["ln1_b"][l] = ffn_bwd(
            dx1, sv["x0"], sv["a1"], sv["b1"], sv["h1"], sv["xh1"], sv["rs1"], row(p["ln1_g"], l), W["wg1"], W["wu1"],
            W["wd1"], l, ("dx1", "dwg1", "dwu1", "dwd1"))
        big[l] = dict(wg1=dwg1, wu1=dwu1, wd1=dwd1, win=dwin, wbr=dwbr, wout=dwout, wg2=dwg2, wu2=dwu2, wd2=dwd2,
                      wglu=dwglu)
        if weave is not None:
            weave.grads_y(l, dict(wg1=dwg1, wu1=dwu1, wd1=dwd1))
        dy = dx0

    small = {k: jnp.stack(v) for k, v in small.items()}
    eye = jnp.eye(N_GROUPS, dtype=F32)
    dabar = small.pop("dabar")
    dbf = small.pop("dbfull").reshape(L, N_GROUPS, SSM_GROUP, 2, N_GROUPS, SSM_STATE)
    dbbar = jnp.einsum("lgcrhp,gh->rlgcp", dbf, eye).reshape(2, L * N_GROUPS, SSM_GROUP, SSM_STATE)
    dcf = small.pop("dcfull").reshape(L, 2, N_GROUPS, SSM_STATE, N_GROUPS, SSM_GROUP)
    dc = jnp.einsum("lrgphc,gh->rlgcp", dcf, eye)
    G = L * N_GROUPS
    cts = (dabar[:, 0:8].reshape(G, 1, SSM_STATE), dabar[:, 8:16].reshape(G, 1, SSM_STATE), dbbar[0], dbbar[1])
    dlr, dli, dldt, dbr, dbi = _disc_bwd(*disc_in, cts)
    shp_b = (L, N_GROUPS, SSM_GROUP, SSM_STATE)
    small.update(
        ssm_a_re=dlr.reshape(L, N_GROUPS, SSM_STATE), ssm_a_im=dli.reshape(L, N_GROUPS, SSM_STATE),
        ssm_log_dt=dldt.reshape(L, N_GROUPS), ssm_b_re=jnp.swapaxes(dbr.reshape(shp_b), 2, 3),
        ssm_b_im=jnp.swapaxes(dbi.reshape(shp_b), 2, 3), ssm_c_re=dc[0], ssm_c_im=-dc[1],
        ln1_g=small["ln1_g"][:, 0], ln1_b=small["ln1_b"][:, 0], ln2_g=small["ln2_g"][:, 0],
        ln2_b=small["ln2_b"][:, 0], ln3_g=small["ln3_g"][:, 0], ln3_b=small["ln3_b"][:, 0],
        conv_b=small["conv_b"][:, 0], ssm_d=small["ssm_d"][:, 0])
    return loss, dy, big, small


_SMALL_ORDER = ("ln1_g", "ln1_b", "ln2_g", "ln2_b", "ln3_g", "ln3_b", "conv_w", "conv_b", "ssm_a_re", "ssm_a_im",
                "ssm_log_dt", "ssm_b_re", "ssm_b_im", "ssm_c_re", "ssm_c_im", "ssm_d", "attn_sinks")
_BIG_ORDER = ("wg1", "wu1", "wd1", "win", "wbr", "wout", "wg2", "wu2", "wd2", "wglu")
_WEIGHTS = ("ffn1_w_gate", "ffn1_w_up", "ffn1_w_down", "ln1_g", "ln1_b", "w_in", "conv_w", "conv_b", "ssm_a_re",
            "ssm_a_im", "ssm_log_dt", "ssm_b_re", "ssm_b_im", "ssm_c_re", "ssm_c_im", "ssm_d", "ssm_w_glu",
            "attn_sinks", "w_br_conv", "w_br_ssm", "w_br_attn", "w_out", "ln2_g", "ln2_b", "ffn2_w_gate",
            "ffn2_w_up", "ffn2_w_down", "ln3_g", "ln3_b")


def _weight_shards(w):
    t = lambda a: jnp.swapaxes(a, 1, 2).astype(BF16)
    wbr = jnp.concatenate([t(w["w_br_conv"]), t(w["w_br_ssm"]), t(w["w_br_attn"])], axis=2)
    return dict(wg1=t(w["ffn1_w_gate"]), wu1=t(w["ffn1_w_up"]), wd1=w["ffn1_w_down"].astype(BF16), win=t(w["w_in"]),
                wbr=wbr, wout=w["w_out"].astype(BF16), wg2=t(w["ffn2_w_gate"]), wu2=t(w["ffn2_w_up"]),
                wd2=w["ffn2_w_down"].astype(BF16), wglu=w["ssm_w_glu"].astype(BF16))


def _gather_conv_w(conv_w, chip):
    L = conv_w.shape[0]
    n = L * 3 * 64
    slots = lax.dynamic_update_slice(jnp.zeros((N_CHIPS, n), F32), conv_w.reshape(1, n), (chip, 0))
    got = (_all_reduce_small(slots.reshape(-1, 128), "conv_w_gather") * 0.5).reshape(N_CHIPS, L, 3, 64)
    return jnp.transpose(got, (1, 2, 0, 3)).reshape(L, 3, N_CHIPS * 64)


def kernel(x, ffn1_w_gate, ffn1_w_up, ffn1_w_down, ln1_g, ln1_b, w_in, conv_w, conv_b, ssm_a_re, ssm_a_im, ssm_log_dt, ssm_b_re, ssm_b_im, ssm_c_re, ssm_c_im, ssm_d, ssm_w_glu, attn_sinks, w_br_conv, w_br_ssm, w_br_attn, w_out, ln2_g, ln2_b, ffn2_w_gate, ffn2_w_up, ffn2_w_down, ln3_g, ln3_b, loss_target, m_ffn1_w_gate, m_ffn1_w_up, m_ffn1_w_down, m_ln1_g, m_ln1_b, m_w_in, m_conv_w, m_conv_b, m_ssm_a_re, m_ssm_a_im, m_ssm_log_dt, m_ssm_b_re, m_ssm_b_im, m_ssm_c_re, m_ssm_c_im, m_ssm_d, m_ssm_w_glu, m_attn_sinks, m_w_br_conv, m_w_br_ssm, m_w_br_attn, m_w_out, m_ln2_g, m_ln2_b, m_ffn2_w_gate, m_ffn2_w_up, m_ffn2_w_down, m_ln3_g, m_ln3_b, v_ffn1_w_gate, v_ffn1_w_up, v_ffn1_w_down, v_ln1_g, v_ln1_b, v_w_in, v_conv_w, v_conv_b, v_ssm_a_re, v_ssm_a_im, v_ssm_log_dt, v_ssm_b_re, v_ssm_b_im, v_ssm_c_re, v_ssm_c_im, v_ssm_d, v_ssm_w_glu, v_attn_sinks, v_w_br_conv, v_w_br_ssm, v_w_br_attn, v_w_out, v_ln2_g, v_ln2_b, v_ffn2_w_gate, v_ffn2_w_up, v_ffn2_w_down, v_ln3_g, v_ln3_b):
    args = dict(locals())
    w = {k: args[k] for k in _WEIGHTS}
    L = ln1_g.shape[0]
    cx, cy, cc = _coords()
    chip = 2 * cx + cy

    shards = _weight_shards(w)
    full = {k: [lax.empty((N_CHIPS * s.shape[1], s.shape[2]), BF16) for _ in range(L)] for k, s in shards.items()}
    weave = _Weave(shards, full, L)
    weave.gather_first()
    p = {k: w[k] for k in ("ln1_g", "ln1_b", "ln2_g", "ln2_b", "ln3_g", "ln3_b", "conv_b", "ssm_a_re", "ssm_a_im",
                           "ssm_log_dt", "ssm_b_re", "ssm_b_im", "ssm_c_re", "ssm_c_im", "ssm_d", "attn_sinks")}
    p["conv_w"] = _gather_conv_w(conv_w, chip)

    loss, grad_x, _, small = _local_step(x[0], loss_target[0], full, p, L, weave)
    weave.flush()
    loss = lax.psum(loss[0, 0], ("x", "y", "c"))

    sizes = [math.prod(small[k].shape) for k in _SMALL_ORDER]
    pad = (-sum(sizes)) % 1024
    flat = jnp.concatenate([small[k].reshape(-1) for k in _SMALL_ORDER] + [jnp.zeros((pad,), F32)])
    flat = _all_reduce_small(flat.reshape(-1, 128), "small_grads").reshape(-1)
    sm, off = {}, 0
    for k, n in zip(_SMALL_ORDER, sizes):
        sm[k] = flat[off:off + n].reshape(small[k].shape)
        off += n
    red = {k: t.reshape(L, 2 * t.shape[2], t.shape[3]) for k, t in weave.final.items()}
    tr = lambda a: jnp.swapaxes(a, 1, 2)
    grads = dict(sm)
    grads.update(
        ffn1_w_gate=tr(red["wg1"]), ffn1_w_up=tr(red["wu1"]), ffn1_w_down=red["wd1"], w_in=tr(red["win"]),
        w_br_conv=tr(red["wbr"][:, :, _BR[0][0]:_BR[0][1]]), w_br_ssm=tr(red["wbr"][:, :, _BR[1][0]:_BR[1][1]]),
        w_br_attn=tr(red["wbr"][:, :, _BR[2][0]:_BR[2][1]]), w_out=red["wout"], ffn2_w_gate=tr(red["wg2"]),
        ffn2_w_up=tr(red["wu2"]), ffn2_w_down=red["wd2"], ssm_w_glu=red["wglu"],
        conv_w=lax.dynamic_slice_in_dim(sm["conv_w"], chip * 64, 64, axis=2))

    outs = [[], [], [], []]
    for k in _WEIGHTS:
        d, nm, nv = _adamw(w[k], grads[k], args["m_" + k], args["v_" + k], "adamw",
                           blocks_3d=k in ("ffn1_w_down", "ffn2_w_down", "w_out"))
        for lst, val in zip(outs, (grads[k], d, nm, nv)):
            lst.append(val)
    return (loss, grad_x[None], *outs[0], *outs[1], *outs[2], *outs[3])
```

```python
import functools
import math

import jax
import jax.numpy as jnp
from jax import lax
from jax.experimental import pallas as pl
from jax.experimental.pallas import tpu as pltpu

F32 = jnp.float32
BF16 = jnp.bfloat16

LN_EPS = 1e-5
D_CONV = 256
D_SSM = 256
N_GROUPS = 16
SSM_GROUP = 16
SSM_STATE = 64
N_Q_HEADS = 8
HEAD_DIM = 64
D_ATTN = 512
D_KV = 128
BLOCK = 128
D_A = 3 * D_CONV + D_SSM + D_ATTN + 2 * D_KV
ADAM_LR = 0.001
ADAM_B1 = 0.9
ADAM_B2 = 0.999
ADAM_EPS = 1e-08
ADAM_WD = 0.01
ADAM_STEP = 10

VMEM_LIMIT_BYTES = 56 * 1024 * 1024
MESH = pl.DeviceIdType.MESH
N_CHIPS = 4
N_DEV = 8


def _params(sem=None):
    return pltpu.CompilerParams(dimension_semantics=sem, vmem_limit_bytes=VMEM_LIMIT_BYTES)


def _op(op, block, imap):
    if isinstance(op, tuple):
        arr, l = op
        return arr, pl.BlockSpec((None,) + block, lambda *g: (l,) + imap(*g))
    return op, pl.BlockSpec(block, imap)


def _call(body, *, grid, in_specs, out_specs, out_shape, scratch=(), sem=None, name, ops, comm=None):
    if comm is None:
        return pl.pallas_call(body, grid=grid, in_specs=in_specs, out_specs=out_specs, out_shape=out_shape,
                              scratch_shapes=list(scratch), compiler_params=_params(sem), name=name)(*ops), []
    build, cins, couts, n_sem = comm
    n_in, n_out, n_scr, nci, nco, ng = len(in_specs), len(out_specs), len(scratch), len(cins), len(couts), len(grid)
    hbm = pl.BlockSpec(memory_space=pltpu.HBM)
    aliases = {n_in + o: n_out + j for j, o in enumerate(couts) if isinstance(o, int)}
    cshapes = [jax.ShapeDtypeStruct(cins[o].shape, cins[o].dtype) if isinstance(o, int) else o for o in couts]

    def hosted(*refs):
        refs = list(refs)
        main = refs[:n_in] + refs[n_in + nci:n_in + nci + n_out] + refs[n_in + nci + n_out + nco:-2]
        crefs, co = refs[n_in:n_in + nci], refs[n_in + nci + n_out:n_in + nci + n_out + nco]
        send_sems, recv_sems = refs[-2:]
        fresh = []
        for j, o in enumerate(couts):
            if isinstance(o, int):
                crefs[o] = co[j]
            else:
                fresh.append(co[j])

        def start():
            for cp in build(crefs, fresh, send_sems, recv_sems)[0]:
                cp.start()

        def finish():
            sends, recvs = build(crefs, fresh, send_sems, recv_sems)
            for cp in recvs:
                cp.wait_recv()
            for cp in sends:
                cp.wait_send()

        if ng == 0:
            start()
            finish()
            return
        ids = [pl.program_id(d) for d in range(ng)]
        first = functools.reduce(jnp.logical_and, [ids[d] == 0 for d in range(ng)])
        last = functools.reduce(jnp.logical_and, [ids[d] == grid[d] - 1 for d in range(ng)])
        pl.when(first)(start)
        body(*main)
        pl.when(last)(finish)

    res = pl.pallas_call(
        hosted, grid=grid, in_specs=list(in_specs) + [hbm] * nci, out_specs=list(out_specs) + [hbm] * nco,
        out_shape=list(out_shape) + cshapes,
        scratch_shapes=list(scratch) + [pltpu.SemaphoreType.DMA((n_sem,)), pltpu.SemaphoreType.DMA((n_sem,))],
        input_output_aliases=aliases,
        compiler_params=pltpu.CompilerParams(dimension_semantics=("arbitrary",) * ng if ng else None,
                                             vmem_limit_bytes=VMEM_LIMIT_BYTES, has_side_effects=True),
        name=name)(*ops, *cins)
    return res[:n_out], res[n_out:]


def _mm(pairs, mode, m, n, k, out_dtypes, *, name, tm=512, tn=512, tk=512, add=None, a_off=0, b_off=0, comm=None):
    tm, tn, tk = min(tm, m), min(tn, n), min(tk, k)
    assert m % tm == 0 and n % tn == 0 and k % tk == 0 and b_off % tn == 0, (name, m, n, k, tm, tn, tk)
    nk, npair, jo = k // tk, len(pairs), b_off // tn
    if mode == "nn":
        ao = a_off // tk
        ab, ai, bb, bi = (tm, tk), (lambda i, j, kk: (i, kk + ao)), (tk, tn), (lambda i, j, kk: (kk, j))
        dims = (((1,), (0,)), ((), ()))
    elif mode == "nt":
        ao = a_off // tk
        ab, ai, bb, bi = (tm, tk), (lambda i, j, kk: (i, kk + ao)), (tn, tk), (lambda i, j, kk: (j + jo, kk))
        dims = (((1,), (1,)), ((), ()))
    else:
        ao = a_off // tm
        ab, ai, bb, bi = (tk, tm), (lambda i, j, kk: (kk, i + ao)), (tk, tn), (lambda i, j, kk: (kk, j))
        dims = (((0,), (0,)), ((), ()))
    assert a_off % (tm if mode == "tn" else tk) == 0, (name, a_off)
    ops, specs = [], []
    for a, b in pairs:
        for o, blk, im in ((a, ab, ai), (b, bb, bi)):
            arr, sp = _op(o, blk, im)
            ops.append(arr)
            specs.append(sp)
    has_add = add is not None
    if has_add:
        ops.append(add)
        specs.append(pl.BlockSpec((tm, tn), lambda i, j, kk: (i, j)))
    nout = len(out_dtypes)

    def body(*refs):
        outs, acc = refs[2 * npair + has_add:2 * npair + has_add + nout], refs[-1]
        kk = pl.program_id(2)
        t = None
        for p in range(npair):
            d = lax.dot_general(refs[2 * p][...].astype(BF16), refs[2 * p + 1][...].astype(BF16), dims,
                                preferred_element_type=F32)
            t = d if t is None else t + d

        def finish(r):
            if has_add:
                r = r + refs[2 * npair][...]
            for o in outs:
                o[...] = r.astype(o.dtype)

        if nk == 1:
            finish(t)
            return

        @pl.when(kk == 0)
        def _():
            acc[...] = jnp.zeros_like(acc)

        acc[...] += t

        @pl.when(kk == nk - 1)
        def _():
            finish(acc[...])

    res, couts = _call(
        body, grid=(m // tm, n // tn, nk), in_specs=specs,
        out_specs=[pl.BlockSpec((tm, tn), lambda i, j, kk: (i, j))] * nout,
        out_shape=[jax.ShapeDtypeStruct((m, n), dt) for dt in out_dtypes],
        scratch=[pltpu.VMEM((tm, tn), F32)], sem=("parallel", "parallel", "arbitrary"), name=name, ops=ops, comm=comm)
    res = res[0] if nout == 1 else res
    return res if comm is None else (res, couts)


def _rows(width, col=0, tm=None):
    return pl.BlockSpec((tm, width), lambda i: (i, col))


def _whole(shape):
    nd = len(shape)
    return pl.BlockSpec(shape, lambda i: (0,) * nd)


def _sigmoid(x):
    return 0.5 * jnp.tanh(0.5 * x) + 0.5


def _mm_ln(a, w, x, g, b, s, alpha, name, tk, comm=None):
    S, K = a.shape
    Dm = x.shape[1]
    tm, tk = min(1024, S), min(tk, K)
    nk = K // tk

    def body(a_ref, w_ref, x_ref, g_ref, b_ref, y_ref, yb_ref, xh_ref, rs_ref, acc):
        kk = pl.program_id(1)
        d = jnp.dot(a_ref[...], w_ref[...], preferred_element_type=F32)

        if nk > 1:
            @pl.when(kk == 0)
            def _():
                acc[...] = d

            @pl.when((kk > 0) & (kk < nk - 1))
            def _():
                acc[...] += d

        @pl.when(kk == nk - 1)
        def _():
            z = alpha * x_ref[...] + s * (d if nk == 1 else acc[...] + d)
            mu = jnp.mean(z, axis=-1, keepdims=True)
            zc = z - mu
            var = jnp.mean(zc * zc, axis=-1, keepdims=True)
            rstd = lax.rsqrt(var + LN_EPS)
            xh = zc * rstd
            y = xh * g_ref[...] + b_ref[...]
            y_ref[...] = y
            yb_ref[...] = y.astype(BF16)
            xh_ref[...] = xh
            rs_ref[...] = rstd

    wa, ws = _op(w, (tk, Dm), lambda i, kk: (kk, 0))
    row = pl.BlockSpec((tm, Dm), lambda i, kk: (i, 0))
    vec = pl.BlockSpec((1, Dm), lambda i, kk: (0, 0))
    res, couts = _call(
        body, grid=(S // tm, nk),
        in_specs=[pl.BlockSpec((tm, tk), lambda i, kk: (i, kk)), ws, row, vec, vec],
        out_specs=[row, row, row, pl.BlockSpec((tm, 1), lambda i, kk: (i, 0))],
        out_shape=[jax.ShapeDtypeStruct((S, Dm), F32), jax.ShapeDtypeStruct((S, Dm), BF16),
                   jax.ShapeDtypeStruct((S, Dm), F32), jax.ShapeDtypeStruct((S, 1), F32)],
        scratch=[pltpu.VMEM((tm, Dm), F32)], sem=("parallel", "arbitrary"), name=name, ops=[a, wa, x, g, b],
        comm=comm)
    return res if comm is None else (res, couts)


def _ln_bwd(dy, xh, rs, g, s, alpha, name):
    S, Dm = dy.shape
    tm = min(512, S)

    def body(dy_ref, xh_ref, rs_ref, g_ref, dres_ref, dbr_ref, dg_ref, db_ref):
        @pl.when(pl.program_id(0) == 0)
        def _():
            dg_ref[...] = jnp.zeros_like(dg_ref)
            db_ref[...] = jnp.zeros_like(db_ref)

        dy, xh = dy_ref[...], xh_ref[...]
        dyg = dy * g_ref[...]
        m1 = jnp.mean(dyg, axis=-1, keepdims=True)
        m2 = jnp.mean(dyg * xh, axis=-1, keepdims=True)
        dz = rs_ref[...] * (dyg - m1 - xh * m2)
        dres_ref[...] = alpha * dz
        dbr_ref[...] = (s * dz).astype(BF16)
        dg_ref[...] += jnp.sum(dy * xh, axis=0, keepdims=True)
        db_ref[...] += jnp.sum(dy, axis=0, keepdims=True)

    return pl.pallas_call(
        body, grid=(S // tm,),
        in_specs=[_rows(Dm, tm=tm), _rows(Dm, tm=tm), _rows(1, tm=tm), _whole((1, Dm))],
        out_specs=[_rows(Dm, tm=tm), _rows(Dm, tm=tm), _whole((1, Dm)), _whole((1, Dm))],
        out_shape=[jax.ShapeDtypeStruct((S, Dm), F32), jax.ShapeDtypeStruct((S, Dm), BF16),
                   jax.ShapeDtypeStruct((1, Dm), F32), jax.ShapeDtypeStruct((1, Dm), F32)],
        compiler_params=_params(("arbitrary",)), name=name)(dy, xh, rs, g)


def _ffn_up(x, wg, wu, dff, name, comm=None):
    S, Dm = x.shape
    tm, tn = min(512, S), dff // 2

    def body(x_ref, wg_ref, wu_ref, a_ref, b_ref, h_ref):
        xb = x_ref[...]
        dims = (((1,), (1,)), ((), ()))
        a = lax.dot_general(xb, wg_ref[...], dims, preferred_element_type=F32)
        b = lax.dot_general(xb, wu_ref[...], dims, preferred_element_type=F32)
        a_ref[...] = a.astype(BF16)
        b_ref[...] = b.astype(BF16)
        h_ref[...] = (a * _sigmoid(a) * b).astype(BF16)

    wga, wgs = _op(wg, (tn, Dm), lambda i, j: (j, 0))
    wua, wus = _op(wu, (tn, Dm), lambda i, j: (j, 0))
    ob = pl.BlockSpec((tm, tn), lambda i, j: (i, j))
    res, couts = _call(
        body, grid=(S // tm, dff // tn),
        in_specs=[pl.BlockSpec((tm, Dm), lambda i, j: (i, 0)), wgs, wus], out_specs=[ob, ob, ob],
        out_shape=[jax.ShapeDtypeStruct((S, dff), BF16)] * 3, sem=("parallel", "parallel"), name=name,
        ops=[x, wga, wua], comm=comm)
    return res if comm is None else (res, couts)


def _ffn_dh(df, wd, a, b, name):
    S, Dm = df.shape
    dff = a.shape[1]
    tm, tn = min(512, S), dff // 2

    def body(df_ref, wd_ref, a_ref, b_ref, da_ref, db_ref):
        dh = lax.dot_general(df_ref[...], wd_ref[...], (((1,), (1,)), ((), ())), preferred_element_type=F32)
        a, b = a_ref[...].astype(F32), b_ref[...].astype(F32)
        sg = _sigmoid(a)
        da_ref[...] = (dh * b * (sg * (1.0 + a * (1.0 - sg)))).astype(BF16)
        db_ref[...] = (dh * (a * sg)).astype(BF16)

    wda, wds = _op(wd, (tn, Dm), lambda i, j: (j, 0))
    ob = pl.BlockSpec((tm, tn), lambda i, j: (i, j))
    return pl.pallas_call(
        body, grid=(S // tm, dff // tn),
        in_specs=[pl.BlockSpec((tm, Dm), lambda i, j: (i, 0)), wds, ob, ob], out_specs=[ob, ob],
        out_shape=[jax.ShapeDtypeStruct((S, dff), BF16), jax.ShapeDtypeStruct((S, dff), BF16)],
        compiler_params=_params(("parallel", "parallel")), name=name)(df, wda, a, b)


def _halo_prev(width, col, tm):
    return pl.BlockSpec((8, width), lambda i: (jnp.maximum(i * (tm // 8) - 1, 0), col))


def _halo_next(width, col, tm, S):
    return pl.BlockSpec((8, width), lambda i: (jnp.minimum((i + 1) * (tm // 8), S // 8 - 1), col))


def _shift_down(prev8, cur, n):
    ext = jnp.concatenate([prev8, cur], axis=0)
    return pltpu.roll(ext, n, axis=0)[8:]


def _shift_up(cur, next8, n):
    ext = jnp.concatenate([cur, next8], axis=0)
    return pltpu.roll(ext, ext.shape[0] - n, axis=0)[:cur.shape[0]]


def _conv_fwd(proj_a, conv_w, conv_b, name):
    S = proj_a.shape[0]
    tm = min(512, S)
    C = D_CONV

    def body(bg_ref, cg_ref, h_ref, cgp_ref, hp_ref, w_ref, cb_ref, out_ref):
        z = cg_ref[...] * h_ref[...]
        zp = jnp.where(pl.program_id(0) > 0, cgp_ref[...] * hp_ref[...], 0.0)
        w = w_ref[...]
        y = w[2:3] * z + w[1:2] * _shift_down(zp, z, 1) + w[0:1] * _shift_down(zp, z, 2) + cb_ref[...]
        out_ref[...] = (bg_ref[...] * y).astype(BF16)

    return pl.pallas_call(
        body, grid=(S // tm,),
        in_specs=[_rows(C, 0, tm), _rows(C, 1, tm), _rows(C, 2, tm), _halo_prev(C, 1, tm), _halo_prev(C, 2, tm),
                  _whole((3, C)), _whole((1, C))],
        out_specs=_rows(C, 0, tm), out_shape=jax.ShapeDtypeStruct((S, C), BF16),
        compiler_params=_params(("parallel",)), name=name)(proj_a, proj_a, proj_a, proj_a, proj_a, conv_w, conv_b)


def _conv_bwd(proj_a, dmc, conv_w, conv_b, name):
    S = proj_a.shape[0]
    tm = min(512, S)
    C = D_CONV
    nblk = S // tm

    def body(bg_ref, cg_ref, h_ref, cgp_ref, hp_ref, bgn_ref, d_ref, dn_ref, w_ref, cb_ref, out_ref, dw_ref,
             dcb_ref):
        i = pl.program_id(0)

        @pl.when(i == 0)
        def _():
            dw_ref[...] = jnp.zeros_like(dw_ref)
            dcb_ref[...] = jnp.zeros_like(dcb_ref)

        bg, cg, h, d = bg_ref[...], cg_ref[...], h_ref[...], d_ref[...]
        z = cg * h
        zp = jnp.where(i > 0, cgp_ref[...] * hp_ref[...], 0.0)
        w = w_ref[...]
        z1, z2 = _shift_down(zp, z, 1), _shift_down(zp, z, 2)
        y = w[2:3] * z + w[1:2] * z1 + w[0:1] * z2 + cb_ref[...]
        dy = d * bg
        dyn = jnp.where(i < nblk - 1, dn_ref[...] * bgn_ref[...], 0.0)
        dz = w[2:3] * dy + w[1:2] * _shift_up(dy, dyn, 1) + w[0:1] * _shift_up(dy, dyn, 2)
        out_ref[:, 0:C] = (d * y).astype(BF16)
        out_ref[:, C:2 * C] = (dz * h).astype(BF16)
        out_ref[:, 2 * C:3 * C] = (dz * cg).astype(BF16)
        dw_ref[0:1, :] += jnp.sum(dy * z2, axis=0, keepdims=True)
        dw_ref[1:2, :] += jnp.sum(dy * z1, axis=0, keepdims=True)
        dw_ref[2:3, :] += jnp.sum(dy * z, axis=0, keepdims=True)
        dcb_ref[...] += jnp.sum(dy, axis=0, keepdims=True)

    return pl.pallas_call(
        body, grid=(nblk,),
        in_specs=[_rows(C, 0, tm), _rows(C, 1, tm), _rows(C, 2, tm), _halo_prev(C, 1, tm), _halo_prev(C, 2, tm),
                  _halo_next(C, 0, tm, S), _rows(C, 0, tm), _halo_next(C, 0, tm, S), _whole((3, C)), _whole((1, C))],
        out_specs=[_rows(3 * C, 0, tm), _whole((3, C)), _whole((1, C))],
        out_shape=[jax.ShapeDtypeStruct((S, 3 * C), BF16), jax.ShapeDtypeStruct((3, C), F32),
                   jax.ShapeDtypeStruct((1, C), F32)],
        compiler_params=_params(("arbitrary",)), name=name)(
            proj_a, proj_a, proj_a, proj_a, proj_a, proj_a, dmc, dmc, conv_w, conv_b)


def _disc_math(lr, li, ldt, br, bi):
    dt = jnp.exp(ldt)
    mag = jnp.exp(lr * dt)
    ang = li * dt
    are = mag * jnp.cos(ang)
    aim = mag * jnp.sin(ang)
    nr = are - 1.0
    den = lr * lr + li * li
    cre = (nr * lr + aim * li) / den
    cim = (aim * lr - nr * li) / den
    return are, aim, cre * br - cim * bi, cre * bi + cim * br


def _disc_fwd(lr, li, ldt, br, bi):
    shapes = [lr.shape, lr.shape, br.shape, br.shape]

    def body(lr_ref, li_ref, ldt_ref, br_ref, bi_ref, *outs):
        for o, v in zip(outs, _disc_math(lr_ref[...], li_ref[...], ldt_ref[...], br_ref[...], bi_ref[...])):
            o[...] = v

    return pl.pallas_call(body, out_shape=[jax.ShapeDtypeStruct(s, F32) for s in shapes],
                          compiler_params=_params(), name="ssm_disc")(lr, li, ldt, br, bi)


def _disc_bwd(lr, li, ldt, br, bi, cts):
    shapes = [lr.shape, lr.shape, ldt.shape, br.shape, br.shape]

    def body(lr_ref, li_ref, ldt_ref, br_ref, bi_ref, c0, c1, c2, c3, *outs):
        _, vjp = jax.vjp(_disc_math, lr_ref[...], li_ref[...], ldt_ref[...], br_ref[...], bi_ref[...])
        for o, v in zip(outs, vjp((c0[...], c1[...], c2[...], c3[...]))):
            o[...] = v

    return pl.pallas_call(body, out_shape=[jax.ShapeDtypeStruct(s, F32) for s in shapes],
                          compiler_params=_params(), name="ssm_disc_bwd")(lr, li, ldt, br, bi, *cts)


def _scan_fwd(bu, abar, name):
    S = bu.shape[0]
    tb = min(256, S)

    def body(bu_ref, a_ref, xs_ref, st_ref):
        @pl.when(pl.program_id(0) == 0)
        def _():
            st_ref[...] = jnp.zeros_like(st_ref)

        ar, ai = a_ref[0:8, :], a_ref[8:16, :]
        a2r, a2i = ar * ar - ai * ai, 2.0 * ar * ai

        def step(s, c):
            xr, xi = c
            t = 2 * s
            b0r, b0i, b1r, b1i = bu_ref[t, 0:8, :], bu_ref[t, 8:16, :], bu_ref[t + 1, 0:8, :], bu_ref[t + 1, 8:16, :]
            cr = ar * b0r - ai * b0i + b1r
            ci = ar * b0i + ai * b0r + b1i
            xs_ref[t, 0:8, :] = ar * xr - ai * xi + b0r
            xs_ref[t, 8:16, :] = ar * xi + ai * xr + b0i
            nr = a2r * xr - a2i * xi + cr
            ni = a2r * xi + a2i * xr + ci
            xs_ref[t + 1, 0:8, :] = nr
            xs_ref[t + 1, 8:16, :] = ni
            return nr, ni

        xr, xi = lax.fori_loop(0, tb // 2, step, (st_ref[0:8, :], st_ref[8:16, :]), unroll=4)
        st_ref[0:8, :] = xr
        st_ref[8:16, :] = xi

    blk = pl.BlockSpec((tb, 16, 128), lambda i: (i, 0, 0))
    return pl.pallas_call(
        body, grid=(S // tb,), in_specs=[blk, _whole((16, 128))], out_specs=blk,
        out_shape=jax.ShapeDtypeStruct((S, 16, 128), F32), scratch_shapes=[pltpu.VMEM((16, 128), F32)],
        compiler_params=_params(("arbitrary",)), name=name)(bu, abar)


def _scan_bwd(dxs, xs, abar, name):
    S = dxs.shape[0]
    tb = min(256, S)
    nblk = S // tb

    def body(d_ref, x_ref, xp_ref, a_ref, lam_ref, da_ref, st_ref):
        i = pl.program_id(0)

        @pl.when(i == 0)
        def _():
            st_ref[...] = jnp.zeros_like(st_ref)
            da_ref[...] = jnp.zeros_like(da_ref)

        ar, ai = a_ref[0:8, :], a_ref[8:16, :]
        a2r, a2i = ar * ar - ai * ai, 2.0 * ar * ai

        def pair(t, c, p1r, p1i, p0r, p0i):
            lr, li, gr, gi = c
            d1r, d1i, d0r, d0i = d_ref[t, 0:8, :], d_ref[t, 8:16, :], d_ref[t - 1, 0:8, :], d_ref[t - 1, 8:16, :]
            n1r = d1r + ar * lr + ai * li
            n1i = d1i - ai * lr + ar * li
            cr = d0r + ar * d1r + ai * d1i
            ci = d0i - ai * d1r + ar * d1i
            n0r = cr + a2r * lr + a2i * li
            n0i = ci - a2i * lr + a2r * li
            lam_ref[t, 0:8, :] = n1r
            lam_ref[t, 8:16, :] = n1i
            lam_ref[t - 1, 0:8, :] = n0r
            lam_ref[t - 1, 8:16, :] = n0i
            gr = gr + ((n1r * p1r + n1i * p1i) + (n0r * p0r + n0i * p0i))
            gi = gi + ((n1i * p1r - n1r * p1i) + (n0i * p0r - n0r * p0i))
            return n0r, n0i, gr, gi

        def step(s, c):
            t = tb - 1 - 2 * s
            return pair(t, c, x_ref[t - 1, 0:8, :], x_ref[t - 1, 8:16, :], x_ref[t - 2, 0:8, :], x_ref[t - 2, 8:16, :])

        c = (st_ref[0:8, :], st_ref[8:16, :], jnp.zeros((8, 128), F32), jnp.zeros((8, 128), F32))
        c = lax.fori_loop(0, tb // 2 - 1, step, c, unroll=4)
        first = i == nblk - 1
        pr = jnp.where(first, 0.0, xp_ref[0, 0:8, :])
        pi = jnp.where(first, 0.0, xp_ref[0, 8:16, :])
        lr, li, gr, gi = pair(1, c, x_ref[0, 0:8, :], x_ref[0, 8:16, :], pr, pi)
        st_ref[0:8, :] = lr
        st_ref[8:16, :] = li
        da_ref[0:8, :] += gr
        da_ref[8:16, :] += gi

    blk = pl.BlockSpec((tb, 16, 128), lambda i: (nblk - 1 - i, 0, 0))
    prev = pl.BlockSpec((1, 16, 128), lambda i: (jnp.maximum((nblk - 1 - i) * tb - 1, 0), 0, 0))
    return pl.pallas_call(
        body, grid=(nblk,), in_specs=[blk, blk, prev, _whole((16, 128))], out_specs=[blk, _whole((16, 128))],
        out_shape=[jax.ShapeDtypeStruct((S, 16, 128), F32), jax.ShapeDtypeStruct((16, 128), F32)],
        scratch_shapes=[pltpu.VMEM((16, 128), F32)],
        compiler_params=_params(("arbitrary",)), name=name)(dxs, xs, xs, abar)


def _gelu(x):
    return 0.5 * x * (1.0 + jnp.tanh(0.7978845608028654 * (x + 0.044715 * x * x * x)))


def _gelu_grad(x):
    t = jnp.tanh(0.7978845608028654 * (x + 0.044715 * x * x * x))
    return 0.5 * (1.0 + t) + 0.5 * x * (1.0 - t * t) * 0.7978845608028654 * (1.0 + 3.0 * 0.044715 * x * x)


def _ssm_out(xs, proj_a, cfull, dskip, wglu, name):
    S = xs.shape[0]
    tm = min(512, S)
    C = D_SSM

    def body(xs_ref, u_ref, c_ref, d_ref, wg_ref, y1_ref, ms_ref):
        y1 = jnp.dot(xs_ref[...].astype(BF16), c_ref[...], preferred_element_type=F32) + d_ref[...] * u_ref[...]
        y2 = _gelu(y1)
        gl = jnp.dot(y2.astype(BF16), wg_ref[...], preferred_element_type=F32)
        y1_ref[...] = y1
        ms_ref[...] = (y2 * _sigmoid(gl)).astype(BF16)

    wga, wgs = _op(wglu, (C, C), lambda i: (0, 0))
    return pl.pallas_call(
        body, grid=(S // tm,),
        in_specs=[_rows(2 * 1024, 0, tm), _rows(C, 3, tm), _whole((2 * 1024, C)), _whole((1, C)), wgs],
        out_specs=[_rows(C, 0, tm), _rows(C, 0, tm)],
        out_shape=[jax.ShapeDtypeStruct((S, C), F32), jax.ShapeDtypeStruct((S, C), BF16)],
        compiler_params=_params(("parallel",)), name=name)(xs, proj_a, cfull, dskip, wga)


def _ssm_out_bwd(dms, y1, proj_a, cfull, dskip, wglu, name):
    S = y1.shape[0]
    tm = min(512, S)
    C = D_SSM

    def body(dms_ref, y1_ref, u_ref, c_ref, d_ref, wg_ref, dy1_ref, y2_ref, dgl_ref, dxs_ref, du_ref, dd_ref):
        @pl.when(pl.program_id(0) == 0)
        def _():
            dd_ref[...] = jnp.zeros_like(dd_ref)

        dms, y1 = dms_ref[...], y1_ref[...]
        y2 = _gelu(y1)
        y2b = y2.astype(BF16)
        sg = _sigmoid(jnp.dot(y2b, wg_ref[...], preferred_element_type=F32))
        dgl = (dms * y2 * sg * (1.0 - sg)).astype(BF16)
        dy2 = dms * sg + lax.dot_general(dgl, wg_ref[...], (((1,), (1,)), ((), ())), preferred_element_type=F32)
        dy1 = dy2 * _gelu_grad(y1)
        dy1b = dy1.astype(BF16)
        dy1_ref[...] = dy1b
        y2_ref[...] = y2b
        dgl_ref[...] = dgl
        dxs_ref[...] = lax.dot_general(dy1b, c_ref[...], (((1,), (1,)), ((), ())), preferred_element_type=F32)
        du_ref[...] = d_ref[...] * dy1
        dd_ref[...] += jnp.sum(dy1 * u_ref[...], axis=0, keepdims=True)

    wga, wgs = _op(wglu, (C, C), lambda i: (0, 0))
    rc = _rows(C, 0, tm)
    return pl.pallas_call(
        body, grid=(S // tm,),
        in_specs=[rc, rc, _rows(C, 3, tm), _whole((2 * 1024, C)), _whole((1, C)), wgs],
        out_specs=[rc, rc, rc, _rows(2 * 1024, 0, tm), rc, _whole((1, C))],
        out_shape=[jax.ShapeDtypeStruct((S, C), BF16), jax.ShapeDtypeStruct((S, C), BF16),
                   jax.ShapeDtypeStruct((S, C), BF16), jax.ShapeDtypeStruct((S, 2 * 1024), F32),
                   jax.ShapeDtypeStruct((S, C), F32), jax.ShapeDtypeStruct((1, C), F32)],
        compiler_params=_params(("arbitrary",)), name=name)(dms, y1, proj_a, cfull, dskip, wga)


_NT = (((1,), (1,)), ((), ()))
_TN = (((0,), (0,)), ((), ()))


def _attn_heads(q_ref, kp_ref, kc_ref, vp_ref, vc_ref):
    lane = lax.broadcasted_iota(jnp.int32, (BLOCK, 128), 1)
    kk = jnp.concatenate([kp_ref[...], kc_ref[...]], axis=0).astype(BF16)
    vv = jnp.concatenate([vp_ref[...], vc_ref[...]], axis=0).astype(BF16)
    kk_r, vv_r = pltpu.roll(kk, 64, axis=1), pltpu.roll(vv, 64, axis=1)
    heads = []
    for hq in range(N_Q_HEADS):
        j, e = hq // 2, hq % 2
        qj = (q_ref[:, 128 * j:128 * (j + 1)] * (HEAD_DIM ** -0.5)).astype(BF16)
        own = (lane >= 64) if e else (lane < 64)
        aligned = e == hq // 4
        heads.append((own, jnp.where(own, qj, jnp.zeros_like(qj)), kk if aligned else kk_r, vv if aligned else vv_r,
                      aligned))
    return heads


def _attn_probs(i, heads, s_ref):
    n = N_Q_HEADS * BLOCK
    s = jnp.concatenate([lax.dot_general(qm, ks, _NT, preferred_element_type=F32) for _, qm, ks, _, _ in heads],
                        axis=0)
    row = lax.broadcasted_iota(jnp.int32, (n, 2 * BLOCK), 0) & (BLOCK - 1)
    col = lax.broadcasted_iota(jnp.int32, (n, 2 * BLOCK), 1)
    mask = (col > row) & (col <= row + BLOCK) & ((col >= BLOCK) | (i > 0))
    s = jnp.where(mask, s, -1e30)
    sink = jnp.concatenate([jnp.full((BLOCK, 1), s_ref[0, hq], F32) for hq in range(N_Q_HEADS)], axis=0)
    m = jnp.maximum(jnp.max(s, axis=1, keepdims=True), sink)
    p = jnp.exp(s - m)
    es = jnp.exp(sink - m)
    inv = 1.0 / (jnp.sum(p, axis=1, keepdims=True) + es)
    return p * inv, es * inv


def _attn_fwd(proj_a, sinks, name):
    S = proj_a.shape[0]
    nb = S // BLOCK

    def body(q_ref, kp_ref, kc_ref, vp_ref, vc_ref, s_ref, out_ref):
        i = pl.program_id(0)
        heads = _attn_heads(q_ref, kp_ref, kc_ref, vp_ref, vc_ref)
        row = lax.broadcasted_iota(jnp.int32, (BLOCK, 2 * BLOCK), 0)
        col = lax.broadcasted_iota(jnp.int32, (BLOCK, 2 * BLOCK), 1)
        mask = (col > row) & (col <= row + BLOCK) & ((col >= BLOCK) | (i > 0))
        outs = []
        for hq, (_, qm, ks, vs, _) in enumerate(heads):
            s = jnp.where(mask, lax.dot_general(qm, ks, _NT, preferred_element_type=F32), -1e30)
            sink = s_ref[0, hq]
            m = jnp.maximum(jnp.max(s, axis=1, keepdims=True), sink)
            p = jnp.exp(s - m)
            inv = 1.0 / (jnp.sum(p, axis=1, keepdims=True) + jnp.exp(sink - m))
            outs.append(jnp.dot((p * inv).astype(BF16), vs, preferred_element_type=F32))
        for j in range(4):
            out_ref[:, 128 * j:128 * (j + 1)] = jnp.where(heads[2 * j][0], outs[2 * j], outs[2 * j + 1]).astype(BF16)

    prev = lambda c: pl.BlockSpec((BLOCK, 128), lambda i: (jnp.maximum(i - 1, 0), c))
    cur = lambda c: pl.BlockSpec((BLOCK, 128), lambda i: (i, c))
    return pl.pallas_call(
        body, grid=(nb,),
        in_specs=[_rows(D_ATTN, 2, BLOCK), prev(12), cur(12), prev(13), cur(13),
                  pl.BlockSpec(memory_space=pltpu.SMEM)],
        out_specs=_rows(D_ATTN, 0, BLOCK), out_shape=jax.ShapeDtypeStruct((S, D_ATTN), BF16),
        compiler_params=_params(("parallel",)), name=name)(proj_a, proj_a, proj_a, proj_a, proj_a, sinks)


def _attn_bwd(proj_a, dout, sinks, name):
    S = proj_a.shape[0]
    nb = S // BLOCK

    def body(q_ref, kp_ref, kc_ref, vp_ref, vc_ref, do_ref, s_ref, out_ref, dk_ref, dv_ref, ds_ref, ck_ref, cv_ref):
        i = pl.program_id(0)

        @pl.when(i == 0)
        def _():
            ds_ref[...] = jnp.zeros_like(ds_ref)
            ck_ref[...] = jnp.zeros_like(ck_ref)
            cv_ref[...] = jnp.zeros_like(cv_ref)

        @pl.when(i < nb)
        def _():
            heads = _attn_heads(q_ref, kp_ref, kc_ref, vp_ref, vc_ref)
            pn, psink = _attn_probs(i, heads, s_ref)
            doms = []
            for hq, (own, _, _, _, _) in enumerate(heads):
                doj = do_ref[:, 128 * (hq // 2):128 * (hq // 2 + 1)].astype(BF16)
                doms.append(jnp.where(own, doj, jnp.zeros_like(doj)))
            dp = jnp.concatenate([lax.dot_general(dom, vs, _NT, preferred_element_type=F32)
                                  for dom, (_, _, _, vs, _) in zip(doms, heads)], axis=0)
            delta = jnp.sum(pn * dp, axis=1, keepdims=True)
            dsb = (pn * (dp - delta)).astype(BF16)
            pnb = pn.astype(BF16)
            sk = psink * delta
            dkk = jnp.zeros((2 * BLOCK, 128), F32)
            dvv = jnp.zeros((2 * BLOCK, 128), F32)
            dqs = []
            for hq, (own, qm, ks, vs, aligned) in enumerate(heads):
                rows = slice(BLOCK * hq, BLOCK * (hq + 1))
                ds_ref[hq:hq + 1, :] += jnp.broadcast_to(-jnp.sum(sk[rows]), (1, 128))
                dqs.append(jnp.dot(dsb[rows], ks, preferred_element_type=F32) * (HEAD_DIM ** -0.5))
                dk = lax.dot_general(dsb[rows], qm, _TN, preferred_element_type=F32)
                dv = lax.dot_general(pnb[rows], doms[hq], _TN, preferred_element_type=F32)
                dkk = dkk + (dk if aligned else pltpu.roll(dk, 64, axis=1))
                dvv = dvv + (dv if aligned else pltpu.roll(dv, 64, axis=1))
            for j in range(4):
                out_ref[:, 128 * j:128 * (j + 1)] = jnp.where(heads[2 * j][0], dqs[2 * j], dqs[2 * j + 1]).astype(BF16)
            ck_ref[0:BLOCK, :] = ck_ref[BLOCK:, :] + dkk[0:BLOCK]
            cv_ref[0:BLOCK, :] = cv_ref[BLOCK:, :] + dvv[0:BLOCK]
            ck_ref[BLOCK:, :] = dkk[BLOCK:]
            cv_ref[BLOCK:, :] = dvv[BLOCK:]

        @pl.when(i == nb)
        def _():
            ck_ref[0:BLOCK, :] = ck_ref[BLOCK:, :]
            cv_ref[0:BLOCK, :] = cv_ref[BLOCK:, :]

        dk_ref[...] = ck_ref[0:BLOCK, :].astype(BF16)
        dv_ref[...] = cv_ref[0:BLOCK, :].astype(BF16)

    last = nb - 1
    prev = lambda c: pl.BlockSpec((BLOCK, 128), lambda i: (jnp.clip(i - 1, 0, last), c))
    cur = lambda c: pl.BlockSpec((BLOCK, 128), lambda i: (jnp.minimum(i, last), c))
    qrow = lambda w, c: pl.BlockSpec((BLOCK, w), lambda i: (jnp.minimum(i, last), c))

    dq, dk, dv, ds = pl.pallas_call(
        body, grid=(nb + 1,),
        in_specs=[qrow(D_ATTN, 2), prev(12), cur(12), prev(13), cur(13), qrow(D_ATTN, 0),
                  pl.BlockSpec(memory_space=pltpu.SMEM)],
        out_specs=[qrow(D_ATTN, 0), prev(0), prev(0), _whole((N_Q_HEADS, 128))],
        out_shape=[jax.ShapeDtypeStruct((S, D_ATTN), BF16), jax.ShapeDtypeStruct((S, D_KV), BF16),
                   jax.ShapeDtypeStruct((S, D_KV), BF16), jax.ShapeDtypeStruct((N_Q_HEADS, 128), F32)],
        scratch_shapes=[pltpu.VMEM((2 * BLOCK, 128), F32), pltpu.VMEM((2 * BLOCK, 128), F32)],
        compiler_params=_params(("arbitrary",)), name=name)(proj_a, proj_a, proj_a, proj_a, proj_a, dout, sinks)
    return dq, dk, dv, ds


_BR = ((0, D_CONV), (D_CONV, D_CONV + D_SSM), (D_CONV + D_SSM, D_CONV + D_SSM + D_ATTN))


def _branches(m_refs, wbr_ref):
    nt = (((1,), (1,)), ((), ()))
    return [lax.dot_general(m[...], wbr_ref[:, lo:hi], nt, preferred_element_type=F32)
            for m, (lo, hi) in zip(m_refs, _BR)]


def _merge_fwd(mc, ms, ma, proj_g, wbr_t, name):
    S, Dm = mc.shape[0], proj_g.shape[1] // 3
    tm = min(512, S)

    def body(mc_ref, ms_ref, ma_ref, g_ref, w_ref, out_ref):
        ys = _branches((mc_ref, ms_ref, ma_ref), w_ref)
        acc = None
        for b in range(3):
            t = _sigmoid(g_ref[:, b * Dm:(b + 1) * Dm].astype(F32)) * ys[b]
            acc = t if acc is None else acc + t
        out_ref[...] = acc.astype(BF16)

    wa, ws = _op(wbr_t, (Dm, Dm), lambda i: (0, 0))
    return pl.pallas_call(
        body, grid=(S // tm,),
        in_specs=[_rows(D_CONV, 0, tm), _rows(D_SSM, 0, tm), _rows(D_ATTN, 0, tm), _rows(3 * Dm, 0, tm), ws],
        out_specs=_rows(Dm, 0, tm), out_shape=jax.ShapeDtypeStruct((S, Dm), BF16),
        compiler_params=_params(("parallel",)), name=name)(mc, ms, ma, proj_g, wa)


def _merge_bwd(dmerged, mc, ms, ma, proj_g, wbr_t, name):
    S, Dm = mc.shape[0], proj_g.shape[1] // 3
    tm = min(256, S)

    def body(d_ref, mc_ref, ms_ref, ma_ref, g_ref, w_ref, dg_ref, dy_ref, dmc_ref, dms_ref, dma_ref):
        ys = _branches((mc_ref, ms_ref, ma_ref), w_ref)
        d = d_ref[...]
        for b, (o_ref, (lo, hi)) in enumerate(zip((dmc_ref, dms_ref, dma_ref), _BR)):
            g = _sigmoid(g_ref[:, b * Dm:(b + 1) * Dm].astype(F32))
            dg_ref[:, b * Dm:(b + 1) * Dm] = (d * ys[b] * g * (1.0 - g)).astype(BF16)
            dyb = (g * d).astype(BF16)
            dy_ref[:, b * Dm:(b + 1) * Dm] = dyb
            o_ref[...] = jnp.dot(dyb, w_ref[:, lo:hi], preferred_element_type=F32)

    wa, ws = _op(wbr_t, (Dm, Dm), lambda i: (0, 0))
    return pl.pallas_call(
        body, grid=(S // tm,),
        in_specs=[_rows(Dm, 0, tm), _rows(D_CONV, 0, tm), _rows(D_SSM, 0, tm), _rows(D_ATTN, 0, tm),
                  _rows(3 * Dm, 0, tm), ws],
        out_specs=[_rows(3 * Dm, 0, tm), _rows(3 * Dm, 0, tm), _rows(D_CONV, 0, tm), _rows(D_SSM, 0, tm),
                   _rows(D_ATTN, 0, tm)],
        out_shape=[jax.ShapeDtypeStruct((S, 3 * Dm), BF16), jax.ShapeDtypeStruct((S, 3 * Dm), BF16),
                   jax.ShapeDtypeStruct((S, D_CONV), F32), jax.ShapeDtypeStruct((S, D_SSM), F32),
                   jax.ShapeDtypeStruct((S, D_ATTN), F32)],
        compiler_params=_params(("parallel",)), name=name)(dmerged, mc, ms, ma, proj_g, wa)


def _loss_head(y, target):
    S, Dm = y.shape
    tm = min(512, S)

    def body(y_ref, t_ref, dy_ref, l_ref):
        @pl.when(pl.program_id(0) == 0)
        def _():
            l_ref[...] = jnp.zeros_like(l_ref)

        e = y_ref[...] - t_ref[...]
        dy_ref[...] = e * (1.0 / Dm)
        l_ref[...] += jnp.broadcast_to(0.5 * jnp.sum(jnp.sum(e * e, axis=1, keepdims=True) * (1.0 / Dm)), (1, 128))

    return pl.pallas_call(
        body, grid=(S // tm,), in_specs=[_rows(Dm, 0, tm), _rows(Dm, 0, tm)],
        out_specs=[_rows(Dm, 0, tm), _whole((1, 128))],
        out_shape=[jax.ShapeDtypeStruct((S, Dm), F32), jax.ShapeDtypeStruct((1, 128), F32)],
        compiler_params=_params(("arbitrary",)), name="loss_head")(y, target)


def _view2d(shape):
    n = math.prod(shape)
    if shape[-1] % 128 == 0:
        return (n // shape[-1], shape[-1])
    if n >= (1 << 16) and len(shape) == 3:
        return (shape[0] * shape[1], shape[2])
    if n % 128 == 0:
        return (n // 128, 128)
    return (1, n)


def _adamw(w, g, m, v, name, blocks_3d=False):
    shape = w.shape
    c1 = 1.0 - ADAM_B1 ** ADAM_STEP
    c2 = 1.0 - ADAM_B2 ** ADAM_STEP

    def body(w_ref, g_ref, m_ref, v_ref, d_ref, nm_ref, nv_ref):
        g = g_ref[...]
        nm = ADAM_B1 * m_ref[...] + (1.0 - ADAM_B1) * g
        nv = ADAM_B2 * v_ref[...] + (1.0 - ADAM_B2) * (g * g)
        d_ref[...] = -ADAM_LR * ((nm / c1) / (jnp.sqrt(nv / c2) + ADAM_EPS) + ADAM_WD * w_ref[...])
        nm_ref[...] = nm
        nv_ref[...] = nv

    if blocks_3d:
        Ls, R, C = shape
        tm = max(t for t in range(8, min(R, 512) + 1, 8) if R % t == 0)
        blk = pl.BlockSpec((None, tm, C), lambda l, i: (l, i, 0))
        return pl.pallas_call(
            body, grid=(Ls, R // tm), in_specs=[blk] * 4, out_specs=[blk] * 3,
            out_shape=[jax.ShapeDtypeStruct(shape, F32)] * 3,
            compiler_params=_params(("parallel", "parallel")), name=name)(w, g, m, v)
    R, C = _view2d(shape)
    tm = R
    for cand in (512, 352, 256):
        if R > cand and R % cand == 0:
            tm = cand
            break
    blk = _rows(C, 0, tm)
    outs = pl.pallas_call(
        body, grid=(R // tm,), in_specs=[blk] * 4, out_specs=[blk] * 3,
        out_shape=[jax.ShapeDtypeStruct((R, C), F32)] * 3,
        compiler_params=_params(("parallel",)), name=name)(*[t.reshape(R, C) for t in (w, g, m, v)])
    return [o.reshape(shape) for o in outs]


def _coords():
    return lax.axis_index("x"), lax.axis_index("y"), lax.axis_index("c")


def _other_chips(x, y):
    return [((1 - x, y), 2 * (1 - x) + y), ((x, 1 - y), 2 * x + 1 - y), ((1 - x, 1 - y), 2 * (1 - x) + 1 - y)]


def _comm_call(comm, name):
    return _call(None, grid=(), in_specs=[], out_specs=[], out_shape=[], name=name, ops=[], comm=comm)[1]


def _remote(src, dst, send_sems, recv_sems, k, dev):
    return pltpu.make_async_remote_copy(src_ref=src, dst_ref=dst, send_sem=send_sems.at[k], recv_sem=recv_sems.at[k],
                                        device_id=dev, device_id_type=MESH)


def _half(ref, chip, c, rp):
    return ref.at[pl.ds(pl.multiple_of(chip * rp + c * (rp // 2), 16), rp // 2), :]


def _gather_ici(shards, fulls, l):
    n = len(shards)

    def build(refs, fresh, send_sems, recv_sems):
        x, y, c = _coords()
        me = 2 * x + y
        sends, recvs = [], []
        for a in range(n):
            rp = shards[a].shape[1]
            src = refs[a].at[l].at[pl.ds(pl.multiple_of(c * (rp // 2), 16), rp // 2), :]
            for r, ((px, py), chip) in enumerate(_other_chips(x, y)):
                k = 3 * a + r
                sends.append(_remote(src, _half(refs[n + a], me, c, rp), send_sems, recv_sems, k, (px, py, c)))
                recvs.append(_remote(src, _half(refs[n + a], chip, c, rp), send_sems, recv_sems, k, (px, py, c)))
        return sends, recvs

    return build, list(shards) + list(fulls), [n + a for a in range(n)], 3 * n


def _gather_d2d(fulls, shards, l):
    n = len(fulls)

    def build(refs, fresh, send_sems, recv_sems):
        x, y, c = _coords()
        me = 2 * x + y
        sends, recvs = [], []
        for a in range(n):
            rp = fulls[a].shape[0] // N_CHIPS
            for r, (_, chip) in enumerate(_other_chips(x, y)):
                k = 4 * a + r
                mine, theirs = _half(refs[a], chip, c, rp), _half(refs[a], chip, 1 - c, rp)
                sends.append(_remote(mine, mine, send_sems, recv_sems, k, (x, y, 1 - c)))
                recvs.append(_remote(theirs, theirs, send_sems, recv_sems, k, (x, y, 1 - c)))
            own = refs[a].at[pl.ds(pl.multiple_of(me * rp, 16), rp), :]
            sends.append(_remote(refs[n + a].at[l], own, send_sems, recv_sems, 4 * a + 3, (x, y, 1 - c)))
            recvs.append(_remote(refs[n + a].at[l], own, send_sems, recv_sems, 4 * a + 3, (x, y, 1 - c)))
        return sends, recvs

    return build, list(fulls) + list(shards), list(range(n)), 4 * n


def _pair_sum(a, b, c_idx, half_rows, out_dtype, name):
    n4, rp, W = a.shape
    tr = half_rows // 2
    nblk = half_rows // tr

    def body(c_ref, a_ref, b_ref, o_ref):
        o_ref[...] = (a_ref[...] + b_ref[...]).astype(o_ref.dtype)

    return pl.pallas_call(
        body,
        grid_spec=pltpu.PrefetchScalarGridSpec(
            num_scalar_prefetch=1, grid=(nblk,),
            in_specs=[pl.BlockSpec((n4, tr, W), lambda i, c: (0, c[0] * nblk + i, 0)),
                      pl.BlockSpec((n4, tr, W), lambda i, c: (0, i, 0))],
            out_specs=pl.BlockSpec((n4, tr, W), lambda i, c: (0, i, 0))),
        out_shape=jax.ShapeDtypeStruct((n4, half_rows, W), out_dtype),
        compiler_params=_params(("parallel",)), name=name)(c_idx, a, b)


def _sum4(land, own, me_c, stacked, l, name):
    _, R, W = land.shape
    tr = R // 2

    def body(s_ref, l_ref, o_ref, stacked_ref, out_ref):
        me = s_ref[0]
        acc = None
        for i in range(N_CHIPS):
            t = jnp.where(me == i, o_ref[...], l_ref[i]).astype(F32)
            acc = t if acc is None else acc + t
        out_ref[...] = acc

    return pl.pallas_call(
        body,
        grid_spec=pltpu.PrefetchScalarGridSpec(
            num_scalar_prefetch=1, grid=(R // tr,),
            in_specs=[pl.BlockSpec((N_CHIPS, tr, W), lambda i, s: (0, i, 0)),
                      pl.BlockSpec((None, tr, W), lambda i, s: (s[0], i, 0)),
                      pl.BlockSpec(memory_space=pltpu.HBM)],
            out_specs=pl.BlockSpec((None, None, tr, W), lambda i, s: (l, s[1], i, 0))),
        out_shape=jax.ShapeDtypeStruct(stacked.shape, F32), input_output_aliases={3: 0},
        compiler_params=_params(("parallel",)), name=name)(me_c, land, own, stacked)


def _reduce_d2d(g4):
    n = len(g4)

    def build(refs, fresh, send_sems, recv_sems):
        x, y, c = _coords()
        sends, recvs = [], []
        for a in range(n):
            hr = g4[a].shape[1] // 2
            src = refs[a].at[:, pl.ds(pl.multiple_of((1 - c) * hr, 8), hr), :]
            sends.append(_remote(src, fresh[a], send_sems, recv_sems, a, (x, y, 1 - c)))
            recvs.append(_remote(src, fresh[a], send_sems, recv_sems, a, (x, y, 1 - c)))
        return sends, recvs

    outs = [jax.ShapeDtypeStruct((N_CHIPS, g.shape[1] // 2, g.shape[2]), F32) for g in g4]
    return build, list(g4), outs, n


def _reduce_ici(q, lands):
    n = len(q)

    def build(refs, fresh, send_sems, recv_sems):
        x, y, c = _coords()
        me = 2 * x + y
        sends, recvs = [], []
        for a in range(n):
            for r, ((px, py), chip) in enumerate(_other_chips(x, y)):
                k = 3 * a + r
                sends.append(_remote(refs[a].at[chip], refs[n + a].at[me], send_sems, recv_sems, k, (px, py, c)))
                recvs.append(_remote(refs[a].at[chip], refs[n + a].at[chip], send_sems, recv_sems, k, (px, py, c)))
        return sends, recvs

    return build, list(q) + list(lands), [n + a for a in range(n)], 3 * n


def _reduce_share(r, l):
    n = len(r)

    def build(refs, fresh, send_sems, recv_sems):
        x, y, c = _coords()
        sends, recvs = [], []
        for a in range(n):
            mine, theirs = refs[a].at[l].at[c], refs[a].at[l].at[1 - c]
            sends.append(_remote(mine, mine, send_sems, recv_sems, a, (x, y, 1 - c)))
            recvs.append(_remote(mine, theirs, send_sems, recv_sems, a, (x, y, 1 - c)))
        return sends, recvs

    return build, list(r), list(range(n)), n


class _SemOffset:
    def __init__(self, sems, off):
        self.sems, self.off = sems, off

    @property
    def at(self):
        return self

    def __getitem__(self, k):
        return self.sems.at[k + self.off]


def _merge(c1, c2):
    b1, i1, o1, s1 = c1
    b2, i2, o2, s2 = c2
    n1, f1 = len(i1), sum(not isinstance(o, int) for o in o1)

    def build(refs, fresh, send_sems, recv_sems):
        sa, ra = b1(refs[:n1], fresh[:f1], send_sems, recv_sems)
        sb, rb = b2(refs[n1:], fresh[f1:], _SemOffset(send_sems, s1), _SemOffset(recv_sems, s1))
        return sa + sb, ra + rb

    return build, list(i1) + list(i2), list(o1) + [o + n1 if isinstance(o, int) else o for o in o2], s1 + s2


class _Weave:
    A, B, C = ("wg1", "wu1", "wd1"), ("win", "wbr", "wout", "wglu"), ("wg2", "wu2", "wd2")
    X, Y = C + B, A
    ICI_X = {"dwg1": ("wg2", "wu2"), "dwu1": ("wd2", "wbr", "wout", "wglu"), "dwd1": ("win",)}
    ICI_Y = {"dwg2": ("wg1", "wu1"), "dwu2": ("wd1",)}

    def __init__(self, shards, W, L):
        self.shards, self.W, self.L = shards, W, L
        x, y, c = _coords()
        self.c_idx = jnp.reshape(c, (1,)).astype(jnp.int32)
        self.me_c = jnp.stack([2 * x + y, c]).astype(jnp.int32)
        self.final = {k: lax.empty((L, 2, s.shape[1] // 2, s.shape[2]), F32) for k, s in shards.items()}
        self.x = self.y = None
        self.parts = []

    def _ici(self, keys, l):
        def cb(outs):
            for k, t in zip(keys, outs):
                self.W[k][l] = t
        return _gather_ici([self.shards[k] for k in keys], [self.W[k][l] for k in keys], l), cb

    def _d2d(self, keys, l):
        def cb(outs):
            for k, t in zip(keys, outs):
                self.W[k][l] = t
        return _gather_d2d([self.W[k][l] for k in keys], [self.shards[k] for k in keys], l), cb

    def _r1(self, st, keys):
        def cb(outs):
            st["q"] = {k: _pair_sum(st["g4"][k], t, self.c_idx, t.shape[1], BF16, "reduce_pair_sum")
                       for k, t in zip(keys, outs)}
        return _reduce_d2d([st["g4"][k] for k in keys]), cb

    def _r2(self, st, keys):
        def cb(outs):
            for k, t in zip(keys, outs):
                self.final[k] = _sum4(t, st["q"][k], self.me_c, self.final[k], st["layer"], "reduce_sum4")
        return _reduce_ici([st["q"][k] for k in keys], [jnp.zeros(st["q"][k].shape, BF16) for k in keys]), cb

    def _r3(self, st, keys):
        def cb(outs):
            for k, t in zip(keys, outs):
                self.final[k] = t
        return _reduce_share([self.final[k] for k in keys], st["layer"]), cb

    def _pieces(self, site, l):
        nxt = l is not None and l + 1 < self.L
        if site == "up1":
            return [self._ici(self.B, l)]
        if site == "down1":
            return [self._d2d(self.B, l), self._ici(self.C, l)]
        if site == "proj_g":
            return [self._d2d(self.C, l)] + ([self._ici(self.A, l + 1)] if nxt else [])
        if site == "up2":
            return [self._d2d(self.A, l + 1)] if nxt else []
        if site == "dx1" and self.x is not None:
            return [self._r1(self.x, self.X)]
        if site in self.ICI_X and self.x is not None:
            return [self._r2(self.x, self.ICI_X[site])]
        if site == "dx2":
            ps = [self._r3(self.x, self.X)] if self.x is not None else []
            return ps + ([self._r1(self.y, self.Y)] if self.y is not None else [])
        if site in self.ICI_Y and self.y is not None:
            return [self._r2(self.y, self.ICI_Y[site])]
        if site == "dwd2" and self.y is not None:
            return [self._r3(self.y, self.Y)]
        return []

    def take(self, site, l):
        self.parts = self._pieces(site, l)
        if not self.parts:
            return None
        comm = self.parts[0][0]
        for c, _ in self.parts[1:]:
            comm = _merge(comm, c)
        return comm

    def done(self, site, l, outs):
        i = 0
        for comm, cb in self.parts:
            cb(outs[i:i + len(comm[2])])
            i += len(comm[2])
        if site == "dx2":
            self.x = None
        if site == "dwd2":
            self.y = None

    def gather_first(self):
        for piece, name in ((self._ici, "gather0_ici"), (self._d2d, "gather0_d2d")):
            comm, cb = piece(self.A, 0)
            cb(_comm_call(comm, name))

    def _state(self, l, grads):
        return dict(layer=l, g4={k: g.reshape(N_CHIPS, g.shape[0] // N_CHIPS, g.shape[1]) for k, g in grads.items()})

    def grads_x(self, l, grads):
        assert self.x is None
        self.x = self._state(l, grads)

    def grads_y(self, l, grads):
        assert self.y is None
        self.y = self._state(l, grads)

    def flush(self):
        for site, name in (("dx2", "reduce_tail_d2d"), ("dwg2", "reduce_tail_ici"), ("dwu2", "reduce_tail_ici"),
                           ("dwd2", "reduce_tail_share")):
            self.done(site, None, _comm_call(self.take(site, None), name))


def _all_reduce_small(buf, name):
    R = buf.shape[0]

    def build(refs, fresh, send_sems, recv_sems):
        x, y, c = _coords()
        me = 4 * x + 2 * y + c
        sends, recvs = [], []
        k = 0
        for fx in range(2):
            for fy in range(2):
                for fc in range(2):
                    if fx + fy + fc == 0:
                        continue
                    px, py, pc = x ^ fx, y ^ fy, c ^ fc
                    sends.append(_remote(refs[0], refs[1].at[me], send_sems, recv_sems, k, (px, py, pc)))
                    recvs.append(_remote(refs[0], refs[1].at[4 * px + 2 * py + pc], send_sems, recv_sems, k,
                                         (px, py, pc)))
                    k += 1
        return sends, recvs

    got = _comm_call((build, [buf, jnp.zeros((N_DEV, R, 128), F32)], [1], N_DEV - 1), name + "_gather")[0]
    x, y, c = _coords()
    me = jnp.reshape(4 * x + 2 * y + c, (1,)).astype(jnp.int32)

    def body(s_ref, m_ref, own_ref, o_ref):
        acc = None
        for d in range(N_DEV):
            t = jnp.where(s_ref[0] == d, own_ref[...], m_ref[d])
            acc = t if acc is None else acc + t
        o_ref[...] = acc

    tr = max([t for t in range(8, min(R, 1024) + 1, 8) if R % t == 0], default=R)
    return pl.pallas_call(
        body,
        grid_spec=pltpu.PrefetchScalarGridSpec(
            num_scalar_prefetch=1, grid=(R // tr,),
            in_specs=[pl.BlockSpec((N_DEV, tr, 128), lambda i, s: (0, i, 0)),
                      pl.BlockSpec((tr, 128), lambda i, s: (i, 0))],
            out_specs=pl.BlockSpec((tr, 128), lambda i, s: (i, 0))),
        out_shape=jax.ShapeDtypeStruct((R, 128), F32),
        compiler_params=_params(("parallel",)), name=name + "_sum")(me, got, buf)


def _ssm_layouts(p, L):
    G = L * N_GROUPS
    lr = p["ssm_a_re"].reshape(G, 1, SSM_STATE)
    li = p["ssm_a_im"].reshape(G, 1, SSM_STATE)
    ldt = p["ssm_log_dt"].reshape(G, 1, 1)
    br = jnp.swapaxes(p["ssm_b_re"], 2, 3).reshape(G, SSM_GROUP, SSM_STATE)
    bi = jnp.swapaxes(p["ssm_b_im"], 2, 3).reshape(G, SSM_GROUP, SSM_STATE)
    are, aim, bre, bim = _disc_fwd(lr, li, ldt, br, bi)
    eye = jnp.eye(N_GROUPS, dtype=F32)
    abar = jnp.concatenate([are.reshape(L, 8, 128), aim.reshape(L, 8, 128)], axis=1)

    def b_blk(t):
        return jnp.einsum("lgcp,gh->lgchp", t.reshape(L, N_GROUPS, SSM_GROUP, SSM_STATE), eye).reshape(L, 256, 1024)

    def c_blk(t):
        return jnp.einsum("lgcp,gh->lgphc", t, eye).reshape(L, 1024, 256)

    bfull = jnp.concatenate([b_blk(bre), b_blk(bim)], axis=2).astype(BF16)
    cfull = jnp.concatenate([c_blk(p["ssm_c_re"]), -c_blk(p["ssm_c_im"])], axis=1).astype(BF16)
    return (lr, li, ldt, br, bi), abar, bfull, cfull


def _local_step(x, target, W, p, L, weave=None):
    S, Dm = x.shape
    dff = W["wg1"][0].shape[0]
    alpha = (2.0 * L) ** 0.25
    disc_in, abar, bfull, cfull = _ssm_layouts(p, L)
    row = lambda t, l: t[l][None]

    def carry(fn, site, l, *args, **kw):
        comm = weave.take(site, l) if weave is not None else None
        if comm is None:
            return fn(*args, **kw)
        res, couts = fn(*args, comm=comm, **kw)
        weave.done(site, l, couts)
        return res

    saved = []
    h, hb = x, x.astype(BF16)
    for l in range(L):
        sv = {"x0": hb}
        a1, b1, h1 = carry(_ffn_up, "up1", l, hb, W["wg1"][l], W["wu1"][l], dff, "ffn_up")
        x1, x1b, xh1, rs1 = carry(_mm_ln, "down1", l, h1, W["wd1"][l], h, row(p["ln1_g"], l), row(p["ln1_b"], l), 0.5,
                                  alpha, "ffn_down_ln", 1408)
        sv.update(a1=a1, b1=b1, h1=h1, x1=x1b, xh1=xh1, rs1=rs1)
        proj_a = _mm([(x1b, W["win"][l])], "nt", S, D_A, Dm, [F32], name="proj_a", tm=1024, tn=896, tk=1024)
        proj_g = carry(_mm, "proj_g", l, [(x1b, W["win"][l])], "nt", S, 3 * Dm, Dm, [BF16], name="proj_g", tm=2048,
                       tn=256, tk=1024, b_off=D_A)
        mc = _conv_fwd(proj_a, p["conv_w"][l], row(p["conv_b"], l), "conv_fwd")
        bu = _mm([(proj_a, (bfull, l))], "nn", S, 2048, D_SSM, [F32], name="ssm_bu", tm=1024, tn=1024, tk=256,
                 a_off=3 * D_CONV)
        xs3 = _scan_fwd(bu.reshape(S, 16, 128), abar[l], "scan_fwd")
        xs = xs3.reshape(S, 2048)
        y1, ms = _ssm_out(xs, proj_a, cfull[l], row(p["ssm_d"], l), W["wglu"][l], "ssm_out")
        sinks = p["attn_sinks"][l][None]
        ma = _attn_fwd(proj_a, sinks, "attn_fwd")
        merged = _merge_fwd(mc, ms, ma, proj_g, W["wbr"][l], "merge_fwd")
        x2, x2b, xh2, rs2 = _mm_ln(merged, W["wout"][l], x1, row(p["ln2_g"], l), row(p["ln2_b"], l), 1.0, alpha,
                                   "mix_out_ln", 1024)
        sv.update(proj_a=proj_a, proj_g=proj_g, mc=mc, ms=ms, ma=ma, xs=xs, xs3=xs3, y1=y1, merged=merged, x2=x2b,
                  xh2=xh2, rs2=rs2)
        a2, b2, h2 = carry(_ffn_up, "up2", l, x2b, W["wg2"][l], W["wu2"][l], dff, "ffn_up")
        h, hb, xh3, rs3 = _mm_ln(h2, W["wd2"][l], x2, row(p["ln3_g"], l), row(p["ln3_b"], l), 0.5, alpha,
                                 "ffn_down_ln", 1408)
        sv.update(a2=a2, b2=b2, h2=h2, xh3=xh3, rs3=rs3)
        saved.append(sv)

    dy, loss = _loss_head(h, target)
    big = [None] * L
    small = {k: [None] * L for k in ("ln1_g", "ln1_b", "ln2_g", "ln2_b", "ln3_g", "ln3_b", "conv_w", "conv_b", "ssm_d",
                                     "attn_sinks", "dabar", "dbfull", "dcfull")}

    def ffn_bwd(dy_out, x_in, a, b, hh, xh, rs, g, wg, wu, wd, l, sites):
        dres, df, dg, db = _ln_bwd(dy_out, xh, rs, g, 0.5, alpha, "ln_bwd")
        da, dbb = _ffn_dh(df, wd[l], a, b, "ffn_dh")
        dx = carry(_mm, sites[0], l, [(da, wg[l]), (dbb, wu[l])], "nn", S, Dm, dff, [F32], name="ffn_dx", tm=1024,
                   tn=1024, tk=dff // 2, add=dres)
        tn_kw = dict(tm=1408, tn=1024, tk=2048)
        dwg = carry(_mm, sites[1], l, [(da, x_in)], "tn", dff, Dm, S, [F32], name="ffn_dw_up", **tn_kw)
        dwu = carry(_mm, sites[2], l, [(dbb, x_in)], "tn", dff, Dm, S, [F32], name="ffn_dw_up", **tn_kw)
        dwd = carry(_mm, sites[3], l, [(hh, df)], "tn", dff, Dm, S, [F32], name="ffn_dw_down", **tn_kw)
        return dx, dwg, dwu, dwd, dg, db

    for l in reversed(range(L)):
        sv = saved[l]
        dx2, dwg2, dwu2, dwd2, small["ln3_g"][l], small["ln3_b"][l] = ffn_bwd(
            dy, sv["x2"], sv["a2"], sv["b2"], sv["h2"], sv["xh3"], sv["rs3"], row(p["ln3_g"], l), W["wg2"], W["wu2"],
            W["wd2"], l, ("dx2", "dwg2", "dwu2", "dwd2"))
        dres2, dmix, small["ln2_g"][l], small["ln2_b"][l] = _ln_bwd(dx2, sv["xh2"], sv["rs2"], row(p["ln2_g"], l), 1.0,
                                                                   alpha, "ln_bwd")
        dwout = carry(_mm, "dw_out", l, [(sv["merged"], dmix)], "tn", Dm, Dm, S, [F32], name="dw_out", tm=1024, tn=1024,
                      tk=2048)
        dmerged = _mm([(dmix, W["wout"][l])], "nt", S, Dm, Dm, [F32], name="d_merged", tm=1024, tn=1024, tk=512)
        dgates, dyb, dmc, dms, dma = _merge_bwd(dmerged, sv["mc"], sv["ms"], sv["ma"], sv["proj_g"], W["wbr"][l],
                                                "merge_bwd")
        dwbr = jnp.concatenate([
            _mm([(dyb, m)], "tn", Dm, hi - lo, S, [F32], name="dw_br", tm=1024, tn=512, tk=2048, a_off=b * Dm)
            for b, ((lo, hi), m) in enumerate(zip(_BR, (sv["mc"], sv["ms"], sv["ma"])))], axis=1)
        proj_a = sv["proj_a"]
        d_conv, small["conv_w"][l], small["conv_b"][l] = _conv_bwd(proj_a, dmc, p["conv_w"][l], row(p["conv_b"], l),
                                                                  "conv_bwd")
        dy1, y2, dgl, dxs, du_skip, small["ssm_d"][l] = _ssm_out_bwd(dms, sv["y1"], proj_a, cfull[l],
                                                                    row(p["ssm_d"], l), W["wglu"][l], "ssm_out_bwd")
        dwglu = _mm([(y2, dgl)], "tn", D_SSM, D_SSM, S, [F32], name="dw_glu", tk=2048)
        small["dcfull"][l] = _mm([(sv["xs"], dy1)], "tn", 2048, D_SSM, S, [F32], name="d_cfull", tm=1024, tk=1024)
        lam, small["dabar"][l] = _scan_bwd(dxs.reshape(S, 16, 128), sv["xs3"], abar[l], "scan_bwd")
        lam = lam.reshape(S, 2048)
        du = _mm([(lam, (bfull, l))], "nt", S, D_SSM, 2048, [BF16], name="ssm_du", tm=1024, tk=512, add=du_skip)
        small["dbfull"][l] = _mm([(proj_a, lam)], "tn", D_SSM, 2048, S, [F32], name="d_bfull", tm=256, tn=1024,
                                 tk=1024, a_off=3 * D_CONV)
        dq, dk, dv, dsk = _attn_bwd(proj_a, dma, p["attn_sinks"][l][None], "attn_bwd")
        small["attn_sinks"][l] = dsk[:, 0]
        dproj = jnp.concatenate([d_conv, du, dq, dk, dv, dgates], axis=1)
        dx1 = carry(_mm, "d_x1", l, [(dproj, W["win"][l])], "nn", S, Dm, D_A + 3 * Dm, [F32], name="d_x1", tm=1024,
                    tn=1024, tk=(D_A + 3 * Dm) // 2, add=dres2)
        dwin = carry(_mm, "dw_in", l, [(dproj, sv["x1"])], "tn", D_A + 3 * Dm, Dm, S, [F32], name="dw_in", tm=2432,
                     tn=1024, tk=1024)
        if weave is not None:
            weave.grads_x(l, dict(wg2=dwg2, wu2=dwu2, wd2=dwd2, win=dwin, wbr=dwbr, wout=dwout, wglu=dwglu))
        dx0, dwg1, dwu1, dwd1, small["ln1_g"][l], small["ln1_b"][l] = ffn_bwd(
            dx1, sv["x0"], sv["a1"], sv["b1"], sv["h1"], sv["xh1"], sv["rs1"], row(p["ln1_g"], l), W["wg1"], W["wu1"],
            W["wd1"], l, ("dx1", "dwg1", "dwu1", "dwd1"))
        big[l] = dict(wg1=dwg1, wu1=dwu1, wd1=dwd1, win=dwin, wbr=dwbr, wout=dwout, wg2=dwg2, wu2=dwu2, wd2=dwd2,
                      wglu=dwglu)
        if weave is not None:
            weave.grads_y(l, dict(wg1=dwg1, wu1=dwu1, wd1=dwd1))
        dy = dx0

    small = {k: jnp.stack(v) for k, v in small.items()}
    eye = jnp.eye(N_GROUPS, dtype=F32)
    dabar = small.pop("dabar")
    dbf = small.pop("dbfull").reshape(L, N_GROUPS, SSM_GROUP, 2, N_GROUPS, SSM_STATE)
    dbbar = jnp.einsum("lgcrhp,gh->rlgcp", dbf, eye).reshape(2, L * N_GROUPS, SSM_GROUP, SSM_STATE)
    dcf = small.pop("dcfull").reshape(L, 2, N_GROUPS, SSM_STATE, N_GROUPS, SSM_GROUP)
    dc = jnp.einsum("lrgphc,gh->rlgcp", dcf, eye)
    G = L * N_GROUPS
    cts = (dabar[:, 0:8].reshape(G, 1, SSM_STATE), dabar[:, 8:16].reshape(G, 1, SSM_STATE), dbbar[0], dbbar[1])
    dlr, dli, dldt, dbr, dbi = _disc_bwd(*disc_in, cts)
    shp_b = (L, N_GROUPS, SSM_GROUP, SSM_STATE)
    small.update(
        ssm_a_re=dlr.reshape(L, N_GROUPS, SSM_STATE), ssm_a_im=dli.reshape(L, N_GROUPS, SSM_STATE),
        ssm_log_dt=dldt.reshape(L, N_GROUPS), ssm_b_re=jnp.swapaxes(dbr.reshape(shp_b), 2, 3),
        ssm_b_im=jnp.swapaxes(dbi.reshape(shp_b), 2, 3), ssm_c_re=dc[0], ssm_c_im=-dc[1],
        ln1_g=small["ln1_g"][:, 0], ln1_b=small["ln1_b"][:, 0], ln2_g=small["ln2_g"][:, 0],
        ln2_b=small["ln2_b"][:, 0], ln3_g=small["ln3_g"][:, 0], ln3_b=small["ln3_b"][:, 0],
        conv_b=small["conv_b"][:, 0], ssm_d=small["ssm_d"][:, 0])
    return loss, dy, big, small


_SMALL_ORDER = ("ln1_g", "ln1_b", "ln2_g", "ln2_b", "ln3_g", "ln3_b", "conv_w", "conv_b", "ssm_a_re", "ssm_a_im",
                "ssm_log_dt", "ssm_b_re", "ssm_b_im", "ssm_c_re", "ssm_c_im", "ssm_d", "attn_sinks")
_BIG_ORDER = ("wg1", "wu1", "wd1", "win", "wbr", "wout", "wg2", "wu2", "wd2", "wglu")
_WEIGHTS = ("ffn1_w_gate", "ffn1_w_up", "ffn1_w_down", "ln1_g", "ln1_b", "w_in", "conv_w", "conv_b", "ssm_a_re",
            "ssm_a_im", "ssm_log_dt", "ssm_b_re", "ssm_b_im", "ssm_c_re", "ssm_c_im", "ssm_d", "ssm_w_glu",
            "attn_sinks", "w_br_conv", "w_br_ssm", "w_br_attn", "w_out", "ln2_g", "ln2_b", "ffn2_w_gate",
            "ffn2_w_up", "ffn2_w_down", "ln3_g", "ln3_b")


def _weight_shards(w):
    t = lambda a: jnp.swapaxes(a, 1, 2).astype(BF16)
    wbr = jnp.concatenate([t(w["w_br_conv"]), t(w["w_br_ssm"]), t(w["w_br_attn"])], axis=2)
    return dict(wg1=t(w["ffn1_w_gate"]), wu1=t(w["ffn1_w_up"]), wd1=w["ffn1_w_down"].astype(BF16), win=t(w["w_in"]),
                wbr=wbr, wout=w["w_out"].astype(BF16), wg2=t(w["ffn2_w_gate"]), wu2=t(w["ffn2_w_up"]),
                wd2=w["ffn2_w_down"].astype(BF16), wglu=w["ssm_w_glu"].astype(BF16))


def _gather_conv_w(conv_w, chip):
    L = conv_w.shape[0]
    n = L * 3 * 64
    slots = lax.dynamic_update_slice(jnp.zeros((N_CHIPS, n), F32), conv_w.reshape(1, n), (chip, 0))
    got = (_all_reduce_small(slots.reshape(-1, 128), "conv_w_gather") * 0.5).reshape(N_CHIPS, L, 3, 64)
    return jnp.transpose(got, (1, 2, 0, 3)).reshape(L, 3, N_CHIPS * 64)


def kernel(x, ffn1_w_gate, ffn1_w_up, ffn1_w_down, ln1_g, ln1_b, w_in, conv_w, conv_b, ssm_a_re, ssm_a_im, ssm_log_dt, ssm_b_re, ssm_b_im, ssm_c_re, ssm_c_im, ssm_d, ssm_w_glu, attn_sinks, w_br_conv, w_br_ssm, w_br_attn, w_out, ln2_g, ln2_b, ffn2_w_gate, ffn2_w_up, ffn2_w_down, ln3_g, ln3_b, loss_target, m_ffn1_w_gate, m_ffn1_w_up, m_ffn1_w_down, m_ln1_g, m_ln1_b, m_w_in, m_conv_w, m_conv_b, m_ssm_a_re, m_ssm_a_im, m_ssm_log_dt, m_ssm_b_re, m_ssm_b_im, m_ssm_c_re, m_ssm_c_im, m_ssm_d, m_ssm_w_glu, m_attn_sinks, m_w_br_conv, m_w_br_ssm, m_w_br_attn, m_w_out, m_ln2_g, m_ln2_b, m_ffn2_w_gate, m_ffn2_w_up, m_ffn2_w_down, m_ln3_g, m_ln3_b, v_ffn1_w_gate, v_ffn1_w_up, v_ffn1_w_down, v_ln1_g, v_ln1_b, v_w_in, v_conv_w, v_conv_b, v_ssm_a_re, v_ssm_a_im, v_ssm_log_dt, v_ssm_b_re, v_ssm_b_im, v_ssm_c_re, v_ssm_c_im, v_ssm_d, v_ssm_w_glu, v_attn_sinks, v_w_br_conv, v_w_br_ssm, v_w_br_attn, v_w_out, v_ln2_g, v_ln2_b, v_ffn2_w_gate, v_ffn2_w_up, v_ffn2_w_down, v_ln3_g, v_ln3_b):
    args = dict(locals())
    w = {k: args[k] for k in _WEIGHTS}
    L = ln1_g.shape[0]
    cx, cy, cc = _coords()
    chip = 2 * cx + cy

    shards = _weight_shards(w)
    full = {k: [lax.empty((N_CHIPS * s.shape[1], s.shape[2]), BF16) for _ in range(L)] for k, s in shards.items()}
    weave = _Weave(shards, full, L)
    weave.gather_first()
    p = {k: w[k] for k in ("ln1_g", "ln1_b", "ln2_g", "ln2_b", "ln3_g", "ln3_b", "conv_b", "ssm_a_re", "ssm_a_im",
                           "ssm_log_dt", "ssm_b_re", "ssm_b_im", "ssm_c_re", "ssm_c_im", "ssm_d", "attn_sinks")}
    p["conv_w"] = _gather_conv_w(conv_w, chip)

    loss, grad_x, _, small = _local_step(x[0], loss_target[0], full, p, L, weave)
    weave.flush()
    loss = lax.psum(loss[0, 0], ("x", "y", "c"))

    sizes = [math.prod(small[k].shape) for k in _SMALL_ORDER]
    pad = (-sum(sizes)) % 1024
    flat = jnp.concatenate([small[k].reshape(-1) for k in _SMALL_ORDER] + [jnp.zeros((pad,), F32)])
    flat = _all_reduce_small(flat.reshape(-1, 128), "small_grads").reshape(-1)
    sm, off = {}, 0
    for k, n in zip(_SMALL_ORDER, sizes):
        sm[k] = flat[off:off + n].reshape(small[k].shape)
        off += n
    red = {k: t.reshape(L, 2 * t.shape[2], t.shape[3]) for k, t in weave.final.items()}
    tr = lambda a: jnp.swapaxes(a, 1, 2)
    grads = dict(sm)
    grads.update(
        ffn1_w_gate=tr(red["wg1"]), ffn1_w_up=tr(red["wu1"]), ffn1_w_down=red["wd1"], w_in=tr(red["win"]),
        w_br_conv=tr(red["wbr"][:, :, _BR[0][0]:_BR[0][1]]), w_br_ssm=tr(red["wbr"][:, :, _BR[1][0]:_BR[1][1]]),
        w_br_attn=tr(red["wbr"][:, :, _BR[2][0]:_BR[2][1]]), w_out=red["wout"], ffn2_w_gate=tr(red["wg2"]),
        ffn2_w_up=tr(red["wu2"]), ffn2_w_down=red["wd2"], ssm_w_glu=red["wglu"],
        conv_w=lax.dynamic_slice_in_dim(sm["conv_w"], chip * 64, 64, axis=2))

    outs = [[], [], [], []]
    for k in _WEIGHTS:
        d, nm, nv = _adamw(w[k], grads[k], args["m_" + k], args["v_" + k], "adamw",
                           blocks_3d=k in ("ffn1_w_down", "ffn2_w_down", "w_out"))
        for lst, val in zip(outs, (grads[k], d, nm, nv)):
            lst.append(val)
    return (loss, grad_x[None], *outs[0], *outs[1], *outs[2], *outs[3])
```

```python
import functools
import math

import jax
import jax.numpy as jnp
from jax import lax
from jax.experimental import pallas as pl
from jax.experimental.pallas import tpu as pltpu

F32 = jnp.float32
BF16 = jnp.bfloat16

LN_EPS = 1e-5
D_CONV = 256
D_SSM = 256
N_GROUPS = 16
SSM_GROUP = 16
SSM_STATE = 64
N_Q_HEADS = 8
HEAD_DIM = 64
D_ATTN = 512
D_KV = 128
BLOCK = 128
D_A = 3 * D_CONV + D_SSM + D_ATTN + 2 * D_KV
ADAM_LR = 0.001
ADAM_B1 = 0.9
ADAM_B2 = 0.999
ADAM_EPS = 1e-08
ADAM_WD = 0.01
ADAM_STEP = 10

VMEM_LIMIT_BYTES = 56 * 1024 * 1024
MESH = pl.DeviceIdType.MESH
N_CHIPS = 4
N_DEV = 8


def _params(sem=None):
    return pltpu.CompilerParams(dimension_semantics=sem, vmem_limit_bytes=VMEM_LIMIT_BYTES)


def _op(op, block, imap):
    if isinstance(op, tuple):
        arr, l = op
        return arr, pl.BlockSpec((None,) + block, lambda *g: (l,) + imap(*g))
    return op, pl.BlockSpec(block, imap)


def _call(body, *, grid, in_specs, out_specs, out_shape, scratch=(), sem=None, name, ops, comm=None):
    if comm is None:
        return pl.pallas_call(body, grid=grid, in_specs=in_specs, out_specs=out_specs, out_shape=out_shape,
                              scratch_shapes=list(scratch), compiler_params=_params(sem), name=name)(*ops), []
    build, cins, couts, n_sem = comm
    n_in, n_out, n_scr, nci, nco, ng = len(in_specs), len(out_specs), len(scratch), len(cins), len(couts), len(grid)
    hbm = pl.BlockSpec(memory_space=pltpu.HBM)
    aliases = {n_in + o: n_out + j for j, o in enumerate(couts) if isinstance(o, int)}
    cshapes = [jax.ShapeDtypeStruct(cins[o].shape, cins[o].dtype) if isinstance(o, int) else o for o in couts]

    def hosted(*refs):
        refs = list(refs)
        main = refs[:n_in] + refs[n_in + nci:n_in + nci + n_out] + refs[n_in + nci + n_out + nco:-2]
        crefs, co = refs[n_in:n_in + nci], refs[n_in + nci + n_out:n_in + nci + n_out + nco]
        send_sems, recv_sems = refs[-2:]
        fresh = []
        for j, o in enumerate(couts):
            if isinstance(o, int):
                crefs[o] = co[j]
            else:
                fresh.append(co[j])

        def start():
            for cp in build(crefs, fresh, send_sems, recv_sems)[0]:
                cp.start()

        def finish():
            sends, recvs = build(crefs, fresh, send_sems, recv_sems)
            for cp in recvs:
                cp.wait_recv()
            for cp in sends:
                cp.wait_send()

        if ng == 0:
            start()
            finish()
            return
        ids = [pl.program_id(d) for d in range(ng)]
        first = functools.reduce(jnp.logical_and, [ids[d] == 0 for d in range(ng)])
        last = functools.reduce(jnp.logical_and, [ids[d] == grid[d] - 1 for d in range(ng)])
        pl.when(first)(start)
        body(*main)
        pl.when(last)(finish)

    res = pl.pallas_call(
        hosted, grid=grid, in_specs=list(in_specs) + [hbm] * nci, out_specs=list(out_specs) + [hbm] * nco,
        out_shape=list(out_shape) + cshapes,
        scratch_shapes=list(scratch) + [pltpu.SemaphoreType.DMA((n_sem,)), pltpu.SemaphoreType.DMA((n_sem,))],
        input_output_aliases=aliases,
        compiler_params=pltpu.CompilerParams(dimension_semantics=("arbitrary",) * ng if ng else None,
                                             vmem_limit_bytes=VMEM_LIMIT_BYTES, has_side_effects=True),
        name=name)(*ops, *cins)
    return res[:n_out], res[n_out:]


def _mm(pairs, mode, m, n, k, out_dtypes, *, name, tm=512, tn=512, tk=512, add=None, a_off=0, b_off=0, comm=None):
    tm, tn, tk = min(tm, m), min(tn, n), min(tk, k)
    assert m % tm == 0 and n % tn == 0 and k % tk == 0 and b_off % tn == 0, (name, m, n, k, tm, tn, tk)
    nk, npair, jo = k // tk, len(pairs), b_off // tn
    if mode == "nn":
        ao = a_off // tk
        ab, ai, bb, bi = (tm, tk), (lambda i, j, kk: (i, kk + ao)), (tk, tn), (lambda i, j, kk: (kk, j))
        dims = (((1,), (0,)), ((), ()))
    elif mode == "nt":
        ao = a_off // tk
        ab, ai, bb, bi = (tm, tk), (lambda i, j, kk: (i, kk + ao)), (tn, tk), (lambda i, j, kk: (j + jo, kk))
        dims = (((1,), (1,)), ((), ()))
    else:
        ao = a_off // tm
        ab, ai, bb, bi = (tk, tm), (lambda i, j, kk: (kk, i + ao)), (tk, tn), (lambda i, j, kk: (kk, j))
        dims = (((0,), (0,)), ((), ()))
    assert a_off % (tm if mode == "tn" else tk) == 0, (name, a_off)
    ops, specs = [], []
    for a, b in pairs:
        for o, blk, im in ((a, ab, ai), (b, bb, bi)):
            arr, sp = _op(o, blk, im)
            ops.append(arr)
            specs.append(sp)
    has_add = add is not None
    if has_add:
        ops.append(add)
        specs.append(pl.BlockSpec((tm, tn), lambda i, j, kk: (i, j)))
    nout = len(out_dtypes)

    def body(*refs):
        outs, acc = refs[2 * npair + has_add:2 * npair + has_add + nout], refs[-1]
        kk = pl.program_id(2)
        t = None
        for p in range(npair):
            d = lax.dot_general(refs[2 * p][...].astype(BF16), refs[2 * p + 1][...].astype(BF16), dims,
                                preferred_element_type=F32)
            t = d if t is None else t + d

        def finish(r):
            if has_add:
                r = r + refs[2 * npair][...]
            for o in outs:
                o[...] = r.astype(o.dtype)

        if nk == 1:
            finish(t)
            return

        @pl.when(kk == 0)
        def _():
            acc[...] = jnp.zeros_like(acc)

        acc[...] += t

        @pl.when(kk == nk - 1)
        def _():
            finish(acc[...])

    res, couts = _call(
        body, grid=(m // tm, n // tn, nk), in_specs=specs,
        out_specs=[pl.BlockSpec((tm, tn), lambda i, j, kk: (i, j))] * nout,
        out_shape=[jax.ShapeDtypeStruct((m, n), dt) for dt in out_dtypes],
        scratch=[pltpu.VMEM((tm, tn), F32)], sem=("parallel", "parallel", "arbitrary"), name=name, ops=ops, comm=comm)
    res = res[0] if nout == 1 else res
    return res if comm is None else (res, couts)


def _rows(width, col=0, tm=None):
    return pl.BlockSpec((tm, width), lambda i: (i, col))


def _whole(shape):
    nd = len(shape)
    return pl.BlockSpec(shape, lambda i: (0,) * nd)


def _sigmoid(x):
    return 0.5 * jnp.tanh(0.5 * x) + 0.5


def _mm_ln(a, w, x, g, b, s, alpha, name, tk, comm=None):
    S, K = a.shape
    Dm = x.shape[1]
    tm, tk = min(1024, S), min(tk, K)
    nk = K // tk

    def body(a_ref, w_ref, x_ref, g_ref, b_ref, y_ref, yb_ref, xh_ref, rs_ref, acc):
        kk = pl.program_id(1)
        d = jnp.dot(a_ref[...], w_ref[...], preferred_element_type=F32)

        if nk > 1:
            @pl.when(kk == 0)
            def _():
                acc[...] = d

            @pl.when((kk > 0) & (kk < nk - 1))
            def _():
                acc[...] += d

        @pl.when(kk == nk - 1)
        def _():
            z = alpha * x_ref[...] + s * (d if nk == 1 else acc[...] + d)
            mu = jnp.mean(z, axis=-1, keepdims=True)
            zc = z - mu
            var = jnp.mean(zc * zc, axis=-1, keepdims=True)
            rstd = lax.rsqrt(var + LN_EPS)
            xh = zc * rstd
            y = xh * g_ref[...] + b_ref[...]
            y_ref[...] = y
            yb_ref[...] = y.astype(BF16)
            xh_ref[...] = xh
            rs_ref[...] = rstd

    wa, ws = _op(w, (tk, Dm), lambda i, kk: (kk, 0))
    row = pl.BlockSpec((tm, Dm), lambda i, kk: (i, 0))
    vec = pl.BlockSpec((1, Dm), lambda i, kk: (0, 0))
    res, couts = _call(
        body, grid=(S // tm, nk),
        in_specs=[pl.BlockSpec((tm, tk), lambda i, kk: (i, kk)), ws, row, vec, vec],
        out_specs=[row, row, row, pl.BlockSpec((tm, 1), lambda i, kk: (i, 0))],
        out_shape=[jax.ShapeDtypeStruct((S, Dm), F32), jax.ShapeDtypeStruct((S, Dm), BF16),
                   jax.ShapeDtypeStruct((S, Dm), F32), jax.ShapeDtypeStruct((S, 1), F32)],
        scratch=[pltpu.VMEM((tm, Dm), F32)], sem=("parallel", "arbitrary"), name=name, ops=[a, wa, x, g, b],
        comm=comm)
    return res if comm is None else (res, couts)


def _ln_bwd(dy, xh, rs, g, s, alpha, name):
    S, Dm = dy.shape
    tm = min(512, S)

    def body(dy_ref, xh_ref, rs_ref, g_ref, dres_ref, dbr_ref, dg_ref, db_ref):
        @pl.when(pl.program_id(0) == 0)
        def _():
            dg_ref[...] = jnp.zeros_like(dg_ref)
            db_ref[...] = jnp.zeros_like(db_ref)

        dy, xh = dy_ref[...], xh_ref[...]
        dyg = dy * g_ref[...]
        m1 = jnp.mean(dyg, axis=-1, keepdims=True)
        m2 = jnp.mean(dyg * xh, axis=-1, keepdims=True)
        dz = rs_ref[...] * (dyg - m1 - xh * m2)
        dres_ref[...] = alpha * dz
        dbr_ref[...] = (s * dz).astype(BF16)
        dg_ref[...] += jnp.sum(dy * xh, axis=0, keepdims=True)
        db_ref[...] += jnp.sum(dy, axis=0, keepdims=True)

    return pl.pallas_call(
        body, grid=(S // tm,),
        in_specs=[_rows(Dm, tm=tm), _rows(Dm, tm=tm), _rows(1, tm=tm), _whole((1, Dm))],
        out_specs=[_rows(Dm, tm=tm), _rows(Dm, tm=tm), _whole((1, Dm)), _whole((1, Dm))],
        out_shape=[jax.ShapeDtypeStruct((S, Dm), F32), jax.ShapeDtypeStruct((S, Dm), BF16),
                   jax.ShapeDtypeStruct((1, Dm), F32), jax.ShapeDtypeStruct((1, Dm), F32)],
        compiler_params=_params(("arbitrary",)), name=name)(dy, xh, rs, g)


def _ffn_up(x, wg, wu, dff, name, comm=None):
    S, Dm = x.shape
    tm, tn = min(512, S), dff // 2

    def body(x_ref, wg_ref, wu_ref, a_ref, b_ref, h_ref):
        xb = x_ref[...]
        dims = (((1,), (1,)), ((), ()))
        a = lax.dot_general(xb, wg_ref[...], dims, preferred_element_type=F32)
        b = lax.dot_general(xb, wu_ref[...], dims, preferred_element_type=F32)
        a_ref[...] = a.astype(BF16)
        b_ref[...] = b.astype(BF16)
        h_ref[...] = (a * _sigmoid(a) * b).astype(BF16)

    wga, wgs = _op(wg, (tn, Dm), lambda i, j: (j, 0))
    wua, wus = _op(wu, (tn, Dm), lambda i, j: (j, 0))
    ob = pl.BlockSpec((tm, tn), lambda i, j: (i, j))
    res, couts = _call(
        body, grid=(S // tm, dff // tn),
        in_specs=[pl.BlockSpec((tm, Dm), lambda i, j: (i, 0)), wgs, wus], out_specs=[ob, ob, ob],
        out_shape=[jax.ShapeDtypeStruct((S, dff), BF16)] * 3, sem=("parallel", "parallel"), name=name,
        ops=[x, wga, wua], comm=comm)
    return res if comm is None else (res, couts)


def _ffn_dh(df, wd, a, b, name):
    S, Dm = df.shape
    dff = a.shape[1]
    tm, tn = min(512, S), dff // 2

    def body(df_ref, wd_ref, a_ref, b_ref, da_ref, db_ref):
        dh = lax.dot_general(df_ref[...], wd_ref[...], (((1,), (1,)), ((), ())), preferred_element_type=F32)
        a, b = a_ref[...].astype(F32), b_ref[...].astype(F32)
        sg = _sigmoid(a)
        da_ref[...] = (dh * b * (sg * (1.0 + a * (1.0 - sg)))).astype(BF16)
        db_ref[...] = (dh * (a * sg)).astype(BF16)

    wda, wds = _op(wd, (tn, Dm), lambda i, j: (j, 0))
    ob = pl.BlockSpec((tm, tn), lambda i, j: (i, j))
    return pl.pallas_call(
        body, grid=(S // tm, dff // tn),
        in_specs=[pl.BlockSpec((tm, Dm), lambda i, j: (i, 0)), wds, ob, ob], out_specs=[ob, ob],
        out_shape=[jax.ShapeDtypeStruct((S, dff), BF16), jax.ShapeDtypeStruct((S, dff), BF16)],
        compiler_params=_params(("parallel", "parallel")), name=name)(df, wda, a, b)


def _ffn_dh_ln(dy, xh, rs, g, wd, a, b, alpha, name):
    S, Dm = dy.shape
    dff = a.shape[1]
    tm, tn = min(512, S), dff // 2

    def body(dy_ref, xh_ref, rs_ref, g_ref, wd_ref, a_ref, b_ref, dres_ref, df_ref, da_ref, db_ref, dg_ref, dbl_ref,
             df_scr):
        i, j = pl.program_id(0), pl.program_id(1)

        @pl.when((i == 0) & (j == 0))
        def _():
            dg_ref[...] = jnp.zeros_like(dg_ref)
            dbl_ref[...] = jnp.zeros_like(dbl_ref)

        @pl.when(j == 0)
        def _():
            dyv, xhv = dy_ref[...], xh_ref[...]
            dyg = dyv * g_ref[...]
            m1 = jnp.mean(dyg, axis=-1, keepdims=True)
            m2 = jnp.mean(dyg * xhv, axis=-1, keepdims=True)
            dz = rs_ref[...] * (dyg - m1 - xhv * m2)
            dres_ref[...] = alpha * dz
            dfv = (0.5 * dz).astype(BF16)
            df_ref[...] = dfv
            df_scr[...] = dfv
            dg_ref[...] += jnp.sum(dyv * xhv, axis=0, keepdims=True)
            dbl_ref[...] += jnp.sum(dyv, axis=0, keepdims=True)

        dh = lax.dot_general(df_scr[...], wd_ref[...], (((1,), (1,)), ((), ())), preferred_element_type=F32)
        av, bv = a_ref[...].astype(F32), b_ref[...].astype(F32)
        sg = _sigmoid(av)
        da_ref[...] = (dh * bv * (sg * (1.0 + av * (1.0 - sg)))).astype(BF16)
        db_ref[...] = (dh * (av * sg)).astype(BF16)

    row = pl.BlockSpec((tm, Dm), lambda i, j: (i, 0))
    vec = pl.BlockSpec((1, Dm), lambda i, j: (0, 0))
    ob = pl.BlockSpec((tm, tn), lambda i, j: (i, j))
    return pl.pallas_call(
        body, grid=(S // tm, dff // tn),
        in_specs=[row, row, pl.BlockSpec((tm, 1), lambda i, j: (i, 0)), vec,
                  pl.BlockSpec((tn, Dm), lambda i, j: (j, 0)), ob, ob],
        out_specs=[row, row, ob, ob, vec, vec],
        out_shape=[jax.ShapeDtypeStruct((S, Dm), F32), jax.ShapeDtypeStruct((S, Dm), BF16),
                   jax.ShapeDtypeStruct((S, dff), BF16), jax.ShapeDtypeStruct((S, dff), BF16),
                   jax.ShapeDtypeStruct((1, Dm), F32), jax.ShapeDtypeStruct((1, Dm), F32)],
        scratch_shapes=[pltpu.VMEM((tm, Dm), BF16)],
        compiler_params=_params(("arbitrary", "arbitrary")), name=name)(dy, xh, rs, g, wd, a, b)


def _halo_prev(width, col, tm):
    return pl.BlockSpec((8, width), lambda i: (jnp.maximum(i * (tm // 8) - 1, 0), col))


def _halo_next(width, col, tm, S):
    return pl.BlockSpec((8, width), lambda i: (jnp.minimum((i + 1) * (tm // 8), S // 8 - 1), col))


def _shift_down(prev8, cur, n):
    ext = jnp.concatenate([prev8, cur], axis=0)
    return pltpu.roll(ext, n, axis=0)[8:]


def _shift_up(cur, next8, n):
    ext = jnp.concatenate([cur, next8], axis=0)
    return pltpu.roll(ext, ext.shape[0] - n, axis=0)[:cur.shape[0]]


def _conv_fwd(proj_a, conv_w, conv_b, name):
    S = proj_a.shape[0]
    tm = min(512, S)
    C = D_CONV

    def body(bg_ref, cg_ref, h_ref, cgp_ref, hp_ref, w_ref, cb_ref, out_ref):
        z = cg_ref[...] * h_ref[...]
        zp = jnp.where(pl.program_id(0) > 0, cgp_ref[...] * hp_ref[...], 0.0)
        w = w_ref[...]
        y = w[2:3] * z + w[1:2] * _shift_down(zp, z, 1) + w[0:1] * _shift_down(zp, z, 2) + cb_ref[...]
        out_ref[...] = (bg_ref[...] * y).astype(BF16)

    return pl.pallas_call(
        body, grid=(S // tm,),
        in_specs=[_rows(C, 0, tm), _rows(C, 1, tm), _rows(C, 2, tm), _halo_prev(C, 1, tm), _halo_prev(C, 2, tm),
                  _whole((3, C)), _whole((1, C))],
        out_specs=_rows(C, 0, tm), out_shape=jax.ShapeDtypeStruct((S, C), BF16),
        compiler_params=_params(("parallel",)), name=name)(proj_a, proj_a, proj_a, proj_a, proj_a, conv_w, conv_b)


def _conv_bwd(proj_a, dmc, conv_w, conv_b, name):
    S = proj_a.shape[0]
    tm = min(512, S)
    C = D_CONV
    nblk = S // tm

    def body(bg_ref, cg_ref, h_ref, cgp_ref, hp_ref, bgn_ref, d_ref, dn_ref, w_ref, cb_ref, out_ref, dw_ref,
             dcb_ref):
        i = pl.program_id(0)

        @pl.when(i == 0)
        def _():
            dw_ref[...] = jnp.zeros_like(dw_ref)
            dcb_ref[...] = jnp.zeros_like(dcb_ref)

        bg, cg, h, d = bg_ref[...], cg_ref[...], h_ref[...], d_ref[...]
        z = cg * h
        zp = jnp.where(i > 0, cgp_ref[...] * hp_ref[...], 0.0)
        w = w_ref[...]
        z1, z2 = _shift_down(zp, z, 1), _shift_down(zp, z, 2)
        y = w[2:3] * z + w[1:2] * z1 + w[0:1] * z2 + cb_ref[...]
        dy = d * bg
        dyn = jnp.where(i < nblk - 1, dn_ref[...] * bgn_ref[...], 0.0)
        dz = w[2:3] * dy + w[1:2] * _shift_up(dy, dyn, 1) + w[0:1] * _shift_up(dy, dyn, 2)
        out_ref[:, 0:C] = (d * y).astype(BF16)
        out_ref[:, C:2 * C] = (dz * h).astype(BF16)
        out_ref[:, 2 * C:3 * C] = (dz * cg).astype(BF16)
        dw_ref[0:1, :] += jnp.sum(dy * z2, axis=0, keepdims=True)
        dw_ref[1:2, :] += jnp.sum(dy * z1, axis=0, keepdims=True)
        dw_ref[2:3, :] += jnp.sum(dy * z, axis=0, keepdims=True)
        dcb_ref[...] += jnp.sum(dy, axis=0, keepdims=True)

    return pl.pallas_call(
        body, grid=(nblk,),
        in_specs=[_rows(C, 0, tm), _rows(C, 1, tm), _rows(C, 2, tm), _halo_prev(C, 1, tm), _halo_prev(C, 2, tm),
                  _halo_next(C, 0, tm, S), _rows(C, 0, tm), _halo_next(C, 0, tm, S), _whole((3, C)), _whole((1, C))],
        out_specs=[_rows(3 * C, 0, tm), _whole((3, C)), _whole((1, C))],
        out_shape=[jax.ShapeDtypeStruct((S, 3 * C), BF16), jax.ShapeDtypeStruct((3, C), F32),
                   jax.ShapeDtypeStruct((1, C), F32)],
        compiler_params=_params(("arbitrary",)), name=name)(
            proj_a, proj_a, proj_a, proj_a, proj_a, proj_a, dmc, dmc, conv_w, conv_b)


def _disc_math(lr, li, ldt, br, bi):
    dt = jnp.exp(ldt)
    mag = jnp.exp(lr * dt)
    ang = li * dt
    are = mag * jnp.cos(ang)
    aim = mag * jnp.sin(ang)
    nr = are - 1.0
    den = lr * lr + li * li
    cre = (nr * lr + aim * li) / den
    cim = (aim * lr - nr * li) / den
    return are, aim, cre * br - cim * bi, cre * bi + cim * br


def _disc_fwd(lr, li, ldt, br, bi):
    shapes = [lr.shape, lr.shape, br.shape, br.shape]

    def body(lr_ref, li_ref, ldt_ref, br_ref, bi_ref, *outs):
        for o, v in zip(outs, _disc_math(lr_ref[...], li_ref[...], ldt_ref[...], br_ref[...], bi_ref[...])):
            o[...] = v

    return pl.pallas_call(body, out_shape=[jax.ShapeDtypeStruct(s, F32) for s in shapes],
                          compiler_params=_params(), name="ssm_disc")(lr, li, ldt, br, bi)


def _disc_bwd(lr, li, ldt, br, bi, cts):
    shapes = [lr.shape, lr.shape, ldt.shape, br.shape, br.shape]

    def body(lr_ref, li_ref, ldt_ref, br_ref, bi_ref, c0, c1, c2, c3, *outs):
        _, vjp = jax.vjp(_disc_math, lr_ref[...], li_ref[...], ldt_ref[...], br_ref[...], bi_ref[...])
        for o, v in zip(outs, vjp((c0[...], c1[...], c2[...], c3[...]))):
            o[...] = v

    return pl.pallas_call(body, out_shape=[jax.ShapeDtypeStruct(s, F32) for s in shapes],
                          compiler_params=_params(), name="ssm_disc_bwd")(lr, li, ldt, br, bi, *cts)


def _scan_fwd(bu, abar, name):
    S = bu.shape[0]
    tb = min(256, S)

    def body(bu_ref, a_ref, xs_ref, st_ref):
        @pl.when(pl.program_id(0) == 0)
        def _():
            st_ref[...] = jnp.zeros_like(st_ref)

        ar, ai = a_ref[0:8, :], a_ref[8:16, :]
        a2r, a2i = ar * ar - ai * ai, 2.0 * ar * ai

        def step(s, c):
            xr, xi = c
            t = 2 * s
            b0r, b0i, b1r, b1i = bu_ref[t, 0:8, :], bu_ref[t, 8:16, :], bu_ref[t + 1, 0:8, :], bu_ref[t + 1, 8:16, :]
            cr = ar * b0r - ai * b0i + b1r
            ci = ar * b0i + ai * b0r + b1i
            xs_ref[t, 0:8, :] = ar * xr - ai * xi + b0r
            xs_ref[t, 8:16, :] = ar * xi + ai * xr + b0i
            nr = a2r * xr - a2i * xi + cr
            ni = a2r * xi + a2i * xr + ci
            xs_ref[t + 1, 0:8, :] = nr
            xs_ref[t + 1, 8:16, :] = ni
            return nr, ni

        xr, xi = lax.fori_loop(0, tb // 2, step, (st_ref[0:8, :], st_ref[8:16, :]), unroll=4)
        st_ref[0:8, :] = xr
        st_ref[8:16, :] = xi

    blk = pl.BlockSpec((tb, 16, 128), lambda i: (i, 0, 0))
    return pl.pallas_call(
        body, grid=(S // tb,), in_specs=[blk, _whole((16, 128))], out_specs=blk,
        out_shape=jax.ShapeDtypeStruct((S, 16, 128), F32), scratch_shapes=[pltpu.VMEM((16, 128), F32)],
        compiler_params=_params(("arbitrary",)), name=name)(bu, abar)


def _scan_bwd(dxs, xs, abar, name):
    S = dxs.shape[0]
    tb = min(256, S)
    nblk = S // tb

    def body(d_ref, x_ref, xp_ref, a_ref, lam_ref, da_ref, st_ref):
        i = pl.program_id(0)

        @pl.when(i == 0)
        def _():
            st_ref[...] = jnp.zeros_like(st_ref)
            da_ref[...] = jnp.zeros_like(da_ref)

        ar, ai = a_ref[0:8, :], a_ref[8:16, :]
        a2r, a2i = ar * ar - ai * ai, 2.0 * ar * ai

        def pair(t, c, p1r, p1i, p0r, p0i):
            lr, li, gr, gi = c
            d1r, d1i, d0r, d0i = d_ref[t, 0:8, :], d_ref[t, 8:16, :], d_ref[t - 1, 0:8, :], d_ref[t - 1, 8:16, :]
            n1r = d1r + ar * lr + ai * li
            n1i = d1i - ai * lr + ar * li
            cr = d0r + ar * d1r + ai * d1i
            ci = d0i - ai * d1r + ar * d1i
            n0r = cr + a2r * lr + a2i * li
            n0i = ci - a2i * lr + a2r * li
            lam_ref[t, 0:8, :] = n1r
            lam_ref[t, 8:16, :] = n1i
            lam_ref[t - 1, 0:8, :] = n0r
            lam_ref[t - 1, 8:16, :] = n0i
            gr = gr + ((n1r * p1r + n1i * p1i) + (n0r * p0r + n0i * p0i))
            gi = gi + ((n1i * p1r - n1r * p1i) + (n0i * p0r - n0r * p0i))
            return n0r, n0i, gr, gi

        def step(s, c):
            t = tb - 1 - 2 * s
            return pair(t, c, x_ref[t - 1, 0:8, :], x_ref[t - 1, 8:16, :], x_ref[t - 2, 0:8, :], x_ref[t - 2, 8:16, :])

        c = (st_ref[0:8, :], st_ref[8:16, :], jnp.zeros((8, 128), F32), jnp.zeros((8, 128), F32))
        c = lax.fori_loop(0, tb // 2 - 1, step, c, unroll=4)
        first = i == nblk - 1
        pr = jnp.where(first, 0.0, xp_ref[0, 0:8, :])
        pi = jnp.where(first, 0.0, xp_ref[0, 8:16, :])
        lr, li, gr, gi = pair(1, c, x_ref[0, 0:8, :], x_ref[0, 8:16, :], pr, pi)
        st_ref[0:8, :] = lr
        st_ref[8:16, :] = li
        da_ref[0:8, :] += gr
        da_ref[8:16, :] += gi

    blk = pl.BlockSpec((tb, 16, 128), lambda i: (nblk - 1 - i, 0, 0))
    prev = pl.BlockSpec((1, 16, 128), lambda i: (jnp.maximum((nblk - 1 - i) * tb - 1, 0), 0, 0))
    return pl.pallas_call(
        body, grid=(nblk,), in_specs=[blk, blk, prev, _whole((16, 128))], out_specs=[blk, _whole((16, 128))],
        out_shape=[jax.ShapeDtypeStruct((S, 16, 128), F32), jax.ShapeDtypeStruct((16, 128), F32)],
        scratch_shapes=[pltpu.VMEM((16, 128), F32)],
        compiler_params=_params(("arbitrary",)), name=name)(dxs, xs, xs, abar)


def _gelu(x):
    return 0.5 * x * (1.0 + jnp.tanh(0.7978845608028654 * (x + 0.044715 * x * x * x)))


def _gelu_grad(x):
    t = jnp.tanh(0.7978845608028654 * (x + 0.044715 * x * x * x))
    return 0.5 * (1.0 + t) + 0.5 * x * (1.0 - t * t) * 0.7978845608028654 * (1.0 + 3.0 * 0.044715 * x * x)


def _ssm_out(xs, proj_a, cfull, dskip, wglu, name):
    S = xs.shape[0]
    tm = min(512, S)
    C = D_SSM

    def body(xs_ref, u_ref, c_ref, d_ref, wg_ref, y1_ref, ms_ref):
        y1 = jnp.dot(xs_ref[...].astype(BF16), c_ref[...], preferred_element_type=F32) + d_ref[...] * u_ref[...]
        y2 = _gelu(y1)
        gl = jnp.dot(y2.astype(BF16), wg_ref[...], preferred_element_type=F32)
        y1_ref[...] = y1
        ms_ref[...] = (y2 * _sigmoid(gl)).astype(BF16)

    wga, wgs = _op(wglu, (C, C), lambda i: (0, 0))
    return pl.pallas_call(
        body, grid=(S // tm,),
        in_specs=[_rows(2 * 1024, 0, tm), _rows(C, 3, tm), _whole((2 * 1024, C)), _whole((1, C)), wgs],
        out_specs=[_rows(C, 0, tm), _rows(C, 0, tm)],
        out_shape=[jax.ShapeDtypeStruct((S, C), F32), jax.ShapeDtypeStruct((S, C), BF16)],
        compiler_params=_params(("parallel",)), name=name)(xs, proj_a, cfull, dskip, wga)


def _ssm_out_bwd(dms, y1, proj_a, cfull, dskip, wglu, name):
    S = y1.shape[0]
    tm = min(512, S)
    C = D_SSM

    def body(dms_ref, y1_ref, u_ref, c_ref, d_ref, wg_ref, dy1_ref, y2_ref, dgl_ref, dxs_ref, du_ref, dd_ref):
        @pl.when(pl.program_id(0) == 0)
        def _():
            dd_ref[...] = jnp.zeros_like(dd_ref)

        dms, y1 = dms_ref[...], y1_ref[...]
        y2 = _gelu(y1)
        y2b = y2.astype(BF16)
        sg = _sigmoid(jnp.dot(y2b, wg_ref[...], preferred_element_type=F32))
        dgl = (dms * y2 * sg * (1.0 - sg)).astype(BF16)
        dy2 = dms * sg + lax.dot_general(dgl, wg_ref[...], (((1,), (1,)), ((), ())), preferred_element_type=F32)
        dy1 = dy2 * _gelu_grad(y1)
        dy1b = dy1.astype(BF16)
        dy1_ref[...] = dy1b
        y2_ref[...] = y2b
        dgl_ref[...] = dgl
        dxs_ref[...] = lax.dot_general(dy1b, c_ref[...], (((1,), (1,)), ((), ())), preferred_element_type=F32)
        du_ref[...] = d_ref[...] * dy1
        dd_ref[...] += jnp.sum(dy1 * u_ref[...], axis=0, keepdims=True)

    wga, wgs = _op(wglu, (C, C), lambda i: (0, 0))
    rc = _rows(C, 0, tm)
    return pl.pallas_call(
        body, grid=(S // tm,),
        in_specs=[rc, rc, _rows(C, 3, tm), _whole((2 * 1024, C)), _whole((1, C)), wgs],
        out_specs=[rc, rc, rc, _rows(2 * 1024, 0, tm), rc, _whole((1, C))],
        out_shape=[jax.ShapeDtypeStruct((S, C), BF16), jax.ShapeDtypeStruct((S, C), BF16),
                   jax.ShapeDtypeStruct((S, C), BF16), jax.ShapeDtypeStruct((S, 2 * 1024), F32),
                   jax.ShapeDtypeStruct((S, C), F32), jax.ShapeDtypeStruct((1, C), F32)],
        compiler_params=_params(("arbitrary",)), name=name)(dms, y1, proj_a, cfull, dskip, wga)


_NT = (((1,), (1,)), ((), ()))
_TN = (((0,), (0,)), ((), ()))


def _attn_heads(q_ref, kp_ref, kc_ref, vp_ref, vc_ref):
    lane = lax.broadcasted_iota(jnp.int32, (BLOCK, 128), 1)
    kk = jnp.concatenate([kp_ref[...], kc_ref[...]], axis=0).astype(BF16)
    vv = jnp.concatenate([vp_ref[...], vc_ref[...]], axis=0).astype(BF16)
    kk_r, vv_r = pltpu.roll(kk, 64, axis=1), pltpu.roll(vv, 64, axis=1)
    heads = []
    for hq in range(N_Q_HEADS):
        j, e = hq // 2, hq % 2
        qj = (q_ref[:, 128 * j:128 * (j + 1)] * (HEAD_DIM ** -0.5)).astype(BF16)
        own = (lane >= 64) if e else (lane < 64)
        aligned = e == hq // 4
        heads.append((own, jnp.where(own, qj, jnp.zeros_like(qj)), kk if aligned else kk_r, vv if aligned else vv_r,
                      aligned))
    return heads


def _attn_probs(i, heads, s_ref):
    n = N_Q_HEADS * BLOCK
    s = jnp.concatenate([lax.dot_general(qm, ks, _NT, preferred_element_type=F32) for _, qm, ks, _, _ in heads],
                        axis=0)
    row = lax.broadcasted_iota(jnp.int32, (n, 2 * BLOCK), 0) & (BLOCK - 1)
    col = lax.broadcasted_iota(jnp.int32, (n, 2 * BLOCK), 1)
    mask = (col > row) & (col <= row + BLOCK) & ((col >= BLOCK) | (i > 0))
    s = jnp.where(mask, s, -1e30)
    sink = jnp.concatenate([jnp.full((BLOCK, 1), s_ref[0, hq], F32) for hq in range(N_Q_HEADS)], axis=0)
    m = jnp.maximum(jnp.max(s, axis=1, keepdims=True), sink)
    p = jnp.exp(s - m)
    es = jnp.exp(sink - m)
    inv = 1.0 / (jnp.sum(p, axis=1, keepdims=True) + es)
    return p * inv, es * inv


def _attn_fwd(proj_a, sinks, name):
    S = proj_a.shape[0]
    nb = S // BLOCK

    def body(q_ref, kp_ref, kc_ref, vp_ref, vc_ref, s_ref, out_ref):
        i = pl.program_id(0)
        heads = _attn_heads(q_ref, kp_ref, kc_ref, vp_ref, vc_ref)
        row = lax.broadcasted_iota(jnp.int32, (BLOCK, 2 * BLOCK), 0)
        col = lax.broadcasted_iota(jnp.int32, (BLOCK, 2 * BLOCK), 1)
        mask = (col > row) & (col <= row + BLOCK) & ((col >= BLOCK) | (i > 0))
        outs = []
        for hq, (_, qm, ks, vs, _) in enumerate(heads):
            s = jnp.where(mask, lax.dot_general(qm, ks, _NT, preferred_element_type=F32), -1e30)
            sink = s_ref[0, hq]
            m = jnp.maximum(jnp.max(s, axis=1, keepdims=True), sink)
            p = jnp.exp(s - m)
            inv = 1.0 / (jnp.sum(p, axis=1, keepdims=True) + jnp.exp(sink - m))
            outs.append(jnp.dot((p * inv).astype(BF16), vs, preferred_element_type=F32))
        for j in range(4):
            out_ref[:, 128 * j:128 * (j + 1)] = jnp.where(heads[2 * j][0], outs[2 * j], outs[2 * j + 1]).astype(BF16)

    prev = lambda c: pl.BlockSpec((BLOCK, 128), lambda i: (jnp.maximum(i - 1, 0), c))
    cur = lambda c: pl.BlockSpec((BLOCK, 128), lambda i: (i, c))
    return pl.pallas_call(
        body, grid=(nb,),
        in_specs=[_rows(D_ATTN, 2, BLOCK), prev(12), cur(12), prev(13), cur(13),
                  pl.BlockSpec(memory_space=pltpu.SMEM)],
        out_specs=_rows(D_ATTN, 0, BLOCK), out_shape=jax.ShapeDtypeStruct((S, D_ATTN), BF16),
        compiler_params=_params(("parallel",)), name=name)(proj_a, proj_a, proj_a, proj_a, proj_a, sinks)


def _attn_bwd(proj_a, dout, sinks, name):
    S = proj_a.shape[0]
    nb = S // BLOCK

    def body(q_ref, kp_ref, kc_ref, vp_ref, vc_ref, do_ref, s_ref, out_ref, dk_ref, dv_ref, ds_ref, ck_ref, cv_ref):
        i = pl.program_id(0)

        @pl.when(i == 0)
        def _():
            ds_ref[...] = jnp.zeros_like(ds_ref)
            ck_ref[...] = jnp.zeros_like(ck_ref)
            cv_ref[...] = jnp.zeros_like(cv_ref)

        @pl.when(i < nb)
        def _():
            heads = _attn_heads(q_ref, kp_ref, kc_ref, vp_ref, vc_ref)
            pn, psink = _attn_probs(i, heads, s_ref)
            doms = []
            for hq, (own, _, _, _, _) in enumerate(heads):
                doj = do_ref[:, 128 * (hq // 2):128 * (hq // 2 + 1)].astype(BF16)
                doms.append(jnp.where(own, doj, jnp.zeros_like(doj)))
            dp = jnp.concatenate([lax.dot_general(dom, vs, _NT, preferred_element_type=F32)
                                  for dom, (_, _, _, vs, _) in zip(doms, heads)], axis=0)
            delta = jnp.sum(pn * dp, axis=1, keepdims=True)
            dsb = (pn * (dp - delta)).astype(BF16)
            pnb = pn.astype(BF16)
            sk = psink * delta
            dkk = jnp.zeros((2 * BLOCK, 128), F32)
            dvv = jnp.zeros((2 * BLOCK, 128), F32)
            dqs = []
            for hq, (own, qm, ks, vs, aligned) in enumerate(heads):
                rows = slice(BLOCK * hq, BLOCK * (hq + 1))
                ds_ref[hq:hq + 1, :] += jnp.broadcast_to(-jnp.sum(sk[rows]), (1, 128))
                dqs.append(jnp.dot(dsb[rows], ks, preferred_element_type=F32) * (HEAD_DIM ** -0.5))
                dk = lax.dot_general(dsb[rows], qm, _TN, preferred_element_type=F32)
                dv = lax.dot_general(pnb[rows], doms[hq], _TN, preferred_element_type=F32)
                dkk = dkk + (dk if aligned else pltpu.roll(dk, 64, axis=1))
                dvv = dvv + (dv if aligned else pltpu.roll(dv, 64, axis=1))
            for j in range(4):
                out_ref[:, 128 * j:128 * (j + 1)] = jnp.where(heads[2 * j][0], dqs[2 * j], dqs[2 * j + 1]).astype(BF16)
            ck_ref[0:BLOCK, :] = ck_ref[BLOCK:, :] + dkk[0:BLOCK]
            cv_ref[0:BLOCK, :] = cv_ref[BLOCK:, :] + dvv[0:BLOCK]
            ck_ref[BLOCK:, :] = dkk[BLOCK:]
            cv_ref[BLOCK:, :] = dvv[BLOCK:]

        @pl.when(i == nb)
        def _():
            ck_ref[0:BLOCK, :] = ck_ref[BLOCK:, :]
            cv_ref[0:BLOCK, :] = cv_ref[BLOCK:, :]

        dk_ref[...] = ck_ref[0:BLOCK, :].astype(BF16)
        dv_ref[...] = cv_ref[0:BLOCK, :].astype(BF16)

    last = nb - 1
    prev = lambda c: pl.BlockSpec((BLOCK, 128), lambda i: (jnp.clip(i - 1, 0, last), c))
    cur = lambda c: pl.BlockSpec((BLOCK, 128), lambda i: (jnp.minimum(i, last), c))
    qrow = lambda w, c: pl.BlockSpec((BLOCK, w), lambda i: (jnp.minimum(i, last), c))

    dq, dk, dv, ds = pl.pallas_call(
        body, grid=(nb + 1,),
        in_specs=[qrow(D_ATTN, 2), prev(12), cur(12), prev(13), cur(13), qrow(D_ATTN, 0),
                  pl.BlockSpec(memory_space=pltpu.SMEM)],
        out_specs=[qrow(D_ATTN, 0), prev(0), prev(0), _whole((N_Q_HEADS, 128))],
        out_shape=[jax.ShapeDtypeStruct((S, D_ATTN), BF16), jax.ShapeDtypeStruct((S, D_KV), BF16),
                   jax.ShapeDtypeStruct((S, D_KV), BF16), jax.ShapeDtypeStruct((N_Q_HEADS, 128), F32)],
        scratch_shapes=[pltpu.VMEM((2 * BLOCK, 128), F32), pltpu.VMEM((2 * BLOCK, 128), F32)],
        compiler_params=_params(("arbitrary",)), name=name)(proj_a, proj_a, proj_a, proj_a, proj_a, dout, sinks)
    return dq, dk, dv, ds


_BR = ((0, D_CONV), (D_CONV, D_CONV + D_SSM), (D_CONV + D_SSM, D_CONV + D_SSM + D_ATTN))


def _branches(m_refs, wbr_ref):
    nt = (((1,), (1,)), ((), ()))
    return [lax.dot_general(m[...], wbr_ref[:, lo:hi], nt, preferred_element_type=F32)
            for m, (lo, hi) in zip(m_refs, _BR)]


def _merge_fwd(mc, ms, ma, proj_g, wbr_t, name):
    S, Dm = mc.shape[0], proj_g.shape[1] // 3
    tm = min(512, S)

    def body(mc_ref, ms_ref, ma_ref, g_ref, w_ref, out_ref):
        ys = _branches((mc_ref, ms_ref, ma_ref), w_ref)
        acc = None
        for b in range(3):
            t = _sigmoid(g_ref[:, b * Dm:(b + 1) * Dm].astype(F32)) * ys[b]
            acc = t if acc is None else acc + t
        out_ref[...] = acc.astype(BF16)

    wa, ws = _op(wbr_t, (Dm, Dm), lambda i: (0, 0))
    return pl.pallas_call(
        body, grid=(S // tm,),
        in_specs=[_rows(D_CONV, 0, tm), _rows(D_SSM, 0, tm), _rows(D_ATTN, 0, tm), _rows(3 * Dm, 0, tm), ws],
        out_specs=_rows(Dm, 0, tm), out_shape=jax.ShapeDtypeStruct((S, Dm), BF16),
        compiler_params=_params(("parallel",)), name=name)(mc, ms, ma, proj_g, wa)


def _merge_bwd(dmerged, mc, ms, ma, proj_g, wbr_t, name):
    S, Dm = mc.shape[0], proj_g.shape[1] // 3
    tm = min(256, S)

    def body(d_ref, mc_ref, ms_ref, ma_ref, g_ref, w_ref, dg_ref, dy_ref, dmc_ref, dms_ref, dma_ref):
        ys = _branches((mc_ref, ms_ref, ma_ref), w_ref)
        d = d_ref[...]
        for b, (o_ref, (lo, hi)) in enumerate(zip((dmc_ref, dms_ref, dma_ref), _BR)):
            g = _sigmoid(g_ref[:, b * Dm:(b + 1) * Dm].astype(F32))
            dg_ref[:, b * Dm:(b + 1) * Dm] = (d * ys[b] * g * (1.0 - g)).astype(BF16)
            dyb = (g * d).astype(BF16)
            dy_ref[:, b * Dm:(b + 1) * Dm] = dyb
            o_ref[...] = jnp.dot(dyb, w_ref[:, lo:hi], preferred_element_type=F32)

    wa, ws = _op(wbr_t, (Dm, Dm), lambda i: (0, 0))
    return pl.pallas_call(
        body, grid=(S // tm,),
        in_specs=[_rows(Dm, 0, tm), _rows(D_CONV, 0, tm), _rows(D_SSM, 0, tm), _rows(D_ATTN, 0, tm),
                  _rows(3 * Dm, 0, tm), ws],
        out_specs=[_rows(3 * Dm, 0, tm), _rows(3 * Dm, 0, tm), _rows(D_CONV, 0, tm), _rows(D_SSM, 0, tm),
                   _rows(D_ATTN, 0, tm)],
        out_shape=[jax.ShapeDtypeStruct((S, 3 * Dm), BF16), jax.ShapeDtypeStruct((S, 3 * Dm), BF16),
                   jax.ShapeDtypeStruct((S, D_CONV), F32), jax.ShapeDtypeStruct((S, D_SSM), F32),
                   jax.ShapeDtypeStruct((S, D_ATTN), F32)],
        compiler_params=_params(("parallel",)), name=name)(dmerged, mc, ms, ma, proj_g, wa)


def _loss_head(y, target):
    S, Dm = y.shape
    tm = min(512, S)

    def body(y_ref, t_ref, dy_ref, l_ref):
        @pl.when(pl.program_id(0) == 0)
        def _():
            l_ref[...] = jnp.zeros_like(l_ref)

        e = y_ref[...] - t_ref[...]
        dy_ref[...] = e * (1.0 / Dm)
        l_ref[...] += jnp.broadcast_to(0.5 * jnp.sum(jnp.sum(e * e, axis=1, keepdims=True) * (1.0 / Dm)), (1, 128))

    return pl.pallas_call(
        body, grid=(S // tm,), in_specs=[_rows(Dm, 0, tm), _rows(Dm, 0, tm)],
        out_specs=[_rows(Dm, 0, tm), _whole((1, 128))],
        out_shape=[jax.ShapeDtypeStruct((S, Dm), F32), jax.ShapeDtypeStruct((1, 128), F32)],
        compiler_params=_params(("arbitrary",)), name="loss_head")(y, target)


def _view2d(shape):
    n = math.prod(shape)
    if shape[-1] % 128 == 0:
        return (n // shape[-1], shape[-1])
    if n >= (1 << 16) and len(shape) == 3:
        return (shape[0] * shape[1], shape[2])
    if n % 128 == 0:
        return (n // 128, 128)
    return (1, n)


def _adamw(w, g, m, v, name, blocks_3d=False):
    shape = w.shape
    c1 = 1.0 - ADAM_B1 ** ADAM_STEP
    c2 = 1.0 - ADAM_B2 ** ADAM_STEP

    def body(w_ref, g_ref, m_ref, v_ref, d_ref, nm_ref, nv_ref):
        g = g_ref[...]
        nm = ADAM_B1 * m_ref[...] + (1.0 - ADAM_B1) * g
        nv = ADAM_B2 * v_ref[...] + (1.0 - ADAM_B2) * (g * g)
        d_ref[...] = -ADAM_LR * ((nm / c1) / (jnp.sqrt(nv / c2) + ADAM_EPS) + ADAM_WD * w_ref[...])
        nm_ref[...] = nm
        nv_ref[...] = nv

    if blocks_3d:
        Ls, R, C = shape
        tm = max(t for t in range(8, min(R, 512) + 1, 8) if R % t == 0)
        blk = pl.BlockSpec((None, tm, C), lambda l, i: (l, i, 0))
        return pl.pallas_call(
            body, grid=(Ls, R // tm), in_specs=[blk] * 4, out_specs=[blk] * 3,
            out_shape=[jax.ShapeDtypeStruct(shape, F32)] * 3,
            compiler_params=_params(("parallel", "parallel")), name=name)(w, g, m, v)
    R, C = _view2d(shape)
    tm = R
    for cand in (512, 352, 256):
        if R > cand and R % cand == 0:
            tm = cand
            break
    blk = _rows(C, 0, tm)
    outs = pl.pallas_call(
        body, grid=(R // tm,), in_specs=[blk] * 4, out_specs=[blk] * 3,
        out_shape=[jax.ShapeDtypeStruct((R, C), F32)] * 3,
        compiler_params=_params(("parallel",)), name=name)(*[t.reshape(R, C) for t in (w, g, m, v)])
    return [o.reshape(shape) for o in outs]


def _coords():
    return lax.axis_index("x"), lax.axis_index("y"), lax.axis_index("c")


def _other_chips(x, y):
    return [((1 - x, y), 2 * (1 - x) + y), ((x, 1 - y), 2 * x + 1 - y), ((1 - x, 1 - y), 2 * (1 - x) + 1 - y)]


def _comm_call(comm, name):
    return _call(None, grid=(), in_specs=[], out_specs=[], out_shape=[], name=name, ops=[], comm=comm)[1]


def _remote(src, dst, send_sems, recv_sems, k, dev):
    return pltpu.make_async_remote_copy(src_ref=src, dst_ref=dst, send_sem=send_sems.at[k], recv_sem=recv_sems.at[k],
                                        device_id=dev, device_id_type=MESH)


def _half(ref, chip, c, rp):
    return ref.at[pl.ds(pl.multiple_of(chip * rp + c * (rp // 2), 16), rp // 2), :]


def _gather_ici(shards, fulls, l):
    n = len(shards)

    def build(refs, fresh, send_sems, recv_sems):
        x, y, c = _coords()
        me = 2 * x + y
        sends, recvs = [], []
        for a in range(n):
            rp = shards[a].shape[1]
            src = refs[a].at[l].at[pl.ds(pl.multiple_of(c * (rp // 2), 16), rp // 2), :]
            for r, ((px, py), chip) in enumerate(_other_chips(x, y)):
                k = 3 * a + r
                sends.append(_remote(src, _half(refs[n + a], me, c, rp), send_sems, recv_sems, k, (px, py, c)))
                recvs.append(_remote(src, _half(refs[n + a], chip, c, rp), send_sems, recv_sems, k, (px, py, c)))
        return sends, recvs

    return build, list(shards) + list(fulls), [n + a for a in range(n)], 3 * n


def _gather_d2d(fulls, shards, l):
    n = len(fulls)

    def build(refs, fresh, send_sems, recv_sems):
        x, y, c = _coords()
        me = 2 * x + y
        sends, recvs = [], []
        for a in range(n):
            rp = fulls[a].shape[0] // N_CHIPS
            for r, (_, chip) in enumerate(_other_chips(x, y)):
                k = 4 * a + r
                mine, theirs = _half(refs[a], chip, c, rp), _half(refs[a], chip, 1 - c, rp)
                sends.append(_remote(mine, mine, send_sems, recv_sems, k, (x, y, 1 - c)))
                recvs.append(_remote(theirs, theirs, send_sems, recv_sems, k, (x, y, 1 - c)))
            own = refs[a].at[pl.ds(pl.multiple_of(me * rp, 16), rp), :]
            sends.append(_remote(refs[n + a].at[l], own, send_sems, recv_sems, 4 * a + 3, (x, y, 1 - c)))
            recvs.append(_remote(refs[n + a].at[l], own, send_sems, recv_sems, 4 * a + 3, (x, y, 1 - c)))
        return sends, recvs

    return build, list(fulls) + list(shards), list(range(n)), 4 * n


def _pair_sum(a, b, c_idx, half_rows, out_dtype, name):
    n4, rp, W = a.shape
    tr = half_rows // 2
    nblk = half_rows // tr

    def body(c_ref, a_ref, b_ref, o_ref):
        o_ref[...] = (a_ref[...] + b_ref[...]).astype(o_ref.dtype)

    return pl.pallas_call(
        body,
        grid_spec=pltpu.PrefetchScalarGridSpec(
            num_scalar_prefetch=1, grid=(nblk,),
            in_specs=[pl.BlockSpec((n4, tr, W), lambda i, c: (0, c[0] * nblk + i, 0)),
                      pl.BlockSpec((n4, tr, W), lambda i, c: (0, i, 0))],
            out_specs=pl.BlockSpec((n4, tr, W), lambda i, c: (0, i, 0))),
        out_shape=jax.ShapeDtypeStruct((n4, half_rows, W), out_dtype),
        compiler_params=_params(("parallel",)), name=name)(c_idx, a, b)


def _sum4(land, own, me_c, stacked, l, name):
    _, R, W = land.shape
    tr = R // 2

    def body(s_ref, l_ref, o_ref, stacked_ref, out_ref):
        me = s_ref[0]
        acc = None
        for i in range(N_CHIPS):
            t = jnp.where(me == i, o_ref[...], l_ref[i]).astype(F32)
            acc = t if acc is None else acc + t
        out_ref[...] = acc

    return pl.pallas_call(
        body,
        grid_spec=pltpu.PrefetchScalarGridSpec(
            num_scalar_prefetch=1, grid=(R // tr,),
            in_specs=[pl.BlockSpec((N_CHIPS, tr, W), lambda i, s: (0, i, 0)),
                      pl.BlockSpec((None, tr, W), lambda i, s: (s[0], i, 0)),
                      pl.BlockSpec(memory_space=pltpu.HBM)],
            out_specs=pl.BlockSpec((None, None, tr, W), lambda i, s: (l, s[1], i, 0))),
        out_shape=jax.ShapeDtypeStruct(stacked.shape, F32), input_output_aliases={3: 0},
        compiler_params=_params(("parallel",)), name=name)(me_c, land, own, stacked)


def _reduce_d2d(g4):
    n = len(g4)

    def build(refs, fresh, send_sems, recv_sems):
        x, y, c = _coords()
        sends, recvs = [], []
        for a in range(n):
            hr = g4[a].shape[1] // 2
            src = refs[a].at[:, pl.ds(pl.multiple_of((1 - c) * hr, 8), hr), :]
            sends.append(_remote(src, fresh[a], send_sems, recv_sems, a, (x, y, 1 - c)))
            recvs.append(_remote(src, fresh[a], send_sems, recv_sems, a, (x, y, 1 - c)))
        return sends, recvs

    outs = [jax.ShapeDtypeStruct((N_CHIPS, g.shape[1] // 2, g.shape[2]), F32) for g in g4]
    return build, list(g4), outs, n


def _reduce_ici(q, lands):
    n = len(q)

    def build(refs, fresh, send_sems, recv_sems):
        x, y, c = _coords()
        me = 2 * x + y
        sends, recvs = [], []
        for a in range(n):
            for r, ((px, py), chip) in enumerate(_other_chips(x, y)):
                k = 3 * a + r
                sends.append(_remote(refs[a].at[chip], refs[n + a].at[me], send_sems, recv_sems, k, (px, py, c)))
                recvs.append(_remote(refs[a].at[chip], refs[n + a].at[chip], send_sems, recv_sems, k, (px, py, c)))
        return sends, recvs

    return build, list(q) + list(lands), [n + a for a in range(n)], 3 * n


def _reduce_share(r, l):
    n = len(r)

    def build(refs, fresh, send_sems, recv_sems):
        x, y, c = _coords()
        sends, recvs = [], []
        for a in range(n):
            mine, theirs = refs[a].at[l].at[c], refs[a].at[l].at[1 - c]
            sends.append(_remote(mine, mine, send_sems, recv_sems, a, (x, y, 1 - c)))
            recvs.append(_remote(mine, theirs, send_sems, recv_sems, a, (x, y, 1 - c)))
        return sends, recvs

    return build, list(r), list(range(n)), n


class _SemOffset:
    def __init__(self, sems, off):
        self.sems, self.off = sems, off

    @property
    def at(self):
        return self

    def __getitem__(self, k):
        return self.sems.at[k + self.off]


def _merge(c1, c2):
    b1, i1, o1, s1 = c1
    b2, i2, o2, s2 = c2
    n1, f1 = len(i1), sum(not isinstance(o, int) for o in o1)

    def build(refs, fresh, send_sems, recv_sems):
        sa, ra = b1(refs[:n1], fresh[:f1], send_sems, recv_sems)
        sb, rb = b2(refs[n1:], fresh[f1:], _SemOffset(send_sems, s1), _SemOffset(recv_sems, s1))
        return sa + sb, ra + rb

    return build, list(i1) + list(i2), list(o1) + [o + n1 if isinstance(o, int) else o for o in o2], s1 + s2


class _Weave:
    A, B, C = ("wg1", "wu1", "wd1"), ("win", "wbr", "wout", "wglu"), ("wg2", "wu2", "wd2")
    X, Y = C + B, A
    ICI_X = {"dwg1": ("wg2", "wu2"), "dwu1": ("wd2", "wbr", "wout", "wglu"), "dwd1": ("win",)}
    ICI_Y = {"dwg2": ("wg1", "wu1"), "dwu2": ("wd1",)}

    def __init__(self, shards, W, L):
        self.shards, self.W, self.L = shards, W, L
        x, y, c = _coords()
        self.c_idx = jnp.reshape(c, (1,)).astype(jnp.int32)
        self.me_c = jnp.stack([2 * x + y, c]).astype(jnp.int32)
        self.final = {k: lax.empty((L, 2, s.shape[1] // 2, s.shape[2]), F32) for k, s in shards.items()}
        self.x = self.y = None
        self.parts = []

    def _ici(self, keys, l):
        def cb(outs):
            for k, t in zip(keys, outs):
                self.W[k][l] = t
        return _gather_ici([self.shards[k] for k in keys], [self.W[k][l] for k in keys], l), cb

    def _d2d(self, keys, l):
        def cb(outs):
            for k, t in zip(keys, outs):
                self.W[k][l] = t
        return _gather_d2d([self.W[k][l] for k in keys], [self.shards[k] for k in keys], l), cb

    def _r1(self, st, keys):
        def cb(outs):
            st["q"] = {k: _pair_sum(st["g4"][k], t, self.c_idx, t.shape[1], BF16, "reduce_pair_sum")
                       for k, t in zip(keys, outs)}
        return _reduce_d2d([st["g4"][k] for k in keys]), cb

    def _r2(self, st, keys):
        def cb(outs):
            for k, t in zip(keys, outs):
                self.final[k] = _sum4(t, st["q"][k], self.me_c, self.final[k], st["layer"], "reduce_sum4")
        return _reduce_ici([st["q"][k] for k in keys], [jnp.zeros(st["q"][k].shape, BF16) for k in keys]), cb

    def _r3(self, st, keys):
        def cb(outs):
            for k, t in zip(keys, outs):
                self.final[k] = t
        return _reduce_share([self.final[k] for k in keys], st["layer"]), cb

    def _pieces(self, site, l):
        nxt = l is not None and l + 1 < self.L
        if site == "up1":
            return [self._ici(self.B, l)]
        if site == "down1":
            return [self._d2d(self.B, l), self._ici(self.C, l)]
        if site == "proj_g":
            return [self._d2d(self.C, l)] + ([self._ici(self.A, l + 1)] if nxt else [])
        if site == "up2":
            return [self._d2d(self.A, l + 1)] if nxt else []
        if site == "dx1" and self.x is not None:
            return [self._r1(self.x, self.X)]
        if site in self.ICI_X and self.x is not None:
            return [self._r2(self.x, self.ICI_X[site])]
        if site == "dx2":
            ps = [self._r3(self.x, self.X)] if self.x is not None else []
            return ps + ([self._r1(self.y, self.Y)] if self.y is not None else [])
        if site in self.ICI_Y and self.y is not None:
            return [self._r2(self.y, self.ICI_Y[site])]
        if site == "dwd2" and self.y is not None:
            return [self._r3(self.y, self.Y)]
        return []

    def take(self, site, l):
        self.parts = self._pieces(site, l)
        if not self.parts:
            return None
        comm = self.parts[0][0]
        for c, _ in self.parts[1:]:
            comm = _merge(comm, c)
        return comm

    def done(self, site, l, outs):
        i = 0
        for comm, cb in self.parts:
            cb(outs[i:i + len(comm[2])])
            i += len(comm[2])
        if site == "dx2":
            self.x = None
        if site == "dwd2":
            self.y = None

    def gather_first(self):
        for piece, name in ((self._ici, "gather0_ici"), (self._d2d, "gather0_d2d")):
            comm, cb = piece(self.A, 0)
            cb(_comm_call(comm, name))

    def _state(self, l, grads):
        return dict(layer=l, g4={k: g.reshape(N_CHIPS, g.shape[0] // N_CHIPS, g.shape[1]) for k, g in grads.items()})

    def grads_x(self, l, grads):
        assert self.x is None
        self.x = self._state(l, grads)

    def grads_y(self, l, grads):
        assert self.y is None
        self.y = self._state(l, grads)

    def flush(self):
        for site, name in (("dx2", "reduce_tail_d2d"), ("dwg2", "reduce_tail_ici"), ("dwu2", "reduce_tail_ici"),
                           ("dwd2", "reduce_tail_share")):
            self.done(site, None, _comm_call(self.take(site, None), name))


def _all_reduce_small(buf, name):
    R = buf.shape[0]

    def build(refs, fresh, send_sems, recv_sems):
        x, y, c = _coords()
        me = 4 * x + 2 * y + c
        sends, recvs = [], []
        k = 0
        for fx in range(2):
            for fy in range(2):
                for fc in range(2):
                    if fx + fy + fc == 0:
                        continue
                    px, py, pc = x ^ fx, y ^ fy, c ^ fc
                    sends.append(_remote(refs[0], refs[1].at[me], send_sems, recv_sems, k, (px, py, pc)))
                    recvs.append(_remote(refs[0], refs[1].at[4 * px + 2 * py + pc], send_sems, recv_sems, k,
                                         (px, py, pc)))
                    k += 1
        return sends, recvs

    got = _comm_call((build, [buf, jnp.zeros((N_DEV, R, 128), F32)], [1], N_DEV - 1), name + "_gather")[0]
    x, y, c = _coords()
    me = jnp.reshape(4 * x + 2 * y + c, (1,)).astype(jnp.int32)

    def body(s_ref, m_ref, own_ref, o_ref):
        acc = None
        for d in range(N_DEV):
            t = jnp.where(s_ref[0] == d, own_ref[...], m_ref[d])
            acc = t if acc is None else acc + t
        o_ref[...] = acc

    tr = max([t for t in range(8, min(R, 1024) + 1, 8) if R % t == 0], default=R)
    return pl.pallas_call(
        body,
        grid_spec=pltpu.PrefetchScalarGridSpec(
            num_scalar_prefetch=1, grid=(R // tr,),
            in_specs=[pl.BlockSpec((N_DEV, tr, 128), lambda i, s: (0, i, 0)),
                      pl.BlockSpec((tr, 128), lambda i, s: (i, 0))],
            out_specs=pl.BlockSpec((tr, 128), lambda i, s: (i, 0))),
        out_shape=jax.ShapeDtypeStruct((R, 128), F32),
        compiler_params=_params(("parallel",)), name=name + "_sum")(me, got, buf)


def _ssm_layouts(p, L):
    G = L * N_GROUPS
    lr = p["ssm_a_re"].reshape(G, 1, SSM_STATE)
    li = p["ssm_a_im"].reshape(G, 1, SSM_STATE)
    ldt = p["ssm_log_dt"].reshape(G, 1, 1)
    br = jnp.swapaxes(p["ssm_b_re"], 2, 3).reshape(G, SSM_GROUP, SSM_STATE)
    bi = jnp.swapaxes(p["ssm_b_im"], 2, 3).reshape(G, SSM_GROUP, SSM_STATE)
    are, aim, bre, bim = _disc_fwd(lr, li, ldt, br, bi)
    eye = jnp.eye(N_GROUPS, dtype=F32)
    abar = jnp.concatenate([are.reshape(L, 8, 128), aim.reshape(L, 8, 128)], axis=1)

    def b_blk(t):
        return jnp.einsum("lgcp,gh->lgchp", t.reshape(L, N_GROUPS, SSM_GROUP, SSM_STATE), eye).reshape(L, 256, 1024)

    def c_blk(t):
        return jnp.einsum("lgcp,gh->lgphc", t, eye).reshape(L, 1024, 256)

    bfull = jnp.concatenate([b_blk(bre), b_blk(bim)], axis=2).astype(BF16)
    cfull = jnp.concatenate([c_blk(p["ssm_c_re"]), -c_blk(p["ssm_c_im"])], axis=1).astype(BF16)
    return (lr, li, ldt, br, bi), abar, bfull, cfull


def _local_step(x, target, W, p, L, weave=None):
    S, Dm = x.shape
    dff = W["wg1"][0].shape[0]
    alpha = (2.0 * L) ** 0.25
    disc_in, abar, bfull, cfull = _ssm_layouts(p, L)
    row = lambda t, l: t[l][None]

    def carry(fn, site, l, *args, **kw):
        comm = weave.take(site, l) if weave is not None else None
        if comm is None:
            return fn(*args, **kw)
        res, couts = fn(*args, comm=comm, **kw)
        weave.done(site, l, couts)
        return res

    saved = []
    h, hb = x, x.astype(BF16)
    for l in range(L):
        sv = {"x0": hb}
        a1, b1, h1 = carry(_ffn_up, "up1", l, hb, W["wg1"][l], W["wu1"][l], dff, "ffn_up")
        x1, x1b, xh1, rs1 = carry(_mm_ln, "down1", l, h1, W["wd1"][l], h, row(p["ln1_g"], l), row(p["ln1_b"], l), 0.5,
                                  alpha, "ffn_down_ln", 1408)
        sv.update(a1=a1, b1=b1, h1=h1, x1=x1b, xh1=xh1, rs1=rs1)
        proj_a = _mm([(x1b, W["win"][l])], "nt", S, D_A, Dm, [F32], name="proj_a", tm=1024, tn=896, tk=1024)
        proj_g = carry(_mm, "proj_g", l, [(x1b, W["win"][l])], "nt", S, 3 * Dm, Dm, [BF16], name="proj_g", tm=2048,
                       tn=256, tk=1024, b_off=D_A)
        mc = _conv_fwd(proj_a, p["conv_w"][l], row(p["conv_b"], l), "conv_fwd")
        bu = _mm([(proj_a, (bfull, l))], "nn", S, 2048, D_SSM, [F32], name="ssm_bu", tm=1024, tn=1024, tk=256,
                 a_off=3 * D_CONV)
        xs3 = _scan_fwd(bu.reshape(S, 16, 128), abar[l], "scan_fwd")
        xs = xs3.reshape(S, 2048)
        y1, ms = _ssm_out(xs, proj_a, cfull[l], row(p["ssm_d"], l), W["wglu"][l], "ssm_out")
        sinks = p["attn_sinks"][l][None]
        ma = _attn_fwd(proj_a, sinks, "attn_fwd")
        merged = _merge_fwd(mc, ms, ma, proj_g, W["wbr"][l], "merge_fwd")
        x2, x2b, xh2, rs2 = _mm_ln(merged, W["wout"][l], x1, row(p["ln2_g"], l), row(p["ln2_b"], l), 1.0, alpha,
                                   "mix_out_ln", 1024)
        sv.update(proj_a=proj_a, proj_g=proj_g, mc=mc, ms=ms, ma=ma, xs=xs, xs3=xs3, y1=y1, merged=merged, x2=x2b,
                  xh2=xh2, rs2=rs2)
        a2, b2, h2 = carry(_ffn_up, "up2", l, x2b, W["wg2"][l], W["wu2"][l], dff, "ffn_up")
        h, hb, xh3, rs3 = _mm_ln(h2, W["wd2"][l], x2, row(p["ln3_g"], l), row(p["ln3_b"], l), 0.5, alpha,
                                 "ffn_down_ln", 1408)
        sv.update(a2=a2, b2=b2, h2=h2, xh3=xh3, rs3=rs3)
        saved.append(sv)

    dy, loss = _loss_head(h, target)
    big = [None] * L
    small = {k: [None] * L for k in ("ln1_g", "ln1_b", "ln2_g", "ln2_b", "ln3_g", "ln3_b", "conv_w", "conv_b", "ssm_d",
                                     "attn_sinks", "dabar", "dbfull", "dcfull")}

    def ffn_bwd(dy_out, x_in, a, b, hh, xh, rs, g, wg, wu, wd, l, sites):
        dres, df, da, dbb, dg, db = _ffn_dh_ln(dy_out, xh, rs, g, wd[l], a, b, alpha, "ffn_dh_ln")
        dx = carry(_mm, sites[0], l, [(da, wg[l]), (dbb, wu[l])], "nn", S, Dm, dff, [F32], name="ffn_dx", tm=1024,
                   tn=1024, tk=dff // 2, add=dres)
        tn_kw = dict(tm=1408, tn=1024, tk=2048)
        dwg = carry(_mm, sites[1], l, [(da, x_in)], "tn", dff, Dm, S, [F32], name="ffn_dw_up", **tn_kw)
        dwu = carry(_mm, sites[2], l, [(dbb, x_in)], "tn", dff, Dm, S, [F32], name="ffn_dw_up", **tn_kw)
        dwd = carry(_mm, sites[3], l, [(hh, df)], "tn", dff, Dm, S, [F32], name="ffn_dw_down", **tn_kw)
        return dx, dwg, dwu, dwd, dg, db

    for l in reversed(range(L)):
        sv = saved[l]
        dx2, dwg2, dwu2, dwd2, small["ln3_g"][l], small["ln3_b"][l] = ffn_bwd(
            dy, sv["x2"], sv["a2"], sv["b2"], sv["h2"], sv["xh3"], sv["rs3"], row(p["ln3_g"], l), W["wg2"], W["wu2"],
            W["wd2"], l, ("dx2", "dwg2", "dwu2", "dwd2"))
        dres2, dmix, small["ln2_g"][l], small["ln2_b"][l] = _ln_bwd(dx2, sv["xh2"], sv["rs2"], row(p["ln2_g"], l), 1.0,
                                                                   alpha, "ln_bwd")
        dwout = carry(_mm, "dw_out", l, [(sv["merged"], dmix)], "tn", Dm, Dm, S, [F32], name="dw_out", tm=1024, tn=1024,
                      tk=2048)
        dmerged = _mm([(dmix, W["wout"][l])], "nt", S, Dm, Dm, [F32], name="d_merged", tm=1024, tn=1024, tk=512)
        dgates, dyb, dmc, dms, dma = _merge_bwd(dmerged, sv["mc"], sv["ms"], sv["ma"], sv["proj_g"], W["wbr"][l],
                                                "merge_bwd")
        dwbr = jnp.concatenate([
            _mm([(dyb, m)], "tn", Dm, hi - lo, S, [F32], name="dw_br", tm=1024, tn=512, tk=2048, a_off=b * Dm)
            for b, ((lo, hi), m) in enumerate(zip(_BR, (sv["mc"], sv["ms"], sv["ma"])))], axis=1)
        proj_a = sv["proj_a"]
        d_conv, small["conv_w"][l], small["conv_b"][l] = _conv_bwd(proj_a, dmc, p["conv_w"][l], row(p["conv_b"], l),
                                                                  "conv_bwd")
        dy1, y2, dgl, dxs, du_skip, small["ssm_d"][l] = _ssm_out_bwd(dms, sv["y1"], proj_a, cfull[l],
                                                                    row(p["ssm_d"], l), W["wglu"][l], "ssm_out_bwd")
        dwglu = _mm([(y2, dgl)], "tn", D_SSM, D_SSM, S, [F32], name="dw_glu", tk=2048)
        small["dcfull"][l] = _mm([(sv["xs"], dy1)], "tn", 2048, D_SSM, S, [F32], name="d_cfull", tm=1024, tk=1024)
        lam, small["dabar"][l] = _scan_bwd(dxs.reshape(S, 16, 128), sv["xs3"], abar[l], "scan_bwd")
        lam = lam.reshape(S, 2048)
        du = _mm([(lam, (bfull, l))], "nt", S, D_SSM, 2048, [BF16], name="ssm_du", tm=1024, tk=512, add=du_skip)
        small["dbfull"][l] = _mm([(proj_a, lam)], "tn", D_SSM, 2048, S, [F32], name="d_bfull", tm=256, tn=1024,
                                 tk=1024, a_off=3 * D_CONV)
        dq, dk, dv, dsk = _attn_bwd(proj_a, dma, p["attn_sinks"][l][None], "attn_bwd")
        small["attn_sinks"][l] = dsk[:, 0]
        dproj = jnp.concatenate([d_conv, du, dq, dk, dv, dgates], axis=1)
        dx1 = carry(_mm, "d_x1", l, [(dproj, W["win"][l])], "nn", S, Dm, D_A + 3 * Dm, [F32], name="d_x1", tm=1024,
                    tn=1024, tk=(D_A + 3 * Dm) // 2, add=dres2)
        dwin = carry(_mm, "dw_in", l, [(dproj, sv["x1"])], "tn", D_A + 3 * Dm, Dm, S, [F32], name="dw_in", tm=2432,
                     tn=1024, tk=1024)
        if weave is not None:
            weave.grads_x(l, dict(wg2=dwg2, wu2=dwu2, wd2=dwd2, win=dwin, wbr=dwbr, wout=dwout, wglu=dwglu))
        dx0, dwg1, dwu1, dwd1, small["ln1_g"][l], small["ln1_b"][l] = ffn_bwd(
            dx1, sv["x0"], sv["a1"], sv["b1"], sv["h1"], sv["xh1"], sv["rs1"], row(p["ln1_g"], l), W["wg1"], W["wu1"],
            W["wd1"], l, ("dx1", "dwg1", "dwu1", "dwd1"))
        big[l] = dict(wg1=dwg1, wu1=dwu1, wd1=dwd1, win=dwin, wbr=dwbr, wout=dwout, wg2=dwg2, wu2=dwu2, wd2=dwd2,
                      wglu=dwglu)
        if weave is not None:
            weave.grads_y(l, dict(wg1=dwg1, wu1=dwu1, wd1=dwd1))
        dy = dx0

    small = {k: jnp.stack(v) for k, v in small.items()}
    eye = jnp.eye(N_GROUPS, dtype=F32)
    dabar = small.pop("dabar")
    dbf = small.pop("dbfull").reshape(L, N_GROUPS, SSM_GROUP, 2, N_GROUPS, SSM_STATE)
    dbbar = jnp.einsum("lgcrhp,gh->rlgcp", dbf, eye).reshape(2, L * N_GROUPS, SSM_GROUP, SSM_STATE)
    dcf = small.pop("dcfull").reshape(L, 2, N_GROUPS, SSM_STATE, N_GROUPS, SSM_GROUP)
    dc = jnp.einsum("lrgphc,gh->rlgcp", dcf, eye)
    G = L * N_GROUPS
    cts = (dabar[:, 0:8].reshape(G, 1, SSM_STATE), dabar[:, 8:16].reshape(G, 1, SSM_STATE), dbbar[0], dbbar[1])
    dlr, dli, dldt, dbr, dbi = _disc_bwd(*disc_in, cts)
    shp_b = (L, N_GROUPS, SSM_GROUP, SSM_STATE)
    small.update(
        ssm_a_re=dlr.reshape(L, N_GROUPS, SSM_STATE), ssm_a_im=dli.reshape(L, N_GROUPS, SSM_STATE),
        ssm_log_dt=dldt.reshape(L, N_GROUPS), ssm_b_re=jnp.swapaxes(dbr.reshape(shp_b), 2, 3),
        ssm_b_im=jnp.swapaxes(dbi.reshape(shp_b), 2, 3), ssm_c_re=dc[0], ssm_c_im=-dc[1],
        ln1_g=small["ln1_g"][:, 0], ln1_b=small["ln1_b"][:, 0], ln2_g=small["ln2_g"][:, 0],
        ln2_b=small["ln2_b"][:, 0], ln3_g=small["ln3_g"][:, 0], ln3_b=small["ln3_b"][:, 0],
        conv_b=small["conv_b"][:, 0], ssm_d=small["ssm_d"][:, 0])
    return loss, dy, big, small


_SMALL_ORDER = ("ln1_g", "ln1_b", "ln2_g", "ln2_b", "ln3_g", "ln3_b", "conv_w", "conv_b", "ssm_a_re", "ssm_a_im",
                "ssm_log_dt", "ssm_b_re", "ssm_b_im", "ssm_c_re", "ssm_c_im", "ssm_d", "attn_sinks")
_BIG_ORDER = ("wg1", "wu1", "wd1", "win", "wbr", "wout", "wg2", "wu2", "wd2", "wglu")
_WEIGHTS = ("ffn1_w_gate", "ffn1_w_up", "ffn1_w_down", "ln1_g", "ln1_b", "w_in", "conv_w", "conv_b", "ssm_a_re",
            "ssm_a_im", "ssm_log_dt", "ssm_b_re", "ssm_b_im", "ssm_c_re", "ssm_c_im", "ssm_d", "ssm_w_glu",
            "attn_sinks", "w_br_conv", "w_br_ssm", "w_br_attn", "w_out", "ln2_g", "ln2_b", "ffn2_w_gate",
            "ffn2_w_up", "ffn2_w_down", "ln3_g", "ln3_b")


def _weight_shards(w):
    t = lambda a: jnp.swapaxes(a, 1, 2).astype(BF16)
    wbr = jnp.concatenate([t(w["w_br_conv"]), t(w["w_br_ssm"]), t(w["w_br_attn"])], axis=2)
    return dict(wg1=t(w["ffn1_w_gate"]), wu1=t(w["ffn1_w_up"]), wd1=w["ffn1_w_down"].astype(BF16), win=t(w["w_in"]),
                wbr=wbr, wout=w["w_out"].astype(BF16), wg2=t(w["ffn2_w_gate"]), wu2=t(w["ffn2_w_up"]),
                wd2=w["ffn2_w_down"].astype(BF16), wglu=w["ssm_w_glu"].astype(BF16))


def _gather_conv_w(conv_w, chip):
    L = conv_w.shape[0]
    n = L * 3 * 64
    slots = lax.dynamic_update_slice(jnp.zeros((N_CHIPS, n), F32), conv_w.reshape(1, n), (chip, 0))
    got = (_all_reduce_small(slots.reshape(-1, 128), "conv_w_gather") * 0.5).reshape(N_CHIPS, L, 3, 64)
    return jnp.transpose(got, (1, 2, 0, 3)).reshape(L, 3, N_CHIPS * 64)


def kernel(x, ffn1_w_gate, ffn1_w_up, ffn1_w_down, ln1_g, ln1_b, w_in, conv_w, conv_b, ssm_a_re, ssm_a_im, ssm_log_dt, ssm_b_re, ssm_b_im, ssm_c_re, ssm_c_im, ssm_d, ssm_w_glu, attn_sinks, w_br_conv, w_br_ssm, w_br_attn, w_out, ln2_g, ln2_b, ffn2_w_gate, ffn2_w_up, ffn2_w_down, ln3_g, ln3_b, loss_target, m_ffn1_w_gate, m_ffn1_w_up, m_ffn1_w_down, m_ln1_g, m_ln1_b, m_w_in, m_conv_w, m_conv_b, m_ssm_a_re, m_ssm_a_im, m_ssm_log_dt, m_ssm_b_re, m_ssm_b_im, m_ssm_c_re, m_ssm_c_im, m_ssm_d, m_ssm_w_glu, m_attn_sinks, m_w_br_conv, m_w_br_ssm, m_w_br_attn, m_w_out, m_ln2_g, m_ln2_b, m_ffn2_w_gate, m_ffn2_w_up, m_ffn2_w_down, m_ln3_g, m_ln3_b, v_ffn1_w_gate, v_ffn1_w_up, v_ffn1_w_down, v_ln1_g, v_ln1_b, v_w_in, v_conv_w, v_conv_b, v_ssm_a_re, v_ssm_a_im, v_ssm_log_dt, v_ssm_b_re, v_ssm_b_im, v_ssm_c_re, v_ssm_c_im, v_ssm_d, v_ssm_w_glu, v_attn_sinks, v_w_br_conv, v_w_br_ssm, v_w_br_attn, v_w_out, v_ln2_g, v_ln2_b, v_ffn2_w_gate, v_ffn2_w_up, v_ffn2_w_down, v_ln3_g, v_ln3_b):
    args = dict(locals())
    w = {k: args[k] for k in _WEIGHTS}
    L = ln1_g.shape[0]
    cx, cy, cc = _coords()
    chip = 2 * cx + cy

    shards = _weight_shards(w)
    full = {k: [lax.empty((N_CHIPS * s.shape[1], s.shape[2]), BF16) for _ in range(L)] for k, s in shards.items()}
    weave = _Weave(shards, full, L)
    weave.gather_first()
    p = {k: w[k] for k in ("ln1_g", "ln1_b", "ln2_g", "ln2_b", "ln3_g", "ln3_b", "conv_b", "ssm_a_re", "ssm_a_im",
                           "ssm_log_dt", "ssm_b_re", "ssm_b_im", "ssm_c_re", "ssm_c_im", "ssm_d", "attn_sinks")}
    p["conv_w"] = _gather_conv_w(conv_w, chip)

    loss, grad_x, _, small = _local_step(x[0], loss_target[0], full, p, L, weave)
    weave.flush()
    loss = lax.psum(loss[0, 0], ("x", "y", "c"))

    sizes = [math.prod(small[k].shape) for k in _SMALL_ORDER]
    pad = (-sum(sizes)) % 1024
    flat = jnp.concatenate([small[k].reshape(-1) for k in _SMALL_ORDER] + [jnp.zeros((pad,), F32)])
    flat = _all_reduce_small(flat.reshape(-1, 128), "small_grads").reshape(-1)
    sm, off = {}, 0
    for k, n in zip(_SMALL_ORDER, sizes):
        sm[k] = flat[off:off + n].reshape(small[k].shape)
        off += n
    red = {k: t.reshape(L, 2 * t.shape[2], t.shape[3]) for k, t in weave.final.items()}
    tr = lambda a: jnp.swapaxes(a, 1, 2)
    grads = dict(sm)
    grads.update(
        ffn1_w_gate=tr(red["wg1"]), ffn1_w_up=tr(red["wu1"]), ffn1_w_down=red["wd1"], w_in=tr(red["win"]),
        w_br_conv=tr(red["wbr"][:, :, _BR[0][0]:_BR[0][1]]), w_br_ssm=tr(red["wbr"][:, :, _BR[1][0]:_BR[1][1]]),
        w_br_attn=tr(red["wbr"][:, :, _BR[2][0]:_BR[2][1]]), w_out=red["wout"], ffn2_w_gate=tr(red["wg2"]),
        ffn2_w_up=tr(red["wu2"]), ffn2_w_down=red["wd2"], ssm_w_glu=red["wglu"],
        conv_w=lax.dynamic_slice_in_dim(sm["conv_w"], chip * 64, 64, axis=2))

    outs = [[], [], [], []]
    for k in _WEIGHTS:
        d, nm, nv = _adamw(w[k], grads[k], args["m_" + k], args["v_" + k], "adamw",
                           blocks_3d=k in ("ffn1_w_down", "ffn2_w_down", "w_out"))
        for lst, val in zip(outs, (grads[k], d, nm, nv)):
            lst.append(val)
    return (loss, grad_x[None], *outs[0], *outs[1], *outs[2], *outs[3])
```
